```python
import jax, jax.numpy as jnp
from jax import lax
import numpy as np

D_MODEL = 1024
BATCH = 8
SEQ = 8192
DEPTH = 4

EPS = 1e-6
D_LRU = 384
LRU_HEADS = 6
LRU_HEAD_DIM = D_LRU // LRU_HEADS
CONV_WIDTH = 4
LRU_C = 8.0
MLA_HEADS = 6
QK_NOPE_DIM = 64
QK_ROPE_DIM = 32
V_HEAD_DIM = 64
D_MLA = MLA_HEADS * V_HEAD_DIM
Q_LORA_RANK = 384
KV_LORA_RANK = 256
ROPE_BASE = 10000.0
Q_BLOCK = 128
POOL_WINDOWS = (2, 4, 8, 16)
POOL_GROUP_DIM = 64
D_POOL = POOL_GROUP_DIM * len(POOL_WINDOWS)
D_MIX = D_LRU + D_MLA + D_POOL
IN_SIZES = (D_LRU, D_LRU, Q_LORA_RANK, KV_LORA_RANK, QK_ROPE_DIM, D_MLA, D_POOL, D_POOL)
D_IN = sum(IN_SIZES)

kernel_name = "hybrid_lru_mla_pool_trunk"


def rms_norm(x, g):
    xf = x.astype(jnp.float32)
    y = xf * lax.rsqrt(jnp.mean(xf * xf, axis=-1, keepdims=True) + EPS)
    return (y * g.astype(jnp.float32)).astype(x.dtype)


def rope_tables(seq_len):
    pos = jnp.arange(seq_len, dtype=jnp.float32)
    inv_freq = ROPE_BASE ** (-jnp.arange(0, QK_ROPE_DIM, 2, dtype=jnp.float32) / QK_ROPE_DIM)
    ang = pos[:, None] * inv_freq[None, :]
    return jnp.cos(ang), jnp.sin(ang)


def apply_rope(x, cos, sin):
    shape = (1, x.shape[1]) + (1,) * (x.ndim - 3) + (QK_ROPE_DIM // 2,)
    c = cos.reshape(shape)
    s = sin.reshape(shape)
    xf = x.astype(jnp.float32)
    x1, x2 = jnp.split(xf, 2, axis=-1)
    return jnp.concatenate([x1 * c - x2 * s, x1 * s + x2 * c], axis=-1).astype(x.dtype)


def causal_depthwise_conv(x, w, b):
    S = x.shape[1]
    xp = jnp.pad(x, ((0, 0), (CONV_WIDTH - 1, 0), (0, 0)))
    y = b
    for k in range(CONV_WIDTH):
        y = y + xp[:, k:k + S, :] * w[k]
    return y


def rg_lru(x, w_r, b_r, w_i, b_i, lam):
    B, S, _ = x.shape
    xf = x.astype(jnp.float32)
    xh = xf.reshape(B, S, LRU_HEADS, LRU_HEAD_DIM)
    r = jax.nn.sigmoid(jnp.einsum('bshi,hij->bshj', xh, w_r.astype(jnp.float32)).reshape(B, S, D_LRU) + b_r.astype(jnp.float32))
    i = jax.nn.sigmoid(jnp.einsum('bshi,hij->bshj', xh, w_i.astype(jnp.float32)).reshape(B, S, D_LRU) + b_i.astype(jnp.float32))
    log_a = -LRU_C * r * jax.nn.softplus(-lam.astype(jnp.float32))
    a = jnp.exp(log_a)
    u = jnp.sqrt(-jnp.expm1(2.0 * log_a)) * (i * xf)

    def combine(left, right):
        a_l, h_l = left
        a_r, h_r = right
        return a_l * a_r, a_r * h_l + h_r

    _, h = lax.associative_scan(combine, (a, u), axis=1)
    return h.astype(x.dtype)


def mla(c_q, c_kv, k_r, q_norm_g, w_uq, kv_norm_g, w_ukv, cos, sin):
    B, S, _ = c_q.shape
    q = (rms_norm(c_q, q_norm_g) @ w_uq).reshape(B, S, MLA_HEADS, QK_NOPE_DIM + QK_ROPE_DIM)
    q_nope = q[..., :QK_NOPE_DIM]
    q_rope = apply_rope(q[..., QK_NOPE_DIM:], cos, sin)
    kv = (rms_norm(c_kv, kv_norm_g) @ w_ukv).reshape(B, S, MLA_HEADS, QK_NOPE_DIM + V_HEAD_DIM)
    k_nope = kv[..., :QK_NOPE_DIM]
    v = kv[..., QK_NOPE_DIM:]
    k_rope = apply_rope(k_r, cos, sin)
    scale = (QK_NOPE_DIM + QK_ROPE_DIM) ** -0.5
    outs = []
    for blk in range(S // Q_BLOCK):
        q0 = blk * Q_BLOCK
        q1 = q0 + Q_BLOCK
        s = (jnp.einsum('bqhd,bkhd->bhqk', q_nope[:, q0:q1], k_nope[:, :q1])
             + jnp.einsum('bqhr,bkr->bhqk', q_rope[:, q0:q1], k_rope[:, :q1]))
        s = s.astype(jnp.float32) * scale
        mask = jnp.arange(q1)[None, :] <= (q0 + jnp.arange(Q_BLOCK))[:, None]
        s = jnp.where(mask, s, -1e30)
        p = jax.nn.softmax(s, axis=-1).astype(v.dtype)
        outs.append(jnp.einsum('bhqk,bkhd->bqhd', p, v[:, :q1]))
    return jnp.concatenate(outs, axis=1).reshape(B, S, D_MLA)


def multi_scale_pool(x, w_pool, pool_scale):
    B, S, _ = x.shape
    xf = x.astype(jnp.float32)
    n_seen = jnp.arange(1, S + 1, dtype=jnp.float32)
    outs = []
    for g, w in enumerate(POOL_WINDOWS):
        xg = xf[..., g * POOL_GROUP_DIM:(g + 1) * POOL_GROUP_DIM]
        cs = jnp.cumsum(xg, axis=1)
        lower = jnp.pad(cs, ((0, 0), (w, 0), (0, 0)))[:, :S]
        cnt = jnp.minimum(n_seen, float(w))[None, :, None]
        outs.append((cs - lower) / cnt - xg)
    pooled = jnp.stack(outs, axis=2)
    y = jnp.einsum('bsgi,gij->bsgj', pooled, w_pool.astype(jnp.float32)).reshape(B, S, D_POOL)
    return (y * pool_scale.astype(jnp.float32)).astype(x.dtype)


def _fwd_setup_inputs(seed: int = 0) -> dict:
    key = jax.random.key(seed)
    ks = jax.random.split(key, 20)

    def nrm(k, shape, scale):
        return scale * jax.random.normal(k, shape, jnp.float32)

    u = jax.random.uniform(ks[9], (DEPTH, D_LRU), jnp.float32, minval=0.9, maxval=0.999)
    s = u ** (1.0 / LRU_C)
    lru_lambda = jnp.log(s) - jnp.log1p(-s)
    return {
        "x": nrm(ks[0], (BATCH, SEQ, D_MODEL), 1.0),
        "norm_g": 1.0 + nrm(ks[1], (DEPTH, D_MODEL), 0.02),
        "w_in": nrm(ks[2], (DEPTH, D_MODEL, D_IN), D_MODEL ** -0.5),
        "conv_w": nrm(ks[3], (DEPTH, CONV_WIDTH, D_LRU), CONV_WIDTH ** -0.5),
        "conv_b": nrm(ks[4], (DEPTH, D_LRU), 0.01),
        "w_rg": nrm(ks[5], (DEPTH, LRU_HEADS, LRU_HEAD_DIM, LRU_HEAD_DIM), LRU_HEAD_DIM ** -0.5),
        "b_rg": nrm(ks[6], (DEPTH, D_LRU), 0.01),
        "w_ig": nrm(ks[7], (DEPTH, LRU_HEADS, LRU_HEAD_DIM, LRU_HEAD_DIM), LRU_HEAD_DIM ** -0.5),
        "b_ig": nrm(ks[8], (DEPTH, D_LRU), 0.01),
        "lru_lambda": lru_lambda,
        "q_norm_g": 1.0 + nrm(ks[10], (DEPTH, Q_LORA_RANK), 0.02),
        "w_uq": nrm(ks[11], (DEPTH, Q_LORA_RANK, MLA_HEADS * (QK_NOPE_DIM + QK_ROPE_DIM)), Q_LORA_RANK ** -0.5),
        "kv_norm_g": 1.0 + nrm(ks[12], (DEPTH, KV_LORA_RANK), 0.02),
        "w_ukv": nrm(ks[13], (DEPTH, KV_LORA_RANK, MLA_HEADS * (QK_NOPE_DIM + V_HEAD_DIM)), KV_LORA_RANK ** -0.5),
        "w_pool": nrm(ks[14], (DEPTH, len(POOL_WINDOWS), POOL_GROUP_DIM, POOL_GROUP_DIM), POOL_GROUP_DIM ** -0.5),
        "pool_scale": 1.0 + nrm(ks[15], (DEPTH, D_POOL), 0.1),
        "w_out": nrm(ks[16], (DEPTH, D_MIX, D_MODEL), D_MIX ** -0.5),
        "final_norm_g": 1.0 + nrm(ks[17], (D_MODEL,), 0.02),
    }


def _fwd_reference(x, norm_g, w_in, conv_w, conv_b, w_rg, b_rg, w_ig, b_ig, lru_lambda,
              q_norm_g, w_uq, kv_norm_g, w_ukv, w_pool, pool_scale, w_out, final_norm_g):
    S = x.shape[1]
    cos, sin = rope_tables(S)
    offsets = np.cumsum(IN_SIZES)[:-1].tolist()
    for l in range(DEPTH):
        h = rms_norm(x, norm_g[l])
        z = h @ w_in[l]
        za, ga, cq, ckv, kr, gb, zc, gc = jnp.split(z, offsets, axis=-1)
        xa = causal_depthwise_conv(za, conv_w[l], conv_b[l])
        ya = rg_lru(xa, w_rg[l], b_rg[l], w_ig[l], b_ig[l], lru_lambda[l]) * jax.nn.silu(ga)
        yb = mla(cq, ckv, kr, q_norm_g[l], w_uq[l], kv_norm_g[l], w_ukv[l], cos, sin) * jax.nn.silu(gb)
        yc = multi_scale_pool(zc, w_pool[l], pool_scale[l]) * jax.nn.silu(gc)
        y = jnp.concatenate([ya, yb, yc], axis=-1)
        x = x + y @ w_out[l]
    return rms_norm(x, final_norm_g)


import jax as _jax
import jax.numpy as _jnp

TWIN_FORMAT = 'train_step'
FWD_PARAMS = ['x', 'norm_g', 'w_in', 'conv_w', 'conv_b', 'w_rg', 'b_rg', 'w_ig', 'b_ig', 'lru_lambda', 'q_norm_g', 'w_uq', 'kv_norm_g', 'w_ukv', 'w_pool', 'pool_scale', 'w_out', 'final_norm_g']
TWIN_WEIGHTS = ['norm_g', 'w_in', 'conv_w', 'conv_b', 'w_rg', 'b_rg', 'w_ig', 'b_ig', 'lru_lambda', 'q_norm_g', 'w_uq', 'kv_norm_g', 'w_ukv', 'w_pool', 'pool_scale', 'w_out', 'final_norm_g']
TWIN_DIFF_INPUT = 'x'
TWIN_INPUTS = ['x', 'norm_g', 'w_in', 'conv_w', 'conv_b', 'w_rg', 'b_rg', 'w_ig', 'b_ig', 'lru_lambda', 'q_norm_g', 'w_uq', 'kv_norm_g', 'w_ukv', 'w_pool', 'pool_scale', 'w_out', 'final_norm_g', 'loss_target', 'm_norm_g', 'm_w_in', 'm_conv_w', 'm_conv_b', 'm_w_rg', 'm_b_rg', 'm_w_ig', 'm_b_ig', 'm_lru_lambda', 'm_q_norm_g', 'm_w_uq', 'm_kv_norm_g', 'm_w_ukv', 'm_w_pool', 'm_pool_scale', 'm_w_out', 'm_final_norm_g', 'v_norm_g', 'v_w_in', 'v_conv_w', 'v_conv_b', 'v_w_rg', 'v_b_rg', 'v_w_ig', 'v_b_ig', 'v_lru_lambda', 'v_q_norm_g', 'v_w_uq', 'v_kv_norm_g', 'v_w_ukv', 'v_w_pool', 'v_pool_scale', 'v_w_out', 'v_final_norm_g']
TWIN_OUTPUTS = ['loss', 'grad_x', 'grad_norm_g', 'grad_w_in', 'grad_conv_w', 'grad_conv_b', 'grad_w_rg', 'grad_b_rg', 'grad_w_ig', 'grad_b_ig', 'grad_lru_lambda', 'grad_q_norm_g', 'grad_w_uq', 'grad_kv_norm_g', 'grad_w_ukv', 'grad_w_pool', 'grad_pool_scale', 'grad_w_out', 'grad_final_norm_g', 'delta_norm_g', 'delta_w_in', 'delta_conv_w', 'delta_conv_b', 'delta_w_rg', 'delta_b_rg', 'delta_w_ig', 'delta_b_ig', 'delta_lru_lambda', 'delta_q_norm_g', 'delta_w_uq', 'delta_kv_norm_g', 'delta_w_ukv', 'delta_w_pool', 'delta_pool_scale', 'delta_w_out', 'delta_final_norm_g', 'new_m_norm_g', 'new_m_w_in', 'new_m_conv_w', 'new_m_conv_b', 'new_m_w_rg', 'new_m_b_rg', 'new_m_w_ig', 'new_m_b_ig', 'new_m_lru_lambda', 'new_m_q_norm_g', 'new_m_w_uq', 'new_m_kv_norm_g', 'new_m_w_ukv', 'new_m_w_pool', 'new_m_pool_scale', 'new_m_w_out', 'new_m_final_norm_g', 'new_v_norm_g', 'new_v_w_in', 'new_v_conv_w', 'new_v_conv_b', 'new_v_w_rg', 'new_v_b_rg', 'new_v_w_ig', 'new_v_b_ig', 'new_v_lru_lambda', 'new_v_q_norm_g', 'new_v_w_uq', 'new_v_kv_norm_g', 'new_v_w_ukv', 'new_v_w_pool', 'new_v_pool_scale', 'new_v_w_out', 'new_v_final_norm_g']
TWIN_LEAF_KINDS = {'loss': 'loss', 'grad_x': 'grad_x', 'grad_norm_g': 'grad_w', 'grad_w_in': 'grad_w', 'grad_conv_w': 'grad_w', 'grad_conv_b': 'grad_w', 'grad_w_rg': 'grad_w', 'grad_b_rg': 'grad_w', 'grad_w_ig': 'grad_w', 'grad_b_ig': 'grad_w', 'grad_lru_lambda': 'grad_w', 'grad_q_norm_g': 'grad_w', 'grad_w_uq': 'grad_w', 'grad_kv_norm_g': 'grad_w', 'grad_w_ukv': 'grad_w', 'grad_w_pool': 'grad_w', 'grad_pool_scale': 'grad_w', 'grad_w_out': 'grad_w', 'grad_final_norm_g': 'grad_w', 'delta_norm_g': 'delta_w', 'delta_w_in': 'delta_w', 'delta_conv_w': 'delta_w', 'delta_conv_b': 'delta_w', 'delta_w_rg': 'delta_w', 'delta_b_rg': 'delta_w', 'delta_w_ig': 'delta_w', 'delta_b_ig': 'delta_w', 'delta_lru_lambda': 'delta_w', 'delta_q_norm_g': 'delta_w', 'delta_w_uq': 'delta_w', 'delta_kv_norm_g': 'delta_w', 'delta_w_ukv': 'delta_w', 'delta_w_pool': 'delta_w', 'delta_pool_scale': 'delta_w', 'delta_w_out': 'delta_w', 'delta_final_norm_g': 'delta_w', 'new_m_norm_g': 'new_m', 'new_m_w_in': 'new_m', 'new_m_conv_w': 'new_m', 'new_m_conv_b': 'new_m', 'new_m_w_rg': 'new_m', 'new_m_b_rg': 'new_m', 'new_m_w_ig': 'new_m', 'new_m_b_ig': 'new_m', 'new_m_lru_lambda': 'new_m', 'new_m_q_norm_g': 'new_m', 'new_m_w_uq': 'new_m', 'new_m_kv_norm_g': 'new_m', 'new_m_w_ukv': 'new_m', 'new_m_w_pool': 'new_m', 'new_m_pool_scale': 'new_m', 'new_m_w_out': 'new_m', 'new_m_final_norm_g': 'new_m', 'new_v_norm_g': 'new_v', 'new_v_w_in': 'new_v', 'new_v_conv_w': 'new_v', 'new_v_conv_b': 'new_v', 'new_v_w_rg': 'new_v', 'new_v_b_rg': 'new_v', 'new_v_w_ig': 'new_v', 'new_v_b_ig': 'new_v', 'new_v_lru_lambda': 'new_v', 'new_v_q_norm_g': 'new_v', 'new_v_w_uq': 'new_v', 'new_v_kv_norm_g': 'new_v', 'new_v_w_ukv': 'new_v', 'new_v_w_pool': 'new_v', 'new_v_pool_scale': 'new_v', 'new_v_w_out': 'new_v', 'new_v_final_norm_g': 'new_v'}


def _forward(args):
    return _fwd_reference(*[args[k] for k in FWD_PARAMS])


def _output_shape():
    def fwd():
        inp = _fwd_setup_inputs(0)
        return _fwd_reference(*[inp[k] for k in FWD_PARAMS])
    out = _jax.eval_shape(fwd)
    return out.shape, out.dtype

N_MICROBATCH = 1
ADAM_LR = 0.001
ADAM_B1 = 0.9
ADAM_B2 = 0.999
ADAM_EPS = 1e-08
ADAM_WD = 0.01
ADAM_STEP = 10
PER_EXAMPLE_BATCH_AXIS = {'x': 0, 'loss_target': 0}
SHARED_INPUTS = []
_WEIGHT_DTYPES = {'norm_g': _jnp.float32, 'w_in': _jnp.float32, 'conv_w': _jnp.float32, 'conv_b': _jnp.float32, 'w_rg': _jnp.float32, 'b_rg': _jnp.float32, 'w_ig': _jnp.float32, 'b_ig': _jnp.float32, 'lru_lambda': _jnp.float32, 'q_norm_g': _jnp.float32, 'w_uq': _jnp.float32, 'kv_norm_g': _jnp.float32, 'w_ukv': _jnp.float32, 'w_pool': _jnp.float32, 'pool_scale': _jnp.float32, 'w_out': _jnp.float32, 'final_norm_g': _jnp.float32}
MOMENT_SCALE = {'norm_g': 1.337127e-01, 'w_in': 8.656776e-02, 'conv_w': 1.503883e-01, 'conv_b': 1.428354e+00, 'w_rg': 4.774524e-02, 'b_rg': 3.598667e-02, 'w_ig': 8.577066e-02, 'b_ig': 5.063352e-02, 'lru_lambda': 6.381676e-02, 'q_norm_g': 2.723003e-02, 'w_uq': 2.163280e-02, 'kv_norm_g': 4.917241e-02, 'w_ukv': 2.891884e-02, 'w_pool': 1.241260e-01, 'pool_scale': 1.226685e-01, 'w_out': 1.010710e-01, 'final_norm_g': 6.402731e+01}


def _to_microbatches(a, axis):
    t = _jnp.moveaxis(a, axis, 0)
    t = t.reshape((N_MICROBATCH, t.shape[0] // N_MICROBATCH) + t.shape[1:])
    return _jnp.moveaxis(t, 1, axis + 1)


def setup_inputs(seed: int = 0) -> dict:
    inp = _fwd_setup_inputs(seed)
    key = _jax.random.fold_in(_jax.random.key(seed), 7919)
    shape, _ = _output_shape()
    out = dict(inp)
    out["loss_target"] = _jax.random.normal(_jax.random.fold_in(key, 0), shape, _jnp.float32)
    for i, name in enumerate(TWIN_WEIGHTS):
        w = inp[name].astype(_jnp.float32)
        if MOMENT_SCALE is None:
            s = _jnp.sqrt(_jnp.mean(_jnp.square(w)) + 1e-30)
        else:
            s = MOMENT_SCALE[name]
        km, kv = _jax.random.split(_jax.random.fold_in(key, i + 1))
        out[name] = w
        out["m_" + name] = s * _jax.random.normal(km, w.shape, _jnp.float32)
        out["v_" + name] = (s * s) * _jax.random.uniform(kv, w.shape, _jnp.float32, 0.5, 1.5)
    if N_MICROBATCH > 1:
        for name, axis in PER_EXAMPLE_BATCH_AXIS.items():
            out[name] = _to_microbatches(out[name], axis)
    return {'x': out['x'], 'norm_g': out['norm_g'], 'w_in': out['w_in'], 'conv_w': out['conv_w'], 'conv_b': out['conv_b'], 'w_rg': out['w_rg'], 'b_rg': out['b_rg'], 'w_ig': out['w_ig'], 'b_ig': out['b_ig'], 'lru_lambda': out['lru_lambda'], 'q_norm_g': out['q_norm_g'], 'w_uq': out['w_uq'], 'kv_norm_g': out['kv_norm_g'], 'w_ukv': out['w_ukv'], 'w_pool': out['w_pool'], 'pool_scale': out['pool_scale'], 'w_out': out['w_out'], 'final_norm_g': out['final_norm_g'], 'loss_target': out['loss_target'], 'm_norm_g': out['m_norm_g'], 'm_w_in': out['m_w_in'], 'm_conv_w': out['m_conv_w'], 'm_conv_b': out['m_conv_b'], 'm_w_rg': out['m_w_rg'], 'm_b_rg': out['m_b_rg'], 'm_w_ig': out['m_w_ig'], 'm_b_ig': out['m_b_ig'], 'm_lru_lambda': out['m_lru_lambda'], 'm_q_norm_g': out['m_q_norm_g'], 'm_w_uq': out['m_w_uq'], 'm_kv_norm_g': out['m_kv_norm_g'], 'm_w_ukv': out['m_w_ukv'], 'm_w_pool': out['m_w_pool'], 'm_pool_scale': out['m_pool_scale'], 'm_w_out': out['m_w_out'], 'm_final_norm_g': out['m_final_norm_g'], 'v_norm_g': out['v_norm_g'], 'v_w_in': out['v_w_in'], 'v_conv_w': out['v_conv_w'], 'v_conv_b': out['v_conv_b'], 'v_w_rg': out['v_w_rg'], 'v_b_rg': out['v_b_rg'], 'v_w_ig': out['v_w_ig'], 'v_b_ig': out['v_b_ig'], 'v_lru_lambda': out['v_lru_lambda'], 'v_q_norm_g': out['v_q_norm_g'], 'v_w_uq': out['v_w_uq'], 'v_kv_norm_g': out['v_kv_norm_g'], 'v_w_ukv': out['v_w_ukv'], 'v_w_pool': out['v_w_pool'], 'v_pool_scale': out['v_pool_scale'], 'v_w_out': out['v_w_out'], 'v_final_norm_g': out['v_final_norm_g']}


def _loss(weights, diff, rest, loss_target):
    with _jax.named_scope("forward"):
        args = {**rest, TWIN_DIFF_INPUT: diff, **{k: w.astype(_WEIGHT_DTYPES[k]) for k, w in weights.items()}}
        y = _forward(args)
    with _jax.named_scope("loss_head"):
        err = _jnp.square(y.astype(_jnp.float32) - loss_target)
        return 0.5 * _jnp.sum(_jnp.mean(err, axis=-1)) if err.ndim else 0.5 * err


def _adamw(w, g, m, v):
    m = ADAM_B1 * m + (1.0 - ADAM_B1) * g
    v = ADAM_B2 * v + (1.0 - ADAM_B2) * _jnp.square(g)
    m_hat = m / (1.0 - ADAM_B1 ** ADAM_STEP)
    v_hat = v / (1.0 - ADAM_B2 ** ADAM_STEP)
    delta = -ADAM_LR * (m_hat / (_jnp.sqrt(v_hat) + ADAM_EPS) + ADAM_WD * w)
    return delta, m, v


def reference(x, norm_g, w_in, conv_w, conv_b, w_rg, b_rg, w_ig, b_ig, lru_lambda, q_norm_g, w_uq, kv_norm_g, w_ukv, w_pool, pool_scale, w_out, final_norm_g, loss_target, m_norm_g, m_w_in, m_conv_w, m_conv_b, m_w_rg, m_b_rg, m_w_ig, m_b_ig, m_lru_lambda, m_q_norm_g, m_w_uq, m_kv_norm_g, m_w_ukv, m_w_pool, m_pool_scale, m_w_out, m_final_norm_g, v_norm_g, v_w_in, v_conv_w, v_conv_b, v_w_rg, v_b_rg, v_w_ig, v_b_ig, v_lru_lambda, v_q_norm_g, v_w_uq, v_kv_norm_g, v_w_ukv, v_w_pool, v_pool_scale, v_w_out, v_final_norm_g):
    given = dict(x=x, norm_g=norm_g, w_in=w_in, conv_w=conv_w, conv_b=conv_b, w_rg=w_rg, b_rg=b_rg, w_ig=w_ig, b_ig=b_ig, lru_lambda=lru_lambda, q_norm_g=q_norm_g, w_uq=w_uq, kv_norm_g=kv_norm_g, w_ukv=w_ukv, w_pool=w_pool, pool_scale=pool_scale, w_out=w_out, final_norm_g=final_norm_g, loss_target=loss_target, m_norm_g=m_norm_g, m_w_in=m_w_in, m_conv_w=m_conv_w, m_conv_b=m_conv_b, m_w_rg=m_w_rg, m_b_rg=m_b_rg, m_w_ig=m_w_ig, m_b_ig=m_b_ig, m_lru_lambda=m_lru_lambda, m_q_norm_g=m_q_norm_g, m_w_uq=m_w_uq, m_kv_norm_g=m_kv_norm_g, m_w_ukv=m_w_ukv, m_w_pool=m_w_pool, m_pool_scale=m_pool_scale, m_w_out=m_w_out, m_final_norm_g=m_final_norm_g, v_norm_g=v_norm_g, v_w_in=v_w_in, v_conv_w=v_conv_w, v_conv_b=v_conv_b, v_w_rg=v_w_rg, v_b_rg=v_b_rg, v_w_ig=v_w_ig, v_b_ig=v_b_ig, v_lru_lambda=v_lru_lambda, v_q_norm_g=v_q_norm_g, v_w_uq=v_w_uq, v_kv_norm_g=v_kv_norm_g, v_w_ukv=v_w_ukv, v_w_pool=v_w_pool, v_pool_scale=v_pool_scale, v_w_out=v_w_out, v_final_norm_g=v_final_norm_g)
    weights = {n: given[n] for n in TWIN_WEIGHTS}
    shared = {n: given[n] for n in SHARED_INPUTS}
    per_example = {n: given[n] for n in ['x']}
    grad_fn = _jax.value_and_grad(_loss, argnums=(0, 1))

    def one_microbatch(ex, loss_target):
        ex = dict(ex)
        diff = ex.pop(TWIN_DIFF_INPUT)
        return grad_fn(weights, diff, {**shared, **ex}, loss_target)

    if N_MICROBATCH == 1:
        loss, (grad_w, grad_x) = one_microbatch(per_example, given["loss_target"])
    else:
        def body(carry, xs):
            loss_sum, grad_sum = carry
            l_k, (gw_k, gx_k) = one_microbatch(xs[0], xs[1])
            with _jax.named_scope("update"):
                return (loss_sum + l_k, _jax.tree.map(_jnp.add, grad_sum, gw_k)), gx_k

        init = (_jnp.zeros((), _jnp.float32), _jax.tree.map(_jnp.zeros_like, weights))
        (loss, grad_w), grad_x = _jax.lax.scan(body, init, (per_example, given["loss_target"]))
    with _jax.named_scope("update"):
        delta_w, new_m, new_v = {}, {}, {}
        for n in TWIN_WEIGHTS:
            delta_w[n], new_m[n], new_v[n] = _adamw(weights[n], grad_w[n], given["m_" + n], given["v_" + n])
    return (loss, grad_x, *[grad_w[n] for n in TWIN_WEIGHTS], *[delta_w[n] for n in TWIN_WEIGHTS],
            *[new_m[n] for n in TWIN_WEIGHTS], *[new_v[n] for n in TWIN_WEIGHTS])
```

```python
import functools
import math

import jax
import jax.numpy as jnp
from jax import lax
from jax.experimental import pallas as pl
from jax.experimental.pallas import tpu as pltpu

F32 = jnp.float32
MXU_DTYPE = jnp.bfloat16

D_MODEL = 1024
DEPTH = 4
EPS = 1e-6
D_LRU = 384
LRU_HEADS = 6
CONV_WIDTH = 4
LRU_C = 8.0
MLA_HEADS = 6
QK_NOPE = 64
QK_ROPE = 32
V_DIM = 64
D_MLA = MLA_HEADS * V_DIM
Q_RANK = 384
KV_RANK = 256
ROPE_BASE = 10000.0
POOL_WINDOWS = (2, 4, 8, 16)
D_POOL = 256
D_MIX = D_LRU + D_MLA + D_POOL
D_IN = 2336
ATT_SCALE = (QK_NOPE + QK_ROPE) ** -0.5

ADAM_LR = 0.001
ADAM_B1 = 0.9
ADAM_B2 = 0.999
ADAM_EPS = 1e-08
ADAM_WD = 0.01
ADAM_STEP = 10

LANES = 128
SUBLANES = 8
V7X_VMEM_BYTES = 64 << 20
N_DEV = 8

D_INP = 2432
KR_LANE0 = 64
HEAD_BLK = 128
N_PAIR = MLA_HEADS // 2

WEIGHT_NAMES = ['norm_g', 'w_in', 'conv_w', 'conv_b', 'w_rg', 'b_rg', 'w_ig', 'b_ig', 'lru_lambda', 'q_norm_g',
                'w_uq', 'kv_norm_g', 'w_ukv', 'w_pool', 'pool_scale', 'w_out', 'final_norm_g']
SHARDED = ['w_in', 'conv_w', 'w_uq', 'w_ukv', 'w_out']
REPLICATED = [n for n in WEIGHT_NAMES if n not in SHARDED]


def _cp(sem, vmem_mb=None):
    return pltpu.CompilerParams(dimension_semantics=sem,
                                vmem_limit_bytes=None if vmem_mb is None else vmem_mb << 20)


def _dot(a, b):
    return jnp.dot(a.astype(MXU_DTYPE), b.astype(MXU_DTYPE), preferred_element_type=F32)


def _dot_nt(a, b):
    return lax.dot_general(a.astype(MXU_DTYPE), b.astype(MXU_DTYPE), (((1,), (1,)), ((), ())),
                           preferred_element_type=F32)


def _dot_tn(a, b):
    return lax.dot_general(a.astype(MXU_DTYPE), b.astype(MXU_DTYPE), (((0,), (0,)), ((), ())),
                           preferred_element_type=F32)


def _sig(x):
    return 1.0 / (1.0 + jnp.exp(-x))


def _down(x, k):
    return pltpu.roll(x, k, 0)


def _up(x, k):
    return pltpu.roll(x, x.shape[0] - k, 0)


def _rows(shape):
    return lax.broadcasted_iota(jnp.int32, shape, 0)


def _lanes(shape):
    return lax.broadcasted_iota(jnp.int32, shape, 1)


def _tile(s):
    return min(512, s)


def _inproj_fwd(x, g, w):
    s = x.shape[0]
    t = _tile(s)

    def body(x_ref, g_ref, w_ref, h_ref, z_ref):
        xv = x_ref[...]
        rs = lax.rsqrt(jnp.mean(xv * xv, axis=-1, keepdims=True) + EPS)
        h = (xv * rs * g_ref[...]).astype(MXU_DTYPE)
        h_ref[...] = h
        z_ref[...] = jnp.dot(h, w_ref[...], preferred_element_type=F32)

    return pl.pallas_call(
        body, name="inproj_fwd", grid=(s // t,),
        in_specs=[pl.BlockSpec((t, D_MODEL), lambda i: (i, 0)),
                  pl.BlockSpec((1, D_MODEL), lambda i: (0, 0)),
                  pl.BlockSpec((D_MODEL, D_INP), lambda i: (0, 0))],
        out_specs=[pl.BlockSpec((t, D_MODEL), lambda i: (i, 0)),
                   pl.BlockSpec((t, D_INP), lambda i: (i, 0))],
        out_shape=[jax.ShapeDtypeStruct((s, D_MODEL), MXU_DTYPE), jax.ShapeDtypeStruct((s, D_INP), F32)],
        compiler_params=_cp(("parallel",), 40),
    )(x, g, w)


def _lru_gates(za, halo, cw_ref, cb_ref, wr_ref, br_ref, wi_ref, bi_ref, lam_ref):
    t = za.shape[0]
    ext = jnp.concatenate([halo, za], axis=0)
    sh = [za] + [_down(ext, j)[SUBLANES:SUBLANES + t] for j in (1, 2, 3)]
    xa = cb_ref[...] + cw_ref[3:4, :] * sh[0] + cw_ref[2:3, :] * sh[1] + cw_ref[1:2, :] * sh[2] + cw_ref[0:1, :] * sh[3]
    r = _sig(_dot(xa, wr_ref[...]) + br_ref[...])
    ig = _sig(_dot(xa, wi_ref[...]) + bi_ref[...])
    lam = lam_ref[...]
    sp = jnp.maximum(-lam, 0.0) + jnp.log(1.0 + jnp.exp(-jnp.abs(lam)))
    la = (-LRU_C) * r * sp
    a = jnp.exp(la)
    y2 = 2.0 * la
    taylor = -(y2 * (1.0 + y2 * (0.5 + y2 * (1.0 / 6.0 + y2 * (1.0 / 24.0)))))
    m2 = jnp.where(y2 > -0.02, taylor, 1.0 - jnp.exp(y2))
    m = jnp.sqrt(m2)
    return xa, sh, r, ig, sp, a, m


def _lru_fwd(z, cw, cb, wr, br, wi, bi, lam):
    s = z.shape[0]
    t = _tile(s)
    c = D_LRU

    def body(za_ref, ga_ref, cw_ref, cb_ref, wr_ref, br_ref, wi_ref, bi_ref, lam_ref, ya_ref, hs_ref, zprev, hcar):
        i = pl.program_id(0)

        @pl.when(i == 0)
        def _():
            zprev[...] = jnp.zeros_like(zprev)
            hcar[...] = jnp.zeros_like(hcar)

        za = za_ref[...]
        xa, _, _, ig, _, a, m = _lru_gates(za, zprev[...], cw_ref, cb_ref, wr_ref, br_ref, wi_ref, bi_ref, lam_ref)
        u = m * (ig * xa)
        row = _rows((t, c))
        acc_a, acc_h = a, u
        k = 1
        while k < t:
            a_sh = jnp.where(row >= k, _down(acc_a, k), 1.0)
            h_sh = jnp.where(row >= k, _down(acc_h, k), 0.0)
            acc_h = acc_h + acc_a * h_sh
            acc_a = acc_a * a_sh
            k *= 2
        hs = acc_h + acc_a * hcar[...]
        hs_ref[...] = hs
        ga = ga_ref[...]
        ya_ref[...] = (hs * (ga * _sig(ga))).astype(ya_ref.dtype)
        hcar[...] = jnp.sum(jnp.where(row == t - 1, hs, 0.0), axis=0, keepdims=True)
        zprev[...] = za_ref[t - SUBLANES:t, :]

    full = lambda shp: pl.BlockSpec(shp, lambda i: (0, 0))
    return pl.pallas_call(
        body, name="lru_fwd", grid=(s // t,),
        in_specs=[pl.BlockSpec((t, c), lambda i: (i, 0)), pl.BlockSpec((t, c), lambda i: (i, 1)),
                  full((CONV_WIDTH, c)), full((1, c)), full((c, c)), full((1, c)), full((c, c)), full((1, c)),
                  full((1, c))],
        out_specs=[pl.BlockSpec((t, c), lambda i: (i, 0)), pl.BlockSpec((t, c), lambda i: (i, 0))],
        out_shape=[jax.ShapeDtypeStruct((s, c), MXU_DTYPE), jax.ShapeDtypeStruct((s, c), F32)],
        scratch_shapes=[pltpu.VMEM((SUBLANES, c), F32), pltpu.VMEM((1, c), F32)],
        compiler_params=_cp(("arbitrary",), 40),
    )(z, z, cw, cb, wr, br, wi, bi, lam)


def _lru_bwd(z, hs, dy, cw, cb, wr, br, wi, bi, lam):
    s = z.shape[0]
    t = _tile(s)
    nt = s // t
    c = D_LRU
    hb = t // SUBLANES

    def body(za_ref, zh_ref, ga_ref, hs_ref, hh_ref, dy_ref, cw_ref, cb_ref, wr_ref, br_ref, wi_ref, bi_ref, lam_ref,
             dza_ref, dga_ref, dwr_ref, dwi_ref, dcw_ref, dcb_ref, dbr_ref, dbi_ref, dlam_ref, lcar, dxn):
        i = pl.program_id(0)
        tt = nt - 1 - i

        @pl.when(i == 0)
        def _():
            lcar[...] = jnp.zeros_like(lcar)
            dxn[...] = jnp.zeros_like(dxn)
            for ref in (dwr_ref, dwi_ref, dcw_ref, dcb_ref, dbr_ref, dbi_ref, dlam_ref):
                ref[...] = jnp.zeros_like(ref)

        first = (tt > 0).astype(F32)
        za = za_ref[...]
        xa, sh, r, ig, sp, a, m = _lru_gates(za, zh_ref[...] * first, cw_ref, cb_ref, wr_ref, br_ref, wi_ref, bi_ref,
                                             lam_ref)
        hs_v = hs_ref[...]
        hprev = _down(jnp.concatenate([hh_ref[...] * first, hs_v], axis=0), 1)[SUBLANES:SUBLANES + t]
        ga = ga_ref[...]
        sg = _sig(ga)
        silu = ga * sg
        dya = dy_ref[...]
        dga_ref[...] = (dya * hs_v * (sg * (1.0 + ga * (1.0 - sg)))).astype(dga_ref.dtype)
        row = _rows((t, c))
        acc_h = dya * silu + jnp.where(row == t - 1, lcar[...], 0.0)
        acc_b = jnp.where(row < t - 1, _up(a, 1), 0.0)
        k = 1
        while k < t:
            keep = row < t - k
            b_sh = jnp.where(keep, _up(acc_b, k), 0.0)
            h_sh = jnp.where(keep, _up(acc_h, k), 0.0)
            acc_h = acc_h + acc_b * h_sh
            acc_b = acc_b * b_sh
            k *= 2
        lmb = acc_h
        lcar[...] = jnp.sum(jnp.where(row == 0, a * lmb, 0.0), axis=0, keepdims=True)
        da = lmb * hprev
        dxa = lmb * m * ig
        di = lmb * m * xa
        dm = lmb * ig * xa
        dla = da * a - dm * (a * a) / m
        dr = dla * ((-LRU_C) * sp)
        lam = lam_ref[...]
        dsp = jnp.sum(dla * ((-LRU_C) * r), axis=0, keepdims=True)
        dlam_ref[...] += dsp * (-1.0 / (1.0 + jnp.exp(lam)))
        dpr = dr * r * (1.0 - r)
        dpi = di * ig * (1.0 - ig)
        dbr_ref[...] += jnp.sum(dpr, axis=0, keepdims=True)
        dbi_ref[...] += jnp.sum(dpi, axis=0, keepdims=True)
        dwr_ref[...] += _dot_tn(xa, dpr)
        dwi_ref[...] += _dot_tn(xa, dpi)
        dxa = dxa + _dot_nt(dpr, wr_ref[...]) + _dot_nt(dpi, wi_ref[...])
        dcb_ref[...] += jnp.sum(dxa, axis=0, keepdims=True)
        for k in range(CONV_WIDTH):
            dcw_ref[k:k + 1, :] += jnp.sum(dxa * sh[CONV_WIDTH - 1 - k], axis=0, keepdims=True)
        ext = jnp.concatenate([dxa, dxn[...]], axis=0)
        dza = cw_ref[3:4, :] * dxa
        for j in (1, 2, 3):
            dza = dza + cw_ref[3 - j:4 - j, :] * _up(ext, j)[:t]
        dza_ref[...] = dza.astype(dza_ref.dtype)
        dxn[...] = dxa[:SUBLANES]

    full = lambda shp: pl.BlockSpec(shp, lambda i: (0, 0))
    rev = lambda i: nt - 1 - i
    halo = lambda i: (jnp.maximum((nt - 1 - i) * hb - 1, 0), 0)
    outs = pl.pallas_call(
        body, name="lru_bwd", grid=(nt,),
        in_specs=[pl.BlockSpec((t, c), lambda i: (rev(i), 0)), pl.BlockSpec((SUBLANES, c), halo),
                  pl.BlockSpec((t, c), lambda i: (rev(i), 1)),
                  pl.BlockSpec((t, c), lambda i: (rev(i), 0)), pl.BlockSpec((SUBLANES, c), halo),
                  pl.BlockSpec((t, c), lambda i: (rev(i), 0)),
                  full((CONV_WIDTH, c)), full((1, c)), full((c, c)), full((1, c)), full((c, c)), full((1, c)),
                  full((1, c))],
        out_specs=[pl.BlockSpec((t, c), lambda i: (rev(i), 0)), pl.BlockSpec((t, c), lambda i: (rev(i), 0)),
                   full((c, c)), full((c, c)), full((CONV_WIDTH, c)), full((1, c)), full((1, c)), full((1, c)),
                   full((1, c))],
        out_shape=[jax.ShapeDtypeStruct((s, c), MXU_DTYPE), jax.ShapeDtypeStruct((s, c), MXU_DTYPE),
                   jax.ShapeDtypeStruct((c, c), F32), jax.ShapeDtypeStruct((c, c), F32),
                   jax.ShapeDtypeStruct((CONV_WIDTH, c), F32)] + [jax.ShapeDtypeStruct((1, c), F32)] * 4,
        scratch_shapes=[pltpu.VMEM((1, c), F32), pltpu.VMEM((SUBLANES, c), F32)],
        compiler_params=_cp(("arbitrary",), 48),
    )(z, z, z, hs, hs, dy, cw, cb, wr, br, wi, bi, lam)
    return outs


POOL_HALO = 16


def _pool_select(lane, v2, v4, v8, v16):
    return jnp.where(lane < 64, v2, jnp.where(lane < 128, v4, jnp.where(lane < 192, v8, v16)))


def _pool_counts(t0, t, c):
    lane = _lanes((t, c))
    win = _pool_select(lane, 2.0, 4.0, 8.0, 16.0)
    seen = (t0 + _rows((t, c)) + 1).astype(F32)
    return lane, jnp.minimum(seen, win)


def _pooled(zc, halo, lane, cnt):
    t = zc.shape[0]
    ext = jnp.concatenate([halo, zc], axis=0)
    s2 = ext + _down(ext, 1)
    s4 = s2 + _down(s2, 2)
    s8 = s4 + _down(s4, 4)
    s16 = s8 + _down(s8, 8)
    cut = lambda v: v[POOL_HALO:POOL_HALO + t]
    return _pool_select(lane, cut(s2), cut(s4), cut(s8), cut(s16)) / cnt - zc


def _pool_fwd(z, wp, ps):
    s = z.shape[0]
    t = _tile(s)
    c = D_POOL

    def body(zc_ref, gc_ref, wp_ref, ps_ref, yc_ref, zprev):
        i = pl.program_id(0)

        @pl.when(i == 0)
        def _():
            zprev[...] = jnp.zeros_like(zprev)

        zc = zc_ref[...]
        lane, cnt = _pool_counts(i * t, t, c)
        pooled = _pooled(zc, zprev[...], lane, cnt)
        pc = _dot(pooled, wp_ref[...])
        gc = gc_ref[...]
        yc_ref[...] = (pc * ps_ref[...] * (gc * _sig(gc))).astype(yc_ref.dtype)
        zprev[...] = zc_ref[t - POOL_HALO:t, :]

    full = lambda shp: pl.BlockSpec(shp, lambda i: (0, 0))
    return pl.pallas_call(
        body, name="pool_fwd", grid=(s // t,),
        in_specs=[pl.BlockSpec((t, c), lambda i: (i, 7)), pl.BlockSpec((t, c), lambda i: (i, 8)),
                  full((c, c)), full((1, c))],
        out_specs=pl.BlockSpec((t, c), lambda i: (i, 0)),
        out_shape=jax.ShapeDtypeStruct((s, c), MXU_DTYPE),
        scratch_shapes=[pltpu.VMEM((POOL_HALO, c), F32)],
        compiler_params=_cp(("arbitrary",), 32),
    )(z, z, wp, ps)


def _pool_bwd(z, dy, wp, ps):
    s = z.shape[0]
    t = _tile(s)
    nt = s // t
    c = D_POOL
    hb = t // POOL_HALO

    def body(zc_ref, zh_ref, gc_ref, dy_ref, wp_ref, ps_ref, dzc_ref, dgc_ref, dwp_ref, dps_ref, ddn):
        i = pl.program_id(0)
        tt = nt - 1 - i

        @pl.when(i == 0)
        def _():
            ddn[...] = jnp.zeros_like(ddn)
            dwp_ref[...] = jnp.zeros_like(dwp_ref)
            dps_ref[...] = jnp.zeros_like(dps_ref)

        first = (tt > 0).astype(F32)
        zc = zc_ref[...]
        lane, cnt = _pool_counts(tt * t, t, c)
        pooled = _pooled(zc, zh_ref[...] * first, lane, cnt)
        pc = _dot(pooled, wp_ref[...])
        gc = gc_ref[...]
        sg = _sig(gc)
        silu = gc * sg
        dyc = dy_ref[...]
        ps_v = ps_ref[...]
        dgc_ref[...] = (dyc * pc * ps_v * (sg * (1.0 + gc * (1.0 - sg)))).astype(dgc_ref.dtype)
        dps_ref[...] += jnp.sum(dyc * pc * silu, axis=0, keepdims=True)
        dpc = dyc * ps_v * silu
        dwp_ref[...] += _dot_tn(pooled, dpc)
        dpooled = _dot_nt(dpc, wp_ref[...])
        dd = dpooled / cnt
        ext = jnp.concatenate([dd, ddn[...]], axis=0)
        f2 = ext + _up(ext, 1)
        f4 = f2 + _up(f2, 2)
        f8 = f4 + _up(f4, 4)
        f16 = f8 + _up(f8, 8)
        dzc = _pool_select(lane, f2[:t], f4[:t], f8[:t], f16[:t]) - dpooled
        dzc_ref[...] = dzc.astype(dzc_ref.dtype)
        ddn[...] = dd[:POOL_HALO]

    full = lambda shp: pl.BlockSpec(shp, lambda i: (0, 0))
    rev = lambda i: nt - 1 - i
    return pl.pallas_call(
        body, name="pool_bwd", grid=(nt,),
        in_specs=[pl.BlockSpec((t, c), lambda i: (rev(i), 7)),
                  pl.BlockSpec((POOL_HALO, c), lambda i: (jnp.maximum(rev(i) * hb - 1, 0), 7)),
                  pl.BlockSpec((t, c), lambda i: (rev(i), 8)),
                  pl.BlockSpec((t, c), lambda i: (rev(i), 3)),
                  full((c, c)), full((1, c))],
        out_specs=[pl.BlockSpec((t, c), lambda i: (rev(i), 0)), pl.BlockSpec((t, c), lambda i: (rev(i), 0)),
                   full((c, c)), full((1, c))],
        out_shape=[jax.ShapeDtypeStruct((s, c), MXU_DTYPE), jax.ShapeDtypeStruct((s, c), MXU_DTYPE),
                   jax.ShapeDtypeStruct((c, c), F32), jax.ShapeDtypeStruct((1, c), F32)],
        scratch_shapes=[pltpu.VMEM((POOL_HALO, c), F32)],
        compiler_params=_cp(("arbitrary",), 32),
    )(z, z, z, dy, wp, ps)


def _rope_tables(s):
    pos = jnp.arange(s, dtype=F32)
    inv_freq = ROPE_BASE ** (-jnp.arange(0, QK_ROPE, 2, dtype=F32) / QK_ROPE)
    ang = pos[:, None] * inv_freq[None, :]
    cos, sin = jnp.cos(ang), jnp.sin(ang)
    half = QK_ROPE // 2
    ones = jnp.ones((s, QK_NOPE), F32)
    z64 = jnp.zeros((s, QK_NOPE), F32)
    zh = jnp.zeros((s, half), F32)
    z32 = jnp.zeros((s, HEAD_BLK - QK_NOPE - QK_ROPE), F32)
    c_t = jnp.concatenate([ones, cos, cos, z32], axis=1)
    s1_t = jnp.concatenate([z64, -sin, zh, z32], axis=1)
    s2_t = jnp.concatenate([z64, zh, sin, z32], axis=1)
    return c_t, s1_t, s2_t


def _rope(x, c_t, s1_t, s2_t):
    return x * c_t + pltpu.roll(x, HEAD_BLK - 16, 1) * s1_t + pltpu.roll(x, 16, 1) * s2_t


def _unrope(d, c_t, s1_t, s2_t):
    return d * c_t + pltpu.roll(d * s1_t, 16, 1) + pltpu.roll(d * s2_t, HEAD_BLK - 16, 1)


def _mla_prep_fwd(z, gq, gkv, wuq, wukv, c_t, s1_t, s2_t):
    s = z.shape[0]
    t = _tile(s)
    hq = MLA_HEADS * HEAD_BLK

    def body(cq_ref, ckv_ref, kr_ref, gq_ref, gkv_ref, wuq_ref, wukv_ref, c_ref, s1_ref, s2_ref,
             q_ref, k_ref, v_ref, qn_ref, kvn_ref):
        ct, s1, s2 = c_ref[...], s1_ref[...], s2_ref[...]
        cq = cq_ref[...]
        qn = (cq * lax.rsqrt(jnp.mean(cq * cq, axis=-1, keepdims=True) + EPS) * gq_ref[...]).astype(MXU_DTYPE)
        qn_ref[...] = qn
        q = jnp.dot(qn, wuq_ref[...], preferred_element_type=F32)
        ckv = ckv_ref[...]
        kvn = (ckv * lax.rsqrt(jnp.mean(ckv * ckv, axis=-1, keepdims=True) + EPS) * gkv_ref[...]).astype(MXU_DTYPE)
        kvn_ref[...] = kvn
        kvp = jnp.dot(kvn, wukv_ref[...], preferred_element_type=F32)
        krr = _rope(kr_ref[...], ct, s1, s2)
        for h in range(MLA_HEADS):
            blk = slice(h * HEAD_BLK, (h + 1) * HEAD_BLK)
            q_ref[:, blk] = _rope(q[:, blk], ct, s1, s2).astype(q_ref.dtype)
            k_ref[:, blk] = (kvp[:, blk] + krr).astype(k_ref.dtype)
        v_ref[...] = kvp[:, hq:].astype(v_ref.dtype)

    full = lambda shp: pl.BlockSpec(shp, lambda i: (0, 0))
    tab = pl.BlockSpec((t, HEAD_BLK), lambda i: (i, 0))
    return pl.pallas_call(
        body, name="mla_prep_fwd", grid=(s // t,),
        in_specs=[pl.BlockSpec((t, Q_RANK), lambda i: (i, 2)), pl.BlockSpec((t, KV_RANK), lambda i: (i, 6)),
                  pl.BlockSpec((t, HEAD_BLK), lambda i: (i, 18)),
                  full((1, Q_RANK)), full((1, KV_RANK)), full((Q_RANK, hq)), full((KV_RANK, hq + D_MLA)),
                  tab, tab, tab],
        out_specs=[pl.BlockSpec((t, hq), lambda i: (i, 0)), pl.BlockSpec((t, hq), lambda i: (i, 0)),
                   pl.BlockSpec((t, D_MLA), lambda i: (i, 0)), pl.BlockSpec((t, Q_RANK), lambda i: (i, 0)),
                   pl.BlockSpec((t, KV_RANK), lambda i: (i, 0))],
        out_shape=[jax.ShapeDtypeStruct((s, hq), MXU_DTYPE), jax.ShapeDtypeStruct((s, hq), MXU_DTYPE),
                   jax.ShapeDtypeStruct((s, D_MLA), MXU_DTYPE), jax.ShapeDtypeStruct((s, Q_RANK), MXU_DTYPE),
                   jax.ShapeDtypeStruct((s, KV_RANK), MXU_DTYPE)],
        compiler_params=_cp(("parallel",), 40),
    )(z, z, z, gq, gkv, wuq, wukv, c_t, s1_t, s2_t)


def _flash_fwd(q, k, v, z):
    s = q.shape[0]
    t = _tile(s)
    nq = s // t
    pw = 2 * HEAD_BLK

    def body(q_ref, k_ref, v_ref, gb_ref, o_ref, yb_ref, lse_ref):
        i = pl.program_id(1)
        qv = q_ref[...]
        qa, qb = qv[:, :HEAD_BLK], qv[:, HEAD_BLK:]
        lane = _lanes((t, HEAD_BLK))
        lo = lane < V_DIM

        def scores(qh, kh, masked):
            sc = _dot_nt(qh, kh) * ATT_SCALE
            if masked:
                sc = jnp.where(_lanes((t, t)) <= _rows((t, t)), sc, -1e30)
            return sc

        def update(sc, m, l):
            m_new = jnp.maximum(m, jnp.max(sc, axis=-1, keepdims=True))
            alpha = jnp.exp(m - m_new)
            p = jnp.exp(sc - m_new)
            return m_new, alpha * l + jnp.sum(p, axis=-1, keepdims=True), alpha, p

        def step(j, carry, masked):
            ma, la, mb, lb, acc = carry
            kv_rows = pl.ds(pl.multiple_of(j * t, t), t)
            kt = k_ref[kv_rows, :]
            vt = v_ref[kv_rows, :]
            lane_v = _lanes(vt.shape)
            v_lo = jnp.where(lane_v < V_DIM, vt, jnp.zeros_like(vt))
            v_hi = jnp.where(lane_v >= V_DIM, vt, jnp.zeros_like(vt))
            ma, la, al_a, pa = update(scores(qa, kt[:, :HEAD_BLK], masked), ma, la)
            mb, lb, al_b, pb = update(scores(qb, kt[:, HEAD_BLK:], masked), mb, lb)
            acc = acc * jnp.where(lo, al_a, al_b) + _dot(pa, v_lo) + _dot(pb, v_hi)
            return ma, la, mb, lb, acc

        neg = jnp.full((t, 1), -1e30, F32)
        zero = jnp.zeros((t, 1), F32)
        init = (neg, zero, neg, zero, jnp.zeros((t, HEAD_BLK), F32))
        carry = lax.fori_loop(0, i, lambda j, cr: step(j, cr, False), init)
        ma, la, mb, lb, acc = step(i, carry, True)
        o = acc * jnp.where(lo, 1.0 / la, 1.0 / lb)
        o_ref[...] = o
        gb = gb_ref[...]
        yb_ref[...] = (o * (gb * _sig(gb))).astype(yb_ref.dtype)
        lse = jnp.where(lo, ma + jnp.log(la), mb + jnp.log(lb))
        pick = ((_rows((SUBLANES, HEAD_BLK)) == 0) & (_lanes((SUBLANES, HEAD_BLK)) == 0)) | (
            (_rows((SUBLANES, HEAD_BLK)) == 1) & (_lanes((SUBLANES, HEAD_BLK)) == V_DIM))
        lse_ref[0, 0] = lax.dot_general(pick.astype(F32), lse, (((1,), (1,)), ((), ())),
                                        precision=lax.Precision.HIGHEST, preferred_element_type=F32)

    return pl.pallas_call(
        body, name="flash_fwd", grid=(N_PAIR, nq),
        in_specs=[pl.BlockSpec((t, pw), lambda p, i: (i, p)), pl.BlockSpec((s, pw), lambda p, i: (0, p)),
                  pl.BlockSpec((s, HEAD_BLK), lambda p, i: (0, p)),
                  pl.BlockSpec((t, HEAD_BLK), lambda p, i: (i, 9 + p))],
        out_specs=[pl.BlockSpec((t, HEAD_BLK), lambda p, i: (i, p)), pl.BlockSpec((t, HEAD_BLK), lambda p, i: (i, p)),
                   pl.BlockSpec((1, 1, SUBLANES, t), lambda p, i: (p, i, 0, 0))],
        out_shape=[jax.ShapeDtypeStruct((s, D_MLA), F32), jax.ShapeDtypeStruct((s, D_MLA), MXU_DTYPE),
                   jax.ShapeDtypeStruct((N_PAIR, nq, SUBLANES, t), F32)],
        compiler_params=_cp(("parallel", "parallel"), 48),
    )(q, k, v, z)


def _attn_bwd_prep(dy, o, z):
    s = o.shape[0]
    t = _tile(s)
    nq = s // t
    rows = N_PAIR * SUBLANES

    def body(dy_ref, o_ref, gb_ref, do_ref, dgb_ref, dl_ref):
        gb = gb_ref[...]
        sg = _sig(gb)
        dyb = dy_ref[...]
        ov = o_ref[...]
        do = dyb * (gb * sg)
        do_ref[...] = do.astype(do_ref.dtype)
        dgb_ref[...] = (dyb * ov * (sg * (1.0 + gb * (1.0 - sg)))).astype(dgb_ref.dtype)
        r = _rows((rows, D_MLA))
        head = (r // SUBLANES) * 2 + (r % SUBLANES)
        sel = ((r % SUBLANES) < 2) & (_lanes((rows, D_MLA)) // V_DIM == head)
        dl = lax.dot_general(sel.astype(F32), do * ov, (((1,), (1,)), ((), ())),
                             precision=lax.Precision.HIGHEST, preferred_element_type=F32)
        for p in range(N_PAIR):
            dl_ref[p, 0] = dl[p * SUBLANES:(p + 1) * SUBLANES]

    return pl.pallas_call(
        body, name="attn_bwd_prep", grid=(nq,),
        in_specs=[pl.BlockSpec((t, D_MLA), lambda i: (i, 1)), pl.BlockSpec((t, D_MLA), lambda i: (i, 0)),
                  pl.BlockSpec((t, D_MLA), lambda i: (i, 3))],
        out_specs=[pl.BlockSpec((t, D_MLA), lambda i: (i, 0)), pl.BlockSpec((t, D_MLA), lambda i: (i, 0)),
                   pl.BlockSpec((N_PAIR, 1, SUBLANES, t), lambda i: (0, i, 0, 0))],
        out_shape=[jax.ShapeDtypeStruct((s, D_MLA), MXU_DTYPE), jax.ShapeDtypeStruct((s, D_MLA), MXU_DTYPE),
                   jax.ShapeDtypeStruct((N_PAIR, nq, SUBLANES, t), F32)],
        compiler_params=_cp(("parallel",), 32),
    )(dy, o, z)


def _flash_bwd(q, k, v, do, lse, delta):
    s = q.shape[0]
    t = _tile(s)
    nq = s // t
    pw = 2 * HEAD_BLK

    def body(q_ref, do_ref, lse_ref, dl_ref, k_ref, v_ref, dq_ref, dk_ref, dv_ref):
        j = pl.program_id(1)

        @pl.when(j == 0)
        def _():
            dq_ref[...] = jnp.zeros_like(dq_ref)

        kt = k_ref[...]
        ka, kb = kt[:, :HEAD_BLK], kt[:, HEAD_BLK:]
        vt = v_ref[...]

        def head(kh, qh, do_h, lse_row, dl_row, masked):
            st = _dot_nt(kh, qh) * ATT_SCALE
            if masked:
                st = jnp.where(_rows((t, t)) <= _lanes((t, t)), st, -1e30)
            pt = jnp.exp(st - lse_row)
            dv_h = _dot(pt, do_h)
            dst = (pt * (_dot_nt(vt, do_h) - dl_row) * ATT_SCALE).astype(MXU_DTYPE)
            return dv_h, _dot(dst, qh), _dot_tn(dst, kh)

        def step(i, carry, masked):
            dka, dkb, dv = carry
            q_rows = pl.ds(pl.multiple_of(i * t, t), t)
            qv = q_ref[q_rows, :]
            dov = do_ref[q_rows, :]
            lane = _lanes(dov.shape)
            do_lo = jnp.where(lane < V_DIM, dov, jnp.zeros_like(dov))
            do_hi = jnp.where(lane >= V_DIM, dov, jnp.zeros_like(dov))
            dva, dk_a, dq_a = head(ka, qv[:, :HEAD_BLK], do_lo, lse_ref[0, i, 0:1, :], dl_ref[0, i, 0:1, :], masked)
            dvb, dk_b, dq_b = head(kb, qv[:, HEAD_BLK:], do_hi, lse_ref[0, i, 1:2, :], dl_ref[0, i, 1:2, :], masked)
            dq_ref[q_rows, 0:HEAD_BLK] += dq_a
            dq_ref[q_rows, HEAD_BLK:pw] += dq_b
            return dka + dk_a, dkb + dk_b, dv + dva + dvb

        zero = jnp.zeros((t, HEAD_BLK), F32)
        carry = step(j, (zero, zero, zero), True)
        dka, dkb, dv = lax.fori_loop(j + 1, nq, lambda i, cr: step(i, cr, False), carry)
        dk_ref[:, 0:HEAD_BLK] = dka
        dk_ref[:, HEAD_BLK:pw] = dkb
        dv_ref[...] = dv.astype(dv_ref.dtype)

    return pl.pallas_call(
        body, name="flash_bwd", grid=(N_PAIR, nq),
        in_specs=[pl.BlockSpec((s, pw), lambda p, j: (0, p)), pl.BlockSpec((s, HEAD_BLK), lambda p, j: (0, p)),
                  pl.BlockSpec((1, nq, SUBLANES, t), lambda p, j: (p, 0, 0, 0)),
                  pl.BlockSpec((1, nq, SUBLANES, t), lambda p, j: (p, 0, 0, 0)),
                  pl.BlockSpec((t, pw), lambda p, j: (j, p)), pl.BlockSpec((t, HEAD_BLK), lambda p, j: (j, p))],
        out_specs=[pl.BlockSpec((s, pw), lambda p, j: (0, p)), pl.BlockSpec((t, pw), lambda p, j: (j, p)),
                   pl.BlockSpec((t, HEAD_BLK), lambda p, j: (j, p))],
        out_shape=[jax.ShapeDtypeStruct((s, MLA_HEADS * HEAD_BLK), F32),
                   jax.ShapeDtypeStruct((s, MLA_HEADS * HEAD_BLK), F32),
                   jax.ShapeDtypeStruct((s, D_MLA), MXU_DTYPE)],
        compiler_params=_cp(("parallel", "arbitrary"), 56),
    )(q, do, lse, delta, k, v)


def _mla_prep_bwd(dq, dk, dv, z, gq, gkv, wuq, wukv, c_t, s1_t, s2_t):
    s = z.shape[0]
    t = _tile(s)
    hq = MLA_HEADS * HEAD_BLK

    def body(dq_ref, dk_ref, dv_ref, cq_ref, ckv_ref, gq_ref, gkv_ref, wuq_ref, wukv_ref, c_ref, s1_ref, s2_ref,
             dcq_ref, dckv_ref, dkr_ref, dqu_ref, dkvp_ref, dgq_ref, dgkv_ref):
        @pl.when(pl.program_id(0) == 0)
        def _():
            dgq_ref[...] = jnp.zeros_like(dgq_ref)
            dgkv_ref[...] = jnp.zeros_like(dgkv_ref)

        ct, s1, s2 = c_ref[...], s1_ref[...], s2_ref[...]
        dk_sum = jnp.zeros((t, HEAD_BLK), F32)
        for h in range(MLA_HEADS):
            blk = slice(h * HEAD_BLK, (h + 1) * HEAD_BLK)
            dqu_ref[:, blk] = _unrope(dq_ref[:, blk], ct, s1, s2).astype(dqu_ref.dtype)
            dkh = dk_ref[:, blk]
            dk_sum = dk_sum + dkh
            dkvp_ref[:, blk] = dkh.astype(dkvp_ref.dtype)
        dkvp_ref[:, hq:] = dv_ref[...]
        lane = _lanes((t, HEAD_BLK))
        rope_lanes = (lane >= KR_LANE0) & (lane < KR_LANE0 + QK_ROPE)
        dkr_ref[...] = _unrope(jnp.where(rope_lanes, dk_sum, 0.0), ct, s1, s2).astype(dkr_ref.dtype)

        def norm_bwd(c_in, g, dn_out, dc_ref, dg_ref):
            rs = lax.rsqrt(jnp.mean(c_in * c_in, axis=-1, keepdims=True) + EPS)
            n = c_in * rs
            dg_ref[...] += jnp.sum(dn_out * n, axis=0, keepdims=True)
            dn = dn_out * g
            dc_ref[...] = (rs * (dn - n * jnp.mean(dn * n, axis=-1, keepdims=True))).astype(dc_ref.dtype)

        norm_bwd(cq_ref[...], gq_ref[...], _dot_nt(dqu_ref[...], wuq_ref[...]), dcq_ref, dgq_ref)
        norm_bwd(ckv_ref[...], gkv_ref[...], _dot_nt(dkvp_ref[...], wukv_ref[...]), dckv_ref, dgkv_ref)

    full = lambda shp: pl.BlockSpec(shp, lambda i: (0, 0))
    tab = pl.BlockSpec((t, HEAD_BLK), lambda i: (i, 0))
    row = lambda w: pl.BlockSpec((t, w), lambda i: (i, 0))
    return pl.pallas_call(
        body, name="mla_prep_bwd", grid=(s // t,),
        in_specs=[row(hq), row(hq), row(D_MLA),
                  pl.BlockSpec((t, Q_RANK), lambda i: (i, 2)), pl.BlockSpec((t, KV_RANK), lambda i: (i, 6)),
                  full((1, Q_RANK)), full((1, KV_RANK)), full((Q_RANK, hq)), full((KV_RANK, hq + D_MLA)),
                  tab, tab, tab],
        out_specs=[row(Q_RANK), row(KV_RANK), row(HEAD_BLK), row(hq), row(hq + D_MLA),
                   full((1, Q_RANK)), full((1, KV_RANK))],
        out_shape=[jax.ShapeDtypeStruct((s, Q_RANK), MXU_DTYPE), jax.ShapeDtypeStruct((s, KV_RANK), MXU_DTYPE),
                   jax.ShapeDtypeStruct((s, HEAD_BLK), MXU_DTYPE), jax.ShapeDtypeStruct((s, hq), MXU_DTYPE),
                   jax.ShapeDtypeStruct((s, hq + D_MLA), MXU_DTYPE),
                   jax.ShapeDtypeStruct((1, Q_RANK), F32), jax.ShapeDtypeStruct((1, KV_RANK), F32)],
        compiler_params=_cp(("arbitrary",), 48),
    )(dq, dk, dv, z, z, gq, gkv, wuq, wukv, c_t, s1_t, s2_t)


def _outproj_fwd(x, ya, yb, yc, w):
    s = x.shape[0]
    t = _tile(s)

    def body(x_ref, ya_ref, yb_ref, yc_ref, wa_ref, wb_ref, wc_ref, o_ref):
        o_ref[...] = (x_ref[...] + _dot(ya_ref[...], wa_ref[...]) + _dot(yb_ref[...], wb_ref[...])
                      + _dot(yc_ref[...], wc_ref[...]))

    row = lambda w_: pl.BlockSpec((t, w_), lambda i: (i, 0))
    return pl.pallas_call(
        body, name="outproj_fwd", grid=(s // t,),
        in_specs=[row(D_MODEL), row(D_LRU), row(D_MLA), row(D_POOL),
                  pl.BlockSpec((D_LRU, D_MODEL), lambda i: (0, 0)), pl.BlockSpec((D_MLA, D_MODEL), lambda i: (1, 0)),
                  pl.BlockSpec((D_POOL, D_MODEL), lambda i: (3, 0))],
        out_specs=row(D_MODEL),
        out_shape=jax.ShapeDtypeStruct((s, D_MODEL), F32),
        compiler_params=_cp(("parallel",), 40),
    )(x, ya, yb, yc, w, w, w)


def _mm_nt(a, b, name):
    s, kd = a.shape
    n = b.shape[0]
    t = _tile(s)

    def body(a_ref, b_ref, o_ref):
        o_ref[...] = _dot_nt(a_ref[...], b_ref[...])

    return pl.pallas_call(
        body, name=name, grid=(s // t,),
        in_specs=[pl.BlockSpec((t, kd), lambda i: (i, 0)), pl.BlockSpec((n, kd), lambda i: (0, 0))],
        out_specs=pl.BlockSpec((t, n), lambda i: (i, 0)),
        out_shape=jax.ShapeDtypeStruct((s, n), F32),
        compiler_params=_cp(("parallel",), 40),
    )(a, b)


def _mm_tn(a, b, name):
    s, k1 = a.shape
    n = b.shape[1]
    t = _tile(s)

    def body(a_ref, b_ref, o_ref):
        @pl.when(pl.program_id(0) == 0)
        def _():
            o_ref[...] = jnp.zeros_like(o_ref)

        o_ref[...] += _dot_tn(a_ref[...], b_ref[...])

    return pl.pallas_call(
        body, name=name, grid=(s // t,),
        in_specs=[pl.BlockSpec((t, k1), lambda i: (i, 0)), pl.BlockSpec((t, n), lambda i: (i, 0))],
        out_specs=pl.BlockSpec((k1, n), lambda i: (0, 0)),
        out_shape=jax.ShapeDtypeStruct((k1, n), F32),
        compiler_params=_cp(("arbitrary",), 56),
    )(a, b)


def _inproj_bwd(dz, w, x, g, dxn):
    s = x.shape[0]
    t = _tile(s)

    def body(dz_ref, w_ref, x_ref, g_ref, dxn_ref, dx_ref, dg_ref):
        @pl.when(pl.program_id(0) == 0)
        def _():
            dg_ref[...] = jnp.zeros_like(dg_ref)

        dh = _dot_nt(dz_ref[...], w_ref[...])
        xv = x_ref[...]
        rs = lax.rsqrt(jnp.mean(xv * xv, axis=-1, keepdims=True) + EPS)
        n = xv * rs
        dg_ref[...] += jnp.sum(dh * n, axis=0, keepdims=True)
        dn = dh * g_ref[...]
        dx_ref[...] = dxn_ref[...] + rs * (dn - n * jnp.mean(dn * n, axis=-1, keepdims=True))

    row = lambda w_: pl.BlockSpec((t, w_), lambda i: (i, 0))
    return pl.pallas_call(
        body, name="inproj_bwd", grid=(s // t,),
        in_specs=[row(D_INP), pl.BlockSpec((D_MODEL, D_INP), lambda i: (0, 0)), row(D_MODEL),
                  pl.BlockSpec((1, D_MODEL), lambda i: (0, 0)), row(D_MODEL)],
        out_specs=[row(D_MODEL), pl.BlockSpec((1, D_MODEL), lambda i: (0, 0))],
        out_shape=[jax.ShapeDtypeStruct((s, D_MODEL), F32), jax.ShapeDtypeStruct((1, D_MODEL), F32)],
        compiler_params=_cp(("arbitrary",), 48),
    )(dz, w, x, g, dxn)


def _loss_head(x, g, tgt):
    s = x.shape[0]
    t = _tile(s)

    def body(x_ref, g_ref, t_ref, dx_ref, loss_ref, dg_ref):
        @pl.when(pl.program_id(0) == 0)
        def _():
            loss_ref[...] = jnp.zeros_like(loss_ref)
            dg_ref[...] = jnp.zeros_like(dg_ref)

        xv = x_ref[...]
        rs = lax.rsqrt(jnp.mean(xv * xv, axis=-1, keepdims=True) + EPS)
        n = xv * rs
        gv = g_ref[...]
        e = n * gv - t_ref[...]
        loss_ref[...] += 0.5 * jnp.sum(jnp.mean(e * e, axis=-1, keepdims=True))
        dyf = e * (1.0 / D_MODEL)
        dg_ref[...] += jnp.sum(dyf * n, axis=0, keepdims=True)
        dn = dyf * gv
        dx_ref[...] = rs * (dn - n * jnp.mean(dn * n, axis=-1, keepdims=True))

    row = pl.BlockSpec((t, D_MODEL), lambda i: (i, 0))
    vec = pl.BlockSpec((1, D_MODEL), lambda i: (0, 0))
    return pl.pallas_call(
        body, name="loss_head", grid=(s // t,),
        in_specs=[row, vec, row],
        out_specs=[row, pl.BlockSpec((1, LANES), lambda i: (0, 0)), vec],
        out_shape=[jax.ShapeDtypeStruct((s, D_MODEL), F32), jax.ShapeDtypeStruct((1, LANES), F32),
                   jax.ShapeDtypeStruct((1, D_MODEL), F32)],
        compiler_params=_cp(("arbitrary",), 32),
    )(x, g, tgt)


def _block_diag(w):
    h, d, _ = w.shape
    return jnp.einsum('hij,hk->hikj', w, jnp.eye(h, dtype=w.dtype)).reshape(h * d, h * d)


def _diag_blocks(wfull, h):
    d = wfull.shape[0] // h
    return jnp.stack([wfull[i * d:(i + 1) * d, i * d:(i + 1) * d] for i in range(h)])


def _permute_w_in(w):
    pad = lambda n: jnp.zeros(w.shape[:-1] + (n,), w.dtype)
    return jnp.concatenate([w[..., 0:1152], w[..., 1440:1824], w[..., 1152:1408], w[..., 1824:2336],
                            pad(KR_LANE0), w[..., 1408:1440], pad(HEAD_BLK - KR_LANE0 - QK_ROPE)], axis=-1)


def _unpermute_w_in(w):
    return jnp.concatenate([w[..., 0:1152], w[..., 1536:1792], w[..., 2304 + KR_LANE0:2304 + KR_LANE0 + QK_ROPE],
                            w[..., 1152:1536], w[..., 1792:2304]], axis=-1)


def _pad_w_uq(w):
    w3 = w.reshape(Q_RANK, MLA_HEADS, QK_NOPE + QK_ROPE)
    return jnp.pad(w3, ((0, 0), (0, 0), (0, HEAD_BLK - QK_NOPE - QK_ROPE))).reshape(Q_RANK, MLA_HEADS * HEAD_BLK)


def _unpad_w_uq(w):
    return w.reshape(Q_RANK, MLA_HEADS, HEAD_BLK)[:, :, :QK_NOPE + QK_ROPE].reshape(Q_RANK, -1)


def _pad_w_ukv(w):
    w3 = w.reshape(KV_RANK, MLA_HEADS, QK_NOPE + V_DIM)
    kpart = jnp.pad(w3[:, :, :QK_NOPE], ((0, 0), (0, 0), (0, HEAD_BLK - QK_NOPE))).reshape(KV_RANK, -1)
    return jnp.concatenate([kpart, w3[:, :, QK_NOPE:].reshape(KV_RANK, -1)], axis=1)


def _unpad_w_ukv(w):
    hq = MLA_HEADS * HEAD_BLK
    kpart = w[:, :hq].reshape(KV_RANK, MLA_HEADS, HEAD_BLK)[:, :, :QK_NOPE]
    vpart = w[:, hq:].reshape(KV_RANK, MLA_HEADS, V_DIM)
    return jnp.concatenate([kpart, vpart], axis=2).reshape(KV_RANK, -1)


def _local_step(x, tgt, w):
    s = x.shape[0]
    tabs = _rope_tables(s)
    lw = []
    for l in range(DEPTH):
        lw.append(dict(
            g=w['norm_g'][l][None], w_in=_permute_w_in(w['w_in'][l]).astype(MXU_DTYPE),
            cw=w['conv_w'][l], cb=w['conv_b'][l][None],
            wr=_block_diag(w['w_rg'][l]).astype(MXU_DTYPE), br=w['b_rg'][l][None],
            wi=_block_diag(w['w_ig'][l]).astype(MXU_DTYPE), bi=w['b_ig'][l][None],
            lam=w['lru_lambda'][l][None], gq=w['q_norm_g'][l][None], gkv=w['kv_norm_g'][l][None],
            wuq=_pad_w_uq(w['w_uq'][l]).astype(MXU_DTYPE), wukv=_pad_w_ukv(w['w_ukv'][l]).astype(MXU_DTYPE),
            wp=_block_diag(w['w_pool'][l]).astype(MXU_DTYPE), ps=w['pool_scale'][l][None],
            wout=w['w_out'][l].astype(MXU_DTYPE)))

    saved = []
    for l in range(DEPTH):
        p = lw[l]
        h, z = _inproj_fwd(x, p['g'], p['w_in'])
        ya, hs = _lru_fwd(z, p['cw'], p['cb'], p['wr'], p['br'], p['wi'], p['bi'], p['lam'])
        yc = _pool_fwd(z, p['wp'], p['ps'])
        q, k, v, qn, kvn = _mla_prep_fwd(z, p['gq'], p['gkv'], p['wuq'], p['wukv'], *tabs)
        o, yb, lse = _flash_fwd(q, k, v, z)
        saved.append(dict(x=x, h=h, z=z, hs=hs, ya=ya, yb=yb, yc=yc, q=q, k=k, v=v, qn=qn, kvn=kvn, o=o, lse=lse))
        x = _outproj_fwd(x, ya, yb, yc, p['wout'])

    dx, loss, dgf = _loss_head(x, w['final_norm_g'][None], tgt)
    grads = {n: [None] * DEPTH for n in WEIGHT_NAMES if n != 'final_norm_g'}
    grads['final_norm_g'] = dgf[0]
    for l in reversed(range(DEPTH)):
        p, sv = lw[l], saved[l]
        dy = _mm_nt(dx, p['wout'], "outproj_bwd_dy")
        grads['w_out'][l] = jnp.concatenate([_mm_tn(sv['ya'], dx, "dwout_a"), _mm_tn(sv['yb'], dx, "dwout_b"),
                                             _mm_tn(sv['yc'], dx, "dwout_c")], axis=0)
        (dza, dga, dwr, dwi, dcw, dcb, dbr, dbi, dlam) = _lru_bwd(
            sv['z'], sv['hs'], dy, p['cw'], p['cb'], p['wr'], p['br'], p['wi'], p['bi'], p['lam'])
        dzc, dgc, dwp, dps = _pool_bwd(sv['z'], dy, p['wp'], p['ps'])
        do, dgb, delta = _attn_bwd_prep(dy, sv['o'], sv['z'])
        dq, dk, dv = _flash_bwd(sv['q'], sv['k'], sv['v'], do, sv['lse'], delta)
        dcq, dckv, dkr, dqu, dkvp, dgq, dgkv = _mla_prep_bwd(dq, dk, dv, sv['z'], p['gq'], p['gkv'], p['wuq'],
                                                             p['wukv'], *tabs)
        dz = jnp.concatenate([dza, dga, dcq, dgb, dckv, dzc, dgc, dkr], axis=1)
        grads['w_in'][l] = _unpermute_w_in(_mm_tn(sv['h'], dz, "dwin"))
        grads['w_uq'][l] = _unpad_w_uq(_mm_tn(sv['qn'], dqu, "dwuq"))
        grads['w_ukv'][l] = _unpad_w_ukv(_mm_tn(sv['kvn'], dkvp, "dwukv"))
        dx, dg = _inproj_bwd(dz, p['w_in'], sv['x'], p['g'], dx)
        grads['norm_g'][l] = dg[0]
        grads['conv_w'][l] = dcw
        grads['conv_b'][l] = dcb[0]
        grads['w_rg'][l] = _diag_blocks(dwr, LRU_HEADS)
        grads['b_rg'][l] = dbr[0]
        grads['w_ig'][l] = _diag_blocks(dwi, LRU_HEADS)
        grads['b_ig'][l] = dbi[0]
        grads['lru_lambda'][l] = dlam[0]
        grads['q_norm_g'][l] = dgq[0]
        grads['kv_norm_g'][l] = dgkv[0]
        grads['w_pool'][l] = _diag_blocks(dwp, len(POOL_WINDOWS))
        grads['pool_scale'][l] = dps[0]
    for n in grads:
        if n != 'final_norm_g':
            grads[n] = jnp.stack(grads[n])
    return loss[0, 0], dx, grads


WIRE_DTYPE = jnp.bfloat16
MESH_IDS = pl.DeviceIdType.MESH
_HBM = pl.BlockSpec(memory_space=pltpu.HBM)


def _coords():
    return lax.axis_index("x"), lax.axis_index("y"), lax.axis_index("c")


def _all_gather8(xb, name):
    r, cdim = xb.shape

    def body(x_ref, out_ref, send_sems, recv_sems, local_sem):
        x, y, c = _coords()
        me, sibling = (x, y, c), (x, y, 1 - c)
        chips = [(1 - x, y), (x, 1 - y), (1 - x, 1 - y)]

        def slot(px, py, pc):
            return out_ref.at[4 * px + 2 * py + pc]

        def copy(k, block, to, src=None):
            return pltpu.make_async_remote_copy(
                src_ref=slot(*block) if src is None else src, dst_ref=slot(*block),
                send_sem=send_sems.at[k], recv_sem=recv_sems.at[k], device_id=to, device_id_type=MESH_IDS)

        mine = pltpu.make_async_copy(x_ref, slot(*me), local_sem)
        mine.start()
        first = [copy(0, me, sibling, src=x_ref)]
        first += [copy(1 + j, me, (*chip, c), src=x_ref) for j, chip in enumerate(chips)]
        for cp in first:
            cp.start()
        passed = [copy(4 + j, (*chip, c), sibling) for j, chip in enumerate(chips)]
        for j, chip in enumerate(chips):
            copy(1 + j, (*chip, c), me).wait_recv()
            passed[j].start()
        copy(0, sibling, me).wait_recv()
        for j, chip in enumerate(chips):
            copy(4 + j, (*chip, 1 - c), me).wait_recv()
        for cp in first + passed:
            cp.wait_send()
        mine.wait()

    return pl.pallas_call(
        body, name=name, out_shape=jax.ShapeDtypeStruct((N_DEV, r, cdim), xb.dtype),
        in_specs=[_HBM], out_specs=_HBM,
        scratch_shapes=[pltpu.SemaphoreType.DMA((7,)), pltpu.SemaphoreType.DMA((7,)), pltpu.SemaphoreType.DMA],
    )(xb)


def _exchange8(contrib, name):
    _, r, cdim = contrib.shape

    def body(c_ref, out_ref, send_sems, recv_sems, local_sem):
        x, y, c = _coords()
        own = pltpu.make_async_copy(c_ref.at[4 * x + 2 * y + c], out_ref.at[0], local_sem)
        own.start()
        copies = []
        for k in range(1, N_DEV):
            px = 1 - x if k & 4 else x
            py = 1 - y if k & 2 else y
            pc = 1 - c if k & 1 else c
            cp = pltpu.make_async_remote_copy(
                src_ref=c_ref.at[4 * px + 2 * py + pc], dst_ref=out_ref.at[k],
                send_sem=send_sems.at[k - 1], recv_sem=recv_sems.at[k - 1],
                device_id=(px, py, pc), device_id_type=MESH_IDS)
            cp.start()
            copies.append(cp)
        for cp in copies:
            cp.wait()
        own.wait()

    return pl.pallas_call(
        body, name=name, out_shape=jax.ShapeDtypeStruct(contrib.shape, contrib.dtype),
        in_specs=[_HBM], out_specs=_HBM,
        scratch_shapes=[pltpu.SemaphoreType.DMA((7,)), pltpu.SemaphoreType.DMA((7,)), pltpu.SemaphoreType.DMA],
    )(contrib)


def _sibling_gather(xb, name):
    r, cdim = xb.shape

    def body(x_ref, out_ref, send_sem, recv_sem, local_sem):
        x, y, c = _coords()
        own = pltpu.make_async_copy(x_ref, out_ref.at[c], local_sem)
        own.start()
        cp = pltpu.make_async_remote_copy(src_ref=x_ref, dst_ref=out_ref.at[c], send_sem=send_sem, recv_sem=recv_sem,
                                          device_id=(x, y, 1 - c), device_id_type=MESH_IDS)
        cp.start()
        cp.wait()
        own.wait()

    return pl.pallas_call(
        body, name=name, out_shape=jax.ShapeDtypeStruct((2, r, cdim), xb.dtype),
        in_specs=[_HBM], out_specs=_HBM,
        scratch_shapes=[pltpu.SemaphoreType.DMA, pltpu.SemaphoreType.DMA, pltpu.SemaphoreType.DMA],
    )(xb)


SUM_ROWS = 512


def _sum_slots(slots, name):
    n, r, cdim = slots.shape
    tr = r if r <= SUM_ROWS else math.gcd(r, SUM_ROWS)

    def body(s_ref, o_ref):
        acc = s_ref[0].astype(F32)
        for k in range(1, n):
            acc = acc + s_ref[k].astype(F32)
        o_ref[...] = acc

    return pl.pallas_call(
        body, name=name, grid=(r // tr,),
        in_specs=[pl.BlockSpec((n, tr, cdim), lambda i: (0, i, 0))],
        out_specs=pl.BlockSpec((tr, cdim), lambda i: (i, 0)),
        out_shape=jax.ShapeDtypeStruct((r, cdim), F32),
        compiler_params=_cp(("parallel",), 32),
    )(slots)


def _adamw(w, g, m, v, name):
    r, cdim = w.shape
    tr = math.gcd(r, 512) if r * cdim * 4 > (1 << 20) else r

    def body(w_ref, g_ref, m_ref, v_ref, d_ref, mo_ref, vo_ref):
        gv = g_ref[...]
        mn = ADAM_B1 * m_ref[...] + (1.0 - ADAM_B1) * gv
        vn = ADAM_B2 * v_ref[...] + (1.0 - ADAM_B2) * (gv * gv)
        mo_ref[...] = mn
        vo_ref[...] = vn
        m_hat = mn / (1.0 - ADAM_B1 ** ADAM_STEP)
        v_hat = vn / (1.0 - ADAM_B2 ** ADAM_STEP)
        d_ref[...] = (-ADAM_LR) * (m_hat / (jnp.sqrt(v_hat) + ADAM_EPS) + ADAM_WD * w_ref[...])

    blk = pl.BlockSpec((tr, cdim), lambda i: (i, 0))
    return pl.pallas_call(
        body, name=name, grid=(r // tr,),
        in_specs=[blk] * 4, out_specs=[blk] * 3,
        out_shape=[jax.ShapeDtypeStruct((r, cdim), F32)] * 3,
        compiler_params=_cp(("parallel",), 40),
    )(w, g, m, v)


HALF = DEPTH // 2
N_SHARD = 4
BIG = ['w_in', 'w_uq', 'w_ukv', 'w_out']
SHARD_AXIS = {'w_in': 2, 'conv_w': 2, 'w_uq': 2, 'w_ukv': 2, 'w_out': 1}
FULL_SHAPE = {'w_in': (DEPTH, D_MODEL, D_IN), 'conv_w': (DEPTH, CONV_WIDTH, D_LRU),
              'w_uq': (DEPTH, Q_RANK, MLA_HEADS * (QK_NOPE + QK_ROPE)),
              'w_ukv': (DEPTH, KV_RANK, MLA_HEADS * (QK_NOPE + V_DIM)), 'w_out': (DEPTH, D_MIX, D_MODEL)}


def _shard_shape(n):
    shp = list(FULL_SHAPE[n])
    shp[SHARD_AXIS[n]] //= N_SHARD
    return tuple(shp)


def _pad_rows(flat, row_mult):
    n = flat.shape[0]
    chunk = row_mult * LANES
    total = -(-n // chunk) * chunk
    return jnp.pad(flat, (0, total - n)).reshape(total // LANES, LANES)


def _gather_weights(local):
    c = lax.axis_index("c")
    parts = []
    for n in BIG:
        parts.append(lax.dynamic_slice_in_dim(local[n], HALF * c, HALF, axis=0).astype(WIRE_DTYPE).reshape(-1))
    cw = lax.dynamic_slice_in_dim(local['conv_w'], HALF * c, HALF, axis=0)
    parts.append(lax.bitcast_convert_type(cw, WIRE_DTYPE).reshape(-1))
    block = _pad_rows(jnp.concatenate(parts), 16)
    got = _all_gather8(block, "gather_weights").reshape(N_SHARD, 2, -1)
    full, off = {}, 0
    for n in BIG + ['conv_w']:
        shp = (HALF,) + _shard_shape(n)[1:]
        words = math.prod(shp) * (2 if n == 'conv_w' else 1)
        seg = got[:, :, off:off + words]
        off += words
        if n == 'conv_w':
            seg = lax.bitcast_convert_type(seg.reshape((N_SHARD, 2) + shp + (2,)), F32)
        else:
            seg = seg.reshape((N_SHARD, 2) + shp)
        ax = SHARD_AXIS[n] + 2
        seg = jnp.moveaxis(seg, 0, ax - 1)
        full[n] = seg.reshape(FULL_SHAPE[n])
    return full


def _reduce_big(grads):
    blocks = []
    for n in BIG:
        g = grads[n].astype(WIRE_DTYPE)
        shp = _shard_shape(n)
        ax = SHARD_AXIS[n]
        g = g.reshape((2, HALF) + g.shape[1:ax] + (N_SHARD, shp[ax]) + g.shape[ax + 1:])
        g = jnp.moveaxis(g, ax + 1, 0)
        blocks.append(g.reshape(N_DEV, -1))
    contrib = jnp.concatenate(blocks, axis=1)
    chunk = SUM_ROWS * LANES
    rows = -(-contrib.shape[1] // chunk) * SUM_ROWS
    contrib = jnp.pad(contrib, ((0, 0), (0, rows * LANES - contrib.shape[1])))
    got = _exchange8(contrib.reshape(N_DEV, rows, LANES), "exchange_big")
    mine = _sum_slots(got, "sum_big")
    both = _sibling_gather(mine, "sibling_big").reshape(2, -1)
    out, off = {}, 0
    for n in BIG:
        shp = _shard_shape(n)
        words = HALF * math.prod(shp[1:])
        out[n] = both[:, off:off + words].reshape(shp)
        off += words
    return out


SMALL = REPLICATED + ['conv_w']


def _pack_small(tree):
    return _pad_rows(jnp.concatenate([tree[n].reshape(-1) for n in SMALL]), N_DEV * SUBLANES)


def _unpack_small(flat2d, like):
    flat = flat2d.reshape(-1)
    out, off = {}, 0
    for n in SMALL:
        size = math.prod(like[n].shape)
        out[n] = flat[off:off + size].reshape(like[n].shape)
        off += size
    return out


def _reduce_small(grads):
    packed = _pack_small(grads)
    rows = packed.shape[0] // N_DEV
    got = _exchange8(packed.reshape(N_DEV, rows, LANES), "exchange_small")
    mine = _sum_slots(got, "sum_small")
    everyone = _all_gather8(mine, "gather_small")
    return _unpack_small(everyone, grads)


def kernel(x, norm_g, w_in, conv_w, conv_b, w_rg, b_rg, w_ig, b_ig, lru_lambda, q_norm_g, w_uq, kv_norm_g, w_ukv, w_pool, pool_scale, w_out, final_norm_g, loss_target, m_norm_g, m_w_in, m_conv_w, m_conv_b, m_w_rg, m_b_rg, m_w_ig, m_b_ig, m_lru_lambda, m_q_norm_g, m_w_uq, m_kv_norm_g, m_w_ukv, m_w_pool, m_pool_scale, m_w_out, m_final_norm_g, v_norm_g, v_w_in, v_conv_w, v_conv_b, v_w_rg, v_b_rg, v_w_ig, v_b_ig, v_lru_lambda, v_q_norm_g, v_w_uq, v_kv_norm_g, v_w_ukv, v_w_pool, v_pool_scale, v_w_out, v_final_norm_g):
    w_loc = dict(zip(WEIGHT_NAMES, (norm_g, w_in, conv_w, conv_b, w_rg, b_rg, w_ig, b_ig, lru_lambda, q_norm_g, w_uq,
                                    kv_norm_g, w_ukv, w_pool, pool_scale, w_out, final_norm_g)))
    m_loc = dict(zip(WEIGHT_NAMES, (m_norm_g, m_w_in, m_conv_w, m_conv_b, m_w_rg, m_b_rg, m_w_ig, m_b_ig, m_lru_lambda,
                                    m_q_norm_g, m_w_uq, m_kv_norm_g, m_w_ukv, m_w_pool, m_pool_scale, m_w_out,
                                    m_final_norm_g)))
    v_loc = dict(zip(WEIGHT_NAMES, (v_norm_g, v_w_in, v_conv_w, v_conv_b, v_w_rg, v_b_rg, v_w_ig, v_b_ig, v_lru_lambda,
                                    v_q_norm_g, v_w_uq, v_kv_norm_g, v_w_ukv, v_w_pool, v_pool_scale, v_w_out,
                                    v_final_norm_g)))
    w_full = dict(w_loc)
    w_full.update(_gather_weights(w_loc))
    loss_local, dx, g_local = _local_step(x[0], loss_target[0], w_full)
    loss = lax.psum(loss_local, ("x", "y", "c"))

    grads = _reduce_big(g_local)
    g_small = _reduce_small(g_local)
    shard = 2 * lax.axis_index("x") + lax.axis_index("y")
    width = D_LRU // N_SHARD
    grads['conv_w'] = lax.dynamic_slice_in_dim(g_small['conv_w'], shard * width, width, axis=2)
    for n in REPLICATED:
        grads[n] = g_small[n]

    delta, new_m, new_v = {}, {}, {}
    for n in BIG:
        two_d = lambda a: a.reshape(-1, a.shape[-1])
        d, mo, vo = _adamw(two_d(w_loc[n]), two_d(grads[n]), two_d(m_loc[n]), two_d(v_loc[n]), "adamw_" + n)
        delta[n], new_m[n], new_v[n] = (a.reshape(w_loc[n].shape) for a in (d, mo, vo))
    small_g = dict(grads)
    d, mo, vo = _adamw(_pack_small(w_loc), _pack_small(small_g), _pack_small(m_loc), _pack_small(v_loc), "adamw_small")
    for tree, flat in ((delta, d), (new_m, mo), (new_v, vo)):
        tree.update(_unpack_small(flat, w_loc))

    return (loss, dx[None], *[grads[n] for n in WEIGHT_NAMES], *[delta[n] for n in WEIGHT_NAMES],
            *[new_m[n] for n in WEIGHT_NAMES], *[new_v[n] for n in WEIGHT_NAMES])
```

```python
import functools
import math

import jax
import jax.numpy as jnp
import numpy as np
from jax import lax
from jax.experimental import pallas as pl
from jax.experimental.pallas import tpu as pltpu

F32 = jnp.float32
MXU_DTYPE = jnp.bfloat16

D_MODEL = 1024
DEPTH = 4
EPS = 1e-6
D_LRU = 384
LRU_HEADS = 6
CONV_WIDTH = 4
LRU_C = 8.0
MLA_HEADS = 6
QK_NOPE = 64
QK_ROPE = 32
V_DIM = 64
D_MLA = MLA_HEADS * V_DIM
Q_RANK = 384
KV_RANK = 256
ROPE_BASE = 10000.0
POOL_WINDOWS = (2, 4, 8, 16)
D_POOL = 256
D_MIX = D_LRU + D_MLA + D_POOL
D_IN = 2336
ATT_SCALE = (QK_NOPE + QK_ROPE) ** -0.5
LOG2_E = 1.4426950408889634
LN_2 = 0.6931471805599453
Q_PRESCALE = ATT_SCALE * LOG2_E

ADAM_LR = 0.001
ADAM_B1 = 0.9
ADAM_B2 = 0.999
ADAM_EPS = 1e-08
ADAM_WD = 0.01
ADAM_STEP = 10

LANES = 128
SUBLANES = 8
V7X_VMEM_BYTES = 64 << 20
N_DEV = 8

D_INP = 2432
KR_LANE0 = 64
HEAD_BLK = 128
N_PAIR = MLA_HEADS // 2

WEIGHT_NAMES = ['norm_g', 'w_in', 'conv_w', 'conv_b', 'w_rg', 'b_rg', 'w_ig', 'b_ig', 'lru_lambda', 'q_norm_g',
                'w_uq', 'kv_norm_g', 'w_ukv', 'w_pool', 'pool_scale', 'w_out', 'final_norm_g']
SHARDED = ['w_in', 'conv_w', 'w_uq', 'w_ukv', 'w_out']
REPLICATED = [n for n in WEIGHT_NAMES if n not in SHARDED]


def _cp(sem, vmem_mb=None):
    return pltpu.CompilerParams(dimension_semantics=sem,
                                vmem_limit_bytes=None if vmem_mb is None else vmem_mb << 20)


def _dot(a, b):
    return jnp.dot(a.astype(MXU_DTYPE), b.astype(MXU_DTYPE), preferred_element_type=F32)


def _dot_nt(a, b):
    return lax.dot_general(a.astype(MXU_DTYPE), b.astype(MXU_DTYPE), (((1,), (1,)), ((), ())),
                           preferred_element_type=F32)


def _dot_tn(a, b):
    return lax.dot_general(a.astype(MXU_DTYPE), b.astype(MXU_DTYPE), (((0,), (0,)), ((), ())),
                           preferred_element_type=F32)


def _sig(x):
    return 1.0 / (1.0 + jnp.exp(-x))


def _down(x, k):
    return pltpu.roll(x, k, 0)


def _up(x, k):
    return pltpu.roll(x, x.shape[0] - k, 0)


def _rows(shape):
    return lax.broadcasted_iota(jnp.int32, shape, 0)


def _lanes(shape):
    return lax.broadcasted_iota(jnp.int32, shape, 1)


def _tile(s):
    return min(512, s)


def _inproj_fwd(x, g, w):
    s = x.shape[0]
    t = _tile(s)

    def body(x_ref, g_ref, w_ref, h_ref, z_ref):
        xv = x_ref[...]
        rs = lax.rsqrt(jnp.mean(xv * xv, axis=-1, keepdims=True) + EPS)
        h = (xv * rs * g_ref[...]).astype(MXU_DTYPE)
        h_ref[...] = h
        z_ref[...] = jnp.dot(h, w_ref[...], preferred_element_type=F32)

    return pl.pallas_call(
        body, name="inproj_fwd", grid=(s // t,),
        in_specs=[pl.BlockSpec((t, D_MODEL), lambda i: (i, 0)),
                  pl.BlockSpec((1, D_MODEL), lambda i: (0, 0)),
                  pl.BlockSpec((D_MODEL, D_INP), lambda i: (0, 0))],
        out_specs=[pl.BlockSpec((t, D_MODEL), lambda i: (i, 0)),
                   pl.BlockSpec((t, D_INP), lambda i: (i, 0))],
        out_shape=[jax.ShapeDtypeStruct((s, D_MODEL), MXU_DTYPE), jax.ShapeDtypeStruct((s, D_INP), F32)],
        compiler_params=_cp(("parallel",), 40),
    )(x, g, w)


def _lru_gates(za, halo, cw_ref, cb_ref, wr_ref, br_ref, wi_ref, bi_ref, lam_ref):
    t = za.shape[0]
    ext = jnp.concatenate([halo, za], axis=0)
    sh = [za] + [_down(ext, j)[SUBLANES:SUBLANES + t] for j in (1, 2, 3)]
    xa = cb_ref[...] + cw_ref[3:4, :] * sh[0] + cw_ref[2:3, :] * sh[1] + cw_ref[1:2, :] * sh[2] + cw_ref[0:1, :] * sh[3]
    r = _sig(_dot(xa, wr_ref[...]) + br_ref[...])
    ig = _sig(_dot(xa, wi_ref[...]) + bi_ref[...])
    lam = lam_ref[...]
    sp = jnp.maximum(-lam, 0.0) + jnp.log(1.0 + jnp.exp(-jnp.abs(lam)))
    la = (-LRU_C) * r * sp
    a = jnp.exp(la)
    y2 = 2.0 * la
    taylor = -(y2 * (1.0 + y2 * (0.5 + y2 * (1.0 / 6.0 + y2 * (1.0 / 24.0)))))
    m2 = jnp.where(y2 > -0.02, taylor, 1.0 - jnp.exp(y2))
    m = jnp.sqrt(m2)
    return xa, sh, r, ig, sp, a, m


def _lru_fwd(z, cw, cb, wr, br, wi, bi, lam):
    s = z.shape[0]
    t = _tile(s)
    c = D_LRU

    def body(za_ref, ga_ref, cw_ref, cb_ref, wr_ref, br_ref, wi_ref, bi_ref, lam_ref, ya_ref, hs_ref, zprev, hcar):
        i = pl.program_id(0)

        @pl.when(i == 0)
        def _():
            zprev[...] = jnp.zeros_like(zprev)
            hcar[...] = jnp.zeros_like(hcar)

        za = za_ref[...]
        xa, _, _, ig, _, a, m = _lru_gates(za, zprev[...], cw_ref, cb_ref, wr_ref, br_ref, wi_ref, bi_ref, lam_ref)
        u = m * (ig * xa)
        row = _rows((t, c))
        acc_a, acc_h = a, u
        k = 1
        while k < t:
            a_sh = jnp.where(row >= k, _down(acc_a, k), 1.0)
            h_sh = jnp.where(row >= k, _down(acc_h, k), 0.0)
            acc_h = acc_h + acc_a * h_sh
            acc_a = acc_a * a_sh
            k *= 2
        hs = acc_h + acc_a * hcar[...]
        hs_ref[...] = hs
        ga = ga_ref[...]
        ya_ref[...] = (hs * (ga * _sig(ga))).astype(ya_ref.dtype)
        hcar[...] = jnp.sum(jnp.where(row == t - 1, hs, 0.0), axis=0, keepdims=True)
        zprev[...] = za_ref[t - SUBLANES:t, :]

    full = lambda shp: pl.BlockSpec(shp, lambda i: (0, 0))
    return pl.pallas_call(
        body, name="lru_fwd", grid=(s // t,),
        in_specs=[pl.BlockSpec((t, c), lambda i: (i, 0)), pl.BlockSpec((t, c), lambda i: (i, 1)),
                  full((CONV_WIDTH, c)), full((1, c)), full((c, c)), full((1, c)), full((c, c)), full((1, c)),
                  full((1, c))],
        out_specs=[pl.BlockSpec((t, c), lambda i: (i, 0)), pl.BlockSpec((t, c), lambda i: (i, 0))],
        out_shape=[jax.ShapeDtypeStruct((s, c), MXU_DTYPE), jax.ShapeDtypeStruct((s, c), F32)],
        scratch_shapes=[pltpu.VMEM((SUBLANES, c), F32), pltpu.VMEM((1, c), F32)],
        compiler_params=_cp(("arbitrary",), 40),
    )(z, z, cw, cb, wr, br, wi, bi, lam)


def _lru_bwd(z, hs, dy, cw, cb, wr, br, wi, bi, lam):
    s = z.shape[0]
    t = _tile(s)
    nt = s // t
    c = D_LRU
    hb = t // SUBLANES

    def body(za_ref, zh_ref, ga_ref, hs_ref, hh_ref, dy_ref, cw_ref, cb_ref, wr_ref, br_ref, wi_ref, bi_ref, lam_ref,
             dza_ref, dga_ref, dwr_ref, dwi_ref, dcw_ref, dcb_ref, dbr_ref, dbi_ref, dlam_ref, lcar, dxn):
        i = pl.program_id(0)
        tt = nt - 1 - i

        @pl.when(i == 0)
        def _():
            lcar[...] = jnp.zeros_like(lcar)
            dxn[...] = jnp.zeros_like(dxn)
            for ref in (dwr_ref, dwi_ref, dcw_ref, dcb_ref, dbr_ref, dbi_ref, dlam_ref):
                ref[...] = jnp.zeros_like(ref)

        first = (tt > 0).astype(F32)
        za = za_ref[...]
        xa, sh, r, ig, sp, a, m = _lru_gates(za, zh_ref[...] * first, cw_ref, cb_ref, wr_ref, br_ref, wi_ref, bi_ref,
                                             lam_ref)
        hs_v = hs_ref[...]
        hprev = _down(jnp.concatenate([hh_ref[...] * first, hs_v], axis=0), 1)[SUBLANES:SUBLANES + t]
        ga = ga_ref[...]
        sg = _sig(ga)
        silu = ga * sg
        dya = dy_ref[...]
        dga_ref[...] = (dya * hs_v * (sg * (1.0 + ga * (1.0 - sg)))).astype(dga_ref.dtype)
        row = _rows((t, c))
        acc_h = dya * silu + jnp.where(row == t - 1, lcar[...], 0.0)
        acc_b = jnp.where(row < t - 1, _up(a, 1), 0.0)
        k = 1
        while k < t:
            keep = row < t - k
            b_sh = jnp.where(keep, _up(acc_b, k), 0.0)
            h_sh = jnp.where(keep, _up(acc_h, k), 0.0)
            acc_h = acc_h + acc_b * h_sh
            acc_b = acc_b * b_sh
            k *= 2
        lmb = acc_h
        lcar[...] = jnp.sum(jnp.where(row == 0, a * lmb, 0.0), axis=0, keepdims=True)
        da = lmb * hprev
        dxa = lmb * m * ig
        di = lmb * m * xa
        dm = lmb * ig * xa
        dla = da * a - dm * (a * a) / m
        dr = dla * ((-LRU_C) * sp)
        lam = lam_ref[...]
        dsp = jnp.sum(dla * ((-LRU_C) * r), axis=0, keepdims=True)
        dlam_ref[...] += dsp * (-1.0 / (1.0 + jnp.exp(lam)))
        dpr = dr * r * (1.0 - r)
        dpi = di * ig * (1.0 - ig)
        dbr_ref[...] += jnp.sum(dpr, axis=0, keepdims=True)
        dbi_ref[...] += jnp.sum(dpi, axis=0, keepdims=True)
        dwr_ref[...] += _dot_tn(xa, dpr)
        dwi_ref[...] += _dot_tn(xa, dpi)
        dxa = dxa + _dot_nt(dpr, wr_ref[...]) + _dot_nt(dpi, wi_ref[...])
        dcb_ref[...] += jnp.sum(dxa, axis=0, keepdims=True)
        for k in range(CONV_WIDTH):
            dcw_ref[k:k + 1, :] += jnp.sum(dxa * sh[CONV_WIDTH - 1 - k], axis=0, keepdims=True)
        ext = jnp.concatenate([dxa, dxn[...]], axis=0)
        dza = cw_ref[3:4, :] * dxa
        for j in (1, 2, 3):
            dza = dza + cw_ref[3 - j:4 - j, :] * _up(ext, j)[:t]
        dza_ref[...] = dza.astype(dza_ref.dtype)
        dxn[...] = dxa[:SUBLANES]

    full = lambda shp: pl.BlockSpec(shp, lambda i: (0, 0))
    rev = lambda i: nt - 1 - i
    halo = lambda i: (jnp.maximum((nt - 1 - i) * hb - 1, 0), 0)
    outs = pl.pallas_call(
        body, name="lru_bwd", grid=(nt,),
        in_specs=[pl.BlockSpec((t, c), lambda i: (rev(i), 0)), pl.BlockSpec((SUBLANES, c), halo),
                  pl.BlockSpec((t, c), lambda i: (rev(i), 1)),
                  pl.BlockSpec((t, c), lambda i: (rev(i), 0)), pl.BlockSpec((SUBLANES, c), halo),
                  pl.BlockSpec((t, c), lambda i: (rev(i), 0)),
                  full((CONV_WIDTH, c)), full((1, c)), full((c, c)), full((1, c)), full((c, c)), full((1, c)),
                  full((1, c))],
        out_specs=[pl.BlockSpec((t, c), lambda i: (rev(i), 0)), pl.BlockSpec((t, c), lambda i: (rev(i), 0)),
                   full((c, c)), full((c, c)), full((CONV_WIDTH, c)), full((1, c)), full((1, c)), full((1, c)),
                   full((1, c))],
        out_shape=[jax.ShapeDtypeStruct((s, c), MXU_DTYPE), jax.ShapeDtypeStruct((s, c), MXU_DTYPE),
                   jax.ShapeDtypeStruct((c, c), F32), jax.ShapeDtypeStruct((c, c), F32),
                   jax.ShapeDtypeStruct((CONV_WIDTH, c), F32)] + [jax.ShapeDtypeStruct((1, c), F32)] * 4,
        scratch_shapes=[pltpu.VMEM((1, c), F32), pltpu.VMEM((SUBLANES, c), F32)],
        compiler_params=_cp(("arbitrary",), 48),
    )(z, z, z, hs, hs, dy, cw, cb, wr, br, wi, bi, lam)
    return outs


POOL_HALO = 16


def _pool_select(lane, v2, v4, v8, v16):
    return jnp.where(lane < 64, v2, jnp.where(lane < 128, v4, jnp.where(lane < 192, v8, v16)))


def _pool_counts(t0, t, c):
    lane = _lanes((t, c))
    win = _pool_select(lane, 2.0, 4.0, 8.0, 16.0)
    seen = (t0 + _rows((t, c)) + 1).astype(F32)
    return lane, jnp.minimum(seen, win)


def _pooled(zc, halo, lane, cnt):
    t = zc.shape[0]
    ext = jnp.concatenate([halo, zc], axis=0)
    s2 = ext + _down(ext, 1)
    s4 = s2 + _down(s2, 2)
    s8 = s4 + _down(s4, 4)
    s16 = s8 + _down(s8, 8)
    cut = lambda v: v[POOL_HALO:POOL_HALO + t]
    return _pool_select(lane, cut(s2), cut(s4), cut(s8), cut(s16)) / cnt - zc


def _pool_fwd(z, wp, ps):
    s = z.shape[0]
    t = _tile(s)
    c = D_POOL

    def body(zc_ref, gc_ref, wp_ref, ps_ref, yc_ref, zprev):
        i = pl.program_id(0)

        @pl.when(i == 0)
        def _():
            zprev[...] = jnp.zeros_like(zprev)

        zc = zc_ref[...]
        lane, cnt = _pool_counts(i * t, t, c)
        pooled = _pooled(zc, zprev[...], lane, cnt)
        pc = _dot(pooled, wp_ref[...])
        gc = gc_ref[...]
        yc_ref[...] = (pc * ps_ref[...] * (gc * _sig(gc))).astype(yc_ref.dtype)
        zprev[...] = zc_ref[t - POOL_HALO:t, :]

    full = lambda shp: pl.BlockSpec(shp, lambda i: (0, 0))
    return pl.pallas_call(
        body, name="pool_fwd", grid=(s // t,),
        in_specs=[pl.BlockSpec((t, c), lambda i: (i, 7)), pl.BlockSpec((t, c), lambda i: (i, 8)),
                  full((c, c)), full((1, c))],
        out_specs=pl.BlockSpec((t, c), lambda i: (i, 0)),
        out_shape=jax.ShapeDtypeStruct((s, c), MXU_DTYPE),
        scratch_shapes=[pltpu.VMEM((POOL_HALO, c), F32)],
        compiler_params=_cp(("arbitrary",), 32),
    )(z, z, wp, ps)


def _pool_bwd(z, dy, wp, ps):
    s = z.shape[0]
    t = _tile(s)
    nt = s // t
    c = D_POOL
    hb = t // POOL_HALO

    def body(zc_ref, zh_ref, gc_ref, dy_ref, wp_ref, ps_ref, dzc_ref, dgc_ref, dwp_ref, dps_ref, ddn):
        i = pl.program_id(0)
        tt = nt - 1 - i

        @pl.when(i == 0)
        def _():
            ddn[...] = jnp.zeros_like(ddn)
            dwp_ref[...] = jnp.zeros_like(dwp_ref)
            dps_ref[...] = jnp.zeros_like(dps_ref)

        first = (tt > 0).astype(F32)
        zc = zc_ref[...]
        lane, cnt = _pool_counts(tt * t, t, c)
        pooled = _pooled(zc, zh_ref[...] * first, lane, cnt)
        pc = _dot(pooled, wp_ref[...])
        gc = gc_ref[...]
        sg = _sig(gc)
        silu = gc * sg
        dyc = dy_ref[...]
        ps_v = ps_ref[...]
        dgc_ref[...] = (dyc * pc * ps_v * (sg * (1.0 + gc * (1.0 - sg)))).astype(dgc_ref.dtype)
        dps_ref[...] += jnp.sum(dyc * pc * silu, axis=0, keepdims=True)
        dpc = dyc * ps_v * silu
        dwp_ref[...] += _dot_tn(pooled, dpc)
        dpooled = _dot_nt(dpc, wp_ref[...])
        dd = dpooled / cnt
        ext = jnp.concatenate([dd, ddn[...]], axis=0)
        f2 = ext + _up(ext, 1)
        f4 = f2 + _up(f2, 2)
        f8 = f4 + _up(f4, 4)
        f16 = f8 + _up(f8, 8)
        dzc = _pool_select(lane, f2[:t], f4[:t], f8[:t], f16[:t]) - dpooled
        dzc_ref[...] = dzc.astype(dzc_ref.dtype)
        ddn[...] = dd[:POOL_HALO]

    full = lambda shp: pl.BlockSpec(shp, lambda i: (0, 0))
    rev = lambda i: nt - 1 - i
    return pl.pallas_call(
        body, name="pool_bwd", grid=(nt,),
        in_specs=[pl.BlockSpec((t, c), lambda i: (rev(i), 7)),
                  pl.BlockSpec((POOL_HALO, c), lambda i: (jnp.maximum(rev(i) * hb - 1, 0), 7)),
                  pl.BlockSpec((t, c), lambda i: (rev(i), 8)),
                  pl.BlockSpec((t, c), lambda i: (rev(i), 3)),
                  full((c, c)), full((1, c))],
        out_specs=[pl.BlockSpec((t, c), lambda i: (rev(i), 0)), pl.BlockSpec((t, c), lambda i: (rev(i), 0)),
                   full((c, c)), full((1, c))],
        out_shape=[jax.ShapeDtypeStruct((s, c), MXU_DTYPE), jax.ShapeDtypeStruct((s, c), MXU_DTYPE),
                   jax.ShapeDtypeStruct((c, c), F32), jax.ShapeDtypeStruct((1, c), F32)],
        scratch_shapes=[pltpu.VMEM((POOL_HALO, c), F32)],
        compiler_params=_cp(("arbitrary",), 32),
    )(z, z, z, dy, wp, ps)


def _rope_tables(s):
    pos = jnp.arange(s, dtype=F32)
    inv_freq = ROPE_BASE ** (-jnp.arange(0, QK_ROPE, 2, dtype=F32) / QK_ROPE)
    ang = pos[:, None] * inv_freq[None, :]
    cos, sin = jnp.cos(ang), jnp.sin(ang)
    half = QK_ROPE // 2
    ones = jnp.ones((s, QK_NOPE), F32)
    z64 = jnp.zeros((s, QK_NOPE), F32)
    zh = jnp.zeros((s, half), F32)
    z32 = jnp.zeros((s, HEAD_BLK - QK_NOPE - QK_ROPE), F32)
    c_t = jnp.concatenate([ones, cos, cos, z32], axis=1)
    s1_t = jnp.concatenate([z64, -sin, zh, z32], axis=1)
    s2_t = jnp.concatenate([z64, zh, sin, z32], axis=1)
    return c_t, s1_t, s2_t


def _rope(x, c_t, s1_t, s2_t):
    return x * c_t + pltpu.roll(x, HEAD_BLK - 16, 1) * s1_t + pltpu.roll(x, 16, 1) * s2_t


def _unrope(d, c_t, s1_t, s2_t):
    return d * c_t + pltpu.roll(d * s1_t, 16, 1) + pltpu.roll(d * s2_t, HEAD_BLK - 16, 1)


def _mla_prep_fwd(z, gq, gkv, wuq, wukv, c_t, s1_t, s2_t):
    s = z.shape[0]
    t = _tile(s)
    hq = MLA_HEADS * HEAD_BLK

    def body(cq_ref, ckv_ref, kr_ref, gq_ref, gkv_ref, wuq_ref, wukv_ref, c_ref, s1_ref, s2_ref,
             q_ref, k_ref, v_ref, qn_ref, kvn_ref):
        ct, s1, s2 = c_ref[...], s1_ref[...], s2_ref[...]
        cq = cq_ref[...]
        qn = (cq * lax.rsqrt(jnp.mean(cq * cq, axis=-1, keepdims=True) + EPS) * gq_ref[...]).astype(MXU_DTYPE)
        qn_ref[...] = qn
        q = jnp.dot(qn, wuq_ref[...], preferred_element_type=F32)
        ckv = ckv_ref[...]
        kvn = (ckv * lax.rsqrt(jnp.mean(ckv * ckv, axis=-1, keepdims=True) + EPS) * gkv_ref[...]).astype(MXU_DTYPE)
        kvn_ref[...] = kvn
        kvp = jnp.dot(kvn, wukv_ref[...], preferred_element_type=F32)
        krr = _rope(kr_ref[...], ct, s1, s2)
        for h in range(MLA_HEADS):
            blk = slice(h * HEAD_BLK, (h + 1) * HEAD_BLK)
            q_ref[:, blk] = (_rope(q[:, blk], ct, s1, s2) * Q_PRESCALE).astype(q_ref.dtype)
            k_ref[:, blk] = (kvp[:, blk] + krr).astype(k_ref.dtype)
        v_ref[...] = kvp[:, hq:].astype(v_ref.dtype)

    full = lambda shp: pl.BlockSpec(shp, lambda i: (0, 0))
    tab = pl.BlockSpec((t, HEAD_BLK), lambda i: (i, 0))
    return pl.pallas_call(
        body, name="mla_prep_fwd", grid=(s // t,),
        in_specs=[pl.BlockSpec((t, Q_RANK), lambda i: (i, 2)), pl.BlockSpec((t, KV_RANK), lambda i: (i, 6)),
                  pl.BlockSpec((t, HEAD_BLK), lambda i: (i, 18)),
                  full((1, Q_RANK)), full((1, KV_RANK)), full((Q_RANK, hq)), full((KV_RANK, hq + D_MLA)),
                  tab, tab, tab],
        out_specs=[pl.BlockSpec((t, hq), lambda i: (i, 0)), pl.BlockSpec((t, hq), lambda i: (i, 0)),
                   pl.BlockSpec((t, D_MLA), lambda i: (i, 0)), pl.BlockSpec((t, Q_RANK), lambda i: (i, 0)),
                   pl.BlockSpec((t, KV_RANK), lambda i: (i, 0))],
        out_shape=[jax.ShapeDtypeStruct((s, hq), MXU_DTYPE), jax.ShapeDtypeStruct((s, hq), MXU_DTYPE),
                   jax.ShapeDtypeStruct((s, D_MLA), MXU_DTYPE), jax.ShapeDtypeStruct((s, Q_RANK), MXU_DTYPE),
                   jax.ShapeDtypeStruct((s, KV_RANK), MXU_DTYPE)],
        compiler_params=_cp(("parallel",), 40),
    )(z, z, z, gq, gkv, wuq, wukv, c_t, s1_t, s2_t)


SUM_LANE_A = V_DIM
SUM_LANE_B = 0


def _flash_fwd(q, k, v, z):
    s = q.shape[0]
    t = _tile(s)
    nq = s // t
    pw = 2 * HEAD_BLK

    def body(q_ref, k_ref, v_ref, gb_ref, o_ref, yb_ref, lse_ref):
        i = pl.program_id(1)
        qv = q_ref[...]
        qa, qb = qv[:, :HEAD_BLK], qv[:, HEAD_BLK:]
        lane = _lanes((t, HEAD_BLK))
        lo = lane < V_DIM

        def update(qh, kh, vh, m, acc, masked):
            sc = _dot_nt(qh, kh)
            if masked:
                sc = jnp.where(_lanes((t, t)) <= _rows((t, t)), sc, -1e30)
            m_new = jnp.maximum(m, jnp.max(sc, axis=-1, keepdims=True))
            p = jnp.exp2(sc - m_new).astype(MXU_DTYPE)
            return m_new, acc * jnp.exp2(m - m_new) + _dot(p, vh)

        def step(j, carry, masked):
            ma, mb, acc_a, acc_b = carry
            kv_rows = pl.ds(pl.multiple_of(j * t, t), t)
            kt = k_ref[kv_rows, :]
            vt = v_ref[kv_rows, :]
            lane_v = _lanes(vt.shape)
            one = jnp.ones_like(vt)
            zero_v = jnp.zeros_like(vt)
            v_a = jnp.where(lane_v < V_DIM, vt, jnp.where(lane_v == SUM_LANE_A, one, zero_v))
            v_b = jnp.where(lane_v >= V_DIM, vt, jnp.where(lane_v == SUM_LANE_B, one, zero_v))
            ma, acc_a = update(qa, kt[:, :HEAD_BLK], v_a, ma, acc_a, masked)
            mb, acc_b = update(qb, kt[:, HEAD_BLK:], v_b, mb, acc_b, masked)
            return ma, mb, acc_a, acc_b

        neg = jnp.full((t, 1), -1e30, F32)
        zero = jnp.zeros((t, HEAD_BLK), F32)
        carry = lax.fori_loop(0, i, lambda j, cr: step(j, cr, False), (neg, neg, zero, zero))
        ma, mb, acc_a, acc_b = step(i, carry, True)
        la = jnp.sum(jnp.where(lane == SUM_LANE_A, acc_a, 0.0), axis=-1, keepdims=True)
        lb = jnp.sum(jnp.where(lane == SUM_LANE_B, acc_b, 0.0), axis=-1, keepdims=True)
        o = jnp.where(lo, acc_a * (1.0 / la), acc_b * (1.0 / lb))
        o_ref[...] = o
        gb = gb_ref[...]
        yb_ref[...] = (o * (gb * _sig(gb))).astype(yb_ref.dtype)
        lse = jnp.where(lo, ma + jnp.log(la) * LOG2_E, mb + jnp.log(lb) * LOG2_E)
        pick = ((_rows((SUBLANES, HEAD_BLK)) == 0) & (_lanes((SUBLANES, HEAD_BLK)) == 0)) | (
            (_rows((SUBLANES, HEAD_BLK)) == 1) & (_lanes((SUBLANES, HEAD_BLK)) == V_DIM))
        lse_ref[0, 0] = lax.dot_general(pick.astype(F32), lse, (((1,), (1,)), ((), ())),
                                        precision=lax.Precision.HIGHEST, preferred_element_type=F32)

    return pl.pallas_call(
        body, name="flash_fwd", grid=(N_PAIR, nq),
        in_specs=[pl.BlockSpec((t, pw), lambda p, i: (i, p)), pl.BlockSpec((s, pw), lambda p, i: (0, p)),
                  pl.BlockSpec((s, HEAD_BLK), lambda p, i: (0, p)),
                  pl.BlockSpec((t, HEAD_BLK), lambda p, i: (i, 9 + p))],
        out_specs=[pl.BlockSpec((t, HEAD_BLK), lambda p, i: (i, p)), pl.BlockSpec((t, HEAD_BLK), lambda p, i: (i, p)),
                   pl.BlockSpec((1, 1, SUBLANES, t), lambda p, i: (p, i, 0, 0))],
        out_shape=[jax.ShapeDtypeStruct((s, D_MLA), F32), jax.ShapeDtypeStruct((s, D_MLA), MXU_DTYPE),
                   jax.ShapeDtypeStruct((N_PAIR, nq, SUBLANES, t), F32)],
        compiler_params=_cp(("parallel", "parallel"), 48),
    )(q, k, v, z)


def _attn_bwd_prep(dy, o, z):
    s = o.shape[0]
    t = _tile(s)
    nq = s // t
    rows = N_PAIR * SUBLANES

    def body(dy_ref, o_ref, gb_ref, do_ref, dgb_ref, dl_ref):
        gb = gb_ref[...]
        sg = _sig(gb)
        dyb = dy_ref[...]
        ov = o_ref[...]
        do = dyb * (gb * sg)
        do_ref[...] = do.astype(do_ref.dtype)
        dgb_ref[...] = (dyb * ov * (sg * (1.0 + gb * (1.0 - sg)))).astype(dgb_ref.dtype)
        r = _rows((rows, D_MLA))
        head = (r // SUBLANES) * 2 + (r % SUBLANES)
        sel = ((r % SUBLANES) < 2) & (_lanes((rows, D_MLA)) // V_DIM == head)
        dl = lax.dot_general(sel.astype(F32), do * ov, (((1,), (1,)), ((), ())),
                             precision=lax.Precision.HIGHEST, preferred_element_type=F32)
        for p in range(N_PAIR):
            dl_ref[p, 0] = dl[p * SUBLANES:(p + 1) * SUBLANES]

    return pl.pallas_call(
        body, name="attn_bwd_prep", grid=(nq,),
        in_specs=[pl.BlockSpec((t, D_MLA), lambda i: (i, 1)), pl.BlockSpec((t, D_MLA), lambda i: (i, 0)),
                  pl.BlockSpec((t, D_MLA), lambda i: (i, 3))],
        out_specs=[pl.BlockSpec((t, D_MLA), lambda i: (i, 0)), pl.BlockSpec((t, D_MLA), lambda i: (i, 0)),
                   pl.BlockSpec((N_PAIR, 1, SUBLANES, t), lambda i: (0, i, 0, 0))],
        out_shape=[jax.ShapeDtypeStruct((s, D_MLA), MXU_DTYPE), jax.ShapeDtypeStruct((s, D_MLA), MXU_DTYPE),
                   jax.ShapeDtypeStruct((N_PAIR, nq, SUBLANES, t), F32)],
        compiler_params=_cp(("parallel",), 32),
    )(dy, o, z)


def _flash_bwd(q, k, v, do, lse, delta):
    s = q.shape[0]
    t = _tile(s)
    nq = s // t
    pw = 2 * HEAD_BLK

    def body(q_ref, do_ref, lse_ref, dl_ref, k_ref, v_ref, dq_ref, dk_ref, dv_ref):
        j = pl.program_id(1)

        @pl.when(j == 0)
        def _():
            dq_ref[...] = jnp.zeros_like(dq_ref)

        kt = k_ref[...]
        ka, kb = kt[:, :HEAD_BLK], kt[:, HEAD_BLK:]
        vt = v_ref[...]

        def head(kh, qh, do_h, lse_row, dl_row, masked):
            st = _dot_nt(kh, qh)
            if masked:
                st = jnp.where(_rows((t, t)) <= _lanes((t, t)), st, -1e30)
            pt = jnp.exp2(st - lse_row)
            dv_h = _dot(pt, do_h)
            dst = (pt * (_dot_nt(vt, do_h) - dl_row)).astype(MXU_DTYPE)
            return dv_h, _dot(dst, qh), _dot_tn(dst, kh)

        def step(i, carry, masked):
            dka, dkb, dv = carry
            q_rows = pl.ds(pl.multiple_of(i * t, t), t)
            qv = q_ref[q_rows, :]
            dov = do_ref[q_rows, :]
            lane = _lanes(dov.shape)
            do_lo = jnp.where(lane < V_DIM, dov, jnp.zeros_like(dov))
            do_hi = jnp.where(lane >= V_DIM, dov, jnp.zeros_like(dov))
            dva, dk_a, dq_a = head(ka, qv[:, :HEAD_BLK], do_lo, lse_ref[0, i, 0:1, :], dl_ref[0, i, 0:1, :], masked)
            dvb, dk_b, dq_b = head(kb, qv[:, HEAD_BLK:], do_hi, lse_ref[0, i, 1:2, :], dl_ref[0, i, 1:2, :], masked)
            dq_ref[q_rows, 0:HEAD_BLK] += dq_a
            dq_ref[q_rows, HEAD_BLK:pw] += dq_b
            return dka + dk_a, dkb + dk_b, dv + dva + dvb

        zero = jnp.zeros((t, HEAD_BLK), F32)
        carry = step(j, (zero, zero, zero), True)
        dka, dkb, dv = lax.fori_loop(j + 1, nq, lambda i, cr: step(i, cr, False), carry)
        dk_ref[:, 0:HEAD_BLK] = dka * LN_2
        dk_ref[:, HEAD_BLK:pw] = dkb * LN_2
        dv_ref[...] = dv.astype(dv_ref.dtype)

    return pl.pallas_call(
        body, name="flash_bwd", grid=(N_PAIR, nq),
        in_specs=[pl.BlockSpec((s, pw), lambda p, j: (0, p)), pl.BlockSpec((s, HEAD_BLK), lambda p, j: (0, p)),
                  pl.BlockSpec((1, nq, SUBLANES, t), lambda p, j: (p, 0, 0, 0)),
                  pl.BlockSpec((1, nq, SUBLANES, t), lambda p, j: (p, 0, 0, 0)),
                  pl.BlockSpec((t, pw), lambda p, j: (j, p)), pl.BlockSpec((t, HEAD_BLK), lambda p, j: (j, p))],
        out_specs=[pl.BlockSpec((s, pw), lambda p, j: (0, p)), pl.BlockSpec((t, pw), lambda p, j: (j, p)),
                   pl.BlockSpec((t, HEAD_BLK), lambda p, j: (j, p))],
        out_shape=[jax.ShapeDtypeStruct((s, MLA_HEADS * HEAD_BLK), F32),
                   jax.ShapeDtypeStruct((s, MLA_HEADS * HEAD_BLK), F32),
                   jax.ShapeDtypeStruct((s, D_MLA), MXU_DTYPE)],
        compiler_params=_cp(("parallel", "arbitrary"), 56),
    )(q, do, lse, delta, k, v)


def _mla_prep_bwd(dq, dk, dv, z, gq, gkv, wuq, wukv, c_t, s1_t, s2_t):
    s = z.shape[0]
    t = _tile(s)
    hq = MLA_HEADS * HEAD_BLK

    def body(dq_ref, dk_ref, dv_ref, cq_ref, ckv_ref, gq_ref, gkv_ref, wuq_ref, wukv_ref, c_ref, s1_ref, s2_ref,
             dcq_ref, dckv_ref, dkr_ref, dqu_ref, dkvp_ref, dgq_ref, dgkv_ref):
        @pl.when(pl.program_id(0) == 0)
        def _():
            dgq_ref[...] = jnp.zeros_like(dgq_ref)
            dgkv_ref[...] = jnp.zeros_like(dgkv_ref)

        ct, s1, s2 = c_ref[...], s1_ref[...], s2_ref[...]
        dk_sum = jnp.zeros((t, HEAD_BLK), F32)
        for h in range(MLA_HEADS):
            blk = slice(h * HEAD_BLK, (h + 1) * HEAD_BLK)
            dqu_ref[:, blk] = _unrope(dq_ref[:, blk] * ATT_SCALE, ct, s1, s2).astype(dqu_ref.dtype)
            dkh = dk_ref[:, blk]
            dk_sum = dk_sum + dkh
            dkvp_ref[:, blk] = dkh.astype(dkvp_ref.dtype)
        dkvp_ref[:, hq:] = dv_ref[...]
        lane = _lanes((t, HEAD_BLK))
        rope_lanes = (lane >= KR_LANE0) & (lane < KR_LANE0 + QK_ROPE)
        dkr_ref[...] = _unrope(jnp.where(rope_lanes, dk_sum, 0.0), ct, s1, s2).astype(dkr_ref.dtype)

        def norm_bwd(c_in, g, dn_out, dc_ref, dg_ref):
            rs = lax.rsqrt(jnp.mean(c_in * c_in, axis=-1, keepdims=True) + EPS)
            n = c_in * rs
            dg_ref[...] += jnp.sum(dn_out * n, axis=0, keepdims=True)
            dn = dn_out * g
            dc_ref[...] = (rs * (dn - n * jnp.mean(dn * n, axis=-1, keepdims=True))).astype(dc_ref.dtype)

        norm_bwd(cq_ref[...], gq_ref[...], _dot_nt(dqu_ref[...], wuq_ref[...]), dcq_ref, dgq_ref)
        norm_bwd(ckv_ref[...], gkv_ref[...], _dot_nt(dkvp_ref[...], wukv_ref[...]), dckv_ref, dgkv_ref)

    full = lambda shp: pl.BlockSpec(shp, lambda i: (0, 0))
    tab = pl.BlockSpec((t, HEAD_BLK), lambda i: (i, 0))
    row = lambda w: pl.BlockSpec((t, w), lambda i: (i, 0))
    return pl.pallas_call(
        body, name="mla_prep_bwd", grid=(s // t,),
        in_specs=[row(hq), row(hq), row(D_MLA),
                  pl.BlockSpec((t, Q_RANK), lambda i: (i, 2)), pl.BlockSpec((t, KV_RANK), lambda i: (i, 6)),
                  full((1, Q_RANK)), full((1, KV_RANK)), full((Q_RANK, hq)), full((KV_RANK, hq + D_MLA)),
                  tab, tab, tab],
        out_specs=[row(Q_RANK), row(KV_RANK), row(HEAD_BLK), row(hq), row(hq + D_MLA),
                   full((1, Q_RANK)), full((1, KV_RANK))],
        out_shape=[jax.ShapeDtypeStruct((s, Q_RANK), MXU_DTYPE), jax.ShapeDtypeStruct((s, KV_RANK), MXU_DTYPE),
                   jax.ShapeDtypeStruct((s, HEAD_BLK), MXU_DTYPE), jax.ShapeDtypeStruct((s, hq), MXU_DTYPE),
                   jax.ShapeDtypeStruct((s, hq + D_MLA), MXU_DTYPE),
                   jax.ShapeDtypeStruct((1, Q_RANK), F32), jax.ShapeDtypeStruct((1, KV_RANK), F32)],
        compiler_params=_cp(("arbitrary",), 48),
    )(dq, dk, dv, z, z, gq, gkv, wuq, wukv, c_t, s1_t, s2_t)


def _outproj_fwd(x, ya, yb, yc, w):
    s = x.shape[0]
    t = _tile(s)

    def body(x_ref, ya_ref, yb_ref, yc_ref, wa_ref, wb_ref, wc_ref, o_ref):
        o_ref[...] = (x_ref[...] + _dot(ya_ref[...], wa_ref[...]) + _dot(yb_ref[...], wb_ref[...])
                      + _dot(yc_ref[...], wc_ref[...]))

    row = lambda w_: pl.BlockSpec((t, w_), lambda i: (i, 0))
    return pl.pallas_call(
        body, name="outproj_fwd", grid=(s // t,),
        in_specs=[row(D_MODEL), row(D_LRU), row(D_MLA), row(D_POOL),
                  pl.BlockSpec((D_LRU, D_MODEL), lambda i: (0, 0)), pl.BlockSpec((D_MLA, D_MODEL), lambda i: (1, 0)),
                  pl.BlockSpec((D_POOL, D_MODEL), lambda i: (3, 0))],
        out_specs=row(D_MODEL),
        out_shape=jax.ShapeDtypeStruct((s, D_MODEL), F32),
        compiler_params=_cp(("parallel",), 40),
    )(x, ya, yb, yc, w, w, w)


def _mm_nt(a, b, name):
    s, kd = a.shape
    n = b.shape[0]
    t = _tile(s)

    def body(a_ref, b_ref, o_ref):
        o_ref[...] = _dot_nt(a_ref[...], b_ref[...])

    return pl.pallas_call(
        body, name=name, grid=(s // t,),
        in_specs=[pl.BlockSpec((t, kd), lambda i: (i, 0)), pl.BlockSpec((n, kd), lambda i: (0, 0))],
        out_specs=pl.BlockSpec((t, n), lambda i: (i, 0)),
        out_shape=jax.ShapeDtypeStruct((s, n), F32),
        compiler_params=_cp(("parallel",), 40),
    )(a, b)


def _mm_tn(a, b, name):
    s, k1 = a.shape
    n = b.shape[1]
    t = _tile(s)

    def body(a_ref, b_ref, o_ref):
        @pl.when(pl.program_id(0) == 0)
        def _():
            o_ref[...] = jnp.zeros_like(o_ref)

        o_ref[...] += _dot_tn(a_ref[...], b_ref[...])

    return pl.pallas_call(
        body, name=name, grid=(s // t,),
        in_specs=[pl.BlockSpec((t, k1), lambda i: (i, 0)), pl.BlockSpec((t, n), lambda i: (i, 0))],
        out_specs=pl.BlockSpec((k1, n), lambda i: (0, 0)),
        out_shape=jax.ShapeDtypeStruct((k1, n), F32),
        compiler_params=_cp(("arbitrary",), 56),
    )(a, b)


def _inproj_bwd(dz, w, x, g, dxn):
    s = x.shape[0]
    t = _tile(s)

    def body(dz_ref, w_ref, x_ref, g_ref, dxn_ref, dx_ref, dg_ref):
        @pl.when(pl.program_id(0) == 0)
        def _():
            dg_ref[...] = jnp.zeros_like(dg_ref)

        dh = _dot_nt(dz_ref[...], w_ref[...])
        xv = x_ref[...]
        rs = lax.rsqrt(jnp.mean(xv * xv, axis=-1, keepdims=True) + EPS)
        n = xv * rs
        dg_ref[...] += jnp.sum(dh * n, axis=0, keepdims=True)
        dn = dh * g_ref[...]
        dx_ref[...] = dxn_ref[...] + rs * (dn - n * jnp.mean(dn * n, axis=-1, keepdims=True))

    row = lambda w_: pl.BlockSpec((t, w_), lambda i: (i, 0))
    return pl.pallas_call(
        body, name="inproj_bwd", grid=(s // t,),
        in_specs=[row(D_INP), pl.BlockSpec((D_MODEL, D_INP), lambda i: (0, 0)), row(D_MODEL),
                  pl.BlockSpec((1, D_MODEL), lambda i: (0, 0)), row(D_MODEL)],
        out_specs=[row(D_MODEL), pl.BlockSpec((1, D_MODEL), lambda i: (0, 0))],
        out_shape=[jax.ShapeDtypeStruct((s, D_MODEL), F32), jax.ShapeDtypeStruct((1, D_MODEL), F32)],
        compiler_params=_cp(("arbitrary",), 48),
    )(dz, w, x, g, dxn)


def _loss_head(x, g, tgt):
    s = x.shape[0]
    t = _tile(s)

    def body(x_ref, g_ref, t_ref, dx_ref, loss_ref, dg_ref):
        @pl.when(pl.program_id(0) == 0)
        def _():
            loss_ref[...] = jnp.zeros_like(loss_ref)
            dg_ref[...] = jnp.zeros_like(dg_ref)

        xv = x_ref[...]
        rs = lax.rsqrt(jnp.mean(xv * xv, axis=-1, keepdims=True) + EPS)
        n = xv * rs
        gv = g_ref[...]
        e = n * gv - t_ref[...]
        loss_ref[...] += 0.5 * jnp.sum(jnp.mean(e * e, axis=-1, keepdims=True))
        dyf = e * (1.0 / D_MODEL)
        dg_ref[...] += jnp.sum(dyf * n, axis=0, keepdims=True)
        dn = dyf * gv
        dx_ref[...] = rs * (dn - n * jnp.mean(dn * n, axis=-1, keepdims=True))

    row = pl.BlockSpec((t, D_MODEL), lambda i: (i, 0))
    vec = pl.BlockSpec((1, D_MODEL), lambda i: (0, 0))
    return pl.pallas_call(
        body, name="loss_head", grid=(s // t,),
        in_specs=[row, vec, row],
        out_specs=[row, pl.BlockSpec((1, LANES), lambda i: (0, 0)), vec],
        out_shape=[jax.ShapeDtypeStruct((s, D_MODEL), F32), jax.ShapeDtypeStruct((1, LANES), F32),
                   jax.ShapeDtypeStruct((1, D_MODEL), F32)],
        compiler_params=_cp(("arbitrary",), 32),
    )(x, g, tgt)


def _block_diag(w):
    h, d, _ = w.shape
    return jnp.einsum('hij,hk->hikj', w, jnp.eye(h, dtype=w.dtype)).reshape(h * d, h * d)


def _diag_blocks(wfull, h):
    d = wfull.shape[0] // h
    return jnp.stack([wfull[i * d:(i + 1) * d, i * d:(i + 1) * d] for i in range(h)])


REF_TO_PERM = np.concatenate([np.arange(0, 1152), np.arange(1536, 1792),
                              np.arange(2304 + KR_LANE0, 2304 + KR_LANE0 + QK_ROPE),
                              np.arange(1152, 1536), np.arange(1792, 2304)])
N_SHARD = 4
W_IN_SHARD = D_IN // N_SHARD


def _w_in_runs():
    ref_of_perm = -np.ones(D_INP, np.int64)
    ref_of_perm[REF_TO_PERM] = np.arange(D_IN)
    perm_runs, p = [], 0
    while p < D_INP:
        r, q = ref_of_perm[p], p + 1
        if r < 0:
            while q < D_INP and ref_of_perm[q] < 0:
                q += 1
            perm_runs.append((None, q - p, 0))
        else:
            while (q < D_INP and ref_of_perm[q] == ref_of_perm[q - 1] + 1
                   and ref_of_perm[q] // W_IN_SHARD == r // W_IN_SHARD):
                q += 1
            perm_runs.append((int(r // W_IN_SHARD), int(r % W_IN_SHARD), int(r % W_IN_SHARD + q - p)))
        p = q
    shard_runs = []
    for s in range(N_SHARD):
        cols = REF_TO_PERM[s * W_IN_SHARD:(s + 1) * W_IN_SHARD]
        runs, a = [], 0
        for b in range(1, W_IN_SHARD + 1):
            if b == W_IN_SHARD or cols[b] != cols[b - 1] + 1:
                runs.append((int(cols[a]), int(cols[b - 1]) + 1))
                a = b
        shard_runs.append(runs)
    return perm_runs, shard_runs


def _permute_w_in(shards):
    perm_runs, _ = _w_in_runs()
    lead = shards[0].shape[:-1]
    parts = [jnp.zeros(lead + (a,), shards[0].dtype) if s is None else shards[s][..., a:b] for s, a, b in perm_runs]
    return jnp.concatenate(parts, axis=-1)


def _w_in_shard(wp, s):
    _, shard_runs = _w_in_runs()
    return jnp.concatenate([wp[..., a:b] for a, b in shard_runs[s]], axis=-1)


def _pad_w_uq(w):
    w3 = w.reshape(Q_RANK, MLA_HEADS, QK_NOPE + QK_ROPE)
    return jnp.pad(w3, ((0, 0), (0, 0), (0, HEAD_BLK - QK_NOPE - QK_ROPE))).reshape(Q_RANK, MLA_HEADS * HEAD_BLK)


def _unpad_w_uq(w):
    return w.reshape(Q_RANK, MLA_HEADS, HEAD_BLK)[:, :, :QK_NOPE + QK_ROPE].reshape(Q_RANK, -1)


def _pad_w_ukv(w):
    w3 = w.reshape(KV_RANK, MLA_HEADS, QK_NOPE + V_DIM)
    kpart = jnp.pad(w3[:, :, :QK_NOPE], ((0, 0), (0, 0), (0, HEAD_BLK - QK_NOPE))).reshape(KV_RANK, -1)
    return jnp.concatenate([kpart, w3[:, :, QK_NOPE:].reshape(KV_RANK, -1)], axis=1)


def _unpad_w_ukv(w):
    hq = MLA_HEADS * HEAD_BLK
    kpart = w[:, :hq].reshape(KV_RANK, MLA_HEADS, HEAD_BLK)[:, :, :QK_NOPE]
    vpart = w[:, hq:].reshape(KV_RANK, MLA_HEADS, V_DIM)
    return jnp.concatenate([kpart, vpart], axis=2).reshape(KV_RANK, -1)


def _local_step(x, tgt, w):
    s = x.shape[0]
    tabs = _rope_tables(s)
    lw = []
    for l in range(DEPTH):
        lw.append(dict(
            g=w['norm_g'][l][None], w_in=w['w_in'][l].astype(MXU_DTYPE),
            cw=w['conv_w'][l], cb=w['conv_b'][l][None],
            wr=_block_diag(w['w_rg'][l]).astype(MXU_DTYPE), br=w['b_rg'][l][None],
            wi=_block_diag(w['w_ig'][l]).astype(MXU_DTYPE), bi=w['b_ig'][l][None],
            lam=w['lru_lambda'][l][None], gq=w['q_norm_g'][l][None], gkv=w['kv_norm_g'][l][None],
            wuq=_pad_w_uq(w['w_uq'][l]).astype(MXU_DTYPE), wukv=_pad_w_ukv(w['w_ukv'][l]).astype(MXU_DTYPE),
            wp=_block_diag(w['w_pool'][l]).astype(MXU_DTYPE), ps=w['pool_scale'][l][None],
            wout=w['w_out'][l].astype(MXU_DTYPE)))

    saved = []
    for l in range(DEPTH):
        p = lw[l]
        h, z = _inproj_fwd(x, p['g'], p['w_in'])
        ya, hs = _lru_fwd(z, p['cw'], p['cb'], p['wr'], p['br'], p['wi'], p['bi'], p['lam'])
        yc = _pool_fwd(z, p['wp'], p['ps'])
        q, k, v, qn, kvn = _mla_prep_fwd(z, p['gq'], p['gkv'], p['wuq'], p['wukv'], *tabs)
        o, yb, lse = _flash_fwd(q, k, v, z)
        saved.append(dict(x=x, h=h, z=z, hs=hs, ya=ya, yb=yb, yc=yc, q=q, k=k, v=v, qn=qn, kvn=kvn, o=o, lse=lse))
        x = _outproj_fwd(x, ya, yb, yc, p['wout'])

    dx, loss, dgf = _loss_head(x, w['final_norm_g'][None], tgt)
    grads = {n: [None] * DEPTH for n in WEIGHT_NAMES if n != 'final_norm_g'}
    grads['final_norm_g'] = dgf[0]
    for l in reversed(range(DEPTH)):
        p, sv = lw[l], saved[l]
        dy = _mm_nt(dx, p['wout'], "outproj_bwd_dy")
        grads['w_out'][l] = jnp.concatenate([_mm_tn(sv['ya'], dx, "dwout_a"), _mm_tn(sv['yb'], dx, "dwout_b"),
                                             _mm_tn(sv['yc'], dx, "dwout_c")], axis=0)
        (dza, dga, dwr, dwi, dcw, dcb, dbr, dbi, dlam) = _lru_bwd(
            sv['z'], sv['hs'], dy, p['cw'], p['cb'], p['wr'], p['br'], p['wi'], p['bi'], p['lam'])
        dzc, dgc, dwp, dps = _pool_bwd(sv['z'], dy, p['wp'], p['ps'])
        do, dgb, delta = _attn_bwd_prep(dy, sv['o'], sv['z'])
        dq, dk, dv = _flash_bwd(sv['q'], sv['k'], sv['v'], do, sv['lse'], delta)
        dcq, dckv, dkr, dqu, dkvp, dgq, dgkv = _mla_prep_bwd(dq, dk, dv, sv['z'], p['gq'], p['gkv'], p['wuq'],
                                                             p['wukv'], *tabs)
        dz = jnp.concatenate([dza, dga, dcq, dgb, dckv, dzc, dgc, dkr], axis=1)
        grads['w_in'][l] = _mm_tn(sv['h'], dz, "dwin")
        grads['w_uq'][l] = _unpad_w_uq(_mm_tn(sv['qn'], dqu, "dwuq"))
        grads['w_ukv'][l] = _unpad_w_ukv(_mm_tn(sv['kvn'], dkvp, "dwukv"))
        dx, dg = _inproj_bwd(dz, p['w_in'], sv['x'], p['g'], dx)
        grads['norm_g'][l] = dg[0]
        grads['conv_w'][l] = dcw
        grads['conv_b'][l] = dcb[0]
        grads['w_rg'][l] = _diag_blocks(dwr, LRU_HEADS)
        grads['b_rg'][l] = dbr[0]
        grads['w_ig'][l] = _diag_blocks(dwi, LRU_HEADS)
        grads['b_ig'][l] = dbi[0]
        grads['lru_lambda'][l] = dlam[0]
        grads['q_norm_g'][l] = dgq[0]
        grads['kv_norm_g'][l] = dgkv[0]
        grads['w_pool'][l] = _diag_blocks(dwp, len(POOL_WINDOWS))
        grads['pool_scale'][l] = dps[0]
    for n in grads:
        if n != 'final_norm_g':
            grads[n] = jnp.stack(grads[n])
    return loss[0, 0], dx, grads


WIRE_DTYPE = jnp.bfloat16
MESH_IDS = pl.DeviceIdType.MESH
_HBM = pl.BlockSpec(memory_space=pltpu.HBM)


def _coords():
    return lax.axis_index("x"), lax.axis_index("y"), lax.axis_index("c")


def _all_gather8(xb, name):
    r, cdim = xb.shape

    def body(x_ref, out_ref, send_sems, recv_sems, local_sem):
        x, y, c = _coords()
        me, sibling = (x, y, c), (x, y, 1 - c)
        chips = [(1 - x, y), (x, 1 - y), (1 - x, 1 - y)]

        def slot(px, py, pc):
            return out_ref.at[4 * px + 2 * py + pc]

        def copy(k, block, to, src=None):
            return pltpu.make_async_remote_copy(
                src_ref=slot(*block) if src is None else src, dst_ref=slot(*block),
                send_sem=send_sems.at[k], recv_sem=recv_sems.at[k], device_id=to, device_id_type=MESH_IDS)

        mine = pltpu.make_async_copy(x_ref, slot(*me), local_sem)
        mine.start()
        first = [copy(0, me, sibling, src=x_ref)]
        first += [copy(1 + j, me, (*chip, c), src=x_ref) for j, chip in enumerate(chips)]
        for cp in first:
            cp.start()
        passed = [copy(4 + j, (*chip, c), sibling) for j, chip in enumerate(chips)]
        for j, chip in enumerate(chips):
            copy(1 + j, (*chip, c), me).wait_recv()
            passed[j].start()
        copy(0, sibling, me).wait_recv()
        for j, chip in enumerate(chips):
            copy(4 + j, (*chip, 1 - c), me).wait_recv()
        for cp in first + passed:
            cp.wait_send()
        mine.wait()

    return pl.pallas_call(
        body, name=name, out_shape=jax.ShapeDtypeStruct((N_DEV, r, cdim), xb.dtype),
        in_specs=[_HBM], out_specs=_HBM,
        scratch_shapes=[pltpu.SemaphoreType.DMA((7,)), pltpu.SemaphoreType.DMA((7,)), pltpu.SemaphoreType.DMA],
    )(xb)


def _exchange8(contrib, name):
    _, r, cdim = contrib.shape

    def body(c_ref, out_ref, send_sems, recv_sems, local_sem):
        x, y, c = _coords()
        own = pltpu.make_async_copy(c_ref.at[4 * x + 2 * y + c], out_ref.at[0], local_sem)
        own.start()
        copies = []
        for k in range(1, N_DEV):
            px = 1 - x if k & 4 else x
            py = 1 - y if k & 2 else y
            pc = 1 - c if k & 1 else c
            cp = pltpu.make_async_remote_copy(
                src_ref=c_ref.at[4 * px + 2 * py + pc], dst_ref=out_ref.at[k],
                send_sem=send_sems.at[k - 1], recv_sem=recv_sems.at[k - 1],
                device_id=(px, py, pc), device_id_type=MESH_IDS)
            cp.start()
            copies.append(cp)
        for cp in copies:
            cp.wait()
        own.wait()

    return pl.pallas_call(
        body, name=name, out_shape=jax.ShapeDtypeStruct(contrib.shape, contrib.dtype),
        in_specs=[_HBM], out_specs=_HBM,
        scratch_shapes=[pltpu.SemaphoreType.DMA((7,)), pltpu.SemaphoreType.DMA((7,)), pltpu.SemaphoreType.DMA],
    )(contrib)


def _sibling_gather(xb, name):
    r, cdim = xb.shape

    def body(x_ref, out_ref, send_sem, recv_sem, local_sem):
        x, y, c = _coords()
        own = pltpu.make_async_copy(x_ref, out_ref.at[c], local_sem)
        own.start()
        cp = pltpu.make_async_remote_copy(src_ref=x_ref, dst_ref=out_ref.at[c], send_sem=send_sem, recv_sem=recv_sem,
                                          device_id=(x, y, 1 - c), device_id_type=MESH_IDS)
        cp.start()
        cp.wait()
        own.wait()

    return pl.pallas_call(
        body, name=name, out_shape=jax.ShapeDtypeStruct((2, r, cdim), xb.dtype),
        in_specs=[_HBM], out_specs=_HBM,
        scratch_shapes=[pltpu.SemaphoreType.DMA, pltpu.SemaphoreType.DMA, pltpu.SemaphoreType.DMA],
    )(xb)


SUM_ROWS = 512


def _sum_slots(slots, name):
    n, r, cdim = slots.shape
    tr = r if r <= SUM_ROWS else math.gcd(r, SUM_ROWS)

    def body(s_ref, o_ref):
        acc = s_ref[0].astype(F32)
        for k in range(1, n):
            acc = acc + s_ref[k].astype(F32)
        o_ref[...] = acc

    return pl.pallas_call(
        body, name=name, grid=(r // tr,),
        in_specs=[pl.BlockSpec((n, tr, cdim), lambda i: (0, i, 0))],
        out_specs=pl.BlockSpec((tr, cdim), lambda i: (i, 0)),
        out_shape=jax.ShapeDtypeStruct((r, cdim), F32),
        compiler_params=_cp(("parallel",), 32),
    )(slots)


def _adamw(w, g, m, v, name):
    r, cdim = w.shape
    tr = math.gcd(r, 512) if r * cdim * 4 > (1 << 20) else r

    def body(w_ref, g_ref, m_ref, v_ref, d_ref, mo_ref, vo_ref):
        gv = g_ref[...]
        mn = ADAM_B1 * m_ref[...] + (1.0 - ADAM_B1) * gv
        vn = ADAM_B2 * v_ref[...] + (1.0 - ADAM_B2) * (gv * gv)
        mo_ref[...] = mn
        vo_ref[...] = vn
        m_hat = mn / (1.0 - ADAM_B1 ** ADAM_STEP)
        v_hat = vn / (1.0 - ADAM_B2 ** ADAM_STEP)
        d_ref[...] = (-ADAM_LR) * (m_hat / (jnp.sqrt(v_hat) + ADAM_EPS) + ADAM_WD * w_ref[...])

    blk = pl.BlockSpec((tr, cdim), lambda i: (i, 0))
    return pl.pallas_call(
        body, name=name, grid=(r // tr,),
        in_specs=[blk] * 4, out_specs=[blk] * 3,
        out_shape=[jax.ShapeDtypeStruct((r, cdim), F32)] * 3,
        compiler_params=_cp(("parallel",), 40),
    )(w, g, m, v)


HALF = DEPTH // 2
BIG = ['w_in', 'w_uq', 'w_ukv', 'w_out']
SHARD_AXIS = {'w_in': 2, 'conv_w': 2, 'w_uq': 2, 'w_ukv': 2, 'w_out': 1}
FULL_SHAPE = {'w_in': (DEPTH, D_MODEL, D_IN), 'conv_w': (DEPTH, CONV_WIDTH, D_LRU),
              'w_uq': (DEPTH, Q_RANK, MLA_HEADS * (QK_NOPE + QK_ROPE)),
              'w_ukv': (DEPTH, KV_RANK, MLA_HEADS * (QK_NOPE + V_DIM)), 'w_out': (DEPTH, D_MIX, D_MODEL)}


def _shard_shape(n):
    shp = list(FULL_SHAPE[n])
    shp[SHARD_AXIS[n]] //= N_SHARD
    return tuple(shp)


def _pad_rows(flat, row_mult):
    n = flat.shape[0]
    chunk = row_mult * LANES
    total = -(-n // chunk) * chunk
    return jnp.pad(flat, (0, total - n)).reshape(total // LANES, LANES)


def _gather_weights(local):
    c = lax.axis_index("c")
    parts = []
    for n in BIG:
        parts.append(lax.dynamic_slice_in_dim(local[n], HALF * c, HALF, axis=0).astype(WIRE_DTYPE).reshape(-1))
    cw = lax.dynamic_slice_in_dim(local['conv_w'], HALF * c, HALF, axis=0)
    parts.append(lax.bitcast_convert_type(cw, WIRE_DTYPE).reshape(-1))
    block = _pad_rows(jnp.concatenate(parts), 16)
    got = _all_gather8(block, "gather_weights").reshape(N_SHARD, 2, -1)
    full, off = {}, 0
    for n in BIG + ['conv_w']:
        shp = (HALF,) + _shard_shape(n)[1:]
        words = math.prod(shp) * (2 if n == 'conv_w' else 1)
        seg = got[:, :, off:off + words]
        off += words
        if n == 'conv_w':
            seg = lax.bitcast_convert_type(seg.reshape((N_SHARD, 2) + shp + (2,)), F32)
        else:
            seg = seg.reshape((N_SHARD, 2) + shp)
        if n == 'w_in':
            full[n] = _permute_w_in([seg[s].reshape((DEPTH,) + shp[1:]) for s in range(N_SHARD)])
            continue
        ax = SHARD_AXIS[n] + 2
        seg = jnp.moveaxis(seg, 0, ax - 1)
        full[n] = seg.reshape(FULL_SHAPE[n])
    return full


def _reduce_big(grads):
    blocks = []
    for n in BIG:
        if n == 'w_in':
            per_shard = [_w_in_shard(grads[n], s).astype(WIRE_DTYPE).reshape(2, -1) for s in range(N_SHARD)]
            blocks.append(jnp.stack(per_shard).reshape(N_DEV, -1))
            continue
        g = grads[n].astype(WIRE_DTYPE)
        shp = _shard_shape(n)
        ax = SHARD_AXIS[n]
        g = g.reshape((2, HALF) + g.shape[1:ax] + (N_SHARD, shp[ax]) + g.shape[ax + 1:])
        g = jnp.moveaxis(g, ax + 1, 0)
        blocks.append(g.reshape(N_DEV, -1))
    contrib = jnp.concatenate(blocks, axis=1)
    chunk = SUM_ROWS * LANES
    rows = -(-contrib.shape[1] // chunk) * SUM_ROWS
    contrib = jnp.pad(contrib, ((0, 0), (0, rows * LANES - contrib.shape[1])))
    got = _exchange8(contrib.reshape(N_DEV, rows, LANES), "exchange_big")
    mine = _sum_slots(got, "sum_big")
    both = _sibling_gather(mine, "sibling_big").reshape(2, -1)
    out, off = {}, 0
    for n in BIG:
        shp = _shard_shape(n)
        words = HALF * math.prod(shp[1:])
        out[n] = both[:, off:off + words].reshape(shp)
        off += words
    return out


SMALL = REPLICATED + ['conv_w']


def _pack_small(tree):
    return _pad_rows(jnp.concatenate([tree[n].reshape(-1) for n in SMALL]), N_DEV * SUBLANES)


def _unpack_small(flat2d, like):
    flat = flat2d.reshape(-1)
    out, off = {}, 0
    for n in SMALL:
        size = math.prod(like[n].shape)
        out[n] = flat[off:off + size].reshape(like[n].shape)
        off += size
    return out


def _reduce_small(grads):
    packed = _pack_small(grads)
    rows = packed.shape[0] // N_DEV
    got = _exchange8(packed.reshape(N_DEV, rows, LANES), "exchange_small")
    mine = _sum_slots(got, "sum_small")
    everyone = _all_gather8(mine, "gather_small")
    return _unpack_small(everyone, grads)


def kernel(x, norm_g, w_in, conv_w, conv_b, w_rg, b_rg, w_ig, b_ig, lru_lambda, q_norm_g, w_uq, kv_norm_g, w_ukv, w_pool, pool_scale, w_out, final_norm_g, loss_target, m_norm_g, m_w_in, m_conv_w, m_conv_b, m_w_rg, m_b_rg, m_w_ig, m_b_ig, m_lru_lambda, m_q_norm_g, m_w_uq, m_kv_norm_g, m_w_ukv, m_w_pool, m_pool_scale, m_w_out, m_final_norm_g, v_norm_g, v_w_in, v_conv_w, v_conv_b, v_w_rg, v_b_rg, v_w_ig, v_b_ig, v_lru_lambda, v_q_norm_g, v_w_uq, v_kv_norm_g, v_w_ukv, v_w_pool, v_pool_scale, v_w_out, v_final_norm_g):
    w_loc = dict(zip(WEIGHT_NAMES, (norm_g, w_in, conv_w, conv_b, w_rg, b_rg, w_ig, b_ig, lru_lambda, q_norm_g, w_uq,
                                    kv_norm_g, w_ukv, w_pool, pool_scale, w_out, final_norm_g)))
    m_loc = dict(zip(WEIGHT_NAMES, (m_norm_g, m_w_in, m_conv_w, m_conv_b, m_w_rg, m_b_rg, m_w_ig, m_b_ig, m_lru_lambda,
                                    m_q_norm_g, m_w_uq, m_kv_norm_g, m_w_ukv, m_w_pool, m_pool_scale, m_w_out,
                                    m_final_norm_g)))
    v_loc = dict(zip(WEIGHT_NAMES, (v_norm_g, v_w_in, v_conv_w, v_conv_b, v_w_rg, v_b_rg, v_w_ig, v_b_ig, v_lru_lambda,
                                    v_q_norm_g, v_w_uq, v_kv_norm_g, v_w_ukv, v_w_pool, v_pool_scale, v_w_out,
                                    v_final_norm_g)))
    w_full = dict(w_loc)
    w_full.update(_gather_weights(w_loc))
    loss_local, dx, g_local = _local_step(x[0], loss_target[0], w_full)
    loss = lax.psum(loss_local, ("x", "y", "c"))

    grads = _reduce_big(g_local)
    g_small = _reduce_small(g_local)
    shard = 2 * lax.axis_index("x") + lax.axis_index("y")
    width = D_LRU // N_SHARD
    grads['conv_w'] = lax.dynamic_slice_in_dim(g_small['conv_w'], shard * width, width, axis=2)
    for n in REPLICATED:
        grads[n] = g_small[n]

    delta, new_m, new_v = {}, {}, {}
    for n in BIG:
        two_d = lambda a: a.reshape(-1, a.shape[-1])
        d, mo, vo = _adamw(two_d(w_loc[n]), two_d(grads[n]), two_d(m_loc[n]), two_d(v_loc[n]), "adamw_" + n)
        delta[n], new_m[n], new_v[n] = (a.reshape(w_loc[n].shape) for a in (d, mo, vo))
    small_g = dict(grads)
    d, mo, vo = _adamw(_pack_small(w_loc), _pack_small(small_g), _pack_small(m_loc), _pack_small(v_loc), "adamw_small")
    for tree, flat in ((delta, d), (new_m, mo), (new_v, vo)):
        tree.update(_unpack_small(flat, w_loc))

    return (loss, dx[None], *[grads[n] for n in WEIGHT_NAMES], *[delta[n] for n in WEIGHT_NAMES],
            *[new_m[n] for n in WEIGHT_NAMES], *[new_v[n] for n in WEIGHT_NAMES])
```

```python
import functools
import math

import jax
import jax.numpy as jnp
import numpy as np
from jax import lax
from jax.experimental import pallas as pl
from jax.experimental.pallas import tpu as pltpu

F32 = jnp.float32
MXU_DTYPE = jnp.bfloat16

D_MODEL = 1024
DEPTH = 4
EPS = 1e-6
D_LRU = 384
LRU_HEADS = 6
CONV_WIDTH = 4
LRU_C = 8.0
MLA_HEADS = 6
QK_NOPE = 64
QK_ROPE = 32
V_DIM = 64
D_MLA = MLA_HEADS * V_DIM
Q_RANK = 384
KV_RANK = 256
ROPE_BASE = 10000.0
POOL_WINDOWS = (2, 4, 8, 16)
D_POOL = 256
D_MIX = D_LRU + D_MLA + D_POOL
D_IN = 2336
ATT_SCALE = (QK_NOPE + QK_ROPE) ** -0.5
LOG2_E = 1.4426950408889634
LN_2 = 0.6931471805599453
Q_PRESCALE = ATT_SCALE * LOG2_E

ADAM_LR = 0.001
ADAM_B1 = 0.9
ADAM_B2 = 0.999
ADAM_EPS = 1e-08
ADAM_WD = 0.01
ADAM_STEP = 10

LANES = 128
SUBLANES = 8
V7X_VMEM_BYTES = 64 << 20
N_DEV = 8

D_INP = 2432
KR_LANE0 = 64
HEAD_BLK = 128
N_PAIR = MLA_HEADS // 2

WEIGHT_NAMES = ['norm_g', 'w_in', 'conv_w', 'conv_b', 'w_rg', 'b_rg', 'w_ig', 'b_ig', 'lru_lambda', 'q_norm_g',
                'w_uq', 'kv_norm_g', 'w_ukv', 'w_pool', 'pool_scale', 'w_out', 'final_norm_g']
SHARDED = ['w_in', 'conv_w', 'w_uq', 'w_ukv', 'w_out']
REPLICATED = [n for n in WEIGHT_NAMES if n not in SHARDED]


def _cp(sem, vmem_mb=None):
    return pltpu.CompilerParams(dimension_semantics=sem,
                                vmem_limit_bytes=None if vmem_mb is None else vmem_mb << 20)


def _dot(a, b):
    return jnp.dot(a.astype(MXU_DTYPE), b.astype(MXU_DTYPE), preferred_element_type=F32)


def _dot_nt(a, b):
    return lax.dot_general(a.astype(MXU_DTYPE), b.astype(MXU_DTYPE), (((1,), (1,)), ((), ())),
                           preferred_element_type=F32)


def _dot_tn(a, b):
    return lax.dot_general(a.astype(MXU_DTYPE), b.astype(MXU_DTYPE), (((0,), (0,)), ((), ())),
                           preferred_element_type=F32)


def _sig(x):
    return 1.0 / (1.0 + jnp.exp(-x))


def _down(x, k):
    return pltpu.roll(x, k, 0)


def _up(x, k):
    return pltpu.roll(x, x.shape[0] - k, 0)


def _rows(shape):
    return lax.broadcasted_iota(jnp.int32, shape, 0)


def _lanes(shape):
    return lax.broadcasted_iota(jnp.int32, shape, 1)


def _tile(s):
    return min(512, s)


def _lay(l):
    return lambda shp, blk=0: pl.BlockSpec((None,) + shp, lambda *_: (l, blk, 0))


def _inproj_fwd(x, g, w, l):
    s = x.shape[0]
    t = _tile(s)
    lay = _lay(l)

    def body(x_ref, g_ref, w_ref, h_ref, z_ref):
        xv = x_ref[...]
        rs = lax.rsqrt(jnp.mean(xv * xv, axis=-1, keepdims=True) + EPS)
        h = (xv * rs * g_ref[...]).astype(MXU_DTYPE)
        h_ref[...] = h
        z_ref[...] = jnp.dot(h, w_ref[...], preferred_element_type=F32)

    return pl.pallas_call(
        body, name="inproj_fwd", grid=(s // t,),
        in_specs=[pl.BlockSpec((t, D_MODEL), lambda i: (i, 0)),
                  lay((1, D_MODEL)), lay((D_MODEL, D_INP))],
        out_specs=[pl.BlockSpec((t, D_MODEL), lambda i: (i, 0)),
                   pl.BlockSpec((t, D_INP), lambda i: (i, 0))],
        out_shape=[jax.ShapeDtypeStruct((s, D_MODEL), MXU_DTYPE), jax.ShapeDtypeStruct((s, D_INP), F32)],
        compiler_params=_cp(("parallel",), 40),
    )(x, g, w)


def _lru_gates(za, halo, cw_ref, cb_ref, wr_ref, br_ref, wi_ref, bi_ref, lam_ref):
    t = za.shape[0]
    ext = jnp.concatenate([halo, za], axis=0)
    sh = [za] + [_down(ext, j)[SUBLANES:SUBLANES + t] for j in (1, 2, 3)]
    xa = cb_ref[...] + cw_ref[3:4, :] * sh[0] + cw_ref[2:3, :] * sh[1] + cw_ref[1:2, :] * sh[2] + cw_ref[0:1, :] * sh[3]
    r = _sig(_dot(xa, wr_ref[...]) + br_ref[...])
    ig = _sig(_dot(xa, wi_ref[...]) + bi_ref[...])
    lam = lam_ref[...]
    sp = jnp.maximum(-lam, 0.0) + jnp.log(1.0 + jnp.exp(-jnp.abs(lam)))
    la = (-LRU_C) * r * sp
    a = jnp.exp(la)
    y2 = 2.0 * la
    taylor = -(y2 * (1.0 + y2 * (0.5 + y2 * (1.0 / 6.0 + y2 * (1.0 / 24.0)))))
    m2 = jnp.where(y2 > -0.02, taylor, 1.0 - jnp.exp(y2))
    m = jnp.sqrt(m2)
    return xa, sh, r, ig, sp, a, m


def _lru_fwd(z, cw, cb, wr, br, wi, bi, lam, l):
    s = z.shape[0]
    t = _tile(s)
    c = D_LRU
    lay = _lay(l)

    def body(za_ref, ga_ref, cw_ref, cb_ref, wr_ref, br_ref, wi_ref, bi_ref, lam_ref, ya_ref, hs_ref, zprev, hcar):
        i = pl.program_id(0)

        @pl.when(i == 0)
        def _():
            zprev[...] = jnp.zeros_like(zprev)
            hcar[...] = jnp.zeros_like(hcar)

        za = za_ref[...]
        xa, _, _, ig, _, a, m = _lru_gates(za, zprev[...], cw_ref, cb_ref, wr_ref, br_ref, wi_ref, bi_ref, lam_ref)
        u = m * (ig * xa)
        row = _rows((t, c))
        acc_a, acc_h = a, u
        k = 1
        while k < t:
            a_sh = jnp.where(row >= k, _down(acc_a, k), 1.0)
            h_sh = jnp.where(row >= k, _down(acc_h, k), 0.0)
            acc_h = acc_h + acc_a * h_sh
            acc_a = acc_a * a_sh
            k *= 2
        hs = acc_h + acc_a * hcar[...]
        hs_ref[...] = hs
        ga = ga_ref[...]
        ya_ref[...] = (hs * (ga * _sig(ga))).astype(ya_ref.dtype)
        hcar[...] = jnp.sum(jnp.where(row == t - 1, hs, 0.0), axis=0, keepdims=True)
        zprev[...] = za_ref[t - SUBLANES:t, :]

    return pl.pallas_call(
        body, name="lru_fwd", grid=(s // t,),
        in_specs=[pl.BlockSpec((t, c), lambda i: (i, 0)), pl.BlockSpec((t, c), lambda i: (i, 1)),
                  lay((CONV_WIDTH, c)), lay((1, c)), lay((c, c)), lay((1, c)), lay((c, c)), lay((1, c)),
                  lay((1, c))],
        out_specs=[pl.BlockSpec((t, c), lambda i: (i, 0)), pl.BlockSpec((t, c), lambda i: (i, 0))],
        out_shape=[jax.ShapeDtypeStruct((s, c), MXU_DTYPE), jax.ShapeDtypeStruct((s, c), F32)],
        scratch_shapes=[pltpu.VMEM((SUBLANES, c), F32), pltpu.VMEM((1, c), F32)],
        compiler_params=_cp(("arbitrary",), 40),
    )(z, z, cw, cb, wr, br, wi, bi, lam)


def _lru_bwd(z, hs, dy, cw, cb, wr, br, wi, bi, lam, l):
    s = z.shape[0]
    t = _tile(s)
    lay = _lay(l)
    nt = s // t
    c = D_LRU
    hb = t // SUBLANES

    def body(za_ref, zh_ref, ga_ref, hs_ref, hh_ref, dy_ref, cw_ref, cb_ref, wr_ref, br_ref, wi_ref, bi_ref, lam_ref,
             dza_ref, dga_ref, dwr_ref, dwi_ref, dcw_ref, dcb_ref, dbr_ref, dbi_ref, dlam_ref, lcar, dxn):
        i = pl.program_id(0)
        tt = nt - 1 - i

        @pl.when(i == 0)
        def _():
            lcar[...] = jnp.zeros_like(lcar)
            dxn[...] = jnp.zeros_like(dxn)
            for ref in (dwr_ref, dwi_ref, dcw_ref, dcb_ref, dbr_ref, dbi_ref, dlam_ref):
                ref[...] = jnp.zeros_like(ref)

        first = (tt > 0).astype(F32)
        za = za_ref[...]
        xa, sh, r, ig, sp, a, m = _lru_gates(za, zh_ref[...] * first, cw_ref, cb_ref, wr_ref, br_ref, wi_ref, bi_ref,
                                             lam_ref)
        hs_v = hs_ref[...]
        hprev = _down(jnp.concatenate([hh_ref[...] * first, hs_v], axis=0), 1)[SUBLANES:SUBLANES + t]
        ga = ga_ref[...]
        sg = _sig(ga)
        silu = ga * sg
        dya = dy_ref[...]
        dga_ref[...] = (dya * hs_v * (sg * (1.0 + ga * (1.0 - sg)))).astype(dga_ref.dtype)
        row = _rows((t, c))
        acc_h = dya * silu + jnp.where(row == t - 1, lcar[...], 0.0)
        acc_b = jnp.where(row < t - 1, _up(a, 1), 0.0)
        k = 1
        while k < t:
            keep = row < t - k
            b_sh = jnp.where(keep, _up(acc_b, k), 0.0)
            h_sh = jnp.where(keep, _up(acc_h, k), 0.0)
            acc_h = acc_h + acc_b * h_sh
            acc_b = acc_b * b_sh
            k *= 2
        lmb = acc_h
        lcar[...] = jnp.sum(jnp.where(row == 0, a * lmb, 0.0), axis=0, keepdims=True)
        da = lmb * hprev
        dxa = lmb * m * ig
        di = lmb * m * xa
        dm = lmb * ig * xa
        dla = da * a - dm * (a * a) / m
        dr = dla * ((-LRU_C) * sp)
        lam = lam_ref[...]
        dsp = jnp.sum(dla * ((-LRU_C) * r), axis=0, keepdims=True)
        dlam_ref[...] += dsp * (-1.0 / (1.0 + jnp.exp(lam)))
        dpr = dr * r * (1.0 - r)
        dpi = di * ig * (1.0 - ig)
        dbr_ref[...] += jnp.sum(dpr, axis=0, keepdims=True)
        dbi_ref[...] += jnp.sum(dpi, axis=0, keepdims=True)
        dwr_ref[...] += _dot_tn(xa, dpr)
        dwi_ref[...] += _dot_tn(xa, dpi)
        dxa = dxa + _dot_nt(dpr, wr_ref[...]) + _dot_nt(dpi, wi_ref[...])
        dcb_ref[...] += jnp.sum(dxa, axis=0, keepdims=True)
        for k in range(CONV_WIDTH):
            dcw_ref[k:k + 1, :] += jnp.sum(dxa * sh[CONV_WIDTH - 1 - k], axis=0, keepdims=True)
        ext = jnp.concatenate([dxa, dxn[...]], axis=0)
        dza = cw_ref[3:4, :] * dxa
        for j in (1, 2, 3):
            dza = dza + cw_ref[3 - j:4 - j, :] * _up(ext, j)[:t]
        dza_ref[...] = dza.astype(dza_ref.dtype)
        dxn[...] = dxa[:SUBLANES]

    full = lambda shp: pl.BlockSpec(shp, lambda i: (0, 0))
    rev = lambda i: nt - 1 - i
    halo = lambda i: (jnp.maximum((nt - 1 - i) * hb - 1, 0), 0)
    outs = pl.pallas_call(
        body, name="lru_bwd", grid=(nt,),
        in_specs=[pl.BlockSpec((t, c), lambda i: (rev(i), 0)), pl.BlockSpec((SUBLANES, c), halo),
                  pl.BlockSpec((t, c), lambda i: (rev(i), 1)),
                  pl.BlockSpec((t, c), lambda i: (rev(i), 0)), pl.BlockSpec((SUBLANES, c), halo),
                  pl.BlockSpec((t, c), lambda i: (rev(i), 0)),
                  lay((CONV_WIDTH, c)), lay((1, c)), lay((c, c)), lay((1, c)), lay((c, c)), lay((1, c)),
                  lay((1, c))],
        out_specs=[pl.BlockSpec((t, c), lambda i: (rev(i), 0)), pl.BlockSpec((t, c), lambda i: (rev(i), 0)),
                   full((c, c)), full((c, c)), full((CONV_WIDTH, c)), full((1, c)), full((1, c)), full((1, c)),
                   full((1, c))],
        out_shape=[jax.ShapeDtypeStruct((s, c), MXU_DTYPE), jax.ShapeDtypeStruct((s, c), MXU_DTYPE),
                   jax.ShapeDtypeStruct((c, c), F32), jax.ShapeDtypeStruct((c, c), F32),
                   jax.ShapeDtypeStruct((CONV_WIDTH, c), F32)] + [jax.ShapeDtypeStruct((1, c), F32)] * 4,
        scratch_shapes=[pltpu.VMEM((1, c), F32), pltpu.VMEM((SUBLANES, c), F32)],
        compiler_params=_cp(("arbitrary",), 48),
    )(z, z, z, hs, hs, dy, cw, cb, wr, br, wi, bi, lam)
    return outs


POOL_HALO = 16


def _pool_select(lane, v2, v4, v8, v16):
    return jnp.where(lane < 64, v2, jnp.where(lane < 128, v4, jnp.where(lane < 192, v8, v16)))


def _pool_counts(t0, t, c):
    lane = _lanes((t, c))
    win = _pool_select(lane, 2.0, 4.0, 8.0, 16.0)
    seen = (t0 + _rows((t, c)) + 1).astype(F32)
    return lane, jnp.minimum(seen, win)


def _pooled(zc, halo, lane, cnt):
    t = zc.shape[0]
    ext = jnp.concatenate([halo, zc], axis=0)
    s2 = ext + _down(ext, 1)
    s4 = s2 + _down(s2, 2)
    s8 = s4 + _down(s4, 4)
    s16 = s8 + _down(s8, 8)
    cut = lambda v: v[POOL_HALO:POOL_HALO + t]
    return _pool_select(lane, cut(s2), cut(s4), cut(s8), cut(s16)) / cnt - zc


def _pool_fwd(z, wp, ps, l):
    s = z.shape[0]
    t = _tile(s)
    c = D_POOL
    lay = _lay(l)

    def body(zc_ref, gc_ref, wp_ref, ps_ref, yc_ref, zprev):
        i = pl.program_id(0)

        @pl.when(i == 0)
        def _():
            zprev[...] = jnp.zeros_like(zprev)

        zc = zc_ref[...]
        lane, cnt = _pool_counts(i * t, t, c)
        pooled = _pooled(zc, zprev[...], lane, cnt)
        pc = _dot(pooled, wp_ref[...])
        gc = gc_ref[...]
        yc_ref[...] = (pc * ps_ref[...] * (gc * _sig(gc))).astype(yc_ref.dtype)
        zprev[...] = zc_ref[t - POOL_HALO:t, :]

    return pl.pallas_call(
        body, name="pool_fwd", grid=(s // t,),
        in_specs=[pl.BlockSpec((t, c), lambda i: (i, 7)), pl.BlockSpec((t, c), lambda i: (i, 8)),
                  lay((c, c)), lay((1, c))],
        out_specs=pl.BlockSpec((t, c), lambda i: (i, 0)),
        out_shape=jax.ShapeDtypeStruct((s, c), MXU_DTYPE),
        scratch_shapes=[pltpu.VMEM((POOL_HALO, c), F32)],
        compiler_params=_cp(("arbitrary",), 32),
    )(z, z, wp, ps)


def _pool_bwd(z, dy, wp, ps, l):
    s = z.shape[0]
    t = _tile(s)
    lay = _lay(l)
    nt = s // t
    c = D_POOL
    hb = t // POOL_HALO

    def body(zc_ref, zh_ref, gc_ref, dy_ref, wp_ref, ps_ref, dzc_ref, dgc_ref, dwp_ref, dps_ref, ddn):
        i = pl.program_id(0)
        tt = nt - 1 - i

        @pl.when(i == 0)
        def _():
            ddn[...] = jnp.zeros_like(ddn)
            dwp_ref[...] = jnp.zeros_like(dwp_ref)
            dps_ref[...] = jnp.zeros_like(dps_ref)

        first = (tt > 0).astype(F32)
        zc = zc_ref[...]
        lane, cnt = _pool_counts(tt * t, t, c)
        pooled = _pooled(zc, zh_ref[...] * first, lane, cnt)
        pc = _dot(pooled, wp_ref[...])
        gc = gc_ref[...]
        sg = _sig(gc)
        silu = gc * sg
        dyc = dy_ref[...]
        ps_v = ps_ref[...]
        dgc_ref[...] = (dyc * pc * ps_v * (sg * (1.0 + gc * (1.0 - sg)))).astype(dgc_ref.dtype)
        dps_ref[...] += jnp.sum(dyc * pc * silu, axis=0, keepdims=True)
        dpc = dyc * ps_v * silu
        dwp_ref[...] += _dot_tn(pooled, dpc)
        dpooled = _dot_nt(dpc, wp_ref[...])
        dd = dpooled / cnt
        ext = jnp.concatenate([dd, ddn[...]], axis=0)
        f2 = ext + _up(ext, 1)
        f4 = f2 + _up(f2, 2)
        f8 = f4 + _up(f4, 4)
        f16 = f8 + _up(f8, 8)
        dzc = _pool_select(lane, f2[:t], f4[:t], f8[:t], f16[:t]) - dpooled
        dzc_ref[...] = dzc.astype(dzc_ref.dtype)
        ddn[...] = dd[:POOL_HALO]

    full = lambda shp: pl.BlockSpec(shp, lambda i: (0, 0))
    rev = lambda i: nt - 1 - i
    return pl.pallas_call(
        body, name="pool_bwd", grid=(nt,),
        in_specs=[pl.BlockSpec((t, c), lambda i: (rev(i), 7)),
                  pl.BlockSpec((POOL_HALO, c), lambda i: (jnp.maximum(rev(i) * hb - 1, 0), 7)),
                  pl.BlockSpec((t, c), lambda i: (rev(i), 8)),
                  pl.BlockSpec((t, c), lambda i: (rev(i), 3)),
                  lay((c, c)), lay((1, c))],
        out_specs=[pl.BlockSpec((t, c), lambda i: (rev(i), 0)), pl.BlockSpec((t, c), lambda i: (rev(i), 0)),
                   full((c, c)), full((1, c))],
        out_shape=[jax.ShapeDtypeStruct((s, c), MXU_DTYPE), jax.ShapeDtypeStruct((s, c), MXU_DTYPE),
                   jax.ShapeDtypeStruct((c, c), F32), jax.ShapeDtypeStruct((1, c), F32)],
        scratch_shapes=[pltpu.VMEM((POOL_HALO, c), F32)],
        compiler_params=_cp(("arbitrary",), 32),
    )(z, z, z, dy, wp, ps)


def _rope_tables(s):
    pos = jnp.arange(s, dtype=F32)
    inv_freq = ROPE_BASE ** (-jnp.arange(0, QK_ROPE, 2, dtype=F32) / QK_ROPE)
    ang = pos[:, None] * inv_freq[None, :]
    cos, sin = jnp.cos(ang), jnp.sin(ang)
    half = QK_ROPE // 2
    ones = jnp.ones((s, QK_NOPE), F32)
    z64 = jnp.zeros((s, QK_NOPE), F32)
    zh = jnp.zeros((s, half), F32)
    z32 = jnp.zeros((s, HEAD_BLK - QK_NOPE - QK_ROPE), F32)
    c_t = jnp.concatenate([ones, cos, cos, z32], axis=1)
    s1_t = jnp.concatenate([z64, -sin, zh, z32], axis=1)
    s2_t = jnp.concatenate([z64, zh, sin, z32], axis=1)
    return c_t, s1_t, s2_t


def _rope(x, c_t, s1_t, s2_t):
    return x * c_t + pltpu.roll(x, HEAD_BLK - 16, 1) * s1_t + pltpu.roll(x, 16, 1) * s2_t


def _unrope(d, c_t, s1_t, s2_t):
    return d * c_t + pltpu.roll(d * s1_t, 16, 1) + pltpu.roll(d * s2_t, HEAD_BLK - 16, 1)


def _mla_prep_fwd(z, gq, gkv, wuq, wukv, c_t, s1_t, s2_t, l):
    s = z.shape[0]
    t = _tile(s)
    hq = MLA_HEADS * HEAD_BLK
    lay = _lay(l)

    def body(cq_ref, ckv_ref, kr_ref, gq_ref, gkv_ref, wuq_ref, wukv_ref, c_ref, s1_ref, s2_ref,
             q_ref, k_ref, v_ref, qn_ref, kvn_ref):
        ct, s1, s2 = c_ref[...], s1_ref[...], s2_ref[...]
        cq = cq_ref[...]
        qn = (cq * lax.rsqrt(jnp.mean(cq * cq, axis=-1, keepdims=True) + EPS) * gq_ref[...]).astype(MXU_DTYPE)
        qn_ref[...] = qn
        q = jnp.dot(qn, wuq_ref[...], preferred_element_type=F32)
        ckv = ckv_ref[...]
        kvn = (ckv * lax.rsqrt(jnp.mean(ckv * ckv, axis=-1, keepdims=True) + EPS) * gkv_ref[...]).astype(MXU_DTYPE)
        kvn_ref[...] = kvn
        kvp = jnp.dot(kvn, wukv_ref[...], preferred_element_type=F32)
        krr = _rope(kr_ref[...], ct, s1, s2)
        for h in range(MLA_HEADS):
            blk = slice(h * HEAD_BLK, (h + 1) * HEAD_BLK)
            q_ref[:, blk] = (_rope(q[:, blk], ct, s1, s2) * Q_PRESCALE).astype(q_ref.dtype)
            k_ref[:, blk] = (kvp[:, blk] + krr).astype(k_ref.dtype)
        v_ref[...] = kvp[:, hq:].astype(v_ref.dtype)

    tab = pl.BlockSpec((t, HEAD_BLK), lambda i: (i, 0))
    return pl.pallas_call(
        body, name="mla_prep_fwd", grid=(s // t,),
        in_specs=[pl.BlockSpec((t, Q_RANK), lambda i: (i, 2)), pl.BlockSpec((t, KV_RANK), lambda i: (i, 6)),
                  pl.BlockSpec((t, HEAD_BLK), lambda i: (i, 18)),
                  lay((1, Q_RANK)), lay((1, KV_RANK)), lay((Q_RANK, hq)), lay((KV_RANK, hq + D_MLA)),
                  tab, tab, tab],
        out_specs=[pl.BlockSpec((t, hq), lambda i: (i, 0)), pl.BlockSpec((t, hq), lambda i: (i, 0)),
                   pl.BlockSpec((t, D_MLA), lambda i: (i, 0)), pl.BlockSpec((t, Q_RANK), lambda i: (i, 0)),
                   pl.BlockSpec((t, KV_RANK), lambda i: (i, 0))],
        out_shape=[jax.ShapeDtypeStruct((s, hq), MXU_DTYPE), jax.ShapeDtypeStruct((s, hq), MXU_DTYPE),
                   jax.ShapeDtypeStruct((s, D_MLA), MXU_DTYPE), jax.ShapeDtypeStruct((s, Q_RANK), MXU_DTYPE),
                   jax.ShapeDtypeStruct((s, KV_RANK), MXU_DTYPE)],
        compiler_params=_cp(("parallel",), 40),
    )(z, z, z, gq, gkv, wuq, wukv, c_t, s1_t, s2_t)


SUM_LANE_A = V_DIM
SUM_LANE_B = 0


def _flash_fwd(q, k, v, z):
    s = q.shape[0]
    t = _tile(s)
    nq = s // t
    pw = 2 * HEAD_BLK

    def body(q_ref, k_ref, v_ref, gb_ref, o_ref, yb_ref, lse_ref):
        i = pl.program_id(1)
        qv = q_ref[...]
        qa, qb = qv[:, :HEAD_BLK], qv[:, HEAD_BLK:]
        lane = _lanes((t, HEAD_BLK))
        lo = lane < V_DIM

        def update(qh, kh, vh, m, acc, masked):
            sc = _dot_nt(qh, kh)
            if masked:
                sc = jnp.where(_lanes((t, t)) <= _rows((t, t)), sc, -1e30)
            m_new = jnp.maximum(m, jnp.max(sc, axis=-1, keepdims=True))
            p = jnp.exp2(sc - m_new).astype(MXU_DTYPE)
            return m_new, acc * jnp.exp2(m - m_new) + _dot(p, vh)

        def step(j, carry, masked):
            ma, mb, acc_a, acc_b = carry
            kv_rows = pl.ds(pl.multiple_of(j * t, t), t)
            kt = k_ref[kv_rows, :]
            vt = v_ref[kv_rows, :]
            lane_v = _lanes(vt.shape)
            one = jnp.ones_like(vt)
            zero_v = jnp.zeros_like(vt)
            v_a = jnp.where(lane_v < V_DIM, vt, jnp.where(lane_v == SUM_LANE_A, one, zero_v))
            v_b = jnp.where(lane_v >= V_DIM, vt, jnp.where(lane_v == SUM_LANE_B, one, zero_v))
            ma, acc_a = update(qa, kt[:, :HEAD_BLK], v_a, ma, acc_a, masked)
            mb, acc_b = update(qb, kt[:, HEAD_BLK:], v_b, mb, acc_b, masked)
            return ma, mb, acc_a, acc_b

        neg = jnp.full((t, 1), -1e30, F32)
        zero = jnp.zeros((t, HEAD_BLK), F32)
        carry = lax.fori_loop(0, i, lambda j, cr: step(j, cr, False), (neg, neg, zero, zero))
        ma, mb, acc_a, acc_b = step(i, carry, True)
        la = jnp.sum(jnp.where(lane == SUM_LANE_A, acc_a, 0.0), axis=-1, keepdims=True)
        lb = jnp.sum(jnp.where(lane == SUM_LANE_B, acc_b, 0.0), axis=-1, keepdims=True)
        o = jnp.where(lo, acc_a * (1.0 / la), acc_b * (1.0 / lb))
        o_ref[...] = o
        gb = gb_ref[...]
        yb_ref[...] = (o * (gb * _sig(gb))).astype(yb_ref.dtype)
        lse = jnp.where(lo, ma + jnp.log(la) * LOG2_E, mb + jnp.log(lb) * LOG2_E)
        pick = ((_rows((SUBLANES, HEAD_BLK)) == 0) & (_lanes((SUBLANES, HEAD_BLK)) == 0)) | (
            (_rows((SUBLANES, HEAD_BLK)) == 1) & (_lanes((SUBLANES, HEAD_BLK)) == V_DIM))
        lse_ref[0, 0] = lax.dot_general(pick.astype(F32), lse, (((1,), (1,)), ((), ())),
                                        precision=lax.Precision.HIGHEST, preferred_element_type=F32)

    return pl.pallas_call(
        body, name="flash_fwd", grid=(N_PAIR, nq),
        in_specs=[pl.BlockSpec((t, pw), lambda p, i: (i, p)), pl.BlockSpec((s, pw), lambda p, i: (0, p)),
                  pl.BlockSpec((s, HEAD_BLK), lambda p, i: (0, p)),
                  pl.BlockSpec((t, HEAD_BLK), lambda p, i: (i, 9 + p))],
        out_specs=[pl.BlockSpec((t, HEAD_BLK), lambda p, i: (i, p)), pl.BlockSpec((t, HEAD_BLK), lambda p, i: (i, p)),
                   pl.BlockSpec((1, 1, SUBLANES, t), lambda p, i: (p, i, 0, 0))],
        out_shape=[jax.ShapeDtypeStruct((s, D_MLA), F32), jax.ShapeDtypeStruct((s, D_MLA), MXU_DTYPE),
                   jax.ShapeDtypeStruct((N_PAIR, nq, SUBLANES, t), F32)],
        compiler_params=_cp(("parallel", "parallel"), 48),
    )(q, k, v, z)


def _attn_bwd_prep(dy, o, z):
    s = o.shape[0]
    t = _tile(s)
    nq = s // t
    rows = N_PAIR * SUBLANES

    def body(dy_ref, o_ref, gb_ref, do_ref, dgb_ref, dl_ref):
        gb = gb_ref[...]
        sg = _sig(gb)
        dyb = dy_ref[...]
        ov = o_ref[...]
        do = dyb * (gb * sg)
        do_ref[...] = do.astype(do_ref.dtype)
        dgb_ref[...] = (dyb * ov * (sg * (1.0 + gb * (1.0 - sg)))).astype(dgb_ref.dtype)
        r = _rows((rows, D_MLA))
        head = (r // SUBLANES) * 2 + (r % SUBLANES)
        sel = ((r % SUBLANES) < 2) & (_lanes((rows, D_MLA)) // V_DIM == head)
        dl = lax.dot_general(sel.astype(F32), do * ov, (((1,), (1,)), ((), ())),
                             precision=lax.Precision.HIGHEST, preferred_element_type=F32)
        for p in range(N_PAIR):
            dl_ref[p, 0] = dl[p * SUBLANES:(p + 1) * SUBLANES]

    return pl.pallas_call(
        body, name="attn_bwd_prep", grid=(nq,),
        in_specs=[pl.BlockSpec((t, D_MLA), lambda i: (i, 1)), pl.BlockSpec((t, D_MLA), lambda i: (i, 0)),
                  pl.BlockSpec((t, D_MLA), lambda i: (i, 3))],
        out_specs=[pl.BlockSpec((t, D_MLA), lambda i: (i, 0)), pl.BlockSpec((t, D_MLA), lambda i: (i, 0)),
                   pl.BlockSpec((N_PAIR, 1, SUBLANES, t), lambda i: (0, i, 0, 0))],
        out_shape=[jax.ShapeDtypeStruct((s, D_MLA), MXU_DTYPE), jax.ShapeDtypeStruct((s, D_MLA), MXU_DTYPE),
                   jax.ShapeDtypeStruct((N_PAIR, nq, SUBLANES, t), F32)],
        compiler_params=_cp(("parallel",), 32),
    )(dy, o, z)


def _flash_bwd(q, k, v, do, lse, delta):
    s = q.shape[0]
    t = _tile(s)
    nq = s // t
    pw = 2 * HEAD_BLK

    def body(q_ref, do_ref, lse_ref, dl_ref, k_ref, v_ref, dq_ref, dk_ref, dv_ref):
        j = pl.program_id(1)

        @pl.when(j == 0)
        def _():
            dq_ref[...] = jnp.zeros_like(dq_ref)

        kt = k_ref[...]
        ka, kb = kt[:, :HEAD_BLK], kt[:, HEAD_BLK:]
        vt = v_ref[...]

        def head(kh, qh, do_h, lse_row, dl_row, masked):
            st = _dot_nt(kh, qh)
            if masked:
                st = jnp.where(_rows((t, t)) <= _lanes((t, t)), st, -1e30)
            pt = jnp.exp2(st - lse_row)
            dv_h = _dot(pt, do_h)
            dst = (pt * (_dot_nt(vt, do_h) - dl_row)).astype(MXU_DTYPE)
            return dv_h, _dot(dst, qh), _dot_tn(dst, kh)

        def step(i, carry, masked):
            dka, dkb, dv = carry
            q_rows = pl.ds(pl.multiple_of(i * t, t), t)
            qv = q_ref[q_rows, :]
            dov = do_ref[q_rows, :]
            lane = _lanes(dov.shape)
            do_lo = jnp.where(lane < V_DIM, dov, jnp.zeros_like(dov))
            do_hi = jnp.where(lane >= V_DIM, dov, jnp.zeros_like(dov))
            dva, dk_a, dq_a = head(ka, qv[:, :HEAD_BLK], do_lo, lse_ref[0, i, 0:1, :], dl_ref[0, i, 0:1, :], masked)
            dvb, dk_b, dq_b = head(kb, qv[:, HEAD_BLK:], do_hi, lse_ref[0, i, 1:2, :], dl_ref[0, i, 1:2, :], masked)
            dq_ref[q_rows, 0:HEAD_BLK] += dq_a
            dq_ref[q_rows, HEAD_BLK:pw] += dq_b
            return dka + dk_a, dkb + dk_b, dv + dva + dvb

        zero = jnp.zeros((t, HEAD_BLK), F32)
        carry = step(j, (zero, zero, zero), True)
        dka, dkb, dv = lax.fori_loop(j + 1, nq, lambda i, cr: step(i, cr, False), carry)
        dk_ref[:, 0:HEAD_BLK] = dka * LN_2
        dk_ref[:, HEAD_BLK:pw] = dkb * LN_2
        dv_ref[...] = dv.astype(dv_ref.dtype)

    return pl.pallas_call(
        body, name="flash_bwd", grid=(N_PAIR, nq),
        in_specs=[pl.BlockSpec((s, pw), lambda p, j: (0, p)), pl.BlockSpec((s, HEAD_BLK), lambda p, j: (0, p)),
                  pl.BlockSpec((1, nq, SUBLANES, t), lambda p, j: (p, 0, 0, 0)),
                  pl.BlockSpec((1, nq, SUBLANES, t), lambda p, j: (p, 0, 0, 0)),
                  pl.BlockSpec((t, pw), lambda p, j: (j, p)), pl.BlockSpec((t, HEAD_BLK), lambda p, j: (j, p))],
        out_specs=[pl.BlockSpec((s, pw), lambda p, j: (0, p)), pl.BlockSpec((t, pw), lambda p, j: (j, p)),
                   pl.BlockSpec((t, HEAD_BLK), lambda p, j: (j, p))],
        out_shape=[jax.ShapeDtypeStruct((s, MLA_HEADS * HEAD_BLK), F32),
                   jax.ShapeDtypeStruct((s, MLA_HEADS * HEAD_BLK), F32),
                   jax.ShapeDtypeStruct((s, D_MLA), MXU_DTYPE)],
        compiler_params=_cp(("parallel", "arbitrary"), 56),
    )(q, do, lse, delta, k, v)


def _mla_prep_bwd(dq, dk, dv, z, gq, gkv, wuq, wukv, c_t, s1_t, s2_t, l):
    s = z.shape[0]
    t = _tile(s)
    hq = MLA_HEADS * HEAD_BLK
    lay = _lay(l)

    def body(dq_ref, dk_ref, dv_ref, cq_ref, ckv_ref, gq_ref, gkv_ref, wuq_ref, wukv_ref, c_ref, s1_ref, s2_ref,
             dcq_ref, dckv_ref, dkr_ref, dqu_ref, dkvp_ref, dgq_ref, dgkv_ref):
        @pl.when(pl.program_id(0) == 0)
        def _():
            dgq_ref[...] = jnp.zeros_like(dgq_ref)
            dgkv_ref[...] = jnp.zeros_like(dgkv_ref)

        ct, s1, s2 = c_ref[...], s1_ref[...], s2_ref[...]
        dk_sum = jnp.zeros((t, HEAD_BLK), F32)
        for h in range(MLA_HEADS):
            blk = slice(h * HEAD_BLK, (h + 1) * HEAD_BLK)
            dqu_ref[:, blk] = _unrope(dq_ref[:, blk] * ATT_SCALE, ct, s1, s2).astype(dqu_ref.dtype)
            dkh = dk_ref[:, blk]
            dk_sum = dk_sum + dkh
            dkvp_ref[:, blk] = dkh.astype(dkvp_ref.dtype)
        dkvp_ref[:, hq:] = dv_ref[...]
        lane = _lanes((t, HEAD_BLK))
        rope_lanes = (lane >= KR_LANE0) & (lane < KR_LANE0 + QK_ROPE)
        dkr_ref[...] = _unrope(jnp.where(rope_lanes, dk_sum, 0.0), ct, s1, s2).astype(dkr_ref.dtype)

        def norm_bwd(c_in, g, dn_out, dc_ref, dg_ref):
            rs = lax.rsqrt(jnp.mean(c_in * c_in, axis=-1, keepdims=True) + EPS)
            n = c_in * rs
            dg_ref[...] += jnp.sum(dn_out * n, axis=0, keepdims=True)
            dn = dn_out * g
            dc_ref[...] = (rs * (dn - n * jnp.mean(dn * n, axis=-1, keepdims=True))).astype(dc_ref.dtype)

        norm_bwd(cq_ref[...], gq_ref[...], _dot_nt(dqu_ref[...], wuq_ref[...]), dcq_ref, dgq_ref)
        norm_bwd(ckv_ref[...], gkv_ref[...], _dot_nt(dkvp_ref[...], wukv_ref[...]), dckv_ref, dgkv_ref)

    full = lambda shp: pl.BlockSpec(shp, lambda i: (0, 0))
    tab = pl.BlockSpec((t, HEAD_BLK), lambda i: (i, 0))
    row = lambda w: pl.BlockSpec((t, w), lambda i: (i, 0))
    return pl.pallas_call(
        body, name="mla_prep_bwd", grid=(s // t,),
        in_specs=[row(hq), row(hq), row(D_MLA),
                  pl.BlockSpec((t, Q_RANK), lambda i: (i, 2)), pl.BlockSpec((t, KV_RANK), lambda i: (i, 6)),
                  lay((1, Q_RANK)), lay((1, KV_RANK)), lay((Q_RANK, hq)), lay((KV_RANK, hq + D_MLA)),
                  tab, tab, tab],
        out_specs=[row(Q_RANK), row(KV_RANK), row(HEAD_BLK), row(hq), row(hq + D_MLA),
                   full((1, Q_RANK)), full((1, KV_RANK))],
        out_shape=[jax.ShapeDtypeStruct((s, Q_RANK), MXU_DTYPE), jax.ShapeDtypeStruct((s, KV_RANK), MXU_DTYPE),
                   jax.ShapeDtypeStruct((s, HEAD_BLK), MXU_DTYPE), jax.ShapeDtypeStruct((s, hq), MXU_DTYPE),
                   jax.ShapeDtypeStruct((s, hq + D_MLA), MXU_DTYPE),
                   jax.ShapeDtypeStruct((1, Q_RANK), F32), jax.ShapeDtypeStruct((1, KV_RANK), F32)],
        compiler_params=_cp(("arbitrary",), 48),
    )(dq, dk, dv, z, z, gq, gkv, wuq, wukv, c_t, s1_t, s2_t)


def _outproj_fwd(x, ya, yb, yc, w, l):
    s = x.shape[0]
    t = _tile(s)
    lay = _lay(l)

    def body(x_ref, ya_ref, yb_ref, yc_ref, wa_ref, wb_ref, wc_ref, o_ref):
        o_ref[...] = (x_ref[...] + _dot(ya_ref[...], wa_ref[...]) + _dot(yb_ref[...], wb_ref[...])
                      + _dot(yc_ref[...], wc_ref[...]))

    row = lambda w_: pl.BlockSpec((t, w_), lambda i: (i, 0))
    return pl.pallas_call(
        body, name="outproj_fwd", grid=(s // t,),
        in_specs=[row(D_MODEL), row(D_LRU), row(D_MLA), row(D_POOL),
                  lay((D_LRU, D_MODEL), 0), lay((D_MLA, D_MODEL), 1), lay((D_POOL, D_MODEL), 3)],
        out_specs=row(D_MODEL),
        out_shape=jax.ShapeDtypeStruct((s, D_MODEL), F32),
        compiler_params=_cp(("parallel",), 40),
    )(x, ya, yb, yc, w, w, w)


def _outproj_bwd(dx, ya, yb, yc, w, l):
    s = dx.shape[0]
    t = _tile(s)

    def body(dx_ref, ya_ref, yb_ref, yc_ref, w_ref, dy_ref, dw_ref):
        @pl.when(pl.program_id(0) == 0)
        def _():
            dw_ref[...] = jnp.zeros_like(dw_ref)

        dxv = dx_ref[...].astype(MXU_DTYPE)
        dy_ref[...] = _dot_nt(dxv, w_ref[...])
        dw_ref[0:D_LRU, :] += _dot_tn(ya_ref[...], dxv)
        dw_ref[D_LRU:D_LRU + D_MLA, :] += _dot_tn(yb_ref[...], dxv)
        dw_ref[D_LRU + D_MLA:D_MIX, :] += _dot_tn(yc_ref[...], dxv)

    row = lambda w_: pl.BlockSpec((t, w_), lambda i: (i, 0))
    return pl.pallas_call(
        body, name="outproj_bwd", grid=(s // t,),
        in_specs=[row(D_MODEL), row(D_LRU), row(D_MLA), row(D_POOL), _lay(l)((D_MIX, D_MODEL))],
        out_specs=[row(D_MIX), pl.BlockSpec((D_MIX, D_MODEL), lambda i: (0, 0))],
        out_shape=[jax.ShapeDtypeStruct((s, D_MIX), F32), jax.ShapeDtypeStruct((D_MIX, D_MODEL), F32)],
        compiler_params=_cp(("arbitrary",), 48),
    )(dx, ya, yb, yc, w)


def _mm_tn(a, b, name):
    s, k1 = a.shape
    n = b.shape[1]
    t = _tile(s)

    def body(a_ref, b_ref, o_ref):
        @pl.when(pl.program_id(0) == 0)
        def _():
            o_ref[...] = jnp.zeros_like(o_ref)

        o_ref[...] += _dot_tn(a_ref[...], b_ref[...])

    return pl.pallas_call(
        body, name=name, grid=(s // t,),
        in_specs=[pl.BlockSpec((t, k1), lambda i: (i, 0)), pl.BlockSpec((t, n), lambda i: (i, 0))],
        out_specs=pl.BlockSpec((k1, n), lambda i: (0, 0)),
        out_shape=jax.ShapeDtypeStruct((k1, n), F32),
        compiler_params=_cp(("arbitrary",), 56),
    )(a, b)


def _inproj_bwd(dz, w, x, g, dxn, l):
    s = x.shape[0]
    t = _tile(s)
    lay = _lay(l)

    def body(dz_ref, w_ref, x_ref, g_ref, dxn_ref, dx_ref, dg_ref):
        @pl.when(pl.program_id(0) == 0)
        def _():
            dg_ref[...] = jnp.zeros_like(dg_ref)

        dh = _dot_nt(dz_ref[...], w_ref[...])
        xv = x_ref[...]
        rs = lax.rsqrt(jnp.mean(xv * xv, axis=-1, keepdims=True) + EPS)
        n = xv * rs
        dg_ref[...] += jnp.sum(dh * n, axis=0, keepdims=True)
        dn = dh * g_ref[...]
        dx_ref[...] = dxn_ref[...] + rs * (dn - n * jnp.mean(dn * n, axis=-1, keepdims=True))

    row = lambda w_: pl.BlockSpec((t, w_), lambda i: (i, 0))
    return pl.pallas_call(
        body, name="inproj_bwd", grid=(s // t,),
        in_specs=[row(D_INP), lay((D_MODEL, D_INP)), row(D_MODEL), lay((1, D_MODEL)), row(D_MODEL)],
        out_specs=[row(D_MODEL), pl.BlockSpec((1, D_MODEL), lambda i: (0, 0))],
        out_shape=[jax.ShapeDtypeStruct((s, D_MODEL), F32), jax.ShapeDtypeStruct((1, D_MODEL), F32)],
        compiler_params=_cp(("arbitrary",), 48),
    )(dz, w, x, g, dxn)


def _loss_head(x, g, tgt):
    s = x.shape[0]
    t = _tile(s)

    def body(x_ref, g_ref, t_ref, dx_ref, loss_ref, dg_ref):
        @pl.when(pl.program_id(0) == 0)
        def _():
            loss_ref[...] = jnp.zeros_like(loss_ref)
            dg_ref[...] = jnp.zeros_like(dg_ref)

        xv = x_ref[...]
        rs = lax.rsqrt(jnp.mean(xv * xv, axis=-1, keepdims=True) + EPS)
        n = xv * rs
        gv = g_ref[...]
        e = n * gv - t_ref[...]
        loss_ref[...] += 0.5 * jnp.sum(jnp.mean(e * e, axis=-1, keepdims=True))
        dyf = e * (1.0 / D_MODEL)
        dg_ref[...] += jnp.sum(dyf * n, axis=0, keepdims=True)
        dn = dyf * gv
        dx_ref[...] = rs * (dn - n * jnp.mean(dn * n, axis=-1, keepdims=True))

    row = pl.BlockSpec((t, D_MODEL), lambda i: (i, 0))
    vec = pl.BlockSpec((1, D_MODEL), lambda i: (0, 0))
    return pl.pallas_call(
        body, name="loss_head", grid=(s // t,),
        in_specs=[row, vec, row],
        out_specs=[row, pl.BlockSpec((1, LANES), lambda i: (0, 0)), vec],
        out_shape=[jax.ShapeDtypeStruct((s, D_MODEL), F32), jax.ShapeDtypeStruct((1, LANES), F32),
                   jax.ShapeDtypeStruct((1, D_MODEL), F32)],
        compiler_params=_cp(("arbitrary",), 32),
    )(x, g, tgt)


def _block_diag(w):
    n, h, d, _ = w.shape
    return jnp.einsum('lhij,hk->lhikj', w, jnp.eye(h, dtype=w.dtype)).reshape(n, h * d, h * d)


def _diag_blocks(wfull, h):
    d = wfull.shape[-1] // h
    return jnp.stack([wfull[:, i * d:(i + 1) * d, i * d:(i + 1) * d] for i in range(h)], axis=1)


REF_TO_PERM = np.concatenate([np.arange(0, 1152), np.arange(1536, 1792),
                              np.arange(2304 + KR_LANE0, 2304 + KR_LANE0 + QK_ROPE),
                              np.arange(1152, 1536), np.arange(1792, 2304)])
N_SHARD = 4
W_IN_SHARD = D_IN // N_SHARD


def _w_in_runs():
    ref_of_perm = -np.ones(D_INP, np.int64)
    ref_of_perm[REF_TO_PERM] = np.arange(D_IN)
    perm_runs, p = [], 0
    while p < D_INP:
        r, q = ref_of_perm[p], p + 1
        if r < 0:
            while q < D_INP and ref_of_perm[q] < 0:
                q += 1
            perm_runs.append((None, q - p, 0))
        else:
            while (q < D_INP and ref_of_perm[q] == ref_of_perm[q - 1] + 1
                   and ref_of_perm[q] // W_IN_SHARD == r // W_IN_SHARD):
                q += 1
            perm_runs.append((int(r // W_IN_SHARD), int(r % W_IN_SHARD), int(r % W_IN_SHARD + q - p)))
        p = q
    shard_runs = []
    for s in range(N_SHARD):
        cols = REF_TO_PERM[s * W_IN_SHARD:(s + 1) * W_IN_SHARD]
        runs, a = [], 0
        for b in range(1, W_IN_SHARD + 1):
            if b == W_IN_SHARD or cols[b] != cols[b - 1] + 1:
                runs.append((int(cols[a]), int(cols[b - 1]) + 1))
                a = b
        shard_runs.append(runs)
    return perm_runs, shard_runs


def _permute_w_in(shards):
    perm_runs, _ = _w_in_runs()
    lead = shards[0].shape[:-1]
    parts = [jnp.zeros(lead + (a,), shards[0].dtype) if s is None else shards[s][..., a:b] for s, a, b in perm_runs]
    return jnp.concatenate(parts, axis=-1)


def _w_in_shard(wp, s):
    _, shard_runs = _w_in_runs()
    return jnp.concatenate([wp[..., a:b] for a, b in shard_runs[s]], axis=-1)


def _pad_w_uq(w):
    w4 = w.reshape(w.shape[:2] + (MLA_HEADS, QK_NOPE + QK_ROPE))
    return jnp.pad(w4, ((0, 0),) * 3 + ((0, HEAD_BLK - QK_NOPE - QK_ROPE),)).reshape(w.shape[:2] + (-1,))


def _unpad_w_uq(w):
    return w.reshape(w.shape[:2] + (MLA_HEADS, HEAD_BLK))[..., :QK_NOPE + QK_ROPE].reshape(w.shape[:2] + (-1,))


def _pad_w_ukv(w):
    w4 = w.reshape(w.shape[:2] + (MLA_HEADS, QK_NOPE + V_DIM))
    kpart = jnp.pad(w4[..., :QK_NOPE], ((0, 0),) * 3 + ((0, HEAD_BLK - QK_NOPE),)).reshape(w.shape[:2] + (-1,))
    return jnp.concatenate([kpart, w4[..., QK_NOPE:].reshape(w.shape[:2] + (-1,))], axis=2)


def _unpad_w_ukv(w):
    hq = MLA_HEADS * HEAD_BLK
    kpart = w[..., :hq].reshape(w.shape[:2] + (MLA_HEADS, HEAD_BLK))[..., :QK_NOPE]
    vpart = w[..., hq:].reshape(w.shape[:2] + (MLA_HEADS, V_DIM))
    return jnp.concatenate([kpart, vpart], axis=3).reshape(w.shape[:2] + (-1,))


def _local_step(x, tgt, w):
    s = x.shape[0]
    tabs = _rope_tables(s)
    vec = lambda a: a[:, None, :]
    mxu = lambda a: a.astype(MXU_DTYPE)
    p = dict(g=vec(w['norm_g']), w_in=mxu(w['w_in']), cw=w['conv_w'], cb=vec(w['conv_b']),
             wr=mxu(_block_diag(w['w_rg'])), br=vec(w['b_rg']), wi=mxu(_block_diag(w['w_ig'])), bi=vec(w['b_ig']),
             lam=vec(w['lru_lambda']), gq=vec(w['q_norm_g']), gkv=vec(w['kv_norm_g']),
             wuq=mxu(_pad_w_uq(w['w_uq'])), wukv=mxu(_pad_w_ukv(w['w_ukv'])),
             wp=mxu(_block_diag(w['w_pool'])), ps=vec(w['pool_scale']), wout=mxu(w['w_out']))
    lru = lambda l: (p['cw'], p['cb'], p['wr'], p['br'], p['wi'], p['bi'], p['lam'], l)
    mla = lambda l: (p['gq'], p['gkv'], p['wuq'], p['wukv'], *tabs, l)

    saved = []
    for l in range(DEPTH):
        h, z = _inproj_fwd(x, p['g'], p['w_in'], l)
        ya, hs = _lru_fwd(z, *lru(l))
        yc = _pool_fwd(z, p['wp'], p['ps'], l)
        q, k, v, qn, kvn = _mla_prep_fwd(z, *mla(l))
        o, yb, lse = _flash_fwd(q, k, v, z)
        saved.append(dict(x=x, h=h, z=z, hs=hs, ya=ya, yb=yb, yc=yc, q=q, k=k, v=v, qn=qn, kvn=kvn, o=o, lse=lse))
        x = _outproj_fwd(x, ya, yb, yc, p['wout'], l)

    dx, loss, dgf = _loss_head(x, w['final_norm_g'][None], tgt)
    per_layer = {n: [None] * DEPTH for n in WEIGHT_NAMES if n != 'final_norm_g'}
    for l in reversed(range(DEPTH)):
        sv = saved[l]
        dy, per_layer['w_out'][l] = _outproj_bwd(dx, sv['ya'], sv['yb'], sv['yc'], p['wout'], l)
        dza, dga, *lru_grads = _lru_bwd(sv['z'], sv['hs'], dy, *lru(l))
        for n, g in zip(('w_rg', 'w_ig', 'conv_w', 'conv_b', 'b_rg', 'b_ig', 'lru_lambda'), lru_grads):
            per_layer[n][l] = g
        dzc, dgc, per_layer['w_pool'][l], per_layer['pool_scale'][l] = _pool_bwd(sv['z'], dy, p['wp'], p['ps'], l)
        do, dgb, delta = _attn_bwd_prep(dy, sv['o'], sv['z'])
        dq, dk, dv = _flash_bwd(sv['q'], sv['k'], sv['v'], do, sv['lse'], delta)
        dcq, dckv, dkr, dqu, dkvp, per_layer['q_norm_g'][l], per_layer['kv_norm_g'][l] = _mla_prep_bwd(
            dq, dk, dv, sv['z'], *mla(l))
        dz = jnp.concatenate([dza, dga, dcq, dgb, dckv, dzc, dgc, dkr], axis=1)
        per_layer['w_in'][l] = _mm_tn(sv['h'], dz, "dwin")
        per_layer['w_uq'][l] = _mm_tn(sv['qn'], dqu, "dwuq")
        per_layer['w_ukv'][l] = _mm_tn(sv['kvn'], dkvp, "dwukv")
        dx, per_layer['norm_g'][l] = _inproj_bwd(dz, p['w_in'], sv['x'], p['g'], dx, l)
    grads = {n: jnp.stack(g) for n, g in per_layer.items()}
    for n in ('norm_g', 'conv_b', 'b_rg', 'b_ig', 'lru_lambda', 'q_norm_g', 'kv_norm_g', 'pool_scale'):
        grads[n] = grads[n][:, 0, :]
    grads['w_rg'] = _diag_blocks(grads['w_rg'], LRU_HEADS)
    grads['w_ig'] = _diag_blocks(grads['w_ig'], LRU_HEADS)
    grads['w_pool'] = _diag_blocks(grads['w_pool'], len(POOL_WINDOWS))
    grads['w_uq'] = _unpad_w_uq(grads['w_uq'])
    grads['w_ukv'] = _unpad_w_ukv(grads['w_ukv'])
    grads['final_norm_g'] = dgf[0]
    return loss[0, 0], dx, grads


WIRE_DTYPE = jnp.bfloat16
MESH_IDS = pl.DeviceIdType.MESH
_HBM = pl.BlockSpec(memory_space=pltpu.HBM)


def _coords():
    return lax.axis_index("x"), lax.axis_index("y"), lax.axis_index("c")


def _comm_call(body, name, arrays, out_shapes, copies_per_array):
    n = len(arrays)
    return pl.pallas_call(
        body, name=name, out_shape=out_shapes, in_specs=[_HBM] * n, out_specs=[_HBM] * n,
        scratch_shapes=[pltpu.SemaphoreType.DMA((n, copies_per_array)), pltpu.SemaphoreType.DMA((n, copies_per_array)),
                        pltpu.SemaphoreType.DMA((n,))],
    )(*arrays)


def _all_gather8(blocks, name):
    n = len(blocks)
    every = range(n)

    def body(*refs):
        x_refs, out_refs = refs[:n], refs[n:2 * n]
        send_sems, recv_sems, local_sems = refs[2 * n:]
        x, y, c = _coords()
        me, sibling = (x, y, c), (x, y, 1 - c)
        chips = [(1 - x, y), (x, 1 - y), (1 - x, 1 - y)]

        def slot(t, px, py, pc):
            return out_refs[t].at[4 * px + 2 * py + pc]

        def copy(t, k, block, to, own=False):
            return pltpu.make_async_remote_copy(
                src_ref=x_refs[t] if own else slot(t, *block), dst_ref=slot(t, *block),
                send_sem=send_sems.at[t, k], recv_sem=recv_sems.at[t, k], device_id=to, device_id_type=MESH_IDS)

        mine = [pltpu.make_async_copy(x_refs[t], slot(t, *me), local_sems.at[t]) for t in every]
        for cp in mine:
            cp.start()
        first = [copy(t, 0, me, sibling, own=True) for t in every]
        first += [copy(t, 1 + j, me, (*chip, c), own=True) for j, chip in enumerate(chips) for t in every]
        for cp in first:
            cp.start()
        passed = [[copy(t, 4 + j, (*chip, c), sibling) for t in every] for j, chip in enumerate(chips)]
        for j, chip in enumerate(chips):
            for t in every:
                copy(t, 1 + j, (*chip, c), me).wait_recv()
                passed[j][t].start()
        for t in every:
            copy(t, 0, sibling, me).wait_recv()
        for j, chip in enumerate(chips):
            for t in every:
                copy(t, 4 + j, (*chip, 1 - c), me).wait_recv()
        for cp in first + [cp for group in passed for cp in group]:
            cp.wait_send()
        for cp in mine:
            cp.wait()

    outs = [jax.ShapeDtypeStruct((N_DEV,) + b.shape, b.dtype) for b in blocks]
    return _comm_call(body, name, blocks, outs, 7)


def _sibling_halves(arrs, name):
    n = len(arrs)

    def body(*refs):
        a_refs, out_refs = refs[:n], refs[n:2 * n]
        send_sems, recv_sems, local_sems = refs[2 * n:]
        x, y, c = _coords()
        own = [pltpu.make_async_copy(a_refs[t].at[c], out_refs[t].at[0], local_sems.at[t]) for t in range(n)]
        sent = [pltpu.make_async_remote_copy(
            src_ref=a_refs[t].at[1 - c], dst_ref=out_refs[t].at[1], send_sem=send_sems.at[t, 0],
            recv_sem=recv_sems.at[t, 0], device_id=(x, y, 1 - c), device_id_type=MESH_IDS) for t in range(n)]
        for cp in own + sent:
            cp.start()
        for cp in sent + own:
            cp.wait()

    return _comm_call(body, name, arrs, [jax.ShapeDtypeStruct(a.shape, a.dtype) for a in arrs], 1)


def _chip_exchange(arrs, name):
    n = len(arrs)

    def body(*refs):
        a_refs, out_refs = refs[:n], refs[n:2 * n]
        send_sems, recv_sems, local_sems = refs[2 * n:]
        x, y, c = _coords()
        s_me = 2 * x + y
        copies = [pltpu.make_async_copy(a_refs[t].at[s_me], out_refs[t].at[s_me], local_sems.at[t]) for t in range(n)]
        for j, (cx, cy) in enumerate([(1 - x, y), (x, 1 - y), (1 - x, 1 - y)]):
            copies += [pltpu.make_async_remote_copy(
                src_ref=a_refs[t].at[2 * cx + cy], dst_ref=out_refs[t].at[s_me], send_sem=send_sems.at[t, j],
                recv_sem=recv_sems.at[t, j], device_id=(cx, cy, c), device_id_type=MESH_IDS) for t in range(n)]
        for cp in copies:
            cp.start()
        for cp in copies:
            cp.wait()

    return _comm_call(body, name, arrs, [jax.ShapeDtypeStruct(a.shape, a.dtype) for a in arrs], 3)


def _sibling_gather(arrs, name):
    n = len(arrs)

    def body(*refs):
        a_refs, out_refs = refs[:n], refs[n:2 * n]
        send_sems, recv_sems, local_sems = refs[2 * n:]
        x, y, c = _coords()
        own = [pltpu.make_async_copy(a_refs[t], out_refs[t].at[c], local_sems.at[t]) for t in range(n)]
        sent = [pltpu.make_async_remote_copy(
            src_ref=a_refs[t], dst_ref=out_refs[t].at[c], send_sem=send_sems.at[t, 0], recv_sem=recv_sems.at[t, 0],
            device_id=(x, y, 1 - c), device_id_type=MESH_IDS) for t in range(n)]
        for cp in own + sent:
            cp.start()
        for cp in sent + own:
            cp.wait()

    return _comm_call(body, name, arrs, [jax.ShapeDtypeStruct((2,) + a.shape, a.dtype) for a in arrs], 1)


def _sum_leading(arrs, out_dtype, steps, name):
    n = len(arrs)

    def body(*refs):
        for i_ref, o_ref in zip(refs[:n], refs[n:]):
            acc = i_ref[0].astype(F32)
            for k in range(1, i_ref.shape[0]):
                acc = acc + i_ref[k].astype(F32)
            o_ref[...] = acc.astype(o_ref.dtype)

    return pl.pallas_call(
        body, name=name, grid=(steps,),
        in_specs=[pl.BlockSpec((a.shape[0], a.shape[1] // steps, a.shape[2]), lambda i: (0, i, 0)) for a in arrs],
        out_specs=[pl.BlockSpec((a.shape[1] // steps, a.shape[2]), lambda i: (i, 0)) for a in arrs],
        out_shape=[jax.ShapeDtypeStruct(a.shape[1:], out_dtype) for a in arrs],
        compiler_params=_cp(("parallel",), 40),
    )(*arrs)


def _adamw_update(w_ref, g_ref, m_ref, v_ref, d_ref, mo_ref, vo_ref):
    gv = g_ref[...]
    mn = ADAM_B1 * m_ref[...] + (1.0 - ADAM_B1) * gv
    vn = ADAM_B2 * v_ref[...] + (1.0 - ADAM_B2) * (gv * gv)
    mo_ref[...] = mn
    vo_ref[...] = vn
    m_hat = mn / (1.0 - ADAM_B1 ** ADAM_STEP)
    v_hat = vn / (1.0 - ADAM_B2 ** ADAM_STEP)
    d_ref[...] = (-ADAM_LR) * (m_hat / (jnp.sqrt(v_hat) + ADAM_EPS) + ADAM_WD * w_ref[...])


def _adamw(w, g, m, v, name):
    r, cdim = w.shape
    tr = math.gcd(r, 512)

    def body(*refs):
        _adamw_update(*refs)

    blk = pl.BlockSpec((tr, cdim), lambda i: (i, 0))
    return pl.pallas_call(
        body, name=name, grid=(r // tr,),
        in_specs=[blk] * 4, out_specs=[blk] * 3,
        out_shape=[jax.ShapeDtypeStruct((r, cdim), F32)] * 3,
        compiler_params=_cp(("parallel",), 40),
    )(w, g, m, v)


def _adamw_small(ws, gs, ms, vs, name):
    n = len(ws)

    def body(*refs):
        ins, outs = refs[:4 * n], refs[4 * n:]
        for t in range(n):
            _adamw_update(ins[t], ins[n + t], ins[2 * n + t], ins[3 * n + t], outs[t], outs[n + t], outs[2 * n + t])

    shapes = [jax.ShapeDtypeStruct(w.shape, F32) for w in ws]
    outs = pl.pallas_call(body, name=name, out_shape=shapes * 3)(*ws, *gs, *ms, *vs)
    return outs[:n], outs[n:2 * n], outs[2 * n:]


HALF = DEPTH // 2
BIG = ['w_in', 'w_uq', 'w_ukv', 'w_out']
SHARD_AXIS = {'w_in': 2, 'conv_w': 2, 'w_uq': 2, 'w_ukv': 2, 'w_out': 1}
FULL_SHAPE = {'w_in': (DEPTH, D_MODEL, D_IN), 'conv_w': (DEPTH, CONV_WIDTH, D_LRU),
              'w_uq': (DEPTH, Q_RANK, MLA_HEADS * (QK_NOPE + QK_ROPE)),
              'w_ukv': (DEPTH, KV_RANK, MLA_HEADS * (QK_NOPE + V_DIM)), 'w_out': (DEPTH, D_MIX, D_MODEL)}


def _shard_shape(n):
    shp = list(FULL_SHAPE[n])
    shp[SHARD_AXIS[n]] //= N_SHARD
    return tuple(shp)


def _rows_view(a, lead=0):
    return a.reshape(a.shape[:lead] + (-1, a.shape[-1]))


def _gather_weights(local):
    c = lax.axis_index("c")
    names = BIG + ['conv_w']
    halves = [lax.dynamic_slice_in_dim(local[n], HALF * c, HALF, axis=0) for n in names]
    halves = [h.astype(WIRE_DTYPE) if n in BIG else h for n, h in zip(names, halves)]
    got = _all_gather8(halves, "gather_weights")
    full = {}
    for n, g in zip(names, got):
        g = g.reshape((N_SHARD, DEPTH) + g.shape[2:])
        if n == 'w_in':
            full[n] = _permute_w_in([g[s] for s in range(N_SHARD)])
        else:
            full[n] = jnp.moveaxis(g, 0, SHARD_AXIS[n]).reshape(FULL_SHAPE[n])
    return full


def _shard_blocks(g, n):
    width = _shard_shape(n)[SHARD_AXIS[n]]

    def block(h, s):
        part = g[HALF * h:HALF * (h + 1)]
        if n == 'w_in':
            part = _w_in_shard(part, s)
        else:
            part = lax.slice_in_dim(part, s * width, (s + 1) * width, axis=SHARD_AXIS[n])
        return _rows_view(part)

    return jnp.stack([jnp.stack([block(h, s) for s in range(N_SHARD)]) for h in range(2)]).astype(WIRE_DTYPE)


SUM_STEPS = 8


def _reduce_big(grads):
    contrib = [_shard_blocks(grads[n], n) for n in BIG]
    pair = _sibling_halves(contrib, "pair_exchange_big")
    pair_sum = _sum_leading([_rows_view(p, 1) for p in pair], WIRE_DTYPE, SUM_STEPS, "pair_sum_big")
    to_chips = [p.reshape(c.shape[1:]) for p, c in zip(pair_sum, contrib)]
    from_chips = _chip_exchange(to_chips, "chip_exchange_big")
    mine = _sum_leading(from_chips, F32, SUM_STEPS, "chip_sum_big")
    both = _sibling_gather(mine, "sibling_big")
    return {n: b.reshape(_shard_shape(n)) for n, b in zip(BIG, both)}


SMALL = REPLICATED + ['conv_w']


def _reduce_small(grads):
    views = [_rows_view(jnp.atleast_2d(grads[n])) for n in SMALL]
    sums = _sum_leading(_all_gather8(views, "gather_small"), F32, 1, "sum_small")
    return {n: s.reshape(grads[n].shape) for n, s in zip(SMALL, sums)}


def kernel(x, norm_g, w_in, conv_w, conv_b, w_rg, b_rg, w_ig, b_ig, lru_lambda, q_norm_g, w_uq, kv_norm_g, w_ukv, w_pool, pool_scale, w_out, final_norm_g, loss_target, m_norm_g, m_w_in, m_conv_w, m_conv_b, m_w_rg, m_b_rg, m_w_ig, m_b_ig, m_lru_lambda, m_q_norm_g, m_w_uq, m_kv_norm_g, m_w_ukv, m_w_pool, m_pool_scale, m_w_out, m_final_norm_g, v_norm_g, v_w_in, v_conv_w, v_conv_b, v_w_rg, v_b_rg, v_w_ig, v_b_ig, v_lru_lambda, v_q_norm_g, v_w_uq, v_kv_norm_g, v_w_ukv, v_w_pool, v_pool_scale, v_w_out, v_final_norm_g):
    w_loc = dict(zip(WEIGHT_NAMES, (norm_g, w_in, conv_w, conv_b, w_rg, b_rg, w_ig, b_ig, lru_lambda, q_norm_g, w_uq,
                                    kv_norm_g, w_ukv, w_pool, pool_scale, w_out, final_norm_g)))
    m_loc = dict(zip(WEIGHT_NAMES, (m_norm_g, m_w_in, m_conv_w, m_conv_b, m_w_rg, m_b_rg, m_w_ig, m_b_ig, m_lru_lambda,
                                    m_q_norm_g, m_w_uq, m_kv_norm_g, m_w_ukv, m_w_pool, m_pool_scale, m_w_out,
                                    m_final_norm_g)))
    v_loc = dict(zip(WEIGHT_NAMES, (v_norm_g, v_w_in, v_conv_w, v_conv_b, v_w_rg, v_b_rg, v_w_ig, v_b_ig, v_lru_lambda,
                                    v_q_norm_g, v_w_uq, v_kv_norm_g, v_w_ukv, v_w_pool, v_pool_scale, v_w_out,
                                    v_final_norm_g)))
    w_full = dict(w_loc)
    w_full.update(_gather_weights(w_loc))
    loss_local, dx, g_local = _local_step(x[0], loss_target[0], w_full)
    loss = lax.psum(loss_local, ("x", "y", "c"))

    grads = _reduce_big(g_local)
    g_small = _reduce_small(g_local)
    shard = 2 * lax.axis_index("x") + lax.axis_index("y")
    width = D_LRU // N_SHARD
    grads['conv_w'] = lax.dynamic_slice_in_dim(g_small['conv_w'], shard * width, width, axis=2)
    for n in REPLICATED:
        grads[n] = g_small[n]

    delta, new_m, new_v = {}, {}, {}
    for n in BIG:
        d, mo, vo = _adamw(*[_rows_view(t[n]) for t in (w_loc, grads, m_loc, v_loc)], "adamw_" + n)
        delta[n], new_m[n], new_v[n] = (a.reshape(w_loc[n].shape) for a in (d, mo, vo))
    small = [[_rows_view(jnp.atleast_2d(t[n])) for n in SMALL] for t in (w_loc, grads, m_loc, v_loc)]
    for tree, outs in zip((delta, new_m, new_v), _adamw_small(*small, "adamw_small")):
        tree.update({n: a.reshape(w_loc[n].shape) for n, a in zip(SMALL, outs)})

    return (loss, dx[None], *[grads[n] for n in WEIGHT_NAMES], *[delta[n] for n in WEIGHT_NAMES],
            *[new_m[n] for n in WEIGHT_NAMES], *[new_v[n] for n in WEIGHT_NAMES])
```

```python
import functools
import math

import jax
import jax.numpy as jnp
import numpy as np
from jax import lax
from jax.experimental import pallas as pl
from jax.experimental.pallas import tpu as pltpu

F32 = jnp.float32
MXU_DTYPE = jnp.bfloat16

D_MODEL = 1024
DEPTH = 4
EPS = 1e-6
D_LRU = 384
LRU_HEADS = 6
CONV_WIDTH = 4
LRU_C = 8.0
MLA_HEADS = 6
QK_NOPE = 64
QK_ROPE = 32
V_DIM = 64
D_MLA = MLA_HEADS * V_DIM
Q_RANK = 384
KV_RANK = 256
ROPE_BASE = 10000.0
POOL_WINDOWS = (2, 4, 8, 16)
D_POOL = 256
D_MIX = D_LRU + D_MLA + D_POOL
D_IN = 2336
ATT_SCALE = (QK_NOPE + QK_ROPE) ** -0.5
LOG2_E = 1.4426950408889634
LN_2 = 0.6931471805599453
Q_PRESCALE = ATT_SCALE * LOG2_E

ADAM_LR = 0.001
ADAM_B1 = 0.9
ADAM_B2 = 0.999
ADAM_EPS = 1e-08
ADAM_WD = 0.01
ADAM_STEP = 10

LANES = 128
SUBLANES = 8
V7X_VMEM_BYTES = 64 << 20
N_DEV = 8

D_INP = 2432
KR_LANE0 = 64
HEAD_BLK = 128
N_PAIR = MLA_HEADS // 2

WEIGHT_NAMES = ['norm_g', 'w_in', 'conv_w', 'conv_b', 'w_rg', 'b_rg', 'w_ig', 'b_ig', 'lru_lambda', 'q_norm_g',
                'w_uq', 'kv_norm_g', 'w_ukv', 'w_pool', 'pool_scale', 'w_out', 'final_norm_g']
SHARDED = ['w_in', 'conv_w', 'w_uq', 'w_ukv', 'w_out']
REPLICATED = [n for n in WEIGHT_NAMES if n not in SHARDED]


def _cp(sem, vmem_mb=None):
    return pltpu.CompilerParams(dimension_semantics=sem,
                                vmem_limit_bytes=None if vmem_mb is None else vmem_mb << 20)


def _dot(a, b):
    return jnp.dot(a.astype(MXU_DTYPE), b.astype(MXU_DTYPE), preferred_element_type=F32)


def _dot_nt(a, b):
    return lax.dot_general(a.astype(MXU_DTYPE), b.astype(MXU_DTYPE), (((1,), (1,)), ((), ())),
                           preferred_element_type=F32)


def _dot_tn(a, b):
    return lax.dot_general(a.astype(MXU_DTYPE), b.astype(MXU_DTYPE), (((0,), (0,)), ((), ())),
                           preferred_element_type=F32)


def _sig(x):
    return 1.0 / (1.0 + jnp.exp(-x))


def _down(x, k):
    return pltpu.roll(x, k, 0)


def _up(x, k):
    return pltpu.roll(x, x.shape[0] - k, 0)


def _rows(shape):
    return lax.broadcasted_iota(jnp.int32, shape, 0)


def _lanes(shape):
    return lax.broadcasted_iota(jnp.int32, shape, 1)


def _tile(s):
    return min(512, s)


def _lay(l):
    return lambda shp, blk=0: pl.BlockSpec((None,) + shp, lambda *_: (l, blk, 0))


def _inproj_fwd(x, g, w, l):
    s = x.shape[0]
    t = _tile(s)
    lay = _lay(l)

    def body(x_ref, g_ref, w_ref, h_ref, z_ref):
        xv = x_ref[...]
        rs = lax.rsqrt(jnp.mean(xv * xv, axis=-1, keepdims=True) + EPS)
        h = (xv * rs * g_ref[...]).astype(MXU_DTYPE)
        h_ref[...] = h
        z_ref[...] = jnp.dot(h, w_ref[...], preferred_element_type=F32)

    return pl.pallas_call(
        body, name="inproj_fwd", grid=(s // t,),
        in_specs=[pl.BlockSpec((t, D_MODEL), lambda i: (i, 0)),
                  lay((1, D_MODEL)), lay((D_MODEL, D_INP))],
        out_specs=[pl.BlockSpec((t, D_MODEL), lambda i: (i, 0)),
                   pl.BlockSpec((t, D_INP), lambda i: (i, 0))],
        out_shape=[jax.ShapeDtypeStruct((s, D_MODEL), MXU_DTYPE), jax.ShapeDtypeStruct((s, D_INP), F32)],
        compiler_params=_cp(("parallel",), 40),
    )(x, g, w)


def _lru_gates(za, halo, cw_ref, cb_ref, wr_ref, br_ref, wi_ref, bi_ref, lam_ref):
    t = za.shape[0]
    ext = jnp.concatenate([halo, za], axis=0)
    sh = [za] + [_down(ext, j)[SUBLANES:SUBLANES + t] for j in (1, 2, 3)]
    xa = cb_ref[...] + cw_ref[3:4, :] * sh[0] + cw_ref[2:3, :] * sh[1] + cw_ref[1:2, :] * sh[2] + cw_ref[0:1, :] * sh[3]
    r = _sig(_dot(xa, wr_ref[...]) + br_ref[...])
    ig = _sig(_dot(xa, wi_ref[...]) + bi_ref[...])
    lam = lam_ref[...]
    sp = jnp.maximum(-lam, 0.0) + jnp.log(1.0 + jnp.exp(-jnp.abs(lam)))
    la = (-LRU_C) * r * sp
    a = jnp.exp(la)
    y2 = 2.0 * la
    taylor = -(y2 * (1.0 + y2 * (0.5 + y2 * (1.0 / 6.0 + y2 * (1.0 / 24.0)))))
    m2 = jnp.where(y2 > -0.02, taylor, 1.0 - jnp.exp(y2))
    m = jnp.sqrt(m2)
    return xa, sh, r, ig, sp, a, m


def _lru_fwd(z, cw, cb, wr, br, wi, bi, lam, l):
    s = z.shape[0]
    t = _tile(s)
    c = D_LRU
    lay = _lay(l)

    def body(za_ref, ga_ref, cw_ref, cb_ref, wr_ref, br_ref, wi_ref, bi_ref, lam_ref, ya_ref, hs_ref, zprev, hcar):
        i = pl.program_id(0)

        @pl.when(i == 0)
        def _():
            zprev[...] = jnp.zeros_like(zprev)
            hcar[...] = jnp.zeros_like(hcar)

        za = za_ref[...]
        xa, _, _, ig, _, a, m = _lru_gates(za, zprev[...], cw_ref, cb_ref, wr_ref, br_ref, wi_ref, bi_ref, lam_ref)
        u = m * (ig * xa)
        row = _rows((t, c))
        acc_a, acc_h = a, u
        k = 1
        while k < t:
            a_sh = jnp.where(row >= k, _down(acc_a, k), 1.0)
            h_sh = jnp.where(row >= k, _down(acc_h, k), 0.0)
            acc_h = acc_h + acc_a * h_sh
            acc_a = acc_a * a_sh
            k *= 2
        hs = acc_h + acc_a * hcar[...]
        hs_ref[...] = hs
        ga = ga_ref[...]
        ya_ref[...] = (hs * (ga * _sig(ga))).astype(ya_ref.dtype)
        hcar[...] = jnp.sum(jnp.where(row == t - 1, hs, 0.0), axis=0, keepdims=True)
        zprev[...] = za_ref[t - SUBLANES:t, :]

    return pl.pallas_call(
        body, name="lru_fwd", grid=(s // t,),
        in_specs=[pl.BlockSpec((t, c), lambda i: (i, 0)), pl.BlockSpec((t, c), lambda i: (i, 1)),
                  lay((CONV_WIDTH, c)), lay((1, c)), lay((c, c)), lay((1, c)), lay((c, c)), lay((1, c)),
                  lay((1, c))],
        out_specs=[pl.BlockSpec((t, c), lambda i: (i, 0)), pl.BlockSpec((t, c), lambda i: (i, 0))],
        out_shape=[jax.ShapeDtypeStruct((s, c), MXU_DTYPE), jax.ShapeDtypeStruct((s, c), F32)],
        scratch_shapes=[pltpu.VMEM((SUBLANES, c), F32), pltpu.VMEM((1, c), F32)],
        compiler_params=_cp(("arbitrary",), 40),
    )(z, z, cw, cb, wr, br, wi, bi, lam)


def _lru_bwd(z, hs, dy, cw, cb, wr, br, wi, bi, lam, l):
    s = z.shape[0]
    t = _tile(s)
    lay = _lay(l)
    nt = s // t
    c = D_LRU
    hb = t // SUBLANES

    def body(za_ref, zh_ref, ga_ref, hs_ref, hh_ref, dy_ref, cw_ref, cb_ref, wr_ref, br_ref, wi_ref, bi_ref, lam_ref,
             dza_ref, dga_ref, dwr_ref, dwi_ref, dcw_ref, dcb_ref, dbr_ref, dbi_ref, dlam_ref, lcar, dxn):
        i = pl.program_id(0)
        tt = nt - 1 - i

        @pl.when(i == 0)
        def _():
            lcar[...] = jnp.zeros_like(lcar)
            dxn[...] = jnp.zeros_like(dxn)
            for ref in (dwr_ref, dwi_ref, dcw_ref, dcb_ref, dbr_ref, dbi_ref, dlam_ref):
                ref[...] = jnp.zeros_like(ref)

        first = (tt > 0).astype(F32)
        za = za_ref[...]
        xa, sh, r, ig, sp, a, m = _lru_gates(za, zh_ref[...] * first, cw_ref, cb_ref, wr_ref, br_ref, wi_ref, bi_ref,
                                             lam_ref)
        hs_v = hs_ref[...]
        hprev = _down(jnp.concatenate([hh_ref[...] * first, hs_v], axis=0), 1)[SUBLANES:SUBLANES + t]
        ga = ga_ref[...]
        sg = _sig(ga)
        silu = ga * sg
        dya = dy_ref[...]
        dga_ref[...] = (dya * hs_v * (sg * (1.0 + ga * (1.0 - sg)))).astype(dga_ref.dtype)
        row = _rows((t, c))
        acc_h = dya * silu + jnp.where(row == t - 1, lcar[...], 0.0)
        acc_b = jnp.where(row < t - 1, _up(a, 1), 0.0)
        k = 1
        while k < t:
            keep = row < t - k
            b_sh = jnp.where(keep, _up(acc_b, k), 0.0)
            h_sh = jnp.where(keep, _up(acc_h, k), 0.0)
            acc_h = acc_h + acc_b * h_sh
            acc_b = acc_b * b_sh
            k *= 2
        lmb = acc_h
        lcar[...] = jnp.sum(jnp.where(row == 0, a * lmb, 0.0), axis=0, keepdims=True)
        da = lmb * hprev
        dxa = lmb * m * ig
        di = lmb * m * xa
        dm = lmb * ig * xa
        dla = da * a - dm * (a * a) / m
        dr = dla * ((-LRU_C) * sp)
        lam = lam_ref[...]
        dsp = jnp.sum(dla * ((-LRU_C) * r), axis=0, keepdims=True)
        dlam_ref[...] += dsp * (-1.0 / (1.0 + jnp.exp(lam)))
        dpr = dr * r * (1.0 - r)
        dpi = di * ig * (1.0 - ig)
        dbr_ref[...] += jnp.sum(dpr, axis=0, keepdims=True)
        dbi_ref[...] += jnp.sum(dpi, axis=0, keepdims=True)
        dwr_ref[...] += _dot_tn(xa, dpr)
        dwi_ref[...] += _dot_tn(xa, dpi)
        dxa = dxa + _dot_nt(dpr, wr_ref[...]) + _dot_nt(dpi, wi_ref[...])
        dcb_ref[...] += jnp.sum(dxa, axis=0, keepdims=True)
        for k in range(CONV_WIDTH):
            dcw_ref[k:k + 1, :] += jnp.sum(dxa * sh[CONV_WIDTH - 1 - k], axis=0, keepdims=True)
        ext = jnp.concatenate([dxa, dxn[...]], axis=0)
        dza = cw_ref[3:4, :] * dxa
        for j in (1, 2, 3):
            dza = dza + cw_ref[3 - j:4 - j, :] * _up(ext, j)[:t]
        dza_ref[...] = dza.astype(dza_ref.dtype)
        dxn[...] = dxa[:SUBLANES]

    full = lambda shp: pl.BlockSpec(shp, lambda i: (0, 0))
    rev = lambda i: nt - 1 - i
    halo = lambda i: (jnp.maximum((nt - 1 - i) * hb - 1, 0), 0)
    outs = pl.pallas_call(
        body, name="lru_bwd", grid=(nt,),
        in_specs=[pl.BlockSpec((t, c), lambda i: (rev(i), 0)), pl.BlockSpec((SUBLANES, c), halo),
                  pl.BlockSpec((t, c), lambda i: (rev(i), 1)),
                  pl.BlockSpec((t, c), lambda i: (rev(i), 0)), pl.BlockSpec((SUBLANES, c), halo),
                  pl.BlockSpec((t, c), lambda i: (rev(i), 0)),
                  lay((CONV_WIDTH, c)), lay((1, c)), lay((c, c)), lay((1, c)), lay((c, c)), lay((1, c)),
                  lay((1, c))],
        out_specs=[pl.BlockSpec((t, c), lambda i: (rev(i), 0)), pl.BlockSpec((t, c), lambda i: (rev(i), 0)),
                   full((c, c)), full((c, c)), full((CONV_WIDTH, c)), full((1, c)), full((1, c)), full((1, c)),
                   full((1, c))],
        out_shape=[jax.ShapeDtypeStruct((s, c), MXU_DTYPE), jax.ShapeDtypeStruct((s, c), MXU_DTYPE),
                   jax.ShapeDtypeStruct((c, c), F32), jax.ShapeDtypeStruct((c, c), F32),
                   jax.ShapeDtypeStruct((CONV_WIDTH, c), F32)] + [jax.ShapeDtypeStruct((1, c), F32)] * 4,
        scratch_shapes=[pltpu.VMEM((1, c), F32), pltpu.VMEM((SUBLANES, c), F32)],
        compiler_params=_cp(("arbitrary",), 48),
    )(z, z, z, hs, hs, dy, cw, cb, wr, br, wi, bi, lam)
    return outs


POOL_HALO = 16


def _pool_select(lane, v2, v4, v8, v16):
    return jnp.where(lane < 64, v2, jnp.where(lane < 128, v4, jnp.where(lane < 192, v8, v16)))


def _pool_counts(t0, t, c):
    lane = _lanes((t, c))
    win = _pool_select(lane, 2.0, 4.0, 8.0, 16.0)
    seen = (t0 + _rows((t, c)) + 1).astype(F32)
    return lane, jnp.minimum(seen, win)


def _pooled(zc, halo, lane, cnt):
    t = zc.shape[0]
    ext = jnp.concatenate([halo, zc], axis=0)
    s2 = ext + _down(ext, 1)
    s4 = s2 + _down(s2, 2)
    s8 = s4 + _down(s4, 4)
    s16 = s8 + _down(s8, 8)
    cut = lambda v: v[POOL_HALO:POOL_HALO + t]
    return _pool_select(lane, cut(s2), cut(s4), cut(s8), cut(s16)) / cnt - zc


def _pool_fwd(z, wp, ps, l):
    s = z.shape[0]
    t = _tile(s)
    c = D_POOL
    lay = _lay(l)

    def body(zc_ref, gc_ref, wp_ref, ps_ref, yc_ref, zprev):
        i = pl.program_id(0)

        @pl.when(i == 0)
        def _():
            zprev[...] = jnp.zeros_like(zprev)

        zc = zc_ref[...]
        lane, cnt = _pool_counts(i * t, t, c)
        pooled = _pooled(zc, zprev[...], lane, cnt)
        pc = _dot(pooled, wp_ref[...])
        gc = gc_ref[...]
        yc_ref[...] = (pc * ps_ref[...] * (gc * _sig(gc))).astype(yc_ref.dtype)
        zprev[...] = zc_ref[t - POOL_HALO:t, :]

    return pl.pallas_call(
        body, name="pool_fwd", grid=(s // t,),
        in_specs=[pl.BlockSpec((t, c), lambda i: (i, 7)), pl.BlockSpec((t, c), lambda i: (i, 8)),
                  lay((c, c)), lay((1, c))],
        out_specs=pl.BlockSpec((t, c), lambda i: (i, 0)),
        out_shape=jax.ShapeDtypeStruct((s, c), MXU_DTYPE),
        scratch_shapes=[pltpu.VMEM((POOL_HALO, c), F32)],
        compiler_params=_cp(("arbitrary",), 32),
    )(z, z, wp, ps)


def _pool_bwd(z, dy, wp, ps, l):
    s = z.shape[0]
    t = _tile(s)
    lay = _lay(l)
    nt = s // t
    c = D_POOL
    hb = t // POOL_HALO

    def body(zc_ref, zh_ref, gc_ref, dy_ref, wp_ref, ps_ref, dzc_ref, dgc_ref, dwp_ref, dps_ref, ddn):
        i = pl.program_id(0)
        tt = nt - 1 - i

        @pl.when(i == 0)
        def _():
            ddn[...] = jnp.zeros_like(ddn)
            dwp_ref[...] = jnp.zeros_like(dwp_ref)
            dps_ref[...] = jnp.zeros_like(dps_ref)

        first = (tt > 0).astype(F32)
        zc = zc_ref[...]
        lane, cnt = _pool_counts(tt * t, t, c)
        pooled = _pooled(zc, zh_ref[...] * first, lane, cnt)
        pc = _dot(pooled, wp_ref[...])
        gc = gc_ref[...]
        sg = _sig(gc)
        silu = gc * sg
        dyc = dy_ref[...]
        ps_v = ps_ref[...]
        dgc_ref[...] = (dyc * pc * ps_v * (sg * (1.0 + gc * (1.0 - sg)))).astype(dgc_ref.dtype)
        dps_ref[...] += jnp.sum(dyc * pc * silu, axis=0, keepdims=True)
        dpc = dyc * ps_v * silu
        dwp_ref[...] += _dot_tn(pooled, dpc)
        dpooled = _dot_nt(dpc, wp_ref[...])
        dd = dpooled / cnt
        ext = jnp.concatenate([dd, ddn[...]], axis=0)
        f2 = ext + _up(ext, 1)
        f4 = f2 + _up(f2, 2)
        f8 = f4 + _up(f4, 4)
        f16 = f8 + _up(f8, 8)
        dzc = _pool_select(lane, f2[:t], f4[:t], f8[:t], f16[:t]) - dpooled
        dzc_ref[...] = dzc.astype(dzc_ref.dtype)
        ddn[...] = dd[:POOL_HALO]

    full = lambda shp: pl.BlockSpec(shp, lambda i: (0, 0))
    rev = lambda i: nt - 1 - i
    return pl.pallas_call(
        body, name="pool_bwd", grid=(nt,),
        in_specs=[pl.BlockSpec((t, c), lambda i: (rev(i), 7)),
                  pl.BlockSpec((POOL_HALO, c), lambda i: (jnp.maximum(rev(i) * hb - 1, 0), 7)),
                  pl.BlockSpec((t, c), lambda i: (rev(i), 8)),
                  pl.BlockSpec((t, c), lambda i: (rev(i), 3)),
                  lay((c, c)), lay((1, c))],
        out_specs=[pl.BlockSpec((t, c), lambda i: (rev(i), 0)), pl.BlockSpec((t, c), lambda i: (rev(i), 0)),
                   full((c, c)), full((1, c))],
        out_shape=[jax.ShapeDtypeStruct((s, c), MXU_DTYPE), jax.ShapeDtypeStruct((s, c), MXU_DTYPE),
                   jax.ShapeDtypeStruct((c, c), F32), jax.ShapeDtypeStruct((1, c), F32)],
        scratch_shapes=[pltpu.VMEM((POOL_HALO, c), F32)],
        compiler_params=_cp(("arbitrary",), 32),
    )(z, z, z, dy, wp, ps)


def _rope_tables(s):
    pos = jnp.arange(s, dtype=F32)
    inv_freq = ROPE_BASE ** (-jnp.arange(0, QK_ROPE, 2, dtype=F32) / QK_ROPE)
    ang = pos[:, None] * inv_freq[None, :]
    cos, sin = jnp.cos(ang), jnp.sin(ang)
    half = QK_ROPE // 2
    ones = jnp.ones((s, QK_NOPE), F32)
    z64 = jnp.zeros((s, QK_NOPE), F32)
    zh = jnp.zeros((s, half), F32)
    z32 = jnp.zeros((s, HEAD_BLK - QK_NOPE - QK_ROPE), F32)
    c_t = jnp.concatenate([ones, cos, cos, z32], axis=1)
    s1_t = jnp.concatenate([z64, -sin, zh, z32], axis=1)
    s2_t = jnp.concatenate([z64, zh, sin, z32], axis=1)
    return c_t, s1_t, s2_t


def _rope(x, c_t, s1_t, s2_t):
    return x * c_t + pltpu.roll(x, HEAD_BLK - 16, 1) * s1_t + pltpu.roll(x, 16, 1) * s2_t


def _unrope(d, c_t, s1_t, s2_t):
    return d * c_t + pltpu.roll(d * s1_t, 16, 1) + pltpu.roll(d * s2_t, HEAD_BLK - 16, 1)


def _mla_prep_fwd(z, gq, gkv, wuq, wukv, c_t, s1_t, s2_t, l):
    s = z.shape[0]
    t = _tile(s)
    hq = MLA_HEADS * HEAD_BLK
    lay = _lay(l)

    def body(cq_ref, ckv_ref, kr_ref, gq_ref, gkv_ref, wuq_ref, wukv_ref, c_ref, s1_ref, s2_ref,
             q_ref, k_ref, v_ref, qn_ref, kvn_ref):
        ct, s1, s2 = c_ref[...], s1_ref[...], s2_ref[...]
        cq = cq_ref[...]
        qn = (cq * lax.rsqrt(jnp.mean(cq * cq, axis=-1, keepdims=True) + EPS) * gq_ref[...]).astype(MXU_DTYPE)
        qn_ref[...] = qn
        q = jnp.dot(qn, wuq_ref[...], preferred_element_type=F32)
        ckv = ckv_ref[...]
        kvn = (ckv * lax.rsqrt(jnp.mean(ckv * ckv, axis=-1, keepdims=True) + EPS) * gkv_ref[...]).astype(MXU_DTYPE)
        kvn_ref[...] = kvn
        kvp = jnp.dot(kvn, wukv_ref[...], preferred_element_type=F32)
        krr = _rope(kr_ref[...], ct, s1, s2)
        for h in range(MLA_HEADS):
            blk = slice(h * HEAD_BLK, (h + 1) * HEAD_BLK)
            q_ref[:, blk] = (_rope(q[:, blk], ct, s1, s2) * Q_PRESCALE).astype(q_ref.dtype)
            k_ref[:, blk] = (kvp[:, blk] + krr).astype(k_ref.dtype)
        v_ref[...] = kvp[:, hq:].astype(v_ref.dtype)

    tab = pl.BlockSpec((t, HEAD_BLK), lambda i: (i, 0))
    return pl.pallas_call(
        body, name="mla_prep_fwd", grid=(s // t,),
        in_specs=[pl.BlockSpec((t, Q_RANK), lambda i: (i, 2)), pl.BlockSpec((t, KV_RANK), lambda i: (i, 6)),
                  pl.BlockSpec((t, HEAD_BLK), lambda i: (i, 18)),
                  lay((1, Q_RANK)), lay((1, KV_RANK)), lay((Q_RANK, hq)), lay((KV_RANK, hq + D_MLA)),
                  tab, tab, tab],
        out_specs=[pl.BlockSpec((t, hq), lambda i: (i, 0)), pl.BlockSpec((t, hq), lambda i: (i, 0)),
                   pl.BlockSpec((t, D_MLA), lambda i: (i, 0)), pl.BlockSpec((t, Q_RANK), lambda i: (i, 0)),
                   pl.BlockSpec((t, KV_RANK), lambda i: (i, 0))],
        out_shape=[jax.ShapeDtypeStruct((s, hq), MXU_DTYPE), jax.ShapeDtypeStruct((s, hq), MXU_DTYPE),
                   jax.ShapeDtypeStruct((s, D_MLA), MXU_DTYPE), jax.ShapeDtypeStruct((s, Q_RANK), MXU_DTYPE),
                   jax.ShapeDtypeStruct((s, KV_RANK), MXU_DTYPE)],
        compiler_params=_cp(("parallel",), 40),
    )(z, z, z, gq, gkv, wuq, wukv, c_t, s1_t, s2_t)


SUM_LANE_A = V_DIM
SUM_LANE_B = 0


def _flash_fwd(q, k, v, z):
    s = q.shape[0]
    t = _tile(s)
    nq = s // t
    pw = 2 * HEAD_BLK

    def body(q_ref, k_ref, v_ref, gb_ref, o_ref, yb_ref, lse_ref):
        i = pl.program_id(1)
        qv = q_ref[...]
        qa, qb = qv[:, :HEAD_BLK], qv[:, HEAD_BLK:]
        lane = _lanes((t, HEAD_BLK))
        lo = lane < V_DIM

        def update(qh, kh, vh, m, acc, masked):
            sc = _dot_nt(qh, kh)
            if masked:
                sc = jnp.where(_lanes((t, t)) <= _rows((t, t)), sc, -1e30)
            m_new = jnp.maximum(m, jnp.max(sc, axis=-1, keepdims=True))
            p = jnp.exp2(sc - m_new).astype(MXU_DTYPE)
            return m_new, acc * jnp.exp2(m - m_new) + _dot(p, vh)

        def step(j, carry, masked):
            ma, mb, acc_a, acc_b = carry
            kv_rows = pl.ds(pl.multiple_of(j * t, t), t)
            kt = k_ref[kv_rows, :]
            vt = v_ref[kv_rows, :]
            lane_v = _lanes(vt.shape)
            one = jnp.ones_like(vt)
            zero_v = jnp.zeros_like(vt)
            v_a = jnp.where(lane_v < V_DIM, vt, jnp.where(lane_v == SUM_LANE_A, one, zero_v))
            v_b = jnp.where(lane_v >= V_DIM, vt, jnp.where(lane_v == SUM_LANE_B, one, zero_v))
            ma, acc_a = update(qa, kt[:, :HEAD_BLK], v_a, ma, acc_a, masked)
            mb, acc_b = update(qb, kt[:, HEAD_BLK:], v_b, mb, acc_b, masked)
            return ma, mb, acc_a, acc_b

        neg = jnp.full((t, 1), -1e30, F32)
        zero = jnp.zeros((t, HEAD_BLK), F32)
        carry = lax.fori_loop(0, i, lambda j, cr: step(j, cr, False), (neg, neg, zero, zero))
        ma, mb, acc_a, acc_b = step(i, carry, True)
        la = jnp.sum(jnp.where(lane == SUM_LANE_A, acc_a, 0.0), axis=-1, keepdims=True)
        lb = jnp.sum(jnp.where(lane == SUM_LANE_B, acc_b, 0.0), axis=-1, keepdims=True)
        o = jnp.where(lo, acc_a * (1.0 / la), acc_b * (1.0 / lb))
        o_ref[...] = o
        gb = gb_ref[...]
        yb_ref[...] = (o * (gb * _sig(gb))).astype(yb_ref.dtype)
        lse = jnp.where(lo, ma + jnp.log(la) * LOG2_E, mb + jnp.log(lb) * LOG2_E)
        pick = ((_rows((SUBLANES, HEAD_BLK)) == 0) & (_lanes((SUBLANES, HEAD_BLK)) == 0)) | (
            (_rows((SUBLANES, HEAD_BLK)) == 1) & (_lanes((SUBLANES, HEAD_BLK)) == V_DIM))
        lse_ref[0, 0] = lax.dot_general(pick.astype(F32), lse, (((1,), (1,)), ((), ())),
                                        precision=lax.Precision.HIGHEST, preferred_element_type=F32)

    return pl.pallas_call(
        body, name="flash_fwd", grid=(N_PAIR, nq),
        in_specs=[pl.BlockSpec((t, pw), lambda p, i: (i, p)), pl.BlockSpec((s, pw), lambda p, i: (0, p)),
                  pl.BlockSpec((s, HEAD_BLK), lambda p, i: (0, p)),
                  pl.BlockSpec((t, HEAD_BLK), lambda p, i: (i, 9 + p))],
        out_specs=[pl.BlockSpec((t, HEAD_BLK), lambda p, i: (i, p)), pl.BlockSpec((t, HEAD_BLK), lambda p, i: (i, p)),
                   pl.BlockSpec((1, 1, SUBLANES, t), lambda p, i: (p, i, 0, 0))],
        out_shape=[jax.ShapeDtypeStruct((s, D_MLA), F32), jax.ShapeDtypeStruct((s, D_MLA), MXU_DTYPE),
                   jax.ShapeDtypeStruct((N_PAIR, nq, SUBLANES, t), F32)],
        compiler_params=_cp(("parallel", "parallel"), 48),
    )(q, k, v, z)


def _attn_bwd_prep(dy, o, z):
    s = o.shape[0]
    t = _tile(s)
    nq = s // t
    rows = N_PAIR * SUBLANES

    def body(dy_ref, o_ref, gb_ref, do_ref, dgb_ref, dl_ref):
        gb = gb_ref[...]
        sg = _sig(gb)
        dyb = dy_ref[...]
        ov = o_ref[...]
        do = dyb * (gb * sg)
        do_ref[...] = do.astype(do_ref.dtype)
        dgb_ref[...] = (dyb * ov * (sg * (1.0 + gb * (1.0 - sg)))).astype(dgb_ref.dtype)
        r = _rows((rows, D_MLA))
        head = (r // SUBLANES) * 2 + (r % SUBLANES)
        sel = ((r % SUBLANES) < 2) & (_lanes((rows, D_MLA)) // V_DIM == head)
        dl = lax.dot_general(sel.astype(F32), do * ov, (((1,), (1,)), ((), ())),
                             precision=lax.Precision.HIGHEST, preferred_element_type=F32)
        for p in range(N_PAIR):
            dl_ref[p, 0] = dl[p * SUBLANES:(p + 1) * SUBLANES]

    return pl.pallas_call(
        body, name="attn_bwd_prep", grid=(nq,),
        in_specs=[pl.BlockSpec((t, D_MLA), lambda i: (i, 1)), pl.BlockSpec((t, D_MLA), lambda i: (i, 0)),
                  pl.BlockSpec((t, D_MLA), lambda i: (i, 3))],
        out_specs=[pl.BlockSpec((t, D_MLA), lambda i: (i, 0)), pl.BlockSpec((t, D_MLA), lambda i: (i, 0)),
                   pl.BlockSpec((N_PAIR, 1, SUBLANES, t), lambda i: (0, i, 0, 0))],
        out_shape=[jax.ShapeDtypeStruct((s, D_MLA), MXU_DTYPE), jax.ShapeDtypeStruct((s, D_MLA), MXU_DTYPE),
                   jax.ShapeDtypeStruct((N_PAIR, nq, SUBLANES, t), F32)],
        compiler_params=_cp(("parallel",), 32),
    )(dy, o, z)


def _flash_bwd(q, k, v, do, lse, delta):
    s = q.shape[0]
    t = _tile(s)
    nq = s // t
    pw = 2 * HEAD_BLK

    def body(q_ref, do_ref, lse_ref, dl_ref, k_ref, v_ref, dq_ref, dk_ref, dv_ref):
        j = pl.program_id(1)

        @pl.when(j == 0)
        def _():
            dq_ref[...] = jnp.zeros_like(dq_ref)

        kt = k_ref[...]
        ka, kb = kt[:, :HEAD_BLK], kt[:, HEAD_BLK:]
        vt = v_ref[...]

        def head(kh, qh, do_h, lse_row, dl_row, masked):
            st = _dot_nt(kh, qh)
            if masked:
                st = jnp.where(_rows((t, t)) <= _lanes((t, t)), st, -1e30)
            pt = jnp.exp2(st - lse_row)
            dv_h = _dot(pt, do_h)
            dst = (pt * (_dot_nt(vt, do_h) - dl_row)).astype(MXU_DTYPE)
            return dv_h, _dot(dst, qh), _dot_tn(dst, kh)

        def step(i, carry, masked):
            dka, dkb, dv = carry
            q_rows = pl.ds(pl.multiple_of(i * t, t), t)
            qv = q_ref[q_rows, :]
            dov = do_ref[q_rows, :]
            lane = _lanes(dov.shape)
            do_lo = jnp.where(lane < V_DIM, dov, jnp.zeros_like(dov))
            do_hi = jnp.where(lane >= V_DIM, dov, jnp.zeros_like(dov))
            dva, dk_a, dq_a = head(ka, qv[:, :HEAD_BLK], do_lo, lse_ref[0, i, 0:1, :], dl_ref[0, i, 0:1, :], masked)
            dvb, dk_b, dq_b = head(kb, qv[:, HEAD_BLK:], do_hi, lse_ref[0, i, 1:2, :], dl_ref[0, i, 1:2, :], masked)
            dq_ref[q_rows, 0:HEAD_BLK] += dq_a
            dq_ref[q_rows, HEAD_BLK:pw] += dq_b
            return dka + dk_a, dkb + dk_b, dv + dva + dvb

        zero = jnp.zeros((t, HEAD_BLK), F32)
        carry = step(j, (zero, zero, zero), True)
        dka, dkb, dv = lax.fori_loop(j + 1, nq, lambda i, cr: step(i, cr, False), carry)
        dk_ref[:, 0:HEAD_BLK] = dka * LN_2
        dk_ref[:, HEAD_BLK:pw] = dkb * LN_2
        dv_ref[...] = dv.astype(dv_ref.dtype)

    return pl.pallas_call(
        body, name="flash_bwd", grid=(N_PAIR, nq),
        in_specs=[pl.BlockSpec((s, pw), lambda p, j: (0, p)), pl.BlockSpec((s, HEAD_BLK), lambda p, j: (0, p)),
                  pl.BlockSpec((1, nq, SUBLANES, t), lambda p, j: (p, 0, 0, 0)),
                  pl.BlockSpec((1, nq, SUBLANES, t), lambda p, j: (p, 0, 0, 0)),
                  pl.BlockSpec((t, pw), lambda p, j: (j, p)), pl.BlockSpec((t, HEAD_BLK), lambda p, j: (j, p))],
        out_specs=[pl.BlockSpec((s, pw), lambda p, j: (0, p)), pl.BlockSpec((t, pw), lambda p, j: (j, p)),
                   pl.BlockSpec((t, HEAD_BLK), lambda p, j: (j, p))],
        out_shape=[jax.ShapeDtypeStruct((s, MLA_HEADS * HEAD_BLK), F32),
                   jax.ShapeDtypeStruct((s, MLA_HEADS * HEAD_BLK), F32),
                   jax.ShapeDtypeStruct((s, D_MLA), MXU_DTYPE)],
        compiler_params=_cp(("parallel", "arbitrary"), 56),
    )(q, do, lse, delta, k, v)


def _mla_prep_bwd(dq, dk, dv, z, gq, gkv, wuq, wukv, c_t, s1_t, s2_t, l):
    s = z.shape[0]
    t = _tile(s)
    hq = MLA_HEADS * HEAD_BLK
    lay = _lay(l)

    def body(dq_ref, dk_ref, dv_ref, cq_ref, ckv_ref, gq_ref, gkv_ref, wuq_ref, wukv_ref, c_ref, s1_ref, s2_ref,
             dcq_ref, dckv_ref, dkr_ref, dqu_ref, dkvp_ref, dgq_ref, dgkv_ref):
        @pl.when(pl.program_id(0) == 0)
        def _():
            dgq_ref[...] = jnp.zeros_like(dgq_ref)
            dgkv_ref[...] = jnp.zeros_like(dgkv_ref)

        ct, s1, s2 = c_ref[...], s1_ref[...], s2_ref[...]
        dk_sum = jnp.zeros((t, HEAD_BLK), F32)
        for h in range(MLA_HEADS):
            blk = slice(h * HEAD_BLK, (h + 1) * HEAD_BLK)
            dqu_ref[:, blk] = _unrope(dq_ref[:, blk] * ATT_SCALE, ct, s1, s2).astype(dqu_ref.dtype)
            dkh = dk_ref[:, blk]
            dk_sum = dk_sum + dkh
            dkvp_ref[:, blk] = dkh.astype(dkvp_ref.dtype)
        dkvp_ref[:, hq:] = dv_ref[...]
        lane = _lanes((t, HEAD_BLK))
        rope_lanes = (lane >= KR_LANE0) & (lane < KR_LANE0 + QK_ROPE)
        dkr_ref[...] = _unrope(jnp.where(rope_lanes, dk_sum, 0.0), ct, s1, s2).astype(dkr_ref.dtype)

        def norm_bwd(c_in, g, dn_out, dc_ref, dg_ref):
            rs = lax.rsqrt(jnp.mean(c_in * c_in, axis=-1, keepdims=True) + EPS)
            n = c_in * rs
            dg_ref[...] += jnp.sum(dn_out * n, axis=0, keepdims=True)
            dn = dn_out * g
            dc_ref[...] = (rs * (dn - n * jnp.mean(dn * n, axis=-1, keepdims=True))).astype(dc_ref.dtype)

        norm_bwd(cq_ref[...], gq_ref[...], _dot_nt(dqu_ref[...], wuq_ref[...]), dcq_ref, dgq_ref)
        norm_bwd(ckv_ref[...], gkv_ref[...], _dot_nt(dkvp_ref[...], wukv_ref[...]), dckv_ref, dgkv_ref)

    full = lambda shp: pl.BlockSpec(shp, lambda i: (0, 0))
    tab = pl.BlockSpec((t, HEAD_BLK), lambda i: (i, 0))
    row = lambda w: pl.BlockSpec((t, w), lambda i: (i, 0))
    return pl.pallas_call(
        body, name="mla_prep_bwd", grid=(s // t,),
        in_specs=[row(hq), row(hq), row(D_MLA),
                  pl.BlockSpec((t, Q_RANK), lambda i: (i, 2)), pl.BlockSpec((t, KV_RANK), lambda i: (i, 6)),
                  lay((1, Q_RANK)), lay((1, KV_RANK)), lay((Q_RANK, hq)), lay((KV_RANK, hq + D_MLA)),
                  tab, tab, tab],
        out_specs=[row(Q_RANK), row(KV_RANK), row(HEAD_BLK), row(hq), row(hq + D_MLA),
                   full((1, Q_RANK)), full((1, KV_RANK))],
        out_shape=[jax.ShapeDtypeStruct((s, Q_RANK), MXU_DTYPE), jax.ShapeDtypeStruct((s, KV_RANK), MXU_DTYPE),
                   jax.ShapeDtypeStruct((s, HEAD_BLK), MXU_DTYPE), jax.ShapeDtypeStruct((s, hq), MXU_DTYPE),
                   jax.ShapeDtypeStruct((s, hq + D_MLA), MXU_DTYPE),
                   jax.ShapeDtypeStruct((1, Q_RANK), F32), jax.ShapeDtypeStruct((1, KV_RANK), F32)],
        compiler_params=_cp(("arbitrary",), 48),
    )(dq, dk, dv, z, z, gq, gkv, wuq, wukv, c_t, s1_t, s2_t)


def _outproj_fwd(x, ya, yb, yc, w, l):
    s = x.shape[0]
    t = _tile(s)
    lay = _lay(l)

    def body(x_ref, ya_ref, yb_ref, yc_ref, wa_ref, wb_ref, wc_ref, o_ref):
        o_ref[...] = (x_ref[...] + _dot(ya_ref[...], wa_ref[...]) + _dot(yb_ref[...], wb_ref[...])
                      + _dot(yc_ref[...], wc_ref[...]))

    row = lambda w_: pl.BlockSpec((t, w_), lambda i: (i, 0))
    return pl.pallas_call(
        body, name="outproj_fwd", grid=(s // t,),
        in_specs=[row(D_MODEL), row(D_LRU), row(D_MLA), row(D_POOL),
                  lay((D_LRU, D_MODEL), 0), lay((D_MLA, D_MODEL), 1), lay((D_POOL, D_MODEL), 3)],
        out_specs=row(D_MODEL),
        out_shape=jax.ShapeDtypeStruct((s, D_MODEL), F32),
        compiler_params=_cp(("parallel",), 40),
    )(x, ya, yb, yc, w, w, w)


def _outproj_bwd(dx, ya, yb, yc, w, l):
    s = dx.shape[0]
    t = _tile(s)

    def body(dx_ref, ya_ref, yb_ref, yc_ref, w_ref, dy_ref, dw_ref):
        @pl.when(pl.program_id(0) == 0)
        def _():
            dw_ref[...] = jnp.zeros_like(dw_ref)

        dxv = dx_ref[...].astype(MXU_DTYPE)
        dy_ref[...] = _dot_nt(dxv, w_ref[...])
        dw_ref[0:D_LRU, :] += _dot_tn(ya_ref[...], dxv)
        dw_ref[D_LRU:D_LRU + D_MLA, :] += _dot_tn(yb_ref[...], dxv)
        dw_ref[D_LRU + D_MLA:D_MIX, :] += _dot_tn(yc_ref[...], dxv)

    row = lambda w_: pl.BlockSpec((t, w_), lambda i: (i, 0))
    return pl.pallas_call(
        body, name="outproj_bwd", grid=(s // t,),
        in_specs=[row(D_MODEL), row(D_LRU), row(D_MLA), row(D_POOL), _lay(l)((D_MIX, D_MODEL))],
        out_specs=[row(D_MIX), pl.BlockSpec((D_MIX, D_MODEL), lambda i: (0, 0))],
        out_shape=[jax.ShapeDtypeStruct((s, D_MIX), F32), jax.ShapeDtypeStruct((D_MIX, D_MODEL), F32)],
        compiler_params=_cp(("arbitrary",), 48),
    )(dx, ya, yb, yc, w)


def _mm_tn(a, b, name):
    s, k1 = a.shape
    n = b.shape[1]
    t = _tile(s)

    def body(a_ref, b_ref, o_ref):
        @pl.when(pl.program_id(0) == 0)
        def _():
            o_ref[...] = jnp.zeros_like(o_ref)

        o_ref[...] += _dot_tn(a_ref[...], b_ref[...])

    return pl.pallas_call(
        body, name=name, grid=(s // t,),
        in_specs=[pl.BlockSpec((t, k1), lambda i: (i, 0)), pl.BlockSpec((t, n), lambda i: (i, 0))],
        out_specs=pl.BlockSpec((k1, n), lambda i: (0, 0)),
        out_shape=jax.ShapeDtypeStruct((k1, n), F32),
        compiler_params=_cp(("arbitrary",), 56),
    )(a, b)


def _inproj_bwd(dz, w, x, g, dxn, l):
    s = x.shape[0]
    t = _tile(s)
    lay = _lay(l)

    def body(dz_ref, w_ref, x_ref, g_ref, dxn_ref, dx_ref, dg_ref):
        @pl.when(pl.program_id(0) == 0)
        def _():
            dg_ref[...] = jnp.zeros_like(dg_ref)

        dh = _dot_nt(dz_ref[...], w_ref[...])
        xv = x_ref[...]
        rs = lax.rsqrt(jnp.mean(xv * xv, axis=-1, keepdims=True) + EPS)
        n = xv * rs
        dg_ref[...] += jnp.sum(dh * n, axis=0, keepdims=True)
        dn = dh * g_ref[...]
        dx_ref[...] = dxn_ref[...] + rs * (dn - n * jnp.mean(dn * n, axis=-1, keepdims=True))

    row = lambda w_: pl.BlockSpec((t, w_), lambda i: (i, 0))
    return pl.pallas_call(
        body, name="inproj_bwd", grid=(s // t,),
        in_specs=[row(D_INP), lay((D_MODEL, D_INP)), row(D_MODEL), lay((1, D_MODEL)), row(D_MODEL)],
        out_specs=[row(D_MODEL), pl.BlockSpec((1, D_MODEL), lambda i: (0, 0))],
        out_shape=[jax.ShapeDtypeStruct((s, D_MODEL), F32), jax.ShapeDtypeStruct((1, D_MODEL), F32)],
        compiler_params=_cp(("arbitrary",), 48),
    )(dz, w, x, g, dxn)


def _loss_head(x, g, tgt):
    s = x.shape[0]
    t = _tile(s)

    def body(x_ref, g_ref, t_ref, dx_ref, loss_ref, dg_ref):
        @pl.when(pl.program_id(0) == 0)
        def _():
            loss_ref[...] = jnp.zeros_like(loss_ref)
            dg_ref[...] = jnp.zeros_like(dg_ref)

        xv = x_ref[...]
        rs = lax.rsqrt(jnp.mean(xv * xv, axis=-1, keepdims=True) + EPS)
        n = xv * rs
        gv = g_ref[...]
        e = n * gv - t_ref[...]
        loss_ref[...] += 0.5 * jnp.sum(jnp.mean(e * e, axis=-1, keepdims=True))
        dyf = e * (1.0 / D_MODEL)
        dg_ref[...] += jnp.sum(dyf * n, axis=0, keepdims=True)
        dn = dyf * gv
        dx_ref[...] = rs * (dn - n * jnp.mean(dn * n, axis=-1, keepdims=True))

    row = pl.BlockSpec((t, D_MODEL), lambda i: (i, 0))
    vec = pl.BlockSpec((1, D_MODEL), lambda i: (0, 0))
    return pl.pallas_call(
        body, name="loss_head", grid=(s // t,),
        in_specs=[row, vec, row],
        out_specs=[row, pl.BlockSpec((1, LANES), lambda i: (0, 0)), vec],
        out_shape=[jax.ShapeDtypeStruct((s, D_MODEL), F32), jax.ShapeDtypeStruct((1, LANES), F32),
                   jax.ShapeDtypeStruct((1, D_MODEL), F32)],
        compiler_params=_cp(("arbitrary",), 32),
    )(x, g, tgt)


def _block_diag(w):
    n, h, d, _ = w.shape
    return jnp.einsum('lhij,hk->lhikj', w, jnp.eye(h, dtype=w.dtype)).reshape(n, h * d, h * d)


def _diag_blocks(wfull, h):
    d = wfull.shape[-1] // h
    return jnp.stack([wfull[:, i * d:(i + 1) * d, i * d:(i + 1) * d] for i in range(h)], axis=1)


REF_TO_PERM = np.concatenate([np.arange(0, 1152), np.arange(1536, 1792),
                              np.arange(2304 + KR_LANE0, 2304 + KR_LANE0 + QK_ROPE),
                              np.arange(1152, 1536), np.arange(1792, 2304)])
N_SHARD = 4
W_IN_SHARD = D_IN // N_SHARD


def _w_in_runs():
    ref_of_perm = -np.ones(D_INP, np.int64)
    ref_of_perm[REF_TO_PERM] = np.arange(D_IN)
    perm_runs, p = [], 0
    while p < D_INP:
        r, q = ref_of_perm[p], p + 1
        if r < 0:
            while q < D_INP and ref_of_perm[q] < 0:
                q += 1
            perm_runs.append((None, q - p, 0))
        else:
            while (q < D_INP and ref_of_perm[q] == ref_of_perm[q - 1] + 1
                   and ref_of_perm[q] // W_IN_SHARD == r // W_IN_SHARD):
                q += 1
            perm_runs.append((int(r // W_IN_SHARD), int(r % W_IN_SHARD), int(r % W_IN_SHARD + q - p)))
        p = q
    shard_runs = []
    for s in range(N_SHARD):
        cols = REF_TO_PERM[s * W_IN_SHARD:(s + 1) * W_IN_SHARD]
        runs, a = [], 0
        for b in range(1, W_IN_SHARD + 1):
            if b == W_IN_SHARD or cols[b] != cols[b - 1] + 1:
                runs.append((int(cols[a]), int(cols[b - 1]) + 1))
                a = b
        shard_runs.append(runs)
    return perm_runs, shard_runs


def _permute_w_in(shards):
    perm_runs, _ = _w_in_runs()
    lead = shards[0].shape[:-1]
    parts = [jnp.zeros(lead + (a,), shards[0].dtype) if s is None else shards[s][..., a:b] for s, a, b in perm_runs]
    return jnp.concatenate(parts, axis=-1)


def _w_in_shard(wp, s):
    _, shard_runs = _w_in_runs()
    return jnp.concatenate([wp[..., a:b] for a, b in shard_runs[s]], axis=-1)


def _pad_w_uq(w):
    w4 = w.reshape(w.shape[:2] + (MLA_HEADS, QK_NOPE + QK_ROPE))
    return jnp.pad(w4, ((0, 0),) * 3 + ((0, HEAD_BLK - QK_NOPE - QK_ROPE),)).reshape(w.shape[:2] + (-1,))


def _unpad_w_uq(w):
    return w.reshape(w.shape[:2] + (MLA_HEADS, HEAD_BLK))[..., :QK_NOPE + QK_ROPE].reshape(w.shape[:2] + (-1,))


def _pad_w_ukv(w):
    w4 = w.reshape(w.shape[:2] + (MLA_HEADS, QK_NOPE + V_DIM))
    kpart = jnp.pad(w4[..., :QK_NOPE], ((0, 0),) * 3 + ((0, HEAD_BLK - QK_NOPE),)).reshape(w.shape[:2] + (-1,))
    return jnp.concatenate([kpart, w4[..., QK_NOPE:].reshape(w.shape[:2] + (-1,))], axis=2)


def _unpad_w_ukv(w):
    hq = MLA_HEADS * HEAD_BLK
    kpart = w[..., :hq].reshape(w.shape[:2] + (MLA_HEADS, HEAD_BLK))[..., :QK_NOPE]
    vpart = w[..., hq:].reshape(w.shape[:2] + (MLA_HEADS, V_DIM))
    return jnp.concatenate([kpart, vpart], axis=3).reshape(w.shape[:2] + (-1,))


def _local_step(x, tgt, w):
    s = x.shape[0]
    tabs = _rope_tables(s)
    vec = lambda a: a[:, None, :]
    mxu = lambda a: a.astype(MXU_DTYPE)
    p = dict(g=vec(w['norm_g']), w_in=mxu(w['w_in']), cw=w['conv_w'], cb=vec(w['conv_b']),
             wr=mxu(_block_diag(w['w_rg'])), br=vec(w['b_rg']), wi=mxu(_block_diag(w['w_ig'])), bi=vec(w['b_ig']),
             lam=vec(w['lru_lambda']), gq=vec(w['q_norm_g']), gkv=vec(w['kv_norm_g']),
             wuq=mxu(_pad_w_uq(w['w_uq'])), wukv=mxu(_pad_w_ukv(w['w_ukv'])),
             wp=mxu(_block_diag(w['w_pool'])), ps=vec(w['pool_scale']), wout=mxu(w['w_out']))
    lru = lambda l: (p['cw'], p['cb'], p['wr'], p['br'], p['wi'], p['bi'], p['lam'], l)
    mla = lambda l: (p['gq'], p['gkv'], p['wuq'], p['wukv'], *tabs, l)

    saved = []
    for l in range(DEPTH):
        h, z = _inproj_fwd(x, p['g'], p['w_in'], l)
        ya, hs = _lru_fwd(z, *lru(l))
        yc = _pool_fwd(z, p['wp'], p['ps'], l)
        q, k, v, qn, kvn = _mla_prep_fwd(z, *mla(l))
        o, yb, lse = _flash_fwd(q, k, v, z)
        saved.append(dict(x=x, h=h, z=z, hs=hs, ya=ya, yb=yb, yc=yc, q=q, k=k, v=v, qn=qn, kvn=kvn, o=o, lse=lse))
        x = _outproj_fwd(x, ya, yb, yc, p['wout'], l)

    dx, loss, dgf = _loss_head(x, w['final_norm_g'][None], tgt)
    per_layer = {n: [None] * DEPTH for n in WEIGHT_NAMES if n != 'final_norm_g'}
    for l in reversed(range(DEPTH)):
        sv = saved[l]
        dy, per_layer['w_out'][l] = _outproj_bwd(dx, sv['ya'], sv['yb'], sv['yc'], p['wout'], l)
        dza, dga, *lru_grads = _lru_bwd(sv['z'], sv['hs'], dy, *lru(l))
        for n, g in zip(('w_rg', 'w_ig', 'conv_w', 'conv_b', 'b_rg', 'b_ig', 'lru_lambda'), lru_grads):
            per_layer[n][l] = g
        dzc, dgc, per_layer['w_pool'][l], per_layer['pool_scale'][l] = _pool_bwd(sv['z'], dy, p['wp'], p['ps'], l)
        do, dgb, delta = _attn_bwd_prep(dy, sv['o'], sv['z'])
        dq, dk, dv = _flash_bwd(sv['q'], sv['k'], sv['v'], do, sv['lse'], delta)
        dcq, dckv, dkr, dqu, dkvp, per_layer['q_norm_g'][l], per_layer['kv_norm_g'][l] = _mla_prep_bwd(
            dq, dk, dv, sv['z'], *mla(l))
        dz = jnp.concatenate([dza, dga, dcq, dgb, dckv, dzc, dgc, dkr], axis=1)
        per_layer['w_in'][l] = _mm_tn(sv['h'], dz, "dwin")
        per_layer['w_uq'][l] = _mm_tn(sv['qn'], dqu, "dwuq")
        per_layer['w_ukv'][l] = _mm_tn(sv['kvn'], dkvp, "dwukv")
        dx, per_layer['norm_g'][l] = _inproj_bwd(dz, p['w_in'], sv['x'], p['g'], dx, l)
    grads = {n: jnp.stack(g) for n, g in per_layer.items()}
    for n in ('norm_g', 'conv_b', 'b_rg', 'b_ig', 'lru_lambda', 'q_norm_g', 'kv_norm_g', 'pool_scale'):
        grads[n] = grads[n][:, 0, :]
    grads['w_rg'] = _diag_blocks(grads['w_rg'], LRU_HEADS)
    grads['w_ig'] = _diag_blocks(grads['w_ig'], LRU_HEADS)
    grads['w_pool'] = _diag_blocks(grads['w_pool'], len(POOL_WINDOWS))
    grads['w_uq'] = _unpad_w_uq(grads['w_uq'])
    grads['w_ukv'] = _unpad_w_ukv(grads['w_ukv'])
    grads['final_norm_g'] = dgf[0]
    return loss[0, 0], dx, grads


WIRE_DTYPE = jnp.bfloat16
MESH_IDS = pl.DeviceIdType.MESH
_HBM = pl.BlockSpec(memory_space=pltpu.HBM)


def _coords():
    return lax.axis_index("x"), lax.axis_index("y"), lax.axis_index("c")


def _comm_call(body, name, arrays, out_shapes, copies_per_array):
    n = len(arrays)
    return pl.pallas_call(
        body, name=name, out_shape=out_shapes, in_specs=[_HBM] * n, out_specs=[_HBM] * n,
        scratch_shapes=[pltpu.SemaphoreType.DMA((n, copies_per_array)), pltpu.SemaphoreType.DMA((n, copies_per_array)),
                        pltpu.SemaphoreType.DMA((n,))],
    )(*arrays)


def _all_gather8(blocks, name):
    n = len(blocks)
    every = range(n)

    def body(*refs):
        x_refs, out_refs = refs[:n], refs[n:2 * n]
        send_sems, recv_sems, local_sems = refs[2 * n:]
        x, y, c = _coords()
        me, sibling = (x, y, c), (x, y, 1 - c)
        chips = [(1 - x, y), (x, 1 - y), (1 - x, 1 - y)]

        def slot(t, px, py, pc):
            return out_refs[t].at[4 * px + 2 * py + pc]

        def copy(t, k, block, to, own=False):
            return pltpu.make_async_remote_copy(
                src_ref=x_refs[t] if own else slot(t, *block), dst_ref=slot(t, *block),
                send_sem=send_sems.at[t, k], recv_sem=recv_sems.at[t, k], device_id=to, device_id_type=MESH_IDS)

        mine = [pltpu.make_async_copy(x_refs[t], slot(t, *me), local_sems.at[t]) for t in every]
        for cp in mine:
            cp.start()
        first = [copy(t, 0, me, sibling, own=True) for t in every]
        first += [copy(t, 1 + j, me, (*chip, c), own=True) for j, chip in enumerate(chips) for t in every]
        for cp in first:
            cp.start()
        passed = [[copy(t, 4 + j, (*chip, c), sibling) for t in every] for j, chip in enumerate(chips)]
        for j, chip in enumerate(chips):
            for t in every:
                copy(t, 1 + j, (*chip, c), me).wait_recv()
                passed[j][t].start()
        for t in every:
            copy(t, 0, sibling, me).wait_recv()
        for j, chip in enumerate(chips):
            for t in every:
                copy(t, 4 + j, (*chip, 1 - c), me).wait_recv()
        for cp in first + [cp for group in passed for cp in group]:
            cp.wait_send()
        for cp in mine:
            cp.wait()

    outs = [jax.ShapeDtypeStruct((N_DEV,) + b.shape, b.dtype) for b in blocks]
    return _comm_call(body, name, blocks, outs, 7)


D2D_PIECES = 8


def _pieces(ref, dst):
    step = ref.shape[0] // D2D_PIECES
    return [(ref.at[pl.ds(i * step, step)], dst.at[pl.ds(i * step, step)]) for i in range(D2D_PIECES)]


def _sibling_halves(arrs, name):
    n = len(arrs)

    def body(*refs):
        a_refs, out_refs = refs[:n], refs[n:2 * n]
        send_sems, recv_sems, local_sems = refs[2 * n:]
        x, y, c = _coords()
        own = [pltpu.make_async_copy(a_refs[t].at[c], out_refs[t].at[0], local_sems.at[t]) for t in range(n)]
        sent = [pltpu.make_async_remote_copy(
            src_ref=src, dst_ref=dst, send_sem=send_sems.at[t, i], recv_sem=recv_sems.at[t, i],
            device_id=(x, y, 1 - c), device_id_type=MESH_IDS)
            for t in range(n) for i, (src, dst) in enumerate(_pieces(a_refs[t].at[1 - c], out_refs[t].at[1]))]
        for cp in own + sent:
            cp.start()
        for cp in sent + own:
            cp.wait()

    return _comm_call(body, name, arrs, [jax.ShapeDtypeStruct(a.shape, a.dtype) for a in arrs], D2D_PIECES)


def _chip_exchange(arrs, name):
    n = len(arrs)

    def body(*refs):
        a_refs, out_refs = refs[:n], refs[n:2 * n]
        send_sems, recv_sems, local_sems = refs[2 * n:]
        x, y, c = _coords()
        s_me = 2 * x + y
        copies = [pltpu.make_async_copy(a_refs[t].at[s_me], out_refs[t].at[s_me], local_sems.at[t]) for t in range(n)]
        for j, (cx, cy) in enumerate([(1 - x, y), (x, 1 - y), (1 - x, 1 - y)]):
            copies += [pltpu.make_async_remote_copy(
                src_ref=a_refs[t].at[2 * cx + cy], dst_ref=out_refs[t].at[s_me], send_sem=send_sems.at[t, j],
                recv_sem=recv_sems.at[t, j], device_id=(cx, cy, c), device_id_type=MESH_IDS) for t in range(n)]
        for cp in copies:
            cp.start()
        for cp in copies:
            cp.wait()

    return _comm_call(body, name, arrs, [jax.ShapeDtypeStruct(a.shape, a.dtype) for a in arrs], 3)


def _sibling_gather(arrs, name):
    n = len(arrs)

    def body(*refs):
        a_refs, out_refs = refs[:n], refs[n:2 * n]
        send_sems, recv_sems, local_sems = refs[2 * n:]
        x, y, c = _coords()
        own = [pltpu.make_async_copy(a_refs[t], out_refs[t].at[c], local_sems.at[t]) for t in range(n)]
        sent = [pltpu.make_async_remote_copy(
            src_ref=src, dst_ref=dst, send_sem=send_sems.at[t, i], recv_sem=recv_sems.at[t, i],
            device_id=(x, y, 1 - c), device_id_type=MESH_IDS)
            for t in range(n) for i, (src, dst) in enumerate(_pieces(a_refs[t], out_refs[t].at[c]))]
        for cp in own + sent:
            cp.start()
        for cp in sent + own:
            cp.wait()

    return _comm_call(body, name, arrs, [jax.ShapeDtypeStruct((2,) + a.shape, a.dtype) for a in arrs], D2D_PIECES)


def _sum_leading(arrs, out_dtype, steps, name):
    n = len(arrs)

    def body(*refs):
        for i_ref, o_ref in zip(refs[:n], refs[n:]):
            acc = i_ref[0].astype(F32)
            for k in range(1, i_ref.shape[0]):
                acc = acc + i_ref[k].astype(F32)
            o_ref[...] = acc.astype(o_ref.dtype)

    return pl.pallas_call(
        body, name=name, grid=(steps,),
        in_specs=[pl.BlockSpec((a.shape[0], a.shape[1] // steps, a.shape[2]), lambda i: (0, i, 0)) for a in arrs],
        out_specs=[pl.BlockSpec((a.shape[1] // steps, a.shape[2]), lambda i: (i, 0)) for a in arrs],
        out_shape=[jax.ShapeDtypeStruct(a.shape[1:], out_dtype) for a in arrs],
        compiler_params=_cp(("parallel",), 40),
    )(*arrs)


def _adamw_update(w_ref, g_ref, m_ref, v_ref, d_ref, mo_ref, vo_ref):
    gv = g_ref[...]
    mn = ADAM_B1 * m_ref[...] + (1.0 - ADAM_B1) * gv
    vn = ADAM_B2 * v_ref[...] + (1.0 - ADAM_B2) * (gv * gv)
    mo_ref[...] = mn
    vo_ref[...] = vn
    m_hat = mn / (1.0 - ADAM_B1 ** ADAM_STEP)
    v_hat = vn / (1.0 - ADAM_B2 ** ADAM_STEP)
    d_ref[...] = (-ADAM_LR) * (m_hat / (jnp.sqrt(v_hat) + ADAM_EPS) + ADAM_WD * w_ref[...])


def _adamw(w, g, m, v, name):
    r, cdim = w.shape
    tr = math.gcd(r, 512)

    def body(*refs):
        _adamw_update(*refs)

    blk = pl.BlockSpec((tr, cdim), lambda i: (i, 0))
    return pl.pallas_call(
        body, name=name, grid=(r // tr,),
        in_specs=[blk] * 4, out_specs=[blk] * 3,
        out_shape=[jax.ShapeDtypeStruct((r, cdim), F32)] * 3,
        compiler_params=_cp(("parallel",), 40),
    )(w, g, m, v)


def _adamw_small(ws, gs, ms, vs, name):
    n = len(ws)

    def body(*refs):
        ins, outs = refs[:4 * n], refs[4 * n:]
        for t in range(n):
            _adamw_update(ins[t], ins[n + t], ins[2 * n + t], ins[3 * n + t], outs[t], outs[n + t], outs[2 * n + t])

    shapes = [jax.ShapeDtypeStruct(w.shape, F32) for w in ws]
    outs = pl.pallas_call(body, name=name, out_shape=shapes * 3)(*ws, *gs, *ms, *vs)
    return outs[:n], outs[n:2 * n], outs[2 * n:]


HALF = DEPTH // 2
BIG = ['w_in', 'w_uq', 'w_ukv', 'w_out']
SHARD_AXIS = {'w_in': 2, 'conv_w': 2, 'w_uq': 2, 'w_ukv': 2, 'w_out': 1}
FULL_SHAPE = {'w_in': (DEPTH, D_MODEL, D_IN), 'conv_w': (DEPTH, CONV_WIDTH, D_LRU),
              'w_uq': (DEPTH, Q_RANK, MLA_HEADS * (QK_NOPE + QK_ROPE)),
              'w_ukv': (DEPTH, KV_RANK, MLA_HEADS * (QK_NOPE + V_DIM)), 'w_out': (DEPTH, D_MIX, D_MODEL)}


def _shard_shape(n):
    shp = list(FULL_SHAPE[n])
    shp[SHARD_AXIS[n]] //= N_SHARD
    return tuple(shp)


def _rows_view(a, lead=0):
    return a.reshape(a.shape[:lead] + (-1, a.shape[-1]))


def _gather_weights(local):
    c = lax.axis_index("c")
    names = BIG + ['conv_w']
    halves = [lax.dynamic_slice_in_dim(local[n], HALF * c, HALF, axis=0) for n in names]
    halves = [h.astype(WIRE_DTYPE) if n in BIG else h for n, h in zip(names, halves)]
    got = _all_gather8(halves, "gather_weights")
    full = {}
    for n, g in zip(names, got):
        g = g.reshape((N_SHARD, DEPTH) + g.shape[2:])
        if n == 'w_in':
            full[n] = _permute_w_in([g[s] for s in range(N_SHARD)])
        else:
            full[n] = jnp.moveaxis(g, 0, SHARD_AXIS[n]).reshape(FULL_SHAPE[n])
    return full


def _shard_blocks(g, n):
    width = _shard_shape(n)[SHARD_AXIS[n]]

    def block(h, s):
        part = g[HALF * h:HALF * (h + 1)]
        if n == 'w_in':
            part = _w_in_shard(part, s)
        else:
            part = lax.slice_in_dim(part, s * width, (s + 1) * width, axis=SHARD_AXIS[n])
        return _rows_view(part)

    return jnp.stack([jnp.stack([block(h, s) for s in range(N_SHARD)]) for h in range(2)]).astype(WIRE_DTYPE)


SUM_STEPS = 8


def _reduce_big(grads):
    contrib = [_shard_blocks(grads[n], n) for n in BIG]
    cut = lambda a: a.reshape((2, D2D_PIECES, -1, a.shape[-1]))
    pair = _sibling_halves([cut(a) for a in contrib], "pair_exchange_big")
    pair_sum = _sum_leading([_rows_view(p, 1) for p in pair], WIRE_DTYPE, SUM_STEPS, "pair_sum_big")
    to_chips = [p.reshape(c.shape[1:]) for p, c in zip(pair_sum, contrib)]
    from_chips = _chip_exchange(to_chips, "chip_exchange_big")
    mine = _sum_leading(from_chips, F32, SUM_STEPS, "chip_sum_big")
    both = _sibling_gather(mine, "sibling_big")
    return {n: b.reshape(_shard_shape(n)) for n, b in zip(BIG, both)}


SMALL = REPLICATED + ['conv_w']


def _reduce_small(grads):
    views = [_rows_view(jnp.atleast_2d(grads[n])) for n in SMALL]
    sums = _sum_leading(_all_gather8(views, "gather_small"), F32, 1, "sum_small")
    return {n: s.reshape(grads[n].shape) for n, s in zip(SMALL, sums)}


def kernel(x, norm_g, w_in, conv_w, conv_b, w_rg, b_rg, w_ig, b_ig, lru_lambda, q_norm_g, w_uq, kv_norm_g, w_ukv, w_pool, pool_scale, w_out, final_norm_g, loss_target, m_norm_g, m_w_in, m_conv_w, m_conv_b, m_w_rg, m_b_rg, m_w_ig, m_b_ig, m_lru_lambda, m_q_norm_g, m_w_uq, m_kv_norm_g, m_w_ukv, m_w_pool, m_pool_scale, m_w_out, m_final_norm_g, v_norm_g, v_w_in, v_conv_w, v_conv_b, v_w_rg, v_b_rg, v_w_ig, v_b_ig, v_lru_lambda, v_q_norm_g, v_w_uq, v_kv_norm_g, v_w_ukv, v_w_pool, v_pool_scale, v_w_out, v_final_norm_g):
    w_loc = dict(zip(WEIGHT_NAMES, (norm_g, w_in, conv_w, conv_b, w_rg, b_rg, w_ig, b_ig, lru_lambda, q_norm_g, w_uq,
                                    kv_norm_g, w_ukv, w_pool, pool_scale, w_out, final_norm_g)))
    m_loc = dict(zip(WEIGHT_NAMES, (m_norm_g, m_w_in, m_conv_w, m_conv_b, m_w_rg, m_b_rg, m_w_ig, m_b_ig, m_lru_lambda,
                                    m_q_norm_g, m_w_uq, m_kv_norm_g, m_w_ukv, m_w_pool, m_pool_scale, m_w_out,
                                    m_final_norm_g)))
    v_loc = dict(zip(WEIGHT_NAMES, (v_norm_g, v_w_in, v_conv_w, v_conv_b, v_w_rg, v_b_rg, v_w_ig, v_b_ig, v_lru_lambda,
                                    v_q_norm_g, v_w_uq, v_kv_norm_g, v_w_ukv, v_w_pool, v_pool_scale, v_w_out,
                                    v_final_norm_g)))
    w_full = dict(w_loc)
    w_full.update(_gather_weights(w_loc))
    loss_local, dx, g_local = _local_step(x[0], loss_target[0], w_full)
    loss = lax.psum(loss_local, ("x", "y", "c"))

    grads = _reduce_big(g_local)
    g_small = _reduce_small(g_local)
    shard = 2 * lax.axis_index("x") + lax.axis_index("y")
    width = D_LRU // N_SHARD
    grads['conv_w'] = lax.dynamic_slice_in_dim(g_small['conv_w'], shard * width, width, axis=2)
    for n in REPLICATED:
        grads[n] = g_small[n]

    delta, new_m, new_v = {}, {}, {}
    for n in BIG:
        d, mo, vo = _adamw(*[_rows_view(t[n]) for t in (w_loc, grads, m_loc, v_loc)], "adamw_" + n)
        delta[n], new_m[n], new_v[n] = (a.reshape(w_loc[n].shape) for a in (d, mo, vo))
    small = [[_rows_view(jnp.atleast_2d(t[n])) for n in SMALL] for t in (w_loc, grads, m_loc, v_loc)]
    for tree, outs in zip((delta, new_m, new_v), _adamw_small(*small, "adamw_small")):
        tree.update({n: a.reshape(w_loc[n].shape) for n, a in zip(SMALL, outs)})

    return (loss, dx[None], *[grads[n] for n in WEIGHT_NAMES], *[delta[n] for n in WEIGHT_NAMES],
            *[new_m[n] for n in WEIGHT_NAMES], *[new_v[n] for n in WEIGHT_NAMES])
```

```python
import functools
import math

import jax
import jax.numpy as jnp
import numpy as np
from jax import lax
from jax.experimental import pallas as pl
from jax.experimental.pallas import tpu as pltpu

F32 = jnp.float32
MXU_DTYPE = jnp.bfloat16

D_MODEL = 1024
DEPTH = 4
EPS = 1e-6
D_LRU = 384
LRU_HEADS = 6
CONV_WIDTH = 4
LRU_C = 8.0
MLA_HEADS = 6
QK_NOPE = 64
QK_ROPE = 32
V_DIM = 64
D_MLA = MLA_HEADS * V_DIM
Q_RANK = 384
KV_RANK = 256
ROPE_BASE = 10000.0
POOL_WINDOWS = (2, 4, 8, 16)
D_POOL = 256
D_MIX = D_LRU + D_MLA + D_POOL
D_IN = 2336
ATT_SCALE = (QK_NOPE + QK_ROPE) ** -0.5
LOG2_E = 1.4426950408889634
LN_2 = 0.6931471805599453
Q_PRESCALE = ATT_SCALE * LOG2_E

ADAM_LR = 0.001
ADAM_B1 = 0.9
ADAM_B2 = 0.999
ADAM_EPS = 1e-08
ADAM_WD = 0.01
ADAM_STEP = 10

LANES = 128
SUBLANES = 8
V7X_VMEM_BYTES = 64 << 20
N_DEV = 8

D_INP = 2432
KR_LANE0 = 64
HEAD_BLK = 128
N_PAIR = MLA_HEADS // 2

WEIGHT_NAMES = ['norm_g', 'w_in', 'conv_w', 'conv_b', 'w_rg', 'b_rg', 'w_ig', 'b_ig', 'lru_lambda', 'q_norm_g',
                'w_uq', 'kv_norm_g', 'w_ukv', 'w_pool', 'pool_scale', 'w_out', 'final_norm_g']
SHARDED = ['w_in', 'conv_w', 'w_uq', 'w_ukv', 'w_out']
REPLICATED = [n for n in WEIGHT_NAMES if n not in SHARDED]


def _cp(sem, vmem_mb=None):
    return pltpu.CompilerParams(dimension_semantics=sem,
                                vmem_limit_bytes=None if vmem_mb is None else vmem_mb << 20)


def _dot(a, b):
    return jnp.dot(a.astype(MXU_DTYPE), b.astype(MXU_DTYPE), preferred_element_type=F32)


def _dot_nt(a, b):
    return lax.dot_general(a.astype(MXU_DTYPE), b.astype(MXU_DTYPE), (((1,), (1,)), ((), ())),
                           preferred_element_type=F32)


def _dot_tn(a, b):
    return lax.dot_general(a.astype(MXU_DTYPE), b.astype(MXU_DTYPE), (((0,), (0,)), ((), ())),
                           preferred_element_type=F32)


def _sig(x):
    return 1.0 / (1.0 + jnp.exp(-x))


def _down(x, k):
    return pltpu.roll(x, k, 0)


def _up(x, k):
    return pltpu.roll(x, x.shape[0] - k, 0)


def _rows(shape):
    return lax.broadcasted_iota(jnp.int32, shape, 0)


def _lanes(shape):
    return lax.broadcasted_iota(jnp.int32, shape, 1)


def _tile(s):
    return min(512, s)


def _lay(l):
    return lambda shp, blk=0: pl.BlockSpec((None,) + shp, lambda *_: (l, blk, 0))


def _inproj_fwd(x, g, w, l):
    s = x.shape[0]
    t = _tile(s)
    lay = _lay(l)

    def body(x_ref, g_ref, w_ref, h_ref, z_ref):
        xv = x_ref[...]
        rs = lax.rsqrt(jnp.mean(xv * xv, axis=-1, keepdims=True) + EPS)
        h = (xv * rs * g_ref[...]).astype(MXU_DTYPE)
        h_ref[...] = h
        z_ref[...] = jnp.dot(h, w_ref[...], preferred_element_type=F32)

    return pl.pallas_call(
        body, name="inproj_fwd", grid=(s // t,),
        in_specs=[pl.BlockSpec((t, D_MODEL), lambda i: (i, 0)),
                  lay((1, D_MODEL)), lay((D_MODEL, D_INP))],
        out_specs=[pl.BlockSpec((t, D_MODEL), lambda i: (i, 0)),
                   pl.BlockSpec((t, D_INP), lambda i: (i, 0))],
        out_shape=[jax.ShapeDtypeStruct((s, D_MODEL), MXU_DTYPE), jax.ShapeDtypeStruct((s, D_INP), F32)],
        compiler_params=_cp(("parallel",), 40),
    )(x, g, w)


def _lru_gates(za, halo, cw_ref, cb_ref, wr_ref, br_ref, wi_ref, bi_ref, lam_ref):
    t = za.shape[0]
    ext = jnp.concatenate([halo, za], axis=0)
    sh = [za] + [_down(ext, j)[SUBLANES:SUBLANES + t] for j in (1, 2, 3)]
    xa = cb_ref[...] + cw_ref[3:4, :] * sh[0] + cw_ref[2:3, :] * sh[1] + cw_ref[1:2, :] * sh[2] + cw_ref[0:1, :] * sh[3]
    r = _sig(_dot(xa, wr_ref[...]) + br_ref[...])
    ig = _sig(_dot(xa, wi_ref[...]) + bi_ref[...])
    lam = lam_ref[...]
    sp = jnp.maximum(-lam, 0.0) + jnp.log(1.0 + jnp.exp(-jnp.abs(lam)))
    la = (-LRU_C) * r * sp
    a = jnp.exp(la)
    y2 = 2.0 * la
    taylor = -(y2 * (1.0 + y2 * (0.5 + y2 * (1.0 / 6.0 + y2 * (1.0 / 24.0)))))
    m2 = jnp.where(y2 > -0.02, taylor, 1.0 - jnp.exp(y2))
    m = jnp.sqrt(m2)
    return xa, sh, r, ig, sp, a, m


def _lru_fwd(z, cw, cb, wr, br, wi, bi, lam, l):
    s = z.shape[0]
    t = _tile(s)
    c = D_LRU
    lay = _lay(l)

    def body(za_ref, ga_ref, cw_ref, cb_ref, wr_ref, br_ref, wi_ref, bi_ref, lam_ref, ya_ref, hs_ref, zprev, hcar):
        i = pl.program_id(0)

        @pl.when(i == 0)
        def _():
            zprev[...] = jnp.zeros_like(zprev)
            hcar[...] = jnp.zeros_like(hcar)

        za = za_ref[...]
        xa, _, _, ig, _, a, m = _lru_gates(za, zprev[...], cw_ref, cb_ref, wr_ref, br_ref, wi_ref, bi_ref, lam_ref)
        u = m * (ig * xa)
        row = _rows((t, c))
        acc_a, acc_h = a, u
        k = 1
        while k < t:
            a_sh = jnp.where(row >= k, _down(acc_a, k), 1.0)
            h_sh = jnp.where(row >= k, _down(acc_h, k), 0.0)
            acc_h = acc_h + acc_a * h_sh
            acc_a = acc_a * a_sh
            k *= 2
        hs = acc_h + acc_a * hcar[...]
        hs_ref[...] = hs
        ga = ga_ref[...]
        ya_ref[...] = (hs * (ga * _sig(ga))).astype(ya_ref.dtype)
        hcar[...] = jnp.sum(jnp.where(row == t - 1, hs, 0.0), axis=0, keepdims=True)
        zprev[...] = za_ref[t - SUBLANES:t, :]

    return pl.pallas_call(
        body, name="lru_fwd", grid=(s // t,),
        in_specs=[pl.BlockSpec((t, c), lambda i: (i, 0)), pl.BlockSpec((t, c), lambda i: (i, 1)),
                  lay((CONV_WIDTH, c)), lay((1, c)), lay((c, c)), lay((1, c)), lay((c, c)), lay((1, c)),
                  lay((1, c))],
        out_specs=[pl.BlockSpec((t, c), lambda i: (i, 0)), pl.BlockSpec((t, c), lambda i: (i, 0))],
        out_shape=[jax.ShapeDtypeStruct((s, c), MXU_DTYPE), jax.ShapeDtypeStruct((s, c), F32)],
        scratch_shapes=[pltpu.VMEM((SUBLANES, c), F32), pltpu.VMEM((1, c), F32)],
        compiler_params=_cp(("arbitrary",), 40),
    )(z, z, cw, cb, wr, br, wi, bi, lam)


def _lru_bwd(z, hs, dy, cw, cb, wr, br, wi, bi, lam, l):
    s = z.shape[0]
    t = _tile(s)
    lay = _lay(l)
    nt = s // t
    c = D_LRU
    hb = t // SUBLANES

    def body(za_ref, zh_ref, ga_ref, hs_ref, hh_ref, dy_ref, cw_ref, cb_ref, wr_ref, br_ref, wi_ref, bi_ref, lam_ref,
             dza_ref, dga_ref, dwr_ref, dwi_ref, dcw_ref, dcb_ref, dbr_ref, dbi_ref, dlam_ref, lcar, dxn):
        i = pl.program_id(0)
        tt = nt - 1 - i

        @pl.when(i == 0)
        def _():
            lcar[...] = jnp.zeros_like(lcar)
            dxn[...] = jnp.zeros_like(dxn)
            for ref in (dwr_ref, dwi_ref, dcw_ref, dcb_ref, dbr_ref, dbi_ref, dlam_ref):
                ref[...] = jnp.zeros_like(ref)

        first = (tt > 0).astype(F32)
        za = za_ref[...]
        xa, sh, r, ig, sp, a, m = _lru_gates(za, zh_ref[...] * first, cw_ref, cb_ref, wr_ref, br_ref, wi_ref, bi_ref,
                                             lam_ref)
        hs_v = hs_ref[...]
        hprev = _down(jnp.concatenate([hh_ref[...] * first, hs_v], axis=0), 1)[SUBLANES:SUBLANES + t]
        ga = ga_ref[...]
        sg = _sig(ga)
        silu = ga * sg
        dya = dy_ref[...]
        dga_ref[...] = (dya * hs_v * (sg * (1.0 + ga * (1.0 - sg)))).astype(dga_ref.dtype)
        row = _rows((t, c))
        acc_h = dya * silu + jnp.where(row == t - 1, lcar[...], 0.0)
        acc_b = jnp.where(row < t - 1, _up(a, 1), 0.0)
        k = 1
        while k < t:
            keep = row < t - k
            b_sh = jnp.where(keep, _up(acc_b, k), 0.0)
            h_sh = jnp.where(keep, _up(acc_h, k), 0.0)
            acc_h = acc_h + acc_b * h_sh
            acc_b = acc_b * b_sh
            k *= 2
        lmb = acc_h
        lcar[...] = jnp.sum(jnp.where(row == 0, a * lmb, 0.0), axis=0, keepdims=True)
        da = lmb * hprev
        dxa = lmb * m * ig
        di = lmb * m * xa
        dm = lmb * ig * xa
        dla = da * a - dm * (a * a) / m
        dr = dla * ((-LRU_C) * sp)
        lam = lam_ref[...]
        dsp = jnp.sum(dla * ((-LRU_C) * r), axis=0, keepdims=True)
        dlam_ref[...] += dsp * (-1.0 / (1.0 + jnp.exp(lam)))
        dpr = dr * r * (1.0 - r)
        dpi = di * ig * (1.0 - ig)
        dbr_ref[...] += jnp.sum(dpr, axis=0, keepdims=True)
        dbi_ref[...] += jnp.sum(dpi, axis=0, keepdims=True)
        dwr_ref[...] += _dot_tn(xa, dpr)
        dwi_ref[...] += _dot_tn(xa, dpi)
        dxa = dxa + _dot_nt(dpr, wr_ref[...]) + _dot_nt(dpi, wi_ref[...])
        dcb_ref[...] += jnp.sum(dxa, axis=0, keepdims=True)
        for k in range(CONV_WIDTH):
            dcw_ref[k:k + 1, :] += jnp.sum(dxa * sh[CONV_WIDTH - 1 - k], axis=0, keepdims=True)
        ext = jnp.concatenate([dxa, dxn[...]], axis=0)
        dza = cw_ref[3:4, :] * dxa
        for j in (1, 2, 3):
            dza = dza + cw_ref[3 - j:4 - j, :] * _up(ext, j)[:t]
        dza_ref[...] = dza.astype(dza_ref.dtype)
        dxn[...] = dxa[:SUBLANES]

    full = lambda shp: pl.BlockSpec(shp, lambda i: (0, 0))
    rev = lambda i: nt - 1 - i
    halo = lambda i: (jnp.maximum((nt - 1 - i) * hb - 1, 0), 0)
    outs = pl.pallas_call(
        body, name="lru_bwd", grid=(nt,),
        in_specs=[pl.BlockSpec((t, c), lambda i: (rev(i), 0)), pl.BlockSpec((SUBLANES, c), halo),
                  pl.BlockSpec((t, c), lambda i: (rev(i), 1)),
                  pl.BlockSpec((t, c), lambda i: (rev(i), 0)), pl.BlockSpec((SUBLANES, c), halo),
                  pl.BlockSpec((t, c), lambda i: (rev(i), 0)),
                  lay((CONV_WIDTH, c)), lay((1, c)), lay((c, c)), lay((1, c)), lay((c, c)), lay((1, c)),
                  lay((1, c))],
        out_specs=[pl.BlockSpec((t, c), lambda i: (rev(i), 0)), pl.BlockSpec((t, c), lambda i: (rev(i), 0)),
                   full((c, c)), full((c, c)), full((CONV_WIDTH, c)), full((1, c)), full((1, c)), full((1, c)),
                   full((1, c))],
        out_shape=[jax.ShapeDtypeStruct((s, c), MXU_DTYPE), jax.ShapeDtypeStruct((s, c), MXU_DTYPE),
                   jax.ShapeDtypeStruct((c, c), F32), jax.ShapeDtypeStruct((c, c), F32),
                   jax.ShapeDtypeStruct((CONV_WIDTH, c), F32)] + [jax.ShapeDtypeStruct((1, c), F32)] * 4,
        scratch_shapes=[pltpu.VMEM((1, c), F32), pltpu.VMEM((SUBLANES, c), F32)],
        compiler_params=_cp(("arbitrary",), 48),
    )(z, z, z, hs, hs, dy, cw, cb, wr, br, wi, bi, lam)
    return outs


POOL_HALO = 16


def _pool_select(lane, v2, v4, v8, v16):
    return jnp.where(lane < 64, v2, jnp.where(lane < 128, v4, jnp.where(lane < 192, v8, v16)))


def _pool_counts(t0, t, c):
    lane = _lanes((t, c))
    win = _pool_select(lane, 2.0, 4.0, 8.0, 16.0)
    seen = (t0 + _rows((t, c)) + 1).astype(F32)
    return lane, jnp.minimum(seen, win)


def _pooled(zc, halo, lane, cnt):
    t = zc.shape[0]
    ext = jnp.concatenate([halo, zc], axis=0)
    s2 = ext + _down(ext, 1)
    s4 = s2 + _down(s2, 2)
    s8 = s4 + _down(s4, 4)
    s16 = s8 + _down(s8, 8)
    cut = lambda v: v[POOL_HALO:POOL_HALO + t]
    return _pool_select(lane, cut(s2), cut(s4), cut(s8), cut(s16)) / cnt - zc


def _pool_fwd(z, wp, ps, l):
    s = z.shape[0]
    t = _tile(s)
    c = D_POOL
    lay = _lay(l)

    def body(zc_ref, gc_ref, wp_ref, ps_ref, yc_ref, zprev):
        i = pl.program_id(0)

        @pl.when(i == 0)
        def _():
            zprev[...] = jnp.zeros_like(zprev)

        zc = zc_ref[...]
        lane, cnt = _pool_counts(i * t, t, c)
        pooled = _pooled(zc, zprev[...], lane, cnt)
        pc = _dot(pooled, wp_ref[...])
        gc = gc_ref[...]
        yc_ref[...] = (pc * ps_ref[...] * (gc * _sig(gc))).astype(yc_ref.dtype)
        zprev[...] = zc_ref[t - POOL_HALO:t, :]

    return pl.pallas_call(
        body, name="pool_fwd", grid=(s // t,),
        in_specs=[pl.BlockSpec((t, c), lambda i: (i, 7)), pl.BlockSpec((t, c), lambda i: (i, 8)),
                  lay((c, c)), lay((1, c))],
        out_specs=pl.BlockSpec((t, c), lambda i: (i, 0)),
        out_shape=jax.ShapeDtypeStruct((s, c), MXU_DTYPE),
        scratch_shapes=[pltpu.VMEM((POOL_HALO, c), F32)],
        compiler_params=_cp(("arbitrary",), 32),
    )(z, z, wp, ps)


def _pool_bwd(z, dy, wp, ps, l):
    s = z.shape[0]
    t = _tile(s)
    lay = _lay(l)
    nt = s // t
    c = D_POOL
    hb = t // POOL_HALO

    def body(zc_ref, zh_ref, gc_ref, dy_ref, wp_ref, ps_ref, dzc_ref, dgc_ref, dwp_ref, dps_ref, ddn):
        i = pl.program_id(0)
        tt = nt - 1 - i

        @pl.when(i == 0)
        def _():
            ddn[...] = jnp.zeros_like(ddn)
            dwp_ref[...] = jnp.zeros_like(dwp_ref)
            dps_ref[...] = jnp.zeros_like(dps_ref)

        first = (tt > 0).astype(F32)
        zc = zc_ref[...]
        lane, cnt = _pool_counts(tt * t, t, c)
        pooled = _pooled(zc, zh_ref[...] * first, lane, cnt)
        pc = _dot(pooled, wp_ref[...])
        gc = gc_ref[...]
        sg = _sig(gc)
        silu = gc * sg
        dyc = dy_ref[...]
        ps_v = ps_ref[...]
        dgc_ref[...] = (dyc * pc * ps_v * (sg * (1.0 + gc * (1.0 - sg)))).astype(dgc_ref.dtype)
        dps_ref[...] += jnp.sum(dyc * pc * silu, axis=0, keepdims=True)
        dpc = dyc * ps_v * silu
        dwp_ref[...] += _dot_tn(pooled, dpc)
        dpooled = _dot_nt(dpc, wp_ref[...])
        dd = dpooled / cnt
        ext = jnp.concatenate([dd, ddn[...]], axis=0)
        f2 = ext + _up(ext, 1)
        f4 = f2 + _up(f2, 2)
        f8 = f4 + _up(f4, 4)
        f16 = f8 + _up(f8, 8)
        dzc = _pool_select(lane, f2[:t], f4[:t], f8[:t], f16[:t]) - dpooled
        dzc_ref[...] = dzc.astype(dzc_ref.dtype)
        ddn[...] = dd[:POOL_HALO]

    full = lambda shp: pl.BlockSpec(shp, lambda i: (0, 0))
    rev = lambda i: nt - 1 - i
    return pl.pallas_call(
        body, name="pool_bwd", grid=(nt,),
        in_specs=[pl.BlockSpec((t, c), lambda i: (rev(i), 7)),
                  pl.BlockSpec((POOL_HALO, c), lambda i: (jnp.maximum(rev(i) * hb - 1, 0), 7)),
                  pl.BlockSpec((t, c), lambda i: (rev(i), 8)),
                  pl.BlockSpec((t, c), lambda i: (rev(i), 3)),
                  lay((c, c)), lay((1, c))],
        out_specs=[pl.BlockSpec((t, c), lambda i: (rev(i), 0)), pl.BlockSpec((t, c), lambda i: (rev(i), 0)),
                   full((c, c)), full((1, c))],
        out_shape=[jax.ShapeDtypeStruct((s, c), MXU_DTYPE), jax.ShapeDtypeStruct((s, c), MXU_DTYPE),
                   jax.ShapeDtypeStruct((c, c), F32), jax.ShapeDtypeStruct((1, c), F32)],
        scratch_shapes=[pltpu.VMEM((POOL_HALO, c), F32)],
        compiler_params=_cp(("arbitrary",), 32),
    )(z, z, z, dy, wp, ps)


def _rope_tables(s):
    pos = jnp.arange(s, dtype=F32)
    inv_freq = ROPE_BASE ** (-jnp.arange(0, QK_ROPE, 2, dtype=F32) / QK_ROPE)
    ang = pos[:, None] * inv_freq[None, :]
    cos, sin = jnp.cos(ang), jnp.sin(ang)
    half = QK_ROPE // 2
    ones = jnp.ones((s, QK_NOPE), F32)
    z64 = jnp.zeros((s, QK_NOPE), F32)
    zh = jnp.zeros((s, half), F32)
    z32 = jnp.zeros((s, HEAD_BLK - QK_NOPE - QK_ROPE), F32)
    c_t = jnp.concatenate([ones, cos, cos, z32], axis=1)
    s1_t = jnp.concatenate([z64, -sin, zh, z32], axis=1)
    s2_t = jnp.concatenate([z64, zh, sin, z32], axis=1)
    return c_t, s1_t, s2_t


def _rope(x, c_t, s1_t, s2_t):
    return x * c_t + pltpu.roll(x, HEAD_BLK - 16, 1) * s1_t + pltpu.roll(x, 16, 1) * s2_t


def _unrope(d, c_t, s1_t, s2_t):
    return d * c_t + pltpu.roll(d * s1_t, 16, 1) + pltpu.roll(d * s2_t, HEAD_BLK - 16, 1)


def _mla_prep_fwd(z, gq, gkv, wuq, wukv, c_t, s1_t, s2_t, l):
    s = z.shape[0]
    t = _tile(s)
    hq = MLA_HEADS * HEAD_BLK
    lay = _lay(l)

    def body(cq_ref, ckv_ref, kr_ref, gq_ref, gkv_ref, wuq_ref, wukv_ref, c_ref, s1_ref, s2_ref,
             q_ref, k_ref, v_ref, qn_ref, kvn_ref):
        ct, s1, s2 = c_ref[...], s1_ref[...], s2_ref[...]
        cq = cq_ref[...]
        qn = (cq * lax.rsqrt(jnp.mean(cq * cq, axis=-1, keepdims=True) + EPS) * gq_ref[...]).astype(MXU_DTYPE)
        qn_ref[...] = qn
        q = jnp.dot(qn, wuq_ref[...], preferred_element_type=F32)
        ckv = ckv_ref[...]
        kvn = (ckv * lax.rsqrt(jnp.mean(ckv * ckv, axis=-1, keepdims=True) + EPS) * gkv_ref[...]).astype(MXU_DTYPE)
        kvn_ref[...] = kvn
        kvp = jnp.dot(kvn, wukv_ref[...], preferred_element_type=F32)
        krr = _rope(kr_ref[...], ct, s1, s2)
        for h in range(MLA_HEADS):
            blk = slice(h * HEAD_BLK, (h + 1) * HEAD_BLK)
            q_ref[:, blk] = (_rope(q[:, blk], ct, s1, s2) * Q_PRESCALE).astype(q_ref.dtype)
            k_ref[:, blk] = (kvp[:, blk] + krr).astype(k_ref.dtype)
        v_ref[...] = kvp[:, hq:].astype(v_ref.dtype)

    tab = pl.BlockSpec((t, HEAD_BLK), lambda i: (i, 0))
    return pl.pallas_call(
        body, name="mla_prep_fwd", grid=(s // t,),
        in_specs=[pl.BlockSpec((t, Q_RANK), lambda i: (i, 2)), pl.BlockSpec((t, KV_RANK), lambda i: (i, 6)),
                  pl.BlockSpec((t, HEAD_BLK), lambda i: (i, 18)),
                  lay((1, Q_RANK)), lay((1, KV_RANK)), lay((Q_RANK, hq)), lay((KV_RANK, hq + D_MLA)),
                  tab, tab, tab],
        out_specs=[pl.BlockSpec((t, hq), lambda i: (i, 0)), pl.BlockSpec((t, hq), lambda i: (i, 0)),
                   pl.BlockSpec((t, D_MLA), lambda i: (i, 0)), pl.BlockSpec((t, Q_RANK), lambda i: (i, 0)),
                   pl.BlockSpec((t, KV_RANK), lambda i: (i, 0))],
        out_shape=[jax.ShapeDtypeStruct((s, hq), MXU_DTYPE), jax.ShapeDtypeStruct((s, hq), MXU_DTYPE),
                   jax.ShapeDtypeStruct((s, D_MLA), MXU_DTYPE), jax.ShapeDtypeStruct((s, Q_RANK), MXU_DTYPE),
                   jax.ShapeDtypeStruct((s, KV_RANK), MXU_DTYPE)],
        compiler_params=_cp(("parallel",), 40),
    )(z, z, z, gq, gkv, wuq, wukv, c_t, s1_t, s2_t)


SUM_LANE_A = V_DIM
SUM_LANE_B = 0


def _flash_fwd(q, k, v, z):
    s = q.shape[0]
    t = _tile(s)
    nq = s // t
    pw = 2 * HEAD_BLK

    def body(q_ref, k_ref, v_ref, gb_ref, o_ref, yb_ref, lse_ref):
        i = pl.program_id(1)
        qv = q_ref[...]
        qa, qb = qv[:, :HEAD_BLK], qv[:, HEAD_BLK:]
        lane = _lanes((t, HEAD_BLK))
        lo = lane < V_DIM

        def update(qh, kh, vh, m, acc, masked):
            sc = _dot_nt(qh, kh)
            if masked:
                sc = jnp.where(_lanes((t, t)) <= _rows((t, t)), sc, -1e30)
            m_new = jnp.maximum(m, jnp.max(sc, axis=-1, keepdims=True))
            p = jnp.exp2(sc - m_new).astype(MXU_DTYPE)
            return m_new, acc * jnp.exp2(m - m_new) + _dot(p, vh)

        def step(j, carry, masked):
            ma, mb, acc_a, acc_b = carry
            kv_rows = pl.ds(pl.multiple_of(j * t, t), t)
            kt = k_ref[kv_rows, :]
            vt = v_ref[kv_rows, :]
            lane_v = _lanes(vt.shape)
            one = jnp.ones_like(vt)
            zero_v = jnp.zeros_like(vt)
            v_a = jnp.where(lane_v < V_DIM, vt, jnp.where(lane_v == SUM_LANE_A, one, zero_v))
            v_b = jnp.where(lane_v >= V_DIM, vt, jnp.where(lane_v == SUM_LANE_B, one, zero_v))
            ma, acc_a = update(qa, kt[:, :HEAD_BLK], v_a, ma, acc_a, masked)
            mb, acc_b = update(qb, kt[:, HEAD_BLK:], v_b, mb, acc_b, masked)
            return ma, mb, acc_a, acc_b

        neg = jnp.full((t, 1), -1e30, F32)
        zero = jnp.zeros((t, HEAD_BLK), F32)
        carry = lax.fori_loop(0, i, lambda j, cr: step(j, cr, False), (neg, neg, zero, zero))
        ma, mb, acc_a, acc_b = step(i, carry, True)
        la = jnp.sum(jnp.where(lane == SUM_LANE_A, acc_a, 0.0), axis=-1, keepdims=True)
        lb = jnp.sum(jnp.where(lane == SUM_LANE_B, acc_b, 0.0), axis=-1, keepdims=True)
        o = jnp.where(lo, acc_a * (1.0 / la), acc_b * (1.0 / lb))
        o_ref[...] = o
        gb = gb_ref[...]
        yb_ref[...] = (o * (gb * _sig(gb))).astype(yb_ref.dtype)
        lse = jnp.where(lo, ma + jnp.log(la) * LOG2_E, mb + jnp.log(lb) * LOG2_E)
        pick = ((_rows((SUBLANES, HEAD_BLK)) == 0) & (_lanes((SUBLANES, HEAD_BLK)) == 0)) | (
            (_rows((SUBLANES, HEAD_BLK)) == 1) & (_lanes((SUBLANES, HEAD_BLK)) == V_DIM))
        lse_ref[0, 0] = lax.dot_general(pick.astype(F32), lse, (((1,), (1,)), ((), ())),
                                        precision=lax.Precision.HIGHEST, preferred_element_type=F32)

    return pl.pallas_call(
        body, name="flash_fwd", grid=(N_PAIR, nq),
        in_specs=[pl.BlockSpec((t, pw), lambda p, i: (i, p)), pl.BlockSpec((s, pw), lambda p, i: (0, p)),
                  pl.BlockSpec((s, HEAD_BLK), lambda p, i: (0, p)),
                  pl.BlockSpec((t, HEAD_BLK), lambda p, i: (i, 9 + p))],
        out_specs=[pl.BlockSpec((t, HEAD_BLK), lambda p, i: (i, p)), pl.BlockSpec((t, HEAD_BLK), lambda p, i: (i, p)),
                   pl.BlockSpec((1, 1, SUBLANES, t), lambda p, i: (p, i, 0, 0))],
        out_shape=[jax.ShapeDtypeStruct((s, D_MLA), F32), jax.ShapeDtypeStruct((s, D_MLA), MXU_DTYPE),
                   jax.ShapeDtypeStruct((N_PAIR, nq, SUBLANES, t), F32)],
        compiler_params=_cp(("parallel", "parallel"), 48),
    )(q, k, v, z)


def _attn_bwd_prep(dy, o, z):
    s = o.shape[0]
    t = _tile(s)
    nq = s // t
    rows = N_PAIR * SUBLANES

    def body(dy_ref, o_ref, gb_ref, do_ref, dgb_ref, dl_ref):
        gb = gb_ref[...]
        sg = _sig(gb)
        dyb = dy_ref[...]
        ov = o_ref[...]
        do = dyb * (gb * sg)
        do_ref[...] = do.astype(do_ref.dtype)
        dgb_ref[...] = (dyb * ov * (sg * (1.0 + gb * (1.0 - sg)))).astype(dgb_ref.dtype)
        r = _rows((rows, D_MLA))
        head = (r // SUBLANES) * 2 + (r % SUBLANES)
        sel = ((r % SUBLANES) < 2) & (_lanes((rows, D_MLA)) // V_DIM == head)
        dl = lax.dot_general(sel.astype(F32), do * ov, (((1,), (1,)), ((), ())),
                             precision=lax.Precision.HIGHEST, preferred_element_type=F32)
        for p in range(N_PAIR):
            dl_ref[p, 0] = dl[p * SUBLANES:(p + 1) * SUBLANES]

    return pl.pallas_call(
        body, name="attn_bwd_prep", grid=(nq,),
        in_specs=[pl.BlockSpec((t, D_MLA), lambda i: (i, 1)), pl.BlockSpec((t, D_MLA), lambda i: (i, 0)),
                  pl.BlockSpec((t, D_MLA), lambda i: (i, 3))],
        out_specs=[pl.BlockSpec((t, D_MLA), lambda i: (i, 0)), pl.BlockSpec((t, D_MLA), lambda i: (i, 0)),
                   pl.BlockSpec((N_PAIR, 1, SUBLANES, t), lambda i: (0, i, 0, 0))],
        out_shape=[jax.ShapeDtypeStruct((s, D_MLA), MXU_DTYPE), jax.ShapeDtypeStruct((s, D_MLA), MXU_DTYPE),
                   jax.ShapeDtypeStruct((N_PAIR, nq, SUBLANES, t), F32)],
        compiler_params=_cp(("parallel",), 32),
    )(dy, o, z)


def _flash_bwd(q, k, v, do, lse, delta):
    s = q.shape[0]
    t = _tile(s)
    nq = s // t
    pw = 2 * HEAD_BLK

    def body(q_ref, do_ref, lse_ref, dl_ref, k_ref, v_ref, dq_ref, dk_ref, dv_ref):
        j = pl.program_id(1)

        @pl.when(j == 0)
        def _():
            dq_ref[...] = jnp.zeros_like(dq_ref)

        kt = k_ref[...]
        ka, kb = kt[:, :HEAD_BLK], kt[:, HEAD_BLK:]
        vt = v_ref[...]

        def head(kh, qh, do_h, lse_row, dl_row, masked):
            st = _dot_nt(kh, qh)
            if masked:
                st = jnp.where(_rows((t, t)) <= _lanes((t, t)), st, -1e30)
            pt = jnp.exp2(st - lse_row)
            dv_h = _dot(pt, do_h)
            dst = (pt * (_dot_nt(vt, do_h) - dl_row)).astype(MXU_DTYPE)
            return dv_h, _dot(dst, qh), _dot_tn(dst, kh)

        def step(i, carry, masked):
            dka, dkb, dv = carry
            q_rows = pl.ds(pl.multiple_of(i * t, t), t)
            qv = q_ref[q_rows, :]
            dov = do_ref[q_rows, :]
            lane = _lanes(dov.shape)
            do_lo = jnp.where(lane < V_DIM, dov, jnp.zeros_like(dov))
            do_hi = jnp.where(lane >= V_DIM, dov, jnp.zeros_like(dov))
            dva, dk_a, dq_a = head(ka, qv[:, :HEAD_BLK], do_lo, lse_ref[0, i, 0:1, :], dl_ref[0, i, 0:1, :], masked)
            dvb, dk_b, dq_b = head(kb, qv[:, HEAD_BLK:], do_hi, lse_ref[0, i, 1:2, :], dl_ref[0, i, 1:2, :], masked)
            dq_ref[q_rows, 0:HEAD_BLK] += dq_a
            dq_ref[q_rows, HEAD_BLK:pw] += dq_b
            return dka + dk_a, dkb + dk_b, dv + dva + dvb

        zero = jnp.zeros((t, HEAD_BLK), F32)
        carry = step(j, (zero, zero, zero), True)
        dka, dkb, dv = lax.fori_loop(j + 1, nq, lambda i, cr: step(i, cr, False), carry)
        dk_ref[:, 0:HEAD_BLK] = dka * LN_2
        dk_ref[:, HEAD_BLK:pw] = dkb * LN_2
        dv_ref[...] = dv.astype(dv_ref.dtype)

    return pl.pallas_call(
        body, name="flash_bwd", grid=(N_PAIR, nq),
        in_specs=[pl.BlockSpec((s, pw), lambda p, j: (0, p)), pl.BlockSpec((s, HEAD_BLK), lambda p, j: (0, p)),
                  pl.BlockSpec((1, nq, SUBLANES, t), lambda p, j: (p, 0, 0, 0)),
                  pl.BlockSpec((1, nq, SUBLANES, t), lambda p, j: (p, 0, 0, 0)),
                  pl.BlockSpec((t, pw), lambda p, j: (j, p)), pl.BlockSpec((t, HEAD_BLK), lambda p, j: (j, p))],
        out_specs=[pl.BlockSpec((s, pw), lambda p, j: (0, p)), pl.BlockSpec((t, pw), lambda p, j: (j, p)),
                   pl.BlockSpec((t, HEAD_BLK), lambda p, j: (j, p))],
        out_shape=[jax.ShapeDtypeStruct((s, MLA_HEADS * HEAD_BLK), F32),
                   jax.ShapeDtypeStruct((s, MLA_HEADS * HEAD_BLK), F32),
                   jax.ShapeDtypeStruct((s, D_MLA), MXU_DTYPE)],
        compiler_params=_cp(("parallel", "arbitrary"), 56),
    )(q, do, lse, delta, k, v)


def _mla_prep_bwd(dq, dk, dv, z, gq, gkv, wuq, wukv, c_t, s1_t, s2_t, l):
    s = z.shape[0]
    t = _tile(s)
    hq = MLA_HEADS * HEAD_BLK
    lay = _lay(l)

    def body(dq_ref, dk_ref, dv_ref, cq_ref, ckv_ref, gq_ref, gkv_ref, wuq_ref, wukv_ref, c_ref, s1_ref, s2_ref,
             dcq_ref, dckv_ref, dkr_ref, dqu_ref, dkvp_ref, dgq_ref, dgkv_ref):
        @pl.when(pl.program_id(0) == 0)
        def _():
            dgq_ref[...] = jnp.zeros_like(dgq_ref)
            dgkv_ref[...] = jnp.zeros_like(dgkv_ref)

        ct, s1, s2 = c_ref[...], s1_ref[...], s2_ref[...]
        dk_sum = jnp.zeros((t, HEAD_BLK), F32)
        for h in range(MLA_HEADS):
            blk = slice(h * HEAD_BLK, (h + 1) * HEAD_BLK)
            dqu_ref[:, blk] = _unrope(dq_ref[:, blk] * ATT_SCALE, ct, s1, s2).astype(dqu_ref.dtype)
            dkh = dk_ref[:, blk]
            dk_sum = dk_sum + dkh
            dkvp_ref[:, blk] = dkh.astype(dkvp_ref.dtype)
        dkvp_ref[:, hq:] = dv_ref[...]
        lane = _lanes((t, HEAD_BLK))
        rope_lanes = (lane >= KR_LANE0) & (lane < KR_LANE0 + QK_ROPE)
        dkr_ref[...] = _unrope(jnp.where(rope_lanes, dk_sum, 0.0), ct, s1, s2).astype(dkr_ref.dtype)

        def norm_bwd(c_in, g, dn_out, dc_ref, dg_ref):
            rs = lax.rsqrt(jnp.mean(c_in * c_in, axis=-1, keepdims=True) + EPS)
            n = c_in * rs
            dg_ref[...] += jnp.sum(dn_out * n, axis=0, keepdims=True)
            dn = dn_out * g
            dc_ref[...] = (rs * (dn - n * jnp.mean(dn * n, axis=-1, keepdims=True))).astype(dc_ref.dtype)

        norm_bwd(cq_ref[...], gq_ref[...], _dot_nt(dqu_ref[...], wuq_ref[...]), dcq_ref, dgq_ref)
        norm_bwd(ckv_ref[...], gkv_ref[...], _dot_nt(dkvp_ref[...], wukv_ref[...]), dckv_ref, dgkv_ref)

    full = lambda shp: pl.BlockSpec(shp, lambda i: (0, 0))
    tab = pl.BlockSpec((t, HEAD_BLK), lambda i: (i, 0))
    row = lambda w: pl.BlockSpec((t, w), lambda i: (i, 0))
    return pl.pallas_call(
        body, name="mla_prep_bwd", grid=(s // t,),
        in_specs=[row(hq), row(hq), row(D_MLA),
                  pl.BlockSpec((t, Q_RANK), lambda i: (i, 2)), pl.BlockSpec((t, KV_RANK), lambda i: (i, 6)),
                  lay((1, Q_RANK)), lay((1, KV_RANK)), lay((Q_RANK, hq)), lay((KV_RANK, hq + D_MLA)),
                  tab, tab, tab],
        out_specs=[row(Q_RANK), row(KV_RANK), row(HEAD_BLK), row(hq), row(hq + D_MLA),
                   full((1, Q_RANK)), full((1, KV_RANK))],
        out_shape=[jax.ShapeDtypeStruct((s, Q_RANK), MXU_DTYPE), jax.ShapeDtypeStruct((s, KV_RANK), MXU_DTYPE),
                   jax.ShapeDtypeStruct((s, HEAD_BLK), MXU_DTYPE), jax.ShapeDtypeStruct((s, hq), MXU_DTYPE),
                   jax.ShapeDtypeStruct((s, hq + D_MLA), MXU_DTYPE),
                   jax.ShapeDtypeStruct((1, Q_RANK), F32), jax.ShapeDtypeStruct((1, KV_RANK), F32)],
        compiler_params=_cp(("arbitrary",), 48),
    )(dq, dk, dv, z, z, gq, gkv, wuq, wukv, c_t, s1_t, s2_t)


def _outproj_fwd(x, ya, yb, yc, w, l):
    s = x.shape[0]
    t = _tile(s)
    lay = _lay(l)

    def body(x_ref, ya_ref, yb_ref, yc_ref, wa_ref, wb_ref, wc_ref, o_ref):
        o_ref[...] = (x_ref[...] + _dot(ya_ref[...], wa_ref[...]) + _dot(yb_ref[...], wb_ref[...])
                      + _dot(yc_ref[...], wc_ref[...]))

    row = lambda w_: pl.BlockSpec((t, w_), lambda i: (i, 0))
    return pl.pallas_call(
        body, name="outproj_fwd", grid=(s // t,),
        in_specs=[row(D_MODEL), row(D_LRU), row(D_MLA), row(D_POOL),
                  lay((D_LRU, D_MODEL), 0), lay((D_MLA, D_MODEL), 1), lay((D_POOL, D_MODEL), 3)],
        out_specs=row(D_MODEL),
        out_shape=jax.ShapeDtypeStruct((s, D_MODEL), F32),
        compiler_params=_cp(("parallel",), 40),
    )(x, ya, yb, yc, w, w, w)


def _outproj_bwd(dx, ya, yb, yc, w, l):
    s = dx.shape[0]
    t = _tile(s)

    def body(dx_ref, ya_ref, yb_ref, yc_ref, w_ref, dy_ref, dw_ref):
        @pl.when(pl.program_id(0) == 0)
        def _():
            dw_ref[...] = jnp.zeros_like(dw_ref)

        dxv = dx_ref[...].astype(MXU_DTYPE)
        dy_ref[...] = _dot_nt(dxv, w_ref[...])
        dw_ref[0:D_LRU, :] += _dot_tn(ya_ref[...], dxv)
        dw_ref[D_LRU:D_LRU + D_MLA, :] += _dot_tn(yb_ref[...], dxv)
        dw_ref[D_LRU + D_MLA:D_MIX, :] += _dot_tn(yc_ref[...], dxv)

    row = lambda w_: pl.BlockSpec((t, w_), lambda i: (i, 0))
    return pl.pallas_call(
        body, name="outproj_bwd", grid=(s // t,),
        in_specs=[row(D_MODEL), row(D_LRU), row(D_MLA), row(D_POOL), _lay(l)((D_MIX, D_MODEL))],
        out_specs=[row(D_MIX), pl.BlockSpec((D_MIX, D_MODEL), lambda i: (0, 0))],
        out_shape=[jax.ShapeDtypeStruct((s, D_MIX), F32), jax.ShapeDtypeStruct((D_MIX, D_MODEL), F32)],
        compiler_params=_cp(("arbitrary",), 48),
    )(dx, ya, yb, yc, w)


def _mm_tn(a, b, name):
    s, k1 = a.shape
    n = b.shape[1]
    t = _tile(s)

    def body(a_ref, b_ref, o_ref):
        @pl.when(pl.program_id(0) == 0)
        def _():
            o_ref[...] = jnp.zeros_like(o_ref)

        o_ref[...] += _dot_tn(a_ref[...], b_ref[...])

    return pl.pallas_call(
        body, name=name, grid=(s // t,),
        in_specs=[pl.BlockSpec((t, k1), lambda i: (i, 0)), pl.BlockSpec((t, n), lambda i: (i, 0))],
        out_specs=pl.BlockSpec((k1, n), lambda i: (0, 0)),
        out_shape=jax.ShapeDtypeStruct((k1, n), F32),
        compiler_params=_cp(("arbitrary",), 56),
    )(a, b)


def _inproj_bwd(dz, w, x, g, dxn, l):
    s = x.shape[0]
    t = _tile(s)
    lay = _lay(l)

    def body(dz_ref, w_ref, x_ref, g_ref, dxn_ref, dx_ref, dg_ref):
        @pl.when(pl.program_id(0) == 0)
        def _():
            dg_ref[...] = jnp.zeros_like(dg_ref)

        dh = _dot_nt(dz_ref[...], w_ref[...])
        xv = x_ref[...]
        rs = lax.rsqrt(jnp.mean(xv * xv, axis=-1, keepdims=True) + EPS)
        n = xv * rs
        dg_ref[...] += jnp.sum(dh * n, axis=0, keepdims=True)
        dn = dh * g_ref[...]
        dx_ref[...] = dxn_ref[...] + rs * (dn - n * jnp.mean(dn * n, axis=-1, keepdims=True))

    row = lambda w_: pl.BlockSpec((t, w_), lambda i: (i, 0))
    return pl.pallas_call(
        body, name="inproj_bwd", grid=(s // t,),
        in_specs=[row(D_INP), lay((D_MODEL, D_INP)), row(D_MODEL), lay((1, D_MODEL)), row(D_MODEL)],
        out_specs=[row(D_MODEL), pl.BlockSpec((1, D_MODEL), lambda i: (0, 0))],
        out_shape=[jax.ShapeDtypeStruct((s, D_MODEL), F32), jax.ShapeDtypeStruct((1, D_MODEL), F32)],
        compiler_params=_cp(("arbitrary",), 48),
    )(dz, w, x, g, dxn)


def _loss_head(x, g, tgt):
    s = x.shape[0]
    t = _tile(s)

    def body(x_ref, g_ref, t_ref, dx_ref, loss_ref, dg_ref):
        @pl.when(pl.program_id(0) == 0)
        def _():
            loss_ref[...] = jnp.zeros_like(loss_ref)
            dg_ref[...] = jnp.zeros_like(dg_ref)

        xv = x_ref[...]
        rs = lax.rsqrt(jnp.mean(xv * xv, axis=-1, keepdims=True) + EPS)
        n = xv * rs
        gv = g_ref[...]
        e = n * gv - t_ref[...]
        loss_ref[...] += 0.5 * jnp.sum(jnp.mean(e * e, axis=-1, keepdims=True))
        dyf = e * (1.0 / D_MODEL)
        dg_ref[...] += jnp.sum(dyf * n, axis=0, keepdims=True)
        dn = dyf * gv
        dx_ref[...] = rs * (dn - n * jnp.mean(dn * n, axis=-1, keepdims=True))

    row = pl.BlockSpec((t, D_MODEL), lambda i: (i, 0))
    vec = pl.BlockSpec((1, D_MODEL), lambda i: (0, 0))
    return pl.pallas_call(
        body, name="loss_head", grid=(s // t,),
        in_specs=[row, vec, row],
        out_specs=[row, pl.BlockSpec((1, LANES), lambda i: (0, 0)), vec],
        out_shape=[jax.ShapeDtypeStruct((s, D_MODEL), F32), jax.ShapeDtypeStruct((1, LANES), F32),
                   jax.ShapeDtypeStruct((1, D_MODEL), F32)],
        compiler_params=_cp(("arbitrary",), 32),
    )(x, g, tgt)


def _block_diag(w):
    n, h, d, _ = w.shape
    return jnp.einsum('lhij,hk->lhikj', w, jnp.eye(h, dtype=w.dtype)).reshape(n, h * d, h * d)


def _diag_blocks(wfull, h):
    d = wfull.shape[-1] // h
    return jnp.stack([wfull[:, i * d:(i + 1) * d, i * d:(i + 1) * d] for i in range(h)], axis=1)


REF_TO_PERM = np.concatenate([np.arange(0, 1152), np.arange(1536, 1792),
                              np.arange(2304 + KR_LANE0, 2304 + KR_LANE0 + QK_ROPE),
                              np.arange(1152, 1536), np.arange(1792, 2304)])
N_SHARD = 4
W_IN_SHARD = D_IN // N_SHARD


def _w_in_runs():
    ref_of_perm = -np.ones(D_INP, np.int64)
    ref_of_perm[REF_TO_PERM] = np.arange(D_IN)
    perm_runs, p = [], 0
    while p < D_INP:
        r, q = ref_of_perm[p], p + 1
        if r < 0:
            while q < D_INP and ref_of_perm[q] < 0:
                q += 1
            perm_runs.append((None, q - p, 0))
        else:
            while (q < D_INP and ref_of_perm[q] == ref_of_perm[q - 1] + 1
                   and ref_of_perm[q] // W_IN_SHARD == r // W_IN_SHARD):
                q += 1
            perm_runs.append((int(r // W_IN_SHARD), int(r % W_IN_SHARD), int(r % W_IN_SHARD + q - p)))
        p = q
    shard_runs = []
    for s in range(N_SHARD):
        cols = REF_TO_PERM[s * W_IN_SHARD:(s + 1) * W_IN_SHARD]
        runs, a = [], 0
        for b in range(1, W_IN_SHARD + 1):
            if b == W_IN_SHARD or cols[b] != cols[b - 1] + 1:
                runs.append((int(cols[a]), int(cols[b - 1]) + 1))
                a = b
        shard_runs.append(runs)
    return perm_runs, shard_runs


def _permute_w_in(shards):
    perm_runs, _ = _w_in_runs()
    lead = shards[0].shape[:-1]
    parts = [jnp.zeros(lead + (a,), shards[0].dtype) if s is None else shards[s][..., a:b] for s, a, b in perm_runs]
    return jnp.concatenate(parts, axis=-1)


def _w_in_shard(wp, s):
    _, shard_runs = _w_in_runs()
    return jnp.concatenate([wp[..., a:b] for a, b in shard_runs[s]], axis=-1)


def _pad_w_uq(w):
    w4 = w.reshape(w.shape[:2] + (MLA_HEADS, QK_NOPE + QK_ROPE))
    return jnp.pad(w4, ((0, 0),) * 3 + ((0, HEAD_BLK - QK_NOPE - QK_ROPE),)).reshape(w.shape[:2] + (-1,))


def _unpad_w_uq(w):
    return w.reshape(w.shape[:2] + (MLA_HEADS, HEAD_BLK))[..., :QK_NOPE + QK_ROPE].reshape(w.shape[:2] + (-1,))


def _pad_w_ukv(w):
    w4 = w.reshape(w.shape[:2] + (MLA_HEADS, QK_NOPE + V_DIM))
    kpart = jnp.pad(w4[..., :QK_NOPE], ((0, 0),) * 3 + ((0, HEAD_BLK - QK_NOPE),)).reshape(w.shape[:2] + (-1,))
    return jnp.concatenate([kpart, w4[..., QK_NOPE:].reshape(w.shape[:2] + (-1,))], axis=2)


def _unpad_w_ukv(w):
    hq = MLA_HEADS * HEAD_BLK
    kpart = w[..., :hq].reshape(w.shape[:2] + (MLA_HEADS, HEAD_BLK))[..., :QK_NOPE]
    vpart = w[..., hq:].reshape(w.shape[:2] + (MLA_HEADS, V_DIM))
    return jnp.concatenate([kpart, vpart], axis=3).reshape(w.shape[:2] + (-1,))


def _local_step(x, tgt, w):
    s = x.shape[0]
    tabs = _rope_tables(s)
    vec = lambda a: a[:, None, :]
    mxu = lambda a: a.astype(MXU_DTYPE)
    p = dict(g=vec(w['norm_g']), w_in=mxu(w['w_in']), cw=w['conv_w'], cb=vec(w['conv_b']),
             wr=mxu(_block_diag(w['w_rg'])), br=vec(w['b_rg']), wi=mxu(_block_diag(w['w_ig'])), bi=vec(w['b_ig']),
             lam=vec(w['lru_lambda']), gq=vec(w['q_norm_g']), gkv=vec(w['kv_norm_g']),
             wuq=mxu(_pad_w_uq(w['w_uq'])), wukv=mxu(_pad_w_ukv(w['w_ukv'])),
             wp=mxu(_block_diag(w['w_pool'])), ps=vec(w['pool_scale']), wout=mxu(w['w_out']))
    lru = lambda l: (p['cw'], p['cb'], p['wr'], p['br'], p['wi'], p['bi'], p['lam'], l)
    mla = lambda l: (p['gq'], p['gkv'], p['wuq'], p['wukv'], *tabs, l)

    saved = []
    for l in range(DEPTH):
        h, z = _inproj_fwd(x, p['g'], p['w_in'], l)
        ya, hs = _lru_fwd(z, *lru(l))
        yc = _pool_fwd(z, p['wp'], p['ps'], l)
        q, k, v, qn, kvn = _mla_prep_fwd(z, *mla(l))
        o, yb, lse = _flash_fwd(q, k, v, z)
        saved.append(dict(x=x, h=h, z=z, hs=hs, ya=ya, yb=yb, yc=yc, q=q, k=k, v=v, qn=qn, kvn=kvn, o=o, lse=lse))
        x = _outproj_fwd(x, ya, yb, yc, p['wout'], l)

    dx, loss, dgf = _loss_head(x, w['final_norm_g'][None], tgt)
    per_layer = {n: [None] * DEPTH for n in WEIGHT_NAMES if n != 'final_norm_g'}
    for l in reversed(range(DEPTH)):
        sv = saved[l]
        dy, per_layer['w_out'][l] = _outproj_bwd(dx, sv['ya'], sv['yb'], sv['yc'], p['wout'], l)
        dza, dga, *lru_grads = _lru_bwd(sv['z'], sv['hs'], dy, *lru(l))
        for n, g in zip(('w_rg', 'w_ig', 'conv_w', 'conv_b', 'b_rg', 'b_ig', 'lru_lambda'), lru_grads):
            per_layer[n][l] = g
        dzc, dgc, per_layer['w_pool'][l], per_layer['pool_scale'][l] = _pool_bwd(sv['z'], dy, p['wp'], p['ps'], l)
        do, dgb, delta = _attn_bwd_prep(dy, sv['o'], sv['z'])
        dq, dk, dv = _flash_bwd(sv['q'], sv['k'], sv['v'], do, sv['lse'], delta)
        dcq, dckv, dkr, dqu, dkvp, per_layer['q_norm_g'][l], per_layer['kv_norm_g'][l] = _mla_prep_bwd(
            dq, dk, dv, sv['z'], *mla(l))
        dz = jnp.concatenate([dza, dga, dcq, dgb, dckv, dzc, dgc, dkr], axis=1)
        per_layer['w_in'][l] = _mm_tn(sv['h'], dz, "dwin")
        per_layer['w_uq'][l] = _mm_tn(sv['qn'], dqu, "dwuq")
        per_layer['w_ukv'][l] = _mm_tn(sv['kvn'], dkvp, "dwukv")
        dx, per_layer['norm_g'][l] = _inproj_bwd(dz, p['w_in'], sv['x'], p['g'], dx, l)
    grads = {n: jnp.stack(g) for n, g in per_layer.items()}
    for n in ('norm_g', 'conv_b', 'b_rg', 'b_ig', 'lru_lambda', 'q_norm_g', 'kv_norm_g', 'pool_scale'):
        grads[n] = grads[n][:, 0, :]
    grads['w_rg'] = _diag_blocks(grads['w_rg'], LRU_HEADS)
    grads['w_ig'] = _diag_blocks(grads['w_ig'], LRU_HEADS)
    grads['w_pool'] = _diag_blocks(grads['w_pool'], len(POOL_WINDOWS))
    grads['w_uq'] = _unpad_w_uq(grads['w_uq'])
    grads['w_ukv'] = _unpad_w_ukv(grads['w_ukv'])
    grads['final_norm_g'] = dgf[0]
    return loss[0, 0], dx, grads


WIRE_DTYPE = jnp.bfloat16
MESH_IDS = pl.DeviceIdType.MESH
_HBM = pl.BlockSpec(memory_space=pltpu.HBM)


def _coords():
    return lax.axis_index("x"), lax.axis_index("y"), lax.axis_index("c")


def _comm_call(body, name, arrays, out_shapes, copies_per_array):
    n = len(arrays)
    return pl.pallas_call(
        body, name=name, out_shape=out_shapes, in_specs=[_HBM] * n, out_specs=[_HBM] * n,
        scratch_shapes=[pltpu.SemaphoreType.DMA((n, copies_per_array)), pltpu.SemaphoreType.DMA((n, copies_per_array))],
    )(*arrays)


def _all_gather8(blocks, name):
    n = len(blocks)
    every = range(n)

    def body(*refs):
        x_refs, out_refs = refs[:n], refs[n:2 * n]
        send_sems, recv_sems = refs[2 * n:]
        x, y, c = _coords()
        me, sibling = (x, y, c), (x, y, 1 - c)
        chips = [(1 - x, y), (x, 1 - y), (1 - x, 1 - y)]

        def slot(t, px, py, pc):
            return out_refs[t].at[4 * px + 2 * py + pc]

        def copy(t, k, block, to, own=False):
            return pltpu.make_async_remote_copy(
                src_ref=x_refs[t] if own else slot(t, *block), dst_ref=slot(t, *block),
                send_sem=send_sems.at[t, k], recv_sem=recv_sems.at[t, k], device_id=to, device_id_type=MESH_IDS)

        first = [copy(t, 0, me, sibling, own=True) for t in every]
        first += [copy(t, 1 + j, me, (*chip, c), own=True) for j, chip in enumerate(chips) for t in every]
        for cp in first:
            cp.start()
        passed = [[copy(t, 4 + j, (*chip, c), sibling) for t in every] for j, chip in enumerate(chips)]
        for j, chip in enumerate(chips):
            for t in every:
                copy(t, 1 + j, (*chip, c), me).wait_recv()
                passed[j][t].start()
        for t in every:
            copy(t, 0, sibling, me).wait_recv()
        for j, chip in enumerate(chips):
            for t in every:
                copy(t, 4 + j, (*chip, 1 - c), me).wait_recv()
        for cp in first + [cp for group in passed for cp in group]:
            cp.wait_send()

    outs = [jax.ShapeDtypeStruct((N_DEV,) + b.shape, b.dtype) for b in blocks]
    got = _comm_call(body, name, blocks, outs, 7)
    me = 4 * lax.axis_index("x") + 2 * lax.axis_index("y") + lax.axis_index("c")
    return [lax.dynamic_update_index_in_dim(g, b, me, 0) for g, b in zip(got, blocks)]


def _sibling_send(arrs, name, which_half=False):
    n = len(arrs)

    def body(*refs):
        a_refs, out_refs = refs[:n], refs[n:2 * n]
        send_sems, recv_sems = refs[2 * n:]
        x, y, c = _coords()
        sent = [pltpu.make_async_remote_copy(
            src_ref=a_refs[t].at[1 - c] if which_half else a_refs[t], dst_ref=out_refs[t],
            send_sem=send_sems.at[t, 0], recv_sem=recv_sems.at[t, 0],
            device_id=(x, y, 1 - c), device_id_type=MESH_IDS) for t in range(n)]
        for cp in sent:
            cp.start()
        for cp in sent:
            cp.wait()

    shapes = [jax.ShapeDtypeStruct(a.shape[1:] if which_half else a.shape, a.dtype) for a in arrs]
    return _comm_call(body, name, arrs, shapes, 1)


def _chip_exchange(arrs, name):
    n = len(arrs)

    def body(*refs):
        a_refs, out_refs = refs[:n], refs[n:2 * n]
        send_sems, recv_sems = refs[2 * n:]
        x, y, c = _coords()
        copies = []
        for j, (cx, cy) in enumerate([(1 - x, y), (x, 1 - y), (1 - x, 1 - y)]):
            copies += [pltpu.make_async_remote_copy(
                src_ref=a_refs[t].at[2 * cx + cy], dst_ref=out_refs[t].at[j], send_sem=send_sems.at[t, j],
                recv_sem=recv_sems.at[t, j], device_id=(cx, cy, c), device_id_type=MESH_IDS) for t in range(n)]
        for cp in copies:
            cp.start()
        for cp in copies:
            cp.wait()

    return _comm_call(body, name, arrs, [jax.ShapeDtypeStruct((3,) + a.shape[1:], a.dtype) for a in arrs], 3)


def _sum_leading(groups, out_dtype, steps, name):
    flat = [a for g in groups for a in g]

    def body(*refs):
        ins, outs, pos = refs[:len(flat)], refs[len(flat):], 0
        for g, o_ref in zip(groups, outs):
            acc = None
            for i_ref in ins[pos:pos + len(g)]:
                for k in range(i_ref.shape[0]):
                    term = i_ref[k].astype(F32)
                    acc = term if acc is None else acc + term
            pos += len(g)
            o_ref[...] = acc.astype(o_ref.dtype)

    return pl.pallas_call(
        body, name=name, grid=(steps,),
        in_specs=[pl.BlockSpec((a.shape[0], a.shape[1] // steps, a.shape[2]), lambda i: (0, i, 0)) for a in flat],
        out_specs=[pl.BlockSpec((g[0].shape[1] // steps, g[0].shape[2]), lambda i: (i, 0)) for g in groups],
        out_shape=[jax.ShapeDtypeStruct(g[0].shape[1:], out_dtype) for g in groups],
        compiler_params=_cp(("parallel",), 40),
    )(*flat)


def _adamw_update(w_ref, g_ref, m_ref, v_ref, d_ref, mo_ref, vo_ref):
    gv = g_ref[...]
    mn = ADAM_B1 * m_ref[...] + (1.0 - ADAM_B1) * gv
    vn = ADAM_B2 * v_ref[...] + (1.0 - ADAM_B2) * (gv * gv)
    mo_ref[...] = mn
    vo_ref[...] = vn
    m_hat = mn / (1.0 - ADAM_B1 ** ADAM_STEP)
    v_hat = vn / (1.0 - ADAM_B2 ** ADAM_STEP)
    d_ref[...] = (-ADAM_LR) * (m_hat / (jnp.sqrt(v_hat) + ADAM_EPS) + ADAM_WD * w_ref[...])


def _adamw(w, g, m, v, name):
    r, cdim = w.shape
    tr = math.gcd(r, 512)

    def body(*refs):
        _adamw_update(*refs)

    blk = pl.BlockSpec((tr, cdim), lambda i: (i, 0))
    return pl.pallas_call(
        body, name=name, grid=(r // tr,),
        in_specs=[blk] * 4, out_specs=[blk] * 3,
        out_shape=[jax.ShapeDtypeStruct((r, cdim), F32)] * 3,
        compiler_params=_cp(("parallel",), 40),
    )(w, g, m, v)


def _adamw_small(ws, gs, ms, vs, name):
    n = len(ws)

    def body(*refs):
        ins, outs = refs[:4 * n], refs[4 * n:]
        for t in range(n):
            _adamw_update(ins[t], ins[n + t], ins[2 * n + t], ins[3 * n + t], outs[t], outs[n + t], outs[2 * n + t])

    shapes = [jax.ShapeDtypeStruct(w.shape, F32) for w in ws]
    outs = pl.pallas_call(body, name=name, out_shape=shapes * 3)(*ws, *gs, *ms, *vs)
    return outs[:n], outs[n:2 * n], outs[2 * n:]


HALF = DEPTH // 2
BIG = ['w_in', 'w_uq', 'w_ukv', 'w_out']
SHARD_AXIS = {'w_in': 2, 'conv_w': 2, 'w_uq': 2, 'w_ukv': 2, 'w_out': 1}
FULL_SHAPE = {'w_in': (DEPTH, D_MODEL, D_IN), 'conv_w': (DEPTH, CONV_WIDTH, D_LRU),
              'w_uq': (DEPTH, Q_RANK, MLA_HEADS * (QK_NOPE + QK_ROPE)),
              'w_ukv': (DEPTH, KV_RANK, MLA_HEADS * (QK_NOPE + V_DIM)), 'w_out': (DEPTH, D_MIX, D_MODEL)}


def _shard_shape(n):
    shp = list(FULL_SHAPE[n])
    shp[SHARD_AXIS[n]] //= N_SHARD
    return tuple(shp)


def _rows_view(a, lead=0):
    return a.reshape(a.shape[:lead] + (-1, a.shape[-1]))


def _gather_weights(local):
    c = lax.axis_index("c")
    names = BIG + ['conv_w']
    halves = [lax.dynamic_slice_in_dim(local[n], HALF * c, HALF, axis=0) for n in names]
    halves = [h.astype(WIRE_DTYPE) if n in BIG else h for n, h in zip(names, halves)]
    got = _all_gather8(halves, "gather_weights")
    full = {}
    for n, g in zip(names, got):
        g = g.reshape((N_SHARD, DEPTH) + g.shape[2:])
        if n == 'w_in':
            full[n] = _permute_w_in([g[s] for s in range(N_SHARD)])
        else:
            full[n] = jnp.moveaxis(g, 0, SHARD_AXIS[n]).reshape(FULL_SHAPE[n])
    return full


def _shard_blocks(g, n):
    width = _shard_shape(n)[SHARD_AXIS[n]]

    def block(h, s):
        part = g[HALF * h:HALF * (h + 1)]
        if n == 'w_in':
            part = _w_in_shard(part, s)
        else:
            part = lax.slice_in_dim(part, s * width, (s + 1) * width, axis=SHARD_AXIS[n])
        return _rows_view(part)

    return jnp.stack([jnp.stack([block(h, s) for s in range(N_SHARD)]) for h in range(2)]).astype(WIRE_DTYPE)


SUM_STEPS = 8


def _reduce_big(grads):
    c = lax.axis_index("c")
    shard = 2 * lax.axis_index("x") + lax.axis_index("y")
    contrib = [_shard_blocks(grads[n], n) for n in BIG]
    from_sibling = _sibling_send(contrib, "pair_exchange_big", which_half=True)
    own_half = [lax.dynamic_index_in_dim(a, c, 0, keepdims=False) for a in contrib]
    pair_sum = _sum_leading([[_rows_view(o)[None], _rows_view(r)[None]] for o, r in zip(own_half, from_sibling)],
                            WIRE_DTYPE, SUM_STEPS, "pair_sum_big")
    to_chips = [p.reshape(a.shape[1:]) for p, a in zip(pair_sum, contrib)]
    from_chips = _chip_exchange(to_chips, "chip_exchange_big")
    own_block = [lax.dynamic_index_in_dim(a, shard, 0, keepdims=True) for a in to_chips]
    mine = _sum_leading([[o, r] for o, r in zip(own_block, from_chips)], F32, SUM_STEPS, "chip_sum_big")
    theirs = _sibling_send(mine, "sibling_big")
    both = [jnp.where(c == 0, jnp.stack([m, t]), jnp.stack([t, m])) for m, t in zip(mine, theirs)]
    return {n: b.reshape(_shard_shape(n)) for n, b in zip(BIG, both)}


SMALL = REPLICATED + ['conv_w']


def _reduce_small(grads):
    views = [_rows_view(jnp.atleast_2d(grads[n])) for n in SMALL]
    sums = _sum_leading([[g] for g in _all_gather8(views, "gather_small")], F32, 1, "sum_small")
    return {n: s.reshape(grads[n].shape) for n, s in zip(SMALL, sums)}


def kernel(x, norm_g, w_in, conv_w, conv_b, w_rg, b_rg, w_ig, b_ig, lru_lambda, q_norm_g, w_uq, kv_norm_g, w_ukv, w_pool, pool_scale, w_out, final_norm_g, loss_target, m_norm_g, m_w_in, m_conv_w, m_conv_b, m_w_rg, m_b_rg, m_w_ig, m_b_ig, m_lru_lambda, m_q_norm_g, m_w_uq, m_kv_norm_g, m_w_ukv, m_w_pool, m_pool_scale, m_w_out, m_final_norm_g, v_norm_g, v_w_in, v_conv_w, v_conv_b, v_w_rg, v_b_rg, v_w_ig, v_b_ig, v_lru_lambda, v_q_norm_g, v_w_uq, v_kv_norm_g, v_w_ukv, v_w_pool, v_pool_scale, v_w_out, v_final_norm_g):
    w_loc = dict(zip(WEIGHT_NAMES, (norm_g, w_in, conv_w, conv_b, w_rg, b_rg, w_ig, b_ig, lru_lambda, q_norm_g, w_uq,
                                    kv_norm_g, w_ukv, w_pool, pool_scale, w_out, final_norm_g)))
    m_loc = dict(zip(WEIGHT_NAMES, (m_norm_g, m_w_in, m_conv_w, m_conv_b, m_w_rg, m_b_rg, m_w_ig, m_b_ig, m_lru_lambda,
                                    m_q_norm_g, m_w_uq, m_kv_norm_g, m_w_ukv, m_w_pool, m_pool_scale, m_w_out,
                                    m_final_norm_g)))
    v_loc = dict(zip(WEIGHT_NAMES, (v_norm_g, v_w_in, v_conv_w, v_conv_b, v_w_rg, v_b_rg, v_w_ig, v_b_ig, v_lru_lambda,
                                    v_q_norm_g, v_w_uq, v_kv_norm_g, v_w_ukv, v_w_pool, v_pool_scale, v_w_out,
                                    v_final_norm_g)))
    w_full = dict(w_loc)
    w_full.update(_gather_weights(w_loc))
    loss_local, dx, g_local = _local_step(x[0], loss_target[0], w_full)
    loss = lax.psum(loss_local, ("x", "y", "c"))

    grads = _reduce_big(g_local)
    g_small = _reduce_small(g_local)
    shard = 2 * lax.axis_index("x") + lax.axis_index("y")
    width = D_LRU // N_SHARD
    grads['conv_w'] = lax.dynamic_slice_in_dim(g_small['conv_w'], shard * width, width, axis=2)
    for n in REPLICATED:
        grads[n] = g_small[n]

    delta, new_m, new_v = {}, {}, {}
    for n in BIG:
        d, mo, vo = _adamw(*[_rows_view(t[n]) for t in (w_loc, grads, m_loc, v_loc)], "adamw_" + n)
        delta[n], new_m[n], new_v[n] = (a.reshape(w_loc[n].shape) for a in (d, mo, vo))
    small = [[_rows_view(jnp.atleast_2d(t[n])) for n in SMALL] for t in (w_loc, grads, m_loc, v_loc)]
    for tree, outs in zip((delta, new_m, new_v), _adamw_small(*small, "adamw_small")):
        tree.update({n: a.reshape(w_loc[n].shape) for n, a in zip(SMALL, outs)})

    return (loss, dx[None], *[grads[n] for n in WEIGHT_NAMES], *[delta[n] for n in WEIGHT_NAMES],
            *[new_m[n] for n in WEIGHT_NAMES], *[new_v[n] for n in WEIGHT_NAMES])
```

```python
import functools
import math

import jax
import jax.numpy as jnp
import numpy as np
from jax import lax
from jax.experimental import pallas as pl
from jax.experimental.pallas import tpu as pltpu

F32 = jnp.float32
MXU_DTYPE = jnp.bfloat16

D_MODEL = 1024
DEPTH = 4
EPS = 1e-6
D_LRU = 384
LRU_HEADS = 6
CONV_WIDTH = 4
LRU_C = 8.0
MLA_HEADS = 6
QK_NOPE = 64
QK_ROPE = 32
V_DIM = 64
D_MLA = MLA_HEADS * V_DIM
Q_RANK = 384
KV_RANK = 256
ROPE_BASE = 10000.0
POOL_WINDOWS = (2, 4, 8, 16)
D_POOL = 256
D_MIX = D_LRU + D_MLA + D_POOL
D_IN = 2336
ATT_SCALE = (QK_NOPE + QK_ROPE) ** -0.5
LOG2_E = 1.4426950408889634
LN_2 = 0.6931471805599453
Q_PRESCALE = ATT_SCALE * LOG2_E

ADAM_LR = 0.001
ADAM_B1 = 0.9
ADAM_B2 = 0.999
ADAM_EPS = 1e-08
ADAM_WD = 0.01
ADAM_STEP = 10

LANES = 128
SUBLANES = 8
V7X_VMEM_BYTES = 64 << 20
N_DEV = 8

D_INP = 2432
KR_LANE0 = 64
HEAD_BLK = 128
N_PAIR = MLA_HEADS // 2

WEIGHT_NAMES = ['norm_g', 'w_in', 'conv_w', 'conv_b', 'w_rg', 'b_rg', 'w_ig', 'b_ig', 'lru_lambda', 'q_norm_g',
                'w_uq', 'kv_norm_g', 'w_ukv', 'w_pool', 'pool_scale', 'w_out', 'final_norm_g']
SHARDED = ['w_in', 'conv_w', 'w_uq', 'w_ukv', 'w_out']
REPLICATED = [n for n in WEIGHT_NAMES if n not in SHARDED]


def _cp(sem, vmem_mb=None):
    return pltpu.CompilerParams(dimension_semantics=sem,
                                vmem_limit_bytes=None if vmem_mb is None else vmem_mb << 20)


def _dot(a, b):
    return jnp.dot(a.astype(MXU_DTYPE), b.astype(MXU_DTYPE), preferred_element_type=F32)


def _dot_nt(a, b):
    return lax.dot_general(a.astype(MXU_DTYPE), b.astype(MXU_DTYPE), (((1,), (1,)), ((), ())),
                           preferred_element_type=F32)


def _dot_tn(a, b):
    return lax.dot_general(a.astype(MXU_DTYPE), b.astype(MXU_DTYPE), (((0,), (0,)), ((), ())),
                           preferred_element_type=F32)


def _sig(x):
    return 1.0 / (1.0 + jnp.exp(-x))


def _down(x, k):
    return pltpu.roll(x, k, 0)


def _up(x, k):
    return pltpu.roll(x, x.shape[0] - k, 0)


def _rows(shape):
    return lax.broadcasted_iota(jnp.int32, shape, 0)


def _lanes(shape):
    return lax.broadcasted_iota(jnp.int32, shape, 1)


def _tile(s):
    return min(512, s)


def _lay(l):
    return lambda shp, blk=0: pl.BlockSpec((None,) + shp, lambda *_: (l, blk, 0))


def _inproj_fwd(x, g, w, l):
    s = x.shape[0]
    t = _tile(s)
    lay = _lay(l)

    def body(x_ref, g_ref, w_ref, h_ref, z_ref):
        xv = x_ref[...]
        rs = lax.rsqrt(jnp.mean(xv * xv, axis=-1, keepdims=True) + EPS)
        h = (xv * rs * g_ref[...]).astype(MXU_DTYPE)
        h_ref[...] = h
        z_ref[...] = jnp.dot(h, w_ref[...], preferred_element_type=F32)

    return pl.pallas_call(
        body, name="inproj_fwd", grid=(s // t,),
        in_specs=[pl.BlockSpec((t, D_MODEL), lambda i: (i, 0)),
                  lay((1, D_MODEL)), lay((D_MODEL, D_INP))],
        out_specs=[pl.BlockSpec((t, D_MODEL), lambda i: (i, 0)),
                   pl.BlockSpec((t, D_INP), lambda i: (i, 0))],
        out_shape=[jax.ShapeDtypeStruct((s, D_MODEL), MXU_DTYPE), jax.ShapeDtypeStruct((s, D_INP), F32)],
        compiler_params=_cp(("parallel",), 40),
    )(x, g, w)


def _lru_gates(za, halo, cw_ref, cb_ref, wr_ref, br_ref, wi_ref, bi_ref, lam_ref):
    t = za.shape[0]
    ext = jnp.concatenate([halo, za], axis=0)
    sh = [za] + [_down(ext, j)[SUBLANES:SUBLANES + t] for j in (1, 2, 3)]
    xa = cb_ref[...] + cw_ref[3:4, :] * sh[0] + cw_ref[2:3, :] * sh[1] + cw_ref[1:2, :] * sh[2] + cw_ref[0:1, :] * sh[3]
    r = _sig(_dot(xa, wr_ref[...]) + br_ref[...])
    ig = _sig(_dot(xa, wi_ref[...]) + bi_ref[...])
    lam = lam_ref[...]
    sp = jnp.maximum(-lam, 0.0) + jnp.log(1.0 + jnp.exp(-jnp.abs(lam)))
    la = (-LRU_C) * r * sp
    a = jnp.exp(la)
    y2 = 2.0 * la
    taylor = -(y2 * (1.0 + y2 * (0.5 + y2 * (1.0 / 6.0 + y2 * (1.0 / 24.0)))))
    m2 = jnp.where(y2 > -0.02, taylor, 1.0 - jnp.exp(y2))
    m = jnp.sqrt(m2)
    return xa, sh, r, ig, sp, a, m


def _lru_fwd(z, cw, cb, wr, br, wi, bi, lam, l):
    s = z.shape[0]
    t = _tile(s)
    c = D_LRU
    lay = _lay(l)

    def body(za_ref, ga_ref, cw_ref, cb_ref, wr_ref, br_ref, wi_ref, bi_ref, lam_ref, ya_ref, hs_ref, zprev, hcar):
        i = pl.program_id(0)

        @pl.when(i == 0)
        def _():
            zprev[...] = jnp.zeros_like(zprev)
            hcar[...] = jnp.zeros_like(hcar)

        za = za_ref[...]
        xa, _, _, ig, _, a, m = _lru_gates(za, zprev[...], cw_ref, cb_ref, wr_ref, br_ref, wi_ref, bi_ref, lam_ref)
        u = m * (ig * xa)
        row = _rows((t, c))
        acc_a, acc_h = a, u
        k = 1
        while k < t:
            a_sh = jnp.where(row >= k, _down(acc_a, k), 1.0)
            h_sh = jnp.where(row >= k, _down(acc_h, k), 0.0)
            acc_h = acc_h + acc_a * h_sh
            acc_a = acc_a * a_sh
            k *= 2
        hs = acc_h + acc_a * hcar[...]
        hs_ref[...] = hs
        ga = ga_ref[...]
        ya_ref[...] = (hs * (ga * _sig(ga))).astype(ya_ref.dtype)
        hcar[...] = jnp.sum(jnp.where(row == t - 1, hs, 0.0), axis=0, keepdims=True)
        zprev[...] = za_ref[t - SUBLANES:t, :]

    return pl.pallas_call(
        body, name="lru_fwd", grid=(s // t,),
        in_specs=[pl.BlockSpec((t, c), lambda i: (i, 0)), pl.BlockSpec((t, c), lambda i: (i, 1)),
                  lay((CONV_WIDTH, c)), lay((1, c)), lay((c, c)), lay((1, c)), lay((c, c)), lay((1, c)),
                  lay((1, c))],
        out_specs=[pl.BlockSpec((t, c), lambda i: (i, 0)), pl.BlockSpec((t, c), lambda i: (i, 0))],
        out_shape=[jax.ShapeDtypeStruct((s, c), MXU_DTYPE), jax.ShapeDtypeStruct((s, c), F32)],
        scratch_shapes=[pltpu.VMEM((SUBLANES, c), F32), pltpu.VMEM((1, c), F32)],
        compiler_params=_cp(("arbitrary",), 40),
    )(z, z, cw, cb, wr, br, wi, bi, lam)


def _lru_bwd(z, hs, dy, cw, cb, wr, br, wi, bi, lam, l):
    s = z.shape[0]
    t = _tile(s)
    lay = _lay(l)
    nt = s // t
    c = D_LRU
    hb = t // SUBLANES

    def body(za_ref, zh_ref, ga_ref, hs_ref, hh_ref, dy_ref, cw_ref, cb_ref, wr_ref, br_ref, wi_ref, bi_ref, lam_ref,
             dza_ref, dga_ref, dwr_ref, dwi_ref, dcw_ref, dcb_ref, dbr_ref, dbi_ref, dlam_ref, lcar, dxn):
        i = pl.program_id(0)
        tt = nt - 1 - i

        @pl.when(i == 0)
        def _():
            lcar[...] = jnp.zeros_like(lcar)
            dxn[...] = jnp.zeros_like(dxn)
            for ref in (dwr_ref, dwi_ref, dcw_ref, dcb_ref, dbr_ref, dbi_ref, dlam_ref):
                ref[...] = jnp.zeros_like(ref)

        first = (tt > 0).astype(F32)
        za = za_ref[...]
        xa, sh, r, ig, sp, a, m = _lru_gates(za, zh_ref[...] * first, cw_ref, cb_ref, wr_ref, br_ref, wi_ref, bi_ref,
                                             lam_ref)
        hs_v = hs_ref[...]
        hprev = _down(jnp.concatenate([hh_ref[...] * first, hs_v], axis=0), 1)[SUBLANES:SUBLANES + t]
        ga = ga_ref[...]
        sg = _sig(ga)
        silu = ga * sg
        dya = dy_ref[...]
        dga_ref[...] = (dya * hs_v * (sg * (1.0 + ga * (1.0 - sg)))).astype(dga_ref.dtype)
        row = _rows((t, c))
        acc_h = dya * silu + jnp.where(row == t - 1, lcar[...], 0.0)
        acc_b = jnp.where(row < t - 1, _up(a, 1), 0.0)
        k = 1
        while k < t:
            keep = row < t - k
            b_sh = jnp.where(keep, _up(acc_b, k), 0.0)
            h_sh = jnp.where(keep, _up(acc_h, k), 0.0)
            acc_h = acc_h + acc_b * h_sh
            acc_b = acc_b * b_sh
            k *= 2
        lmb = acc_h
        lcar[...] = jnp.sum(jnp.where(row == 0, a * lmb, 0.0), axis=0, keepdims=True)
        da = lmb * hprev
        dxa = lmb * m * ig
        di = lmb * m * xa
        dm = lmb * ig * xa
        dla = da * a - dm * (a * a) / m
        dr = dla * ((-LRU_C) * sp)
        lam = lam_ref[...]
        dsp = jnp.sum(dla * ((-LRU_C) * r), axis=0, keepdims=True)
        dlam_ref[...] += dsp * (-1.0 / (1.0 + jnp.exp(lam)))
        dpr = dr * r * (1.0 - r)
        dpi = di * ig * (1.0 - ig)
        dbr_ref[...] += jnp.sum(dpr, axis=0, keepdims=True)
        dbi_ref[...] += jnp.sum(dpi, axis=0, keepdims=True)
        dwr_ref[...] += _dot_tn(xa, dpr)
        dwi_ref[...] += _dot_tn(xa, dpi)
        dxa = dxa + _dot_nt(dpr, wr_ref[...]) + _dot_nt(dpi, wi_ref[...])
        dcb_ref[...] += jnp.sum(dxa, axis=0, keepdims=True)
        for k in range(CONV_WIDTH):
            dcw_ref[k:k + 1, :] += jnp.sum(dxa * sh[CONV_WIDTH - 1 - k], axis=0, keepdims=True)
        ext = jnp.concatenate([dxa, dxn[...]], axis=0)
        dza = cw_ref[3:4, :] * dxa
        for j in (1, 2, 3):
            dza = dza + cw_ref[3 - j:4 - j, :] * _up(ext, j)[:t]
        dza_ref[...] = dza.astype(dza_ref.dtype)
        dxn[...] = dxa[:SUBLANES]

    full = lambda shp: pl.BlockSpec(shp, lambda i: (0, 0))
    rev = lambda i: nt - 1 - i
    halo = lambda i: (jnp.maximum((nt - 1 - i) * hb - 1, 0), 0)
    outs = pl.pallas_call(
        body, name="lru_bwd", grid=(nt,),
        in_specs=[pl.BlockSpec((t, c), lambda i: (rev(i), 0)), pl.BlockSpec((SUBLANES, c), halo),
                  pl.BlockSpec((t, c), lambda i: (rev(i), 1)),
                  pl.BlockSpec((t, c), lambda i: (rev(i), 0)), pl.BlockSpec((SUBLANES, c), halo),
                  pl.BlockSpec((t, c), lambda i: (rev(i), 0)),
                  lay((CONV_WIDTH, c)), lay((1, c)), lay((c, c)), lay((1, c)), lay((c, c)), lay((1, c)),
                  lay((1, c))],
        out_specs=[pl.BlockSpec((t, c), lambda i: (rev(i), 0)), pl.BlockSpec((t, c), lambda i: (rev(i), 0)),
                   full((c, c)), full((c, c)), full((CONV_WIDTH, c)), full((1, c)), full((1, c)), full((1, c)),
                   full((1, c))],
        out_shape=[jax.ShapeDtypeStruct((s, c), MXU_DTYPE), jax.ShapeDtypeStruct((s, c), MXU_DTYPE),
                   jax.ShapeDtypeStruct((c, c), F32), jax.ShapeDtypeStruct((c, c), F32),
                   jax.ShapeDtypeStruct((CONV_WIDTH, c), F32)] + [jax.ShapeDtypeStruct((1, c), F32)] * 4,
        scratch_shapes=[pltpu.VMEM((1, c), F32), pltpu.VMEM((SUBLANES, c), F32)],
        compiler_params=_cp(("arbitrary",), 48),
    )(z, z, z, hs, hs, dy, cw, cb, wr, br, wi, bi, lam)
    return outs


POOL_HALO = 16


def _pool_select(lane, v2, v4, v8, v16):
    return jnp.where(lane < 64, v2, jnp.where(lane < 128, v4, jnp.where(lane < 192, v8, v16)))


def _pool_counts(t0, t, c):
    lane = _lanes((t, c))
    win = _pool_select(lane, 2.0, 4.0, 8.0, 16.0)
    seen = (t0 + _rows((t, c)) + 1).astype(F32)
    return lane, jnp.minimum(seen, win)


def _pooled(zc, halo, lane, cnt):
    t = zc.shape[0]
    ext = jnp.concatenate([halo, zc], axis=0)
    s2 = ext + _down(ext, 1)
    s4 = s2 + _down(s2, 2)
    s8 = s4 + _down(s4, 4)
    s16 = s8 + _down(s8, 8)
    cut = lambda v: v[POOL_HALO:POOL_HALO + t]
    return _pool_select(lane, cut(s2), cut(s4), cut(s8), cut(s16)) / cnt - zc


def _pool_fwd(z, wp, ps, l):
    s = z.shape[0]
    t = _tile(s)
    c = D_POOL
    lay = _lay(l)

    def body(zc_ref, gc_ref, wp_ref, ps_ref, yc_ref, zprev):
        i = pl.program_id(0)

        @pl.when(i == 0)
        def _():
            zprev[...] = jnp.zeros_like(zprev)

        zc = zc_ref[...]
        lane, cnt = _pool_counts(i * t, t, c)
        pooled = _pooled(zc, zprev[...], lane, cnt)
        pc = _dot(pooled, wp_ref[...])
        gc = gc_ref[...]
        yc_ref[...] = (pc * ps_ref[...] * (gc * _sig(gc))).astype(yc_ref.dtype)
        zprev[...] = zc_ref[t - POOL_HALO:t, :]

    return pl.pallas_call(
        body, name="pool_fwd", grid=(s // t,),
        in_specs=[pl.BlockSpec((t, c), lambda i: (i, 7)), pl.BlockSpec((t, c), lambda i: (i, 8)),
                  lay((c, c)), lay((1, c))],
        out_specs=pl.BlockSpec((t, c), lambda i: (i, 0)),
        out_shape=jax.ShapeDtypeStruct((s, c), MXU_DTYPE),
        scratch_shapes=[pltpu.VMEM((POOL_HALO, c), F32)],
        compiler_params=_cp(("arbitrary",), 32),
    )(z, z, wp, ps)


def _pool_bwd(z, dy, wp, ps, l):
    s = z.shape[0]
    t = _tile(s)
    lay = _lay(l)
    nt = s // t
    c = D_POOL
    hb = t // POOL_HALO

    def body(zc_ref, zh_ref, gc_ref, dy_ref, wp_ref, ps_ref, dzc_ref, dgc_ref, dwp_ref, dps_ref, ddn):
        i = pl.program_id(0)
        tt = nt - 1 - i

        @pl.when(i == 0)
        def _():
            ddn[...] = jnp.zeros_like(ddn)
            dwp_ref[...] = jnp.zeros_like(dwp_ref)
            dps_ref[...] = jnp.zeros_like(dps_ref)

        first = (tt > 0).astype(F32)
        zc = zc_ref[...]
        lane, cnt = _pool_counts(tt * t, t, c)
        pooled = _pooled(zc, zh_ref[...] * first, lane, cnt)
        pc = _dot(pooled, wp_ref[...])
        gc = gc_ref[...]
        sg = _sig(gc)
        silu = gc * sg
        dyc = dy_ref[...]
        ps_v = ps_ref[...]
        dgc_ref[...] = (dyc * pc * ps_v * (sg * (1.0 + gc * (1.0 - sg)))).astype(dgc_ref.dtype)
        dps_ref[...] += jnp.sum(dyc * pc * silu, axis=0, keepdims=True)
        dpc = dyc * ps_v * silu
        dwp_ref[...] += _dot_tn(pooled, dpc)
        dpooled = _dot_nt(dpc, wp_ref[...])
        dd = dpooled / cnt
        ext = jnp.concatenate([dd, ddn[...]], axis=0)
        f2 = ext + _up(ext, 1)
        f4 = f2 + _up(f2, 2)
        f8 = f4 + _up(f4, 4)
        f16 = f8 + _up(f8, 8)
        dzc = _pool_select(lane, f2[:t], f4[:t], f8[:t], f16[:t]) - dpooled
        dzc_ref[...] = dzc.astype(dzc_ref.dtype)
        ddn[...] = dd[:POOL_HALO]

    full = lambda shp: pl.BlockSpec(shp, lambda i: (0, 0))
    rev = lambda i: nt - 1 - i
    return pl.pallas_call(
        body, name="pool_bwd", grid=(nt,),
        in_specs=[pl.BlockSpec((t, c), lambda i: (rev(i), 7)),
                  pl.BlockSpec((POOL_HALO, c), lambda i: (jnp.maximum(rev(i) * hb - 1, 0), 7)),
                  pl.BlockSpec((t, c), lambda i: (rev(i), 8)),
                  pl.BlockSpec((t, c), lambda i: (rev(i), 3)),
                  lay((c, c)), lay((1, c))],
        out_specs=[pl.BlockSpec((t, c), lambda i: (rev(i), 0)), pl.BlockSpec((t, c), lambda i: (rev(i), 0)),
                   full((c, c)), full((1, c))],
        out_shape=[jax.ShapeDtypeStruct((s, c), MXU_DTYPE), jax.ShapeDtypeStruct((s, c), MXU_DTYPE),
                   jax.ShapeDtypeStruct((c, c), F32), jax.ShapeDtypeStruct((1, c), F32)],
        scratch_shapes=[pltpu.VMEM((POOL_HALO, c), F32)],
        compiler_params=_cp(("arbitrary",), 32),
    )(z, z, z, dy, wp, ps)


def _rope_tables(s):
    pos = jnp.arange(s, dtype=F32)
    inv_freq = ROPE_BASE ** (-jnp.arange(0, QK_ROPE, 2, dtype=F32) / QK_ROPE)
    ang = pos[:, None] * inv_freq[None, :]
    cos, sin = jnp.cos(ang), jnp.sin(ang)
    half = QK_ROPE // 2
    ones = jnp.ones((s, QK_NOPE), F32)
    z64 = jnp.zeros((s, QK_NOPE), F32)
    zh = jnp.zeros((s, half), F32)
    z32 = jnp.zeros((s, HEAD_BLK - QK_NOPE - QK_ROPE), F32)
    c_t = jnp.concatenate([ones, cos, cos, z32], axis=1)
    s1_t = jnp.concatenate([z64, -sin, zh, z32], axis=1)
    s2_t = jnp.concatenate([z64, zh, sin, z32], axis=1)
    return c_t, s1_t, s2_t


def _rope(x, c_t, s1_t, s2_t):
    return x * c_t + pltpu.roll(x, HEAD_BLK - 16, 1) * s1_t + pltpu.roll(x, 16, 1) * s2_t


def _unrope(d, c_t, s1_t, s2_t):
    return d * c_t + pltpu.roll(d * s1_t, 16, 1) + pltpu.roll(d * s2_t, HEAD_BLK - 16, 1)


def _mla_prep_fwd(z, gq, gkv, wuq, wukv, c_t, s1_t, s2_t, l):
    s = z.shape[0]
    t = _tile(s)
    hq = MLA_HEADS * HEAD_BLK
    lay = _lay(l)

    def body(cq_ref, ckv_ref, kr_ref, gq_ref, gkv_ref, wuq_ref, wukv_ref, c_ref, s1_ref, s2_ref,
             q_ref, k_ref, v_ref, qn_ref, kvn_ref):
        ct, s1, s2 = c_ref[...], s1_ref[...], s2_ref[...]
        cq = cq_ref[...]
        qn = (cq * lax.rsqrt(jnp.mean(cq * cq, axis=-1, keepdims=True) + EPS) * gq_ref[...]).astype(MXU_DTYPE)
        qn_ref[...] = qn
        q = jnp.dot(qn, wuq_ref[...], preferred_element_type=F32)
        ckv = ckv_ref[...]
        kvn = (ckv * lax.rsqrt(jnp.mean(ckv * ckv, axis=-1, keepdims=True) + EPS) * gkv_ref[...]).astype(MXU_DTYPE)
        kvn_ref[...] = kvn
        kvp = jnp.dot(kvn, wukv_ref[...], preferred_element_type=F32)
        krr = _rope(kr_ref[...], ct, s1, s2)
        for h in range(MLA_HEADS):
            blk = slice(h * HEAD_BLK, (h + 1) * HEAD_BLK)
            q_ref[:, blk] = (_rope(q[:, blk], ct, s1, s2) * Q_PRESCALE).astype(q_ref.dtype)
            k_ref[:, blk] = (kvp[:, blk] + krr).astype(k_ref.dtype)
        v_ref[...] = kvp[:, hq:].astype(v_ref.dtype)

    tab = pl.BlockSpec((t, HEAD_BLK), lambda i: (i, 0))
    return pl.pallas_call(
        body, name="mla_prep_fwd", grid=(s // t,),
        in_specs=[pl.BlockSpec((t, Q_RANK), lambda i: (i, 2)), pl.BlockSpec((t, KV_RANK), lambda i: (i, 6)),
                  pl.BlockSpec((t, HEAD_BLK), lambda i: (i, 18)),
                  lay((1, Q_RANK)), lay((1, KV_RANK)), lay((Q_RANK, hq)), lay((KV_RANK, hq + D_MLA)),
                  tab, tab, tab],
        out_specs=[pl.BlockSpec((t, hq), lambda i: (i, 0)), pl.BlockSpec((t, hq), lambda i: (i, 0)),
                   pl.BlockSpec((t, D_MLA), lambda i: (i, 0)), pl.BlockSpec((t, Q_RANK), lambda i: (i, 0)),
                   pl.BlockSpec((t, KV_RANK), lambda i: (i, 0))],
        out_shape=[jax.ShapeDtypeStruct((s, hq), MXU_DTYPE), jax.ShapeDtypeStruct((s, hq), MXU_DTYPE),
                   jax.ShapeDtypeStruct((s, D_MLA), MXU_DTYPE), jax.ShapeDtypeStruct((s, Q_RANK), MXU_DTYPE),
                   jax.ShapeDtypeStruct((s, KV_RANK), MXU_DTYPE)],
        compiler_params=_cp(("parallel",), 40),
    )(z, z, z, gq, gkv, wuq, wukv, c_t, s1_t, s2_t)


SUM_LANE_A = V_DIM
SUM_LANE_B = 0


def _flash_fwd(q, k, v, z):
    s = q.shape[0]
    t = _tile(s)
    nq = s // t
    pw = 2 * HEAD_BLK

    def body(q_ref, k_ref, v_ref, gb_ref, o_ref, yb_ref, lse_ref):
        i = pl.program_id(1)
        qv = q_ref[...]
        qa, qb = qv[:, :HEAD_BLK], qv[:, HEAD_BLK:]
        lane = _lanes((t, HEAD_BLK))
        lo = lane < V_DIM

        def update(qh, kh, vh, m, acc, masked):
            sc = _dot_nt(qh, kh)
            if masked:
                sc = jnp.where(_lanes((t, t)) <= _rows((t, t)), sc, -1e30)
            m_new = jnp.maximum(m, jnp.max(sc, axis=-1, keepdims=True))
            p = jnp.exp2(sc - m_new).astype(MXU_DTYPE)
            return m_new, acc * jnp.exp2(m - m_new) + _dot(p, vh)

        def step(j, carry, masked):
            ma, mb, acc_a, acc_b = carry
            kv_rows = pl.ds(pl.multiple_of(j * t, t), t)
            kt = k_ref[kv_rows, :]
            vt = v_ref[kv_rows, :]
            lane_v = _lanes(vt.shape)
            one = jnp.ones_like(vt)
            zero_v = jnp.zeros_like(vt)
            v_a = jnp.where(lane_v < V_DIM, vt, jnp.where(lane_v == SUM_LANE_A, one, zero_v))
            v_b = jnp.where(lane_v >= V_DIM, vt, jnp.where(lane_v == SUM_LANE_B, one, zero_v))
            ma, acc_a = update(qa, kt[:, :HEAD_BLK], v_a, ma, acc_a, masked)
            mb, acc_b = update(qb, kt[:, HEAD_BLK:], v_b, mb, acc_b, masked)
            return ma, mb, acc_a, acc_b

        neg = jnp.full((t, 1), -1e30, F32)
        zero = jnp.zeros((t, HEAD_BLK), F32)
        carry = lax.fori_loop(0, i // 2, lambda jj, cr: step(2 * jj + 1, step(2 * jj, cr, False), False),
                              (neg, neg, zero, zero))
        carry = lax.cond(i % 2 == 1, lambda cr: step(i - 1, cr, False), lambda cr: cr, carry)
        ma, mb, acc_a, acc_b = step(i, carry, True)
        la = jnp.sum(jnp.where(lane == SUM_LANE_A, acc_a, 0.0), axis=-1, keepdims=True)
        lb = jnp.sum(jnp.where(lane == SUM_LANE_B, acc_b, 0.0), axis=-1, keepdims=True)
        o = jnp.where(lo, acc_a * (1.0 / la), acc_b * (1.0 / lb))
        o_ref[...] = o
        gb = gb_ref[...]
        yb_ref[...] = (o * (gb * _sig(gb))).astype(yb_ref.dtype)
        lse = jnp.where(lo, ma + jnp.log(la) * LOG2_E, mb + jnp.log(lb) * LOG2_E)
        pick = ((_rows((SUBLANES, HEAD_BLK)) == 0) & (_lanes((SUBLANES, HEAD_BLK)) == 0)) | (
            (_rows((SUBLANES, HEAD_BLK)) == 1) & (_lanes((SUBLANES, HEAD_BLK)) == V_DIM))
        lse_ref[0, 0] = lax.dot_general(pick.astype(F32), lse, (((1,), (1,)), ((), ())),
                                        precision=lax.Precision.HIGHEST, preferred_element_type=F32)

    return pl.pallas_call(
        body, name="flash_fwd", grid=(N_PAIR, nq),
        in_specs=[pl.BlockSpec((t, pw), lambda p, i: (i, p)), pl.BlockSpec((s, pw), lambda p, i: (0, p)),
                  pl.BlockSpec((s, HEAD_BLK), lambda p, i: (0, p)),
                  pl.BlockSpec((t, HEAD_BLK), lambda p, i: (i, 9 + p))],
        out_specs=[pl.BlockSpec((t, HEAD_BLK), lambda p, i: (i, p)), pl.BlockSpec((t, HEAD_BLK), lambda p, i: (i, p)),
                   pl.BlockSpec((1, 1, SUBLANES, t), lambda p, i: (p, i, 0, 0))],
        out_shape=[jax.ShapeDtypeStruct((s, D_MLA), F32), jax.ShapeDtypeStruct((s, D_MLA), MXU_DTYPE),
                   jax.ShapeDtypeStruct((N_PAIR, nq, SUBLANES, t), F32)],
        compiler_params=_cp(("parallel", "parallel"), 48),
    )(q, k, v, z)


def _attn_bwd_prep(dy, o, z):
    s = o.shape[0]
    t = _tile(s)
    nq = s // t
    rows = N_PAIR * SUBLANES

    def body(dy_ref, o_ref, gb_ref, do_ref, dgb_ref, dl_ref):
        gb = gb_ref[...]
        sg = _sig(gb)
        dyb = dy_ref[...]
        ov = o_ref[...]
        do = dyb * (gb * sg)
        do_ref[...] = do.astype(do_ref.dtype)
        dgb_ref[...] = (dyb * ov * (sg * (1.0 + gb * (1.0 - sg)))).astype(dgb_ref.dtype)
        r = _rows((rows, D_MLA))
        head = (r // SUBLANES) * 2 + (r % SUBLANES)
        sel = ((r % SUBLANES) < 2) & (_lanes((rows, D_MLA)) // V_DIM == head)
        dl = lax.dot_general(sel.astype(F32), do * ov, (((1,), (1,)), ((), ())),
                             precision=lax.Precision.HIGHEST, preferred_element_type=F32)
        for p in range(N_PAIR):
            dl_ref[p, 0] = dl[p * SUBLANES:(p + 1) * SUBLANES]

    return pl.pallas_call(
        body, name="attn_bwd_prep", grid=(nq,),
        in_specs=[pl.BlockSpec((t, D_MLA), lambda i: (i, 1)), pl.BlockSpec((t, D_MLA), lambda i: (i, 0)),
                  pl.BlockSpec((t, D_MLA), lambda i: (i, 3))],
        out_specs=[pl.BlockSpec((t, D_MLA), lambda i: (i, 0)), pl.BlockSpec((t, D_MLA), lambda i: (i, 0)),
                   pl.BlockSpec((N_PAIR, 1, SUBLANES, t), lambda i: (0, i, 0, 0))],
        out_shape=[jax.ShapeDtypeStruct((s, D_MLA), MXU_DTYPE), jax.ShapeDtypeStruct((s, D_MLA), MXU_DTYPE),
                   jax.ShapeDtypeStruct((N_PAIR, nq, SUBLANES, t), F32)],
        compiler_params=_cp(("parallel",), 32),
    )(dy, o, z)


def _flash_bwd(q, k, v, do, lse, delta):
    s = q.shape[0]
    t = _tile(s)
    nq = s // t
    pw = 2 * HEAD_BLK

    def body(q_ref, do_ref, lse_ref, dl_ref, k_ref, v_ref, dq_ref, dk_ref, dv_ref):
        j = pl.program_id(1)

        @pl.when(j == 0)
        def _():
            dq_ref[...] = jnp.zeros_like(dq_ref)

        kt = k_ref[...]
        ka, kb = kt[:, :HEAD_BLK], kt[:, HEAD_BLK:]
        vt = v_ref[...]

        def head(kh, qh, do_h, lse_row, dl_row, masked):
            st = _dot_nt(kh, qh)
            if masked:
                st = jnp.where(_rows((t, t)) <= _lanes((t, t)), st, -1e30)
            pt = jnp.exp2(st - lse_row)
            dv_h = _dot(pt, do_h)
            dst = (pt * (_dot_nt(vt, do_h) - dl_row)).astype(MXU_DTYPE)
            return dv_h, _dot(dst, qh), _dot_tn(dst, kh)

        def step(i, carry, masked):
            dka, dkb, dv = carry
            q_rows = pl.ds(pl.multiple_of(i * t, t), t)
            qv = q_ref[q_rows, :]
            dov = do_ref[q_rows, :]
            lane = _lanes(dov.shape)
            do_lo = jnp.where(lane < V_DIM, dov, jnp.zeros_like(dov))
            do_hi = jnp.where(lane >= V_DIM, dov, jnp.zeros_like(dov))
            dva, dk_a, dq_a = head(ka, qv[:, :HEAD_BLK], do_lo, lse_ref[0, i, 0:1, :], dl_ref[0, i, 0:1, :], masked)
            dvb, dk_b, dq_b = head(kb, qv[:, HEAD_BLK:], do_hi, lse_ref[0, i, 1:2, :], dl_ref[0, i, 1:2, :], masked)
            dq_ref[q_rows, 0:HEAD_BLK] += dq_a
            dq_ref[q_rows, HEAD_BLK:pw] += dq_b
            return dka + dk_a, dkb + dk_b, dv + dva + dvb

        zero = jnp.zeros((t, HEAD_BLK), F32)
        carry = step(j, (zero, zero, zero), True)
        dka, dkb, dv = lax.fori_loop(j + 1, nq, lambda i, cr: step(i, cr, False), carry)
        dk_ref[:, 0:HEAD_BLK] = dka * LN_2
        dk_ref[:, HEAD_BLK:pw] = dkb * LN_2
        dv_ref[...] = dv.astype(dv_ref.dtype)

    return pl.pallas_call(
        body, name="flash_bwd", grid=(N_PAIR, nq),
        in_specs=[pl.BlockSpec((s, pw), lambda p, j: (0, p)), pl.BlockSpec((s, HEAD_BLK), lambda p, j: (0, p)),
                  pl.BlockSpec((1, nq, SUBLANES, t), lambda p, j: (p, 0, 0, 0)),
                  pl.BlockSpec((1, nq, SUBLANES, t), lambda p, j: (p, 0, 0, 0)),
                  pl.BlockSpec((t, pw), lambda p, j: (j, p)), pl.BlockSpec((t, HEAD_BLK), lambda p, j: (j, p))],
        out_specs=[pl.BlockSpec((s, pw), lambda p, j: (0, p)), pl.BlockSpec((t, pw), lambda p, j: (j, p)),
                   pl.BlockSpec((t, HEAD_BLK), lambda p, j: (j, p))],
        out_shape=[jax.ShapeDtypeStruct((s, MLA_HEADS * HEAD_BLK), F32),
                   jax.ShapeDtypeStruct((s, MLA_HEADS * HEAD_BLK), F32),
                   jax.ShapeDtypeStruct((s, D_MLA), MXU_DTYPE)],
        compiler_params=_cp(("parallel", "arbitrary"), 56),
    )(q, do, lse, delta, k, v)


def _mla_prep_bwd(dq, dk, dv, z, gq, gkv, wuq, wukv, c_t, s1_t, s2_t, l):
    s = z.shape[0]
    t = _tile(s)
    hq = MLA_HEADS * HEAD_BLK
    lay = _lay(l)

    def body(dq_ref, dk_ref, dv_ref, cq_ref, ckv_ref, gq_ref, gkv_ref, wuq_ref, wukv_ref, c_ref, s1_ref, s2_ref,
             dcq_ref, dckv_ref, dkr_ref, dqu_ref, dkvp_ref, dgq_ref, dgkv_ref):
        @pl.when(pl.program_id(0) == 0)
        def _():
            dgq_ref[...] = jnp.zeros_like(dgq_ref)
            dgkv_ref[...] = jnp.zeros_like(dgkv_ref)

        ct, s1, s2 = c_ref[...], s1_ref[...], s2_ref[...]
        dk_sum = jnp.zeros((t, HEAD_BLK), F32)
        for h in range(MLA_HEADS):
            blk = slice(h * HEAD_BLK, (h + 1) * HEAD_BLK)
            dqu_ref[:, blk] = _unrope(dq_ref[:, blk] * ATT_SCALE, ct, s1, s2).astype(dqu_ref.dtype)
            dkh = dk_ref[:, blk]
            dk_sum = dk_sum + dkh
            dkvp_ref[:, blk] = dkh.astype(dkvp_ref.dtype)
        dkvp_ref[:, hq:] = dv_ref[...]
        lane = _lanes((t, HEAD_BLK))
        rope_lanes = (lane >= KR_LANE0) & (lane < KR_LANE0 + QK_ROPE)
        dkr_ref[...] = _unrope(jnp.where(rope_lanes, dk_sum, 0.0), ct, s1, s2).astype(dkr_ref.dtype)

        def norm_bwd(c_in, g, dn_out, dc_ref, dg_ref):
            rs = lax.rsqrt(jnp.mean(c_in * c_in, axis=-1, keepdims=True) + EPS)
            n = c_in * rs
            dg_ref[...] += jnp.sum(dn_out * n, axis=0, keepdims=True)
            dn = dn_out * g
            dc_ref[...] = (rs * (dn - n * jnp.mean(dn * n, axis=-1, keepdims=True))).astype(dc_ref.dtype)

        norm_bwd(cq_ref[...], gq_ref[...], _dot_nt(dqu_ref[...], wuq_ref[...]), dcq_ref, dgq_ref)
        norm_bwd(ckv_ref[...], gkv_ref[...], _dot_nt(dkvp_ref[...], wukv_ref[...]), dckv_ref, dgkv_ref)

    full = lambda shp: pl.BlockSpec(shp, lambda i: (0, 0))
    tab = pl.BlockSpec((t, HEAD_BLK), lambda i: (i, 0))
    row = lambda w: pl.BlockSpec((t, w), lambda i: (i, 0))
    return pl.pallas_call(
        body, name="mla_prep_bwd", grid=(s // t,),
        in_specs=[row(hq), row(hq), row(D_MLA),
                  pl.BlockSpec((t, Q_RANK), lambda i: (i, 2)), pl.BlockSpec((t, KV_RANK), lambda i: (i, 6)),
                  lay((1, Q_RANK)), lay((1, KV_RANK)), lay((Q_RANK, hq)), lay((KV_RANK, hq + D_MLA)),
                  tab, tab, tab],
        out_specs=[row(Q_RANK), row(KV_RANK), row(HEAD_BLK), row(hq), row(hq + D_MLA),
                   full((1, Q_RANK)), full((1, KV_RANK))],
        out_shape=[jax.ShapeDtypeStruct((s, Q_RANK), MXU_DTYPE), jax.ShapeDtypeStruct((s, KV_RANK), MXU_DTYPE),
                   jax.ShapeDtypeStruct((s, HEAD_BLK), MXU_DTYPE), jax.ShapeDtypeStruct((s, hq), MXU_DTYPE),
                   jax.ShapeDtypeStruct((s, hq + D_MLA), MXU_DTYPE),
                   jax.ShapeDtypeStruct((1, Q_RANK), F32), jax.ShapeDtypeStruct((1, KV_RANK), F32)],
        compiler_params=_cp(("arbitrary",), 48),
    )(dq, dk, dv, z, z, gq, gkv, wuq, wukv, c_t, s1_t, s2_t)


def _outproj_fwd(x, ya, yb, yc, w, l):
    s = x.shape[0]
    t = _tile(s)
    lay = _lay(l)

    def body(x_ref, ya_ref, yb_ref, yc_ref, wa_ref, wb_ref, wc_ref, o_ref):
        o_ref[...] = (x_ref[...] + _dot(ya_ref[...], wa_ref[...]) + _dot(yb_ref[...], wb_ref[...])
                      + _dot(yc_ref[...], wc_ref[...]))

    row = lambda w_: pl.BlockSpec((t, w_), lambda i: (i, 0))
    return pl.pallas_call(
        body, name="outproj_fwd", grid=(s // t,),
        in_specs=[row(D_MODEL), row(D_LRU), row(D_MLA), row(D_POOL),
                  lay((D_LRU, D_MODEL), 0), lay((D_MLA, D_MODEL), 1), lay((D_POOL, D_MODEL), 3)],
        out_specs=row(D_MODEL),
        out_shape=jax.ShapeDtypeStruct((s, D_MODEL), F32),
        compiler_params=_cp(("parallel",), 40),
    )(x, ya, yb, yc, w, w, w)


def _outproj_bwd(dx, ya, yb, yc, w, l):
    s = dx.shape[0]
    t = _tile(s)

    def body(dx_ref, ya_ref, yb_ref, yc_ref, w_ref, dy_ref, dw_ref):
        @pl.when(pl.program_id(0) == 0)
        def _():
            dw_ref[...] = jnp.zeros_like(dw_ref)

        dxv = dx_ref[...].astype(MXU_DTYPE)
        dy_ref[...] = _dot_nt(dxv, w_ref[...])
        dw_ref[0:D_LRU, :] += _dot_tn(ya_ref[...], dxv)
        dw_ref[D_LRU:D_LRU + D_MLA, :] += _dot_tn(yb_ref[...], dxv)
        dw_ref[D_LRU + D_MLA:D_MIX, :] += _dot_tn(yc_ref[...], dxv)

    row = lambda w_: pl.BlockSpec((t, w_), lambda i: (i, 0))
    return pl.pallas_call(
        body, name="outproj_bwd", grid=(s // t,),
        in_specs=[row(D_MODEL), row(D_LRU), row(D_MLA), row(D_POOL), _lay(l)((D_MIX, D_MODEL))],
        out_specs=[row(D_MIX), pl.BlockSpec((D_MIX, D_MODEL), lambda i: (0, 0))],
        out_shape=[jax.ShapeDtypeStruct((s, D_MIX), F32), jax.ShapeDtypeStruct((D_MIX, D_MODEL), F32)],
        compiler_params=_cp(("arbitrary",), 48),
    )(dx, ya, yb, yc, w)


def _mm_tn(a, b, name):
    s, k1 = a.shape
    n = b.shape[1]
    t = _tile(s)

    def body(a_ref, b_ref, o_ref):
        @pl.when(pl.program_id(0) == 0)
        def _():
            o_ref[...] = jnp.zeros_like(o_ref)

        o_ref[...] += _dot_tn(a_ref[...], b_ref[...])

    return pl.pallas_call(
        body, name=name, grid=(s // t,),
        in_specs=[pl.BlockSpec((t, k1), lambda i: (i, 0)), pl.BlockSpec((t, n), lambda i: (i, 0))],
        out_specs=pl.BlockSpec((k1, n), lambda i: (0, 0)),
        out_shape=jax.ShapeDtypeStruct((k1, n), F32),
        compiler_params=_cp(("arbitrary",), 56),
    )(a, b)


DZ_WIDTHS = (D_LRU, D_LRU, Q_RANK, D_MLA, KV_RANK, D_POOL, D_POOL, HEAD_BLK)
DZ_STARTS = tuple(sum(DZ_WIDTHS[:k]) for k in range(len(DZ_WIDTHS)))


def _dwin(h, dz_parts):
    s = h.shape[0]
    t = _tile(s)

    def body(h_ref, *refs):
        o_ref = refs[-1]

        @pl.when(pl.program_id(0) == 0)
        def _():
            o_ref[...] = jnp.zeros_like(o_ref)

        hv = h_ref[...]
        for part_ref, a, wd in zip(refs[:-1], DZ_STARTS, DZ_WIDTHS):
            o_ref[:, a:a + wd] += _dot_tn(hv, part_ref[...])

    row = lambda w_: pl.BlockSpec((t, w_), lambda i: (i, 0))
    return pl.pallas_call(
        body, name="dwin", grid=(s // t,),
        in_specs=[row(D_MODEL)] + [row(wd) for wd in DZ_WIDTHS],
        out_specs=pl.BlockSpec((D_MODEL, D_INP), lambda i: (0, 0)),
        out_shape=jax.ShapeDtypeStruct((D_MODEL, D_INP), F32),
        compiler_params=_cp(("arbitrary",), 56),
    )(h, *dz_parts)


def _inproj_bwd(dz_parts, w, x, g, dxn, l):
    s = x.shape[0]
    t = _tile(s)
    lay = _lay(l)
    n_parts = len(DZ_WIDTHS)

    def body(*refs):
        part_refs = refs[:n_parts]
        w_ref, x_ref, g_ref, dxn_ref, dx_ref, dg_ref = refs[n_parts:]

        @pl.when(pl.program_id(0) == 0)
        def _():
            dg_ref[...] = jnp.zeros_like(dg_ref)

        dh = None
        for part_ref, a, wd in zip(part_refs, DZ_STARTS, DZ_WIDTHS):
            term = _dot_nt(part_ref[...], w_ref[:, a:a + wd])
            dh = term if dh is None else dh + term
        xv = x_ref[...]
        rs = lax.rsqrt(jnp.mean(xv * xv, axis=-1, keepdims=True) + EPS)
        n = xv * rs
        dg_ref[...] += jnp.sum(dh * n, axis=0, keepdims=True)
        dn = dh * g_ref[...]
        dx_ref[...] = dxn_ref[...] + rs * (dn - n * jnp.mean(dn * n, axis=-1, keepdims=True))

    row = lambda w_: pl.BlockSpec((t, w_), lambda i: (i, 0))
    return pl.pallas_call(
        body, name="inproj_bwd", grid=(s // t,),
        in_specs=[row(wd) for wd in DZ_WIDTHS] + [lay((D_MODEL, D_INP)), row(D_MODEL), lay((1, D_MODEL)),
                                                  row(D_MODEL)],
        out_specs=[row(D_MODEL), pl.BlockSpec((1, D_MODEL), lambda i: (0, 0))],
        out_shape=[jax.ShapeDtypeStruct((s, D_MODEL), F32), jax.ShapeDtypeStruct((1, D_MODEL), F32)],
        compiler_params=_cp(("arbitrary",), 48),
    )(*dz_parts, w, x, g, dxn)


def _loss_head(x, g, tgt):
    s = x.shape[0]
    t = _tile(s)

    def body(x_ref, g_ref, t_ref, dx_ref, loss_ref, dg_ref):
        @pl.when(pl.program_id(0) == 0)
        def _():
            loss_ref[...] = jnp.zeros_like(loss_ref)
            dg_ref[...] = jnp.zeros_like(dg_ref)

        xv = x_ref[...]
        rs = lax.rsqrt(jnp.mean(xv * xv, axis=-1, keepdims=True) + EPS)
        n = xv * rs
        gv = g_ref[...]
        e = n * gv - t_ref[...]
        loss_ref[...] += 0.5 * jnp.sum(jnp.mean(e * e, axis=-1, keepdims=True))
        dyf = e * (1.0 / D_MODEL)
        dg_ref[...] += jnp.sum(dyf * n, axis=0, keepdims=True)
        dn = dyf * gv
        dx_ref[...] = rs * (dn - n * jnp.mean(dn * n, axis=-1, keepdims=True))

    row = pl.BlockSpec((t, D_MODEL), lambda i: (i, 0))
    vec = pl.BlockSpec((1, D_MODEL), lambda i: (0, 0))
    return pl.pallas_call(
        body, name="loss_head", grid=(s // t,),
        in_specs=[row, vec, row],
        out_specs=[row, pl.BlockSpec((1, LANES), lambda i: (0, 0)), vec],
        out_shape=[jax.ShapeDtypeStruct((s, D_MODEL), F32), jax.ShapeDtypeStruct((1, LANES), F32),
                   jax.ShapeDtypeStruct((1, D_MODEL), F32)],
        compiler_params=_cp(("arbitrary",), 32),
    )(x, g, tgt)


def _block_diag(w):
    n, h, d, _ = w.shape
    return jnp.einsum('lhij,hk->lhikj', w, jnp.eye(h, dtype=w.dtype)).reshape(n, h * d, h * d)


def _diag_blocks(wfull, h):
    d = wfull.shape[-1] // h
    return jnp.stack([wfull[:, i * d:(i + 1) * d, i * d:(i + 1) * d] for i in range(h)], axis=1)


REF_TO_PERM = np.concatenate([np.arange(0, 1152), np.arange(1536, 1792),
                              np.arange(2304 + KR_LANE0, 2304 + KR_LANE0 + QK_ROPE),
                              np.arange(1152, 1536), np.arange(1792, 2304)])
N_SHARD = 4
W_IN_SHARD = D_IN // N_SHARD


def _w_in_runs():
    ref_of_perm = -np.ones(D_INP, np.int64)
    ref_of_perm[REF_TO_PERM] = np.arange(D_IN)
    perm_runs, p = [], 0
    while p < D_INP:
        r, q = ref_of_perm[p], p + 1
        if r < 0:
            while q < D_INP and ref_of_perm[q] < 0:
                q += 1
            perm_runs.append((None, q - p, 0))
        else:
            while (q < D_INP and ref_of_perm[q] == ref_of_perm[q - 1] + 1
                   and ref_of_perm[q] // W_IN_SHARD == r // W_IN_SHARD):
                q += 1
            perm_runs.append((int(r // W_IN_SHARD), int(r % W_IN_SHARD), int(r % W_IN_SHARD + q - p)))
        p = q
    shard_runs = []
    for s in range(N_SHARD):
        cols = REF_TO_PERM[s * W_IN_SHARD:(s + 1) * W_IN_SHARD]
        runs, a = [], 0
        for b in range(1, W_IN_SHARD + 1):
            if b == W_IN_SHARD or cols[b] != cols[b - 1] + 1:
                runs.append((int(cols[a]), int(cols[b - 1]) + 1))
                a = b
        shard_runs.append(runs)
    return perm_runs, shard_runs


def _permute_w_in(shards):
    perm_runs, _ = _w_in_runs()
    lead = shards[0].shape[:-1]
    parts = [jnp.zeros(lead + (a,), shards[0].dtype) if s is None else shards[s][..., a:b] for s, a, b in perm_runs]
    return jnp.concatenate(parts, axis=-1)


def _w_in_shard(wp, s):
    _, shard_runs = _w_in_runs()
    return jnp.concatenate([wp[..., a:b] for a, b in shard_runs[s]], axis=-1)


def _pad_w_uq(w):
    w4 = w.reshape(w.shape[:2] + (MLA_HEADS, QK_NOPE + QK_ROPE))
    return jnp.pad(w4, ((0, 0),) * 3 + ((0, HEAD_BLK - QK_NOPE - QK_ROPE),)).reshape(w.shape[:2] + (-1,))


def _unpad_w_uq(w):
    return w.reshape(w.shape[:2] + (MLA_HEADS, HEAD_BLK))[..., :QK_NOPE + QK_ROPE].reshape(w.shape[:2] + (-1,))


def _pad_w_ukv(w):
    w4 = w.reshape(w.shape[:2] + (MLA_HEADS, QK_NOPE + V_DIM))
    kpart = jnp.pad(w4[..., :QK_NOPE], ((0, 0),) * 3 + ((0, HEAD_BLK - QK_NOPE),)).reshape(w.shape[:2] + (-1,))
    return jnp.concatenate([kpart, w4[..., QK_NOPE:].reshape(w.shape[:2] + (-1,))], axis=2)


def _unpad_w_ukv(w):
    hq = MLA_HEADS * HEAD_BLK
    kpart = w[..., :hq].reshape(w.shape[:2] + (MLA_HEADS, HEAD_BLK))[..., :QK_NOPE]
    vpart = w[..., hq:].reshape(w.shape[:2] + (MLA_HEADS, V_DIM))
    return jnp.concatenate([kpart, vpart], axis=3).reshape(w.shape[:2] + (-1,))


def _local_step(x, tgt, w):
    s = x.shape[0]
    tabs = _rope_tables(s)
    vec = lambda a: a[:, None, :]
    mxu = lambda a: a.astype(MXU_DTYPE)
    p = dict(g=vec(w['norm_g']), w_in=mxu(w['w_in']), cw=w['conv_w'], cb=vec(w['conv_b']),
             wr=mxu(_block_diag(w['w_rg'])), br=vec(w['b_rg']), wi=mxu(_block_diag(w['w_ig'])), bi=vec(w['b_ig']),
             lam=vec(w['lru_lambda']), gq=vec(w['q_norm_g']), gkv=vec(w['kv_norm_g']),
             wuq=mxu(_pad_w_uq(w['w_uq'])), wukv=mxu(_pad_w_ukv(w['w_ukv'])),
             wp=mxu(_block_diag(w['w_pool'])), ps=vec(w['pool_scale']), wout=mxu(w['w_out']))
    lru = lambda l: (p['cw'], p['cb'], p['wr'], p['br'], p['wi'], p['bi'], p['lam'], l)
    mla = lambda l: (p['gq'], p['gkv'], p['wuq'], p['wukv'], *tabs, l)

    saved = []
    for l in range(DEPTH):
        h, z = _inproj_fwd(x, p['g'], p['w_in'], l)
        ya, hs = _lru_fwd(z, *lru(l))
        yc = _pool_fwd(z, p['wp'], p['ps'], l)
        q, k, v, qn, kvn = _mla_prep_fwd(z, *mla(l))
        o, yb, lse = _flash_fwd(q, k, v, z)
        saved.append(dict(x=x, h=h, z=z, hs=hs, ya=ya, yb=yb, yc=yc, q=q, k=k, v=v, qn=qn, kvn=kvn, o=o, lse=lse))
        x = _outproj_fwd(x, ya, yb, yc, p['wout'], l)

    dx, loss, dgf = _loss_head(x, w['final_norm_g'][None], tgt)
    per_layer = {n: [None] * DEPTH for n in WEIGHT_NAMES if n != 'final_norm_g'}
    for l in reversed(range(DEPTH)):
        sv = saved[l]
        dy, per_layer['w_out'][l] = _outproj_bwd(dx, sv['ya'], sv['yb'], sv['yc'], p['wout'], l)
        dza, dga, *lru_grads = _lru_bwd(sv['z'], sv['hs'], dy, *lru(l))
        for n, g in zip(('w_rg', 'w_ig', 'conv_w', 'conv_b', 'b_rg', 'b_ig', 'lru_lambda'), lru_grads):
            per_layer[n][l] = g
        dzc, dgc, per_layer['w_pool'][l], per_layer['pool_scale'][l] = _pool_bwd(sv['z'], dy, p['wp'], p['ps'], l)
        do, dgb, delta = _attn_bwd_prep(dy, sv['o'], sv['z'])
        dq, dk, dv = _flash_bwd(sv['q'], sv['k'], sv['v'], do, sv['lse'], delta)
        dcq, dckv, dkr, dqu, dkvp, per_layer['q_norm_g'][l], per_layer['kv_norm_g'][l] = _mla_prep_bwd(
            dq, dk, dv, sv['z'], *mla(l))
        dz_parts = (dza, dga, dcq, dgb, dckv, dzc, dgc, dkr)
        per_layer['w_in'][l] = _dwin(sv['h'], dz_parts)
        per_layer['w_uq'][l] = _mm_tn(sv['qn'], dqu, "dwuq")
        per_layer['w_ukv'][l] = _mm_tn(sv['kvn'], dkvp, "dwukv")
        dx, per_layer['norm_g'][l] = _inproj_bwd(dz_parts, p['w_in'], sv['x'], p['g'], dx, l)
    grads = {n: jnp.stack(g) for n, g in per_layer.items()}
    for n in ('norm_g', 'conv_b', 'b_rg', 'b_ig', 'lru_lambda', 'q_norm_g', 'kv_norm_g', 'pool_scale'):
        grads[n] = grads[n][:, 0, :]
    grads['w_rg'] = _diag_blocks(grads['w_rg'], LRU_HEADS)
    grads['w_ig'] = _diag_blocks(grads['w_ig'], LRU_HEADS)
    grads['w_pool'] = _diag_blocks(grads['w_pool'], len(POOL_WINDOWS))
    grads['w_uq'] = _unpad_w_uq(grads['w_uq'])
    grads['w_ukv'] = _unpad_w_ukv(grads['w_ukv'])
    grads['final_norm_g'] = dgf[0]
    return loss[0, 0], dx, grads


WIRE_DTYPE = jnp.bfloat16
MESH_IDS = pl.DeviceIdType.MESH
_HBM = pl.BlockSpec(memory_space=pltpu.HBM)


def _coords():
    return lax.axis_index("x"), lax.axis_index("y"), lax.axis_index("c")


def _comm_call(body, name, arrays, out_shapes, copies_per_array):
    n = len(arrays)
    return pl.pallas_call(
        body, name=name, out_shape=out_shapes, in_specs=[_HBM] * n, out_specs=[_HBM] * n,
        scratch_shapes=[pltpu.SemaphoreType.DMA((n, copies_per_array)), pltpu.SemaphoreType.DMA((n, copies_per_array))],
    )(*arrays)


def _all_gather8(blocks, name):
    n = len(blocks)
    every = range(n)

    def body(*refs):
        x_refs, out_refs = refs[:n], refs[n:2 * n]
        send_sems, recv_sems = refs[2 * n:]
        x, y, c = _coords()
        me, sibling = (x, y, c), (x, y, 1 - c)
        chips = [(1 - x, y), (x, 1 - y), (1 - x, 1 - y)]

        def slot(t, px, py, pc):
            return out_refs[t].at[4 * px + 2 * py + pc]

        def copy(t, k, block, to, own=False):
            return pltpu.make_async_remote_copy(
                src_ref=x_refs[t] if own else slot(t, *block), dst_ref=slot(t, *block),
                send_sem=send_sems.at[t, k], recv_sem=recv_sems.at[t, k], device_id=to, device_id_type=MESH_IDS)

        first = [copy(t, 0, me, sibling, own=True) for t in every]
        first += [copy(t, 1 + j, me, (*chip, c), own=True) for j, chip in enumerate(chips) for t in every]
        for cp in first:
            cp.start()
        passed = [[copy(t, 4 + j, (*chip, c), sibling) for t in every] for j, chip in enumerate(chips)]
        for j, chip in enumerate(chips):
            for t in every:
                copy(t, 1 + j, (*chip, c), me).wait_recv()
                passed[j][t].start()
        for t in every:
            copy(t, 0, sibling, me).wait_recv()
        for j, chip in enumerate(chips):
            for t in every:
                copy(t, 4 + j, (*chip, 1 - c), me).wait_recv()
        for cp in first + [cp for group in passed for cp in group]:
            cp.wait_send()

    outs = [jax.ShapeDtypeStruct((N_DEV,) + b.shape, b.dtype) for b in blocks]
    got = _comm_call(body, name, blocks, outs, 7)
    me = 4 * lax.axis_index("x") + 2 * lax.axis_index("y") + lax.axis_index("c")
    return [lax.dynamic_update_index_in_dim(g, b, me, 0) for g, b in zip(got, blocks)]


def _sibling_send(arrs, name, which_half=False):
    n = len(arrs)

    def body(*refs):
        a_refs, out_refs = refs[:n], refs[n:2 * n]
        send_sems, recv_sems = refs[2 * n:]
        x, y, c = _coords()
        sent = [pltpu.make_async_remote_copy(
            src_ref=a_refs[t].at[1 - c] if which_half else a_refs[t], dst_ref=out_refs[t],
            send_sem=send_sems.at[t, 0], recv_sem=recv_sems.at[t, 0],
            device_id=(x, y, 1 - c), device_id_type=MESH_IDS) for t in range(n)]
        for cp in sent:
            cp.start()
        for cp in sent:
            cp.wait()

    shapes = [jax.ShapeDtypeStruct(a.shape[1:] if which_half else a.shape, a.dtype) for a in arrs]
    return _comm_call(body, name, arrs, shapes, 1)


def _chip_exchange(arrs, name):
    n = len(arrs)

    def body(*refs):
        a_refs, out_refs = refs[:n], refs[n:2 * n]
        send_sems, recv_sems = refs[2 * n:]
        x, y, c = _coords()
        copies = []
        for j, (cx, cy) in enumerate([(1 - x, y), (x, 1 - y), (1 - x, 1 - y)]):
            copies += [pltpu.make_async_remote_copy(
                src_ref=a_refs[t].at[2 * cx + cy], dst_ref=out_refs[t].at[j], send_sem=send_sems.at[t, j],
                recv_sem=recv_sems.at[t, j], device_id=(cx, cy, c), device_id_type=MESH_IDS) for t in range(n)]
        for cp in copies:
            cp.start()
        for cp in copies:
            cp.wait()

    return _comm_call(body, name, arrs, [jax.ShapeDtypeStruct((3,) + a.shape[1:], a.dtype) for a in arrs], 3)


def _sum_leading(groups, out_dtype, steps, name):
    flat = [a for g in groups for a in g]

    def body(*refs):
        ins, outs, pos = refs[:len(flat)], refs[len(flat):], 0
        for g, o_ref in zip(groups, outs):
            acc = None
            for i_ref in ins[pos:pos + len(g)]:
                for k in range(i_ref.shape[0]):
                    term = i_ref[k].astype(F32)
                    acc = term if acc is None else acc + term
            pos += len(g)
            o_ref[...] = acc.astype(o_ref.dtype)

    return pl.pallas_call(
        body, name=name, grid=(steps,),
        in_specs=[pl.BlockSpec((a.shape[0], a.shape[1] // steps, a.shape[2]), lambda i: (0, i, 0)) for a in flat],
        out_specs=[pl.BlockSpec((g[0].shape[1] // steps, g[0].shape[2]), lambda i: (i, 0)) for g in groups],
        out_shape=[jax.ShapeDtypeStruct(g[0].shape[1:], out_dtype) for g in groups],
        compiler_params=_cp(("parallel",), 40),
    )(*flat)


def _adamw_update(w_ref, g_ref, m_ref, v_ref, d_ref, mo_ref, vo_ref):
    gv = g_ref[...]
    mn = ADAM_B1 * m_ref[...] + (1.0 - ADAM_B1) * gv
    vn = ADAM_B2 * v_ref[...] + (1.0 - ADAM_B2) * (gv * gv)
    mo_ref[...] = mn
    vo_ref[...] = vn
    m_hat = mn / (1.0 - ADAM_B1 ** ADAM_STEP)
    v_hat = vn / (1.0 - ADAM_B2 ** ADAM_STEP)
    d_ref[...] = (-ADAM_LR) * (m_hat / (jnp.sqrt(v_hat) + ADAM_EPS) + ADAM_WD * w_ref[...])


def _adamw(w, g, m, v, name):
    r, cdim = w.shape
    tr = math.gcd(r, 512)

    def body(*refs):
        _adamw_update(*refs)

    blk = pl.BlockSpec((tr, cdim), lambda i: (i, 0))
    return pl.pallas_call(
        body, name=name, grid=(r // tr,),
        in_specs=[blk] * 4, out_specs=[blk] * 3,
        out_shape=[jax.ShapeDtypeStruct((r, cdim), F32)] * 3,
        compiler_params=_cp(("parallel",), 40),
    )(w, g, m, v)


def _adamw_small(ws, gs, ms, vs, name):
    n = len(ws)

    def body(*refs):
        ins, outs = refs[:4 * n], refs[4 * n:]
        for t in range(n):
            _adamw_update(ins[t], ins[n + t], ins[2 * n + t], ins[3 * n + t], outs[t], outs[n + t], outs[2 * n + t])

    shapes = [jax.ShapeDtypeStruct(w.shape, F32) for w in ws]
    outs = pl.pallas_call(body, name=name, out_shape=shapes * 3)(*ws, *gs, *ms, *vs)
    return outs[:n], outs[n:2 * n], outs[2 * n:]


HALF = DEPTH // 2
BIG = ['w_in', 'w_uq', 'w_ukv', 'w_out']
SHARD_AXIS = {'w_in': 2, 'conv_w': 2, 'w_uq': 2, 'w_ukv': 2, 'w_out': 1}
FULL_SHAPE = {'w_in': (DEPTH, D_MODEL, D_IN), 'conv_w': (DEPTH, CONV_WIDTH, D_LRU),
              'w_uq': (DEPTH, Q_RANK, MLA_HEADS * (QK_NOPE + QK_ROPE)),
              'w_ukv': (DEPTH, KV_RANK, MLA_HEADS * (QK_NOPE + V_DIM)), 'w_out': (DEPTH, D_MIX, D_MODEL)}


def _shard_shape(n):
    shp = list(FULL_SHAPE[n])
    shp[SHARD_AXIS[n]] //= N_SHARD
    return tuple(shp)


def _rows_view(a, lead=0):
    return a.reshape(a.shape[:lead] + (-1, a.shape[-1]))


def _gather_weights(local):
    c = lax.axis_index("c")
    names = BIG + ['conv_w']
    halves = [lax.dynamic_slice_in_dim(local[n], HALF * c, HALF, axis=0) for n in names]
    halves = [h.astype(WIRE_DTYPE) if n in BIG else h for n, h in zip(names, halves)]
    got = _all_gather8(halves, "gather_weights")
    full = {}
    for n, g in zip(names, got):
        g = g.reshape((N_SHARD, DEPTH) + g.shape[2:])
        if n == 'w_in':
            full[n] = _permute_w_in([g[s] for s in range(N_SHARD)])
        else:
            full[n] = jnp.moveaxis(g, 0, SHARD_AXIS[n]).reshape(FULL_SHAPE[n])
    return full


def _shard_blocks(g, n):
    width = _shard_shape(n)[SHARD_AXIS[n]]

    def block(h, s):
        part = g[HALF * h:HALF * (h + 1)]
        if n == 'w_in':
            part = _w_in_shard(part, s)
        else:
            part = lax.slice_in_dim(part, s * width, (s + 1) * width, axis=SHARD_AXIS[n])
        return _rows_view(part)

    return jnp.stack([jnp.stack([block(h, s) for s in range(N_SHARD)]) for h in range(2)]).astype(WIRE_DTYPE)


SUM_STEPS = 8


def _reduce_big(grads):
    c = lax.axis_index("c")
    shard = 2 * lax.axis_index("x") + lax.axis_index("y")
    contrib = [_shard_blocks(grads[n], n) for n in BIG]
    from_sibling = _sibling_send(contrib, "pair_exchange_big", which_half=True)
    own_half = [lax.dynamic_index_in_dim(a, c, 0, keepdims=False) for a in contrib]
    pair_sum = _sum_leading([[_rows_view(o)[None], _rows_view(r)[None]] for o, r in zip(own_half, from_sibling)],
                            WIRE_DTYPE, SUM_STEPS, "pair_sum_big")
    to_chips = [p.reshape(a.shape[1:]) for p, a in zip(pair_sum, contrib)]
    from_chips = _chip_exchange(to_chips, "chip_exchange_big")
    own_block = [lax.dynamic_index_in_dim(a, shard, 0, keepdims=True) for a in to_chips]
    mine = _sum_leading([[o, r] for o, r in zip(own_block, from_chips)], F32, SUM_STEPS, "chip_sum_big")
    theirs = _sibling_send(mine, "sibling_big")
    both = [jnp.where(c == 0, jnp.stack([m, t]), jnp.stack([t, m])) for m, t in zip(mine, theirs)]
    return {n: b.reshape(_shard_shape(n)) for n, b in zip(BIG, both)}


SMALL = REPLICATED + ['conv_w']


def _reduce_small(grads):
    views = [_rows_view(jnp.atleast_2d(grads[n])) for n in SMALL]
    sums = _sum_leading([[g] for g in _all_gather8(views, "gather_small")], F32, 1, "sum_small")
    return {n: s.reshape(grads[n].shape) for n, s in zip(SMALL, sums)}


def kernel(x, norm_g, w_in, conv_w, conv_b, w_rg, b_rg, w_ig, b_ig, lru_lambda, q_norm_g, w_uq, kv_norm_g, w_ukv, w_pool, pool_scale, w_out, final_norm_g, loss_target, m_norm_g, m_w_in, m_conv_w, m_conv_b, m_w_rg, m_b_rg, m_w_ig, m_b_ig, m_lru_lambda, m_q_norm_g, m_w_uq, m_kv_norm_g, m_w_ukv, m_w_pool, m_pool_scale, m_w_out, m_final_norm_g, v_norm_g, v_w_in, v_conv_w, v_conv_b, v_w_rg, v_b_rg, v_w_ig, v_b_ig, v_lru_lambda, v_q_norm_g, v_w_uq, v_kv_norm_g, v_w_ukv, v_w_pool, v_pool_scale, v_w_out, v_final_norm_g):
    w_loc = dict(zip(WEIGHT_NAMES, (norm_g, w_in, conv_w, conv_b, w_rg, b_rg, w_ig, b_ig, lru_lambda, q_norm_g, w_uq,
                                    kv_norm_g, w_ukv, w_pool, pool_scale, w_out, final_norm_g)))
    m_loc = dict(zip(WEIGHT_NAMES, (m_norm_g, m_w_in, m_conv_w, m_conv_b, m_w_rg, m_b_rg, m_w_ig, m_b_ig, m_lru_lambda,
                                    m_q_norm_g, m_w_uq, m_kv_norm_g, m_w_ukv, m_w_pool, m_pool_scale, m_w_out,
                                    m_final_norm_g)))
    v_loc = dict(zip(WEIGHT_NAMES, (v_norm_g, v_w_in, v_conv_w, v_conv_b, v_w_rg, v_b_rg, v_w_ig, v_b_ig, v_lru_lambda,
                                    v_q_norm_g, v_w_uq, v_kv_norm_g, v_w_ukv, v_w_pool, v_pool_scale, v_w_out,
                                    v_final_norm_g)))
    w_full = dict(w_loc)
    w_full.update(_gather_weights(w_loc))
    loss_local, dx, g_local = _local_step(x[0], loss_target[0], w_full)
    loss = lax.psum(loss_local, ("x", "y", "c"))

    grads = _reduce_big(g_local)
    g_small = _reduce_small(g_local)
    shard = 2 * lax.axis_index("x") + lax.axis_index("y")
    width = D_LRU // N_SHARD
    grads['conv_w'] = lax.dynamic_slice_in_dim(g_small['conv_w'], shard * width, width, axis=2)
    for n in REPLICATED:
        grads[n] = g_small[n]

    delta, new_m, new_v = {}, {}, {}
    for n in BIG:
        d, mo, vo = _adamw(*[_rows_view(t[n]) for t in (w_loc, grads, m_loc, v_loc)], "adamw_" + n)
        delta[n], new_m[n], new_v[n] = (a.reshape(w_loc[n].shape) for a in (d, mo, vo))
    small = [[_rows_view(jnp.atleast_2d(t[n])) for n in SMALL] for t in (w_loc, grads, m_loc, v_loc)]
    for tree, outs in zip((delta, new_m, new_v), _adamw_small(*small, "adamw_small")):
        tree.update({n: a.reshape(w_loc[n].shape) for n, a in zip(SMALL, outs)})

    return (loss, dx[None], *[grads[n] for n in WEIGHT_NAMES], *[delta[n] for n in WEIGHT_NAMES],
            *[new_m[n] for n in WEIGHT_NAMES], *[new_v[n] for n in WEIGHT_NAMES])
```

```python
import functools
import math

import jax
import jax.numpy as jnp
import numpy as np
from jax import lax
from jax.experimental import pallas as pl
from jax.experimental.pallas import tpu as pltpu

F32 = jnp.float32
MXU_DTYPE = jnp.bfloat16

D_MODEL = 1024
DEPTH = 4
EPS = 1e-6
D_LRU = 384
LRU_HEADS = 6
CONV_WIDTH = 4
LRU_C = 8.0
MLA_HEADS = 6
QK_NOPE = 64
QK_ROPE = 32
V_DIM = 64
D_MLA = MLA_HEADS * V_DIM
Q_RANK = 384
KV_RANK = 256
ROPE_BASE = 10000.0
POOL_WINDOWS = (2, 4, 8, 16)
D_POOL = 256
D_MIX = D_LRU + D_MLA + D_POOL
D_IN = 2336
ATT_SCALE = (QK_NOPE + QK_ROPE) ** -0.5
LOG2_E = 1.4426950408889634
LN_2 = 0.6931471805599453
Q_PRESCALE = ATT_SCALE * LOG2_E

ADAM_LR = 0.001
ADAM_B1 = 0.9
ADAM_B2 = 0.999
ADAM_EPS = 1e-08
ADAM_WD = 0.01
ADAM_STEP = 10

LANES = 128
SUBLANES = 8
V7X_VMEM_BYTES = 64 << 20
N_DEV = 8

D_INP = 2432
KR_LANE0 = 64
HEAD_BLK = 128
N_PAIR = MLA_HEADS // 2

WEIGHT_NAMES = ['norm_g', 'w_in', 'conv_w', 'conv_b', 'w_rg', 'b_rg', 'w_ig', 'b_ig', 'lru_lambda', 'q_norm_g',
                'w_uq', 'kv_norm_g', 'w_ukv', 'w_pool', 'pool_scale', 'w_out', 'final_norm_g']
SHARDED = ['w_in', 'conv_w', 'w_uq', 'w_ukv', 'w_out']
REPLICATED = [n for n in WEIGHT_NAMES if n not in SHARDED]


def _cp(sem, vmem_mb=None):
    return pltpu.CompilerParams(dimension_semantics=sem,
                                vmem_limit_bytes=None if vmem_mb is None else vmem_mb << 20)


def _dot(a, b):
    return jnp.dot(a.astype(MXU_DTYPE), b.astype(MXU_DTYPE), preferred_element_type=F32)


def _dot_nt(a, b):
    return lax.dot_general(a.astype(MXU_DTYPE), b.astype(MXU_DTYPE), (((1,), (1,)), ((), ())),
                           preferred_element_type=F32)


def _dot_tn(a, b):
    return lax.dot_general(a.astype(MXU_DTYPE), b.astype(MXU_DTYPE), (((0,), (0,)), ((), ())),
                           preferred_element_type=F32)


def _sig(x):
    return 1.0 / (1.0 + jnp.exp(-x))


def _down(x, k):
    return pltpu.roll(x, k, 0)


def _up(x, k):
    return pltpu.roll(x, x.shape[0] - k, 0)


def _rows(shape):
    return lax.broadcasted_iota(jnp.int32, shape, 0)


def _lanes(shape):
    return lax.broadcasted_iota(jnp.int32, shape, 1)


def _tile(s):
    return min(512, s)


def _lay(l):
    return lambda shp, blk=0: pl.BlockSpec((None,) + shp, lambda *_: (l, blk, 0))


def _inproj_fwd(x, g, w, l):
    s = x.shape[0]
    t = _tile(s)
    lay = _lay(l)

    def body(x_ref, g_ref, w_ref, h_ref, z_ref):
        xv = x_ref[...]
        rs = lax.rsqrt(jnp.mean(xv * xv, axis=-1, keepdims=True) + EPS)
        h = (xv * rs * g_ref[...]).astype(MXU_DTYPE)
        h_ref[...] = h
        z_ref[...] = jnp.dot(h, w_ref[...], preferred_element_type=F32)

    return pl.pallas_call(
        body, name="inproj_fwd", grid=(s // t,),
        in_specs=[pl.BlockSpec((t, D_MODEL), lambda i: (i, 0)),
                  lay((1, D_MODEL)), lay((D_MODEL, D_INP))],
        out_specs=[pl.BlockSpec((t, D_MODEL), lambda i: (i, 0)),
                   pl.BlockSpec((t, D_INP), lambda i: (i, 0))],
        out_shape=[jax.ShapeDtypeStruct((s, D_MODEL), MXU_DTYPE), jax.ShapeDtypeStruct((s, D_INP), F32)],
        compiler_params=_cp(("parallel",), 40),
    )(x, g, w)


def _lru_gates(za, halo, cw_ref, cb_ref, wr_ref, br_ref, wi_ref, bi_ref, lam_ref):
    t = za.shape[0]
    ext = jnp.concatenate([halo, za], axis=0)
    sh = [za] + [_down(ext, j)[SUBLANES:SUBLANES + t] for j in (1, 2, 3)]
    xa = cb_ref[...] + cw_ref[3:4, :] * sh[0] + cw_ref[2:3, :] * sh[1] + cw_ref[1:2, :] * sh[2] + cw_ref[0:1, :] * sh[3]
    r = _sig(_dot(xa, wr_ref[...]) + br_ref[...])
    ig = _sig(_dot(xa, wi_ref[...]) + bi_ref[...])
    lam = lam_ref[...]
    sp = jnp.maximum(-lam, 0.0) + jnp.log(1.0 + jnp.exp(-jnp.abs(lam)))
    la = (-LRU_C) * r * sp
    a = jnp.exp(la)
    y2 = 2.0 * la
    taylor = -(y2 * (1.0 + y2 * (0.5 + y2 * (1.0 / 6.0 + y2 * (1.0 / 24.0)))))
    m2 = jnp.where(y2 > -0.02, taylor, 1.0 - jnp.exp(y2))
    m = jnp.sqrt(m2)
    return xa, sh, r, ig, sp, a, m


def _lru_fwd(z, cw, cb, wr, br, wi, bi, lam, l):
    s = z.shape[0]
    t = _tile(s)
    c = D_LRU
    lay = _lay(l)

    def body(za_ref, ga_ref, cw_ref, cb_ref, wr_ref, br_ref, wi_ref, bi_ref, lam_ref, ya_ref, hs_ref, zprev, hcar):
        i = pl.program_id(0)

        @pl.when(i == 0)
        def _():
            zprev[...] = jnp.zeros_like(zprev)
            hcar[...] = jnp.zeros_like(hcar)

        za = za_ref[...]
        xa, _, _, ig, _, a, m = _lru_gates(za, zprev[...], cw_ref, cb_ref, wr_ref, br_ref, wi_ref, bi_ref, lam_ref)
        row = _rows((t, c))
        acc_h = m * (ig * xa) + jnp.where(row == 0, a * hcar[...], 0.0)
        acc_a = jnp.where(row == 0, 0.0, a)
        k = 1
        while k < t:
            acc_h = acc_h + acc_a * _down(acc_h, k)
            acc_a = acc_a * _down(acc_a, k)
            k *= 2
        hs = acc_h
        hs_ref[...] = hs
        ga = ga_ref[...]
        ya_ref[...] = (hs * (ga * _sig(ga))).astype(ya_ref.dtype)
        hcar[...] = jnp.sum(jnp.where(row == t - 1, hs, 0.0), axis=0, keepdims=True)
        zprev[...] = za_ref[t - SUBLANES:t, :]

    return pl.pallas_call(
        body, name="lru_fwd", grid=(s // t,),
        in_specs=[pl.BlockSpec((t, c), lambda i: (i, 0)), pl.BlockSpec((t, c), lambda i: (i, 1)),
                  lay((CONV_WIDTH, c)), lay((1, c)), lay((c, c)), lay((1, c)), lay((c, c)), lay((1, c)),
                  lay((1, c))],
        out_specs=[pl.BlockSpec((t, c), lambda i: (i, 0)), pl.BlockSpec((t, c), lambda i: (i, 0))],
        out_shape=[jax.ShapeDtypeStruct((s, c), MXU_DTYPE), jax.ShapeDtypeStruct((s, c), F32)],
        scratch_shapes=[pltpu.VMEM((SUBLANES, c), F32), pltpu.VMEM((1, c), F32)],
        compiler_params=_cp(("arbitrary",), 40),
    )(z, z, cw, cb, wr, br, wi, bi, lam)


def _lru_bwd(z, hs, dy, cw, cb, wr, br, wi, bi, lam, l):
    s = z.shape[0]
    t = _tile(s)
    lay = _lay(l)
    nt = s // t
    c = D_LRU
    hb = t // SUBLANES

    def body(za_ref, zh_ref, ga_ref, hs_ref, hh_ref, dy_ref, cw_ref, cb_ref, wr_ref, br_ref, wi_ref, bi_ref, lam_ref,
             dza_ref, dga_ref, dwr_ref, dwi_ref, dcw_ref, dcb_ref, dbr_ref, dbi_ref, dlam_ref, lcar, dxn):
        i = pl.program_id(0)
        tt = nt - 1 - i

        @pl.when(i == 0)
        def _():
            lcar[...] = jnp.zeros_like(lcar)
            dxn[...] = jnp.zeros_like(dxn)
            for ref in (dwr_ref, dwi_ref, dcw_ref, dcb_ref, dbr_ref, dbi_ref, dlam_ref):
                ref[...] = jnp.zeros_like(ref)

        first = (tt > 0).astype(F32)
        za = za_ref[...]
        xa, sh, r, ig, sp, a, m = _lru_gates(za, zh_ref[...] * first, cw_ref, cb_ref, wr_ref, br_ref, wi_ref, bi_ref,
                                             lam_ref)
        hs_v = hs_ref[...]
        hprev = _down(jnp.concatenate([hh_ref[...] * first, hs_v], axis=0), 1)[SUBLANES:SUBLANES + t]
        ga = ga_ref[...]
        sg = _sig(ga)
        silu = ga * sg
        dya = dy_ref[...]
        dga_ref[...] = (dya * hs_v * (sg * (1.0 + ga * (1.0 - sg)))).astype(dga_ref.dtype)
        row = _rows((t, c))
        acc_h = dya * silu + jnp.where(row == t - 1, lcar[...], 0.0)
        acc_b = jnp.where(row < t - 1, _up(a, 1), 0.0)
        k = 1
        while k < t:
            acc_h = acc_h + acc_b * _up(acc_h, k)
            acc_b = acc_b * _up(acc_b, k)
            k *= 2
        lmb = acc_h
        lcar[...] = jnp.sum(jnp.where(row == 0, a * lmb, 0.0), axis=0, keepdims=True)
        da = lmb * hprev
        dxa = lmb * m * ig
        di = lmb * m * xa
        dm = lmb * ig * xa
        dla = da * a - dm * (a * a) / m
        dr = dla * ((-LRU_C) * sp)
        lam = lam_ref[...]
        dsp = jnp.sum(dla * ((-LRU_C) * r), axis=0, keepdims=True)
        dlam_ref[...] += dsp * (-1.0 / (1.0 + jnp.exp(lam)))
        dpr = dr * r * (1.0 - r)
        dpi = di * ig * (1.0 - ig)
        dbr_ref[...] += jnp.sum(dpr, axis=0, keepdims=True)
        dbi_ref[...] += jnp.sum(dpi, axis=0, keepdims=True)
        dwr_ref[...] += _dot_tn(xa, dpr)
        dwi_ref[...] += _dot_tn(xa, dpi)
        dxa = dxa + _dot_nt(dpr, wr_ref[...]) + _dot_nt(dpi, wi_ref[...])
        dcb_ref[...] += jnp.sum(dxa, axis=0, keepdims=True)
        for k in range(CONV_WIDTH):
            dcw_ref[k:k + 1, :] += jnp.sum(dxa * sh[CONV_WIDTH - 1 - k], axis=0, keepdims=True)
        ext = jnp.concatenate([dxa, dxn[...]], axis=0)
        dza = cw_ref[3:4, :] * dxa
        for j in (1, 2, 3):
            dza = dza + cw_ref[3 - j:4 - j, :] * _up(ext, j)[:t]
        dza_ref[...] = dza.astype(dza_ref.dtype)
        dxn[...] = dxa[:SUBLANES]

    full = lambda shp: pl.BlockSpec(shp, lambda i: (0, 0))
    rev = lambda i: nt - 1 - i
    halo = lambda i: (jnp.maximum((nt - 1 - i) * hb - 1, 0), 0)
    outs = pl.pallas_call(
        body, name="lru_bwd", grid=(nt,),
        in_specs=[pl.BlockSpec((t, c), lambda i: (rev(i), 0)), pl.BlockSpec((SUBLANES, c), halo),
                  pl.BlockSpec((t, c), lambda i: (rev(i), 1)),
                  pl.BlockSpec((t, c), lambda i: (rev(i), 0)), pl.BlockSpec((SUBLANES, c), halo),
                  pl.BlockSpec((t, c), lambda i: (rev(i), 0)),
                  lay((CONV_WIDTH, c)), lay((1, c)), lay((c, c)), lay((1, c)), lay((c, c)), lay((1, c)),
                  lay((1, c))],
        out_specs=[pl.BlockSpec((t, c), lambda i: (rev(i), 0)), pl.BlockSpec((t, c), lambda i: (rev(i), 0)),
                   full((c, c)), full((c, c)), full((CONV_WIDTH, c)), full((1, c)), full((1, c)), full((1, c)),
                   full((1, c))],
        out_shape=[jax.ShapeDtypeStruct((s, c), MXU_DTYPE), jax.ShapeDtypeStruct((s, c), MXU_DTYPE),
                   jax.ShapeDtypeStruct((c, c), F32), jax.ShapeDtypeStruct((c, c), F32),
                   jax.ShapeDtypeStruct((CONV_WIDTH, c), F32)] + [jax.ShapeDtypeStruct((1, c), F32)] * 4,
        scratch_shapes=[pltpu.VMEM((1, c), F32), pltpu.VMEM((SUBLANES, c), F32)],
        compiler_params=_cp(("arbitrary",), 48),
    )(z, z, z, hs, hs, dy, cw, cb, wr, br, wi, bi, lam)
    return outs


POOL_HALO = 16


def _pool_select(lane, v2, v4, v8, v16):
    return jnp.where(lane < 64, v2, jnp.where(lane < 128, v4, jnp.where(lane < 192, v8, v16)))


def _pool_counts(t0, t, c):
    lane = _lanes((t, c))
    win = _pool_select(lane, 2.0, 4.0, 8.0, 16.0)
    seen = (t0 + _rows((t, c)) + 1).astype(F32)
    return lane, jnp.minimum(seen, win)


def _pooled(zc, halo, lane, cnt):
    t = zc.shape[0]
    ext = jnp.concatenate([halo, zc], axis=0)
    s2 = ext + _down(ext, 1)
    s4 = s2 + _down(s2, 2)
    s8 = s4 + _down(s4, 4)
    s16 = s8 + _down(s8, 8)
    cut = lambda v: v[POOL_HALO:POOL_HALO + t]
    return _pool_select(lane, cut(s2), cut(s4), cut(s8), cut(s16)) / cnt - zc


def _pool_fwd(z, wp, ps, l):
    s = z.shape[0]
    t = _tile(s)
    c = D_POOL
    lay = _lay(l)

    def body(zc_ref, gc_ref, wp_ref, ps_ref, yc_ref, zprev):
        i = pl.program_id(0)

        @pl.when(i == 0)
        def _():
            zprev[...] = jnp.zeros_like(zprev)

        zc = zc_ref[...]
        lane, cnt = _pool_counts(i * t, t, c)
        pooled = _pooled(zc, zprev[...], lane, cnt)
        pc = _dot(pooled, wp_ref[...])
        gc = gc_ref[...]
        yc_ref[...] = (pc * ps_ref[...] * (gc * _sig(gc))).astype(yc_ref.dtype)
        zprev[...] = zc_ref[t - POOL_HALO:t, :]

    return pl.pallas_call(
        body, name="pool_fwd", grid=(s // t,),
        in_specs=[pl.BlockSpec((t, c), lambda i: (i, 7)), pl.BlockSpec((t, c), lambda i: (i, 8)),
                  lay((c, c)), lay((1, c))],
        out_specs=pl.BlockSpec((t, c), lambda i: (i, 0)),
        out_shape=jax.ShapeDtypeStruct((s, c), MXU_DTYPE),
        scratch_shapes=[pltpu.VMEM((POOL_HALO, c), F32)],
        compiler_params=_cp(("arbitrary",), 32),
    )(z, z, wp, ps)


def _pool_bwd(z, dy, wp, ps, l):
    s = z.shape[0]
    t = _tile(s)
    lay = _lay(l)
    nt = s // t
    c = D_POOL
    hb = t // POOL_HALO

    def body(zc_ref, zh_ref, gc_ref, dy_ref, wp_ref, ps_ref, dzc_ref, dgc_ref, dwp_ref, dps_ref, ddn):
        i = pl.program_id(0)
        tt = nt - 1 - i

        @pl.when(i == 0)
        def _():
            ddn[...] = jnp.zeros_like(ddn)
            dwp_ref[...] = jnp.zeros_like(dwp_ref)
            dps_ref[...] = jnp.zeros_like(dps_ref)

        first = (tt > 0).astype(F32)
        zc = zc_ref[...]
        lane, cnt = _pool_counts(tt * t, t, c)
        pooled = _pooled(zc, zh_ref[...] * first, lane, cnt)
        pc = _dot(pooled, wp_ref[...])
        gc = gc_ref[...]
        sg = _sig(gc)
        silu = gc * sg
        dyc = dy_ref[...]
        ps_v = ps_ref[...]
        dgc_ref[...] = (dyc * pc * ps_v * (sg * (1.0 + gc * (1.0 - sg)))).astype(dgc_ref.dtype)
        dps_ref[...] += jnp.sum(dyc * pc * silu, axis=0, keepdims=True)
        dpc = dyc * ps_v * silu
        dwp_ref[...] += _dot_tn(pooled, dpc)
        dpooled = _dot_nt(dpc, wp_ref[...])
        dd = dpooled / cnt
        ext = jnp.concatenate([dd, ddn[...]], axis=0)
        f2 = ext + _up(ext, 1)
        f4 = f2 + _up(f2, 2)
        f8 = f4 + _up(f4, 4)
        f16 = f8 + _up(f8, 8)
        dzc = _pool_select(lane, f2[:t], f4[:t], f8[:t], f16[:t]) - dpooled
        dzc_ref[...] = dzc.astype(dzc_ref.dtype)
        ddn[...] = dd[:POOL_HALO]

    full = lambda shp: pl.BlockSpec(shp, lambda i: (0, 0))
    rev = lambda i: nt - 1 - i
    return pl.pallas_call(
        body, name="pool_bwd", grid=(nt,),
        in_specs=[pl.BlockSpec((t, c), lambda i: (rev(i), 7)),
                  pl.BlockSpec((POOL_HALO, c), lambda i: (jnp.maximum(rev(i) * hb - 1, 0), 7)),
                  pl.BlockSpec((t, c), lambda i: (rev(i), 8)),
                  pl.BlockSpec((t, c), lambda i: (rev(i), 3)),
                  lay((c, c)), lay((1, c))],
        out_specs=[pl.BlockSpec((t, c), lambda i: (rev(i), 0)), pl.BlockSpec((t, c), lambda i: (rev(i), 0)),
                   full((c, c)), full((1, c))],
        out_shape=[jax.ShapeDtypeStruct((s, c), MXU_DTYPE), jax.ShapeDtypeStruct((s, c), MXU_DTYPE),
                   jax.ShapeDtypeStruct((c, c), F32), jax.ShapeDtypeStruct((1, c), F32)],
        scratch_shapes=[pltpu.VMEM((POOL_HALO, c), F32)],
        compiler_params=_cp(("arbitrary",), 32),
    )(z, z, z, dy, wp, ps)


def _rope_tables(s):
    pos = jnp.arange(s, dtype=F32)
    inv_freq = ROPE_BASE ** (-jnp.arange(0, QK_ROPE, 2, dtype=F32) / QK_ROPE)
    ang = pos[:, None] * inv_freq[None, :]
    cos, sin = jnp.cos(ang), jnp.sin(ang)
    half = QK_ROPE // 2
    ones = jnp.ones((s, QK_NOPE), F32)
    z64 = jnp.zeros((s, QK_NOPE), F32)
    zh = jnp.zeros((s, half), F32)
    z32 = jnp.zeros((s, HEAD_BLK - QK_NOPE - QK_ROPE), F32)
    c_t = jnp.concatenate([ones, cos, cos, z32], axis=1)
    s1_t = jnp.concatenate([z64, -sin, zh, z32], axis=1)
    s2_t = jnp.concatenate([z64, zh, sin, z32], axis=1)
    return c_t, s1_t, s2_t


def _rope(x, c_t, s1_t, s2_t):
    return x * c_t + pltpu.roll(x, HEAD_BLK - 16, 1) * s1_t + pltpu.roll(x, 16, 1) * s2_t


def _unrope(d, c_t, s1_t, s2_t):
    return d * c_t + pltpu.roll(d * s1_t, 16, 1) + pltpu.roll(d * s2_t, HEAD_BLK - 16, 1)


def _mla_prep_fwd(z, gq, gkv, wuq, wukv, c_t, s1_t, s2_t, l):
    s = z.shape[0]
    t = _tile(s)
    hq = MLA_HEADS * HEAD_BLK
    lay = _lay(l)

    def body(cq_ref, ckv_ref, kr_ref, gq_ref, gkv_ref, wuq_ref, wukv_ref, c_ref, s1_ref, s2_ref,
             q_ref, k_ref, v_ref, qn_ref, kvn_ref):
        ct, s1, s2 = c_ref[...], s1_ref[...], s2_ref[...]
        cq = cq_ref[...]
        qn = (cq * lax.rsqrt(jnp.mean(cq * cq, axis=-1, keepdims=True) + EPS) * gq_ref[...]).astype(MXU_DTYPE)
        qn_ref[...] = qn
        q = jnp.dot(qn, wuq_ref[...], preferred_element_type=F32)
        ckv = ckv_ref[...]
        kvn = (ckv * lax.rsqrt(jnp.mean(ckv * ckv, axis=-1, keepdims=True) + EPS) * gkv_ref[...]).astype(MXU_DTYPE)
        kvn_ref[...] = kvn
        kvp = jnp.dot(kvn, wukv_ref[...], preferred_element_type=F32)
        krr = _rope(kr_ref[...], ct, s1, s2)
        for h in range(MLA_HEADS):
            blk = slice(h * HEAD_BLK, (h + 1) * HEAD_BLK)
            q_ref[:, blk] = (_rope(q[:, blk], ct, s1, s2) * Q_PRESCALE).astype(q_ref.dtype)
            k_ref[:, blk] = (kvp[:, blk] + krr).astype(k_ref.dtype)
        v_ref[...] = kvp[:, hq:].astype(v_ref.dtype)

    tab = pl.BlockSpec((t, HEAD_BLK), lambda i: (i, 0))
    return pl.pallas_call(
        body, name="mla_prep_fwd", grid=(s // t,),
        in_specs=[pl.BlockSpec((t, Q_RANK), lambda i: (i, 2)), pl.BlockSpec((t, KV_RANK), lambda i: (i, 6)),
                  pl.BlockSpec((t, HEAD_BLK), lambda i: (i, 18)),
                  lay((1, Q_RANK)), lay((1, KV_RANK)), lay((Q_RANK, hq)), lay((KV_RANK, hq + D_MLA)),
                  tab, tab, tab],
        out_specs=[pl.BlockSpec((t, hq), lambda i: (i, 0)), pl.BlockSpec((t, hq), lambda i: (i, 0)),
                   pl.BlockSpec((t, D_MLA), lambda i: (i, 0)), pl.BlockSpec((t, Q_RANK), lambda i: (i, 0)),
                   pl.BlockSpec((t, KV_RANK), lambda i: (i, 0))],
        out_shape=[jax.ShapeDtypeStruct((s, hq), MXU_DTYPE), jax.ShapeDtypeStruct((s, hq), MXU_DTYPE),
                   jax.ShapeDtypeStruct((s, D_MLA), MXU_DTYPE), jax.ShapeDtypeStruct((s, Q_RANK), MXU_DTYPE),
                   jax.ShapeDtypeStruct((s, KV_RANK), MXU_DTYPE)],
        compiler_params=_cp(("parallel",), 40),
    )(z, z, z, gq, gkv, wuq, wukv, c_t, s1_t, s2_t)


SUM_LANE_A = V_DIM
SUM_LANE_B = 0
FWD_TILES_PER_TRIP = 4


def _flash_fwd(q, k, v, z):
    s = q.shape[0]
    t = _tile(s)
    nq = s // t
    pw = 2 * HEAD_BLK

    def body(q_ref, k_ref, v_ref, gb_ref, o_ref, yb_ref, lse_ref):
        i = pl.program_id(1)
        qv = q_ref[...]
        qa, qb = qv[:, :HEAD_BLK], qv[:, HEAD_BLK:]
        lane = _lanes((t, HEAD_BLK))
        lo = lane < V_DIM

        def update(qh, kh, vh, m, acc, masked):
            sc = _dot_nt(qh, kh)
            if masked:
                sc = jnp.where(_lanes((t, t)) <= _rows((t, t)), sc, -1e30)
            m_new = jnp.maximum(m, jnp.max(sc, axis=-1, keepdims=True))
            p = jnp.exp2(sc - m_new).astype(MXU_DTYPE)
            return m_new, acc * jnp.exp2(m - m_new) + _dot(p, vh)

        def step(j, carry, masked):
            ma, mb, acc_a, acc_b = carry
            kv_rows = pl.ds(pl.multiple_of(j * t, t), t)
            kt = k_ref[kv_rows, :]
            vt = v_ref[kv_rows, :]
            lane_v = _lanes(vt.shape)
            one = jnp.ones_like(vt)
            zero_v = jnp.zeros_like(vt)
            v_a = jnp.where(lane_v < V_DIM, vt, jnp.where(lane_v == SUM_LANE_A, one, zero_v))
            v_b = jnp.where(lane_v >= V_DIM, vt, jnp.where(lane_v == SUM_LANE_B, one, zero_v))
            ma, acc_a = update(qa, kt[:, :HEAD_BLK], v_a, ma, acc_a, masked)
            mb, acc_b = update(qb, kt[:, HEAD_BLK:], v_b, mb, acc_b, masked)
            return ma, mb, acc_a, acc_b

        neg = jnp.full((t, 1), -1e30, F32)
        zero = jnp.zeros((t, HEAD_BLK), F32)
        def trip(jj, cr):
            for u in range(FWD_TILES_PER_TRIP):
                cr = step(FWD_TILES_PER_TRIP * jj + u, cr, False)
            return cr

        whole = i // FWD_TILES_PER_TRIP
        carry = lax.fori_loop(0, whole, trip, (neg, neg, zero, zero))
        carry = lax.fori_loop(FWD_TILES_PER_TRIP * whole, i, lambda j, cr: step(j, cr, False), carry)
        ma, mb, acc_a, acc_b = step(i, carry, True)
        la = jnp.sum(jnp.where(lane == SUM_LANE_A, acc_a, 0.0), axis=-1, keepdims=True)
        lb = jnp.sum(jnp.where(lane == SUM_LANE_B, acc_b, 0.0), axis=-1, keepdims=True)
        o = jnp.where(lo, acc_a * (1.0 / la), acc_b * (1.0 / lb))
        o_ref[...] = o
        gb = gb_ref[...]
        yb_ref[...] = (o * (gb * _sig(gb))).astype(yb_ref.dtype)
        lse = jnp.where(lo, ma + jnp.log(la) * LOG2_E, mb + jnp.log(lb) * LOG2_E)
        pick = ((_rows((SUBLANES, HEAD_BLK)) == 0) & (_lanes((SUBLANES, HEAD_BLK)) == 0)) | (
            (_rows((SUBLANES, HEAD_BLK)) == 1) & (_lanes((SUBLANES, HEAD_BLK)) == V_DIM))
        lse_ref[0, 0] = lax.dot_general(pick.astype(F32), lse, (((1,), (1,)), ((), ())),
                                        precision=lax.Precision.HIGHEST, preferred_element_type=F32)

    return pl.pallas_call(
        body, name="flash_fwd", grid=(N_PAIR, nq),
        in_specs=[pl.BlockSpec((t, pw), lambda p, i: (i, p)), pl.BlockSpec((s, pw), lambda p, i: (0, p)),
                  pl.BlockSpec((s, HEAD_BLK), lambda p, i: (0, p)),
                  pl.BlockSpec((t, HEAD_BLK), lambda p, i: (i, 9 + p))],
        out_specs=[pl.BlockSpec((t, HEAD_BLK), lambda p, i: (i, p)), pl.BlockSpec((t, HEAD_BLK), lambda p, i: (i, p)),
                   pl.BlockSpec((1, 1, SUBLANES, t), lambda p, i: (p, i, 0, 0))],
        out_shape=[jax.ShapeDtypeStruct((s, D_MLA), F32), jax.ShapeDtypeStruct((s, D_MLA), MXU_DTYPE),
                   jax.ShapeDtypeStruct((N_PAIR, nq, SUBLANES, t), F32)],
        compiler_params=_cp(("parallel", "parallel"), 48),
    )(q, k, v, z)


def _attn_bwd_prep(dy, o, z):
    s = o.shape[0]
    t = _tile(s)
    nq = s // t
    rows = N_PAIR * SUBLANES

    def body(dy_ref, o_ref, gb_ref, do_ref, dgb_ref, dl_ref):
        gb = gb_ref[...]
        sg = _sig(gb)
        dyb = dy_ref[...]
        ov = o_ref[...]
        do = dyb * (gb * sg)
        do_ref[...] = do.astype(do_ref.dtype)
        dgb_ref[...] = (dyb * ov * (sg * (1.0 + gb * (1.0 - sg)))).astype(dgb_ref.dtype)
        r = _rows((rows, D_MLA))
        head = (r // SUBLANES) * 2 + (r % SUBLANES)
        sel = ((r % SUBLANES) < 2) & (_lanes((rows, D_MLA)) // V_DIM == head)
        dl = lax.dot_general(sel.astype(F32), do * ov, (((1,), (1,)), ((), ())),
                             precision=lax.Precision.HIGHEST, preferred_element_type=F32)
        for p in range(N_PAIR):
            dl_ref[p, 0] = dl[p * SUBLANES:(p + 1) * SUBLANES]

    return pl.pallas_call(
        body, name="attn_bwd_prep", grid=(nq,),
        in_specs=[pl.BlockSpec((t, D_MLA), lambda i: (i, 1)), pl.BlockSpec((t, D_MLA), lambda i: (i, 0)),
                  pl.BlockSpec((t, D_MLA), lambda i: (i, 3))],
        out_specs=[pl.BlockSpec((t, D_MLA), lambda i: (i, 0)), pl.BlockSpec((t, D_MLA), lambda i: (i, 0)),
                   pl.BlockSpec((N_PAIR, 1, SUBLANES, t), lambda i: (0, i, 0, 0))],
        out_shape=[jax.ShapeDtypeStruct((s, D_MLA), MXU_DTYPE), jax.ShapeDtypeStruct((s, D_MLA), MXU_DTYPE),
                   jax.ShapeDtypeStruct((N_PAIR, nq, SUBLANES, t), F32)],
        compiler_params=_cp(("parallel",), 32),
    )(dy, o, z)


def _flash_bwd(q, k, v, do, lse, delta):
    s = q.shape[0]
    t = _tile(s)
    nq = s // t
    pw = 2 * HEAD_BLK

    def body(q_ref, do_ref, lse_ref, dl_ref, k_ref, v_ref, dq_ref, dk_ref, dv_ref):
        j = pl.program_id(1)

        @pl.when(j == 0)
        def _():
            dq_ref[...] = jnp.zeros_like(dq_ref)

        kt = k_ref[...]
        ka, kb = kt[:, :HEAD_BLK], kt[:, HEAD_BLK:]
        vt = v_ref[...]

        def head(kh, qh, do_h, lse_row, dl_row, masked):
            st = _dot_nt(kh, qh)
            if masked:
                st = jnp.where(_rows((t, t)) <= _lanes((t, t)), st, -1e30)
            pt = jnp.exp2(st - lse_row)
            dv_h = _dot(pt, do_h)
            dst = (pt * (_dot_nt(vt, do_h) - dl_row)).astype(MXU_DTYPE)
            return dv_h, _dot(dst, qh), _dot_tn(dst, kh)

        def step(i, carry, masked):
            dka, dkb, dv = carry
            q_rows = pl.ds(pl.multiple_of(i * t, t), t)
            qv = q_ref[q_rows, :]
            dov = do_ref[q_rows, :]
            lane = _lanes(dov.shape)
            do_lo = jnp.where(lane < V_DIM, dov, jnp.zeros_like(dov))
            do_hi = jnp.where(lane >= V_DIM, dov, jnp.zeros_like(dov))
            dva, dk_a, dq_a = head(ka, qv[:, :HEAD_BLK], do_lo, lse_ref[0, i, 0:1, :], dl_ref[0, i, 0:1, :], masked)
            dvb, dk_b, dq_b = head(kb, qv[:, HEAD_BLK:], do_hi, lse_ref[0, i, 1:2, :], dl_ref[0, i, 1:2, :], masked)
            dq_ref[q_rows, 0:HEAD_BLK] += dq_a
            dq_ref[q_rows, HEAD_BLK:pw] += dq_b
            return dka + dk_a, dkb + dk_b, dv + dva + dvb

        zero = jnp.zeros((t, HEAD_BLK), F32)
        carry = step(j, (zero, zero, zero), True)
        rest = nq - 1 - j
        carry = lax.cond(rest % 2 == 1, lambda cr: step(j + 1, cr, False), lambda cr: cr, carry)
        first = j + 1 + rest % 2
        dka, dkb, dv = lax.fori_loop(
            0, rest // 2, lambda ii, cr: step(first + 2 * ii + 1, step(first + 2 * ii, cr, False), False), carry)
        dk_ref[:, 0:HEAD_BLK] = dka * LN_2
        dk_ref[:, HEAD_BLK:pw] = dkb * LN_2
        dv_ref[...] = dv.astype(dv_ref.dtype)

    return pl.pallas_call(
        body, name="flash_bwd", grid=(N_PAIR, nq),
        in_specs=[pl.BlockSpec((s, pw), lambda p, j: (0, p)), pl.BlockSpec((s, HEAD_BLK), lambda p, j: (0, p)),
                  pl.BlockSpec((1, nq, SUBLANES, t), lambda p, j: (p, 0, 0, 0)),
                  pl.BlockSpec((1, nq, SUBLANES, t), lambda p, j: (p, 0, 0, 0)),
                  pl.BlockSpec((t, pw), lambda p, j: (j, p)), pl.BlockSpec((t, HEAD_BLK), lambda p, j: (j, p))],
        out_specs=[pl.BlockSpec((s, pw), lambda p, j: (0, p)), pl.BlockSpec((t, pw), lambda p, j: (j, p)),
                   pl.BlockSpec((t, HEAD_BLK), lambda p, j: (j, p))],
        out_shape=[jax.ShapeDtypeStruct((s, MLA_HEADS * HEAD_BLK), F32),
                   jax.ShapeDtypeStruct((s, MLA_HEADS * HEAD_BLK), F32),
                   jax.ShapeDtypeStruct((s, D_MLA), MXU_DTYPE)],
        compiler_params=_cp(("parallel", "arbitrary"), 56),
    )(q, do, lse, delta, k, v)


def _mla_prep_bwd(dq, dk, dv, z, gq, gkv, wuq, wukv, c_t, s1_t, s2_t, l):
    s = z.shape[0]
    t = _tile(s)
    hq = MLA_HEADS * HEAD_BLK
    lay = _lay(l)

    def body(dq_ref, dk_ref, dv_ref, cq_ref, ckv_ref, gq_ref, gkv_ref, wuq_ref, wukv_ref, c_ref, s1_ref, s2_ref,
             dcq_ref, dckv_ref, dkr_ref, dqu_ref, dkvp_ref, dgq_ref, dgkv_ref):
        @pl.when(pl.program_id(0) == 0)
        def _():
            dgq_ref[...] = jnp.zeros_like(dgq_ref)
            dgkv_ref[...] = jnp.zeros_like(dgkv_ref)

        ct, s1, s2 = c_ref[...], s1_ref[...], s2_ref[...]
        dk_sum = jnp.zeros((t, HEAD_BLK), F32)
        for h in range(MLA_HEADS):
            blk = slice(h * HEAD_BLK, (h + 1) * HEAD_BLK)
            dqu_ref[:, blk] = _unrope(dq_ref[:, blk] * ATT_SCALE, ct, s1, s2).astype(dqu_ref.dtype)
            dkh = dk_ref[:, blk]
            dk_sum = dk_sum + dkh
            dkvp_ref[:, blk] = dkh.astype(dkvp_ref.dtype)
        dkvp_ref[:, hq:] = dv_ref[...]
        lane = _lanes((t, HEAD_BLK))
        rope_lanes = (lane >= KR_LANE0) & (lane < KR_LANE0 + QK_ROPE)
        dkr_ref[...] = _unrope(jnp.where(rope_lanes, dk_sum, 0.0), ct, s1, s2).astype(dkr_ref.dtype)

        def norm_bwd(c_in, g, dn_out, dc_ref, dg_ref):
            rs = lax.rsqrt(jnp.mean(c_in * c_in, axis=-1, keepdims=True) + EPS)
            n = c_in * rs
            dg_ref[...] += jnp.sum(dn_out * n, axis=0, keepdims=True)
            dn = dn_out * g
            dc_ref[...] = (rs * (dn - n * jnp.mean(dn * n, axis=-1, keepdims=True))).astype(dc_ref.dtype)

        norm_bwd(cq_ref[...], gq_ref[...], _dot_nt(dqu_ref[...], wuq_ref[...]), dcq_ref, dgq_ref)
        norm_bwd(ckv_ref[...], gkv_ref[...], _dot_nt(dkvp_ref[...], wukv_ref[...]), dckv_ref, dgkv_ref)

    full = lambda shp: pl.BlockSpec(shp, lambda i: (0, 0))
    tab = pl.BlockSpec((t, HEAD_BLK), lambda i: (i, 0))
    row = lambda w: pl.BlockSpec((t, w), lambda i: (i, 0))
    return pl.pallas_call(
        body, name="mla_prep_bwd", grid=(s // t,),
        in_specs=[row(hq), row(hq), row(D_MLA),
                  pl.BlockSpec((t, Q_RANK), lambda i: (i, 2)), pl.BlockSpec((t, KV_RANK), lambda i: (i, 6)),
                  lay((1, Q_RANK)), lay((1, KV_RANK)), lay((Q_RANK, hq)), lay((KV_RANK, hq + D_MLA)),
                  tab, tab, tab],
        out_specs=[row(Q_RANK), row(KV_RANK), row(HEAD_BLK), row(hq), row(hq + D_MLA),
                   full((1, Q_RANK)), full((1, KV_RANK))],
        out_shape=[jax.ShapeDtypeStruct((s, Q_RANK), MXU_DTYPE), jax.ShapeDtypeStruct((s, KV_RANK), MXU_DTYPE),
                   jax.ShapeDtypeStruct((s, HEAD_BLK), MXU_DTYPE), jax.ShapeDtypeStruct((s, hq), MXU_DTYPE),
                   jax.ShapeDtypeStruct((s, hq + D_MLA), MXU_DTYPE),
                   jax.ShapeDtypeStruct((1, Q_RANK), F32), jax.ShapeDtypeStruct((1, KV_RANK), F32)],
        compiler_params=_cp(("arbitrary",), 48),
    )(dq, dk, dv, z, z, gq, gkv, wuq, wukv, c_t, s1_t, s2_t)


def _outproj_fwd(x, ya, yb, yc, w, l):
    s = x.shape[0]
    t = _tile(s)
    lay = _lay(l)

    def body(x_ref, ya_ref, yb_ref, yc_ref, wa_ref, wb_ref, wc_ref, o_ref):
        o_ref[...] = (x_ref[...] + _dot(ya_ref[...], wa_ref[...]) + _dot(yb_ref[...], wb_ref[...])
                      + _dot(yc_ref[...], wc_ref[...]))

    row = lambda w_: pl.BlockSpec((t, w_), lambda i: (i, 0))
    return pl.pallas_call(
        body, name="outproj_fwd", grid=(s // t,),
        in_specs=[row(D_MODEL), row(D_LRU), row(D_MLA), row(D_POOL),
                  lay((D_LRU, D_MODEL), 0), lay((D_MLA, D_MODEL), 1), lay((D_POOL, D_MODEL), 3)],
        out_specs=row(D_MODEL),
        out_shape=jax.ShapeDtypeStruct((s, D_MODEL), F32),
        compiler_params=_cp(("parallel",), 40),
    )(x, ya, yb, yc, w, w, w)


def _outproj_bwd(dx, ya, yb, yc, w, l):
    s = dx.shape[0]
    t = _tile(s)

    def body(dx_ref, ya_ref, yb_ref, yc_ref, w_ref, dy_ref, dw_ref):
        @pl.when(pl.program_id(0) == 0)
        def _():
            dw_ref[...] = jnp.zeros_like(dw_ref)

        dxv = dx_ref[...].astype(MXU_DTYPE)
        dy_ref[...] = _dot_nt(dxv, w_ref[...])
        dw_ref[0:D_LRU, :] += _dot_tn(ya_ref[...], dxv)
        dw_ref[D_LRU:D_LRU + D_MLA, :] += _dot_tn(yb_ref[...], dxv)
        dw_ref[D_LRU + D_MLA:D_MIX, :] += _dot_tn(yc_ref[...], dxv)

    row = lambda w_: pl.BlockSpec((t, w_), lambda i: (i, 0))
    return pl.pallas_call(
        body, name="outproj_bwd", grid=(s // t,),
        in_specs=[row(D_MODEL), row(D_LRU), row(D_MLA), row(D_POOL), _lay(l)((D_MIX, D_MODEL))],
        out_specs=[row(D_MIX), pl.BlockSpec((D_MIX, D_MODEL), lambda i: (0, 0))],
        out_shape=[jax.ShapeDtypeStruct((s, D_MIX), F32), jax.ShapeDtypeStruct((D_MIX, D_MODEL), F32)],
        compiler_params=_cp(("arbitrary",), 48),
    )(dx, ya, yb, yc, w)


def _mm_tn(a, b, name):
    s, k1 = a.shape
    n = b.shape[1]
    t = _tile(s)

    def body(a_ref, b_ref, o_ref):
        @pl.when(pl.program_id(0) == 0)
        def _():
            o_ref[...] = jnp.zeros_like(o_ref)

        o_ref[...] += _dot_tn(a_ref[...], b_ref[...])

    return pl.pallas_call(
        body, name=name, grid=(s // t,),
        in_specs=[pl.BlockSpec((t, k1), lambda i: (i, 0)), pl.BlockSpec((t, n), lambda i: (i, 0))],
        out_specs=pl.BlockSpec((k1, n), lambda i: (0, 0)),
        out_shape=jax.ShapeDtypeStruct((k1, n), F32),
        compiler_params=_cp(("arbitrary",), 56),
    )(a, b)


DZ_WIDTHS = (D_LRU, D_LRU, Q_RANK, D_MLA, KV_RANK, D_POOL, D_POOL, HEAD_BLK)


def _dwin(h, dz_parts):
    s = h.shape[0]
    t = _tile(s)

    def body(h_ref, *refs):
        o_ref = refs[-1]

        @pl.when(pl.program_id(0) == 0)
        def _():
            o_ref[...] = jnp.zeros_like(o_ref)

        o_ref[...] += _dot_tn(h_ref[...], jnp.concatenate([r[...] for r in refs[:-1]], axis=1))

    row = lambda w_: pl.BlockSpec((t, w_), lambda i: (i, 0))
    return pl.pallas_call(
        body, name="dwin", grid=(s // t,),
        in_specs=[row(D_MODEL)] + [row(wd) for wd in DZ_WIDTHS],
        out_specs=pl.BlockSpec((D_MODEL, D_INP), lambda i: (0, 0)),
        out_shape=jax.ShapeDtypeStruct((D_MODEL, D_INP), F32),
        compiler_params=_cp(("arbitrary",), 56),
    )(h, *dz_parts)


def _inproj_bwd(dz_parts, w, x, g, dxn, l):
    s = x.shape[0]
    t = _tile(s)
    lay = _lay(l)
    n_parts = len(DZ_WIDTHS)

    def body(*refs):
        part_refs = refs[:n_parts]
        w_ref, x_ref, g_ref, dxn_ref, dx_ref, dg_ref = refs[n_parts:]

        @pl.when(pl.program_id(0) == 0)
        def _():
            dg_ref[...] = jnp.zeros_like(dg_ref)

        dh = _dot_nt(jnp.concatenate([r[...] for r in part_refs], axis=1), w_ref[...])
        xv = x_ref[...]
        rs = lax.rsqrt(jnp.mean(xv * xv, axis=-1, keepdims=True) + EPS)
        n = xv * rs
        dg_ref[...] += jnp.sum(dh * n, axis=0, keepdims=True)
        dn = dh * g_ref[...]
        dx_ref[...] = dxn_ref[...] + rs * (dn - n * jnp.mean(dn * n, axis=-1, keepdims=True))

    row = lambda w_: pl.BlockSpec((t, w_), lambda i: (i, 0))
    return pl.pallas_call(
        body, name="inproj_bwd", grid=(s // t,),
        in_specs=[row(wd) for wd in DZ_WIDTHS] + [lay((D_MODEL, D_INP)), row(D_MODEL), lay((1, D_MODEL)),
                                                  row(D_MODEL)],
        out_specs=[row(D_MODEL), pl.BlockSpec((1, D_MODEL), lambda i: (0, 0))],
        out_shape=[jax.ShapeDtypeStruct((s, D_MODEL), F32), jax.ShapeDtypeStruct((1, D_MODEL), F32)],
        compiler_params=_cp(("arbitrary",), 48),
    )(*dz_parts, w, x, g, dxn)


def _loss_head(x, g, tgt):
    s = x.shape[0]
    t = _tile(s)

    def body(x_ref, g_ref, t_ref, dx_ref, loss_ref, dg_ref):
        @pl.when(pl.program_id(0) == 0)
        def _():
            loss_ref[...] = jnp.zeros_like(loss_ref)
            dg_ref[...] = jnp.zeros_like(dg_ref)

        xv = x_ref[...]
        rs = lax.rsqrt(jnp.mean(xv * xv, axis=-1, keepdims=True) + EPS)
        n = xv * rs
        gv = g_ref[...]
        e = n * gv - t_ref[...]
        loss_ref[...] += 0.5 * jnp.sum(jnp.mean(e * e, axis=-1, keepdims=True))
        dyf = e * (1.0 / D_MODEL)
        dg_ref[...] += jnp.sum(dyf * n, axis=0, keepdims=True)
        dn = dyf * gv
        dx_ref[...] = rs * (dn - n * jnp.mean(dn * n, axis=-1, keepdims=True))

    row = pl.BlockSpec((t, D_MODEL), lambda i: (i, 0))
    vec = pl.BlockSpec((1, D_MODEL), lambda i: (0, 0))
    return pl.pallas_call(
        body, name="loss_head", grid=(s // t,),
        in_specs=[row, vec, row],
        out_specs=[row, pl.BlockSpec((1, LANES), lambda i: (0, 0)), vec],
        out_shape=[jax.ShapeDtypeStruct((s, D_MODEL), F32), jax.ShapeDtypeStruct((1, LANES), F32),
                   jax.ShapeDtypeStruct((1, D_MODEL), F32)],
        compiler_params=_cp(("arbitrary",), 32),
    )(x, g, tgt)


def _block_diag(w):
    n, h, d, _ = w.shape
    return jnp.einsum('lhij,hk->lhikj', w, jnp.eye(h, dtype=w.dtype)).reshape(n, h * d, h * d)


def _diag_blocks(wfull, h):
    d = wfull.shape[-1] // h
    return jnp.stack([wfull[:, i * d:(i + 1) * d, i * d:(i + 1) * d] for i in range(h)], axis=1)


REF_TO_PERM = np.concatenate([np.arange(0, 1152), np.arange(1536, 1792),
                              np.arange(2304 + KR_LANE0, 2304 + KR_LANE0 + QK_ROPE),
                              np.arange(1152, 1536), np.arange(1792, 2304)])
N_SHARD = 4
W_IN_SHARD = D_IN // N_SHARD


def _w_in_runs():
    ref_of_perm = -np.ones(D_INP, np.int64)
    ref_of_perm[REF_TO_PERM] = np.arange(D_IN)
    perm_runs, p = [], 0
    while p < D_INP:
        r, q = ref_of_perm[p], p + 1
        if r < 0:
            while q < D_INP and ref_of_perm[q] < 0:
                q += 1
            perm_runs.append((None, q - p, 0))
        else:
            while (q < D_INP and ref_of_perm[q] == ref_of_perm[q - 1] + 1
                   and ref_of_perm[q] // W_IN_SHARD == r // W_IN_SHARD):
                q += 1
            perm_runs.append((int(r // W_IN_SHARD), int(r % W_IN_SHARD), int(r % W_IN_SHARD + q - p)))
        p = q
    shard_runs = []
    for s in range(N_SHARD):
        cols = REF_TO_PERM[s * W_IN_SHARD:(s + 1) * W_IN_SHARD]
        runs, a = [], 0
        for b in range(1, W_IN_SHARD + 1):
            if b == W_IN_SHARD or cols[b] != cols[b - 1] + 1:
                runs.append((int(cols[a]), int(cols[b - 1]) + 1))
                a = b
        shard_runs.append(runs)
    return perm_runs, shard_runs


def _permute_w_in(shards):
    perm_runs, _ = _w_in_runs()
    lead = shards[0].shape[:-1]
    parts = [jnp.zeros(lead + (a,), shards[0].dtype) if s is None else shards[s][..., a:b] for s, a, b in perm_runs]
    return jnp.concatenate(parts, axis=-1)


def _w_in_shard(wp, s):
    _, shard_runs = _w_in_runs()
    return jnp.concatenate([wp[..., a:b] for a, b in shard_runs[s]], axis=-1)


def _pad_w_uq(w):
    w4 = w.reshape(w.shape[:2] + (MLA_HEADS, QK_NOPE + QK_ROPE))
    return jnp.pad(w4, ((0, 0),) * 3 + ((0, HEAD_BLK - QK_NOPE - QK_ROPE),)).reshape(w.shape[:2] + (-1,))


def _unpad_w_uq(w):
    return w.reshape(w.shape[:2] + (MLA_HEADS, HEAD_BLK))[..., :QK_NOPE + QK_ROPE].reshape(w.shape[:2] + (-1,))


def _pad_w_ukv(w):
    w4 = w.reshape(w.shape[:2] + (MLA_HEADS, QK_NOPE + V_DIM))
    kpart = jnp.pad(w4[..., :QK_NOPE], ((0, 0),) * 3 + ((0, HEAD_BLK - QK_NOPE),)).reshape(w.shape[:2] + (-1,))
    return jnp.concatenate([kpart, w4[..., QK_NOPE:].reshape(w.shape[:2] + (-1,))], axis=2)


def _unpad_w_ukv(w):
    hq = MLA_HEADS * HEAD_BLK
    kpart = w[..., :hq].reshape(w.shape[:2] + (MLA_HEADS, HEAD_BLK))[..., :QK_NOPE]
    vpart = w[..., hq:].reshape(w.shape[:2] + (MLA_HEADS, V_DIM))
    return jnp.concatenate([kpart, vpart], axis=3).reshape(w.shape[:2] + (-1,))


def _local_step(x, tgt, w):
    s = x.shape[0]
    tabs = _rope_tables(s)
    vec = lambda a: a[:, None, :]
    mxu = lambda a: a.astype(MXU_DTYPE)
    p = dict(g=vec(w['norm_g']), w_in=mxu(w['w_in']), cw=w['conv_w'], cb=vec(w['conv_b']),
             wr=mxu(_block_diag(w['w_rg'])), br=vec(w['b_rg']), wi=mxu(_block_diag(w['w_ig'])), bi=vec(w['b_ig']),
             lam=vec(w['lru_lambda']), gq=vec(w['q_norm_g']), gkv=vec(w['kv_norm_g']),
             wuq=mxu(_pad_w_uq(w['w_uq'])), wukv=mxu(_pad_w_ukv(w['w_ukv'])),
             wp=mxu(_block_diag(w['w_pool'])), ps=vec(w['pool_scale']), wout=mxu(w['w_out']))
    lru = lambda l: (p['cw'], p['cb'], p['wr'], p['br'], p['wi'], p['bi'], p['lam'], l)
    mla = lambda l: (p['gq'], p['gkv'], p['wuq'], p['wukv'], *tabs, l)

    saved = []
    for l in range(DEPTH):
        h, z = _inproj_fwd(x, p['g'], p['w_in'], l)
        ya, hs = _lru_fwd(z, *lru(l))
        yc = _pool_fwd(z, p['wp'], p['ps'], l)
        q, k, v, qn, kvn = _mla_prep_fwd(z, *mla(l))
        o, yb, lse = _flash_fwd(q, k, v, z)
        saved.append(dict(x=x, h=h, z=z, hs=hs, ya=ya, yb=yb, yc=yc, q=q, k=k, v=v, qn=qn, kvn=kvn, o=o, lse=lse))
        x = _outproj_fwd(x, ya, yb, yc, p['wout'], l)

    dx, loss, dgf = _loss_head(x, w['final_norm_g'][None], tgt)
    per_layer = {n: [None] * DEPTH for n in WEIGHT_NAMES if n != 'final_norm_g'}
    for l in reversed(range(DEPTH)):
        sv = saved[l]
        dy, per_layer['w_out'][l] = _outproj_bwd(dx, sv['ya'], sv['yb'], sv['yc'], p['wout'], l)
        dza, dga, *lru_grads = _lru_bwd(sv['z'], sv['hs'], dy, *lru(l))
        for n, g in zip(('w_rg', 'w_ig', 'conv_w', 'conv_b', 'b_rg', 'b_ig', 'lru_lambda'), lru_grads):
            per_layer[n][l] = g
        dzc, dgc, per_layer['w_pool'][l], per_layer['pool_scale'][l] = _pool_bwd(sv['z'], dy, p['wp'], p['ps'], l)
        do, dgb, delta = _attn_bwd_prep(dy, sv['o'], sv['z'])
        dq, dk, dv = _flash_bwd(sv['q'], sv['k'], sv['v'], do, sv['lse'], delta)
        dcq, dckv, dkr, dqu, dkvp, per_layer['q_norm_g'][l], per_layer['kv_norm_g'][l] = _mla_prep_bwd(
            dq, dk, dv, sv['z'], *mla(l))
        dz_parts = (dza, dga, dcq, dgb, dckv, dzc, dgc, dkr)
        per_layer['w_in'][l] = _dwin(sv['h'], dz_parts)
        per_layer['w_uq'][l] = _mm_tn(sv['qn'], dqu, "dwuq")
        per_layer['w_ukv'][l] = _mm_tn(sv['kvn'], dkvp, "dwukv")
        dx, per_layer['norm_g'][l] = _inproj_bwd(dz_parts, p['w_in'], sv['x'], p['g'], dx, l)
    grads = {n: jnp.stack(g) for n, g in per_layer.items()}
    for n in ('norm_g', 'conv_b', 'b_rg', 'b_ig', 'lru_lambda', 'q_norm_g', 'kv_norm_g', 'pool_scale'):
        grads[n] = grads[n][:, 0, :]
    grads['w_rg'] = _diag_blocks(grads['w_rg'], LRU_HEADS)
    grads['w_ig'] = _diag_blocks(grads['w_ig'], LRU_HEADS)
    grads['w_pool'] = _diag_blocks(grads['w_pool'], len(POOL_WINDOWS))
    grads['w_uq'] = _unpad_w_uq(grads['w_uq'])
    grads['w_ukv'] = _unpad_w_ukv(grads['w_ukv'])
    grads['final_norm_g'] = dgf[0]
    return loss[0, 0], dx, grads


WIRE_DTYPE = jnp.bfloat16
MESH_IDS = pl.DeviceIdType.MESH
_HBM = pl.BlockSpec(memory_space=pltpu.HBM)


def _coords():
    return lax.axis_index("x"), lax.axis_index("y"), lax.axis_index("c")


def _comm_call(body, name, arrays, out_shapes, copies_per_array):
    n = len(arrays)
    return pl.pallas_call(
        body, name=name, out_shape=out_shapes, in_specs=[_HBM] * n, out_specs=[_HBM] * n,
        scratch_shapes=[pltpu.SemaphoreType.DMA((n, copies_per_array)), pltpu.SemaphoreType.DMA((n, copies_per_array))],
    )(*arrays)


def _all_gather8(blocks, name):
    n = len(blocks)
    every = range(n)

    def body(*refs):
        x_refs, out_refs = refs[:n], refs[n:2 * n]
        send_sems, recv_sems = refs[2 * n:]
        x, y, c = _coords()
        me, sibling = (x, y, c), (x, y, 1 - c)
        chips = [(1 - x, y), (x, 1 - y), (1 - x, 1 - y)]

        def slot(t, px, py, pc):
            return out_refs[t].at[4 * px + 2 * py + pc]

        def copy(t, k, block, to, own=False):
            return pltpu.make_async_remote_copy(
                src_ref=x_refs[t] if own else slot(t, *block), dst_ref=slot(t, *block),
                send_sem=send_sems.at[t, k], recv_sem=recv_sems.at[t, k], device_id=to, device_id_type=MESH_IDS)

        first = [copy(t, 0, me, sibling, own=True) for t in every]
        first += [copy(t, 1 + j, me, (*chip, c), own=True) for j, chip in enumerate(chips) for t in every]
        for cp in first:
            cp.start()
        passed = [[copy(t, 4 + j, (*chip, c), sibling) for t in every] for j, chip in enumerate(chips)]
        for j, chip in enumerate(chips):
            for t in every:
                copy(t, 1 + j, (*chip, c), me).wait_recv()
                passed[j][t].start()
        for t in every:
            copy(t, 0, sibling, me).wait_recv()
        for j, chip in enumerate(chips):
            for t in every:
                copy(t, 4 + j, (*chip, 1 - c), me).wait_recv()
        for cp in first + [cp for group in passed for cp in group]:
            cp.wait_send()

    outs = [jax.ShapeDtypeStruct((N_DEV,) + b.shape, b.dtype) for b in blocks]
    got = _comm_call(body, name, blocks, outs, 7)
    me = 4 * lax.axis_index("x") + 2 * lax.axis_index("y") + lax.axis_index("c")
    return [lax.dynamic_update_index_in_dim(g, b, me, 0) for g, b in zip(got, blocks)]


def _sibling_send(arrs, name, which_half=False):
    n = len(arrs)

    def body(*refs):
        a_refs, out_refs = refs[:n], refs[n:2 * n]
        send_sems, recv_sems = refs[2 * n:]
        x, y, c = _coords()
        sent = [pltpu.make_async_remote_copy(
            src_ref=a_refs[t].at[1 - c] if which_half else a_refs[t], dst_ref=out_refs[t],
            send_sem=send_sems.at[t, 0], recv_sem=recv_sems.at[t, 0],
            device_id=(x, y, 1 - c), device_id_type=MESH_IDS) for t in range(n)]
        for cp in sent:
            cp.start()
        for cp in sent:
            cp.wait()

    shapes = [jax.ShapeDtypeStruct(a.shape[1:] if which_half else a.shape, a.dtype) for a in arrs]
    return _comm_call(body, name, arrs, shapes, 1)


def _chip_exchange(arrs, name):
    n = len(arrs)

    def body(*refs):
        a_refs, out_refs = refs[:n], refs[n:2 * n]
        send_sems, recv_sems = refs[2 * n:]
        x, y, c = _coords()
        copies = []
        for j, (cx, cy) in enumerate([(1 - x, y), (x, 1 - y), (1 - x, 1 - y)]):
            copies += [pltpu.make_async_remote_copy(
                src_ref=a_refs[t].at[2 * cx + cy], dst_ref=out_refs[t].at[j], send_sem=send_sems.at[t, j],
                recv_sem=recv_sems.at[t, j], device_id=(cx, cy, c), device_id_type=MESH_IDS) for t in range(n)]
        for cp in copies:
            cp.start()
        for cp in copies:
            cp.wait()

    return _comm_call(body, name, arrs, [jax.ShapeDtypeStruct((3,) + a.shape[1:], a.dtype) for a in arrs], 3)


def _sum_leading(groups, out_dtype, steps, name):
    flat = [a for g in groups for a in g]

    def body(*refs):
        ins, outs, pos = refs[:len(flat)], refs[len(flat):], 0
        for g, o_ref in zip(groups, outs):
            acc = None
            for i_ref in ins[pos:pos + len(g)]:
                for k in range(i_ref.shape[0]):
                    term = i_ref[k].astype(F32)
                    acc = term if acc is None else acc + term
            pos += len(g)
            o_ref[...] = acc.astype(o_ref.dtype)

    return pl.pallas_call(
        body, name=name, grid=(steps,),
        in_specs=[pl.BlockSpec((a.shape[0], a.shape[1] // steps, a.shape[2]), lambda i: (0, i, 0)) for a in flat],
        out_specs=[pl.BlockSpec((g[0].shape[1] // steps, g[0].shape[2]), lambda i: (i, 0)) for g in groups],
        out_shape=[jax.ShapeDtypeStruct(g[0].shape[1:], out_dtype) for g in groups],
        compiler_params=_cp(("parallel",), 40),
    )(*flat)


def _adamw_update(w_ref, g_ref, m_ref, v_ref, d_ref, mo_ref, vo_ref):
    gv = g_ref[...]
    mn = ADAM_B1 * m_ref[...] + (1.0 - ADAM_B1) * gv
    vn = ADAM_B2 * v_ref[...] + (1.0 - ADAM_B2) * (gv * gv)
    mo_ref[...] = mn
    vo_ref[...] = vn
    m_hat = mn / (1.0 - ADAM_B1 ** ADAM_STEP)
    v_hat = vn / (1.0 - ADAM_B2 ** ADAM_STEP)
    d_ref[...] = (-ADAM_LR) * (m_hat / (jnp.sqrt(v_hat) + ADAM_EPS) + ADAM_WD * w_ref[...])


def _adamw(w, g, m, v, name):
    r, cdim = w.shape
    tr = math.gcd(r, 512)

    def body(*refs):
        _adamw_update(*refs)

    blk = pl.BlockSpec((tr, cdim), lambda i: (i, 0))
    return pl.pallas_call(
        body, name=name, grid=(r // tr,),
        in_specs=[blk] * 4, out_specs=[blk] * 3,
        out_shape=[jax.ShapeDtypeStruct((r, cdim), F32)] * 3,
        compiler_params=_cp(("parallel",), 40),
    )(w, g, m, v)


def _adamw_small(ws, gs, ms, vs, name):
    n = len(ws)

    def body(*refs):
        ins, outs = refs[:4 * n], refs[4 * n:]
        for t in range(n):
            _adamw_update(ins[t], ins[n + t], ins[2 * n + t], ins[3 * n + t], outs[t], outs[n + t], outs[2 * n + t])

    shapes = [jax.ShapeDtypeStruct(w.shape, F32) for w in ws]
    outs = pl.pallas_call(body, name=name, out_shape=shapes * 3)(*ws, *gs, *ms, *vs)
    return outs[:n], outs[n:2 * n], outs[2 * n:]


HALF = DEPTH // 2
BIG = ['w_in', 'w_uq', 'w_ukv', 'w_out']
SHARD_AXIS = {'w_in': 2, 'conv_w': 2, 'w_uq': 2, 'w_ukv': 2, 'w_out': 1}
FULL_SHAPE = {'w_in': (DEPTH, D_MODEL, D_IN), 'conv_w': (DEPTH, CONV_WIDTH, D_LRU),
              'w_uq': (DEPTH, Q_RANK, MLA_HEADS * (QK_NOPE + QK_ROPE)),
              'w_ukv': (DEPTH, KV_RANK, MLA_HEADS * (QK_NOPE + V_DIM)), 'w_out': (DEPTH, D_MIX, D_MODEL)}


def _shard_shape(n):
    shp = list(FULL_SHAPE[n])
    shp[SHARD_AXIS[n]] //= N_SHARD
    return tuple(shp)


def _rows_view(a, lead=0):
    return a.reshape(a.shape[:lead] + (-1, a.shape[-1]))


def _gather_weights(local):
    c = lax.axis_index("c")
    names = BIG + ['conv_w']
    halves = [lax.dynamic_slice_in_dim(local[n], HALF * c, HALF, axis=0) for n in names]
    halves = [h.astype(WIRE_DTYPE) if n in BIG else h for n, h in zip(names, halves)]
    got = _all_gather8(halves, "gather_weights")
    full = {}
    for n, g in zip(names, got):
        g = g.reshape((N_SHARD, DEPTH) + g.shape[2:])
        if n == 'w_in':
            full[n] = _permute_w_in([g[s] for s in range(N_SHARD)])
        else:
            full[n] = jnp.moveaxis(g, 0, SHARD_AXIS[n]).reshape(FULL_SHAPE[n])
    return full


def _shard_blocks(g, n):
    width = _shard_shape(n)[SHARD_AXIS[n]]

    def block(h, s):
        part = g[HALF * h:HALF * (h + 1)]
        if n == 'w_in':
            part = _w_in_shard(part, s)
        else:
            part = lax.slice_in_dim(part, s * width, (s + 1) * width, axis=SHARD_AXIS[n])
        return _rows_view(part)

    return jnp.stack([jnp.stack([block(h, s) for s in range(N_SHARD)]) for h in range(2)]).astype(WIRE_DTYPE)


SUM_STEPS = 8


def _reduce_big(grads):
    c = lax.axis_index("c")
    shard = 2 * lax.axis_index("x") + lax.axis_index("y")
    contrib = [_shard_blocks(grads[n], n) for n in BIG]
    from_sibling = _sibling_send(contrib, "pair_exchange_big", which_half=True)
    own_half = [lax.dynamic_index_in_dim(a, c, 0, keepdims=False) for a in contrib]
    pair_sum = _sum_leading([[_rows_view(o)[None], _rows_view(r)[None]] for o, r in zip(own_half, from_sibling)],
                            WIRE_DTYPE, SUM_STEPS, "pair_sum_big")
    to_chips = [p.reshape(a.shape[1:]) for p, a in zip(pair_sum, contrib)]
    from_chips = _chip_exchange(to_chips, "chip_exchange_big")
    own_block = [lax.dynamic_index_in_dim(a, shard, 0, keepdims=True) for a in to_chips]
    mine = _sum_leading([[o, r] for o, r in zip(own_block, from_chips)], F32, SUM_STEPS, "chip_sum_big")
    theirs = _sibling_send(mine, "sibling_big")
    both = [jnp.where(c == 0, jnp.stack([m, t]), jnp.stack([t, m])) for m, t in zip(mine, theirs)]
    return {n: b.reshape(_shard_shape(n)) for n, b in zip(BIG, both)}


SMALL = REPLICATED + ['conv_w']


def _reduce_small(grads):
    views = [_rows_view(jnp.atleast_2d(grads[n])) for n in SMALL]
    sums = _sum_leading([[g] for g in _all_gather8(views, "gather_small")], F32, 1, "sum_small")
    return {n: s.reshape(grads[n].shape) for n, s in zip(SMALL, sums)}


def kernel(x, norm_g, w_in, conv_w, conv_b, w_rg, b_rg, w_ig, b_ig, lru_lambda, q_norm_g, w_uq, kv_norm_g, w_ukv, w_pool, pool_scale, w_out, final_norm_g, loss_target, m_norm_g, m_w_in, m_conv_w, m_conv_b, m_w_rg, m_b_rg, m_w_ig, m_b_ig, m_lru_lambda, m_q_norm_g, m_w_uq, m_kv_norm_g, m_w_ukv, m_w_pool, m_pool_scale, m_w_out, m_final_norm_g, v_norm_g, v_w_in, v_conv_w, v_conv_b, v_w_rg, v_b_rg, v_w_ig, v_b_ig, v_lru_lambda, v_q_norm_g, v_w_uq, v_kv_norm_g, v_w_ukv, v_w_pool, v_pool_scale, v_w_out, v_final_norm_g):
    w_loc = dict(zip(WEIGHT_NAMES, (norm_g, w_in, conv_w, conv_b, w_rg, b_rg, w_ig, b_ig, lru_lambda, q_norm_g, w_uq,
                                    kv_norm_g, w_ukv, w_pool, pool_scale, w_out, final_norm_g)))
    m_loc = dict(zip(WEIGHT_NAMES, (m_norm_g, m_w_in, m_conv_w, m_conv_b, m_w_rg, m_b_rg, m_w_ig, m_b_ig, m_lru_lambda,
                                    m_q_norm_g, m_w_uq, m_kv_norm_g, m_w_ukv, m_w_pool, m_pool_scale, m_w_out,
                                    m_final_norm_g)))
    v_loc = dict(zip(WEIGHT_NAMES, (v_norm_g, v_w_in, v_conv_w, v_conv_b, v_w_rg, v_b_rg, v_w_ig, v_b_ig, v_lru_lambda,
                                    v_q_norm_g, v_w_uq, v_kv_norm_g, v_w_ukv, v_w_pool, v_pool_scale, v_w_out,
                                    v_final_norm_g)))
    w_full = dict(w_loc)
    w_full.update(_gather_weights(w_loc))
    loss_local, dx, g_local = _local_step(x[0], loss_target[0], w_full)
    loss = lax.psum(loss_local, ("x", "y", "c"))

    grads = _reduce_big(g_local)
    g_small = _reduce_small(g_local)
    shard = 2 * lax.axis_index("x") + lax.axis_index("y")
    width = D_LRU // N_SHARD
    grads['conv_w'] = lax.dynamic_slice_in_dim(g_small['conv_w'], shard * width, width, axis=2)
    for n in REPLICATED:
        grads[n] = g_small[n]

    delta, new_m, new_v = {}, {}, {}
    for n in BIG:
        d, mo, vo = _adamw(*[_rows_view(t[n]) for t in (w_loc, grads, m_loc, v_loc)], "adamw_" + n)
        delta[n], new_m[n], new_v[n] = (a.reshape(w_loc[n].shape) for a in (d, mo, vo))
    small = [[_rows_view(jnp.atleast_2d(t[n])) for n in SMALL] for t in (w_loc, grads, m_loc, v_loc)]
    for tree, outs in zip((delta, new_m, new_v), _adamw_small(*small, "adamw_small")):
        tree.update({n: a.reshape(w_loc[n].shape) for n, a in zip(SMALL, outs)})

    return (loss, dx[None], *[grads[n] for n in WEIGHT_NAMES], *[delta[n] for n in WEIGHT_NAMES],
            *[new_m[n] for n in WEIGHT_NAMES], *[new_v[n] for n in WEIGHT_NAMES])
```

```python
import functools
import math

import jax
import jax.numpy as jnp
import numpy as np
from jax import lax
from jax.experimental import pallas as pl
from jax.experimental.pallas import tpu as pltpu

F32 = jnp.float32
MXU_DTYPE = jnp.bfloat16

D_MODEL = 1024
DEPTH = 4
EPS = 1e-6
D_LRU = 384
LRU_HEADS = 6
CONV_WIDTH = 4
LRU_C = 8.0
MLA_HEADS = 6
QK_NOPE = 64
QK_ROPE = 32
V_DIM = 64
D_MLA = MLA_HEADS * V_DIM
Q_RANK = 384
KV_RANK = 256
ROPE_BASE = 10000.0
POOL_WINDOWS = (2, 4, 8, 16)
D_POOL = 256
D_MIX = D_LRU + D_MLA + D_POOL
D_IN = 2336
ATT_SCALE = (QK_NOPE + QK_ROPE) ** -0.5
LOG2_E = 1.4426950408889634
LN_2 = 0.6931471805599453
Q_PRESCALE = ATT_SCALE * LOG2_E

ADAM_LR = 0.001
ADAM_B1 = 0.9
ADAM_B2 = 0.999
ADAM_EPS = 1e-08
ADAM_WD = 0.01
ADAM_STEP = 10

LANES = 128
SUBLANES = 8
V7X_VMEM_BYTES = 64 << 20
N_DEV = 8

D_INP = 2432
KR_LANE0 = 64
HEAD_BLK = 128
N_PAIR = MLA_HEADS // 2

WEIGHT_NAMES = ['norm_g', 'w_in', 'conv_w', 'conv_b', 'w_rg', 'b_rg', 'w_ig', 'b_ig', 'lru_lambda', 'q_norm_g',
                'w_uq', 'kv_norm_g', 'w_ukv', 'w_pool', 'pool_scale', 'w_out', 'final_norm_g']
SHARDED = ['w_in', 'conv_w', 'w_uq', 'w_ukv', 'w_out']
REPLICATED = [n for n in WEIGHT_NAMES if n not in SHARDED]


def _cp(sem, vmem_mb=None):
    return pltpu.CompilerParams(dimension_semantics=sem,
                                vmem_limit_bytes=None if vmem_mb is None else vmem_mb << 20)


def _dot(a, b):
    return jnp.dot(a.astype(MXU_DTYPE), b.astype(MXU_DTYPE), preferred_element_type=F32)


def _dot_nt(a, b):
    return lax.dot_general(a.astype(MXU_DTYPE), b.astype(MXU_DTYPE), (((1,), (1,)), ((), ())),
                           preferred_element_type=F32)


def _dot_tn(a, b):
    return lax.dot_general(a.astype(MXU_DTYPE), b.astype(MXU_DTYPE), (((0,), (0,)), ((), ())),
                           preferred_element_type=F32)


def _sig(x):
    return 0.5 * jnp.tanh(0.5 * x) + 0.5


def _down(x, k):
    return pltpu.roll(x, k, 0)


def _up(x, k):
    return pltpu.roll(x, x.shape[0] - k, 0)


def _rows(shape):
    return lax.broadcasted_iota(jnp.int32, shape, 0)


def _lanes(shape):
    return lax.broadcasted_iota(jnp.int32, shape, 1)


def _tile(s):
    return min(512, s)


def _lay(l):
    return lambda shp, blk=0: pl.BlockSpec((None,) + shp, lambda *_: (l, blk, 0))


def _inproj_fwd(x, g, w, l):
    s = x.shape[0]
    t = _tile(s)
    lay = _lay(l)

    def body(x_ref, g_ref, w_ref, h_ref, z_ref):
        xv = x_ref[...]
        rs = lax.rsqrt(jnp.mean(xv * xv, axis=-1, keepdims=True) + EPS)
        h = (xv * rs * g_ref[...]).astype(MXU_DTYPE)
        h_ref[...] = h
        z_ref[...] = jnp.dot(h, w_ref[...], preferred_element_type=F32)

    return pl.pallas_call(
        body, name="inproj_fwd", grid=(s // t,),
        in_specs=[pl.BlockSpec((t, D_MODEL), lambda i: (i, 0)),
                  lay((1, D_MODEL)), lay((D_MODEL, D_INP))],
        out_specs=[pl.BlockSpec((t, D_MODEL), lambda i: (i, 0)),
                   pl.BlockSpec((t, D_INP), lambda i: (i, 0))],
        out_shape=[jax.ShapeDtypeStruct((s, D_MODEL), MXU_DTYPE), jax.ShapeDtypeStruct((s, D_INP), F32)],
        compiler_params=_cp(("parallel",), 40),
    )(x, g, w)


def _lru_gates(za, halo, cw_ref, cb_ref, wr_ref, br_ref, wi_ref, bi_ref, lam_ref):
    t = za.shape[0]
    ext = jnp.concatenate([halo, za], axis=0)
    sh = [za] + [_down(ext, j)[SUBLANES:SUBLANES + t] for j in (1, 2, 3)]
    xa = cb_ref[...] + cw_ref[3:4, :] * sh[0] + cw_ref[2:3, :] * sh[1] + cw_ref[1:2, :] * sh[2] + cw_ref[0:1, :] * sh[3]
    r = 1.0 / (1.0 + jnp.exp(-(_dot(xa, wr_ref[...]) + br_ref[...])))
    ig = _sig(_dot(xa, wi_ref[...]) + bi_ref[...])
    lam = lam_ref[...]
    sp = jnp.maximum(-lam, 0.0) + jnp.log(1.0 + jnp.exp(-jnp.abs(lam)))
    la = (-LRU_C) * r * sp
    a = jnp.exp(la)
    y2 = 2.0 * la
    m2 = jnp.where(y2 > -0.01, -(y2 * (1.0 + y2 * (0.5 + y2 * (1.0 / 6.0)))), 1.0 - a * a)
    return xa, sh, r, ig, sp, a, jnp.sqrt(m2), m2


def _lru_fwd(z, cw, cb, wr, br, wi, bi, lam, l):
    s = z.shape[0]
    t = _tile(s)
    c = D_LRU
    lay = _lay(l)

    def body(za_ref, ga_ref, cw_ref, cb_ref, wr_ref, br_ref, wi_ref, bi_ref, lam_ref, ya_ref, hs_ref, zprev, hcar):
        i = pl.program_id(0)

        @pl.when(i == 0)
        def _():
            zprev[...] = jnp.zeros_like(zprev)
            hcar[...] = jnp.zeros_like(hcar)

        za = za_ref[...]
        xa, _, _, ig, _, a, m, _ = _lru_gates(za, zprev[...], cw_ref, cb_ref, wr_ref, br_ref, wi_ref, bi_ref, lam_ref)
        row = _rows((t, c))
        acc_h = m * (ig * xa) + jnp.where(row == 0, a * hcar[...], 0.0)
        acc_a = jnp.where(row == 0, 0.0, a)
        k = 1
        while k < t:
            acc_h = acc_h + acc_a * _down(acc_h, k)
            acc_a = acc_a * _down(acc_a, k)
            k *= 2
        hs = acc_h
        hs_ref[...] = hs
        ga = ga_ref[...]
        ya_ref[...] = (hs * (ga * _sig(ga))).astype(ya_ref.dtype)
        hcar[...] = jnp.sum(jnp.where(row == t - 1, hs, 0.0), axis=0, keepdims=True)
        zprev[...] = za_ref[t - SUBLANES:t, :]

    return pl.pallas_call(
        body, name="lru_fwd", grid=(s // t,),
        in_specs=[pl.BlockSpec((t, c), lambda i: (i, 0)), pl.BlockSpec((t, c), lambda i: (i, 1)),
                  lay((CONV_WIDTH, c)), lay((1, c)), lay((c, c)), lay((1, c)), lay((c, c)), lay((1, c)),
                  lay((1, c))],
        out_specs=[pl.BlockSpec((t, c), lambda i: (i, 0)), pl.BlockSpec((t, c), lambda i: (i, 0))],
        out_shape=[jax.ShapeDtypeStruct((s, c), MXU_DTYPE), jax.ShapeDtypeStruct((s, c), F32)],
        scratch_shapes=[pltpu.VMEM((SUBLANES, c), F32), pltpu.VMEM((1, c), F32)],
        compiler_params=_cp(("arbitrary",), 40),
    )(z, z, cw, cb, wr, br, wi, bi, lam)


def _lru_bwd(z, hs, dy, cw, cb, wr, br, wi, bi, lam, l):
    s = z.shape[0]
    t = _tile(s)
    lay = _lay(l)
    nt = s // t
    c = D_LRU
    hb = t // SUBLANES

    def body(za_ref, zh_ref, ga_ref, hs_ref, hh_ref, dy_ref, cw_ref, cb_ref, wr_ref, br_ref, wi_ref, bi_ref, lam_ref,
             dza_ref, dga_ref, dwr_ref, dwi_ref, dcw_ref, dcb_ref, dbr_ref, dbi_ref, dlam_ref, lcar, dxn):
        i = pl.program_id(0)
        tt = nt - 1 - i

        @pl.when(i == 0)
        def _():
            lcar[...] = jnp.zeros_like(lcar)
            dxn[...] = jnp.zeros_like(dxn)
            for ref in (dwr_ref, dwi_ref, dcw_ref, dcb_ref, dbr_ref, dbi_ref, dlam_ref):
                ref[...] = jnp.zeros_like(ref)

        first = (tt > 0).astype(F32)
        za = za_ref[...]
        xa, sh, r, ig, sp, a, m, m2 = _lru_gates(za, zh_ref[...] * first, cw_ref, cb_ref, wr_ref, br_ref, wi_ref,
                                                 bi_ref, lam_ref)
        hs_v = hs_ref[...]
        hprev = _down(jnp.concatenate([hh_ref[...] * first, hs_v], axis=0), 1)[SUBLANES:SUBLANES + t]
        ga = ga_ref[...]
        sg = _sig(ga)
        silu = ga * sg
        dya = dy_ref[...]
        dga_ref[...] = (dya * hs_v * (sg * (1.0 + ga * (1.0 - sg)))).astype(dga_ref.dtype)
        row = _rows((t, c))
        acc_h = dya * silu + jnp.where(row == t - 1, lcar[...], 0.0)
        acc_b = jnp.where(row < t - 1, _up(a, 1), 0.0)
        k = 1
        while k < t:
            acc_h = acc_h + acc_b * _up(acc_h, k)
            acc_b = acc_b * _up(acc_b, k)
            k *= 2
        lmb = acc_h
        lcar[...] = jnp.sum(jnp.where(row == 0, a * lmb, 0.0), axis=0, keepdims=True)
        da = lmb * hprev
        dxa = lmb * m * ig
        di = lmb * m * xa
        dm = lmb * ig * xa
        dla = da * a - dm * (a * a) * lax.rsqrt(m2)
        dr = dla * ((-LRU_C) * sp)
        lam = lam_ref[...]
        dsp = jnp.sum(dla * ((-LRU_C) * r), axis=0, keepdims=True)
        dlam_ref[...] += dsp * (-1.0 / (1.0 + jnp.exp(lam)))
        dpr = dr * r * (1.0 - r)
        dpi = di * ig * (1.0 - ig)
        dbr_ref[...] += jnp.sum(dpr, axis=0, keepdims=True)
        dbi_ref[...] += jnp.sum(dpi, axis=0, keepdims=True)
        dwr_ref[...] += _dot_tn(xa, dpr)
        dwi_ref[...] += _dot_tn(xa, dpi)
        dxa = dxa + _dot_nt(dpr, wr_ref[...]) + _dot_nt(dpi, wi_ref[...])
        dcb_ref[...] += jnp.sum(dxa, axis=0, keepdims=True)
        for k in range(CONV_WIDTH):
            dcw_ref[k:k + 1, :] += jnp.sum(dxa * sh[CONV_WIDTH - 1 - k], axis=0, keepdims=True)
        ext = jnp.concatenate([dxa, dxn[...]], axis=0)
        dza = cw_ref[3:4, :] * dxa
        for j in (1, 2, 3):
            dza = dza + cw_ref[3 - j:4 - j, :] * _up(ext, j)[:t]
        dza_ref[...] = dza.astype(dza_ref.dtype)
        dxn[...] = dxa[:SUBLANES]

    full = lambda shp: pl.BlockSpec(shp, lambda i: (0, 0))
    rev = lambda i: nt - 1 - i
    halo = lambda i: (jnp.maximum((nt - 1 - i) * hb - 1, 0), 0)
    outs = pl.pallas_call(
        body, name="lru_bwd", grid=(nt,),
        in_specs=[pl.BlockSpec((t, c), lambda i: (rev(i), 0)), pl.BlockSpec((SUBLANES, c), halo),
                  pl.BlockSpec((t, c), lambda i: (rev(i), 1)),
                  pl.BlockSpec((t, c), lambda i: (rev(i), 0)), pl.BlockSpec((SUBLANES, c), halo),
                  pl.BlockSpec((t, c), lambda i: (rev(i), 0)),
                  lay((CONV_WIDTH, c)), lay((1, c)), lay((c, c)), lay((1, c)), lay((c, c)), lay((1, c)),
                  lay((1, c))],
        out_specs=[pl.BlockSpec((t, c), lambda i: (rev(i), 0)), pl.BlockSpec((t, c), lambda i: (rev(i), 0)),
                   full((c, c)), full((c, c)), full((CONV_WIDTH, c)), full((1, c)), full((1, c)), full((1, c)),
                   full((1, c))],
        out_shape=[jax.ShapeDtypeStruct((s, c), MXU_DTYPE), jax.ShapeDtypeStruct((s, c), MXU_DTYPE),
                   jax.ShapeDtypeStruct((c, c), F32), jax.ShapeDtypeStruct((c, c), F32),
                   jax.ShapeDtypeStruct((CONV_WIDTH, c), F32)] + [jax.ShapeDtypeStruct((1, c), F32)] * 4,
        scratch_shapes=[pltpu.VMEM((1, c), F32), pltpu.VMEM((SUBLANES, c), F32)],
        compiler_params=_cp(("arbitrary",), 48),
    )(z, z, z, hs, hs, dy, cw, cb, wr, br, wi, bi, lam)
    return outs


POOL_HALO = 16


def _pool_select(lane, v2, v4, v8, v16):
    return jnp.where(lane < 64, v2, jnp.where(lane < 128, v4, jnp.where(lane < 192, v8, v16)))


def _pool_counts(t0, t, c):
    lane = _lanes((t, c))
    win = _pool_select(lane, 2.0, 4.0, 8.0, 16.0)
    seen = (t0 + _rows((t, c)) + 1).astype(F32)
    return lane, jnp.minimum(seen, win)


def _pooled(zc, halo, lane, cnt):
    t = zc.shape[0]
    ext = jnp.concatenate([halo, zc], axis=0)
    s2 = ext + _down(ext, 1)
    s4 = s2 + _down(s2, 2)
    s8 = s4 + _down(s4, 4)
    s16 = s8 + _down(s8, 8)
    cut = lambda v: v[POOL_HALO:POOL_HALO + t]
    return _pool_select(lane, cut(s2), cut(s4), cut(s8), cut(s16)) / cnt - zc


def _pool_fwd(z, wp, ps, l):
    s = z.shape[0]
    t = _tile(s)
    c = D_POOL
    lay = _lay(l)

    def body(zc_ref, gc_ref, wp_ref, ps_ref, yc_ref, zprev):
        i = pl.program_id(0)

        @pl.when(i == 0)
        def _():
            zprev[...] = jnp.zeros_like(zprev)

        zc = zc_ref[...]
        lane, cnt = _pool_counts(i * t, t, c)
        pooled = _pooled(zc, zprev[...], lane, cnt)
        pc = _dot(pooled, wp_ref[...])
        gc = gc_ref[...]
        yc_ref[...] = (pc * ps_ref[...] * (gc * _sig(gc))).astype(yc_ref.dtype)
        zprev[...] = zc_ref[t - POOL_HALO:t, :]

    return pl.pallas_call(
        body, name="pool_fwd", grid=(s // t,),
        in_specs=[pl.BlockSpec((t, c), lambda i: (i, 7)), pl.BlockSpec((t, c), lambda i: (i, 8)),
                  lay((c, c)), lay((1, c))],
        out_specs=pl.BlockSpec((t, c), lambda i: (i, 0)),
        out_shape=jax.ShapeDtypeStruct((s, c), MXU_DTYPE),
        scratch_shapes=[pltpu.VMEM((POOL_HALO, c), F32)],
        compiler_params=_cp(("arbitrary",), 32),
    )(z, z, wp, ps)


def _pool_bwd(z, dy, wp, ps, l):
    s = z.shape[0]
    t = _tile(s)
    lay = _lay(l)
    nt = s // t
    c = D_POOL
    hb = t // POOL_HALO

    def body(zc_ref, zh_ref, gc_ref, dy_ref, wp_ref, ps_ref, dzc_ref, dgc_ref, dwp_ref, dps_ref, ddn):
        i = pl.program_id(0)
        tt = nt - 1 - i

        @pl.when(i == 0)
        def _():
            ddn[...] = jnp.zeros_like(ddn)
            dwp_ref[...] = jnp.zeros_like(dwp_ref)
            dps_ref[...] = jnp.zeros_like(dps_ref)

        first = (tt > 0).astype(F32)
        zc = zc_ref[...]
        lane, cnt = _pool_counts(tt * t, t, c)
        pooled = _pooled(zc, zh_ref[...] * first, lane, cnt)
        pc = _dot(pooled, wp_ref[...])
        gc = gc_ref[...]
        sg = _sig(gc)
        silu = gc * sg
        dyc = dy_ref[...]
        ps_v = ps_ref[...]
        dgc_ref[...] = (dyc * pc * ps_v * (sg * (1.0 + gc * (1.0 - sg)))).astype(dgc_ref.dtype)
        dps_ref[...] += jnp.sum(dyc * pc * silu, axis=0, keepdims=True)
        dpc = dyc * ps_v * silu
        dwp_ref[...] += _dot_tn(pooled, dpc)
        dpooled = _dot_nt(dpc, wp_ref[...])
        dd = dpooled / cnt
        ext = jnp.concatenate([dd, ddn[...]], axis=0)
        f2 = ext + _up(ext, 1)
        f4 = f2 + _up(f2, 2)
        f8 = f4 + _up(f4, 4)
        f16 = f8 + _up(f8, 8)
        dzc = _pool_select(lane, f2[:t], f4[:t], f8[:t], f16[:t]) - dpooled
        dzc_ref[...] = dzc.astype(dzc_ref.dtype)
        ddn[...] = dd[:POOL_HALO]

    full = lambda shp: pl.BlockSpec(shp, lambda i: (0, 0))
    rev = lambda i: nt - 1 - i
    return pl.pallas_call(
        body, name="pool_bwd", grid=(nt,),
        in_specs=[pl.BlockSpec((t, c), lambda i: (rev(i), 7)),
                  pl.BlockSpec((POOL_HALO, c), lambda i: (jnp.maximum(rev(i) * hb - 1, 0), 7)),
                  pl.BlockSpec((t, c), lambda i: (rev(i), 8)),
                  pl.BlockSpec((t, c), lambda i: (rev(i), 3)),
                  lay((c, c)), lay((1, c))],
        out_specs=[pl.BlockSpec((t, c), lambda i: (rev(i), 0)), pl.BlockSpec((t, c), lambda i: (rev(i), 0)),
                   full((c, c)), full((1, c))],
        out_shape=[jax.ShapeDtypeStruct((s, c), MXU_DTYPE), jax.ShapeDtypeStruct((s, c), MXU_DTYPE),
                   jax.ShapeDtypeStruct((c, c), F32), jax.ShapeDtypeStruct((1, c), F32)],
        scratch_shapes=[pltpu.VMEM((POOL_HALO, c), F32)],
        compiler_params=_cp(("arbitrary",), 32),
    )(z, z, z, dy, wp, ps)


def _rope_tables(s):
    pos = jnp.arange(s, dtype=F32)
    inv_freq = ROPE_BASE ** (-jnp.arange(0, QK_ROPE, 2, dtype=F32) / QK_ROPE)
    ang = pos[:, None] * inv_freq[None, :]
    cos, sin = jnp.cos(ang), jnp.sin(ang)
    half = QK_ROPE // 2
    ones = jnp.ones((s, QK_NOPE), F32)
    z64 = jnp.zeros((s, QK_NOPE), F32)
    zh = jnp.zeros((s, half), F32)
    z32 = jnp.zeros((s, HEAD_BLK - QK_NOPE - QK_ROPE), F32)
    c_t = jnp.concatenate([ones, cos, cos, z32], axis=1)
    s1_t = jnp.concatenate([z64, -sin, zh, z32], axis=1)
    s2_t = jnp.concatenate([z64, zh, sin, z32], axis=1)
    return c_t, s1_t, s2_t


def _rope(x, c_t, s1_t, s2_t):
    return x * c_t + pltpu.roll(x, HEAD_BLK - 16, 1) * s1_t + pltpu.roll(x, 16, 1) * s2_t


def _unrope(d, c_t, s1_t, s2_t):
    return d * c_t + pltpu.roll(d * s1_t, 16, 1) + pltpu.roll(d * s2_t, HEAD_BLK - 16, 1)


def _mla_prep_fwd(z, gq, gkv, wuq, wukv, c_t, s1_t, s2_t, l):
    s = z.shape[0]
    t = _tile(s)
    hq = MLA_HEADS * HEAD_BLK
    lay = _lay(l)

    def body(cq_ref, ckv_ref, kr_ref, gq_ref, gkv_ref, wuq_ref, wukv_ref, c_ref, s1_ref, s2_ref,
             q_ref, k_ref, v_ref, qn_ref, kvn_ref):
        ct, s1, s2 = c_ref[...], s1_ref[...], s2_ref[...]
        cq = cq_ref[...]
        qn = (cq * lax.rsqrt(jnp.mean(cq * cq, axis=-1, keepdims=True) + EPS) * gq_ref[...]).astype(MXU_DTYPE)
        qn_ref[...] = qn
        q = jnp.dot(qn, wuq_ref[...], preferred_element_type=F32)
        ckv = ckv_ref[...]
        kvn = (ckv * lax.rsqrt(jnp.mean(ckv * ckv, axis=-1, keepdims=True) + EPS) * gkv_ref[...]).astype(MXU_DTYPE)
        kvn_ref[...] = kvn
        kvp = jnp.dot(kvn, wukv_ref[...], preferred_element_type=F32)
        krr = _rope(kr_ref[...], ct, s1, s2)
        for h in range(MLA_HEADS):
            blk = slice(h * HEAD_BLK, (h + 1) * HEAD_BLK)
            q_ref[:, blk] = (_rope(q[:, blk], ct, s1, s2) * Q_PRESCALE).astype(q_ref.dtype)
            k_ref[:, blk] = (kvp[:, blk] + krr).astype(k_ref.dtype)
        v_ref[...] = kvp[:, hq:].astype(v_ref.dtype)

    tab = pl.BlockSpec((t, HEAD_BLK), lambda i: (i, 0))
    return pl.pallas_call(
        body, name="mla_prep_fwd", grid=(s // t,),
        in_specs=[pl.BlockSpec((t, Q_RANK), lambda i: (i, 2)), pl.BlockSpec((t, KV_RANK), lambda i: (i, 6)),
                  pl.BlockSpec((t, HEAD_BLK), lambda i: (i, 18)),
                  lay((1, Q_RANK)), lay((1, KV_RANK)), lay((Q_RANK, hq)), lay((KV_RANK, hq + D_MLA)),
                  tab, tab, tab],
        out_specs=[pl.BlockSpec((t, hq), lambda i: (i, 0)), pl.BlockSpec((t, hq), lambda i: (i, 0)),
                   pl.BlockSpec((t, D_MLA), lambda i: (i, 0)), pl.BlockSpec((t, Q_RANK), lambda i: (i, 0)),
                   pl.BlockSpec((t, KV_RANK), lambda i: (i, 0))],
        out_shape=[jax.ShapeDtypeStruct((s, hq), MXU_DTYPE), jax.ShapeDtypeStruct((s, hq), MXU_DTYPE),
                   jax.ShapeDtypeStruct((s, D_MLA), MXU_DTYPE), jax.ShapeDtypeStruct((s, Q_RANK), MXU_DTYPE),
                   jax.ShapeDtypeStruct((s, KV_RANK), MXU_DTYPE)],
        compiler_params=_cp(("parallel",), 40),
    )(z, z, z, gq, gkv, wuq, wukv, c_t, s1_t, s2_t)


SUM_LANE_A = V_DIM
SUM_LANE_B = 0
FWD_TILES_PER_TRIP = 4


def _flash_fwd(q, k, v, z):
    s = q.shape[0]
    t = _tile(s)
    nq = s // t
    pw = 2 * HEAD_BLK

    def body(q_ref, k_ref, v_ref, gb_ref, o_ref, yb_ref, lse_ref):
        i = pl.program_id(1)
        qv = q_ref[...]
        qa, qb = qv[:, :HEAD_BLK], qv[:, HEAD_BLK:]
        lane = _lanes((t, HEAD_BLK))
        lo = lane < V_DIM

        def update(qh, kh, vh, m, acc, masked):
            sc = _dot_nt(qh, kh)
            if masked:
                sc = jnp.where(_lanes((t, t)) <= _rows((t, t)), sc, -1e30)
            m_new = jnp.maximum(m, jnp.max(sc, axis=-1, keepdims=True))
            p = jnp.exp2(sc - m_new).astype(MXU_DTYPE)
            return m_new, acc * jnp.exp2(m - m_new) + _dot(p, vh)

        def step(j, carry, masked):
            ma, mb, acc_a, acc_b = carry
            kv_rows = pl.ds(pl.multiple_of(j * t, t), t)
            kt = k_ref[kv_rows, :]
            vt = v_ref[kv_rows, :]
            lane_v = _lanes(vt.shape)
            one = jnp.ones_like(vt)
            zero_v = jnp.zeros_like(vt)
            v_a = jnp.where(lane_v < V_DIM, vt, jnp.where(lane_v == SUM_LANE_A, one, zero_v))
            v_b = jnp.where(lane_v >= V_DIM, vt, jnp.where(lane_v == SUM_LANE_B, one, zero_v))
            ma, acc_a = update(qa, kt[:, :HEAD_BLK], v_a, ma, acc_a, masked)
            mb, acc_b = update(qb, kt[:, HEAD_BLK:], v_b, mb, acc_b, masked)
            return ma, mb, acc_a, acc_b

        neg = jnp.full((t, 1), -1e30, F32)
        zero = jnp.zeros((t, HEAD_BLK), F32)
        def trip(jj, cr):
            for u in range(FWD_TILES_PER_TRIP):
                cr = step(FWD_TILES_PER_TRIP * jj + u, cr, False)
            return cr

        whole = i // FWD_TILES_PER_TRIP
        carry = lax.fori_loop(0, whole, trip, (neg, neg, zero, zero))
        carry = lax.fori_loop(FWD_TILES_PER_TRIP * whole, i, lambda j, cr: step(j, cr, False), carry)
        ma, mb, acc_a, acc_b = step(i, carry, True)
        la = jnp.sum(jnp.where(lane == SUM_LANE_A, acc_a, 0.0), axis=-1, keepdims=True)
        lb = jnp.sum(jnp.where(lane == SUM_LANE_B, acc_b, 0.0), axis=-1, keepdims=True)
        o = jnp.where(lo, acc_a * (1.0 / la), acc_b * (1.0 / lb))
        o_ref[...] = o
        gb = gb_ref[...]
        yb_ref[...] = (o * (gb * _sig(gb))).astype(yb_ref.dtype)
        lse = jnp.where(lo, ma + jnp.log(la) * LOG2_E, mb + jnp.log(lb) * LOG2_E)
        pick = ((_rows((SUBLANES, HEAD_BLK)) == 0) & (_lanes((SUBLANES, HEAD_BLK)) == 0)) | (
            (_rows((SUBLANES, HEAD_BLK)) == 1) & (_lanes((SUBLANES, HEAD_BLK)) == V_DIM))
        lse_ref[0, 0] = lax.dot_general(pick.astype(F32), lse, (((1,), (1,)), ((), ())),
                                        precision=lax.Precision.HIGHEST, preferred_element_type=F32)

    return pl.pallas_call(
        body, name="flash_fwd", grid=(N_PAIR, nq),
        in_specs=[pl.BlockSpec((t, pw), lambda p, i: (i, p)), pl.BlockSpec((s, pw), lambda p, i: (0, p)),
                  pl.BlockSpec((s, HEAD_BLK), lambda p, i: (0, p)),
                  pl.BlockSpec((t, HEAD_BLK), lambda p, i: (i, 9 + p))],
        out_specs=[pl.BlockSpec((t, HEAD_BLK), lambda p, i: (i, p)), pl.BlockSpec((t, HEAD_BLK), lambda p, i: (i, p)),
                   pl.BlockSpec((1, 1, SUBLANES, t), lambda p, i: (p, i, 0, 0))],
        out_shape=[jax.ShapeDtypeStruct((s, D_MLA), F32), jax.ShapeDtypeStruct((s, D_MLA), MXU_DTYPE),
                   jax.ShapeDtypeStruct((N_PAIR, nq, SUBLANES, t), F32)],
        compiler_params=_cp(("parallel", "parallel"), 48),
    )(q, k, v, z)


def _attn_bwd_prep(dy, o, z):
    s = o.shape[0]
    t = _tile(s)
    nq = s // t
    rows = N_PAIR * SUBLANES

    def body(dy_ref, o_ref, gb_ref, do_ref, dgb_ref, dl_ref):
        gb = gb_ref[...]
        sg = _sig(gb)
        dyb = dy_ref[...]
        ov = o_ref[...]
        do = dyb * (gb * sg)
        do_ref[...] = do.astype(do_ref.dtype)
        dgb_ref[...] = (dyb * ov * (sg * (1.0 + gb * (1.0 - sg)))).astype(dgb_ref.dtype)
        r = _rows((rows, D_MLA))
        head = (r // SUBLANES) * 2 + (r % SUBLANES)
        sel = ((r % SUBLANES) < 2) & (_lanes((rows, D_MLA)) // V_DIM == head)
        dl = lax.dot_general(sel.astype(F32), do * ov, (((1,), (1,)), ((), ())),
                             precision=lax.Precision.HIGHEST, preferred_element_type=F32)
        for p in range(N_PAIR):
            dl_ref[p, 0] = dl[p * SUBLANES:(p + 1) * SUBLANES]

    return pl.pallas_call(
        body, name="attn_bwd_prep", grid=(nq,),
        in_specs=[pl.BlockSpec((t, D_MLA), lambda i: (i, 1)), pl.BlockSpec((t, D_MLA), lambda i: (i, 0)),
                  pl.BlockSpec((t, D_MLA), lambda i: (i, 3))],
        out_specs=[pl.BlockSpec((t, D_MLA), lambda i: (i, 0)), pl.BlockSpec((t, D_MLA), lambda i: (i, 0)),
                   pl.BlockSpec((N_PAIR, 1, SUBLANES, t), lambda i: (0, i, 0, 0))],
        out_shape=[jax.ShapeDtypeStruct((s, D_MLA), MXU_DTYPE), jax.ShapeDtypeStruct((s, D_MLA), MXU_DTYPE),
                   jax.ShapeDtypeStruct((N_PAIR, nq, SUBLANES, t), F32)],
        compiler_params=_cp(("parallel",), 32),
    )(dy, o, z)


def _flash_bwd(q, k, v, do, lse, delta):
    s = q.shape[0]
    t = _tile(s)
    nq = s // t
    pw = 2 * HEAD_BLK

    def body(q_ref, do_ref, lse_ref, dl_ref, k_ref, v_ref, dq_ref, dk_ref, dv_ref):
        j = pl.program_id(1)

        @pl.when(j == 0)
        def _():
            dq_ref[...] = jnp.zeros_like(dq_ref)

        kt = k_ref[...]
        ka, kb = kt[:, :HEAD_BLK], kt[:, HEAD_BLK:]
        vt = v_ref[...]

        def head(kh, qh, do_h, lse_row, dl_row, masked):
            st = _dot_nt(kh, qh)
            if masked:
                st = jnp.where(_rows((t, t)) <= _lanes((t, t)), st, -1e30)
            pt = jnp.exp2(st - lse_row)
            dv_h = _dot(pt, do_h)
            dst = (pt * (_dot_nt(vt, do_h) - dl_row)).astype(MXU_DTYPE)
            return dv_h, _dot(dst, qh), _dot_tn(dst, kh)

        def step(i, carry, masked):
            dka, dkb, dv = carry
            q_rows = pl.ds(pl.multiple_of(i * t, t), t)
            qv = q_ref[q_rows, :]
            dov = do_ref[q_rows, :]
            lane = _lanes(dov.shape)
            do_lo = jnp.where(lane < V_DIM, dov, jnp.zeros_like(dov))
            do_hi = jnp.where(lane >= V_DIM, dov, jnp.zeros_like(dov))
            dva, dk_a, dq_a = head(ka, qv[:, :HEAD_BLK], do_lo, lse_ref[0, i, 0:1, :], dl_ref[0, i, 0:1, :], masked)
            dvb, dk_b, dq_b = head(kb, qv[:, HEAD_BLK:], do_hi, lse_ref[0, i, 1:2, :], dl_ref[0, i, 1:2, :], masked)
            dq_ref[q_rows, 0:HEAD_BLK] += dq_a
            dq_ref[q_rows, HEAD_BLK:pw] += dq_b
            return dka + dk_a, dkb + dk_b, dv + dva + dvb

        zero = jnp.zeros((t, HEAD_BLK), F32)
        carry = step(j, (zero, zero, zero), True)
        rest = nq - 1 - j
        carry = lax.cond(rest % 2 == 1, lambda cr: step(j + 1, cr, False), lambda cr: cr, carry)
        first = j + 1 + rest % 2
        dka, dkb, dv = lax.fori_loop(
            0, rest // 2, lambda ii, cr: step(first + 2 * ii + 1, step(first + 2 * ii, cr, False), False), carry)
        dk_ref[:, 0:HEAD_BLK] = dka * LN_2
        dk_ref[:, HEAD_BLK:pw] = dkb * LN_2
        dv_ref[...] = dv.astype(dv_ref.dtype)

    return pl.pallas_call(
        body, name="flash_bwd", grid=(N_PAIR, nq),
        in_specs=[pl.BlockSpec((s, pw), lambda p, j: (0, p)), pl.BlockSpec((s, HEAD_BLK), lambda p, j: (0, p)),
                  pl.BlockSpec((1, nq, SUBLANES, t), lambda p, j: (p, 0, 0, 0)),
                  pl.BlockSpec((1, nq, SUBLANES, t), lambda p, j: (p, 0, 0, 0)),
                  pl.BlockSpec((t, pw), lambda p, j: (j, p)), pl.BlockSpec((t, HEAD_BLK), lambda p, j: (j, p))],
        out_specs=[pl.BlockSpec((s, pw), lambda p, j: (0, p)), pl.BlockSpec((t, pw), lambda p, j: (j, p)),
                   pl.BlockSpec((t, HEAD_BLK), lambda p, j: (j, p))],
        out_shape=[jax.ShapeDtypeStruct((s, MLA_HEADS * HEAD_BLK), F32),
                   jax.ShapeDtypeStruct((s, MLA_HEADS * HEAD_BLK), F32),
                   jax.ShapeDtypeStruct((s, D_MLA), MXU_DTYPE)],
        compiler_params=_cp(("parallel", "arbitrary"), 56),
    )(q, do, lse, delta, k, v)


def _mla_prep_bwd(dq, dk, dv, z, qn, kvn, gq, gkv, wuq, wukv, c_t, s1_t, s2_t, l):
    s = z.shape[0]
    t = _tile(s)
    hq = MLA_HEADS * HEAD_BLK
    lay = _lay(l)

    def body(dq_ref, dk_ref, dv_ref, cq_ref, ckv_ref, qn_ref, kvn_ref, gq_ref, gkv_ref, wuq_ref, wukv_ref,
             c_ref, s1_ref, s2_ref, dcq_ref, dckv_ref, dkr_ref, dwuq_ref, dwukv_ref, dgq_ref, dgkv_ref,
             dqu_ref, dkvp_ref):
        @pl.when(pl.program_id(0) == 0)
        def _():
            for ref in (dwuq_ref, dwukv_ref, dgq_ref, dgkv_ref):
                ref[...] = jnp.zeros_like(ref)

        ct, s1, s2 = c_ref[...], s1_ref[...], s2_ref[...]
        dk_sum = jnp.zeros((t, HEAD_BLK), F32)
        for h in range(MLA_HEADS):
            blk = slice(h * HEAD_BLK, (h + 1) * HEAD_BLK)
            dqu_ref[:, blk] = _unrope(dq_ref[:, blk] * ATT_SCALE, ct, s1, s2).astype(dqu_ref.dtype)
            dkh = dk_ref[:, blk]
            dk_sum = dk_sum + dkh
            dkvp_ref[:, blk] = dkh.astype(dkvp_ref.dtype)
        dkvp_ref[:, hq:] = dv_ref[...]
        lane = _lanes((t, HEAD_BLK))
        rope_lanes = (lane >= KR_LANE0) & (lane < KR_LANE0 + QK_ROPE)
        dkr_ref[...] = _unrope(jnp.where(rope_lanes, dk_sum, 0.0), ct, s1, s2).astype(dkr_ref.dtype)

        def norm_bwd(c_in, g, dn_out, dc_ref, dg_ref):
            rs = lax.rsqrt(jnp.mean(c_in * c_in, axis=-1, keepdims=True) + EPS)
            n = c_in * rs
            dg_ref[...] += jnp.sum(dn_out * n, axis=0, keepdims=True)
            dn = dn_out * g
            dc_ref[...] = (rs * (dn - n * jnp.mean(dn * n, axis=-1, keepdims=True))).astype(dc_ref.dtype)

        dqu, dkvp = dqu_ref[...], dkvp_ref[...]
        dwuq_ref[...] += _dot_tn(qn_ref[...], dqu)
        dwukv_ref[...] += _dot_tn(kvn_ref[...], dkvp)
        norm_bwd(cq_ref[...], gq_ref[...], _dot_nt(dqu, wuq_ref[...]), dcq_ref, dgq_ref)
        norm_bwd(ckv_ref[...], gkv_ref[...], _dot_nt(dkvp, wukv_ref[...]), dckv_ref, dgkv_ref)

    full = lambda shp: pl.BlockSpec(shp, lambda i: (0, 0))
    tab = pl.BlockSpec((t, HEAD_BLK), lambda i: (i, 0))
    row = lambda w: pl.BlockSpec((t, w), lambda i: (i, 0))
    return pl.pallas_call(
        body, name="mla_prep_bwd", grid=(s // t,),
        in_specs=[row(hq), row(hq), row(D_MLA),
                  pl.BlockSpec((t, Q_RANK), lambda i: (i, 2)), pl.BlockSpec((t, KV_RANK), lambda i: (i, 6)),
                  row(Q_RANK), row(KV_RANK),
                  lay((1, Q_RANK)), lay((1, KV_RANK)), lay((Q_RANK, hq)), lay((KV_RANK, hq + D_MLA)),
                  tab, tab, tab],
        out_specs=[row(Q_RANK), row(KV_RANK), row(HEAD_BLK), full((Q_RANK, hq)), full((KV_RANK, hq + D_MLA)),
                   full((1, Q_RANK)), full((1, KV_RANK))],
        out_shape=[jax.ShapeDtypeStruct((s, Q_RANK), MXU_DTYPE), jax.ShapeDtypeStruct((s, KV_RANK), MXU_DTYPE),
                   jax.ShapeDtypeStruct((s, HEAD_BLK), MXU_DTYPE),
                   jax.ShapeDtypeStruct((Q_RANK, hq), F32), jax.ShapeDtypeStruct((KV_RANK, hq + D_MLA), F32),
                   jax.ShapeDtypeStruct((1, Q_RANK), F32), jax.ShapeDtypeStruct((1, KV_RANK), F32)],
        scratch_shapes=[pltpu.VMEM((t, hq), MXU_DTYPE), pltpu.VMEM((t, hq + D_MLA), MXU_DTYPE)],
        compiler_params=_cp(("arbitrary",), 48),
    )(dq, dk, dv, z, z, qn, kvn, gq, gkv, wuq, wukv, c_t, s1_t, s2_t)


def _outproj_fwd(x, ya, yb, yc, w, l):
    s = x.shape[0]
    t = _tile(s)
    lay = _lay(l)

    def body(x_ref, ya_ref, yb_ref, yc_ref, wa_ref, wb_ref, wc_ref, o_ref):
        o_ref[...] = (x_ref[...] + _dot(ya_ref[...], wa_ref[...]) + _dot(yb_ref[...], wb_ref[...])
                      + _dot(yc_ref[...], wc_ref[...]))

    row = lambda w_: pl.BlockSpec((t, w_), lambda i: (i, 0))
    return pl.pallas_call(
        body, name="outproj_fwd", grid=(s // t,),
        in_specs=[row(D_MODEL), row(D_LRU), row(D_MLA), row(D_POOL),
                  lay((D_LRU, D_MODEL), 0), lay((D_MLA, D_MODEL), 1), lay((D_POOL, D_MODEL), 3)],
        out_specs=row(D_MODEL),
        out_shape=jax.ShapeDtypeStruct((s, D_MODEL), F32),
        compiler_params=_cp(("parallel",), 40),
    )(x, ya, yb, yc, w, w, w)


def _outproj_bwd(dx, ya, yb, yc, w, l):
    s = dx.shape[0]
    t = _tile(s)

    def body(dx_ref, ya_ref, yb_ref, yc_ref, w_ref, dy_ref, dw_ref):
        @pl.when(pl.program_id(0) == 0)
        def _():
            dw_ref[...] = jnp.zeros_like(dw_ref)

        dxv = dx_ref[...].astype(MXU_DTYPE)
        dy_ref[...] = _dot_nt(dxv, w_ref[...])
        dw_ref[0:D_LRU, :] += _dot_tn(ya_ref[...], dxv)
        dw_ref[D_LRU:D_LRU + D_MLA, :] += _dot_tn(yb_ref[...], dxv)
        dw_ref[D_LRU + D_MLA:D_MIX, :] += _dot_tn(yc_ref[...], dxv)

    row = lambda w_: pl.BlockSpec((t, w_), lambda i: (i, 0))
    return pl.pallas_call(
        body, name="outproj_bwd", grid=(s // t,),
        in_specs=[row(D_MODEL), row(D_LRU), row(D_MLA), row(D_POOL), _lay(l)((D_MIX, D_MODEL))],
        out_specs=[row(D_MIX), pl.BlockSpec((D_MIX, D_MODEL), lambda i: (0, 0))],
        out_shape=[jax.ShapeDtypeStruct((s, D_MIX), F32), jax.ShapeDtypeStruct((D_MIX, D_MODEL), F32)],
        compiler_params=_cp(("arbitrary",), 48),
    )(dx, ya, yb, yc, w)


DZ_WIDTHS = (D_LRU, D_LRU, Q_RANK, D_MLA, KV_RANK, D_POOL, D_POOL, HEAD_BLK)


def _dwin(h, dz_parts):
    s = h.shape[0]
    t = _tile(s)

    def body(h_ref, *refs):
        o_ref = refs[-1]

        @pl.when(pl.program_id(0) == 0)
        def _():
            o_ref[...] = jnp.zeros_like(o_ref)

        o_ref[...] += _dot_tn(h_ref[...], jnp.concatenate([r[...] for r in refs[:-1]], axis=1))

    row = lambda w_: pl.BlockSpec((t, w_), lambda i: (i, 0))
    return pl.pallas_call(
        body, name="dwin", grid=(s // t,),
        in_specs=[row(D_MODEL)] + [row(wd) for wd in DZ_WIDTHS],
        out_specs=pl.BlockSpec((D_MODEL, D_INP), lambda i: (0, 0)),
        out_shape=jax.ShapeDtypeStruct((D_MODEL, D_INP), F32),
        compiler_params=_cp(("arbitrary",), 56),
    )(h, *dz_parts)


def _inproj_bwd(dz_parts, w, x, g, dxn, l):
    s = x.shape[0]
    t = _tile(s)
    lay = _lay(l)
    n_parts = len(DZ_WIDTHS)

    def body(*refs):
        part_refs = refs[:n_parts]
        w_ref, x_ref, g_ref, dxn_ref, dx_ref, dg_ref = refs[n_parts:]

        @pl.when(pl.program_id(0) == 0)
        def _():
            dg_ref[...] = jnp.zeros_like(dg_ref)

        dh = _dot_nt(jnp.concatenate([r[...] for r in part_refs], axis=1), w_ref[...])
        xv = x_ref[...]
        rs = lax.rsqrt(jnp.mean(xv * xv, axis=-1, keepdims=True) + EPS)
        n = xv * rs
        dg_ref[...] += jnp.sum(dh * n, axis=0, keepdims=True)
        dn = dh * g_ref[...]
        dx_ref[...] = dxn_ref[...] + rs * (dn - n * jnp.mean(dn * n, axis=-1, keepdims=True))

    row = lambda w_: pl.BlockSpec((t, w_), lambda i: (i, 0))
    return pl.pallas_call(
        body, name="inproj_bwd", grid=(s // t,),
        in_specs=[row(wd) for wd in DZ_WIDTHS] + [lay((D_MODEL, D_INP)), row(D_MODEL), lay((1, D_MODEL)),
                                                  row(D_MODEL)],
        out_specs=[row(D_MODEL), pl.BlockSpec((1, D_MODEL), lambda i: (0, 0))],
        out_shape=[jax.ShapeDtypeStruct((s, D_MODEL), F32), jax.ShapeDtypeStruct((1, D_MODEL), F32)],
        compiler_params=_cp(("arbitrary",), 48),
    )(*dz_parts, w, x, g, dxn)


def _loss_head(x, g, tgt):
    s = x.shape[0]
    t = _tile(s)

    def body(x_ref, g_ref, t_ref, dx_ref, loss_ref, dg_ref):
        @pl.when(pl.program_id(0) == 0)
        def _():
            loss_ref[...] = jnp.zeros_like(loss_ref)
            dg_ref[...] = jnp.zeros_like(dg_ref)

        xv = x_ref[...]
        rs = lax.rsqrt(jnp.mean(xv * xv, axis=-1, keepdims=True) + EPS)
        n = xv * rs
        gv = g_ref[...]
        e = n * gv - t_ref[...]
        loss_ref[...] += 0.5 * jnp.sum(jnp.mean(e * e, axis=-1, keepdims=True))
        dyf = e * (1.0 / D_MODEL)
        dg_ref[...] += jnp.sum(dyf * n, axis=0, keepdims=True)
        dn = dyf * gv
        dx_ref[...] = rs * (dn - n * jnp.mean(dn * n, axis=-1, keepdims=True))

    row = pl.BlockSpec((t, D_MODEL), lambda i: (i, 0))
    vec = pl.BlockSpec((1, D_MODEL), lambda i: (0, 0))
    return pl.pallas_call(
        body, name="loss_head", grid=(s // t,),
        in_specs=[row, vec, row],
        out_specs=[row, pl.BlockSpec((1, LANES), lambda i: (0, 0)), vec],
        out_shape=[jax.ShapeDtypeStruct((s, D_MODEL), F32), jax.ShapeDtypeStruct((1, LANES), F32),
                   jax.ShapeDtypeStruct((1, D_MODEL), F32)],
        compiler_params=_cp(("arbitrary",), 32),
    )(x, g, tgt)


def _block_diag(w):
    n, h, d, _ = w.shape
    return jnp.einsum('lhij,hk->lhikj', w, jnp.eye(h, dtype=w.dtype)).reshape(n, h * d, h * d)


def _diag_blocks(wfull, h):
    d = wfull.shape[-1] // h
    return jnp.stack([wfull[:, i * d:(i + 1) * d, i * d:(i + 1) * d] for i in range(h)], axis=1)


REF_TO_PERM = np.concatenate([np.arange(0, 1152), np.arange(1536, 1792),
                              np.arange(2304 + KR_LANE0, 2304 + KR_LANE0 + QK_ROPE),
                              np.arange(1152, 1536), np.arange(1792, 2304)])
N_SHARD = 4
W_IN_SHARD = D_IN // N_SHARD


def _w_in_runs():
    ref_of_perm = -np.ones(D_INP, np.int64)
    ref_of_perm[REF_TO_PERM] = np.arange(D_IN)
    perm_runs, p = [], 0
    while p < D_INP:
        r, q = ref_of_perm[p], p + 1
        if r < 0:
            while q < D_INP and ref_of_perm[q] < 0:
                q += 1
            perm_runs.append((None, q - p, 0))
        else:
            while (q < D_INP and ref_of_perm[q] == ref_of_perm[q - 1] + 1
                   and ref_of_perm[q] // W_IN_SHARD == r // W_IN_SHARD):
                q += 1
            perm_runs.append((int(r // W_IN_SHARD), int(r % W_IN_SHARD), int(r % W_IN_SHARD + q - p)))
        p = q
    shard_runs = []
    for s in range(N_SHARD):
        cols = REF_TO_PERM[s * W_IN_SHARD:(s + 1) * W_IN_SHARD]
        runs, a = [], 0
        for b in range(1, W_IN_SHARD + 1):
            if b == W_IN_SHARD or cols[b] != cols[b - 1] + 1:
                runs.append((int(cols[a]), int(cols[b - 1]) + 1))
                a = b
        shard_runs.append(runs)
    return perm_runs, shard_runs


def _permute_w_in(shards):
    perm_runs, _ = _w_in_runs()
    lead = shards[0].shape[:-1]
    parts = [jnp.zeros(lead + (a,), shards[0].dtype) if s is None else shards[s][..., a:b] for s, a, b in perm_runs]
    return jnp.concatenate(parts, axis=-1)


def _w_in_shard(wp, s):
    _, shard_runs = _w_in_runs()
    return jnp.concatenate([wp[..., a:b] for a, b in shard_runs[s]], axis=-1)


def _pad_w_uq(w):
    w4 = w.reshape(w.shape[:2] + (MLA_HEADS, QK_NOPE + QK_ROPE))
    return jnp.pad(w4, ((0, 0),) * 3 + ((0, HEAD_BLK - QK_NOPE - QK_ROPE),)).reshape(w.shape[:2] + (-1,))


def _unpad_w_uq(w):
    return w.reshape(w.shape[:2] + (MLA_HEADS, HEAD_BLK))[..., :QK_NOPE + QK_ROPE].reshape(w.shape[:2] + (-1,))


def _pad_w_ukv(w):
    w4 = w.reshape(w.shape[:2] + (MLA_HEADS, QK_NOPE + V_DIM))
    kpart = jnp.pad(w4[..., :QK_NOPE], ((0, 0),) * 3 + ((0, HEAD_BLK - QK_NOPE),)).reshape(w.shape[:2] + (-1,))
    return jnp.concatenate([kpart, w4[..., QK_NOPE:].reshape(w.shape[:2] + (-1,))], axis=2)


def _unpad_w_ukv(w):
    hq = MLA_HEADS * HEAD_BLK
    kpart = w[..., :hq].reshape(w.shape[:2] + (MLA_HEADS, HEAD_BLK))[..., :QK_NOPE]
    vpart = w[..., hq:].reshape(w.shape[:2] + (MLA_HEADS, V_DIM))
    return jnp.concatenate([kpart, vpart], axis=3).reshape(w.shape[:2] + (-1,))


def _local_step(x, tgt, w):
    s = x.shape[0]
    tabs = _rope_tables(s)
    vec = lambda a: a[:, None, :]
    mxu = lambda a: a.astype(MXU_DTYPE)
    p = dict(g=vec(w['norm_g']), w_in=mxu(w['w_in']), cw=w['conv_w'], cb=vec(w['conv_b']),
             wr=mxu(_block_diag(w['w_rg'])), br=vec(w['b_rg']), wi=mxu(_block_diag(w['w_ig'])), bi=vec(w['b_ig']),
             lam=vec(w['lru_lambda']), gq=vec(w['q_norm_g']), gkv=vec(w['kv_norm_g']),
             wuq=mxu(_pad_w_uq(w['w_uq'])), wukv=mxu(_pad_w_ukv(w['w_ukv'])),
             wp=mxu(_block_diag(w['w_pool'])), ps=vec(w['pool_scale']), wout=mxu(w['w_out']))
    lru = lambda l: (p['cw'], p['cb'], p['wr'], p['br'], p['wi'], p['bi'], p['lam'], l)
    mla = lambda l: (p['gq'], p['gkv'], p['wuq'], p['wukv'], *tabs, l)

    saved = []
    for l in range(DEPTH):
        h, z = _inproj_fwd(x, p['g'], p['w_in'], l)
        ya, hs = _lru_fwd(z, *lru(l))
        yc = _pool_fwd(z, p['wp'], p['ps'], l)
        q, k, v, qn, kvn = _mla_prep_fwd(z, *mla(l))
        o, yb, lse = _flash_fwd(q, k, v, z)
        saved.append(dict(x=x, h=h, z=z, hs=hs, ya=ya, yb=yb, yc=yc, q=q, k=k, v=v, qn=qn, kvn=kvn, o=o, lse=lse))
        x = _outproj_fwd(x, ya, yb, yc, p['wout'], l)

    dx, loss, dgf = _loss_head(x, w['final_norm_g'][None], tgt)
    per_layer = {n: [None] * DEPTH for n in WEIGHT_NAMES if n != 'final_norm_g'}
    for l in reversed(range(DEPTH)):
        sv = saved[l]
        dy, per_layer['w_out'][l] = _outproj_bwd(dx, sv['ya'], sv['yb'], sv['yc'], p['wout'], l)
        dza, dga, *lru_grads = _lru_bwd(sv['z'], sv['hs'], dy, *lru(l))
        for n, g in zip(('w_rg', 'w_ig', 'conv_w', 'conv_b', 'b_rg', 'b_ig', 'lru_lambda'), lru_grads):
            per_layer[n][l] = g
        dzc, dgc, per_layer['w_pool'][l], per_layer['pool_scale'][l] = _pool_bwd(sv['z'], dy, p['wp'], p['ps'], l)
        do, dgb, delta = _attn_bwd_prep(dy, sv['o'], sv['z'])
        dq, dk, dv = _flash_bwd(sv['q'], sv['k'], sv['v'], do, sv['lse'], delta)
        (dcq, dckv, dkr, per_layer['w_uq'][l], per_layer['w_ukv'][l], per_layer['q_norm_g'][l],
         per_layer['kv_norm_g'][l]) = _mla_prep_bwd(dq, dk, dv, sv['z'], sv['qn'], sv['kvn'], *mla(l))
        dz_parts = (dza, dga, dcq, dgb, dckv, dzc, dgc, dkr)
        per_layer['w_in'][l] = _dwin(sv['h'], dz_parts)
        dx, per_layer['norm_g'][l] = _inproj_bwd(dz_parts, p['w_in'], sv['x'], p['g'], dx, l)
    grads = {n: jnp.stack(g) for n, g in per_layer.items()}
    for n in ('norm_g', 'conv_b', 'b_rg', 'b_ig', 'lru_lambda', 'q_norm_g', 'kv_norm_g', 'pool_scale'):
        grads[n] = grads[n][:, 0, :]
    grads['w_rg'] = _diag_blocks(grads['w_rg'], LRU_HEADS)
    grads['w_ig'] = _diag_blocks(grads['w_ig'], LRU_HEADS)
    grads['w_pool'] = _diag_blocks(grads['w_pool'], len(POOL_WINDOWS))
    grads['w_uq'] = _unpad_w_uq(grads['w_uq'])
    grads['w_ukv'] = _unpad_w_ukv(grads['w_ukv'])
    grads['final_norm_g'] = dgf[0]
    return loss[0, 0], dx, grads


WIRE_DTYPE = jnp.bfloat16
MESH_IDS = pl.DeviceIdType.MESH
_HBM = pl.BlockSpec(memory_space=pltpu.HBM)


def _coords():
    return lax.axis_index("x"), lax.axis_index("y"), lax.axis_index("c")


def _comm_call(body, name, arrays, out_shapes, copies_per_array):
    n = len(arrays)
    return pl.pallas_call(
        body, name=name, out_shape=out_shapes, in_specs=[_HBM] * n, out_specs=[_HBM] * n,
        scratch_shapes=[pltpu.SemaphoreType.DMA((n, copies_per_array)), pltpu.SemaphoreType.DMA((n, copies_per_array))],
    )(*arrays)


def _all_gather8(blocks, name):
    n = len(blocks)
    every = range(n)

    def body(*refs):
        x_refs, out_refs = refs[:n], refs[n:2 * n]
        send_sems, recv_sems = refs[2 * n:]
        x, y, c = _coords()
        me, sibling = (x, y, c), (x, y, 1 - c)
        chips = [(1 - x, y), (x, 1 - y), (1 - x, 1 - y)]

        def slot(t, px, py, pc):
            return out_refs[t].at[4 * px + 2 * py + pc]

        def copy(t, k, block, to, own=False):
            return pltpu.make_async_remote_copy(
                src_ref=x_refs[t] if own else slot(t, *block), dst_ref=slot(t, *block),
                send_sem=send_sems.at[t, k], recv_sem=recv_sems.at[t, k], device_id=to, device_id_type=MESH_IDS)

        first = [copy(t, 0, me, sibling, own=True) for t in every]
        first += [copy(t, 1 + j, me, (*chip, c), own=True) for j, chip in enumerate(chips) for t in every]
        for cp in first:
            cp.start()
        passed = [[copy(t, 4 + j, (*chip, c), sibling) for t in every] for j, chip in enumerate(chips)]
        for j, chip in enumerate(chips):
            for t in every:
                copy(t, 1 + j, (*chip, c), me).wait_recv()
                passed[j][t].start()
        for t in every:
            copy(t, 0, sibling, me).wait_recv()
        for j, chip in enumerate(chips):
            for t in every:
                copy(t, 4 + j, (*chip, 1 - c), me).wait_recv()
        for cp in first + [cp for group in passed for cp in group]:
            cp.wait_send()

    outs = [jax.ShapeDtypeStruct((N_DEV,) + b.shape, b.dtype) for b in blocks]
    got = _comm_call(body, name, blocks, outs, 7)
    me = 4 * lax.axis_index("x") + 2 * lax.axis_index("y") + lax.axis_index("c")
    return [lax.dynamic_update_index_in_dim(g, b, me, 0) for g, b in zip(got, blocks)]


def _sibling_send(arrs, name, which_half=False):
    n = len(arrs)

    def body(*refs):
        a_refs, out_refs = refs[:n], refs[n:2 * n]
        send_sems, recv_sems = refs[2 * n:]
        x, y, c = _coords()
        sent = [pltpu.make_async_remote_copy(
            src_ref=a_refs[t].at[1 - c] if which_half else a_refs[t], dst_ref=out_refs[t],
            send_sem=send_sems.at[t, 0], recv_sem=recv_sems.at[t, 0],
            device_id=(x, y, 1 - c), device_id_type=MESH_IDS) for t in range(n)]
        for cp in sent:
            cp.start()
        for cp in sent:
            cp.wait()

    shapes = [jax.ShapeDtypeStruct(a.shape[1:] if which_half else a.shape, a.dtype) for a in arrs]
    return _comm_call(body, name, arrs, shapes, 1)


def _chip_exchange(arrs, name):
    n = len(arrs)

    def body(*refs):
        a_refs, out_refs = refs[:n], refs[n:2 * n]
        send_sems, recv_sems = refs[2 * n:]
        x, y, c = _coords()
        copies = []
        for j, (cx, cy) in enumerate([(1 - x, y), (x, 1 - y), (1 - x, 1 - y)]):
            copies += [pltpu.make_async_remote_copy(
                src_ref=a_refs[t].at[2 * cx + cy], dst_ref=out_refs[t].at[j], send_sem=send_sems.at[t, j],
                recv_sem=recv_sems.at[t, j], device_id=(cx, cy, c), device_id_type=MESH_IDS) for t in range(n)]
        for cp in copies:
            cp.start()
        for cp in copies:
            cp.wait()

    return _comm_call(body, name, arrs, [jax.ShapeDtypeStruct((3,) + a.shape[1:], a.dtype) for a in arrs], 3)


def _sum_leading(groups, out_dtype, steps, name):
    flat = [a for g in groups for a in g]

    def body(*refs):
        ins, outs, pos = refs[:len(flat)], refs[len(flat):], 0
        for g, o_ref in zip(groups, outs):
            acc = None
            for i_ref in ins[pos:pos + len(g)]:
                for k in range(i_ref.shape[0]):
                    term = i_ref[k].astype(F32)
                    acc = term if acc is None else acc + term
            pos += len(g)
            o_ref[...] = acc.astype(o_ref.dtype)

    return pl.pallas_call(
        body, name=name, grid=(steps,),
        in_specs=[pl.BlockSpec((a.shape[0], a.shape[1] // steps, a.shape[2]), lambda i: (0, i, 0)) for a in flat],
        out_specs=[pl.BlockSpec((g[0].shape[1] // steps, g[0].shape[2]), lambda i: (i, 0)) for g in groups],
        out_shape=[jax.ShapeDtypeStruct(g[0].shape[1:], out_dtype) for g in groups],
        compiler_params=_cp(("parallel",), 40),
    )(*flat)


def _adamw_update(w_ref, g_ref, m_ref, v_ref, d_ref, mo_ref, vo_ref):
    gv = g_ref[...]
    mn = ADAM_B1 * m_ref[...] + (1.0 - ADAM_B1) * gv
    vn = ADAM_B2 * v_ref[...] + (1.0 - ADAM_B2) * (gv * gv)
    mo_ref[...] = mn
    vo_ref[...] = vn
    m_hat = mn / (1.0 - ADAM_B1 ** ADAM_STEP)
    v_hat = vn / (1.0 - ADAM_B2 ** ADAM_STEP)
    d_ref[...] = (-ADAM_LR) * (m_hat / (jnp.sqrt(v_hat) + ADAM_EPS) + ADAM_WD * w_ref[...])


def _adamw(w, g, m, v, name):
    n, r, cdim = w.shape
    tr = math.gcd(r, 512)

    def body(*refs):
        _adamw_update(*refs)

    blk = pl.BlockSpec((None, tr, cdim), lambda l, i: (l, i, 0))
    return pl.pallas_call(
        body, name=name, grid=(n, r // tr),
        in_specs=[blk] * 4, out_specs=[blk] * 3,
        out_shape=[jax.ShapeDtypeStruct(w.shape, F32)] * 3,
        compiler_params=_cp(("parallel", "parallel"), 40),
    )(w, g, m, v)


def _adamw_small(ws, gs, ms, vs, name):
    n = len(ws)

    def body(*refs):
        ins, outs = refs[:4 * n], refs[4 * n:]
        for t in range(n):
            _adamw_update(ins[t], ins[n + t], ins[2 * n + t], ins[3 * n + t], outs[t], outs[n + t], outs[2 * n + t])

    shapes = [jax.ShapeDtypeStruct(w.shape, F32) for w in ws]
    outs = pl.pallas_call(body, name=name, out_shape=shapes * 3)(*ws, *gs, *ms, *vs)
    return outs[:n], outs[n:2 * n], outs[2 * n:]


HALF = DEPTH // 2
BIG = ['w_in', 'w_uq', 'w_ukv', 'w_out']
SHARD_AXIS = {'w_in': 2, 'conv_w': 2, 'w_uq': 2, 'w_ukv': 2, 'w_out': 1}
FULL_SHAPE = {'w_in': (DEPTH, D_MODEL, D_IN), 'conv_w': (DEPTH, CONV_WIDTH, D_LRU),
              'w_uq': (DEPTH, Q_RANK, MLA_HEADS * (QK_NOPE + QK_ROPE)),
              'w_ukv': (DEPTH, KV_RANK, MLA_HEADS * (QK_NOPE + V_DIM)), 'w_out': (DEPTH, D_MIX, D_MODEL)}


def _shard_shape(n):
    shp = list(FULL_SHAPE[n])
    shp[SHARD_AXIS[n]] //= N_SHARD
    return tuple(shp)


def _rows_view(a, lead=0):
    return a.reshape(a.shape[:lead] + (-1, a.shape[-1]))


def _gather_weights(local):
    c = lax.axis_index("c")
    names = BIG + ['conv_w']
    halves = [lax.dynamic_slice_in_dim(local[n], HALF * c, HALF, axis=0) for n in names]
    halves = [h.astype(WIRE_DTYPE) if n in BIG else h for n, h in zip(names, halves)]
    got = _all_gather8(halves, "gather_weights")
    full = {}
    for n, g in zip(names, got):
        g = g.reshape((N_SHARD, DEPTH) + g.shape[2:])
        if n == 'w_in':
            full[n] = _permute_w_in([g[s] for s in range(N_SHARD)])
        else:
            full[n] = jnp.moveaxis(g, 0, SHARD_AXIS[n]).reshape(FULL_SHAPE[n])
    return full


def _shard_blocks(g, n):
    width = _shard_shape(n)[SHARD_AXIS[n]]

    def block(h, s):
        part = g[HALF * h:HALF * (h + 1)]
        if n == 'w_in':
            part = _w_in_shard(part, s)
        else:
            part = lax.slice_in_dim(part, s * width, (s + 1) * width, axis=SHARD_AXIS[n])
        return _rows_view(part)

    return jnp.stack([jnp.stack([block(h, s) for s in range(N_SHARD)]) for h in range(2)]).astype(WIRE_DTYPE)


SUM_STEPS = 8


def _reduce_big(grads):
    c = lax.axis_index("c")
    shard = 2 * lax.axis_index("x") + lax.axis_index("y")
    contrib = [_shard_blocks(grads[n], n) for n in BIG]
    from_sibling = _sibling_send(contrib, "pair_exchange_big", which_half=True)
    own_half = [lax.dynamic_index_in_dim(a, c, 0, keepdims=False) for a in contrib]
    pair_sum = _sum_leading([[_rows_view(o)[None], _rows_view(r)[None]] for o, r in zip(own_half, from_sibling)],
                            WIRE_DTYPE, SUM_STEPS, "pair_sum_big")
    to_chips = [p.reshape(a.shape[1:]) for p, a in zip(pair_sum, contrib)]
    from_chips = _chip_exchange(to_chips, "chip_exchange_big")
    own_block = [lax.dynamic_index_in_dim(a, shard, 0, keepdims=True) for a in to_chips]
    mine = _sum_leading([[o, r] for o, r in zip(own_block, from_chips)], F32, SUM_STEPS, "chip_sum_big")
    theirs = _sibling_send(mine, "sibling_big")
    both = [jnp.where(c == 0, jnp.stack([m, t]), jnp.stack([t, m])) for m, t in zip(mine, theirs)]
    return {n: b.reshape(_shard_shape(n)) for n, b in zip(BIG, both)}


SMALL = REPLICATED + ['conv_w']


def _reduce_small(grads):
    views = [_rows_view(jnp.atleast_2d(grads[n])) for n in SMALL]
    sums = _sum_leading([[g] for g in _all_gather8(views, "gather_small")], F32, 1, "sum_small")
    return {n: s.reshape(grads[n].shape) for n, s in zip(SMALL, sums)}


def kernel(x, norm_g, w_in, conv_w, conv_b, w_rg, b_rg, w_ig, b_ig, lru_lambda, q_norm_g, w_uq, kv_norm_g, w_ukv, w_pool, pool_scale, w_out, final_norm_g, loss_target, m_norm_g, m_w_in, m_conv_w, m_conv_b, m_w_rg, m_b_rg, m_w_ig, m_b_ig, m_lru_lambda, m_q_norm_g, m_w_uq, m_kv_norm_g, m_w_ukv, m_w_pool, m_pool_scale, m_w_out, m_final_norm_g, v_norm_g, v_w_in, v_conv_w, v_conv_b, v_w_rg, v_b_rg, v_w_ig, v_b_ig, v_lru_lambda, v_q_norm_g, v_w_uq, v_kv_norm_g, v_w_ukv, v_w_pool, v_pool_scale, v_w_out, v_final_norm_g):
    w_loc = dict(zip(WEIGHT_NAMES, (norm_g, w_in, conv_w, conv_b, w_rg, b_rg, w_ig, b_ig, lru_lambda, q_norm_g, w_uq,
                                    kv_norm_g, w_ukv, w_pool, pool_scale, w_out, final_norm_g)))
    m_loc = dict(zip(WEIGHT_NAMES, (m_norm_g, m_w_in, m_conv_w, m_conv_b, m_w_rg, m_b_rg, m_w_ig, m_b_ig, m_lru_lambda,
                                    m_q_norm_g, m_w_uq, m_kv_norm_g, m_w_ukv, m_w_pool, m_pool_scale, m_w_out,
                                    m_final_norm_g)))
    v_loc = dict(zip(WEIGHT_NAMES, (v_norm_g, v_w_in, v_conv_w, v_conv_b, v_w_rg, v_b_rg, v_w_ig, v_b_ig, v_lru_lambda,
                                    v_q_norm_g, v_w_uq, v_kv_norm_g, v_w_ukv, v_w_pool, v_pool_scale, v_w_out,
                                    v_final_norm_g)))
    w_full = dict(w_loc)
    w_full.update(_gather_weights(w_loc))
    loss_local, dx, g_local = _local_step(x[0], loss_target[0], w_full)
    loss = lax.psum(loss_local, ("x", "y", "c"))

    grads = _reduce_big(g_local)
    g_small = _reduce_small(g_local)
    shard = 2 * lax.axis_index("x") + lax.axis_index("y")
    width = D_LRU // N_SHARD
    grads['conv_w'] = lax.dynamic_slice_in_dim(g_small['conv_w'], shard * width, width, axis=2)
    for n in REPLICATED:
        grads[n] = g_small[n]

    delta, new_m, new_v = {}, {}, {}
    for n in BIG:
        delta[n], new_m[n], new_v[n] = _adamw(w_loc[n], grads[n], m_loc[n], v_loc[n], "adamw_" + n)
    small = [[_rows_view(jnp.atleast_2d(t[n])) for n in SMALL] for t in (w_loc, grads, m_loc, v_loc)]
    for tree, outs in zip((delta, new_m, new_v), _adamw_small(*small, "adamw_small")):
        tree.update({n: a.reshape(w_loc[n].shape) for n, a in zip(SMALL, outs)})

    return (loss, dx[None], *[grads[n] for n in WEIGHT_NAMES], *[delta[n] for n in WEIGHT_NAMES],
            *[new_m[n] for n in WEIGHT_NAMES], *[new_v[n] for n in WEIGHT_NAMES])
```

```python
import functools
import math

import jax
import jax.numpy as jnp
import numpy as np
from jax import lax
from jax.experimental import pallas as pl
from jax.experimental.pallas import tpu as pltpu

F32 = jnp.float32
MXU_DTYPE = jnp.bfloat16

D_MODEL = 1024
DEPTH = 4
EPS = 1e-6
D_LRU = 384
LRU_HEADS = 6
CONV_WIDTH = 4
LRU_C = 8.0
MLA_HEADS = 6
QK_NOPE = 64
QK_ROPE = 32
V_DIM = 64
D_MLA = MLA_HEADS * V_DIM
Q_RANK = 384
KV_RANK = 256
ROPE_BASE = 10000.0
POOL_WINDOWS = (2, 4, 8, 16)
D_POOL = 256
D_MIX = D_LRU + D_MLA + D_POOL
D_IN = 2336
ATT_SCALE = (QK_NOPE + QK_ROPE) ** -0.5
LOG2_E = 1.4426950408889634
LN_2 = 0.6931471805599453
Q_PRESCALE = ATT_SCALE * LOG2_E

ADAM_LR = 0.001
ADAM_B1 = 0.9
ADAM_B2 = 0.999
ADAM_EPS = 1e-08
ADAM_WD = 0.01
ADAM_STEP = 10

LANES = 128
SUBLANES = 8
V7X_VMEM_BYTES = 64 << 20
N_DEV = 8

D_INP = 2432
KR_LANE0 = 64
HEAD_BLK = 128
N_PAIR = MLA_HEADS // 2

WEIGHT_NAMES = ['norm_g', 'w_in', 'conv_w', 'conv_b', 'w_rg', 'b_rg', 'w_ig', 'b_ig', 'lru_lambda', 'q_norm_g',
                'w_uq', 'kv_norm_g', 'w_ukv', 'w_pool', 'pool_scale', 'w_out', 'final_norm_g']
SHARDED = ['w_in', 'conv_w', 'w_uq', 'w_ukv', 'w_out']
REPLICATED = [n for n in WEIGHT_NAMES if n not in SHARDED]


def _cp(sem, vmem_mb=None):
    return pltpu.CompilerParams(dimension_semantics=sem,
                                vmem_limit_bytes=None if vmem_mb is None else vmem_mb << 20)


def _dot(a, b):
    return jnp.dot(a.astype(MXU_DTYPE), b.astype(MXU_DTYPE), preferred_element_type=F32)


def _dot_nt(a, b):
    return lax.dot_general(a.astype(MXU_DTYPE), b.astype(MXU_DTYPE), (((1,), (1,)), ((), ())),
                           preferred_element_type=F32)


def _dot_tn(a, b):
    return lax.dot_general(a.astype(MXU_DTYPE), b.astype(MXU_DTYPE), (((0,), (0,)), ((), ())),
                           preferred_element_type=F32)


def _sig(x):
    return 0.5 * jnp.tanh(0.5 * x) + 0.5


def _down(x, k):
    return pltpu.roll(x, k, 0)


def _up(x, k):
    return pltpu.roll(x, x.shape[0] - k, 0)


def _rows(shape):
    return lax.broadcasted_iota(jnp.int32, shape, 0)


def _lanes(shape):
    return lax.broadcasted_iota(jnp.int32, shape, 1)


def _tile(s):
    return min(512, s)


def _lay(l):
    return lambda shp, blk=0: pl.BlockSpec((None,) + shp, lambda *_: (l, blk, 0))


def _inproj_fwd(x, g, w, l):
    s = x.shape[0]
    t = _tile(s)
    lay = _lay(l)

    def body(x_ref, g_ref, w_ref, h_ref, z_ref):
        xv = x_ref[...]
        rs = lax.rsqrt(jnp.mean(xv * xv, axis=-1, keepdims=True) + EPS)
        h = (xv * rs * g_ref[...]).astype(MXU_DTYPE)
        h_ref[...] = h
        z_ref[...] = jnp.dot(h, w_ref[...], preferred_element_type=F32)

    return pl.pallas_call(
        body, name="inproj_fwd", grid=(s // t,),
        in_specs=[pl.BlockSpec((t, D_MODEL), lambda i: (i, 0)),
                  lay((1, D_MODEL)), lay((D_MODEL, D_INP))],
        out_specs=[pl.BlockSpec((t, D_MODEL), lambda i: (i, 0)),
                   pl.BlockSpec((t, D_INP), lambda i: (i, 0))],
        out_shape=[jax.ShapeDtypeStruct((s, D_MODEL), MXU_DTYPE), jax.ShapeDtypeStruct((s, D_INP), F32)],
        compiler_params=_cp(("parallel",), 40),
    )(x, g, w)


def _lru_gates(za, halo, cw_ref, cb_ref, wr_ref, br_ref, wi_ref, bi_ref, lam_ref):
    t = za.shape[0]
    ext = jnp.concatenate([halo, za], axis=0)
    sh = [za] + [_down(ext, j)[SUBLANES:SUBLANES + t] for j in (1, 2, 3)]
    xa = cb_ref[...] + cw_ref[3:4, :] * sh[0] + cw_ref[2:3, :] * sh[1] + cw_ref[1:2, :] * sh[2] + cw_ref[0:1, :] * sh[3]
    r = 1.0 / (1.0 + jnp.exp(-(_dot(xa, wr_ref[...]) + br_ref[...])))
    ig = _sig(_dot(xa, wi_ref[...]) + bi_ref[...])
    lam = lam_ref[...]
    sp = jnp.maximum(-lam, 0.0) + jnp.log(1.0 + jnp.exp(-jnp.abs(lam)))
    la = (-LRU_C) * r * sp
    a = jnp.exp(la)
    y2 = 2.0 * la
    m2 = jnp.where(y2 > -0.01, -(y2 * (1.0 + y2 * (0.5 + y2 * (1.0 / 6.0)))), 1.0 - a * a)
    return xa, sh, r, ig, sp, a, jnp.sqrt(m2), m2


def _lru_fwd(z, cw, cb, wr, br, wi, bi, lam, l):
    s = z.shape[0]
    t = _tile(s)
    c = D_LRU
    lay = _lay(l)

    def body(za_ref, ga_ref, cw_ref, cb_ref, wr_ref, br_ref, wi_ref, bi_ref, lam_ref, ya_ref, hs_ref, zprev, hcar):
        i = pl.program_id(0)

        @pl.when(i == 0)
        def _():
            zprev[...] = jnp.zeros_like(zprev)
            hcar[...] = jnp.zeros_like(hcar)

        za = za_ref[...]
        xa, _, _, ig, _, a, m, _ = _lru_gates(za, zprev[...], cw_ref, cb_ref, wr_ref, br_ref, wi_ref, bi_ref, lam_ref)
        row = _rows((t, c))
        acc_h = m * (ig * xa) + jnp.where(row == 0, a * hcar[...], 0.0)
        acc_a = jnp.where(row == 0, 0.0, a)
        k = 1
        while k < t:
            acc_h = acc_h + acc_a * _down(acc_h, k)
            acc_a = acc_a * _down(acc_a, k)
            k *= 2
        hs = acc_h
        hs_ref[...] = hs
        ga = ga_ref[...]
        ya_ref[...] = (hs * (ga * _sig(ga))).astype(ya_ref.dtype)
        hcar[...] = jnp.sum(jnp.where(row == t - 1, hs, 0.0), axis=0, keepdims=True)
        zprev[...] = za_ref[t - SUBLANES:t, :]

    return pl.pallas_call(
        body, name="lru_fwd", grid=(s // t,),
        in_specs=[pl.BlockSpec((t, c), lambda i: (i, 0)), pl.BlockSpec((t, c), lambda i: (i, 1)),
                  lay((CONV_WIDTH, c)), lay((1, c)), lay((c, c)), lay((1, c)), lay((c, c)), lay((1, c)),
                  lay((1, c))],
        out_specs=[pl.BlockSpec((t, c), lambda i: (i, 0)), pl.BlockSpec((t, c), lambda i: (i, 0))],
        out_shape=[jax.ShapeDtypeStruct((s, c), MXU_DTYPE), jax.ShapeDtypeStruct((s, c), F32)],
        scratch_shapes=[pltpu.VMEM((SUBLANES, c), F32), pltpu.VMEM((1, c), F32)],
        compiler_params=_cp(("arbitrary",), 40),
    )(z, z, cw, cb, wr, br, wi, bi, lam)


def _lru_bwd(z, hs, dy, cw, cb, wr, br, wi, bi, lam, l):
    s = z.shape[0]
    t = _tile(s)
    lay = _lay(l)
    nt = s // t
    c = D_LRU
    hb = t // SUBLANES

    def body(za_ref, zh_ref, ga_ref, hs_ref, hh_ref, dy_ref, cw_ref, cb_ref, wr_ref, br_ref, wi_ref, bi_ref, lam_ref,
             dza_ref, dga_ref, dwr_ref, dwi_ref, dcw_ref, dcb_ref, dbr_ref, dbi_ref, dlam_ref, lcar, dxn):
        i = pl.program_id(0)
        tt = nt - 1 - i

        @pl.when(i == 0)
        def _():
            lcar[...] = jnp.zeros_like(lcar)
            dxn[...] = jnp.zeros_like(dxn)
            for ref in (dwr_ref, dwi_ref, dcw_ref, dcb_ref, dbr_ref, dbi_ref, dlam_ref):
                ref[...] = jnp.zeros_like(ref)

        first = (tt > 0).astype(F32)
        za = za_ref[...]
        xa, sh, r, ig, sp, a, m, m2 = _lru_gates(za, zh_ref[...] * first, cw_ref, cb_ref, wr_ref, br_ref, wi_ref,
                                                 bi_ref, lam_ref)
        hs_v = hs_ref[...]
        hprev = _down(jnp.concatenate([hh_ref[...] * first, hs_v], axis=0), 1)[SUBLANES:SUBLANES + t]
        ga = ga_ref[...]
        sg = _sig(ga)
        silu = ga * sg
        dya = dy_ref[...]
        dga_ref[...] = (dya * hs_v * (sg * (1.0 + ga * (1.0 - sg)))).astype(dga_ref.dtype)
        row = _rows((t, c))
        acc_h = dya * silu + jnp.where(row == t - 1, lcar[...], 0.0)
        acc_b = jnp.where(row < t - 1, _up(a, 1), 0.0)
        k = 1
        while k < t:
            acc_h = acc_h + acc_b * _up(acc_h, k)
            acc_b = acc_b * _up(acc_b, k)
            k *= 2
        lmb = acc_h
        lcar[...] = jnp.sum(jnp.where(row == 0, a * lmb, 0.0), axis=0, keepdims=True)
        da = lmb * hprev
        dxa = lmb * m * ig
        di = lmb * m * xa
        dm = lmb * ig * xa
        dla = da * a - dm * (a * a) * lax.rsqrt(m2)
        dr = dla * ((-LRU_C) * sp)
        lam = lam_ref[...]
        dsp = jnp.sum(dla * ((-LRU_C) * r), axis=0, keepdims=True)
        dlam_ref[...] += dsp * (-1.0 / (1.0 + jnp.exp(lam)))
        dpr = dr * r * (1.0 - r)
        dpi = di * ig * (1.0 - ig)
        dbr_ref[...] += jnp.sum(dpr, axis=0, keepdims=True)
        dbi_ref[...] += jnp.sum(dpi, axis=0, keepdims=True)
        dwr_ref[...] += _dot_tn(xa, dpr)
        dwi_ref[...] += _dot_tn(xa, dpi)
        dxa = dxa + _dot_nt(dpr, wr_ref[...]) + _dot_nt(dpi, wi_ref[...])
        dcb_ref[...] += jnp.sum(dxa, axis=0, keepdims=True)
        for k in range(CONV_WIDTH):
            dcw_ref[k:k + 1, :] += jnp.sum(dxa * sh[CONV_WIDTH - 1 - k], axis=0, keepdims=True)
        ext = jnp.concatenate([dxa, dxn[...]], axis=0)
        dza = cw_ref[3:4, :] * dxa
        for j in (1, 2, 3):
            dza = dza + cw_ref[3 - j:4 - j, :] * _up(ext, j)[:t]
        dza_ref[...] = dza.astype(dza_ref.dtype)
        dxn[...] = dxa[:SUBLANES]

    full = lambda shp: pl.BlockSpec(shp, lambda i: (0, 0))
    rev = lambda i: nt - 1 - i
    halo = lambda i: (jnp.maximum((nt - 1 - i) * hb - 1, 0), 0)
    outs = pl.pallas_call(
        body, name="lru_bwd", grid=(nt,),
        in_specs=[pl.BlockSpec((t, c), lambda i: (rev(i), 0)), pl.BlockSpec((SUBLANES, c), halo),
                  pl.BlockSpec((t, c), lambda i: (rev(i), 1)),
                  pl.BlockSpec((t, c), lambda i: (rev(i), 0)), pl.BlockSpec((SUBLANES, c), halo),
                  pl.BlockSpec((t, c), lambda i: (rev(i), 0)),
                  lay((CONV_WIDTH, c)), lay((1, c)), lay((c, c)), lay((1, c)), lay((c, c)), lay((1, c)),
                  lay((1, c))],
        out_specs=[pl.BlockSpec((t, c), lambda i: (rev(i), 0)), pl.BlockSpec((t, c), lambda i: (rev(i), 0)),
                   full((c, c)), full((c, c)), full((CONV_WIDTH, c)), full((1, c)), full((1, c)), full((1, c)),
                   full((1, c))],
        out_shape=[jax.ShapeDtypeStruct((s, c), MXU_DTYPE), jax.ShapeDtypeStruct((s, c), MXU_DTYPE),
                   jax.ShapeDtypeStruct((c, c), F32), jax.ShapeDtypeStruct((c, c), F32),
                   jax.ShapeDtypeStruct((CONV_WIDTH, c), F32)] + [jax.ShapeDtypeStruct((1, c), F32)] * 4,
        scratch_shapes=[pltpu.VMEM((1, c), F32), pltpu.VMEM((SUBLANES, c), F32)],
        compiler_params=_cp(("arbitrary",), 48),
    )(z, z, z, hs, hs, dy, cw, cb, wr, br, wi, bi, lam)
    return outs


POOL_HALO = 16


def _pool_select(lane, v2, v4, v8, v16):
    return jnp.where(lane < 64, v2, jnp.where(lane < 128, v4, jnp.where(lane < 192, v8, v16)))


def _pool_counts(t0, t, c):
    lane = _lanes((t, c))
    win = _pool_select(lane, 2.0, 4.0, 8.0, 16.0)
    seen = (t0 + _rows((t, c)) + 1).astype(F32)
    return lane, jnp.minimum(seen, win)


def _pooled(zc, halo, lane, cnt):
    t = zc.shape[0]
    ext = jnp.concatenate([halo, zc], axis=0)
    s2 = ext + _down(ext, 1)
    s4 = s2 + _down(s2, 2)
    s8 = s4 + _down(s4, 4)
    s16 = s8 + _down(s8, 8)
    cut = lambda v: v[POOL_HALO:POOL_HALO + t]
    return _pool_select(lane, cut(s2), cut(s4), cut(s8), cut(s16)) / cnt - zc


def _pool_fwd(z, wp, ps, l):
    s = z.shape[0]
    t = _tile(s)
    c = D_POOL
    lay = _lay(l)

    def body(zc_ref, gc_ref, wp_ref, ps_ref, yc_ref, zprev):
        i = pl.program_id(0)

        @pl.when(i == 0)
        def _():
            zprev[...] = jnp.zeros_like(zprev)

        zc = zc_ref[...]
        lane, cnt = _pool_counts(i * t, t, c)
        pooled = _pooled(zc, zprev[...], lane, cnt)
        pc = _dot(pooled, wp_ref[...])
        gc = gc_ref[...]
        yc_ref[...] = (pc * ps_ref[...] * (gc * _sig(gc))).astype(yc_ref.dtype)
        zprev[...] = zc_ref[t - POOL_HALO:t, :]

    return pl.pallas_call(
        body, name="pool_fwd", grid=(s // t,),
        in_specs=[pl.BlockSpec((t, c), lambda i: (i, 7)), pl.BlockSpec((t, c), lambda i: (i, 8)),
                  lay((c, c)), lay((1, c))],
        out_specs=pl.BlockSpec((t, c), lambda i: (i, 0)),
        out_shape=jax.ShapeDtypeStruct((s, c), MXU_DTYPE),
        scratch_shapes=[pltpu.VMEM((POOL_HALO, c), F32)],
        compiler_params=_cp(("arbitrary",), 32),
    )(z, z, wp, ps)


def _pool_bwd(z, dy, wp, ps, l):
    s = z.shape[0]
    t = _tile(s)
    lay = _lay(l)
    nt = s // t
    c = D_POOL
    hb = t // POOL_HALO

    def body(zc_ref, zh_ref, gc_ref, dy_ref, wp_ref, ps_ref, dzc_ref, dgc_ref, dwp_ref, dps_ref, ddn):
        i = pl.program_id(0)
        tt = nt - 1 - i

        @pl.when(i == 0)
        def _():
            ddn[...] = jnp.zeros_like(ddn)
            dwp_ref[...] = jnp.zeros_like(dwp_ref)
            dps_ref[...] = jnp.zeros_like(dps_ref)

        first = (tt > 0).astype(F32)
        zc = zc_ref[...]
        lane, cnt = _pool_counts(tt * t, t, c)
        pooled = _pooled(zc, zh_ref[...] * first, lane, cnt)
        pc = _dot(pooled, wp_ref[...])
        gc = gc_ref[...]
        sg = _sig(gc)
        silu = gc * sg
        dyc = dy_ref[...]
        ps_v = ps_ref[...]
        dgc_ref[...] = (dyc * pc * ps_v * (sg * (1.0 + gc * (1.0 - sg)))).astype(dgc_ref.dtype)
        dps_ref[...] += jnp.sum(dyc * pc * silu, axis=0, keepdims=True)
        dpc = dyc * ps_v * silu
        dwp_ref[...] += _dot_tn(pooled, dpc)
        dpooled = _dot_nt(dpc, wp_ref[...])
        dd = dpooled / cnt
        ext = jnp.concatenate([dd, ddn[...]], axis=0)
        f2 = ext + _up(ext, 1)
        f4 = f2 + _up(f2, 2)
        f8 = f4 + _up(f4, 4)
        f16 = f8 + _up(f8, 8)
        dzc = _pool_select(lane, f2[:t], f4[:t], f8[:t], f16[:t]) - dpooled
        dzc_ref[...] = dzc.astype(dzc_ref.dtype)
        ddn[...] = dd[:POOL_HALO]

    full = lambda shp: pl.BlockSpec(shp, lambda i: (0, 0))
    rev = lambda i: nt - 1 - i
    return pl.pallas_call(
        body, name="pool_bwd", grid=(nt,),
        in_specs=[pl.BlockSpec((t, c), lambda i: (rev(i), 7)),
                  pl.BlockSpec((POOL_HALO, c), lambda i: (jnp.maximum(rev(i) * hb - 1, 0), 7)),
                  pl.BlockSpec((t, c), lambda i: (rev(i), 8)),
                  pl.BlockSpec((t, c), lambda i: (rev(i), 0)),
                  lay((c, c)), lay((1, c))],
        out_specs=[pl.BlockSpec((t, c), lambda i: (rev(i), 0)), pl.BlockSpec((t, c), lambda i: (rev(i), 0)),
                   full((c, c)), full((1, c))],
        out_shape=[jax.ShapeDtypeStruct((s, c), MXU_DTYPE), jax.ShapeDtypeStruct((s, c), MXU_DTYPE),
                   jax.ShapeDtypeStruct((c, c), F32), jax.ShapeDtypeStruct((1, c), F32)],
        scratch_shapes=[pltpu.VMEM((POOL_HALO, c), F32)],
        compiler_params=_cp(("arbitrary",), 32),
    )(z, z, z, dy, wp, ps)


def _rope_tables(s):
    pos = jnp.arange(s, dtype=F32)
    inv_freq = ROPE_BASE ** (-jnp.arange(0, QK_ROPE, 2, dtype=F32) / QK_ROPE)
    ang = pos[:, None] * inv_freq[None, :]
    cos, sin = jnp.cos(ang), jnp.sin(ang)
    half = QK_ROPE // 2
    ones = jnp.ones((s, QK_NOPE), F32)
    z64 = jnp.zeros((s, QK_NOPE), F32)
    zh = jnp.zeros((s, half), F32)
    z32 = jnp.zeros((s, HEAD_BLK - QK_NOPE - QK_ROPE), F32)
    c_t = jnp.concatenate([ones, cos, cos, z32], axis=1)
    s1_t = jnp.concatenate([z64, -sin, zh, z32], axis=1)
    s2_t = jnp.concatenate([z64, zh, sin, z32], axis=1)
    return c_t, s1_t, s2_t


def _rope(x, c_t, s1_t, s2_t):
    return x * c_t + pltpu.roll(x, HEAD_BLK - 16, 1) * s1_t + pltpu.roll(x, 16, 1) * s2_t


def _unrope(d, c_t, s1_t, s2_t):
    return d * c_t + pltpu.roll(d * s1_t, 16, 1) + pltpu.roll(d * s2_t, HEAD_BLK - 16, 1)


def _mla_prep_fwd(z, gq, gkv, wuq, wukv, c_t, s1_t, s2_t, l):
    s = z.shape[0]
    t = _tile(s)
    hq = MLA_HEADS * HEAD_BLK
    lay = _lay(l)

    def body(cq_ref, ckv_ref, kr_ref, gq_ref, gkv_ref, wuq_ref, wukv_ref, c_ref, s1_ref, s2_ref,
             q_ref, k_ref, v_ref, qn_ref, kvn_ref):
        ct, s1, s2 = c_ref[...], s1_ref[...], s2_ref[...]
        cq = cq_ref[...]
        qn = (cq * lax.rsqrt(jnp.mean(cq * cq, axis=-1, keepdims=True) + EPS) * gq_ref[...]).astype(MXU_DTYPE)
        qn_ref[...] = qn
        q = jnp.dot(qn, wuq_ref[...], preferred_element_type=F32)
        ckv = ckv_ref[...]
        kvn = (ckv * lax.rsqrt(jnp.mean(ckv * ckv, axis=-1, keepdims=True) + EPS) * gkv_ref[...]).astype(MXU_DTYPE)
        kvn_ref[...] = kvn
        kvp = jnp.dot(kvn, wukv_ref[...], preferred_element_type=F32)
        krr = _rope(kr_ref[...], ct, s1, s2)
        for h in range(MLA_HEADS):
            blk = slice(h * HEAD_BLK, (h + 1) * HEAD_BLK)
            q_ref[:, blk] = (_rope(q[:, blk], ct, s1, s2) * Q_PRESCALE).astype(q_ref.dtype)
            k_ref[:, blk] = (kvp[:, blk] + krr).astype(k_ref.dtype)
        v_ref[...] = kvp[:, hq:].astype(v_ref.dtype)

    tab = pl.BlockSpec((t, HEAD_BLK), lambda i: (i, 0))
    return pl.pallas_call(
        body, name="mla_prep_fwd", grid=(s // t,),
        in_specs=[pl.BlockSpec((t, Q_RANK), lambda i: (i, 2)), pl.BlockSpec((t, KV_RANK), lambda i: (i, 6)),
                  pl.BlockSpec((t, HEAD_BLK), lambda i: (i, 18)),
                  lay((1, Q_RANK)), lay((1, KV_RANK)), lay((Q_RANK, hq)), lay((KV_RANK, hq + D_MLA)),
                  tab, tab, tab],
        out_specs=[pl.BlockSpec((t, hq), lambda i: (i, 0)), pl.BlockSpec((t, hq), lambda i: (i, 0)),
                   pl.BlockSpec((t, D_MLA), lambda i: (i, 0)), pl.BlockSpec((t, Q_RANK), lambda i: (i, 0)),
                   pl.BlockSpec((t, KV_RANK), lambda i: (i, 0))],
        out_shape=[jax.ShapeDtypeStruct((s, hq), MXU_DTYPE), jax.ShapeDtypeStruct((s, hq), MXU_DTYPE),
                   jax.ShapeDtypeStruct((s, D_MLA), MXU_DTYPE), jax.ShapeDtypeStruct((s, Q_RANK), MXU_DTYPE),
                   jax.ShapeDtypeStruct((s, KV_RANK), MXU_DTYPE)],
        compiler_params=_cp(("parallel",), 40),
    )(z, z, z, gq, gkv, wuq, wukv, c_t, s1_t, s2_t)


SUM_LANE_A = V_DIM
SUM_LANE_B = 0
FWD_TILES_PER_TRIP = 4


def _flash_fwd(q, k, v, z):
    s = q.shape[0]
    t = _tile(s)
    nq = s // t
    pw = 2 * HEAD_BLK

    def body(q_ref, k_ref, v_ref, gb_ref, o_ref, yb_ref, lse_ref):
        i = pl.program_id(1)
        qv = q_ref[...]
        qa, qb = qv[:, :HEAD_BLK], qv[:, HEAD_BLK:]
        lane = _lanes((t, HEAD_BLK))
        lo = lane < V_DIM

        def update(qh, kh, vh, m, acc, masked):
            sc = _dot_nt(qh, kh)
            if masked:
                sc = jnp.where(_lanes((t, t)) <= _rows((t, t)), sc, -1e30)
            m_new = jnp.maximum(m, jnp.max(sc, axis=-1, keepdims=True))
            p = jnp.exp2(sc - m_new).astype(MXU_DTYPE)
            return m_new, acc * jnp.exp2(m - m_new) + _dot(p, vh)

        def step(j, carry, masked):
            ma, mb, acc_a, acc_b = carry
            kv_rows = pl.ds(pl.multiple_of(j * t, t), t)
            kt = k_ref[kv_rows, :]
            vt = v_ref[kv_rows, :]
            lane_v = _lanes(vt.shape)
            one = jnp.ones_like(vt)
            zero_v = jnp.zeros_like(vt)
            v_a = jnp.where(lane_v < V_DIM, vt, jnp.where(lane_v == SUM_LANE_A, one, zero_v))
            v_b = jnp.where(lane_v >= V_DIM, vt, jnp.where(lane_v == SUM_LANE_B, one, zero_v))
            ma, acc_a = update(qa, kt[:, :HEAD_BLK], v_a, ma, acc_a, masked)
            mb, acc_b = update(qb, kt[:, HEAD_BLK:], v_b, mb, acc_b, masked)
            return ma, mb, acc_a, acc_b

        neg = jnp.full((t, 1), -1e30, F32)
        zero = jnp.zeros((t, HEAD_BLK), F32)
        def trip(jj, cr):
            for u in range(FWD_TILES_PER_TRIP):
                cr = step(FWD_TILES_PER_TRIP * jj + u, cr, False)
            return cr

        whole = i // FWD_TILES_PER_TRIP
        carry = lax.fori_loop(0, whole, trip, (neg, neg, zero, zero))
        carry = lax.fori_loop(FWD_TILES_PER_TRIP * whole, i, lambda j, cr: step(j, cr, False), carry)
        ma, mb, acc_a, acc_b = step(i, carry, True)
        la = jnp.sum(jnp.where(lane == SUM_LANE_A, acc_a, 0.0), axis=-1, keepdims=True)
        lb = jnp.sum(jnp.where(lane == SUM_LANE_B, acc_b, 0.0), axis=-1, keepdims=True)
        o = jnp.where(lo, acc_a * (1.0 / la), acc_b * (1.0 / lb))
        o_ref[...] = o
        gb = gb_ref[...]
        yb_ref[...] = (o * (gb * _sig(gb))).astype(yb_ref.dtype)
        lse = jnp.where(lo, ma + jnp.log(la) * LOG2_E, mb + jnp.log(lb) * LOG2_E)
        pick = ((_rows((SUBLANES, HEAD_BLK)) == 0) & (_lanes((SUBLANES, HEAD_BLK)) == 0)) | (
            (_rows((SUBLANES, HEAD_BLK)) == 1) & (_lanes((SUBLANES, HEAD_BLK)) == V_DIM))
        lse_ref[0, 0] = lax.dot_general(pick.astype(F32), lse, (((1,), (1,)), ((), ())),
                                        precision=lax.Precision.HIGHEST, preferred_element_type=F32)

    return pl.pallas_call(
        body, name="flash_fwd", grid=(N_PAIR, nq),
        in_specs=[pl.BlockSpec((t, pw), lambda p, i: (i, p)), pl.BlockSpec((s, pw), lambda p, i: (0, p)),
                  pl.BlockSpec((s, HEAD_BLK), lambda p, i: (0, p)),
                  pl.BlockSpec((t, HEAD_BLK), lambda p, i: (i, 9 + p))],
        out_specs=[pl.BlockSpec((t, HEAD_BLK), lambda p, i: (i, p)), pl.BlockSpec((t, HEAD_BLK), lambda p, i: (i, p)),
                   pl.BlockSpec((1, 1, SUBLANES, t), lambda p, i: (p, i, 0, 0))],
        out_shape=[jax.ShapeDtypeStruct((s, D_MLA), F32), jax.ShapeDtypeStruct((s, D_MLA), MXU_DTYPE),
                   jax.ShapeDtypeStruct((N_PAIR, nq, SUBLANES, t), F32)],
        compiler_params=_cp(("parallel", "parallel"), 48),
    )(q, k, v, z)


def _flash_bwd(q, k, v, do, lse, delta):
    s = q.shape[0]
    t = _tile(s)
    nq = s // t
    pw = 2 * HEAD_BLK

    def body(q_ref, do_ref, lse_ref, dl_ref, k_ref, v_ref, dq_ref, dk_ref, dv_ref):
        j = pl.program_id(1)

        @pl.when(j == 0)
        def _():
            dq_ref[...] = jnp.zeros_like(dq_ref)

        kt = k_ref[...]
        ka, kb = kt[:, :HEAD_BLK], kt[:, HEAD_BLK:]
        vt = v_ref[...]

        def head(kh, qh, do_h, lse_row, dl_row, masked):
            st = _dot_nt(kh, qh)
            if masked:
                st = jnp.where(_rows((t, t)) <= _lanes((t, t)), st, -1e30)
            pt = jnp.exp2(st - lse_row)
            dv_h = _dot(pt, do_h)
            dst = (pt * (_dot_nt(vt, do_h) - dl_row)).astype(MXU_DTYPE)
            return dv_h, _dot(dst, qh), _dot_tn(dst, kh)

        def step(i, carry, masked):
            dka, dkb, dv = carry
            q_rows = pl.ds(pl.multiple_of(i * t, t), t)
            qv = q_ref[q_rows, :]
            dov = do_ref[q_rows, :]
            lane = _lanes(dov.shape)
            do_lo = jnp.where(lane < V_DIM, dov, jnp.zeros_like(dov))
            do_hi = jnp.where(lane >= V_DIM, dov, jnp.zeros_like(dov))
            dva, dk_a, dq_a = head(ka, qv[:, :HEAD_BLK], do_lo, lse_ref[0, i, 0:1, :], dl_ref[0, i, 0:1, :], masked)
            dvb, dk_b, dq_b = head(kb, qv[:, HEAD_BLK:], do_hi, lse_ref[0, i, 1:2, :], dl_ref[0, i, 1:2, :], masked)
            dq_ref[q_rows, 0:HEAD_BLK] += dq_a
            dq_ref[q_rows, HEAD_BLK:pw] += dq_b
            return dka + dk_a, dkb + dk_b, dv + dva + dvb

        zero = jnp.zeros((t, HEAD_BLK), F32)
        carry = step(j, (zero, zero, zero), True)
        rest = nq - 1 - j
        carry = lax.cond(rest % 2 == 1, lambda cr: step(j + 1, cr, False), lambda cr: cr, carry)
        first = j + 1 + rest % 2
        dka, dkb, dv = lax.fori_loop(
            0, rest // 2, lambda ii, cr: step(first + 2 * ii + 1, step(first + 2 * ii, cr, False), False), carry)
        dk_ref[:, 0:HEAD_BLK] = dka * LN_2
        dk_ref[:, HEAD_BLK:pw] = dkb * LN_2
        dv_ref[...] = dv.astype(dv_ref.dtype)

    return pl.pallas_call(
        body, name="flash_bwd", grid=(N_PAIR, nq),
        in_specs=[pl.BlockSpec((s, pw), lambda p, j: (0, p)), pl.BlockSpec((s, HEAD_BLK), lambda p, j: (0, p)),
                  pl.BlockSpec((1, nq, SUBLANES, t), lambda p, j: (p, 0, 0, 0)),
                  pl.BlockSpec((1, nq, SUBLANES, t), lambda p, j: (p, 0, 0, 0)),
                  pl.BlockSpec((t, pw), lambda p, j: (j, p)), pl.BlockSpec((t, HEAD_BLK), lambda p, j: (j, p))],
        out_specs=[pl.BlockSpec((s, pw), lambda p, j: (0, p)), pl.BlockSpec((t, pw), lambda p, j: (j, p)),
                   pl.BlockSpec((t, HEAD_BLK), lambda p, j: (j, p))],
        out_shape=[jax.ShapeDtypeStruct((s, MLA_HEADS * HEAD_BLK), F32),
                   jax.ShapeDtypeStruct((s, MLA_HEADS * HEAD_BLK), F32),
                   jax.ShapeDtypeStruct((s, D_MLA), MXU_DTYPE)],
        compiler_params=_cp(("parallel", "arbitrary"), 56),
    )(q, do, lse, delta, k, v)


def _mla_prep_bwd(dq, dk, dv, z, qn, kvn, gq, gkv, wuq, wukv, c_t, s1_t, s2_t, l):
    s = z.shape[0]
    t = _tile(s)
    hq = MLA_HEADS * HEAD_BLK
    lay = _lay(l)

    def body(dq_ref, dk_ref, dv_ref, cq_ref, ckv_ref, qn_ref, kvn_ref, gq_ref, gkv_ref, wuq_ref, wukv_ref,
             c_ref, s1_ref, s2_ref, dcq_ref, dckv_ref, dkr_ref, dwuq_ref, dwukv_ref, dgq_ref, dgkv_ref,
             dqu_ref, dkvp_ref):
        @pl.when(pl.program_id(0) == 0)
        def _():
            for ref in (dwuq_ref, dwukv_ref, dgq_ref, dgkv_ref):
                ref[...] = jnp.zeros_like(ref)

        ct, s1, s2 = c_ref[...], s1_ref[...], s2_ref[...]
        dk_sum = jnp.zeros((t, HEAD_BLK), F32)
        for h in range(MLA_HEADS):
            blk = slice(h * HEAD_BLK, (h + 1) * HEAD_BLK)
            dqu_ref[:, blk] = _unrope(dq_ref[:, blk] * ATT_SCALE, ct, s1, s2).astype(dqu_ref.dtype)
            dkh = dk_ref[:, blk]
            dk_sum = dk_sum + dkh
            dkvp_ref[:, blk] = dkh.astype(dkvp_ref.dtype)
        dkvp_ref[:, hq:] = dv_ref[...]
        lane = _lanes((t, HEAD_BLK))
        rope_lanes = (lane >= KR_LANE0) & (lane < KR_LANE0 + QK_ROPE)
        dkr_ref[...] = _unrope(jnp.where(rope_lanes, dk_sum, 0.0), ct, s1, s2).astype(dkr_ref.dtype)

        def norm_bwd(c_in, g, dn_out, dc_ref, dg_ref):
            rs = lax.rsqrt(jnp.mean(c_in * c_in, axis=-1, keepdims=True) + EPS)
            n = c_in * rs
            dg_ref[...] += jnp.sum(dn_out * n, axis=0, keepdims=True)
            dn = dn_out * g
            dc_ref[...] = (rs * (dn - n * jnp.mean(dn * n, axis=-1, keepdims=True))).astype(dc_ref.dtype)

        dqu, dkvp = dqu_ref[...], dkvp_ref[...]
        dwuq_ref[...] += _dot_tn(qn_ref[...], dqu)
        dwukv_ref[...] += _dot_tn(kvn_ref[...], dkvp)
        norm_bwd(cq_ref[...], gq_ref[...], _dot_nt(dqu, wuq_ref[...]), dcq_ref, dgq_ref)
        norm_bwd(ckv_ref[...], gkv_ref[...], _dot_nt(dkvp, wukv_ref[...]), dckv_ref, dgkv_ref)

    full = lambda shp: pl.BlockSpec(shp, lambda i: (0, 0))
    tab = pl.BlockSpec((t, HEAD_BLK), lambda i: (i, 0))
    row = lambda w: pl.BlockSpec((t, w), lambda i: (i, 0))
    return pl.pallas_call(
        body, name="mla_prep_bwd", grid=(s // t,),
        in_specs=[row(hq), row(hq), row(D_MLA),
                  pl.BlockSpec((t, Q_RANK), lambda i: (i, 2)), pl.BlockSpec((t, KV_RANK), lambda i: (i, 6)),
                  row(Q_RANK), row(KV_RANK),
                  lay((1, Q_RANK)), lay((1, KV_RANK)), lay((Q_RANK, hq)), lay((KV_RANK, hq + D_MLA)),
                  tab, tab, tab],
        out_specs=[row(Q_RANK), row(KV_RANK), row(HEAD_BLK), full((Q_RANK, hq)), full((KV_RANK, hq + D_MLA)),
                   full((1, Q_RANK)), full((1, KV_RANK))],
        out_shape=[jax.ShapeDtypeStruct((s, Q_RANK), MXU_DTYPE), jax.ShapeDtypeStruct((s, KV_RANK), MXU_DTYPE),
                   jax.ShapeDtypeStruct((s, HEAD_BLK), MXU_DTYPE),
                   jax.ShapeDtypeStruct((Q_RANK, hq), F32), jax.ShapeDtypeStruct((KV_RANK, hq + D_MLA), F32),
                   jax.ShapeDtypeStruct((1, Q_RANK), F32), jax.ShapeDtypeStruct((1, KV_RANK), F32)],
        scratch_shapes=[pltpu.VMEM((t, hq), MXU_DTYPE), pltpu.VMEM((t, hq + D_MLA), MXU_DTYPE)],
        compiler_params=_cp(("arbitrary",), 48),
    )(dq, dk, dv, z, z, qn, kvn, gq, gkv, wuq, wukv, c_t, s1_t, s2_t)


def _outproj_fwd(x, ya, yb, yc, w, l):
    s = x.shape[0]
    t = _tile(s)
    lay = _lay(l)

    def body(x_ref, ya_ref, yb_ref, yc_ref, wa_ref, wb_ref, wc_ref, o_ref):
        o_ref[...] = (x_ref[...] + _dot(ya_ref[...], wa_ref[...]) + _dot(yb_ref[...], wb_ref[...])
                      + _dot(yc_ref[...], wc_ref[...]))

    row = lambda w_: pl.BlockSpec((t, w_), lambda i: (i, 0))
    return pl.pallas_call(
        body, name="outproj_fwd", grid=(s // t,),
        in_specs=[row(D_MODEL), row(D_LRU), row(D_MLA), row(D_POOL),
                  lay((D_LRU, D_MODEL), 0), lay((D_MLA, D_MODEL), 1), lay((D_POOL, D_MODEL), 3)],
        out_specs=row(D_MODEL),
        out_shape=jax.ShapeDtypeStruct((s, D_MODEL), F32),
        compiler_params=_cp(("parallel",), 40),
    )(x, ya, yb, yc, w, w, w)


def _outproj_bwd(dx, ya, yb, yc, o, z, w, l):
    s = dx.shape[0]
    t = _tile(s)
    nq = s // t
    rows = N_PAIR * SUBLANES

    def body(dx_ref, ya_ref, yb_ref, yc_ref, o_ref, gb_ref, w_ref, dya_ref, dyc_ref, do_ref, dgb_ref, dl_ref, dw_ref):
        @pl.when(pl.program_id(0) == 0)
        def _():
            dw_ref[...] = jnp.zeros_like(dw_ref)

        dxv = dx_ref[...].astype(MXU_DTYPE)
        dy = _dot_nt(dxv, w_ref[...])
        dw_ref[0:D_LRU, :] += _dot_tn(ya_ref[...], dxv)
        dw_ref[D_LRU:D_LRU + D_MLA, :] += _dot_tn(yb_ref[...], dxv)
        dw_ref[D_LRU + D_MLA:D_MIX, :] += _dot_tn(yc_ref[...], dxv)
        dya_ref[...] = dy[:, :D_LRU]
        dyc_ref[...] = dy[:, D_LRU + D_MLA:]
        dyb = dy[:, D_LRU:D_LRU + D_MLA]
        gb = gb_ref[...]
        sg = _sig(gb)
        ov = o_ref[...]
        do = dyb * (gb * sg)
        do_ref[...] = do.astype(do_ref.dtype)
        dgb_ref[...] = (dyb * ov * (sg * (1.0 + gb * (1.0 - sg)))).astype(dgb_ref.dtype)
        r = _rows((rows, D_MLA))
        head = (r // SUBLANES) * 2 + (r % SUBLANES)
        sel = ((r % SUBLANES) < 2) & (_lanes((rows, D_MLA)) // V_DIM == head)
        dl = lax.dot_general(sel.astype(F32), do * ov, (((1,), (1,)), ((), ())),
                             precision=lax.Precision.HIGHEST, preferred_element_type=F32)
        for p in range(N_PAIR):
            dl_ref[p, 0] = dl[p * SUBLANES:(p + 1) * SUBLANES]

    row = lambda w_: pl.BlockSpec((t, w_), lambda i: (i, 0))
    return pl.pallas_call(
        body, name="outproj_bwd", grid=(nq,),
        in_specs=[row(D_MODEL), row(D_LRU), row(D_MLA), row(D_POOL), row(D_MLA),
                  pl.BlockSpec((t, D_MLA), lambda i: (i, 3)), _lay(l)((D_MIX, D_MODEL))],
        out_specs=[row(D_LRU), row(D_POOL), row(D_MLA), row(D_MLA),
                   pl.BlockSpec((N_PAIR, 1, SUBLANES, t), lambda i: (0, i, 0, 0)),
                   pl.BlockSpec((D_MIX, D_MODEL), lambda i: (0, 0))],
        out_shape=[jax.ShapeDtypeStruct((s, D_LRU), F32), jax.ShapeDtypeStruct((s, D_POOL), F32),
                   jax.ShapeDtypeStruct((s, D_MLA), MXU_DTYPE), jax.ShapeDtypeStruct((s, D_MLA), MXU_DTYPE),
                   jax.ShapeDtypeStruct((N_PAIR, nq, SUBLANES, t), F32),
                   jax.ShapeDtypeStruct((D_MIX, D_MODEL), F32)],
        compiler_params=_cp(("arbitrary",), 48),
    )(dx, ya, yb, yc, o, z, w)


DZ_WIDTHS = (D_LRU, D_LRU, Q_RANK, D_MLA, KV_RANK, D_POOL, D_POOL, HEAD_BLK)


def _dwin(h, dz_parts):
    s = h.shape[0]
    t = _tile(s)

    def body(h_ref, *refs):
        o_ref = refs[-1]

        @pl.when(pl.program_id(0) == 0)
        def _():
            o_ref[...] = jnp.zeros_like(o_ref)

        o_ref[...] += _dot_tn(h_ref[...], jnp.concatenate([r[...] for r in refs[:-1]], axis=1))

    row = lambda w_: pl.BlockSpec((t, w_), lambda i: (i, 0))
    return pl.pallas_call(
        body, name="dwin", grid=(s // t,),
        in_specs=[row(D_MODEL)] + [row(wd) for wd in DZ_WIDTHS],
        out_specs=pl.BlockSpec((D_MODEL, D_INP), lambda i: (0, 0)),
        out_shape=jax.ShapeDtypeStruct((D_MODEL, D_INP), F32),
        compiler_params=_cp(("arbitrary",), 56),
    )(h, *dz_parts)


def _inproj_bwd(dz_parts, w, x, g, dxn, l):
    s = x.shape[0]
    t = _tile(s)
    lay = _lay(l)
    n_parts = len(DZ_WIDTHS)

    def body(*refs):
        part_refs = refs[:n_parts]
        w_ref, x_ref, g_ref, dxn_ref, dx_ref, dg_ref = refs[n_parts:]

        @pl.when(pl.program_id(0) == 0)
        def _():
            dg_ref[...] = jnp.zeros_like(dg_ref)

        dh = _dot_nt(jnp.concatenate([r[...] for r in part_refs], axis=1), w_ref[...])
        xv = x_ref[...]
        rs = lax.rsqrt(jnp.mean(xv * xv, axis=-1, keepdims=True) + EPS)
        n = xv * rs
        dg_ref[...] += jnp.sum(dh * n, axis=0, keepdims=True)
        dn = dh * g_ref[...]
        dx_ref[...] = dxn_ref[...] + rs * (dn - n * jnp.mean(dn * n, axis=-1, keepdims=True))

    row = lambda w_: pl.BlockSpec((t, w_), lambda i: (i, 0))
    return pl.pallas_call(
        body, name="inproj_bwd", grid=(s // t,),
        in_specs=[row(wd) for wd in DZ_WIDTHS] + [lay((D_MODEL, D_INP)), row(D_MODEL), lay((1, D_MODEL)),
                                                  row(D_MODEL)],
        out_specs=[row(D_MODEL), pl.BlockSpec((1, D_MODEL), lambda i: (0, 0))],
        out_shape=[jax.ShapeDtypeStruct((s, D_MODEL), F32), jax.ShapeDtypeStruct((1, D_MODEL), F32)],
        compiler_params=_cp(("arbitrary",), 48),
    )(*dz_parts, w, x, g, dxn)


def _loss_head(x, g, tgt):
    s = x.shape[0]
    t = _tile(s)

    def body(x_ref, g_ref, t_ref, dx_ref, loss_ref, dg_ref):
        @pl.when(pl.program_id(0) == 0)
        def _():
            loss_ref[...] = jnp.zeros_like(loss_ref)
            dg_ref[...] = jnp.zeros_like(dg_ref)

        xv = x_ref[...]
        rs = lax.rsqrt(jnp.mean(xv * xv, axis=-1, keepdims=True) + EPS)
        n = xv * rs
        gv = g_ref[...]
        e = n * gv - t_ref[...]
        loss_ref[...] += 0.5 * jnp.sum(jnp.mean(e * e, axis=-1, keepdims=True))
        dyf = e * (1.0 / D_MODEL)
        dg_ref[...] += jnp.sum(dyf * n, axis=0, keepdims=True)
        dn = dyf * gv
        dx_ref[...] = rs * (dn - n * jnp.mean(dn * n, axis=-1, keepdims=True))

    row = pl.BlockSpec((t, D_MODEL), lambda i: (i, 0))
    vec = pl.BlockSpec((1, D_MODEL), lambda i: (0, 0))
    return pl.pallas_call(
        body, name="loss_head", grid=(s // t,),
        in_specs=[row, vec, row],
        out_specs=[row, pl.BlockSpec((1, LANES), lambda i: (0, 0)), vec],
        out_shape=[jax.ShapeDtypeStruct((s, D_MODEL), F32), jax.ShapeDtypeStruct((1, LANES), F32),
                   jax.ShapeDtypeStruct((1, D_MODEL), F32)],
        compiler_params=_cp(("arbitrary",), 32),
    )(x, g, tgt)


def _block_diag(w):
    n, h, d, _ = w.shape
    return jnp.einsum('lhij,hk->lhikj', w, jnp.eye(h, dtype=w.dtype)).reshape(n, h * d, h * d)


def _diag_blocks(wfull, h):
    d = wfull.shape[-1] // h
    return jnp.stack([wfull[:, i * d:(i + 1) * d, i * d:(i + 1) * d] for i in range(h)], axis=1)


REF_TO_PERM = np.concatenate([np.arange(0, 1152), np.arange(1536, 1792),
                              np.arange(2304 + KR_LANE0, 2304 + KR_LANE0 + QK_ROPE),
                              np.arange(1152, 1536), np.arange(1792, 2304)])
N_SHARD = 4
W_IN_SHARD = D_IN // N_SHARD


def _w_in_runs():
    ref_of_perm = -np.ones(D_INP, np.int64)
    ref_of_perm[REF_TO_PERM] = np.arange(D_IN)
    perm_runs, p = [], 0
    while p < D_INP:
        r, q = ref_of_perm[p], p + 1
        if r < 0:
            while q < D_INP and ref_of_perm[q] < 0:
                q += 1
            perm_runs.append((None, q - p, 0))
        else:
            while (q < D_INP and ref_of_perm[q] == ref_of_perm[q - 1] + 1
                   and ref_of_perm[q] // W_IN_SHARD == r // W_IN_SHARD):
                q += 1
            perm_runs.append((int(r // W_IN_SHARD), int(r % W_IN_SHARD), int(r % W_IN_SHARD + q - p)))
        p = q
    shard_runs = []
    for s in range(N_SHARD):
        cols = REF_TO_PERM[s * W_IN_SHARD:(s + 1) * W_IN_SHARD]
        runs, a = [], 0
        for b in range(1, W_IN_SHARD + 1):
            if b == W_IN_SHARD or cols[b] != cols[b - 1] + 1:
                runs.append((int(cols[a]), int(cols[b - 1]) + 1))
                a = b
        shard_runs.append(runs)
    return perm_runs, shard_runs


def _permute_w_in(shards):
    perm_runs, _ = _w_in_runs()
    lead = shards[0].shape[:-1]
    parts = [jnp.zeros(lead + (a,), shards[0].dtype) if s is None else shards[s][..., a:b] for s, a, b in perm_runs]
    return jnp.concatenate(parts, axis=-1)


def _w_in_shard(wp, s):
    _, shard_runs = _w_in_runs()
    return jnp.concatenate([wp[..., a:b] for a, b in shard_runs[s]], axis=-1)


def _pad_w_uq(w):
    w4 = w.reshape(w.shape[:2] + (MLA_HEADS, QK_NOPE + QK_ROPE))
    return jnp.pad(w4, ((0, 0),) * 3 + ((0, HEAD_BLK - QK_NOPE - QK_ROPE),)).reshape(w.shape[:2] + (-1,))


def _unpad_w_uq(w):
    return w.reshape(w.shape[:2] + (MLA_HEADS, HEAD_BLK))[..., :QK_NOPE + QK_ROPE].reshape(w.shape[:2] + (-1,))


def _pad_w_ukv(w):
    w4 = w.reshape(w.shape[:2] + (MLA_HEADS, QK_NOPE + V_DIM))
    kpart = jnp.pad(w4[..., :QK_NOPE], ((0, 0),) * 3 + ((0, HEAD_BLK - QK_NOPE),)).reshape(w.shape[:2] + (-1,))
    return jnp.concatenate([kpart, w4[..., QK_NOPE:].reshape(w.shape[:2] + (-1,))], axis=2)


def _unpad_w_ukv(w):
    hq = MLA_HEADS * HEAD_BLK
    kpart = w[..., :hq].reshape(w.shape[:2] + (MLA_HEADS, HEAD_BLK))[..., :QK_NOPE]
    vpart = w[..., hq:].reshape(w.shape[:2] + (MLA_HEADS, V_DIM))
    return jnp.concatenate([kpart, vpart], axis=3).reshape(w.shape[:2] + (-1,))


def _local_step(x, tgt, w):
    s = x.shape[0]
    tabs = _rope_tables(s)
    vec = lambda a: a[:, None, :]
    mxu = lambda a: a.astype(MXU_DTYPE)
    p = dict(g=vec(w['norm_g']), w_in=mxu(w['w_in']), cw=w['conv_w'], cb=vec(w['conv_b']),
             wr=mxu(_block_diag(w['w_rg'])), br=vec(w['b_rg']), wi=mxu(_block_diag(w['w_ig'])), bi=vec(w['b_ig']),
             lam=vec(w['lru_lambda']), gq=vec(w['q_norm_g']), gkv=vec(w['kv_norm_g']),
             wuq=mxu(_pad_w_uq(w['w_uq'])), wukv=mxu(_pad_w_ukv(w['w_ukv'])),
             wp=mxu(_block_diag(w['w_pool'])), ps=vec(w['pool_scale']), wout=mxu(w['w_out']))
    lru = lambda l: (p['cw'], p['cb'], p['wr'], p['br'], p['wi'], p['bi'], p['lam'], l)
    mla = lambda l: (p['gq'], p['gkv'], p['wuq'], p['wukv'], *tabs, l)

    saved = []
    for l in range(DEPTH):
        h, z = _inproj_fwd(x, p['g'], p['w_in'], l)
        ya, hs = _lru_fwd(z, *lru(l))
        yc = _pool_fwd(z, p['wp'], p['ps'], l)
        q, k, v, qn, kvn = _mla_prep_fwd(z, *mla(l))
        o, yb, lse = _flash_fwd(q, k, v, z)
        saved.append(dict(x=x, h=h, z=z, hs=hs, ya=ya, yb=yb, yc=yc, q=q, k=k, v=v, qn=qn, kvn=kvn, o=o, lse=lse))
        x = _outproj_fwd(x, ya, yb, yc, p['wout'], l)

    dx, loss, dgf = _loss_head(x, w['final_norm_g'][None], tgt)
    per_layer = {n: [None] * DEPTH for n in WEIGHT_NAMES if n != 'final_norm_g'}
    for l in reversed(range(DEPTH)):
        sv = saved[l]
        dya, dyc, do, dgb, delta, per_layer['w_out'][l] = _outproj_bwd(
            dx, sv['ya'], sv['yb'], sv['yc'], sv['o'], sv['z'], p['wout'], l)
        dza, dga, *lru_grads = _lru_bwd(sv['z'], sv['hs'], dya, *lru(l))
        for n, g in zip(('w_rg', 'w_ig', 'conv_w', 'conv_b', 'b_rg', 'b_ig', 'lru_lambda'), lru_grads):
            per_layer[n][l] = g
        dzc, dgc, per_layer['w_pool'][l], per_layer['pool_scale'][l] = _pool_bwd(sv['z'], dyc, p['wp'], p['ps'], l)
        dq, dk, dv = _flash_bwd(sv['q'], sv['k'], sv['v'], do, sv['lse'], delta)
        (dcq, dckv, dkr, per_layer['w_uq'][l], per_layer['w_ukv'][l], per_layer['q_norm_g'][l],
         per_layer['kv_norm_g'][l]) = _mla_prep_bwd(dq, dk, dv, sv['z'], sv['qn'], sv['kvn'], *mla(l))
        dz_parts = (dza, dga, dcq, dgb, dckv, dzc, dgc, dkr)
        per_layer['w_in'][l] = _dwin(sv['h'], dz_parts)
        dx, per_layer['norm_g'][l] = _inproj_bwd(dz_parts, p['w_in'], sv['x'], p['g'], dx, l)
    grads = {n: jnp.stack(g) for n, g in per_layer.items()}
    for n in ('norm_g', 'conv_b', 'b_rg', 'b_ig', 'lru_lambda', 'q_norm_g', 'kv_norm_g', 'pool_scale'):
        grads[n] = grads[n][:, 0, :]
    grads['w_rg'] = _diag_blocks(grads['w_rg'], LRU_HEADS)
    grads['w_ig'] = _diag_blocks(grads['w_ig'], LRU_HEADS)
    grads['w_pool'] = _diag_blocks(grads['w_pool'], len(POOL_WINDOWS))
    grads['w_uq'] = _unpad_w_uq(grads['w_uq'])
    grads['w_ukv'] = _unpad_w_ukv(grads['w_ukv'])
    grads['final_norm_g'] = dgf[0]
    return loss[0, 0], dx, grads


WIRE_DTYPE = jnp.bfloat16
MESH_IDS = pl.DeviceIdType.MESH
_HBM = pl.BlockSpec(memory_space=pltpu.HBM)


def _coords():
    return lax.axis_index("x"), lax.axis_index("y"), lax.axis_index("c")


def _comm_call(body, name, arrays, out_shapes, copies_per_array):
    n = len(arrays)
    return pl.pallas_call(
        body, name=name, out_shape=out_shapes, in_specs=[_HBM] * n, out_specs=[_HBM] * n,
        scratch_shapes=[pltpu.SemaphoreType.DMA((n, copies_per_array)), pltpu.SemaphoreType.DMA((n, copies_per_array))],
    )(*arrays)


def _all_gather8(blocks, name):
    n = len(blocks)
    every = range(n)

    def body(*refs):
        x_refs, out_refs = refs[:n], refs[n:2 * n]
        send_sems, recv_sems = refs[2 * n:]
        x, y, c = _coords()
        me, sibling = (x, y, c), (x, y, 1 - c)
        chips = [(1 - x, y), (x, 1 - y), (1 - x, 1 - y)]

        def slot(t, px, py, pc):
            return out_refs[t].at[4 * px + 2 * py + pc]

        def copy(t, k, block, to, own=False):
            return pltpu.make_async_remote_copy(
                src_ref=x_refs[t] if own else slot(t, *block), dst_ref=slot(t, *block),
                send_sem=send_sems.at[t, k], recv_sem=recv_sems.at[t, k], device_id=to, device_id_type=MESH_IDS)

        first = [copy(t, 0, me, sibling, own=True) for t in every]
        first += [copy(t, 1 + j, me, (*chip, c), own=True) for j, chip in enumerate(chips) for t in every]
        for cp in first:
            cp.start()
        passed = [[copy(t, 4 + j, (*chip, c), sibling) for t in every] for j, chip in enumerate(chips)]
        for j, chip in enumerate(chips):
            for t in every:
                copy(t, 1 + j, (*chip, c), me).wait_recv()
                passed[j][t].start()
        for t in every:
            copy(t, 0, sibling, me).wait_recv()
        for j, chip in enumerate(chips):
            for t in every:
                copy(t, 4 + j, (*chip, 1 - c), me).wait_recv()
        for cp in first + [cp for group in passed for cp in group]:
            cp.wait_send()

    outs = [jax.ShapeDtypeStruct((N_DEV,) + b.shape, b.dtype) for b in blocks]
    got = _comm_call(body, name, blocks, outs, 7)
    me = 4 * lax.axis_index("x") + 2 * lax.axis_index("y") + lax.axis_index("c")
    return [lax.dynamic_update_index_in_dim(g, b, me, 0) for g, b in zip(got, blocks)]


def _sibling_send(arrs, name, which_half=False):
    n = len(arrs)

    def body(*refs):
        a_refs, out_refs = refs[:n], refs[n:2 * n]
        send_sems, recv_sems = refs[2 * n:]
        x, y, c = _coords()
        sent = [pltpu.make_async_remote_copy(
            src_ref=a_refs[t].at[1 - c] if which_half else a_refs[t], dst_ref=out_refs[t],
            send_sem=send_sems.at[t, 0], recv_sem=recv_sems.at[t, 0],
            device_id=(x, y, 1 - c), device_id_type=MESH_IDS) for t in range(n)]
        for cp in sent:
            cp.start()
        for cp in sent:
            cp.wait()

    shapes = [jax.ShapeDtypeStruct(a.shape[1:] if which_half else a.shape, a.dtype) for a in arrs]
    return _comm_call(body, name, arrs, shapes, 1)


def _chip_exchange(arrs, name):
    n = len(arrs)

    def body(*refs):
        a_refs, out_refs = refs[:n], refs[n:2 * n]
        send_sems, recv_sems = refs[2 * n:]
        x, y, c = _coords()
        copies = []
        for j, (cx, cy) in enumerate([(1 - x, y), (x, 1 - y), (1 - x, 1 - y)]):
            copies += [pltpu.make_async_remote_copy(
                src_ref=a_refs[t].at[2 * cx + cy], dst_ref=out_refs[t].at[j], send_sem=send_sems.at[t, j],
                recv_sem=recv_sems.at[t, j], device_id=(cx, cy, c), device_id_type=MESH_IDS) for t in range(n)]
        for cp in copies:
            cp.start()
        for cp in copies:
            cp.wait()

    return _comm_call(body, name, arrs, [jax.ShapeDtypeStruct((3,) + a.shape[1:], a.dtype) for a in arrs], 3)


def _sum_leading(groups, out_dtype, steps, name):
    flat = [a for g in groups for a in g]

    def body(*refs):
        ins, outs, pos = refs[:len(flat)], refs[len(flat):], 0
        for g, o_ref in zip(groups, outs):
            acc = None
            for i_ref in ins[pos:pos + len(g)]:
                for k in range(i_ref.shape[0]):
                    term = i_ref[k].astype(F32)
                    acc = term if acc is None else acc + term
            pos += len(g)
            o_ref[...] = acc.astype(o_ref.dtype)

    return pl.pallas_call(
        body, name=name, grid=(steps,),
        in_specs=[pl.BlockSpec((a.shape[0], a.shape[1] // steps, a.shape[2]), lambda i: (0, i, 0)) for a in flat],
        out_specs=[pl.BlockSpec((g[0].shape[1] // steps, g[0].shape[2]), lambda i: (i, 0)) for g in groups],
        out_shape=[jax.ShapeDtypeStruct(g[0].shape[1:], out_dtype) for g in groups],
        compiler_params=_cp(("parallel",), 40),
    )(*flat)


def _adamw_update(w_ref, g_ref, m_ref, v_ref, d_ref, mo_ref, vo_ref):
    gv = g_ref[...]
    mn = ADAM_B1 * m_ref[...] + (1.0 - ADAM_B1) * gv
    vn = ADAM_B2 * v_ref[...] + (1.0 - ADAM_B2) * (gv * gv)
    mo_ref[...] = mn
    vo_ref[...] = vn
    m_hat = mn / (1.0 - ADAM_B1 ** ADAM_STEP)
    v_hat = vn / (1.0 - ADAM_B2 ** ADAM_STEP)
    d_ref[...] = (-ADAM_LR) * (m_hat / (jnp.sqrt(v_hat) + ADAM_EPS) + ADAM_WD * w_ref[...])


def _adamw(w, g, m, v, name):
    n, r, cdim = w.shape
    tr = math.gcd(r, 512)

    def body(*refs):
        _adamw_update(*refs)

    blk = pl.BlockSpec((None, tr, cdim), lambda l, i: (l, i, 0))
    return pl.pallas_call(
        body, name=name, grid=(n, r // tr),
        in_specs=[blk] * 4, out_specs=[blk] * 3,
        out_shape=[jax.ShapeDtypeStruct(w.shape, F32)] * 3,
        compiler_params=_cp(("parallel", "parallel"), 40),
    )(w, g, m, v)


def _adamw_small(ws, gs, ms, vs, name):
    n = len(ws)

    def body(*refs):
        ins, outs = refs[:4 * n], refs[4 * n:]
        for t in range(n):
            _adamw_update(ins[t], ins[n + t], ins[2 * n + t], ins[3 * n + t], outs[t], outs[n + t], outs[2 * n + t])

    shapes = [jax.ShapeDtypeStruct(w.shape, F32) for w in ws]
    outs = pl.pallas_call(body, name=name, out_shape=shapes * 3)(*ws, *gs, *ms, *vs)
    return outs[:n], outs[n:2 * n], outs[2 * n:]


HALF = DEPTH // 2
BIG = ['w_in', 'w_uq', 'w_ukv', 'w_out']
SHARD_AXIS = {'w_in': 2, 'conv_w': 2, 'w_uq': 2, 'w_ukv': 2, 'w_out': 1}
FULL_SHAPE = {'w_in': (DEPTH, D_MODEL, D_IN), 'conv_w': (DEPTH, CONV_WIDTH, D_LRU),
              'w_uq': (DEPTH, Q_RANK, MLA_HEADS * (QK_NOPE + QK_ROPE)),
              'w_ukv': (DEPTH, KV_RANK, MLA_HEADS * (QK_NOPE + V_DIM)), 'w_out': (DEPTH, D_MIX, D_MODEL)}


def _shard_shape(n):
    shp = list(FULL_SHAPE[n])
    shp[SHARD_AXIS[n]] //= N_SHARD
    return tuple(shp)


def _rows_view(a, lead=0):
    return a.reshape(a.shape[:lead] + (-1, a.shape[-1]))


def _gather_weights(local):
    c = lax.axis_index("c")
    names = BIG + ['conv_w']
    halves = [lax.dynamic_slice_in_dim(local[n], HALF * c, HALF, axis=0) for n in names]
    halves = [h.astype(WIRE_DTYPE) if n in BIG else h for n, h in zip(names, halves)]
    got = _all_gather8(halves, "gather_weights")
    full = {}
    for n, g in zip(names, got):
        g = g.reshape((N_SHARD, DEPTH) + g.shape[2:])
        if n == 'w_in':
            full[n] = _permute_w_in([g[s] for s in range(N_SHARD)])
        else:
            full[n] = jnp.moveaxis(g, 0, SHARD_AXIS[n]).reshape(FULL_SHAPE[n])
    return full


def _shard_blocks(g, n):
    width = _shard_shape(n)[SHARD_AXIS[n]]

    def block(h, s):
        part = g[HALF * h:HALF * (h + 1)]
        if n == 'w_in':
            part = _w_in_shard(part, s)
        else:
            part = lax.slice_in_dim(part, s * width, (s + 1) * width, axis=SHARD_AXIS[n])
        return _rows_view(part)

    return jnp.stack([jnp.stack([block(h, s) for s in range(N_SHARD)]) for h in range(2)]).astype(WIRE_DTYPE)


SUM_STEPS = 8


def _reduce_big(grads):
    c = lax.axis_index("c")
    shard = 2 * lax.axis_index("x") + lax.axis_index("y")
    contrib = [_shard_blocks(grads[n], n) for n in BIG]
    from_sibling = _sibling_send(contrib, "pair_exchange_big", which_half=True)
    own_half = [lax.dynamic_index_in_dim(a, c, 0, keepdims=False) for a in contrib]
    pair_sum = _sum_leading([[_rows_view(o)[None], _rows_view(r)[None]] for o, r in zip(own_half, from_sibling)],
                            WIRE_DTYPE, SUM_STEPS, "pair_sum_big")
    to_chips = [p.reshape(a.shape[1:]) for p, a in zip(pair_sum, contrib)]
    from_chips = _chip_exchange(to_chips, "chip_exchange_big")
    own_block = [lax.dynamic_index_in_dim(a, shard, 0, keepdims=True) for a in to_chips]
    mine = _sum_leading([[o, r] for o, r in zip(own_block, from_chips)], F32, SUM_STEPS, "chip_sum_big")
    theirs = _sibling_send(mine, "sibling_big")
    both = [jnp.where(c == 0, jnp.stack([m, t]), jnp.stack([t, m])) for m, t in zip(mine, theirs)]
    return {n: b.reshape(_shard_shape(n)) for n, b in zip(BIG, both)}


SMALL = REPLICATED + ['conv_w']


def _reduce_small(grads):
    views = {n: _rows_view(jnp.atleast_2d(grads[n])) for n in SMALL}
    widths = sorted({v.shape[1] for v in views.values()})
    groups = [[n for n in SMALL if views[n].shape[1] == wd] for wd in widths]
    packed = [jnp.concatenate([views[n] for n in names], axis=0) for names in groups]
    sums = _sum_leading([[g] for g in _all_gather8(packed, "gather_small")], F32, 1, "sum_small")
    out = {}
    for names, total in zip(groups, sums):
        off = 0
        for n in names:
            rows = views[n].shape[0]
            out[n] = total[off:off + rows].reshape(grads[n].shape)
            off += rows
    return out


def kernel(x, norm_g, w_in, conv_w, conv_b, w_rg, b_rg, w_ig, b_ig, lru_lambda, q_norm_g, w_uq, kv_norm_g, w_ukv, w_pool, pool_scale, w_out, final_norm_g, loss_target, m_norm_g, m_w_in, m_conv_w, m_conv_b, m_w_rg, m_b_rg, m_w_ig, m_b_ig, m_lru_lambda, m_q_norm_g, m_w_uq, m_kv_norm_g, m_w_ukv, m_w_pool, m_pool_scale, m_w_out, m_final_norm_g, v_norm_g, v_w_in, v_conv_w, v_conv_b, v_w_rg, v_b_rg, v_w_ig, v_b_ig, v_lru_lambda, v_q_norm_g, v_w_uq, v_kv_norm_g, v_w_ukv, v_w_pool, v_pool_scale, v_w_out, v_final_norm_g):
    w_loc = dict(zip(WEIGHT_NAMES, (norm_g, w_in, conv_w, conv_b, w_rg, b_rg, w_ig, b_ig, lru_lambda, q_norm_g, w_uq,
                                    kv_norm_g, w_ukv, w_pool, pool_scale, w_out, final_norm_g)))
    m_loc = dict(zip(WEIGHT_NAMES, (m_norm_g, m_w_in, m_conv_w, m_conv_b, m_w_rg, m_b_rg, m_w_ig, m_b_ig, m_lru_lambda,
                                    m_q_norm_g, m_w_uq, m_kv_norm_g, m_w_ukv, m_w_pool, m_pool_scale, m_w_out,
                                    m_final_norm_g)))
    v_loc = dict(zip(WEIGHT_NAMES, (v_norm_g, v_w_in, v_conv_w, v_conv_b, v_w_rg, v_b_rg, v_w_ig, v_b_ig, v_lru_lambda,
                                    v_q_norm_g, v_w_uq, v_kv_norm_g, v_w_ukv, v_w_pool, v_pool_scale, v_w_out,
                                    v_final_norm_g)))
    w_full = dict(w_loc)
    w_full.update(_gather_weights(w_loc))
    loss_local, dx, g_local = _local_step(x[0], loss_target[0], w_full)
    loss = lax.psum(loss_local, ("x", "y", "c"))

    grads = _reduce_big(g_local)
    g_small = _reduce_small(g_local)
    shard = 2 * lax.axis_index("x") + lax.axis_index("y")
    width = D_LRU // N_SHARD
    grads['conv_w'] = lax.dynamic_slice_in_dim(g_small['conv_w'], shard * width, width, axis=2)
    for n in REPLICATED:
        grads[n] = g_small[n]

    delta, new_m, new_v = {}, {}, {}
    for n in BIG:
        delta[n], new_m[n], new_v[n] = _adamw(w_loc[n], grads[n], m_loc[n], v_loc[n], "adamw_" + n)
    small = [[_rows_view(jnp.atleast_2d(t[n])) for n in SMALL] for t in (w_loc, grads, m_loc, v_loc)]
    for tree, outs in zip((delta, new_m, new_v), _adamw_small(*small, "adamw_small")):
        tree.update({n: a.reshape(w_loc[n].shape) for n, a in zip(SMALL, outs)})

    return (loss, dx[None], *[grads[n] for n in WEIGHT_NAMES], *[delta[n] for n in WEIGHT_NAMES],
            *[new_m[n] for n in WEIGHT_NAMES], *[new_v[n] for n in WEIGHT_NAMES])
```

```python
import functools
import math

import jax
import jax.numpy as jnp
import numpy as np
from jax import lax
from jax.experimental import pallas as pl
from jax.experimental.pallas import tpu as pltpu

F32 = jnp.float32
MXU_DTYPE = jnp.bfloat16

D_MODEL = 1024
DEPTH = 4
EPS = 1e-6
D_LRU = 384
LRU_HEADS = 6
CONV_WIDTH = 4
LRU_C = 8.0
MLA_HEADS = 6
QK_NOPE = 64
QK_ROPE = 32
V_DIM = 64
D_MLA = MLA_HEADS * V_DIM
Q_RANK = 384
KV_RANK = 256
ROPE_BASE = 10000.0
POOL_WINDOWS = (2, 4, 8, 16)
D_POOL = 256
D_MIX = D_LRU + D_MLA + D_POOL
D_IN = 2336
ATT_SCALE = (QK_NOPE + QK_ROPE) ** -0.5
LOG2_E = 1.4426950408889634
LN_2 = 0.6931471805599453
Q_PRESCALE = ATT_SCALE * LOG2_E

ADAM_LR = 0.001
ADAM_B1 = 0.9
ADAM_B2 = 0.999
ADAM_EPS = 1e-08
ADAM_WD = 0.01
ADAM_STEP = 10

LANES = 128
SUBLANES = 8
V7X_VMEM_BYTES = 64 << 20
N_DEV = 8

D_INP = 2432
KR_LANE0 = 64
HEAD_BLK = 128
N_PAIR = MLA_HEADS // 2

WEIGHT_NAMES = ['norm_g', 'w_in', 'conv_w', 'conv_b', 'w_rg', 'b_rg', 'w_ig', 'b_ig', 'lru_lambda', 'q_norm_g',
                'w_uq', 'kv_norm_g', 'w_ukv', 'w_pool', 'pool_scale', 'w_out', 'final_norm_g']
SHARDED = ['w_in', 'conv_w', 'w_uq', 'w_ukv', 'w_out']
REPLICATED = [n for n in WEIGHT_NAMES if n not in SHARDED]


def _cp(sem, vmem_mb=None):
    return pltpu.CompilerParams(dimension_semantics=sem,
                                vmem_limit_bytes=None if vmem_mb is None else vmem_mb << 20)


def _dot(a, b):
    return jnp.dot(a.astype(MXU_DTYPE), b.astype(MXU_DTYPE), preferred_element_type=F32)


def _dot_nt(a, b):
    return lax.dot_general(a.astype(MXU_DTYPE), b.astype(MXU_DTYPE), (((1,), (1,)), ((), ())),
                           preferred_element_type=F32)


def _dot_tn(a, b):
    return lax.dot_general(a.astype(MXU_DTYPE), b.astype(MXU_DTYPE), (((0,), (0,)), ((), ())),
                           preferred_element_type=F32)


def _sig(x):
    return 0.5 * jnp.tanh(0.5 * x) + 0.5


def _down(x, k):
    return pltpu.roll(x, k, 0)


def _up(x, k):
    return pltpu.roll(x, x.shape[0] - k, 0)


def _rows(shape):
    return lax.broadcasted_iota(jnp.int32, shape, 0)


def _lanes(shape):
    return lax.broadcasted_iota(jnp.int32, shape, 1)


def _tile(s):
    return min(512, s)


def _lay(l):
    return lambda shp, blk=0: pl.BlockSpec((None,) + shp, lambda *_: (l, blk, 0))


def _inproj_fwd(x, g, w, l):
    s = x.shape[0]
    t = _tile(s)
    lay = _lay(l)

    def body(x_ref, g_ref, w_ref, h_ref, z_ref):
        xv = x_ref[...]
        rs = lax.rsqrt(jnp.mean(xv * xv, axis=-1, keepdims=True) + EPS)
        h = (xv * rs * g_ref[...]).astype(MXU_DTYPE)
        h_ref[...] = h
        z_ref[...] = jnp.dot(h, w_ref[...], preferred_element_type=F32)

    return pl.pallas_call(
        body, name="inproj_fwd", grid=(s // t,),
        in_specs=[pl.BlockSpec((t, D_MODEL), lambda i: (i, 0)),
                  lay((1, D_MODEL)), lay((D_MODEL, D_INP))],
        out_specs=[pl.BlockSpec((t, D_MODEL), lambda i: (i, 0)),
                   pl.BlockSpec((t, D_INP), lambda i: (i, 0))],
        out_shape=[jax.ShapeDtypeStruct((s, D_MODEL), MXU_DTYPE), jax.ShapeDtypeStruct((s, D_INP), F32)],
        compiler_params=_cp(("parallel",), 40),
    )(x, g, w)


def _lru_gates(za, halo, cw_ref, cb_ref, wr_ref, br_ref, wi_ref, bi_ref, lam_ref):
    t = za.shape[0]
    ext = jnp.concatenate([halo, za], axis=0)
    sh = [za] + [_down(ext, j)[SUBLANES:SUBLANES + t] for j in (1, 2, 3)]
    xa = cb_ref[...] + cw_ref[3:4, :] * sh[0] + cw_ref[2:3, :] * sh[1] + cw_ref[1:2, :] * sh[2] + cw_ref[0:1, :] * sh[3]
    r = 1.0 / (1.0 + jnp.exp(-(_dot(xa, wr_ref[...]) + br_ref[...])))
    ig = _sig(_dot(xa, wi_ref[...]) + bi_ref[...])
    lam = lam_ref[...]
    sp = jnp.maximum(-lam, 0.0) + jnp.log(1.0 + jnp.exp(-jnp.abs(lam)))
    la = (-LRU_C) * r * sp
    a = jnp.exp(la)
    y2 = 2.0 * la
    m2 = jnp.where(y2 > -0.01, -(y2 * (1.0 + y2 * (0.5 + y2 * (1.0 / 6.0)))), 1.0 - a * a)
    return xa, sh, r, ig, sp, a, jnp.sqrt(m2), m2


def _lru_fwd(z, cw, cb, wr, br, wi, bi, lam, l):
    s = z.shape[0]
    t = _tile(s)
    c = D_LRU
    lay = _lay(l)

    def body(za_ref, ga_ref, cw_ref, cb_ref, wr_ref, br_ref, wi_ref, bi_ref, lam_ref, ya_ref, hs_ref, zprev, hcar):
        i = pl.program_id(0)

        @pl.when(i == 0)
        def _():
            zprev[...] = jnp.zeros_like(zprev)
            hcar[...] = jnp.zeros_like(hcar)

        za = za_ref[...]
        xa, _, _, ig, _, a, m, _ = _lru_gates(za, zprev[...], cw_ref, cb_ref, wr_ref, br_ref, wi_ref, bi_ref, lam_ref)
        row = _rows((t, c))
        acc_h = m * (ig * xa) + jnp.where(row == 0, a * hcar[...], 0.0)
        acc_a = jnp.where(row == 0, 0.0, a)
        k = 1
        while k < t:
            acc_h = acc_h + acc_a * _down(acc_h, k)
            acc_a = acc_a * _down(acc_a, k)
            k *= 2
        hs = acc_h
        hs_ref[...] = hs
        ga = ga_ref[...]
        ya_ref[...] = (hs * (ga * _sig(ga))).astype(ya_ref.dtype)
        hcar[...] = jnp.sum(jnp.where(row == t - 1, hs, 0.0), axis=0, keepdims=True)
        zprev[...] = za_ref[t - SUBLANES:t, :]

    return pl.pallas_call(
        body, name="lru_fwd", grid=(s // t,),
        in_specs=[pl.BlockSpec((t, c), lambda i: (i, 0)), pl.BlockSpec((t, c), lambda i: (i, 1)),
                  lay((CONV_WIDTH, c)), lay((1, c)), lay((c, c)), lay((1, c)), lay((c, c)), lay((1, c)),
                  lay((1, c))],
        out_specs=[pl.BlockSpec((t, c), lambda i: (i, 0)), pl.BlockSpec((t, c), lambda i: (i, 0))],
        out_shape=[jax.ShapeDtypeStruct((s, c), MXU_DTYPE), jax.ShapeDtypeStruct((s, c), F32)],
        scratch_shapes=[pltpu.VMEM((SUBLANES, c), F32), pltpu.VMEM((1, c), F32)],
        compiler_params=_cp(("arbitrary",), 40),
    )(z, z, cw, cb, wr, br, wi, bi, lam)


def _lru_bwd(z, hs, dy, cw, cb, wr, br, wi, bi, lam, l):
    s = z.shape[0]
    t = _tile(s)
    lay = _lay(l)
    nt = s // t
    c = D_LRU
    hb = t // SUBLANES

    def body(za_ref, zh_ref, ga_ref, hs_ref, hh_ref, dy_ref, cw_ref, cb_ref, wr_ref, br_ref, wi_ref, bi_ref, lam_ref,
             dza_ref, dga_ref, dwr_ref, dwi_ref, dcw_ref, dcb_ref, dbr_ref, dbi_ref, dlam_ref, lcar, dxn):
        i = pl.program_id(0)
        tt = nt - 1 - i

        @pl.when(i == 0)
        def _():
            lcar[...] = jnp.zeros_like(lcar)
            dxn[...] = jnp.zeros_like(dxn)
            for ref in (dwr_ref, dwi_ref, dcw_ref, dcb_ref, dbr_ref, dbi_ref, dlam_ref):
                ref[...] = jnp.zeros_like(ref)

        first = (tt > 0).astype(F32)
        za = za_ref[...]
        xa, sh, r, ig, sp, a, m, m2 = _lru_gates(za, zh_ref[...] * first, cw_ref, cb_ref, wr_ref, br_ref, wi_ref,
                                                 bi_ref, lam_ref)
        hs_v = hs_ref[...]
        hprev = _down(jnp.concatenate([hh_ref[...] * first, hs_v], axis=0), 1)[SUBLANES:SUBLANES + t]
        ga = ga_ref[...]
        sg = _sig(ga)
        silu = ga * sg
        dya = dy_ref[...]
        dga_ref[...] = (dya * hs_v * (sg * (1.0 + ga * (1.0 - sg)))).astype(dga_ref.dtype)
        row = _rows((t, c))
        acc_h = dya * silu + jnp.where(row == t - 1, lcar[...], 0.0)
        acc_b = jnp.where(row < t - 1, _up(a, 1), 0.0)
        k = 1
        while k < t:
            acc_h = acc_h + acc_b * _up(acc_h, k)
            acc_b = acc_b * _up(acc_b, k)
            k *= 2
        lmb = acc_h
        lcar[...] = jnp.sum(jnp.where(row == 0, a * lmb, 0.0), axis=0, keepdims=True)
        da = lmb * hprev
        dxa = lmb * m * ig
        di = lmb * m * xa
        dm = lmb * ig * xa
        dla = da * a - dm * (a * a) * lax.rsqrt(m2)
        dr = dla * ((-LRU_C) * sp)
        lam = lam_ref[...]
        dsp = jnp.sum(dla * ((-LRU_C) * r), axis=0, keepdims=True)
        dlam_ref[...] += dsp * (-1.0 / (1.0 + jnp.exp(lam)))
        dpr = dr * r * (1.0 - r)
        dpi = di * ig * (1.0 - ig)
        dbr_ref[...] += jnp.sum(dpr, axis=0, keepdims=True)
        dbi_ref[...] += jnp.sum(dpi, axis=0, keepdims=True)
        dwr_ref[...] += _dot_tn(xa, dpr)
        dwi_ref[...] += _dot_tn(xa, dpi)
        dxa = dxa + _dot_nt(dpr, wr_ref[...]) + _dot_nt(dpi, wi_ref[...])
        dcb_ref[...] += jnp.sum(dxa, axis=0, keepdims=True)
        for k in range(CONV_WIDTH):
            dcw_ref[k:k + 1, :] += jnp.sum(dxa * sh[CONV_WIDTH - 1 - k], axis=0, keepdims=True)
        ext = jnp.concatenate([dxa, dxn[...]], axis=0)
        dza = cw_ref[3:4, :] * dxa
        for j in (1, 2, 3):
            dza = dza + cw_ref[3 - j:4 - j, :] * _up(ext, j)[:t]
        dza_ref[...] = dza.astype(dza_ref.dtype)
        dxn[...] = dxa[:SUBLANES]

    full = lambda shp: pl.BlockSpec(shp, lambda i: (0, 0))
    rev = lambda i: nt - 1 - i
    halo = lambda i: (jnp.maximum((nt - 1 - i) * hb - 1, 0), 0)
    outs = pl.pallas_call(
        body, name="lru_bwd", grid=(nt,),
        in_specs=[pl.BlockSpec((t, c), lambda i: (rev(i), 0)), pl.BlockSpec((SUBLANES, c), halo),
                  pl.BlockSpec((t, c), lambda i: (rev(i), 1)),
                  pl.BlockSpec((t, c), lambda i: (rev(i), 0)), pl.BlockSpec((SUBLANES, c), halo),
                  pl.BlockSpec((t, c), lambda i: (rev(i), 0)),
                  lay((CONV_WIDTH, c)), lay((1, c)), lay((c, c)), lay((1, c)), lay((c, c)), lay((1, c)),
                  lay((1, c))],
        out_specs=[pl.BlockSpec((t, c), lambda i: (rev(i), 0)), pl.BlockSpec((t, c), lambda i: (rev(i), 0)),
                   full((c, c)), full((c, c)), full((CONV_WIDTH, c)), full((1, c)), full((1, c)), full((1, c)),
                   full((1, c))],
        out_shape=[jax.ShapeDtypeStruct((s, c), MXU_DTYPE), jax.ShapeDtypeStruct((s, c), MXU_DTYPE),
                   jax.ShapeDtypeStruct((c, c), F32), jax.ShapeDtypeStruct((c, c), F32),
                   jax.ShapeDtypeStruct((CONV_WIDTH, c), F32)] + [jax.ShapeDtypeStruct((1, c), F32)] * 4,
        scratch_shapes=[pltpu.VMEM((1, c), F32), pltpu.VMEM((SUBLANES, c), F32)],
        compiler_params=_cp(("arbitrary",), 48),
    )(z, z, z, hs, hs, dy, cw, cb, wr, br, wi, bi, lam)
    return outs


POOL_HALO = 16


def _pool_select(lane, v2, v4, v8, v16):
    return jnp.where(lane < 64, v2, jnp.where(lane < 128, v4, jnp.where(lane < 192, v8, v16)))


def _pool_counts(t0, t, c):
    lane = _lanes((t, c))
    win = _pool_select(lane, 2.0, 4.0, 8.0, 16.0)
    seen = (t0 + _rows((t, c)) + 1).astype(F32)
    return lane, jnp.minimum(seen, win)


def _pooled(zc, halo, lane, cnt):
    t = zc.shape[0]
    ext = jnp.concatenate([halo, zc], axis=0)
    s2 = ext + _down(ext, 1)
    s4 = s2 + _down(s2, 2)
    s8 = s4 + _down(s4, 4)
    s16 = s8 + _down(s8, 8)
    cut = lambda v: v[POOL_HALO:POOL_HALO + t]
    return _pool_select(lane, cut(s2), cut(s4), cut(s8), cut(s16)) / cnt - zc


def _pool_fwd(z, wp, ps, l):
    s = z.shape[0]
    t = _tile(s)
    c = D_POOL
    lay = _lay(l)

    def body(zc_ref, gc_ref, wp_ref, ps_ref, yc_ref, zprev):
        i = pl.program_id(0)

        @pl.when(i == 0)
        def _():
            zprev[...] = jnp.zeros_like(zprev)

        zc = zc_ref[...]
        lane, cnt = _pool_counts(i * t, t, c)
        pooled = _pooled(zc, zprev[...], lane, cnt)
        pc = _dot(pooled, wp_ref[...])
        gc = gc_ref[...]
        yc_ref[...] = (pc * ps_ref[...] * (gc * _sig(gc))).astype(yc_ref.dtype)
        zprev[...] = zc_ref[t - POOL_HALO:t, :]

    return pl.pallas_call(
        body, name="pool_fwd", grid=(s // t,),
        in_specs=[pl.BlockSpec((t, c), lambda i: (i, 7)), pl.BlockSpec((t, c), lambda i: (i, 8)),
                  lay((c, c)), lay((1, c))],
        out_specs=pl.BlockSpec((t, c), lambda i: (i, 0)),
        out_shape=jax.ShapeDtypeStruct((s, c), MXU_DTYPE),
        scratch_shapes=[pltpu.VMEM((POOL_HALO, c), F32)],
        compiler_params=_cp(("arbitrary",), 32),
    )(z, z, wp, ps)


def _pool_bwd(z, dy, wp, ps, l):
    s = z.shape[0]
    t = _tile(s)
    lay = _lay(l)
    nt = s // t
    c = D_POOL
    hb = t // POOL_HALO

    def body(zc_ref, zh_ref, gc_ref, dy_ref, wp_ref, ps_ref, dzc_ref, dgc_ref, dwp_ref, dps_ref, ddn):
        i = pl.program_id(0)
        tt = nt - 1 - i

        @pl.when(i == 0)
        def _():
            ddn[...] = jnp.zeros_like(ddn)
            dwp_ref[...] = jnp.zeros_like(dwp_ref)
            dps_ref[...] = jnp.zeros_like(dps_ref)

        first = (tt > 0).astype(F32)
        zc = zc_ref[...]
        lane, cnt = _pool_counts(tt * t, t, c)
        pooled = _pooled(zc, zh_ref[...] * first, lane, cnt)
        pc = _dot(pooled, wp_ref[...])
        gc = gc_ref[...]
        sg = _sig(gc)
        silu = gc * sg
        dyc = dy_ref[...]
        ps_v = ps_ref[...]
        dgc_ref[...] = (dyc * pc * ps_v * (sg * (1.0 + gc * (1.0 - sg)))).astype(dgc_ref.dtype)
        dps_ref[...] += jnp.sum(dyc * pc * silu, axis=0, keepdims=True)
        dpc = dyc * ps_v * silu
        dwp_ref[...] += _dot_tn(pooled, dpc)
        dpooled = _dot_nt(dpc, wp_ref[...])
        dd = dpooled / cnt
        ext = jnp.concatenate([dd, ddn[...]], axis=0)
        f2 = ext + _up(ext, 1)
        f4 = f2 + _up(f2, 2)
        f8 = f4 + _up(f4, 4)
        f16 = f8 + _up(f8, 8)
        dzc = _pool_select(lane, f2[:t], f4[:t], f8[:t], f16[:t]) - dpooled
        dzc_ref[...] = dzc.astype(dzc_ref.dtype)
        ddn[...] = dd[:POOL_HALO]

    full = lambda shp: pl.BlockSpec(shp, lambda i: (0, 0))
    rev = lambda i: nt - 1 - i
    return pl.pallas_call(
        body, name="pool_bwd", grid=(nt,),
        in_specs=[pl.BlockSpec((t, c), lambda i: (rev(i), 7)),
                  pl.BlockSpec((POOL_HALO, c), lambda i: (jnp.maximum(rev(i) * hb - 1, 0), 7)),
                  pl.BlockSpec((t, c), lambda i: (rev(i), 8)),
                  pl.BlockSpec((t, c), lambda i: (rev(i), 0)),
                  lay((c, c)), lay((1, c))],
        out_specs=[pl.BlockSpec((t, c), lambda i: (rev(i), 0)), pl.BlockSpec((t, c), lambda i: (rev(i), 0)),
                   full((c, c)), full((1, c))],
        out_shape=[jax.ShapeDtypeStruct((s, c), MXU_DTYPE), jax.ShapeDtypeStruct((s, c), MXU_DTYPE),
                   jax.ShapeDtypeStruct((c, c), F32), jax.ShapeDtypeStruct((1, c), F32)],
        scratch_shapes=[pltpu.VMEM((POOL_HALO, c), F32)],
        compiler_params=_cp(("arbitrary",), 32),
    )(z, z, z, dy, wp, ps)


def _rope_tables(s):
    pos = jnp.arange(s, dtype=F32)
    inv_freq = ROPE_BASE ** (-jnp.arange(0, QK_ROPE, 2, dtype=F32) / QK_ROPE)
    ang = pos[:, None] * inv_freq[None, :]
    cos, sin = jnp.cos(ang), jnp.sin(ang)
    half = QK_ROPE // 2
    ones = jnp.ones((s, QK_NOPE), F32)
    z64 = jnp.zeros((s, QK_NOPE), F32)
    zh = jnp.zeros((s, half), F32)
    z32 = jnp.zeros((s, HEAD_BLK - QK_NOPE - QK_ROPE), F32)
    c_t = jnp.concatenate([ones, cos, cos, z32], axis=1)
    s1_t = jnp.concatenate([z64, -sin, zh, z32], axis=1)
    s2_t = jnp.concatenate([z64, zh, sin, z32], axis=1)
    return c_t, s1_t, s2_t


def _rope(x, c_t, s1_t, s2_t):
    return x * c_t + pltpu.roll(x, HEAD_BLK - 16, 1) * s1_t + pltpu.roll(x, 16, 1) * s2_t


def _unrope(d, c_t, s1_t, s2_t):
    return d * c_t + pltpu.roll(d * s1_t, 16, 1) + pltpu.roll(d * s2_t, HEAD_BLK - 16, 1)


def _mla_prep_fwd(z, gq, gkv, wuq, wukv, c_t, s1_t, s2_t, l):
    s = z.shape[0]
    t = _tile(s)
    hq = MLA_HEADS * HEAD_BLK
    lay = _lay(l)

    def body(cq_ref, ckv_ref, kr_ref, gq_ref, gkv_ref, wuq_ref, wukv_ref, c_ref, s1_ref, s2_ref,
             q_ref, k_ref, v_ref, qn_ref, kvn_ref):
        ct, s1, s2 = c_ref[...], s1_ref[...], s2_ref[...]
        cq = cq_ref[...]
        qn = (cq * lax.rsqrt(jnp.mean(cq * cq, axis=-1, keepdims=True) + EPS) * gq_ref[...]).astype(MXU_DTYPE)
        qn_ref[...] = qn
        q = jnp.dot(qn, wuq_ref[...], preferred_element_type=F32)
        ckv = ckv_ref[...]
        kvn = (ckv * lax.rsqrt(jnp.mean(ckv * ckv, axis=-1, keepdims=True) + EPS) * gkv_ref[...]).astype(MXU_DTYPE)
        kvn_ref[...] = kvn
        kvp = jnp.dot(kvn, wukv_ref[...], preferred_element_type=F32)
        krr = _rope(kr_ref[...], ct, s1, s2)
        for h in range(MLA_HEADS):
            blk = slice(h * HEAD_BLK, (h + 1) * HEAD_BLK)
            q_ref[:, blk] = (_rope(q[:, blk], ct, s1, s2) * Q_PRESCALE).astype(q_ref.dtype)
            k_ref[:, blk] = (kvp[:, blk] + krr).astype(k_ref.dtype)
        v_ref[...] = kvp[:, hq:].astype(v_ref.dtype)

    tab = pl.BlockSpec((t, HEAD_BLK), lambda i: (i, 0))
    return pl.pallas_call(
        body, name="mla_prep_fwd", grid=(s // t,),
        in_specs=[pl.BlockSpec((t, Q_RANK), lambda i: (i, 2)), pl.BlockSpec((t, KV_RANK), lambda i: (i, 6)),
                  pl.BlockSpec((t, HEAD_BLK), lambda i: (i, 18)),
                  lay((1, Q_RANK)), lay((1, KV_RANK)), lay((Q_RANK, hq)), lay((KV_RANK, hq + D_MLA)),
                  tab, tab, tab],
        out_specs=[pl.BlockSpec((t, hq), lambda i: (i, 0)), pl.BlockSpec((t, hq), lambda i: (i, 0)),
                   pl.BlockSpec((t, D_MLA), lambda i: (i, 0)), pl.BlockSpec((t, Q_RANK), lambda i: (i, 0)),
                   pl.BlockSpec((t, KV_RANK), lambda i: (i, 0))],
        out_shape=[jax.ShapeDtypeStruct((s, hq), MXU_DTYPE), jax.ShapeDtypeStruct((s, hq), MXU_DTYPE),
                   jax.ShapeDtypeStruct((s, D_MLA), MXU_DTYPE), jax.ShapeDtypeStruct((s, Q_RANK), MXU_DTYPE),
                   jax.ShapeDtypeStruct((s, KV_RANK), MXU_DTYPE)],
        compiler_params=_cp(("parallel",), 40),
    )(z, z, z, gq, gkv, wuq, wukv, c_t, s1_t, s2_t)


SUM_LANE_A = V_DIM
SUM_LANE_B = 0
FWD_TILES_PER_TRIP = 4


def _flash_fwd(q, k, v, z):
    s = q.shape[0]
    t = _tile(s)
    nq = s // t
    pw = 2 * HEAD_BLK

    def body(q_ref, k_ref, v_ref, gb_ref, o_ref, yb_ref, lse_ref):
        i = pl.program_id(1)
        qv = q_ref[...]
        qa, qb = qv[:, :HEAD_BLK], qv[:, HEAD_BLK:]
        lane = _lanes((t, HEAD_BLK))
        lo = lane < V_DIM

        def update(qh, kh, vh, m, acc, masked):
            sc = _dot_nt(qh, kh)
            if masked:
                sc = jnp.where(_lanes((t, t)) <= _rows((t, t)), sc, -1e30)
            m_new = jnp.maximum(m, jnp.max(sc, axis=-1, keepdims=True))
            p = jnp.exp2(sc - m_new).astype(MXU_DTYPE)
            return m_new, acc * jnp.exp2(m - m_new) + _dot(p, vh)

        def step(j, carry, masked):
            ma, mb, acc_a, acc_b = carry
            kv_rows = pl.ds(pl.multiple_of(j * t, t), t)
            kt = k_ref[kv_rows, :]
            vt = v_ref[kv_rows, :]
            lane_v = _lanes(vt.shape)
            one = jnp.ones_like(vt)
            zero_v = jnp.zeros_like(vt)
            v_a = jnp.where(lane_v < V_DIM, vt, jnp.where(lane_v == SUM_LANE_A, one, zero_v))
            v_b = jnp.where(lane_v >= V_DIM, vt, jnp.where(lane_v == SUM_LANE_B, one, zero_v))
            ma, acc_a = update(qa, kt[:, :HEAD_BLK], v_a, ma, acc_a, masked)
            mb, acc_b = update(qb, kt[:, HEAD_BLK:], v_b, mb, acc_b, masked)
            return ma, mb, acc_a, acc_b

        neg = jnp.full((t, 1), -1e30, F32)
        zero = jnp.zeros((t, HEAD_BLK), F32)
        def trip(jj, cr):
            for u in range(FWD_TILES_PER_TRIP):
                cr = step(FWD_TILES_PER_TRIP * jj + u, cr, False)
            return cr

        whole = i // FWD_TILES_PER_TRIP
        carry = lax.fori_loop(0, whole, trip, (neg, neg, zero, zero))
        carry = lax.fori_loop(FWD_TILES_PER_TRIP * whole, i, lambda j, cr: step(j, cr, False), carry)
        ma, mb, acc_a, acc_b = step(i, carry, True)
        la = jnp.sum(jnp.where(lane == SUM_LANE_A, acc_a, 0.0), axis=-1, keepdims=True)
        lb = jnp.sum(jnp.where(lane == SUM_LANE_B, acc_b, 0.0), axis=-1, keepdims=True)
        o = jnp.where(lo, acc_a * (1.0 / la), acc_b * (1.0 / lb))
        o_ref[...] = o
        gb = gb_ref[...]
        yb_ref[...] = (o * (gb * _sig(gb))).astype(yb_ref.dtype)
        lse = jnp.where(lo, ma + jnp.log(la) * LOG2_E, mb + jnp.log(lb) * LOG2_E)
        pick = ((_rows((SUBLANES, HEAD_BLK)) == 0) & (_lanes((SUBLANES, HEAD_BLK)) == 0)) | (
            (_rows((SUBLANES, HEAD_BLK)) == 1) & (_lanes((SUBLANES, HEAD_BLK)) == V_DIM))
        lse_ref[0, 0] = lax.dot_general(pick.astype(F32), lse, (((1,), (1,)), ((), ())),
                                        precision=lax.Precision.HIGHEST, preferred_element_type=F32)

    return pl.pallas_call(
        body, name="flash_fwd", grid=(N_PAIR, nq),
        in_specs=[pl.BlockSpec((t, pw), lambda p, i: (i, p)), pl.BlockSpec((s, pw), lambda p, i: (0, p)),
                  pl.BlockSpec((s, HEAD_BLK), lambda p, i: (0, p)),
                  pl.BlockSpec((t, HEAD_BLK), lambda p, i: (i, 9 + p))],
        out_specs=[pl.BlockSpec((t, HEAD_BLK), lambda p, i: (i, p)), pl.BlockSpec((t, HEAD_BLK), lambda p, i: (i, p)),
                   pl.BlockSpec((1, 1, SUBLANES, t), lambda p, i: (p, i, 0, 0))],
        out_shape=[jax.ShapeDtypeStruct((s, D_MLA), F32), jax.ShapeDtypeStruct((s, D_MLA), MXU_DTYPE),
                   jax.ShapeDtypeStruct((N_PAIR, nq, SUBLANES, t), F32)],
        compiler_params=_cp(("parallel", "parallel"), 48),
    )(q, k, v, z)


def _flash_bwd(q, k, v, do, lse, delta):
    s = q.shape[0]
    t = _tile(s)
    nq = s // t
    pw = 2 * HEAD_BLK

    def body(q_ref, do_ref, lse_ref, dl_ref, k_ref, v_ref, dq_ref, dk_ref, dv_ref):
        j = pl.program_id(1)

        @pl.when(j == 0)
        def _():
            dq_ref[...] = jnp.zeros_like(dq_ref)

        kt = k_ref[...]
        ka, kb = kt[:, :HEAD_BLK], kt[:, HEAD_BLK:]
        vt = v_ref[...]

        def head(kh, qh, do_h, lse_row, dl_row, masked):
            st = _dot_nt(kh, qh)
            if masked:
                st = jnp.where(_rows((t, t)) <= _lanes((t, t)), st, -1e30)
            pt = jnp.exp2(st - lse_row)
            dv_h = _dot(pt, do_h)
            dst = (pt * (_dot_nt(vt, do_h) - dl_row)).astype(MXU_DTYPE)
            return dv_h, _dot(dst, qh), _dot_tn(dst, kh)

        def step(i, carry, masked):
            dka, dkb, dv = carry
            q_rows = pl.ds(pl.multiple_of(i * t, t), t)
            qv = q_ref[q_rows, :]
            dov = do_ref[q_rows, :]
            lane = _lanes(dov.shape)
            do_lo = jnp.where(lane < V_DIM, dov, jnp.zeros_like(dov))
            do_hi = jnp.where(lane >= V_DIM, dov, jnp.zeros_like(dov))
            dva, dk_a, dq_a = head(ka, qv[:, :HEAD_BLK], do_lo, lse_ref[0, i, 0:1, :], dl_ref[0, i, 0:1, :], masked)
            dvb, dk_b, dq_b = head(kb, qv[:, HEAD_BLK:], do_hi, lse_ref[0, i, 1:2, :], dl_ref[0, i, 1:2, :], masked)
            dq_ref[q_rows, 0:HEAD_BLK] += dq_a
            dq_ref[q_rows, HEAD_BLK:pw] += dq_b
            return dka + dk_a, dkb + dk_b, dv + dva + dvb

        zero = jnp.zeros((t, HEAD_BLK), F32)
        carry = step(j, (zero, zero, zero), True)
        rest = nq - 1 - j
        carry = lax.cond(rest % 2 == 1, lambda cr: step(j + 1, cr, False), lambda cr: cr, carry)
        first = j + 1 + rest % 2
        dka, dkb, dv = lax.fori_loop(
            0, rest // 2, lambda ii, cr: step(first + 2 * ii + 1, step(first + 2 * ii, cr, False), False), carry)
        dk_ref[:, 0:HEAD_BLK] = dka * LN_2
        dk_ref[:, HEAD_BLK:pw] = dkb * LN_2
        dv_ref[...] = dv.astype(dv_ref.dtype)

    return pl.pallas_call(
        body, name="flash_bwd", grid=(N_PAIR, nq),
        in_specs=[pl.BlockSpec((s, pw), lambda p, j: (0, p)), pl.BlockSpec((s, HEAD_BLK), lambda p, j: (0, p)),
                  pl.BlockSpec((1, nq, SUBLANES, t), lambda p, j: (p, 0, 0, 0)),
                  pl.BlockSpec((1, nq, SUBLANES, t), lambda p, j: (p, 0, 0, 0)),
                  pl.BlockSpec((t, pw), lambda p, j: (j, p)), pl.BlockSpec((t, HEAD_BLK), lambda p, j: (j, p))],
        out_specs=[pl.BlockSpec((s, pw), lambda p, j: (0, p)), pl.BlockSpec((t, pw), lambda p, j: (j, p)),
                   pl.BlockSpec((t, HEAD_BLK), lambda p, j: (j, p))],
        out_shape=[jax.ShapeDtypeStruct((s, MLA_HEADS * HEAD_BLK), F32),
                   jax.ShapeDtypeStruct((s, MLA_HEADS * HEAD_BLK), F32),
                   jax.ShapeDtypeStruct((s, D_MLA), MXU_DTYPE)],
        compiler_params=_cp(("parallel", "arbitrary"), 56),
    )(q, do, lse, delta, k, v)


def _mla_prep_bwd(dq, dk, dv, z, qn, kvn, gq, gkv, wuq, wukv, c_t, s1_t, s2_t, l):
    s = z.shape[0]
    t = _tile(s)
    hq = MLA_HEADS * HEAD_BLK
    lay = _lay(l)

    def body(dq_ref, dk_ref, dv_ref, cq_ref, ckv_ref, qn_ref, kvn_ref, gq_ref, gkv_ref, wuq_ref, wukv_ref,
             c_ref, s1_ref, s2_ref, dcq_ref, dckv_ref, dkr_ref, dwuq_ref, dwukv_ref, dgq_ref, dgkv_ref,
             dqu_ref, dkvp_ref):
        @pl.when(pl.program_id(0) == 0)
        def _():
            for ref in (dwuq_ref, dwukv_ref, dgq_ref, dgkv_ref):
                ref[...] = jnp.zeros_like(ref)

        ct, s1, s2 = c_ref[...], s1_ref[...], s2_ref[...]
        dk_sum = jnp.zeros((t, HEAD_BLK), F32)
        for h in range(MLA_HEADS):
            blk = slice(h * HEAD_BLK, (h + 1) * HEAD_BLK)
            dqu_ref[:, blk] = _unrope(dq_ref[:, blk] * ATT_SCALE, ct, s1, s2).astype(dqu_ref.dtype)
            dkh = dk_ref[:, blk]
            dk_sum = dk_sum + dkh
            dkvp_ref[:, blk] = dkh.astype(dkvp_ref.dtype)
        dkvp_ref[:, hq:] = dv_ref[...]
        lane = _lanes((t, HEAD_BLK))
        rope_lanes = (lane >= KR_LANE0) & (lane < KR_LANE0 + QK_ROPE)
        dkr_ref[...] = _unrope(jnp.where(rope_lanes, dk_sum, 0.0), ct, s1, s2).astype(dkr_ref.dtype)

        def norm_bwd(c_in, g, dn_out, dc_ref, dg_ref):
            rs = lax.rsqrt(jnp.mean(c_in * c_in, axis=-1, keepdims=True) + EPS)
            n = c_in * rs
            dg_ref[...] += jnp.sum(dn_out * n, axis=0, keepdims=True)
            dn = dn_out * g
            dc_ref[...] = (rs * (dn - n * jnp.mean(dn * n, axis=-1, keepdims=True))).astype(dc_ref.dtype)

        dqu, dkvp = dqu_ref[...], dkvp_ref[...]
        dwuq_ref[...] += _dot_tn(qn_ref[...], dqu)
        dwukv_ref[...] += _dot_tn(kvn_ref[...], dkvp)
        norm_bwd(cq_ref[...], gq_ref[...], _dot_nt(dqu, wuq_ref[...]), dcq_ref, dgq_ref)
        norm_bwd(ckv_ref[...], gkv_ref[...], _dot_nt(dkvp, wukv_ref[...]), dckv_ref, dgkv_ref)

    full = lambda shp: pl.BlockSpec(shp, lambda i: (0, 0))
    tab = pl.BlockSpec((t, HEAD_BLK), lambda i: (i, 0))
    row = lambda w: pl.BlockSpec((t, w), lambda i: (i, 0))
    return pl.pallas_call(
        body, name="mla_prep_bwd", grid=(s // t,),
        in_specs=[row(hq), row(hq), row(D_MLA),
                  pl.BlockSpec((t, Q_RANK), lambda i: (i, 2)), pl.BlockSpec((t, KV_RANK), lambda i: (i, 6)),
                  row(Q_RANK), row(KV_RANK),
                  lay((1, Q_RANK)), lay((1, KV_RANK)), lay((Q_RANK, hq)), lay((KV_RANK, hq + D_MLA)),
                  tab, tab, tab],
        out_specs=[row(Q_RANK), row(KV_RANK), row(HEAD_BLK), full((Q_RANK, hq)), full((KV_RANK, hq + D_MLA)),
                   full((1, Q_RANK)), full((1, KV_RANK))],
        out_shape=[jax.ShapeDtypeStruct((s, Q_RANK), MXU_DTYPE), jax.ShapeDtypeStruct((s, KV_RANK), MXU_DTYPE),
                   jax.ShapeDtypeStruct((s, HEAD_BLK), MXU_DTYPE),
                   jax.ShapeDtypeStruct((Q_RANK, hq), F32), jax.ShapeDtypeStruct((KV_RANK, hq + D_MLA), F32),
                   jax.ShapeDtypeStruct((1, Q_RANK), F32), jax.ShapeDtypeStruct((1, KV_RANK), F32)],
        scratch_shapes=[pltpu.VMEM((t, hq), MXU_DTYPE), pltpu.VMEM((t, hq + D_MLA), MXU_DTYPE)],
        compiler_params=_cp(("arbitrary",), 48),
    )(dq, dk, dv, z, z, qn, kvn, gq, gkv, wuq, wukv, c_t, s1_t, s2_t)


def _outproj_fwd(x, ya, yb, yc, w, l):
    s = x.shape[0]
    t = _tile(s)
    lay = _lay(l)

    def body(x_ref, ya_ref, yb_ref, yc_ref, wa_ref, wb_ref, wc_ref, o_ref):
        o_ref[...] = (x_ref[...] + _dot(ya_ref[...], wa_ref[...]) + _dot(yb_ref[...], wb_ref[...])
                      + _dot(yc_ref[...], wc_ref[...]))

    row = lambda w_: pl.BlockSpec((t, w_), lambda i: (i, 0))
    return pl.pallas_call(
        body, name="outproj_fwd", grid=(s // t,),
        in_specs=[row(D_MODEL), row(D_LRU), row(D_MLA), row(D_POOL),
                  lay((D_LRU, D_MODEL), 0), lay((D_MLA, D_MODEL), 1), lay((D_POOL, D_MODEL), 3)],
        out_specs=row(D_MODEL),
        out_shape=jax.ShapeDtypeStruct((s, D_MODEL), F32),
        compiler_params=_cp(("parallel",), 40),
    )(x, ya, yb, yc, w, w, w)


def _outproj_bwd(dx, ya, yb, yc, o, z, w, l):
    s = dx.shape[0]
    t = _tile(s)
    nq = s // t
    rows = N_PAIR * SUBLANES

    def body(dx_ref, ya_ref, yb_ref, yc_ref, o_ref, gb_ref, w_ref, dya_ref, dyc_ref, do_ref, dgb_ref, dl_ref, dw_ref):
        @pl.when(pl.program_id(0) == 0)
        def _():
            dw_ref[...] = jnp.zeros_like(dw_ref)

        dxv = dx_ref[...].astype(MXU_DTYPE)
        dy = _dot_nt(dxv, w_ref[...])
        dw_ref[0:D_LRU, :] += _dot_tn(ya_ref[...], dxv)
        dw_ref[D_LRU:D_LRU + D_MLA, :] += _dot_tn(yb_ref[...], dxv)
        dw_ref[D_LRU + D_MLA:D_MIX, :] += _dot_tn(yc_ref[...], dxv)
        dya_ref[...] = dy[:, :D_LRU]
        dyc_ref[...] = dy[:, D_LRU + D_MLA:]
        dyb = dy[:, D_LRU:D_LRU + D_MLA]
        gb = gb_ref[...]
        sg = _sig(gb)
        ov = o_ref[...]
        do = dyb * (gb * sg)
        do_ref[...] = do.astype(do_ref.dtype)
        dgb_ref[...] = (dyb * ov * (sg * (1.0 + gb * (1.0 - sg)))).astype(dgb_ref.dtype)
        r = _rows((rows, D_MLA))
        head = (r // SUBLANES) * 2 + (r % SUBLANES)
        sel = ((r % SUBLANES) < 2) & (_lanes((rows, D_MLA)) // V_DIM == head)
        dl = lax.dot_general(sel.astype(F32), do * ov, (((1,), (1,)), ((), ())),
                             precision=lax.Precision.HIGHEST, preferred_element_type=F32)
        for p in range(N_PAIR):
            dl_ref[p, 0] = dl[p * SUBLANES:(p + 1) * SUBLANES]

    row = lambda w_: pl.BlockSpec((t, w_), lambda i: (i, 0))
    return pl.pallas_call(
        body, name="outproj_bwd", grid=(nq,),
        in_specs=[row(D_MODEL), row(D_LRU), row(D_MLA), row(D_POOL), row(D_MLA),
                  pl.BlockSpec((t, D_MLA), lambda i: (i, 3)), _lay(l)((D_MIX, D_MODEL))],
        out_specs=[row(D_LRU), row(D_POOL), row(D_MLA), row(D_MLA),
                   pl.BlockSpec((N_PAIR, 1, SUBLANES, t), lambda i: (0, i, 0, 0)),
                   pl.BlockSpec((D_MIX, D_MODEL), lambda i: (0, 0))],
        out_shape=[jax.ShapeDtypeStruct((s, D_LRU), F32), jax.ShapeDtypeStruct((s, D_POOL), F32),
                   jax.ShapeDtypeStruct((s, D_MLA), MXU_DTYPE), jax.ShapeDtypeStruct((s, D_MLA), MXU_DTYPE),
                   jax.ShapeDtypeStruct((N_PAIR, nq, SUBLANES, t), F32),
                   jax.ShapeDtypeStruct((D_MIX, D_MODEL), F32)],
        compiler_params=_cp(("arbitrary",), 48),
    )(dx, ya, yb, yc, o, z, w)


DZ_WIDTHS = (D_LRU, D_LRU, Q_RANK, D_MLA, KV_RANK, D_POOL, D_POOL, HEAD_BLK)


def _dwin(h, dz_parts):
    s = h.shape[0]
    t = _tile(s)

    def body(h_ref, *refs):
        o_ref = refs[-1]

        @pl.when(pl.program_id(0) == 0)
        def _():
            o_ref[...] = jnp.zeros_like(o_ref)

        o_ref[...] += _dot_tn(h_ref[...], jnp.concatenate([r[...] for r in refs[:-1]], axis=1))

    row = lambda w_: pl.BlockSpec((t, w_), lambda i: (i, 0))
    return pl.pallas_call(
        body, name="dwin", grid=(s // t,),
        in_specs=[row(D_MODEL)] + [row(wd) for wd in DZ_WIDTHS],
        out_specs=pl.BlockSpec((D_MODEL, D_INP), lambda i: (0, 0)),
        out_shape=jax.ShapeDtypeStruct((D_MODEL, D_INP), F32),
        compiler_params=_cp(("arbitrary",), 56),
    )(h, *dz_parts)


def _inproj_bwd(dz_parts, w, x, g, dxn, l):
    s = x.shape[0]
    t = _tile(s)
    lay = _lay(l)
    n_parts = len(DZ_WIDTHS)

    def body(*refs):
        part_refs = refs[:n_parts]
        w_ref, x_ref, g_ref, dxn_ref, dx_ref, dg_ref = refs[n_parts:]

        @pl.when(pl.program_id(0) == 0)
        def _():
            dg_ref[...] = jnp.zeros_like(dg_ref)

        dh = _dot_nt(jnp.concatenate([r[...] for r in part_refs], axis=1), w_ref[...])
        xv = x_ref[...]
        rs = lax.rsqrt(jnp.mean(xv * xv, axis=-1, keepdims=True) + EPS)
        n = xv * rs
        dg_ref[...] += jnp.sum(dh * n, axis=0, keepdims=True)
        dn = dh * g_ref[...]
        dx_ref[...] = dxn_ref[...] + rs * (dn - n * jnp.mean(dn * n, axis=-1, keepdims=True))

    row = lambda w_: pl.BlockSpec((t, w_), lambda i: (i, 0))
    return pl.pallas_call(
        body, name="inproj_bwd", grid=(s // t,),
        in_specs=[row(wd) for wd in DZ_WIDTHS] + [lay((D_MODEL, D_INP)), row(D_MODEL), lay((1, D_MODEL)),
                                                  row(D_MODEL)],
        out_specs=[row(D_MODEL), pl.BlockSpec((1, D_MODEL), lambda i: (0, 0))],
        out_shape=[jax.ShapeDtypeStruct((s, D_MODEL), F32), jax.ShapeDtypeStruct((1, D_MODEL), F32)],
        compiler_params=_cp(("arbitrary",), 48),
    )(*dz_parts, w, x, g, dxn)


def _loss_head(x, g, tgt):
    s = x.shape[0]
    t = _tile(s)

    def body(x_ref, g_ref, t_ref, dx_ref, loss_ref, dg_ref):
        @pl.when(pl.program_id(0) == 0)
        def _():
            loss_ref[...] = jnp.zeros_like(loss_ref)
            dg_ref[...] = jnp.zeros_like(dg_ref)

        xv = x_ref[...]
        rs = lax.rsqrt(jnp.mean(xv * xv, axis=-1, keepdims=True) + EPS)
        n = xv * rs
        gv = g_ref[...]
        e = n * gv - t_ref[...]
        loss_ref[...] += 0.5 * jnp.sum(jnp.mean(e * e, axis=-1, keepdims=True))
        dyf = e * (1.0 / D_MODEL)
        dg_ref[...] += jnp.sum(dyf * n, axis=0, keepdims=True)
        dn = dyf * gv
        dx_ref[...] = rs * (dn - n * jnp.mean(dn * n, axis=-1, keepdims=True))

    row = pl.BlockSpec((t, D_MODEL), lambda i: (i, 0))
    vec = pl.BlockSpec((1, D_MODEL), lambda i: (0, 0))
    return pl.pallas_call(
        body, name="loss_head", grid=(s // t,),
        in_specs=[row, vec, row],
        out_specs=[row, pl.BlockSpec((1, LANES), lambda i: (0, 0)), vec],
        out_shape=[jax.ShapeDtypeStruct((s, D_MODEL), F32), jax.ShapeDtypeStruct((1, LANES), F32),
                   jax.ShapeDtypeStruct((1, D_MODEL), F32)],
        compiler_params=_cp(("arbitrary",), 32),
    )(x, g, tgt)


def _block_diag(w):
    n, h, d, _ = w.shape
    return jnp.einsum('lhij,hk->lhikj', w, jnp.eye(h, dtype=w.dtype)).reshape(n, h * d, h * d)


def _diag_blocks(wfull, h):
    d = wfull.shape[-1] // h
    return jnp.stack([wfull[:, i * d:(i + 1) * d, i * d:(i + 1) * d] for i in range(h)], axis=1)


REF_TO_PERM = np.concatenate([np.arange(0, 1152), np.arange(1536, 1792),
                              np.arange(2304 + KR_LANE0, 2304 + KR_LANE0 + QK_ROPE),
                              np.arange(1152, 1536), np.arange(1792, 2304)])
N_SHARD = 4
W_IN_SHARD = D_IN // N_SHARD


def _w_in_runs():
    ref_of_perm = -np.ones(D_INP, np.int64)
    ref_of_perm[REF_TO_PERM] = np.arange(D_IN)
    perm_runs, p = [], 0
    while p < D_INP:
        r, q = ref_of_perm[p], p + 1
        if r < 0:
            while q < D_INP and ref_of_perm[q] < 0:
                q += 1
            perm_runs.append((None, q - p, 0))
        else:
            while (q < D_INP and ref_of_perm[q] == ref_of_perm[q - 1] + 1
                   and ref_of_perm[q] // W_IN_SHARD == r // W_IN_SHARD):
                q += 1
            perm_runs.append((int(r // W_IN_SHARD), int(r % W_IN_SHARD), int(r % W_IN_SHARD + q - p)))
        p = q
    shard_runs = []
    for s in range(N_SHARD):
        cols = REF_TO_PERM[s * W_IN_SHARD:(s + 1) * W_IN_SHARD]
        runs, a = [], 0
        for b in range(1, W_IN_SHARD + 1):
            if b == W_IN_SHARD or cols[b] != cols[b - 1] + 1:
                runs.append((int(cols[a]), int(cols[b - 1]) + 1))
                a = b
        shard_runs.append(runs)
    return perm_runs, shard_runs


def _permute_w_in(shards):
    perm_runs, _ = _w_in_runs()
    lead = shards[0].shape[:-1]
    parts = [jnp.zeros(lead + (a,), shards[0].dtype) if s is None else shards[s][..., a:b] for s, a, b in perm_runs]
    return jnp.concatenate(parts, axis=-1)


def _w_in_shard(wp, s):
    _, shard_runs = _w_in_runs()
    return jnp.concatenate([wp[..., a:b] for a, b in shard_runs[s]], axis=-1)


def _pad_w_uq(w):
    w4 = w.reshape(w.shape[:2] + (MLA_HEADS, QK_NOPE + QK_ROPE))
    return jnp.pad(w4, ((0, 0),) * 3 + ((0, HEAD_BLK - QK_NOPE - QK_ROPE),)).reshape(w.shape[:2] + (-1,))


def _unpad_w_uq(w):
    return w.reshape(w.shape[:2] + (MLA_HEADS, HEAD_BLK))[..., :QK_NOPE + QK_ROPE].reshape(w.shape[:2] + (-1,))


def _pad_w_ukv(w):
    w4 = w.reshape(w.shape[:2] + (MLA_HEADS, QK_NOPE + V_DIM))
    kpart = jnp.pad(w4[..., :QK_NOPE], ((0, 0),) * 3 + ((0, HEAD_BLK - QK_NOPE),)).reshape(w.shape[:2] + (-1,))
    return jnp.concatenate([kpart, w4[..., QK_NOPE:].reshape(w.shape[:2] + (-1,))], axis=2)


def _unpad_w_ukv(w):
    hq = MLA_HEADS * HEAD_BLK
    kpart = w[..., :hq].reshape(w.shape[:2] + (MLA_HEADS, HEAD_BLK))[..., :QK_NOPE]
    vpart = w[..., hq:].reshape(w.shape[:2] + (MLA_HEADS, V_DIM))
    return jnp.concatenate([kpart, vpart], axis=3).reshape(w.shape[:2] + (-1,))


def _local_step(x, tgt, w):
    s = x.shape[0]
    tabs = _rope_tables(s)
    vec = lambda a: a[:, None, :]
    mxu = lambda a: a.astype(MXU_DTYPE)
    p = dict(g=vec(w['norm_g']), w_in=mxu(w['w_in']), cw=w['conv_w'], cb=vec(w['conv_b']),
             wr=mxu(_block_diag(w['w_rg'])), br=vec(w['b_rg']), wi=mxu(_block_diag(w['w_ig'])), bi=vec(w['b_ig']),
             lam=vec(w['lru_lambda']), gq=vec(w['q_norm_g']), gkv=vec(w['kv_norm_g']),
             wuq=mxu(_pad_w_uq(w['w_uq'])), wukv=mxu(_pad_w_ukv(w['w_ukv'])),
             wp=mxu(_block_diag(w['w_pool'])), ps=vec(w['pool_scale']), wout=mxu(w['w_out']))
    lru = lambda l: (p['cw'], p['cb'], p['wr'], p['br'], p['wi'], p['bi'], p['lam'], l)
    mla = lambda l: (p['gq'], p['gkv'], p['wuq'], p['wukv'], *tabs, l)

    saved = []
    for l in range(DEPTH):
        h, z = _inproj_fwd(x, p['g'], p['w_in'], l)
        ya, hs = _lru_fwd(z, *lru(l))
        yc = _pool_fwd(z, p['wp'], p['ps'], l)
        q, k, v, qn, kvn = _mla_prep_fwd(z, *mla(l))
        o, yb, lse = _flash_fwd(q, k, v, z)
        saved.append(dict(x=x, h=h, z=z, hs=hs, ya=ya, yb=yb, yc=yc, q=q, k=k, v=v, qn=qn, kvn=kvn, o=o, lse=lse))
        x = _outproj_fwd(x, ya, yb, yc, p['wout'], l)

    dx, loss, dgf = _loss_head(x, w['final_norm_g'][None], tgt)
    per_layer = {n: [None] * DEPTH for n in WEIGHT_NAMES if n != 'final_norm_g'}
    for l in reversed(range(DEPTH)):
        sv = saved[l]
        dya, dyc, do, dgb, delta, per_layer['w_out'][l] = _outproj_bwd(
            dx, sv['ya'], sv['yb'], sv['yc'], sv['o'], sv['z'], p['wout'], l)
        dza, dga, *lru_grads = _lru_bwd(sv['z'], sv['hs'], dya, *lru(l))
        for n, g in zip(('w_rg', 'w_ig', 'conv_w', 'conv_b', 'b_rg', 'b_ig', 'lru_lambda'), lru_grads):
            per_layer[n][l] = g
        dzc, dgc, per_layer['w_pool'][l], per_layer['pool_scale'][l] = _pool_bwd(sv['z'], dyc, p['wp'], p['ps'], l)
        dq, dk, dv = _flash_bwd(sv['q'], sv['k'], sv['v'], do, sv['lse'], delta)
        (dcq, dckv, dkr, per_layer['w_uq'][l], per_layer['w_ukv'][l], per_layer['q_norm_g'][l],
         per_layer['kv_norm_g'][l]) = _mla_prep_bwd(dq, dk, dv, sv['z'], sv['qn'], sv['kvn'], *mla(l))
        dz_parts = (dza, dga, dcq, dgb, dckv, dzc, dgc, dkr)
        per_layer['w_in'][l] = _dwin(sv['h'], dz_parts)
        dx, per_layer['norm_g'][l] = _inproj_bwd(dz_parts, p['w_in'], sv['x'], p['g'], dx, l)
    grads = {n: jnp.stack(g) for n, g in per_layer.items()}
    for n in ('norm_g', 'conv_b', 'b_rg', 'b_ig', 'lru_lambda', 'q_norm_g', 'kv_norm_g', 'pool_scale'):
        grads[n] = grads[n][:, 0, :]
    grads['w_rg'] = _diag_blocks(grads['w_rg'], LRU_HEADS)
    grads['w_ig'] = _diag_blocks(grads['w_ig'], LRU_HEADS)
    grads['w_pool'] = _diag_blocks(grads['w_pool'], len(POOL_WINDOWS))
    grads['w_uq'] = _unpad_w_uq(grads['w_uq'])
    grads['w_ukv'] = _unpad_w_ukv(grads['w_ukv'])
    grads['final_norm_g'] = dgf[0]
    return loss[0, 0], dx, grads


WIRE_DTYPE = jnp.bfloat16
MESH_IDS = pl.DeviceIdType.MESH
_HBM = pl.BlockSpec(memory_space=pltpu.HBM)


def _coords():
    return lax.axis_index("x"), lax.axis_index("y"), lax.axis_index("c")


def _comm_call(body, name, arrays, out_shapes, copies_per_array):
    n = len(arrays)
    return pl.pallas_call(
        body, name=name, out_shape=out_shapes, in_specs=[_HBM] * n, out_specs=[_HBM] * n,
        scratch_shapes=[pltpu.SemaphoreType.DMA((n, copies_per_array)), pltpu.SemaphoreType.DMA((n, copies_per_array))],
    )(*arrays)


def _all_gather8(blocks, name):
    n = len(blocks)
    every = range(n)

    def body(*refs):
        x_refs, out_refs = refs[:n], refs[n:2 * n]
        send_sems, recv_sems = refs[2 * n:]
        x, y, c = _coords()
        me, sibling = (x, y, c), (x, y, 1 - c)
        chips = [(1 - x, y), (x, 1 - y), (1 - x, 1 - y)]

        def slot(t, px, py, pc):
            return out_refs[t].at[4 * px + 2 * py + pc]

        def copy(t, k, block, to, own=False):
            return pltpu.make_async_remote_copy(
                src_ref=x_refs[t] if own else slot(t, *block), dst_ref=slot(t, *block),
                send_sem=send_sems.at[t, k], recv_sem=recv_sems.at[t, k], device_id=to, device_id_type=MESH_IDS)

        first = [copy(t, 0, me, sibling, own=True) for t in every]
        first += [copy(t, 1 + j, me, (*chip, c), own=True) for j, chip in enumerate(chips) for t in every]
        for cp in first:
            cp.start()
        passed = [[copy(t, 4 + j, (*chip, c), sibling) for t in every] for j, chip in enumerate(chips)]
        for j, chip in enumerate(chips):
            for t in every:
                copy(t, 1 + j, (*chip, c), me).wait_recv()
                passed[j][t].start()
        for t in every:
            copy(t, 0, sibling, me).wait_recv()
        for j, chip in enumerate(chips):
            for t in every:
                copy(t, 4 + j, (*chip, 1 - c), me).wait_recv()
        for cp in first + [cp for group in passed for cp in group]:
            cp.wait_send()

    outs = [jax.ShapeDtypeStruct((N_DEV,) + b.shape, b.dtype) for b in blocks]
    got = _comm_call(body, name, blocks, outs, 7)
    me = 4 * lax.axis_index("x") + 2 * lax.axis_index("y") + lax.axis_index("c")
    return [lax.dynamic_update_index_in_dim(g, b, me, 0) for g, b in zip(got, blocks)]


def _sibling_send(arrs, name, which_half=False):
    n = len(arrs)

    def body(*refs):
        a_refs, out_refs = refs[:n], refs[n:2 * n]
        send_sems, recv_sems = refs[2 * n:]
        x, y, c = _coords()
        sent = [pltpu.make_async_remote_copy(
            src_ref=a_refs[t].at[1 - c] if which_half else a_refs[t], dst_ref=out_refs[t],
            send_sem=send_sems.at[t, 0], recv_sem=recv_sems.at[t, 0],
            device_id=(x, y, 1 - c), device_id_type=MESH_IDS) for t in range(n)]
        for cp in sent:
            cp.start()
        for cp in sent:
            cp.wait()

    shapes = [jax.ShapeDtypeStruct(a.shape[1:] if which_half else a.shape, a.dtype) for a in arrs]
    return _comm_call(body, name, arrs, shapes, 1)


def _chip_exchange(arrs, name):
    n = len(arrs)

    def body(*refs):
        a_refs, out_refs = refs[:n], refs[n:2 * n]
        send_sems, recv_sems = refs[2 * n:]
        x, y, c = _coords()
        copies = []
        for j, (cx, cy) in enumerate([(1 - x, y), (x, 1 - y), (1 - x, 1 - y)]):
            copies += [pltpu.make_async_remote_copy(
                src_ref=a_refs[t].at[2 * cx + cy], dst_ref=out_refs[t].at[j], send_sem=send_sems.at[t, j],
                recv_sem=recv_sems.at[t, j], device_id=(cx, cy, c), device_id_type=MESH_IDS) for t in range(n)]
        for cp in copies:
            cp.start()
        for cp in copies:
            cp.wait()

    return _comm_call(body, name, arrs, [jax.ShapeDtypeStruct((3,) + a.shape[1:], a.dtype) for a in arrs], 3)


def _sum_leading(groups, out_dtype, steps, name):
    flat = [a for g in groups for a in g]

    def body(*refs):
        ins, outs, pos = refs[:len(flat)], refs[len(flat):], 0
        for g, o_ref in zip(groups, outs):
            acc = None
            for i_ref in ins[pos:pos + len(g)]:
                for k in range(i_ref.shape[0]):
                    term = i_ref[k].astype(F32)
                    acc = term if acc is None else acc + term
            pos += len(g)
            o_ref[...] = acc.astype(o_ref.dtype)

    return pl.pallas_call(
        body, name=name, grid=(steps,),
        in_specs=[pl.BlockSpec((a.shape[0], a.shape[1] // steps, a.shape[2]), lambda i: (0, i, 0)) for a in flat],
        out_specs=[pl.BlockSpec((g[0].shape[1] // steps, g[0].shape[2]), lambda i: (i, 0)) for g in groups],
        out_shape=[jax.ShapeDtypeStruct(g[0].shape[1:], out_dtype) for g in groups],
        compiler_params=_cp(("parallel",), 40),
    )(*flat)


def _adamw_update(w_ref, g_ref, m_ref, v_ref, d_ref, mo_ref, vo_ref):
    gv = g_ref[...]
    mn = ADAM_B1 * m_ref[...] + (1.0 - ADAM_B1) * gv
    vn = ADAM_B2 * v_ref[...] + (1.0 - ADAM_B2) * (gv * gv)
    mo_ref[...] = mn
    vo_ref[...] = vn
    m_hat = mn / (1.0 - ADAM_B1 ** ADAM_STEP)
    v_hat = vn / (1.0 - ADAM_B2 ** ADAM_STEP)
    d_ref[...] = (-ADAM_LR) * (m_hat / (jnp.sqrt(v_hat) + ADAM_EPS) + ADAM_WD * w_ref[...])


def _adamw(w, g, m, v, name):
    n, r, cdim = w.shape
    tr = math.gcd(r, 512)

    def body(*refs):
        _adamw_update(*refs)

    blk = pl.BlockSpec((None, tr, cdim), lambda l, i: (l, i, 0))
    return pl.pallas_call(
        body, name=name, grid=(n, r // tr),
        in_specs=[blk] * 4, out_specs=[blk] * 3,
        out_shape=[jax.ShapeDtypeStruct(w.shape, F32)] * 3,
        compiler_params=_cp(("parallel", "parallel"), 40),
    )(w, g, m, v)


def _adamw_small(ws, gs, ms, vs, name):
    n = len(ws)

    def body(*refs):
        ins, outs = refs[:4 * n], refs[4 * n:]
        for t in range(n):
            _adamw_update(ins[t], ins[n + t], ins[2 * n + t], ins[3 * n + t], outs[t], outs[n + t], outs[2 * n + t])

    shapes = [jax.ShapeDtypeStruct(w.shape, F32) for w in ws]
    outs = pl.pallas_call(body, name=name, out_shape=shapes * 3)(*ws, *gs, *ms, *vs)
    return outs[:n], outs[n:2 * n], outs[2 * n:]


HALF = DEPTH // 2
BIG = ['w_in', 'w_uq', 'w_ukv', 'w_out']
SHARD_AXIS = {'w_in': 2, 'conv_w': 2, 'w_uq': 2, 'w_ukv': 2, 'w_out': 1}
FULL_SHAPE = {'w_in': (DEPTH, D_MODEL, D_IN), 'conv_w': (DEPTH, CONV_WIDTH, D_LRU),
              'w_uq': (DEPTH, Q_RANK, MLA_HEADS * (QK_NOPE + QK_ROPE)),
              'w_ukv': (DEPTH, KV_RANK, MLA_HEADS * (QK_NOPE + V_DIM)), 'w_out': (DEPTH, D_MIX, D_MODEL)}


def _shard_shape(n):
    shp = list(FULL_SHAPE[n])
    shp[SHARD_AXIS[n]] //= N_SHARD
    return tuple(shp)


def _rows_view(a, lead=0):
    return a.reshape(a.shape[:lead] + (-1, a.shape[-1]))


def _gather_weights(local):
    c = lax.axis_index("c")
    names = BIG + ['conv_w']
    halves = [lax.dynamic_slice_in_dim(local[n], HALF * c, HALF, axis=0) for n in names]
    halves = [h.astype(WIRE_DTYPE) if n in BIG else h for n, h in zip(names, halves)]
    got = _all_gather8(halves, "gather_weights")
    full = {}
    for n, g in zip(names, got):
        g = g.reshape((N_SHARD, DEPTH) + g.shape[2:])
        if n == 'w_in':
            full[n] = _permute_w_in([g[s] for s in range(N_SHARD)])
        else:
            full[n] = jnp.moveaxis(g, 0, SHARD_AXIS[n]).reshape(FULL_SHAPE[n])
    return full


def _shard_blocks(g, n):
    width = _shard_shape(n)[SHARD_AXIS[n]]

    def block(h, s):
        part = g[HALF * h:HALF * (h + 1)]
        if n == 'w_in':
            part = _w_in_shard(part, s)
        else:
            part = lax.slice_in_dim(part, s * width, (s + 1) * width, axis=SHARD_AXIS[n])
        return _rows_view(part)

    return jnp.stack([jnp.stack([block(h, s) for s in range(N_SHARD)]) for h in range(2)]).astype(WIRE_DTYPE)


SUM_STEPS = 8


def _reduce_big(grads):
    c = lax.axis_index("c")
    shard = 2 * lax.axis_index("x") + lax.axis_index("y")
    contrib = [_shard_blocks(grads[n], n) for n in BIG]
    from_sibling = _sibling_send(contrib, "pair_exchange_big", which_half=True)
    own_half = [lax.dynamic_index_in_dim(a, c, 0, keepdims=False) for a in contrib]
    pair_sum = _sum_leading([[_rows_view(o)[None], _rows_view(r)[None]] for o, r in zip(own_half, from_sibling)],
                            WIRE_DTYPE, SUM_STEPS, "pair_sum_big")
    to_chips = [p.reshape(a.shape[1:]) for p, a in zip(pair_sum, contrib)]
    from_chips = _chip_exchange(to_chips, "chip_exchange_big")
    own_block = [lax.dynamic_index_in_dim(a, shard, 0, keepdims=True) for a in to_chips]
    mine = _sum_leading([[o, r] for o, r in zip(own_block, from_chips)], F32, SUM_STEPS, "chip_sum_big")
    theirs = _sibling_send(mine, "sibling_big")
    both = [jnp.where(c == 0, jnp.stack([m, t]), jnp.stack([t, m])) for m, t in zip(mine, theirs)]
    return {n: b.reshape(_shard_shape(n)) for n, b in zip(BIG, both)}


SMALL = REPLICATED + ['conv_w']


def _reduce_small(grads):
    views = [grads[n].reshape(-1, LANES) if grads[n].shape[-1] < LANES else _rows_view(jnp.atleast_2d(grads[n]))
             for n in SMALL]
    sums = _sum_leading([[g] for g in _all_gather8(views, "gather_small")], F32, 1, "sum_small")
    return {n: s.reshape(grads[n].shape) for n, s in zip(SMALL, sums)}


def kernel(x, norm_g, w_in, conv_w, conv_b, w_rg, b_rg, w_ig, b_ig, lru_lambda, q_norm_g, w_uq, kv_norm_g, w_ukv, w_pool, pool_scale, w_out, final_norm_g, loss_target, m_norm_g, m_w_in, m_conv_w, m_conv_b, m_w_rg, m_b_rg, m_w_ig, m_b_ig, m_lru_lambda, m_q_norm_g, m_w_uq, m_kv_norm_g, m_w_ukv, m_w_pool, m_pool_scale, m_w_out, m_final_norm_g, v_norm_g, v_w_in, v_conv_w, v_conv_b, v_w_rg, v_b_rg, v_w_ig, v_b_ig, v_lru_lambda, v_q_norm_g, v_w_uq, v_kv_norm_g, v_w_ukv, v_w_pool, v_pool_scale, v_w_out, v_final_norm_g):
    w_loc = dict(zip(WEIGHT_NAMES, (norm_g, w_in, conv_w, conv_b, w_rg, b_rg, w_ig, b_ig, lru_lambda, q_norm_g, w_uq,
                                    kv_norm_g, w_ukv, w_pool, pool_scale, w_out, final_norm_g)))
    m_loc = dict(zip(WEIGHT_NAMES, (m_norm_g, m_w_in, m_conv_w, m_conv_b, m_w_rg, m_b_rg, m_w_ig, m_b_ig, m_lru_lambda,
                                    m_q_norm_g, m_w_uq, m_kv_norm_g, m_w_ukv, m_w_pool, m_pool_scale, m_w_out,
                                    m_final_norm_g)))
    v_loc = dict(zip(WEIGHT_NAMES, (v_norm_g, v_w_in, v_conv_w, v_conv_b, v_w_rg, v_b_rg, v_w_ig, v_b_ig, v_lru_lambda,
                                    v_q_norm_g, v_w_uq, v_kv_norm_g, v_w_ukv, v_w_pool, v_pool_scale, v_w_out,
                                    v_final_norm_g)))
    w_full = dict(w_loc)
    w_full.update(_gather_weights(w_loc))
    loss_local, dx, g_local = _local_step(x[0], loss_target[0], w_full)
    loss = lax.psum(loss_local, ("x", "y", "c"))

    grads = _reduce_big(g_local)
    g_small = _reduce_small(g_local)
    shard = 2 * lax.axis_index("x") + lax.axis_index("y")
    width = D_LRU // N_SHARD
    grads['conv_w'] = lax.dynamic_slice_in_dim(g_small['conv_w'], shard * width, width, axis=2)
    for n in REPLICATED:
        grads[n] = g_small[n]

    delta, new_m, new_v = {}, {}, {}
    for n in BIG:
        delta[n], new_m[n], new_v[n] = _adamw(w_loc[n], grads[n], m_loc[n], v_loc[n], "adamw_" + n)
    small = [[_rows_view(jnp.atleast_2d(t[n])) for n in SMALL] for t in (w_loc, grads, m_loc, v_loc)]
    for tree, outs in zip((delta, new_m, new_v), _adamw_small(*small, "adamw_small")):
        tree.update({n: a.reshape(w_loc[n].shape) for n, a in zip(SMALL, outs)})

    return (loss, dx[None], *[grads[n] for n in WEIGHT_NAMES], *[delta[n] for n in WEIGHT_NAMES],
            *[new_m[n] for n in WEIGHT_NAMES], *[new_v[n] for n in WEIGHT_NAMES])
```

```python
import functools
import math

import jax
import jax.numpy as jnp
import numpy as np
from jax import lax
from jax.experimental import pallas as pl
from jax.experimental.pallas import tpu as pltpu

F32 = jnp.float32
MXU_DTYPE = jnp.bfloat16

D_MODEL = 1024
DEPTH = 4
EPS = 1e-6
D_LRU = 384
LRU_HEADS = 6
CONV_WIDTH = 4
LRU_C = 8.0
MLA_HEADS = 6
QK_NOPE = 64
QK_ROPE = 32
V_DIM = 64
D_MLA = MLA_HEADS * V_DIM
Q_RANK = 384
KV_RANK = 256
ROPE_BASE = 10000.0
POOL_WINDOWS = (2, 4, 8, 16)
D_POOL = 256
D_MIX = D_LRU + D_MLA + D_POOL
D_IN = 2336
ATT_SCALE = (QK_NOPE + QK_ROPE) ** -0.5
LOG2_E = 1.4426950408889634
LN_2 = 0.6931471805599453
Q_PRESCALE = ATT_SCALE * LOG2_E

ADAM_LR = 0.001
ADAM_B1 = 0.9
ADAM_B2 = 0.999
ADAM_EPS = 1e-08
ADAM_WD = 0.01
ADAM_STEP = 10

LANES = 128
SUBLANES = 8
V7X_VMEM_BYTES = 64 << 20
N_DEV = 8

D_INP = 2432
KR_LANE0 = 64
HEAD_BLK = 128
N_PAIR = MLA_HEADS // 2

WEIGHT_NAMES = ['norm_g', 'w_in', 'conv_w', 'conv_b', 'w_rg', 'b_rg', 'w_ig', 'b_ig', 'lru_lambda', 'q_norm_g',
                'w_uq', 'kv_norm_g', 'w_ukv', 'w_pool', 'pool_scale', 'w_out', 'final_norm_g']
SHARDED = ['w_in', 'conv_w', 'w_uq', 'w_ukv', 'w_out']
REPLICATED = [n for n in WEIGHT_NAMES if n not in SHARDED]


def _cp(sem, vmem_mb=None):
    return pltpu.CompilerParams(dimension_semantics=sem,
                                vmem_limit_bytes=None if vmem_mb is None else vmem_mb << 20)


def _dot(a, b):
    return jnp.dot(a.astype(MXU_DTYPE), b.astype(MXU_DTYPE), preferred_element_type=F32)


def _dot_nt(a, b):
    return lax.dot_general(a.astype(MXU_DTYPE), b.astype(MXU_DTYPE), (((1,), (1,)), ((), ())),
                           preferred_element_type=F32)


def _dot_tn(a, b):
    return lax.dot_general(a.astype(MXU_DTYPE), b.astype(MXU_DTYPE), (((0,), (0,)), ((), ())),
                           preferred_element_type=F32)


def _sig(x):
    return 0.5 * jnp.tanh(0.5 * x) + 0.5


def _down(x, k):
    return pltpu.roll(x, k, 0)


def _up(x, k):
    return pltpu.roll(x, x.shape[0] - k, 0)


def _rows(shape):
    return lax.broadcasted_iota(jnp.int32, shape, 0)


def _lanes(shape):
    return lax.broadcasted_iota(jnp.int32, shape, 1)


def _tile(s):
    return min(512, s)


def _lay(l):
    return lambda shp, blk=0: pl.BlockSpec((None,) + shp, lambda *_: (l, blk, 0))


def _inproj_fwd(x, g, w, l):
    s = x.shape[0]
    t = _tile(s)
    lay = _lay(l)

    def body(x_ref, g_ref, w_ref, h_ref, z_ref):
        xv = x_ref[...]
        rs = lax.rsqrt(jnp.mean(xv * xv, axis=-1, keepdims=True) + EPS)
        h = (xv * rs * g_ref[...]).astype(MXU_DTYPE)
        h_ref[...] = h
        z_ref[...] = jnp.dot(h, w_ref[...], preferred_element_type=F32)

    return pl.pallas_call(
        body, name="inproj_fwd", grid=(s // t,),
        in_specs=[pl.BlockSpec((t, D_MODEL), lambda i: (i, 0)),
                  lay((1, D_MODEL)), lay((D_MODEL, D_INP))],
        out_specs=[pl.BlockSpec((t, D_MODEL), lambda i: (i, 0)),
                   pl.BlockSpec((t, D_INP), lambda i: (i, 0))],
        out_shape=[jax.ShapeDtypeStruct((s, D_MODEL), MXU_DTYPE), jax.ShapeDtypeStruct((s, D_INP), F32)],
        compiler_params=_cp(("parallel",), 40),
    )(x, g, w)


def _lru_gates(za, halo, cw_ref, cb_ref, wr_ref, br_ref, wi_ref, bi_ref, lam_ref):
    t = za.shape[0]
    ext = jnp.concatenate([halo, za], axis=0)
    sh = [za] + [_down(ext, j)[SUBLANES:SUBLANES + t] for j in (1, 2, 3)]
    xa = cb_ref[...] + cw_ref[3:4, :] * sh[0] + cw_ref[2:3, :] * sh[1] + cw_ref[1:2, :] * sh[2] + cw_ref[0:1, :] * sh[3]
    r = 1.0 / (1.0 + jnp.exp(-(_dot(xa, wr_ref[...]) + br_ref[...])))
    ig = _sig(_dot(xa, wi_ref[...]) + bi_ref[...])
    lam = lam_ref[...]
    sp = jnp.maximum(-lam, 0.0) + jnp.log(1.0 + jnp.exp(-jnp.abs(lam)))
    la = (-LRU_C) * r * sp
    a = jnp.exp(la)
    y2 = 2.0 * la
    m2 = jnp.where(y2 > -0.01, -(y2 * (1.0 + y2 * (0.5 + y2 * (1.0 / 6.0)))), 1.0 - a * a)
    return xa, sh, r, ig, sp, a, jnp.sqrt(m2), m2


def _lru_fwd(z, cw, cb, wr, br, wi, bi, lam, l):
    s = z.shape[0]
    t = _tile(s)
    c = D_LRU
    lay = _lay(l)

    def body(za_ref, ga_ref, cw_ref, cb_ref, wr_ref, br_ref, wi_ref, bi_ref, lam_ref, ya_ref, hs_ref, zprev, hcar):
        i = pl.program_id(0)

        @pl.when(i == 0)
        def _():
            zprev[...] = jnp.zeros_like(zprev)
            hcar[...] = jnp.zeros_like(hcar)

        za = za_ref[...]
        xa, _, _, ig, _, a, m, _ = _lru_gates(za, zprev[...], cw_ref, cb_ref, wr_ref, br_ref, wi_ref, bi_ref, lam_ref)
        row = _rows((t, c))
        acc_h = m * (ig * xa) + jnp.where(row == 0, a * hcar[...], 0.0)
        acc_a = jnp.where(row == 0, 0.0, a)
        k = 1
        while k < t:
            acc_h = acc_h + acc_a * _down(acc_h, k)
            acc_a = acc_a * _down(acc_a, k)
            k *= 2
        hs = acc_h
        hs_ref[...] = hs
        ga = ga_ref[...]
        ya_ref[...] = (hs * (ga * _sig(ga))).astype(ya_ref.dtype)
        hcar[...] = jnp.sum(jnp.where(row == t - 1, hs, 0.0), axis=0, keepdims=True)
        zprev[...] = za_ref[t - SUBLANES:t, :]

    return pl.pallas_call(
        body, name="lru_fwd", grid=(s // t,),
        in_specs=[pl.BlockSpec((t, c), lambda i: (i, 0)), pl.BlockSpec((t, c), lambda i: (i, 1)),
                  lay((CONV_WIDTH, c)), lay((1, c)), lay((c, c)), lay((1, c)), lay((c, c)), lay((1, c)),
                  lay((1, c))],
        out_specs=[pl.BlockSpec((t, c), lambda i: (i, 0)), pl.BlockSpec((t, c), lambda i: (i, 0))],
        out_shape=[jax.ShapeDtypeStruct((s, c), MXU_DTYPE), jax.ShapeDtypeStruct((s, c), F32)],
        scratch_shapes=[pltpu.VMEM((SUBLANES, c), F32), pltpu.VMEM((1, c), F32)],
        compiler_params=_cp(("arbitrary",), 40),
    )(z, z, cw, cb, wr, br, wi, bi, lam)


def _lru_bwd(z, hs, dy, cw, cb, wr, br, wi, bi, lam, l):
    s = z.shape[0]
    t = _tile(s)
    lay = _lay(l)
    nt = s // t
    c = D_LRU
    hb = t // SUBLANES

    def body(za_ref, zh_ref, ga_ref, hs_ref, hh_ref, dy_ref, cw_ref, cb_ref, wr_ref, br_ref, wi_ref, bi_ref, lam_ref,
             dza_ref, dga_ref, dwr_ref, dwi_ref, dcw_ref, dcb_ref, dbr_ref, dbi_ref, dlam_ref, lcar, dxn):
        i = pl.program_id(0)
        tt = nt - 1 - i

        @pl.when(i == 0)
        def _():
            lcar[...] = jnp.zeros_like(lcar)
            dxn[...] = jnp.zeros_like(dxn)
            for ref in (dwr_ref, dwi_ref, dcw_ref, dcb_ref, dbr_ref, dbi_ref, dlam_ref):
                ref[...] = jnp.zeros_like(ref)

        first = (tt > 0).astype(F32)
        za = za_ref[...]
        xa, sh, r, ig, sp, a, m, m2 = _lru_gates(za, zh_ref[...] * first, cw_ref, cb_ref, wr_ref, br_ref, wi_ref,
                                                 bi_ref, lam_ref)
        hs_v = hs_ref[...]
        hprev = _down(jnp.concatenate([hh_ref[...] * first, hs_v], axis=0), 1)[SUBLANES:SUBLANES + t]
        ga = ga_ref[...]
        sg = _sig(ga)
        silu = ga * sg
        dya = dy_ref[...]
        dga_ref[...] = (dya * hs_v * (sg * (1.0 + ga * (1.0 - sg)))).astype(dga_ref.dtype)
        row = _rows((t, c))
        acc_h = dya * silu + jnp.where(row == t - 1, lcar[...], 0.0)
        acc_b = jnp.where(row < t - 1, _up(a, 1), 0.0)
        k = 1
        while k < t:
            acc_h = acc_h + acc_b * _up(acc_h, k)
            acc_b = acc_b * _up(acc_b, k)
            k *= 2
        lmb = acc_h
        lcar[...] = jnp.sum(jnp.where(row == 0, a * lmb, 0.0), axis=0, keepdims=True)
        da = lmb * hprev
        dxa = lmb * m * ig
        di = lmb * m * xa
        dm = lmb * ig * xa
        dla = da * a - dm * (a * a) * lax.rsqrt(m2)
        dr = dla * ((-LRU_C) * sp)
        lam = lam_ref[...]
        dsp = jnp.sum(dla * ((-LRU_C) * r), axis=0, keepdims=True)
        dlam_ref[...] += dsp * (-1.0 / (1.0 + jnp.exp(lam)))
        dpr = dr * r * (1.0 - r)
        dpi = di * ig * (1.0 - ig)
        dbr_ref[...] += jnp.sum(dpr, axis=0, keepdims=True)
        dbi_ref[...] += jnp.sum(dpi, axis=0, keepdims=True)
        dwr_ref[...] += _dot_tn(xa, dpr)
        dwi_ref[...] += _dot_tn(xa, dpi)
        dxa = dxa + _dot_nt(dpr, wr_ref[...]) + _dot_nt(dpi, wi_ref[...])
        dcb_ref[...] += jnp.sum(dxa, axis=0, keepdims=True)
        for k in range(CONV_WIDTH):
            dcw_ref[k:k + 1, :] += jnp.sum(dxa * sh[CONV_WIDTH - 1 - k], axis=0, keepdims=True)
        ext = jnp.concatenate([dxa, dxn[...]], axis=0)
        dza = cw_ref[3:4, :] * dxa
        for j in (1, 2, 3):
            dza = dza + cw_ref[3 - j:4 - j, :] * _up(ext, j)[:t]
        dza_ref[...] = dza.astype(dza_ref.dtype)
        dxn[...] = dxa[:SUBLANES]

    full = lambda shp: pl.BlockSpec(shp, lambda i: (0, 0))
    rev = lambda i: nt - 1 - i
    halo = lambda i: (jnp.maximum((nt - 1 - i) * hb - 1, 0), 0)
    outs = pl.pallas_call(
        body, name="lru_bwd", grid=(nt,),
        in_specs=[pl.BlockSpec((t, c), lambda i: (rev(i), 0)), pl.BlockSpec((SUBLANES, c), halo),
                  pl.BlockSpec((t, c), lambda i: (rev(i), 1)),
                  pl.BlockSpec((t, c), lambda i: (rev(i), 0)), pl.BlockSpec((SUBLANES, c), halo),
                  pl.BlockSpec((t, c), lambda i: (rev(i), 0)),
                  lay((CONV_WIDTH, c)), lay((1, c)), lay((c, c)), lay((1, c)), lay((c, c)), lay((1, c)),
                  lay((1, c))],
        out_specs=[pl.BlockSpec((t, c), lambda i: (rev(i), 0)), pl.BlockSpec((t, c), lambda i: (rev(i), 0)),
                   full((c, c)), full((c, c)), full((CONV_WIDTH, c)), full((1, c)), full((1, c)), full((1, c)),
                   full((1, c))],
        out_shape=[jax.ShapeDtypeStruct((s, c), MXU_DTYPE), jax.ShapeDtypeStruct((s, c), MXU_DTYPE),
                   jax.ShapeDtypeStruct((c, c), F32), jax.ShapeDtypeStruct((c, c), F32),
                   jax.ShapeDtypeStruct((CONV_WIDTH, c), F32)] + [jax.ShapeDtypeStruct((1, c), F32)] * 4,
        scratch_shapes=[pltpu.VMEM((1, c), F32), pltpu.VMEM((SUBLANES, c), F32)],
        compiler_params=_cp(("arbitrary",), 48),
    )(z, z, z, hs, hs, dy, cw, cb, wr, br, wi, bi, lam)
    return outs


POOL_HALO = 16


def _pool_select(lane, v2, v4, v8, v16):
    return jnp.where(lane < 64, v2, jnp.where(lane < 128, v4, jnp.where(lane < 192, v8, v16)))


def _pool_counts(t0, t, c):
    lane = _lanes((t, c))
    win = _pool_select(lane, 2.0, 4.0, 8.0, 16.0)
    seen = (t0 + _rows((t, c)) + 1).astype(F32)
    return lane, jnp.minimum(seen, win)


def _pooled(zc, halo, lane, cnt):
    t = zc.shape[0]
    ext = jnp.concatenate([halo, zc], axis=0)
    s2 = ext + _down(ext, 1)
    s4 = s2 + _down(s2, 2)
    s8 = s4 + _down(s4, 4)
    s16 = s8 + _down(s8, 8)
    cut = lambda v: v[POOL_HALO:POOL_HALO + t]
    return _pool_select(lane, cut(s2), cut(s4), cut(s8), cut(s16)) / cnt - zc


def _pool_fwd(z, wp, ps, l):
    s = z.shape[0]
    t = _tile(s)
    c = D_POOL
    lay = _lay(l)

    def body(zc_ref, gc_ref, wp_ref, ps_ref, yc_ref, zprev):
        i = pl.program_id(0)

        @pl.when(i == 0)
        def _():
            zprev[...] = jnp.zeros_like(zprev)

        zc = zc_ref[...]
        lane, cnt = _pool_counts(i * t, t, c)
        pooled = _pooled(zc, zprev[...], lane, cnt)
        pc = _dot(pooled, wp_ref[...])
        gc = gc_ref[...]
        yc_ref[...] = (pc * ps_ref[...] * (gc * _sig(gc))).astype(yc_ref.dtype)
        zprev[...] = zc_ref[t - POOL_HALO:t, :]

    return pl.pallas_call(
        body, name="pool_fwd", grid=(s // t,),
        in_specs=[pl.BlockSpec((t, c), lambda i: (i, 7)), pl.BlockSpec((t, c), lambda i: (i, 8)),
                  lay((c, c)), lay((1, c))],
        out_specs=pl.BlockSpec((t, c), lambda i: (i, 0)),
        out_shape=jax.ShapeDtypeStruct((s, c), MXU_DTYPE),
        scratch_shapes=[pltpu.VMEM((POOL_HALO, c), F32)],
        compiler_params=_cp(("arbitrary",), 32),
    )(z, z, wp, ps)


def _pool_bwd(z, dy, wp, ps, l):
    s = z.shape[0]
    t = _tile(s)
    lay = _lay(l)
    nt = s // t
    c = D_POOL
    hb = t // POOL_HALO

    def body(zc_ref, zh_ref, gc_ref, dy_ref, wp_ref, ps_ref, dzc_ref, dgc_ref, dwp_ref, dps_ref, ddn):
        i = pl.program_id(0)
        tt = nt - 1 - i

        @pl.when(i == 0)
        def _():
            ddn[...] = jnp.zeros_like(ddn)
            dwp_ref[...] = jnp.zeros_like(dwp_ref)
            dps_ref[...] = jnp.zeros_like(dps_ref)

        first = (tt > 0).astype(F32)
        zc = zc_ref[...]
        lane, cnt = _pool_counts(tt * t, t, c)
        pooled = _pooled(zc, zh_ref[...] * first, lane, cnt)
        pc = _dot(pooled, wp_ref[...])
        gc = gc_ref[...]
        sg = _sig(gc)
        silu = gc * sg
        dyc = dy_ref[...]
        ps_v = ps_ref[...]
        dgc_ref[...] = (dyc * pc * ps_v * (sg * (1.0 + gc * (1.0 - sg)))).astype(dgc_ref.dtype)
        dps_ref[...] += jnp.sum(dyc * pc * silu, axis=0, keepdims=True)
        dpc = dyc * ps_v * silu
        dwp_ref[...] += _dot_tn(pooled, dpc)
        dpooled = _dot_nt(dpc, wp_ref[...])
        dd = dpooled / cnt
        ext = jnp.concatenate([dd, ddn[...]], axis=0)
        f2 = ext + _up(ext, 1)
        f4 = f2 + _up(f2, 2)
        f8 = f4 + _up(f4, 4)
        f16 = f8 + _up(f8, 8)
        dzc = _pool_select(lane, f2[:t], f4[:t], f8[:t], f16[:t]) - dpooled
        dzc_ref[...] = dzc.astype(dzc_ref.dtype)
        ddn[...] = dd[:POOL_HALO]

    full = lambda shp: pl.BlockSpec(shp, lambda i: (0, 0))
    rev = lambda i: nt - 1 - i
    return pl.pallas_call(
        body, name="pool_bwd", grid=(nt,),
        in_specs=[pl.BlockSpec((t, c), lambda i: (rev(i), 7)),
                  pl.BlockSpec((POOL_HALO, c), lambda i: (jnp.maximum(rev(i) * hb - 1, 0), 7)),
                  pl.BlockSpec((t, c), lambda i: (rev(i), 8)),
                  pl.BlockSpec((t, c), lambda i: (rev(i), 0)),
                  lay((c, c)), lay((1, c))],
        out_specs=[pl.BlockSpec((t, c), lambda i: (rev(i), 0)), pl.BlockSpec((t, c), lambda i: (rev(i), 0)),
                   full((c, c)), full((1, c))],
        out_shape=[jax.ShapeDtypeStruct((s, c), MXU_DTYPE), jax.ShapeDtypeStruct((s, c), MXU_DTYPE),
                   jax.ShapeDtypeStruct((c, c), F32), jax.ShapeDtypeStruct((1, c), F32)],
        scratch_shapes=[pltpu.VMEM((POOL_HALO, c), F32)],
        compiler_params=_cp(("arbitrary",), 32),
    )(z, z, z, dy, wp, ps)


def _rope_tables(s):
    pos = jnp.arange(s, dtype=F32)
    inv_freq = ROPE_BASE ** (-jnp.arange(0, QK_ROPE, 2, dtype=F32) / QK_ROPE)
    ang = pos[:, None] * inv_freq[None, :]
    cos, sin = jnp.cos(ang), jnp.sin(ang)
    half = QK_ROPE // 2
    ones = jnp.ones((s, QK_NOPE), F32)
    z64 = jnp.zeros((s, QK_NOPE), F32)
    zh = jnp.zeros((s, half), F32)
    z32 = jnp.zeros((s, HEAD_BLK - QK_NOPE - QK_ROPE), F32)
    c_t = jnp.concatenate([ones, cos, cos, z32], axis=1)
    s1_t = jnp.concatenate([z64, -sin, zh, z32], axis=1)
    s2_t = jnp.concatenate([z64, zh, sin, z32], axis=1)
    return c_t, s1_t, s2_t


def _rope(x, c_t, s1_t, s2_t):
    return x * c_t + pltpu.roll(x, HEAD_BLK - 16, 1) * s1_t + pltpu.roll(x, 16, 1) * s2_t


def _unrope(d, c_t, s1_t, s2_t):
    return d * c_t + pltpu.roll(d * s1_t, 16, 1) + pltpu.roll(d * s2_t, HEAD_BLK - 16, 1)


def _mla_prep_fwd(z, gq, gkv, wuq, wukv, c_t, s1_t, s2_t, l):
    s = z.shape[0]
    t = _tile(s)
    hq = MLA_HEADS * HEAD_BLK
    lay = _lay(l)

    def body(cq_ref, ckv_ref, kr_ref, gq_ref, gkv_ref, wuq_ref, wukv_ref, c_ref, s1_ref, s2_ref,
             q_ref, k_ref, v_ref, qn_ref, kvn_ref):
        ct, s1, s2 = c_ref[...], s1_ref[...], s2_ref[...]
        cq = cq_ref[...]
        qn = (cq * lax.rsqrt(jnp.mean(cq * cq, axis=-1, keepdims=True) + EPS) * gq_ref[...]).astype(MXU_DTYPE)
        qn_ref[...] = qn
        q = jnp.dot(qn, wuq_ref[...], preferred_element_type=F32)
        ckv = ckv_ref[...]
        kvn = (ckv * lax.rsqrt(jnp.mean(ckv * ckv, axis=-1, keepdims=True) + EPS) * gkv_ref[...]).astype(MXU_DTYPE)
        kvn_ref[...] = kvn
        kvp = jnp.dot(kvn, wukv_ref[...], preferred_element_type=F32)
        krr = _rope(kr_ref[...], ct, s1, s2)
        for h in range(MLA_HEADS):
            blk = slice(h * HEAD_BLK, (h + 1) * HEAD_BLK)
            q_ref[:, blk] = (_rope(q[:, blk], ct, s1, s2) * Q_PRESCALE).astype(q_ref.dtype)
            k_ref[:, blk] = (kvp[:, blk] + krr).astype(k_ref.dtype)
        v_ref[...] = kvp[:, hq:].astype(v_ref.dtype)

    tab = pl.BlockSpec((t, HEAD_BLK), lambda i: (i, 0))
    return pl.pallas_call(
        body, name="mla_prep_fwd", grid=(s // t,),
        in_specs=[pl.BlockSpec((t, Q_RANK), lambda i: (i, 2)), pl.BlockSpec((t, KV_RANK), lambda i: (i, 6)),
                  pl.BlockSpec((t, HEAD_BLK), lambda i: (i, 18)),
                  lay((1, Q_RANK)), lay((1, KV_RANK)), lay((Q_RANK, hq)), lay((KV_RANK, hq + D_MLA)),
                  tab, tab, tab],
        out_specs=[pl.BlockSpec((t, hq), lambda i: (i, 0)), pl.BlockSpec((t, hq), lambda i: (i, 0)),
                   pl.BlockSpec((t, D_MLA), lambda i: (i, 0)), pl.BlockSpec((t, Q_RANK), lambda i: (i, 0)),
                   pl.BlockSpec((t, KV_RANK), lambda i: (i, 0))],
        out_shape=[jax.ShapeDtypeStruct((s, hq), MXU_DTYPE), jax.ShapeDtypeStruct((s, hq), MXU_DTYPE),
                   jax.ShapeDtypeStruct((s, D_MLA), MXU_DTYPE), jax.ShapeDtypeStruct((s, Q_RANK), MXU_DTYPE),
                   jax.ShapeDtypeStruct((s, KV_RANK), MXU_DTYPE)],
        compiler_params=_cp(("parallel",), 40),
    )(z, z, z, gq, gkv, wuq, wukv, c_t, s1_t, s2_t)


SUM_LANE_A = V_DIM
SUM_LANE_B = 0
FLASH_TILES_PER_TRIP = 4


def _loop_in_trips(lo, hi, per_trip, step, carry):
    whole = (hi - lo) // per_trip

    def trip(n, cr):
        for u in range(per_trip):
            cr = step(lo + per_trip * n + u, cr)
        return cr

    carry = lax.fori_loop(0, whole, trip, carry)
    return lax.fori_loop(lo + per_trip * whole, hi, step, carry)


def _flash_fwd(q, k, v, z):
    s = q.shape[0]
    t = _tile(s)
    nq = s // t
    pw = 2 * HEAD_BLK

    def body(q_ref, k_ref, v_ref, gb_ref, o_ref, yb_ref, lse_ref):
        i = pl.program_id(1)
        qv = q_ref[...]
        qa, qb = qv[:, :HEAD_BLK], qv[:, HEAD_BLK:]
        lane = _lanes((t, HEAD_BLK))
        lo = lane < V_DIM

        def update(qh, kh, vh, m, acc, masked):
            sc = _dot_nt(qh, kh)
            if masked:
                sc = jnp.where(_lanes((t, t)) <= _rows((t, t)), sc, -1e30)
            m_new = jnp.maximum(m, jnp.max(sc, axis=-1, keepdims=True))
            p = jnp.exp2(sc - m_new).astype(MXU_DTYPE)
            return m_new, acc * jnp.exp2(m - m_new) + _dot(p, vh)

        def step(j, carry, masked):
            ma, mb, acc_a, acc_b = carry
            kv_rows = pl.ds(pl.multiple_of(j * t, t), t)
            kt = k_ref[kv_rows, :]
            vt = v_ref[kv_rows, :]
            lane_v = _lanes(vt.shape)
            one = jnp.ones_like(vt)
            zero_v = jnp.zeros_like(vt)
            v_a = jnp.where(lane_v < V_DIM, vt, jnp.where(lane_v == SUM_LANE_A, one, zero_v))
            v_b = jnp.where(lane_v >= V_DIM, vt, jnp.where(lane_v == SUM_LANE_B, one, zero_v))
            ma, acc_a = update(qa, kt[:, :HEAD_BLK], v_a, ma, acc_a, masked)
            mb, acc_b = update(qb, kt[:, HEAD_BLK:], v_b, mb, acc_b, masked)
            return ma, mb, acc_a, acc_b

        neg = jnp.full((t, 1), -1e30, F32)
        zero = jnp.zeros((t, HEAD_BLK), F32)
        carry = _loop_in_trips(0, i, FLASH_TILES_PER_TRIP, lambda j, cr: step(j, cr, False), (neg, neg, zero, zero))
        ma, mb, acc_a, acc_b = step(i, carry, True)
        la = jnp.sum(jnp.where(lane == SUM_LANE_A, acc_a, 0.0), axis=-1, keepdims=True)
        lb = jnp.sum(jnp.where(lane == SUM_LANE_B, acc_b, 0.0), axis=-1, keepdims=True)
        o = jnp.where(lo, acc_a * (1.0 / la), acc_b * (1.0 / lb))
        o_ref[...] = o
        gb = gb_ref[...]
        yb_ref[...] = (o * (gb * _sig(gb))).astype(yb_ref.dtype)
        lse = jnp.where(lo, ma + jnp.log(la) * LOG2_E, mb + jnp.log(lb) * LOG2_E)
        pick = ((_rows((SUBLANES, HEAD_BLK)) == 0) & (_lanes((SUBLANES, HEAD_BLK)) == 0)) | (
            (_rows((SUBLANES, HEAD_BLK)) == 1) & (_lanes((SUBLANES, HEAD_BLK)) == V_DIM))
        lse_ref[0, 0] = lax.dot_general(pick.astype(F32), lse, (((1,), (1,)), ((), ())),
                                        precision=lax.Precision.HIGHEST, preferred_element_type=F32)

    return pl.pallas_call(
        body, name="flash_fwd", grid=(N_PAIR, nq),
        in_specs=[pl.BlockSpec((t, pw), lambda p, i: (i, p)), pl.BlockSpec((s, pw), lambda p, i: (0, p)),
                  pl.BlockSpec((s, HEAD_BLK), lambda p, i: (0, p)),
                  pl.BlockSpec((t, HEAD_BLK), lambda p, i: (i, 9 + p))],
        out_specs=[pl.BlockSpec((t, HEAD_BLK), lambda p, i: (i, p)), pl.BlockSpec((t, HEAD_BLK), lambda p, i: (i, p)),
                   pl.BlockSpec((1, 1, SUBLANES, t), lambda p, i: (p, i, 0, 0))],
        out_shape=[jax.ShapeDtypeStruct((s, D_MLA), F32), jax.ShapeDtypeStruct((s, D_MLA), MXU_DTYPE),
                   jax.ShapeDtypeStruct((N_PAIR, nq, SUBLANES, t), F32)],
        compiler_params=_cp(("parallel", "parallel"), 48),
    )(q, k, v, z)


def _flash_bwd(q, k, v, do, lse, delta):
    s = q.shape[0]
    t = _tile(s)
    nq = s // t
    pw = 2 * HEAD_BLK

    def body(q_ref, do_ref, lse_ref, dl_ref, k_ref, v_ref, dq_ref, dk_ref, dv_ref):
        j = pl.program_id(1)

        @pl.when(j == 0)
        def _():
            dq_ref[...] = jnp.zeros_like(dq_ref)

        kt = k_ref[...]
        ka, kb = kt[:, :HEAD_BLK], kt[:, HEAD_BLK:]
        vt = v_ref[...]

        def head(kh, qh, do_h, lse_row, dl_row, masked):
            st = _dot_nt(kh, qh)
            if masked:
                st = jnp.where(_rows((t, t)) <= _lanes((t, t)), st, -1e30)
            pt = jnp.exp2(st - lse_row)
            dv_h = _dot(pt, do_h)
            dst = (pt * (_dot_nt(vt, do_h) - dl_row)).astype(MXU_DTYPE)
            return dv_h, _dot(dst, qh), _dot_tn(dst, kh)

        def step(i, carry, masked):
            dka, dkb, dv = carry
            q_rows = pl.ds(pl.multiple_of(i * t, t), t)
            qv = q_ref[q_rows, :]
            dov = do_ref[q_rows, :]
            lane = _lanes(dov.shape)
            do_lo = jnp.where(lane < V_DIM, dov, jnp.zeros_like(dov))
            do_hi = jnp.where(lane >= V_DIM, dov, jnp.zeros_like(dov))
            dva, dk_a, dq_a = head(ka, qv[:, :HEAD_BLK], do_lo, lse_ref[0, i, 0:1, :], dl_ref[0, i, 0:1, :], masked)
            dvb, dk_b, dq_b = head(kb, qv[:, HEAD_BLK:], do_hi, lse_ref[0, i, 1:2, :], dl_ref[0, i, 1:2, :], masked)
            dq_ref[q_rows, 0:HEAD_BLK] += dq_a
            dq_ref[q_rows, HEAD_BLK:pw] += dq_b
            return dka + dk_a, dkb + dk_b, dv + dva + dvb

        zero = jnp.zeros((t, HEAD_BLK), F32)
        carry = step(j, (zero, zero, zero), True)
        dka, dkb, dv = _loop_in_trips(j + 1, nq, FLASH_TILES_PER_TRIP, lambda i, cr: step(i, cr, False), carry)
        dk_ref[:, 0:HEAD_BLK] = dka * LN_2
        dk_ref[:, HEAD_BLK:pw] = dkb * LN_2
        dv_ref[...] = dv.astype(dv_ref.dtype)

    return pl.pallas_call(
        body, name="flash_bwd", grid=(N_PAIR, nq),
        in_specs=[pl.BlockSpec((s, pw), lambda p, j: (0, p)), pl.BlockSpec((s, HEAD_BLK), lambda p, j: (0, p)),
                  pl.BlockSpec((1, nq, SUBLANES, t), lambda p, j: (p, 0, 0, 0)),
                  pl.BlockSpec((1, nq, SUBLANES, t), lambda p, j: (p, 0, 0, 0)),
                  pl.BlockSpec((t, pw), lambda p, j: (j, p)), pl.BlockSpec((t, HEAD_BLK), lambda p, j: (j, p))],
        out_specs=[pl.BlockSpec((s, pw), lambda p, j: (0, p)), pl.BlockSpec((t, pw), lambda p, j: (j, p)),
                   pl.BlockSpec((t, HEAD_BLK), lambda p, j: (j, p))],
        out_shape=[jax.ShapeDtypeStruct((s, MLA_HEADS * HEAD_BLK), F32),
                   jax.ShapeDtypeStruct((s, MLA_HEADS * HEAD_BLK), F32),
                   jax.ShapeDtypeStruct((s, D_MLA), MXU_DTYPE)],
        compiler_params=_cp(("parallel", "arbitrary"), 56),
    )(q, do, lse, delta, k, v)


def _mla_prep_bwd(dq, dk, dv, z, qn, kvn, gq, gkv, wuq, wukv, c_t, s1_t, s2_t, l):
    s = z.shape[0]
    t = _tile(s)
    hq = MLA_HEADS * HEAD_BLK
    lay = _lay(l)

    def body(dq_ref, dk_ref, dv_ref, cq_ref, ckv_ref, qn_ref, kvn_ref, gq_ref, gkv_ref, wuq_ref, wukv_ref,
             c_ref, s1_ref, s2_ref, dcq_ref, dckv_ref, dkr_ref, dwuq_ref, dwukv_ref, dgq_ref, dgkv_ref,
             dqu_ref, dkvp_ref):
        @pl.when(pl.program_id(0) == 0)
        def _():
            for ref in (dwuq_ref, dwukv_ref, dgq_ref, dgkv_ref):
                ref[...] = jnp.zeros_like(ref)

        ct, s1, s2 = c_ref[...], s1_ref[...], s2_ref[...]
        dk_sum = jnp.zeros((t, HEAD_BLK), F32)
        for h in range(MLA_HEADS):
            blk = slice(h * HEAD_BLK, (h + 1) * HEAD_BLK)
            dqu_ref[:, blk] = _unrope(dq_ref[:, blk] * ATT_SCALE, ct, s1, s2).astype(dqu_ref.dtype)
            dkh = dk_ref[:, blk]
            dk_sum = dk_sum + dkh
            dkvp_ref[:, blk] = dkh.astype(dkvp_ref.dtype)
        dkvp_ref[:, hq:] = dv_ref[...]
        lane = _lanes((t, HEAD_BLK))
        rope_lanes = (lane >= KR_LANE0) & (lane < KR_LANE0 + QK_ROPE)
        dkr_ref[...] = _unrope(jnp.where(rope_lanes, dk_sum, 0.0), ct, s1, s2).astype(dkr_ref.dtype)

        def norm_bwd(c_in, g, dn_out, dc_ref, dg_ref):
            rs = lax.rsqrt(jnp.mean(c_in * c_in, axis=-1, keepdims=True) + EPS)
            n = c_in * rs
            dg_ref[...] += jnp.sum(dn_out * n, axis=0, keepdims=True)
            dn = dn_out * g
            dc_ref[...] = (rs * (dn - n * jnp.mean(dn * n, axis=-1, keepdims=True))).astype(dc_ref.dtype)

        dqu, dkvp = dqu_ref[...], dkvp_ref[...]
        dwuq_ref[...] += _dot_tn(qn_ref[...], dqu)
        dwukv_ref[...] += _dot_tn(kvn_ref[...], dkvp)
        norm_bwd(cq_ref[...], gq_ref[...], _dot_nt(dqu, wuq_ref[...]), dcq_ref, dgq_ref)
        norm_bwd(ckv_ref[...], gkv_ref[...], _dot_nt(dkvp, wukv_ref[...]), dckv_ref, dgkv_ref)

    full = lambda shp: pl.BlockSpec(shp, lambda i: (0, 0))
    tab = pl.BlockSpec((t, HEAD_BLK), lambda i: (i, 0))
    row = lambda w: pl.BlockSpec((t, w), lambda i: (i, 0))
    return pl.pallas_call(
        body, name="mla_prep_bwd", grid=(s // t,),
        in_specs=[row(hq), row(hq), row(D_MLA),
                  pl.BlockSpec((t, Q_RANK), lambda i: (i, 2)), pl.BlockSpec((t, KV_RANK), lambda i: (i, 6)),
                  row(Q_RANK), row(KV_RANK),
                  lay((1, Q_RANK)), lay((1, KV_RANK)), lay((Q_RANK, hq)), lay((KV_RANK, hq + D_MLA)),
                  tab, tab, tab],
        out_specs=[row(Q_RANK), row(KV_RANK), row(HEAD_BLK), full((Q_RANK, hq)), full((KV_RANK, hq + D_MLA)),
                   full((1, Q_RANK)), full((1, KV_RANK))],
        out_shape=[jax.ShapeDtypeStruct((s, Q_RANK), MXU_DTYPE), jax.ShapeDtypeStruct((s, KV_RANK), MXU_DTYPE),
                   jax.ShapeDtypeStruct((s, HEAD_BLK), MXU_DTYPE),
                   jax.ShapeDtypeStruct((Q_RANK, hq), F32), jax.ShapeDtypeStruct((KV_RANK, hq + D_MLA), F32),
                   jax.ShapeDtypeStruct((1, Q_RANK), F32), jax.ShapeDtypeStruct((1, KV_RANK), F32)],
        scratch_shapes=[pltpu.VMEM((t, hq), MXU_DTYPE), pltpu.VMEM((t, hq + D_MLA), MXU_DTYPE)],
        compiler_params=_cp(("arbitrary",), 48),
    )(dq, dk, dv, z, z, qn, kvn, gq, gkv, wuq, wukv, c_t, s1_t, s2_t)


def _outproj_fwd(x, ya, yb, yc, w, l):
    s = x.shape[0]
    t = _tile(s)
    lay = _lay(l)

    def body(x_ref, ya_ref, yb_ref, yc_ref, wa_ref, wb_ref, wc_ref, o_ref):
        o_ref[...] = (x_ref[...] + _dot(ya_ref[...], wa_ref[...]) + _dot(yb_ref[...], wb_ref[...])
                      + _dot(yc_ref[...], wc_ref[...]))

    row = lambda w_: pl.BlockSpec((t, w_), lambda i: (i, 0))
    return pl.pallas_call(
        body, name="outproj_fwd", grid=(s // t,),
        in_specs=[row(D_MODEL), row(D_LRU), row(D_MLA), row(D_POOL),
                  lay((D_LRU, D_MODEL), 0), lay((D_MLA, D_MODEL), 1), lay((D_POOL, D_MODEL), 3)],
        out_specs=row(D_MODEL),
        out_shape=jax.ShapeDtypeStruct((s, D_MODEL), F32),
        compiler_params=_cp(("parallel",), 40),
    )(x, ya, yb, yc, w, w, w)


def _outproj_bwd(dx, ya, yb, yc, o, z, w, l):
    s = dx.shape[0]
    t = _tile(s)
    nq = s // t
    rows = N_PAIR * SUBLANES

    def body(dx_ref, ya_ref, yb_ref, yc_ref, o_ref, gb_ref, w_ref, dya_ref, dyc_ref, do_ref, dgb_ref, dl_ref, dw_ref):
        @pl.when(pl.program_id(0) == 0)
        def _():
            dw_ref[...] = jnp.zeros_like(dw_ref)

        dxv = dx_ref[...].astype(MXU_DTYPE)
        dy = _dot_nt(dxv, w_ref[...])
        dw_ref[0:D_LRU, :] += _dot_tn(ya_ref[...], dxv)
        dw_ref[D_LRU:D_LRU + D_MLA, :] += _dot_tn(yb_ref[...], dxv)
        dw_ref[D_LRU + D_MLA:D_MIX, :] += _dot_tn(yc_ref[...], dxv)
        dya_ref[...] = dy[:, :D_LRU]
        dyc_ref[...] = dy[:, D_LRU + D_MLA:]
        dyb = dy[:, D_LRU:D_LRU + D_MLA]
        gb = gb_ref[...]
        sg = _sig(gb)
        ov = o_ref[...]
        do = dyb * (gb * sg)
        do_ref[...] = do.astype(do_ref.dtype)
        dgb_ref[...] = (dyb * ov * (sg * (1.0 + gb * (1.0 - sg)))).astype(dgb_ref.dtype)
        r = _rows((rows, D_MLA))
        head = (r // SUBLANES) * 2 + (r % SUBLANES)
        sel = ((r % SUBLANES) < 2) & (_lanes((rows, D_MLA)) // V_DIM == head)
        dl = lax.dot_general(sel.astype(F32), do * ov, (((1,), (1,)), ((), ())),
                             precision=lax.Precision.HIGHEST, preferred_element_type=F32)
        for p in range(N_PAIR):
            dl_ref[p, 0] = dl[p * SUBLANES:(p + 1) * SUBLANES]

    row = lambda w_: pl.BlockSpec((t, w_), lambda i: (i, 0))
    return pl.pallas_call(
        body, name="outproj_bwd", grid=(nq,),
        in_specs=[row(D_MODEL), row(D_LRU), row(D_MLA), row(D_POOL), row(D_MLA),
                  pl.BlockSpec((t, D_MLA), lambda i: (i, 3)), _lay(l)((D_MIX, D_MODEL))],
        out_specs=[row(D_LRU), row(D_POOL), row(D_MLA), row(D_MLA),
                   pl.BlockSpec((N_PAIR, 1, SUBLANES, t), lambda i: (0, i, 0, 0)),
                   pl.BlockSpec((D_MIX, D_MODEL), lambda i: (0, 0))],
        out_shape=[jax.ShapeDtypeStruct((s, D_LRU), F32), jax.ShapeDtypeStruct((s, D_POOL), F32),
                   jax.ShapeDtypeStruct((s, D_MLA), MXU_DTYPE), jax.ShapeDtypeStruct((s, D_MLA), MXU_DTYPE),
                   jax.ShapeDtypeStruct((N_PAIR, nq, SUBLANES, t), F32),
                   jax.ShapeDtypeStruct((D_MIX, D_MODEL), F32)],
        compiler_params=_cp(("arbitrary",), 48),
    )(dx, ya, yb, yc, o, z, w)


DZ_WIDTHS = (D_LRU, D_LRU, Q_RANK, D_MLA, KV_RANK, D_POOL, D_POOL, HEAD_BLK)


def _dwin(h, dz_parts):
    s = h.shape[0]
    t = _tile(s)

    def body(h_ref, *refs):
        o_ref = refs[-1]

        @pl.when(pl.program_id(0) == 0)
        def _():
            o_ref[...] = jnp.zeros_like(o_ref)

        o_ref[...] += _dot_tn(h_ref[...], jnp.concatenate([r[...] for r in refs[:-1]], axis=1))

    row = lambda w_: pl.BlockSpec((t, w_), lambda i: (i, 0))
    return pl.pallas_call(
        body, name="dwin", grid=(s // t,),
        in_specs=[row(D_MODEL)] + [row(wd) for wd in DZ_WIDTHS],
        out_specs=pl.BlockSpec((D_MODEL, D_INP), lambda i: (0, 0)),
        out_shape=jax.ShapeDtypeStruct((D_MODEL, D_INP), F32),
        compiler_params=_cp(("arbitrary",), 56),
    )(h, *dz_parts)


def _inproj_bwd(dz_parts, w, x, g, dxn, l):
    s = x.shape[0]
    t = _tile(s)
    lay = _lay(l)
    n_parts = len(DZ_WIDTHS)

    def body(*refs):
        part_refs = refs[:n_parts]
        w_ref, x_ref, g_ref, dxn_ref, dx_ref, dg_ref = refs[n_parts:]

        @pl.when(pl.program_id(0) == 0)
        def _():
            dg_ref[...] = jnp.zeros_like(dg_ref)

        dh = _dot_nt(jnp.concatenate([r[...] for r in part_refs], axis=1), w_ref[...])
        xv = x_ref[...]
        rs = lax.rsqrt(jnp.mean(xv * xv, axis=-1, keepdims=True) + EPS)
        n = xv * rs
        dg_ref[...] += jnp.sum(dh * n, axis=0, keepdims=True)
        dn = dh * g_ref[...]
        dx_ref[...] = dxn_ref[...] + rs * (dn - n * jnp.mean(dn * n, axis=-1, keepdims=True))

    row = lambda w_: pl.BlockSpec((t, w_), lambda i: (i, 0))
    return pl.pallas_call(
        body, name="inproj_bwd", grid=(s // t,),
        in_specs=[row(wd) for wd in DZ_WIDTHS] + [lay((D_MODEL, D_INP)), row(D_MODEL), lay((1, D_MODEL)),
                                                  row(D_MODEL)],
        out_specs=[row(D_MODEL), pl.BlockSpec((1, D_MODEL), lambda i: (0, 0))],
        out_shape=[jax.ShapeDtypeStruct((s, D_MODEL), F32), jax.ShapeDtypeStruct((1, D_MODEL), F32)],
        compiler_params=_cp(("arbitrary",), 48),
    )(*dz_parts, w, x, g, dxn)


def _loss_head(x, g, tgt):
    s = x.shape[0]
    t = _tile(s)

    def body(x_ref, g_ref, t_ref, dx_ref, loss_ref, dg_ref):
        @pl.when(pl.program_id(0) == 0)
        def _():
            loss_ref[...] = jnp.zeros_like(loss_ref)
            dg_ref[...] = jnp.zeros_like(dg_ref)

        xv = x_ref[...]
        rs = lax.rsqrt(jnp.mean(xv * xv, axis=-1, keepdims=True) + EPS)
        n = xv * rs
        gv = g_ref[...]
        e = n * gv - t_ref[...]
        loss_ref[...] += 0.5 * jnp.sum(jnp.mean(e * e, axis=-1, keepdims=True))
        dyf = e * (1.0 / D_MODEL)
        dg_ref[...] += jnp.sum(dyf * n, axis=0, keepdims=True)
        dn = dyf * gv
        dx_ref[...] = rs * (dn - n * jnp.mean(dn * n, axis=-1, keepdims=True))

    row = pl.BlockSpec((t, D_MODEL), lambda i: (i, 0))
    vec = pl.BlockSpec((1, D_MODEL), lambda i: (0, 0))
    return pl.pallas_call(
        body, name="loss_head", grid=(s // t,),
        in_specs=[row, vec, row],
        out_specs=[row, pl.BlockSpec((1, LANES), lambda i: (0, 0)), vec],
        out_shape=[jax.ShapeDtypeStruct((s, D_MODEL), F32), jax.ShapeDtypeStruct((1, LANES), F32),
                   jax.ShapeDtypeStruct((1, D_MODEL), F32)],
        compiler_params=_cp(("arbitrary",), 32),
    )(x, g, tgt)


def _block_diag(w):
    n, h, d, _ = w.shape
    return jnp.einsum('lhij,hk->lhikj', w, jnp.eye(h, dtype=w.dtype)).reshape(n, h * d, h * d)


def _diag_blocks(wfull, h):
    d = wfull.shape[-1] // h
    return jnp.stack([wfull[:, i * d:(i + 1) * d, i * d:(i + 1) * d] for i in range(h)], axis=1)


REF_TO_PERM = np.concatenate([np.arange(0, 1152), np.arange(1536, 1792),
                              np.arange(2304 + KR_LANE0, 2304 + KR_LANE0 + QK_ROPE),
                              np.arange(1152, 1536), np.arange(1792, 2304)])
N_SHARD = 4
W_IN_SHARD = D_IN // N_SHARD


def _w_in_runs():
    ref_of_perm = -np.ones(D_INP, np.int64)
    ref_of_perm[REF_TO_PERM] = np.arange(D_IN)
    perm_runs, p = [], 0
    while p < D_INP:
        r, q = ref_of_perm[p], p + 1
        if r < 0:
            while q < D_INP and ref_of_perm[q] < 0:
                q += 1
            perm_runs.append((None, q - p, 0))
        else:
            while (q < D_INP and ref_of_perm[q] == ref_of_perm[q - 1] + 1
                   and ref_of_perm[q] // W_IN_SHARD == r // W_IN_SHARD):
                q += 1
            perm_runs.append((int(r // W_IN_SHARD), int(r % W_IN_SHARD), int(r % W_IN_SHARD + q - p)))
        p = q
    shard_runs = []
    for s in range(N_SHARD):
        cols = REF_TO_PERM[s * W_IN_SHARD:(s + 1) * W_IN_SHARD]
        runs, a = [], 0
        for b in range(1, W_IN_SHARD + 1):
            if b == W_IN_SHARD or cols[b] != cols[b - 1] + 1:
                runs.append((int(cols[a]), int(cols[b - 1]) + 1))
                a = b
        shard_runs.append(runs)
    return perm_runs, shard_runs


def _permute_w_in(shards):
    perm_runs, _ = _w_in_runs()
    lead = shards[0].shape[:-1]
    parts = [jnp.zeros(lead + (a,), shards[0].dtype) if s is None else shards[s][..., a:b] for s, a, b in perm_runs]
    return jnp.concatenate(parts, axis=-1)


def _w_in_shard(wp, s):
    _, shard_runs = _w_in_runs()
    return jnp.concatenate([wp[..., a:b] for a, b in shard_runs[s]], axis=-1)


def _pad_w_uq(w):
    w4 = w.reshape(w.shape[:2] + (MLA_HEADS, QK_NOPE + QK_ROPE))
    return jnp.pad(w4, ((0, 0),) * 3 + ((0, HEAD_BLK - QK_NOPE - QK_ROPE),)).reshape(w.shape[:2] + (-1,))


def _unpad_w_uq(w):
    return w.reshape(w.shape[:2] + (MLA_HEADS, HEAD_BLK))[..., :QK_NOPE + QK_ROPE].reshape(w.shape[:2] + (-1,))


def _pad_w_ukv(w):
    w4 = w.reshape(w.shape[:2] + (MLA_HEADS, QK_NOPE + V_DIM))
    kpart = jnp.pad(w4[..., :QK_NOPE], ((0, 0),) * 3 + ((0, HEAD_BLK - QK_NOPE),)).reshape(w.shape[:2] + (-1,))
    return jnp.concatenate([kpart, w4[..., QK_NOPE:].reshape(w.shape[:2] + (-1,))], axis=2)


def _unpad_w_ukv(w):
    hq = MLA_HEADS * HEAD_BLK
    kpart = w[..., :hq].reshape(w.shape[:2] + (MLA_HEADS, HEAD_BLK))[..., :QK_NOPE]
    vpart = w[..., hq:].reshape(w.shape[:2] + (MLA_HEADS, V_DIM))
    return jnp.concatenate([kpart, vpart], axis=3).reshape(w.shape[:2] + (-1,))


def _local_step(x, tgt, w):
    s = x.shape[0]
    tabs = _rope_tables(s)
    vec = lambda a: a[:, None, :]
    mxu = lambda a: a.astype(MXU_DTYPE)
    p = dict(g=vec(w['norm_g']), w_in=mxu(w['w_in']), cw=w['conv_w'], cb=vec(w['conv_b']),
             wr=mxu(_block_diag(w['w_rg'])), br=vec(w['b_rg']), wi=mxu(_block_diag(w['w_ig'])), bi=vec(w['b_ig']),
             lam=vec(w['lru_lambda']), gq=vec(w['q_norm_g']), gkv=vec(w['kv_norm_g']),
             wuq=mxu(_pad_w_uq(w['w_uq'])), wukv=mxu(_pad_w_ukv(w['w_ukv'])),
             wp=mxu(_block_diag(w['w_pool'])), ps=vec(w['pool_scale']), wout=mxu(w['w_out']))
    lru = lambda l: (p['cw'], p['cb'], p['wr'], p['br'], p['wi'], p['bi'], p['lam'], l)
    mla = lambda l: (p['gq'], p['gkv'], p['wuq'], p['wukv'], *tabs, l)

    saved = []
    for l in range(DEPTH):
        h, z = _inproj_fwd(x, p['g'], p['w_in'], l)
        ya, hs = _lru_fwd(z, *lru(l))
        yc = _pool_fwd(z, p['wp'], p['ps'], l)
        q, k, v, qn, kvn = _mla_prep_fwd(z, *mla(l))
        o, yb, lse = _flash_fwd(q, k, v, z)
        saved.append(dict(x=x, h=h, z=z, hs=hs, ya=ya, yb=yb, yc=yc, q=q, k=k, v=v, qn=qn, kvn=kvn, o=o, lse=lse))
        x = _outproj_fwd(x, ya, yb, yc, p['wout'], l)

    dx, loss, dgf = _loss_head(x, w['final_norm_g'][None], tgt)
    per_layer = {n: [None] * DEPTH for n in WEIGHT_NAMES if n != 'final_norm_g'}
    for l in reversed(range(DEPTH)):
        sv = saved[l]
        dya, dyc, do, dgb, delta, per_layer['w_out'][l] = _outproj_bwd(
            dx, sv['ya'], sv['yb'], sv['yc'], sv['o'], sv['z'], p['wout'], l)
        dza, dga, *lru_grads = _lru_bwd(sv['z'], sv['hs'], dya, *lru(l))
        for n, g in zip(('w_rg', 'w_ig', 'conv_w', 'conv_b', 'b_rg', 'b_ig', 'lru_lambda'), lru_grads):
            per_layer[n][l] = g
        dzc, dgc, per_layer['w_pool'][l], per_layer['pool_scale'][l] = _pool_bwd(sv['z'], dyc, p['wp'], p['ps'], l)
        dq, dk, dv = _flash_bwd(sv['q'], sv['k'], sv['v'], do, sv['lse'], delta)
        (dcq, dckv, dkr, per_layer['w_uq'][l], per_layer['w_ukv'][l], per_layer['q_norm_g'][l],
         per_layer['kv_norm_g'][l]) = _mla_prep_bwd(dq, dk, dv, sv['z'], sv['qn'], sv['kvn'], *mla(l))
        dz_parts = (dza, dga, dcq, dgb, dckv, dzc, dgc, dkr)
        per_layer['w_in'][l] = _dwin(sv['h'], dz_parts)
        dx, per_layer['norm_g'][l] = _inproj_bwd(dz_parts, p['w_in'], sv['x'], p['g'], dx, l)
    grads = {n: jnp.stack(g) for n, g in per_layer.items()}
    for n in ('norm_g', 'conv_b', 'b_rg', 'b_ig', 'lru_lambda', 'q_norm_g', 'kv_norm_g', 'pool_scale'):
        grads[n] = grads[n][:, 0, :]
    grads['w_rg'] = _diag_blocks(grads['w_rg'], LRU_HEADS)
    grads['w_ig'] = _diag_blocks(grads['w_ig'], LRU_HEADS)
    grads['w_pool'] = _diag_blocks(grads['w_pool'], len(POOL_WINDOWS))
    grads['w_uq'] = _unpad_w_uq(grads['w_uq'])
    grads['w_ukv'] = _unpad_w_ukv(grads['w_ukv'])
    grads['final_norm_g'] = dgf[0]
    return loss[0, 0], dx, grads


WIRE_DTYPE = jnp.bfloat16
MESH_IDS = pl.DeviceIdType.MESH
_HBM = pl.BlockSpec(memory_space=pltpu.HBM)


def _coords():
    return lax.axis_index("x"), lax.axis_index("y"), lax.axis_index("c")


def _comm_call(body, name, arrays, out_shapes, copies_per_array):
    n = len(arrays)
    return pl.pallas_call(
        body, name=name, out_shape=out_shapes, in_specs=[_HBM] * n, out_specs=[_HBM] * n,
        scratch_shapes=[pltpu.SemaphoreType.DMA((n, copies_per_array)), pltpu.SemaphoreType.DMA((n, copies_per_array))],
    )(*arrays)


def _all_gather8(blocks, name):
    n = len(blocks)
    every = range(n)

    def body(*refs):
        x_refs, out_refs = refs[:n], refs[n:2 * n]
        send_sems, recv_sems = refs[2 * n:]
        x, y, c = _coords()
        me, sibling = (x, y, c), (x, y, 1 - c)
        chips = [(1 - x, y), (x, 1 - y), (1 - x, 1 - y)]

        def slot(t, px, py, pc):
            return out_refs[t].at[4 * px + 2 * py + pc]

        def copy(t, k, block, to, own=False):
            return pltpu.make_async_remote_copy(
                src_ref=x_refs[t] if own else slot(t, *block), dst_ref=slot(t, *block),
                send_sem=send_sems.at[t, k], recv_sem=recv_sems.at[t, k], device_id=to, device_id_type=MESH_IDS)

        first = [copy(t, 0, me, sibling, own=True) for t in every]
        first += [copy(t, 1 + j, me, (*chip, c), own=True) for j, chip in enumerate(chips) for t in every]
        for cp in first:
            cp.start()
        passed = [[copy(t, 4 + j, (*chip, c), sibling) for t in every] for j, chip in enumerate(chips)]
        for j, chip in enumerate(chips):
            for t in every:
                copy(t, 1 + j, (*chip, c), me).wait_recv()
                passed[j][t].start()
        for t in every:
            copy(t, 0, sibling, me).wait_recv()
        for j, chip in enumerate(chips):
            for t in every:
                copy(t, 4 + j, (*chip, 1 - c), me).wait_recv()
        for cp in first + [cp for group in passed for cp in group]:
            cp.wait_send()

    outs = [jax.ShapeDtypeStruct((N_DEV,) + b.shape, b.dtype) for b in blocks]
    got = _comm_call(body, name, blocks, outs, 7)
    me = 4 * lax.axis_index("x") + 2 * lax.axis_index("y") + lax.axis_index("c")
    return [lax.dynamic_update_index_in_dim(g, b, me, 0) for g, b in zip(got, blocks)]


def _sibling_send(arrs, name, which_half=False):
    n = len(arrs)

    def body(*refs):
        a_refs, out_refs = refs[:n], refs[n:2 * n]
        send_sems, recv_sems = refs[2 * n:]
        x, y, c = _coords()
        sent = [pltpu.make_async_remote_copy(
            src_ref=a_refs[t].at[1 - c] if which_half else a_refs[t], dst_ref=out_refs[t],
            send_sem=send_sems.at[t, 0], recv_sem=recv_sems.at[t, 0],
            device_id=(x, y, 1 - c), device_id_type=MESH_IDS) for t in range(n)]
        for cp in sent:
            cp.start()
        for cp in sent:
            cp.wait()

    shapes = [jax.ShapeDtypeStruct(a.shape[1:] if which_half else a.shape, a.dtype) for a in arrs]
    return _comm_call(body, name, arrs, shapes, 1)


def _chip_exchange(arrs, name):
    n = len(arrs)

    def body(*refs):
        a_refs, out_refs = refs[:n], refs[n:2 * n]
        send_sems, recv_sems = refs[2 * n:]
        x, y, c = _coords()
        copies = []
        for j, (cx, cy) in enumerate([(1 - x, y), (x, 1 - y), (1 - x, 1 - y)]):
            copies += [pltpu.make_async_remote_copy(
                src_ref=a_refs[t].at[2 * cx + cy], dst_ref=out_refs[t].at[j], send_sem=send_sems.at[t, j],
                recv_sem=recv_sems.at[t, j], device_id=(cx, cy, c), device_id_type=MESH_IDS) for t in range(n)]
        for cp in copies:
            cp.start()
        for cp in copies:
            cp.wait()

    return _comm_call(body, name, arrs, [jax.ShapeDtypeStruct((3,) + a.shape[1:], a.dtype) for a in arrs], 3)


def _sum_leading(groups, out_dtype, steps, name):
    flat = [a for g in groups for a in g]

    def body(*refs):
        ins, outs, pos = refs[:len(flat)], refs[len(flat):], 0
        for g, o_ref in zip(groups, outs):
            acc = None
            for i_ref in ins[pos:pos + len(g)]:
                for k in range(i_ref.shape[0]):
                    term = i_ref[k].astype(F32)
                    acc = term if acc is None else acc + term
            pos += len(g)
            o_ref[...] = acc.astype(o_ref.dtype)

    return pl.pallas_call(
        body, name=name, grid=(steps,),
        in_specs=[pl.BlockSpec((a.shape[0], a.shape[1] // steps, a.shape[2]), lambda i: (0, i, 0)) for a in flat],
        out_specs=[pl.BlockSpec((g[0].shape[1] // steps, g[0].shape[2]), lambda i: (i, 0)) for g in groups],
        out_shape=[jax.ShapeDtypeStruct(g[0].shape[1:], out_dtype) for g in groups],
        compiler_params=_cp(("parallel",), 40),
    )(*flat)


def _adamw_update(w_ref, g_ref, m_ref, v_ref, d_ref, mo_ref, vo_ref):
    gv = g_ref[...]
    mn = ADAM_B1 * m_ref[...] + (1.0 - ADAM_B1) * gv
    vn = ADAM_B2 * v_ref[...] + (1.0 - ADAM_B2) * (gv * gv)
    mo_ref[...] = mn
    vo_ref[...] = vn
    m_hat = mn / (1.0 - ADAM_B1 ** ADAM_STEP)
    v_hat = vn / (1.0 - ADAM_B2 ** ADAM_STEP)
    d_ref[...] = (-ADAM_LR) * (m_hat / (jnp.sqrt(v_hat) + ADAM_EPS) + ADAM_WD * w_ref[...])


def _adamw(w, g, m, v, name):
    n, r, cdim = w.shape
    tr = math.gcd(r, 512)

    def body(*refs):
        _adamw_update(*refs)

    blk = pl.BlockSpec((None, tr, cdim), lambda l, i: (l, i, 0))
    return pl.pallas_call(
        body, name=name, grid=(n, r // tr),
        in_specs=[blk] * 4, out_specs=[blk] * 3,
        out_shape=[jax.ShapeDtypeStruct(w.shape, F32)] * 3,
        compiler_params=_cp(("parallel", "parallel"), 40),
    )(w, g, m, v)


def _adamw_small(ws, gs, ms, vs, name):
    n = len(ws)

    def body(*refs):
        ins, outs = refs[:4 * n], refs[4 * n:]
        for t in range(n):
            _adamw_update(ins[t], ins[n + t], ins[2 * n + t], ins[3 * n + t], outs[t], outs[n + t], outs[2 * n + t])

    shapes = [jax.ShapeDtypeStruct(w.shape, F32) for w in ws]
    outs = pl.pallas_call(body, name=name, out_shape=shapes * 3)(*ws, *gs, *ms, *vs)
    return outs[:n], outs[n:2 * n], outs[2 * n:]


HALF = DEPTH // 2
BIG = ['w_in', 'w_uq', 'w_ukv', 'w_out']
SHARD_AXIS = {'w_in': 2, 'conv_w': 2, 'w_uq': 2, 'w_ukv': 2, 'w_out': 1}
FULL_SHAPE = {'w_in': (DEPTH, D_MODEL, D_IN), 'conv_w': (DEPTH, CONV_WIDTH, D_LRU),
              'w_uq': (DEPTH, Q_RANK, MLA_HEADS * (QK_NOPE + QK_ROPE)),
              'w_ukv': (DEPTH, KV_RANK, MLA_HEADS * (QK_NOPE + V_DIM)), 'w_out': (DEPTH, D_MIX, D_MODEL)}


def _shard_shape(n):
    shp = list(FULL_SHAPE[n])
    shp[SHARD_AXIS[n]] //= N_SHARD
    return tuple(shp)


def _rows_view(a, lead=0):
    return a.reshape(a.shape[:lead] + (-1, a.shape[-1]))


def _gather_weights(local):
    c = lax.axis_index("c")
    names = BIG + ['conv_w']
    halves = [lax.dynamic_slice_in_dim(local[n], HALF * c, HALF, axis=0) for n in names]
    halves = [h.astype(WIRE_DTYPE) if n in BIG else h for n, h in zip(names, halves)]
    got = _all_gather8(halves, "gather_weights")
    full = {}
    for n, g in zip(names, got):
        g = g.reshape((N_SHARD, DEPTH) + g.shape[2:])
        if n == 'w_in':
            full[n] = _permute_w_in([g[s] for s in range(N_SHARD)])
        else:
            full[n] = jnp.moveaxis(g, 0, SHARD_AXIS[n]).reshape(FULL_SHAPE[n])
    return full


def _shard_blocks(g, n):
    width = _shard_shape(n)[SHARD_AXIS[n]]

    def block(h, s):
        part = g[HALF * h:HALF * (h + 1)]
        if n == 'w_in':
            part = _w_in_shard(part, s)
        else:
            part = lax.slice_in_dim(part, s * width, (s + 1) * width, axis=SHARD_AXIS[n])
        return _rows_view(part)

    return jnp.stack([jnp.stack([block(h, s) for s in range(N_SHARD)]) for h in range(2)]).astype(WIRE_DTYPE)


SUM_STEPS = 8


def _reduce_big(grads):
    c = lax.axis_index("c")
    shard = 2 * lax.axis_index("x") + lax.axis_index("y")
    contrib = [_shard_blocks(grads[n], n) for n in BIG]
    from_sibling = _sibling_send(contrib, "pair_exchange_big", which_half=True)
    own_half = [lax.dynamic_index_in_dim(a, c, 0, keepdims=False) for a in contrib]
    pair_sum = _sum_leading([[_rows_view(o)[None], _rows_view(r)[None]] for o, r in zip(own_half, from_sibling)],
                            WIRE_DTYPE, SUM_STEPS, "pair_sum_big")
    to_chips = [p.reshape(a.shape[1:]) for p, a in zip(pair_sum, contrib)]
    from_chips = _chip_exchange(to_chips, "chip_exchange_big")
    own_block = [lax.dynamic_index_in_dim(a, shard, 0, keepdims=True) for a in to_chips]
    mine = _sum_leading([[o, r] for o, r in zip(own_block, from_chips)], F32, SUM_STEPS, "chip_sum_big")
    theirs = _sibling_send(mine, "sibling_big")
    both = [jnp.where(c == 0, jnp.stack([m, t]), jnp.stack([t, m])) for m, t in zip(mine, theirs)]
    return {n: b.reshape(_shard_shape(n)) for n, b in zip(BIG, both)}


SMALL = REPLICATED + ['conv_w']


def _reduce_small(grads):
    views = [grads[n].reshape(-1, LANES) if grads[n].shape[-1] < LANES else _rows_view(jnp.atleast_2d(grads[n]))
             for n in SMALL]
    sums = _sum_leading([[g] for g in _all_gather8(views, "gather_small")], F32, 1, "sum_small")
    return {n: s.reshape(grads[n].shape) for n, s in zip(SMALL, sums)}


def kernel(x, norm_g, w_in, conv_w, conv_b, w_rg, b_rg, w_ig, b_ig, lru_lambda, q_norm_g, w_uq, kv_norm_g, w_ukv, w_pool, pool_scale, w_out, final_norm_g, loss_target, m_norm_g, m_w_in, m_conv_w, m_conv_b, m_w_rg, m_b_rg, m_w_ig, m_b_ig, m_lru_lambda, m_q_norm_g, m_w_uq, m_kv_norm_g, m_w_ukv, m_w_pool, m_pool_scale, m_w_out, m_final_norm_g, v_norm_g, v_w_in, v_conv_w, v_conv_b, v_w_rg, v_b_rg, v_w_ig, v_b_ig, v_lru_lambda, v_q_norm_g, v_w_uq, v_kv_norm_g, v_w_ukv, v_w_pool, v_pool_scale, v_w_out, v_final_norm_g):
    w_loc = dict(zip(WEIGHT_NAMES, (norm_g, w_in, conv_w, conv_b, w_rg, b_rg, w_ig, b_ig, lru_lambda, q_norm_g, w_uq,
                                    kv_norm_g, w_ukv, w_pool, pool_scale, w_out, final_norm_g)))
    m_loc = dict(zip(WEIGHT_NAMES, (m_norm_g, m_w_in, m_conv_w, m_conv_b, m_w_rg, m_b_rg, m_w_ig, m_b_ig, m_lru_lambda,
                                    m_q_norm_g, m_w_uq, m_kv_norm_g, m_w_ukv, m_w_pool, m_pool_scale, m_w_out,
                                    m_final_norm_g)))
    v_loc = dict(zip(WEIGHT_NAMES, (v_norm_g, v_w_in, v_conv_w, v_conv_b, v_w_rg, v_b_rg, v_w_ig, v_b_ig, v_lru_lambda,
                                    v_q_norm_g, v_w_uq, v_kv_norm_g, v_w_ukv, v_w_pool, v_pool_scale, v_w_out,
                                    v_final_norm_g)))
    w_full = dict(w_loc)
    w_full.update(_gather_weights(w_loc))
    loss_local, dx, g_local = _local_step(x[0], loss_target[0], w_full)
    loss = lax.psum(loss_local, ("x", "y", "c"))

    grads = _reduce_big(g_local)
    g_small = _reduce_small(g_local)
    shard = 2 * lax.axis_index("x") + lax.axis_index("y")
    width = D_LRU // N_SHARD
    grads['conv_w'] = lax.dynamic_slice_in_dim(g_small['conv_w'], shard * width, width, axis=2)
    for n in REPLICATED:
        grads[n] = g_small[n]

    delta, new_m, new_v = {}, {}, {}
    for n in BIG:
        delta[n], new_m[n], new_v[n] = _adamw(w_loc[n], grads[n], m_loc[n], v_loc[n], "adamw_" + n)
    small = [[_rows_view(jnp.atleast_2d(t[n])) for n in SMALL] for t in (w_loc, grads, m_loc, v_loc)]
    for tree, outs in zip((delta, new_m, new_v), _adamw_small(*small, "adamw_small")):
        tree.update({n: a.reshape(w_loc[n].shape) for n, a in zip(SMALL, outs)})

    return (loss, dx[None], *[grads[n] for n in WEIGHT_NAMES], *[delta[n] for n in WEIGHT_NAMES],
            *[new_m[n] for n in WEIGHT_NAMES], *[new_v[n] for n in WEIGHT_NAMES])
```

```python
import math

import jax
import jax.numpy as jnp
import numpy as np
from jax import lax
from jax.experimental import pallas as pl
from jax.experimental.pallas import tpu as pltpu

F32 = jnp.float32
MXU_DTYPE = jnp.bfloat16

D_MODEL = 1024
DEPTH = 4
EPS = 1e-6
D_LRU = 384
LRU_HEADS = 6
CONV_WIDTH = 4
LRU_C = 8.0
MLA_HEADS = 6
QK_NOPE = 64
QK_ROPE = 32
V_DIM = 64
D_MLA = MLA_HEADS * V_DIM
Q_RANK = 384
KV_RANK = 256
ROPE_BASE = 10000.0
POOL_WINDOWS = (2, 4, 8, 16)
D_POOL = 256
D_MIX = D_LRU + D_MLA + D_POOL
D_IN = 2336
ATT_SCALE = (QK_NOPE + QK_ROPE) ** -0.5
LOG2_E = 1.4426950408889634
LN_2 = 0.6931471805599453
Q_PRESCALE = ATT_SCALE * LOG2_E

ADAM_LR = 0.001
ADAM_B1 = 0.9
ADAM_B2 = 0.999
ADAM_EPS = 1e-08
ADAM_WD = 0.01
ADAM_STEP = 10

LANES = 128
SUBLANES = 8
N_DEV = 8

D_INP = 2432
KR_LANE0 = 64
HEAD_BLK = 128
N_PAIR = MLA_HEADS // 2

WEIGHT_NAMES = ['norm_g', 'w_in', 'conv_w', 'conv_b', 'w_rg', 'b_rg', 'w_ig', 'b_ig', 'lru_lambda', 'q_norm_g',
                'w_uq', 'kv_norm_g', 'w_ukv', 'w_pool', 'pool_scale', 'w_out', 'final_norm_g']
SHARDED = ['w_in', 'conv_w', 'w_uq', 'w_ukv', 'w_out']
REPLICATED = [n for n in WEIGHT_NAMES if n not in SHARDED]


def _cp(sem, vmem_mb=None):
    return pltpu.CompilerParams(dimension_semantics=sem,
                                vmem_limit_bytes=None if vmem_mb is None else vmem_mb << 20)


def _dot(a, b):
    return jnp.dot(a.astype(MXU_DTYPE), b.astype(MXU_DTYPE), preferred_element_type=F32)


def _dot_nt(a, b):
    return lax.dot_general(a.astype(MXU_DTYPE), b.astype(MXU_DTYPE), (((1,), (1,)), ((), ())),
                           preferred_element_type=F32)


def _dot_tn(a, b):
    return lax.dot_general(a.astype(MXU_DTYPE), b.astype(MXU_DTYPE), (((0,), (0,)), ((), ())),
                           preferred_element_type=F32)


def _sig(x):
    return 0.5 * jnp.tanh(0.5 * x) + 0.5


def _down(x, k):
    return pltpu.roll(x, k, 0)


def _up(x, k):
    return pltpu.roll(x, x.shape[0] - k, 0)


def _rows(shape):
    return lax.broadcasted_iota(jnp.int32, shape, 0)


def _lanes(shape):
    return lax.broadcasted_iota(jnp.int32, shape, 1)


def _tile(s):
    return min(512, s)


def _wide_tile(s):
    return min(2 * _tile(s), s)


def _lay(l):
    return lambda shp, blk=0: pl.BlockSpec((None,) + shp, lambda *_: (l, blk, 0))


def _inproj_fwd(x, g, w, l):
    s = x.shape[0]
    t = _tile(s)
    lay = _lay(l)

    def body(x_ref, g_ref, w_ref, h_ref, z_ref):
        xv = x_ref[...]
        rs = lax.rsqrt(jnp.mean(xv * xv, axis=-1, keepdims=True) + EPS)
        h = (xv * rs * g_ref[...]).astype(MXU_DTYPE)
        h_ref[...] = h
        z_ref[...] = jnp.dot(h, w_ref[...], preferred_element_type=F32)

    return pl.pallas_call(
        body, name="inproj_fwd", grid=(s // t,),
        in_specs=[pl.BlockSpec((t, D_MODEL), lambda i: (i, 0)),
                  lay((1, D_MODEL)), lay((D_MODEL, D_INP))],
        out_specs=[pl.BlockSpec((t, D_MODEL), lambda i: (i, 0)),
                   pl.BlockSpec((t, D_INP), lambda i: (i, 0))],
        out_shape=[jax.ShapeDtypeStruct((s, D_MODEL), MXU_DTYPE), jax.ShapeDtypeStruct((s, D_INP), F32)],
        compiler_params=_cp(("parallel",), 40),
    )(x, g, w)


def _lru_gates(za, halo, cw_ref, cb_ref, wr_ref, br_ref, wi_ref, bi_ref, lam_ref):
    t = za.shape[0]
    ext = jnp.concatenate([halo, za], axis=0)
    sh = [za] + [_down(ext, j)[SUBLANES:SUBLANES + t] for j in (1, 2, 3)]
    xa = cb_ref[...] + cw_ref[3:4, :] * sh[0] + cw_ref[2:3, :] * sh[1] + cw_ref[1:2, :] * sh[2] + cw_ref[0:1, :] * sh[3]
    r = 1.0 / (1.0 + jnp.exp(-(_dot(xa, wr_ref[...]) + br_ref[...])))
    ig = _sig(_dot(xa, wi_ref[...]) + bi_ref[...])
    lam = lam_ref[...]
    sp = jnp.maximum(-lam, 0.0) + jnp.log(1.0 + jnp.exp(-jnp.abs(lam)))
    la = (-LRU_C) * r * sp
    a = jnp.exp(la)
    y2 = 2.0 * la
    m2 = jnp.where(y2 > -0.01, -(y2 * (1.0 + y2 * (0.5 + y2 * (1.0 / 6.0)))), 1.0 - a * a)
    return xa, sh, r, ig, sp, a, jnp.sqrt(m2), m2


def _lru_fwd(z, cw, cb, wr, br, wi, bi, lam, l):
    s = z.shape[0]
    t = _tile(s)
    c = D_LRU
    lay = _lay(l)

    def body(za_ref, ga_ref, cw_ref, cb_ref, wr_ref, br_ref, wi_ref, bi_ref, lam_ref, ya_ref, hs_ref, zprev, hcar):
        i = pl.program_id(0)

        @pl.when(i == 0)
        def _():
            zprev[...] = jnp.zeros_like(zprev)
            hcar[...] = jnp.zeros_like(hcar)

        za = za_ref[...]
        xa, _, _, ig, _, a, m, _ = _lru_gates(za, zprev[...], cw_ref, cb_ref, wr_ref, br_ref, wi_ref, bi_ref, lam_ref)
        row = _rows((t, c))
        acc_h = m * (ig * xa) + jnp.where(row == 0, a * hcar[...], 0.0)
        acc_a = jnp.where(row == 0, 0.0, a)
        k = 1
        while k < t:
            acc_h = acc_h + acc_a * _down(acc_h, k)
            acc_a = acc_a * _down(acc_a, k)
            k *= 2
        hs = acc_h
        hs_ref[...] = hs
        ga = ga_ref[...]
        ya_ref[...] = (hs * (ga * _sig(ga))).astype(ya_ref.dtype)
        hcar[...] = jnp.sum(jnp.where(row == t - 1, hs, 0.0), axis=0, keepdims=True)
        zprev[...] = za_ref[t - SUBLANES:t, :]

    return pl.pallas_call(
        body, name="lru_fwd", grid=(s // t,),
        in_specs=[pl.BlockSpec((t, c), lambda i: (i, 0)), pl.BlockSpec((t, c), lambda i: (i, 1)),
                  lay((CONV_WIDTH, c)), lay((1, c)), lay((c, c)), lay((1, c)), lay((c, c)), lay((1, c)),
                  lay((1, c))],
        out_specs=[pl.BlockSpec((t, c), lambda i: (i, 0)), pl.BlockSpec((t, c), lambda i: (i, 0))],
        out_shape=[jax.ShapeDtypeStruct((s, c), MXU_DTYPE), jax.ShapeDtypeStruct((s, c), F32)],
        scratch_shapes=[pltpu.VMEM((SUBLANES, c), F32), pltpu.VMEM((1, c), F32)],
        compiler_params=_cp(("arbitrary",), 40),
    )(z, z, cw, cb, wr, br, wi, bi, lam)


def _lru_bwd(z, hs, dy, cw, cb, wr, br, wi, bi, lam, l):
    s = z.shape[0]
    t = _tile(s)
    lay = _lay(l)
    nt = s // t
    c = D_LRU
    hb = t // SUBLANES

    def body(za_ref, zh_ref, ga_ref, hs_ref, hh_ref, dy_ref, cw_ref, cb_ref, wr_ref, br_ref, wi_ref, bi_ref, lam_ref,
             dza_ref, dga_ref, dwr_ref, dwi_ref, dcw_ref, dcb_ref, dbr_ref, dbi_ref, dlam_ref, lcar, dxn):
        i = pl.program_id(0)
        tt = nt - 1 - i

        @pl.when(i == 0)
        def _():
            lcar[...] = jnp.zeros_like(lcar)
            dxn[...] = jnp.zeros_like(dxn)
            for ref in (dwr_ref, dwi_ref, dcw_ref, dcb_ref, dbr_ref, dbi_ref, dlam_ref):
                ref[...] = jnp.zeros_like(ref)

        first = (tt > 0).astype(F32)
        za = za_ref[...]
        xa, sh, r, ig, sp, a, m, m2 = _lru_gates(za, zh_ref[...] * first, cw_ref, cb_ref, wr_ref, br_ref, wi_ref,
                                                 bi_ref, lam_ref)
        hs_v = hs_ref[...]
        hprev = _down(jnp.concatenate([hh_ref[...] * first, hs_v], axis=0), 1)[SUBLANES:SUBLANES + t]
        ga = ga_ref[...]
        sg = _sig(ga)
        silu = ga * sg
        dya = dy_ref[...]
        dga_ref[...] = (dya * hs_v * (sg * (1.0 + ga * (1.0 - sg)))).astype(dga_ref.dtype)
        row = _rows((t, c))
        acc_h = dya * silu + jnp.where(row == t - 1, lcar[...], 0.0)
        acc_b = jnp.where(row < t - 1, _up(a, 1), 0.0)
        k = 1
        while k < t:
            acc_h = acc_h + acc_b * _up(acc_h, k)
            acc_b = acc_b * _up(acc_b, k)
            k *= 2
        lmb = acc_h
        lcar[...] = jnp.sum(jnp.where(row == 0, a * lmb, 0.0), axis=0, keepdims=True)
        da = lmb * hprev
        dxa = lmb * m * ig
        di = lmb * m * xa
        dm = lmb * ig * xa
        dla = da * a - dm * (a * a) * lax.rsqrt(m2)
        dr = dla * ((-LRU_C) * sp)
        lam = lam_ref[...]
        dsp = jnp.sum(dla * ((-LRU_C) * r), axis=0, keepdims=True)
        dlam_ref[...] += dsp * (-1.0 / (1.0 + jnp.exp(lam)))
        dpr = dr * r * (1.0 - r)
        dpi = di * ig * (1.0 - ig)
        dbr_ref[...] += jnp.sum(dpr, axis=0, keepdims=True)
        dbi_ref[...] += jnp.sum(dpi, axis=0, keepdims=True)
        dwr_ref[...] += _dot_tn(xa, dpr)
        dwi_ref[...] += _dot_tn(xa, dpi)
        dxa = dxa + _dot_nt(dpr, wr_ref[...]) + _dot_nt(dpi, wi_ref[...])
        dcb_ref[...] += jnp.sum(dxa, axis=0, keepdims=True)
        for k in range(CONV_WIDTH):
            dcw_ref[k:k + 1, :] += jnp.sum(dxa * sh[CONV_WIDTH - 1 - k], axis=0, keepdims=True)
        ext = jnp.concatenate([dxa, dxn[...]], axis=0)
        dza = cw_ref[3:4, :] * dxa
        for j in (1, 2, 3):
            dza = dza + cw_ref[3 - j:4 - j, :] * _up(ext, j)[:t]
        dza_ref[...] = dza.astype(dza_ref.dtype)
        dxn[...] = dxa[:SUBLANES]

    full = lambda shp: pl.BlockSpec(shp, lambda i: (0, 0))
    rev = lambda i: nt - 1 - i
    halo = lambda i: (jnp.maximum((nt - 1 - i) * hb - 1, 0), 0)
    outs = pl.pallas_call(
        body, name="lru_bwd", grid=(nt,),
        in_specs=[pl.BlockSpec((t, c), lambda i: (rev(i), 0)), pl.BlockSpec((SUBLANES, c), halo),
                  pl.BlockSpec((t, c), lambda i: (rev(i), 1)),
                  pl.BlockSpec((t, c), lambda i: (rev(i), 0)), pl.BlockSpec((SUBLANES, c), halo),
                  pl.BlockSpec((t, c), lambda i: (rev(i), 0)),
                  lay((CONV_WIDTH, c)), lay((1, c)), lay((c, c)), lay((1, c)), lay((c, c)), lay((1, c)),
                  lay((1, c))],
        out_specs=[pl.BlockSpec((t, c), lambda i: (rev(i), 0)), pl.BlockSpec((t, c), lambda i: (rev(i), 0)),
                   full((c, c)), full((c, c)), full((CONV_WIDTH, c)), full((1, c)), full((1, c)), full((1, c)),
                   full((1, c))],
        out_shape=[jax.ShapeDtypeStruct((s, c), MXU_DTYPE), jax.ShapeDtypeStruct((s, c), MXU_DTYPE),
                   jax.ShapeDtypeStruct((c, c), F32), jax.ShapeDtypeStruct((c, c), F32),
                   jax.ShapeDtypeStruct((CONV_WIDTH, c), F32)] + [jax.ShapeDtypeStruct((1, c), F32)] * 4,
        scratch_shapes=[pltpu.VMEM((1, c), F32), pltpu.VMEM((SUBLANES, c), F32)],
        compiler_params=_cp(("arbitrary",), 48),
    )(z, z, z, hs, hs, dy, cw, cb, wr, br, wi, bi, lam)
    return outs


POOL_HALO = 16


def _pool_select(lane, v2, v4, v8, v16):
    return jnp.where(lane < 64, v2, jnp.where(lane < 128, v4, jnp.where(lane < 192, v8, v16)))


def _pool_counts(t0, t, c):
    lane = _lanes((t, c))
    win = _pool_select(lane, 2.0, 4.0, 8.0, 16.0)
    seen = (t0 + _rows((t, c)) + 1).astype(F32)
    return lane, jnp.minimum(seen, win)


def _pooled(zc, halo, lane, cnt):
    t = zc.shape[0]
    ext = jnp.concatenate([halo, zc], axis=0)
    s2 = ext + _down(ext, 1)
    s4 = s2 + _down(s2, 2)
    s8 = s4 + _down(s4, 4)
    s16 = s8 + _down(s8, 8)
    cut = lambda v: v[POOL_HALO:POOL_HALO + t]
    return _pool_select(lane, cut(s2), cut(s4), cut(s8), cut(s16)) / cnt - zc


def _pool_fwd(z, wp, ps, l):
    s = z.shape[0]
    t = _wide_tile(s)
    c = D_POOL
    lay = _lay(l)

    def body(zc_ref, gc_ref, wp_ref, ps_ref, yc_ref, zprev):
        i = pl.program_id(0)

        @pl.when(i == 0)
        def _():
            zprev[...] = jnp.zeros_like(zprev)

        zc = zc_ref[...]
        lane, cnt = _pool_counts(i * t, t, c)
        pooled = _pooled(zc, zprev[...], lane, cnt)
        pc = _dot(pooled, wp_ref[...])
        gc = gc_ref[...]
        yc_ref[...] = (pc * ps_ref[...] * (gc * _sig(gc))).astype(yc_ref.dtype)
        zprev[...] = zc_ref[t - POOL_HALO:t, :]

    return pl.pallas_call(
        body, name="pool_fwd", grid=(s // t,),
        in_specs=[pl.BlockSpec((t, c), lambda i: (i, 7)), pl.BlockSpec((t, c), lambda i: (i, 8)),
                  lay((c, c)), lay((1, c))],
        out_specs=pl.BlockSpec((t, c), lambda i: (i, 0)),
        out_shape=jax.ShapeDtypeStruct((s, c), MXU_DTYPE),
        scratch_shapes=[pltpu.VMEM((POOL_HALO, c), F32)],
        compiler_params=_cp(("arbitrary",), 32),
    )(z, z, wp, ps)


def _pool_bwd(z, dy, wp, ps, l):
    s = z.shape[0]
    t = _wide_tile(s)
    lay = _lay(l)
    nt = s // t
    c = D_POOL
    hb = t // POOL_HALO

    def body(zc_ref, zh_ref, gc_ref, dy_ref, wp_ref, ps_ref, dzc_ref, dgc_ref, dwp_ref, dps_ref, ddn):
        i = pl.program_id(0)
        tt = nt - 1 - i

        @pl.when(i == 0)
        def _():
            ddn[...] = jnp.zeros_like(ddn)
            dwp_ref[...] = jnp.zeros_like(dwp_ref)
            dps_ref[...] = jnp.zeros_like(dps_ref)

        first = (tt > 0).astype(F32)
        zc = zc_ref[...]
        lane, cnt = _pool_counts(tt * t, t, c)
        pooled = _pooled(zc, zh_ref[...] * first, lane, cnt)
        pc = _dot(pooled, wp_ref[...])
        gc = gc_ref[...]
        sg = _sig(gc)
        silu = gc * sg
        dyc = dy_ref[...]
        ps_v = ps_ref[...]
        dgc_ref[...] = (dyc * pc * ps_v * (sg * (1.0 + gc * (1.0 - sg)))).astype(dgc_ref.dtype)
        dps_ref[...] += jnp.sum(dyc * pc * silu, axis=0, keepdims=True)
        dpc = dyc * ps_v * silu
        dwp_ref[...] += _dot_tn(pooled, dpc)
        dpooled = _dot_nt(dpc, wp_ref[...])
        dd = dpooled / cnt
        ext = jnp.concatenate([dd, ddn[...]], axis=0)
        f2 = ext + _up(ext, 1)
        f4 = f2 + _up(f2, 2)
        f8 = f4 + _up(f4, 4)
        f16 = f8 + _up(f8, 8)
        dzc = _pool_select(lane, f2[:t], f4[:t], f8[:t], f16[:t]) - dpooled
        dzc_ref[...] = dzc.astype(dzc_ref.dtype)
        ddn[...] = dd[:POOL_HALO]

    full = lambda shp: pl.BlockSpec(shp, lambda i: (0, 0))
    rev = lambda i: nt - 1 - i
    return pl.pallas_call(
        body, name="pool_bwd", grid=(nt,),
        in_specs=[pl.BlockSpec((t, c), lambda i: (rev(i), 7)),
                  pl.BlockSpec((POOL_HALO, c), lambda i: (jnp.maximum(rev(i) * hb - 1, 0), 7)),
                  pl.BlockSpec((t, c), lambda i: (rev(i), 8)),
                  pl.BlockSpec((t, c), lambda i: (rev(i), 0)),
                  lay((c, c)), lay((1, c))],
        out_specs=[pl.BlockSpec((t, c), lambda i: (rev(i), 0)), pl.BlockSpec((t, c), lambda i: (rev(i), 0)),
                   full((c, c)), full((1, c))],
        out_shape=[jax.ShapeDtypeStruct((s, c), MXU_DTYPE), jax.ShapeDtypeStruct((s, c), MXU_DTYPE),
                   jax.ShapeDtypeStruct((c, c), F32), jax.ShapeDtypeStruct((1, c), F32)],
        scratch_shapes=[pltpu.VMEM((POOL_HALO, c), F32)],
        compiler_params=_cp(("arbitrary",), 32),
    )(z, z, z, dy, wp, ps)


def _rope_tables(s):
    pos = jnp.arange(s, dtype=F32)
    inv_freq = ROPE_BASE ** (-jnp.arange(0, QK_ROPE, 2, dtype=F32) / QK_ROPE)
    ang = pos[:, None] * inv_freq[None, :]
    cos, sin = jnp.cos(ang), jnp.sin(ang)
    half = QK_ROPE // 2
    ones = jnp.ones((s, QK_NOPE), F32)
    z64 = jnp.zeros((s, QK_NOPE), F32)
    zh = jnp.zeros((s, half), F32)
    z32 = jnp.zeros((s, HEAD_BLK - QK_NOPE - QK_ROPE), F32)
    c_t = jnp.concatenate([ones, cos, cos, z32], axis=1)
    s1_t = jnp.concatenate([z64, -sin, zh, z32], axis=1)
    s2_t = jnp.concatenate([z64, zh, sin, z32], axis=1)
    return c_t, s1_t, s2_t


def _rope(x, c_t, s1_t, s2_t):
    return x * c_t + pltpu.roll(x, HEAD_BLK - 16, 1) * s1_t + pltpu.roll(x, 16, 1) * s2_t


def _unrope(d, c_t, s1_t, s2_t):
    return d * c_t + pltpu.roll(d * s1_t, 16, 1) + pltpu.roll(d * s2_t, HEAD_BLK - 16, 1)


def _mla_prep_fwd(z, gq, gkv, wuq, wukv, c_t, s1_t, s2_t, l):
    s = z.shape[0]
    t = _wide_tile(s)
    hq = MLA_HEADS * HEAD_BLK
    lay = _lay(l)

    def body(cq_ref, ckv_ref, kr_ref, gq_ref, gkv_ref, wuq_ref, wukv_ref, c_ref, s1_ref, s2_ref,
             q_ref, k_ref, v_ref, qn_ref, kvn_ref):
        ct, s1, s2 = c_ref[...], s1_ref[...], s2_ref[...]
        cq = cq_ref[...]
        qn = (cq * lax.rsqrt(jnp.mean(cq * cq, axis=-1, keepdims=True) + EPS) * gq_ref[...]).astype(MXU_DTYPE)
        qn_ref[...] = qn
        q = jnp.dot(qn, wuq_ref[...], preferred_element_type=F32)
        ckv = ckv_ref[...]
        kvn = (ckv * lax.rsqrt(jnp.mean(ckv * ckv, axis=-1, keepdims=True) + EPS) * gkv_ref[...]).astype(MXU_DTYPE)
        kvn_ref[...] = kvn
        kvp = jnp.dot(kvn, wukv_ref[...], preferred_element_type=F32)
        krr = _rope(kr_ref[...], ct, s1, s2)
        for h in range(MLA_HEADS):
            blk = slice(h * HEAD_BLK, (h + 1) * HEAD_BLK)
            q_ref[:, blk] = (_rope(q[:, blk], ct, s1, s2) * Q_PRESCALE).astype(q_ref.dtype)
            k_ref[:, blk] = (kvp[:, blk] + krr).astype(k_ref.dtype)
        v_ref[...] = kvp[:, hq:].astype(v_ref.dtype)

    tab = pl.BlockSpec((t, HEAD_BLK), lambda i: (i, 0))
    return pl.pallas_call(
        body, name="mla_prep_fwd", grid=(s // t,),
        in_specs=[pl.BlockSpec((t, Q_RANK), lambda i: (i, 2)), pl.BlockSpec((t, KV_RANK), lambda i: (i, 6)),
                  pl.BlockSpec((t, HEAD_BLK), lambda i: (i, 18)),
                  lay((1, Q_RANK)), lay((1, KV_RANK)), lay((Q_RANK, hq)), lay((KV_RANK, hq + D_MLA)),
                  tab, tab, tab],
        out_specs=[pl.BlockSpec((t, hq), lambda i: (i, 0)), pl.BlockSpec((t, hq), lambda i: (i, 0)),
                   pl.BlockSpec((t, D_MLA), lambda i: (i, 0)), pl.BlockSpec((t, Q_RANK), lambda i: (i, 0)),
                   pl.BlockSpec((t, KV_RANK), lambda i: (i, 0))],
        out_shape=[jax.ShapeDtypeStruct((s, hq), MXU_DTYPE), jax.ShapeDtypeStruct((s, hq), MXU_DTYPE),
                   jax.ShapeDtypeStruct((s, D_MLA), MXU_DTYPE), jax.ShapeDtypeStruct((s, Q_RANK), MXU_DTYPE),
                   jax.ShapeDtypeStruct((s, KV_RANK), MXU_DTYPE)],
        compiler_params=_cp(("parallel",), 56),
    )(z, z, z, gq, gkv, wuq, wukv, c_t, s1_t, s2_t)


SUM_LANE_A = V_DIM
SUM_LANE_B = 0
FLASH_TILES_PER_TRIP = 4


def _loop_in_trips(lo, hi, per_trip, step, carry):
    whole = (hi - lo) // per_trip

    def trip(n, cr):
        for u in range(per_trip):
            cr = step(lo + per_trip * n + u, cr)
        return cr

    carry = lax.fori_loop(0, whole, trip, carry)
    return lax.fori_loop(lo + per_trip * whole, hi, step, carry)


def _flash_fwd(q, k, v, z):
    s = q.shape[0]
    t = _tile(s)
    nq = s // t
    pw = 2 * HEAD_BLK

    def body(q_ref, k_ref, v_ref, gb_ref, o_ref, yb_ref, lse_ref):
        i = pl.program_id(1)
        qv = q_ref[...]
        qa, qb = qv[:, :HEAD_BLK], qv[:, HEAD_BLK:]
        lane = _lanes((t, HEAD_BLK))
        lo = lane < V_DIM

        def update(qh, kh, vh, m, acc, masked):
            sc = _dot_nt(qh, kh)
            if masked:
                sc = jnp.where(_lanes((t, t)) <= _rows((t, t)), sc, -1e30)
            m_new = jnp.maximum(m, jnp.max(sc, axis=-1, keepdims=True))
            p = jnp.exp2(sc - m_new).astype(MXU_DTYPE)
            return m_new, acc * jnp.exp2(m - m_new) + _dot(p, vh)

        def step(j, carry, masked):
            ma, mb, acc_a, acc_b = carry
            kv_rows = pl.ds(pl.multiple_of(j * t, t), t)
            kt = k_ref[kv_rows, :]
            vt = v_ref[kv_rows, :]
            lane_v = _lanes(vt.shape)
            one = jnp.ones_like(vt)
            zero_v = jnp.zeros_like(vt)
            v_a = jnp.where(lane_v < V_DIM, vt, jnp.where(lane_v == SUM_LANE_A, one, zero_v))
            v_b = jnp.where(lane_v >= V_DIM, vt, jnp.where(lane_v == SUM_LANE_B, one, zero_v))
            ma, acc_a = update(qa, kt[:, :HEAD_BLK], v_a, ma, acc_a, masked)
            mb, acc_b = update(qb, kt[:, HEAD_BLK:], v_b, mb, acc_b, masked)
            return ma, mb, acc_a, acc_b

        neg = jnp.full((t, 1), -1e30, F32)
        zero = jnp.zeros((t, HEAD_BLK), F32)
        carry = _loop_in_trips(0, i, FLASH_TILES_PER_TRIP, lambda j, cr: step(j, cr, False), (neg, neg, zero, zero))
        ma, mb, acc_a, acc_b = step(i, carry, True)
        la = jnp.sum(jnp.where(lane == SUM_LANE_A, acc_a, 0.0), axis=-1, keepdims=True)
        lb = jnp.sum(jnp.where(lane == SUM_LANE_B, acc_b, 0.0), axis=-1, keepdims=True)
        o = jnp.where(lo, acc_a * (1.0 / la), acc_b * (1.0 / lb))
        o_ref[...] = o
        gb = gb_ref[...]
        yb_ref[...] = (o * (gb * _sig(gb))).astype(yb_ref.dtype)
        lse = jnp.where(lo, ma + jnp.log(la) * LOG2_E, mb + jnp.log(lb) * LOG2_E)
        pick = ((_rows((SUBLANES, HEAD_BLK)) == 0) & (_lanes((SUBLANES, HEAD_BLK)) == 0)) | (
            (_rows((SUBLANES, HEAD_BLK)) == 1) & (_lanes((SUBLANES, HEAD_BLK)) == V_DIM))
        lse_ref[0, 0] = lax.dot_general(pick.astype(F32), lse, (((1,), (1,)), ((), ())),
                                        precision=lax.Precision.HIGHEST, preferred_element_type=F32)

    return pl.pallas_call(
        body, name="flash_fwd", grid=(N_PAIR, nq),
        in_specs=[pl.BlockSpec((t, pw), lambda p, i: (i, p)), pl.BlockSpec((s, pw), lambda p, i: (0, p)),
                  pl.BlockSpec((s, HEAD_BLK), lambda p, i: (0, p)),
                  pl.BlockSpec((t, HEAD_BLK), lambda p, i: (i, 9 + p))],
        out_specs=[pl.BlockSpec((t, HEAD_BLK), lambda p, i: (i, p)), pl.BlockSpec((t, HEAD_BLK), lambda p, i: (i, p)),
                   pl.BlockSpec((1, 1, SUBLANES, t), lambda p, i: (p, i, 0, 0))],
        out_shape=[jax.ShapeDtypeStruct((s, D_MLA), F32), jax.ShapeDtypeStruct((s, D_MLA), MXU_DTYPE),
                   jax.ShapeDtypeStruct((N_PAIR, nq, SUBLANES, t), F32)],
        compiler_params=_cp(("parallel", "parallel"), 48),
    )(q, k, v, z)


def _flash_bwd(q, k, v, do, lse, delta):
    s = q.shape[0]
    t = _tile(s)
    nq = s // t
    pw = 2 * HEAD_BLK

    def body(q_ref, do_ref, lse_ref, dl_ref, k_ref, v_ref, dq_ref, dk_ref, dv_ref):
        j = pl.program_id(1)

        @pl.when(j == 0)
        def _():
            dq_ref[...] = jnp.zeros_like(dq_ref)

        kt = k_ref[...]
        ka, kb = kt[:, :HEAD_BLK], kt[:, HEAD_BLK:]
        vt = v_ref[...]

        def head(kh, qh, do_h, lse_row, dl_row, masked):
            st = _dot_nt(kh, qh)
            if masked:
                st = jnp.where(_rows((t, t)) <= _lanes((t, t)), st, -1e30)
            pt = jnp.exp2(st - lse_row)
            dv_h = _dot(pt, do_h)
            dst = (pt * (_dot_nt(vt, do_h) - dl_row)).astype(MXU_DTYPE)
            return dv_h, _dot(dst, qh), _dot_tn(dst, kh)

        def step(i, carry, masked):
            dka, dkb, dv = carry
            q_rows = pl.ds(pl.multiple_of(i * t, t), t)
            qv = q_ref[q_rows, :]
            dov = do_ref[q_rows, :]
            lane = _lanes(dov.shape)
            do_lo = jnp.where(lane < V_DIM, dov, jnp.zeros_like(dov))
            do_hi = jnp.where(lane >= V_DIM, dov, jnp.zeros_like(dov))
            dva, dk_a, dq_a = head(ka, qv[:, :HEAD_BLK], do_lo, lse_ref[0, i, 0:1, :], dl_ref[0, i, 0:1, :], masked)
            dvb, dk_b, dq_b = head(kb, qv[:, HEAD_BLK:], do_hi, lse_ref[0, i, 1:2, :], dl_ref[0, i, 1:2, :], masked)
            dq_ref[q_rows, 0:HEAD_BLK] += dq_a
            dq_ref[q_rows, HEAD_BLK:pw] += dq_b
            return dka + dk_a, dkb + dk_b, dv + dva + dvb

        zero = jnp.zeros((t, HEAD_BLK), F32)
        carry = step(j, (zero, zero, zero), True)
        dka, dkb, dv = _loop_in_trips(j + 1, nq, FLASH_TILES_PER_TRIP, lambda i, cr: step(i, cr, False), carry)
        dk_ref[:, 0:HEAD_BLK] = dka * LN_2
        dk_ref[:, HEAD_BLK:pw] = dkb * LN_2
        dv_ref[...] = dv.astype(dv_ref.dtype)

    return pl.pallas_call(
        body, name="flash_bwd", grid=(N_PAIR, nq),
        in_specs=[pl.BlockSpec((s, pw), lambda p, j: (0, p)), pl.BlockSpec((s, HEAD_BLK), lambda p, j: (0, p)),
                  pl.BlockSpec((1, nq, SUBLANES, t), lambda p, j: (p, 0, 0, 0)),
                  pl.BlockSpec((1, nq, SUBLANES, t), lambda p, j: (p, 0, 0, 0)),
                  pl.BlockSpec((t, pw), lambda p, j: (j, p)), pl.BlockSpec((t, HEAD_BLK), lambda p, j: (j, p))],
        out_specs=[pl.BlockSpec((s, pw), lambda p, j: (0, p)), pl.BlockSpec((t, pw), lambda p, j: (j, p)),
                   pl.BlockSpec((t, HEAD_BLK), lambda p, j: (j, p))],
        out_shape=[jax.ShapeDtypeStruct((s, MLA_HEADS * HEAD_BLK), F32),
                   jax.ShapeDtypeStruct((s, MLA_HEADS * HEAD_BLK), F32),
                   jax.ShapeDtypeStruct((s, D_MLA), MXU_DTYPE)],
        compiler_params=_cp(("parallel", "arbitrary"), 56),
    )(q, do, lse, delta, k, v)


def _mla_prep_bwd(dq, dk, dv, z, qn, kvn, gq, gkv, wuq, wukv, c_t, s1_t, s2_t, l):
    s = z.shape[0]
    t = _tile(s)
    hq = MLA_HEADS * HEAD_BLK
    lay = _lay(l)

    def body(dq_ref, dk_ref, dv_ref, cq_ref, ckv_ref, qn_ref, kvn_ref, gq_ref, gkv_ref, wuq_ref, wukv_ref,
             c_ref, s1_ref, s2_ref, dcq_ref, dckv_ref, dkr_ref, dwuq_ref, dwukv_ref, dgq_ref, dgkv_ref,
             dqu_ref, dkvp_ref):
        @pl.when(pl.program_id(0) == 0)
        def _():
            for ref in (dwuq_ref, dwukv_ref, dgq_ref, dgkv_ref):
                ref[...] = jnp.zeros_like(ref)

        ct, s1, s2 = c_ref[...], s1_ref[...], s2_ref[...]
        dk_sum = jnp.zeros((t, HEAD_BLK), F32)
        for h in range(MLA_HEADS):
            blk = slice(h * HEAD_BLK, (h + 1) * HEAD_BLK)
            dqu_ref[:, blk] = _unrope(dq_ref[:, blk] * ATT_SCALE, ct, s1, s2).astype(dqu_ref.dtype)
            dkh = dk_ref[:, blk]
            dk_sum = dk_sum + dkh
            dkvp_ref[:, blk] = dkh.astype(dkvp_ref.dtype)
        dkvp_ref[:, hq:] = dv_ref[...]
        lane = _lanes((t, HEAD_BLK))
        rope_lanes = (lane >= KR_LANE0) & (lane < KR_LANE0 + QK_ROPE)
        dkr_ref[...] = _unrope(jnp.where(rope_lanes, dk_sum, 0.0), ct, s1, s2).astype(dkr_ref.dtype)

        def norm_bwd(c_in, g, dn_out, dc_ref, dg_ref):
            rs = lax.rsqrt(jnp.mean(c_in * c_in, axis=-1, keepdims=True) + EPS)
            n = c_in * rs
            dg_ref[...] += jnp.sum(dn_out * n, axis=0, keepdims=True)
            dn = dn_out * g
            dc_ref[...] = (rs * (dn - n * jnp.mean(dn * n, axis=-1, keepdims=True))).astype(dc_ref.dtype)

        dqu, dkvp = dqu_ref[...], dkvp_ref[...]
        dwuq_ref[...] += _dot_tn(qn_ref[...], dqu)
        dwukv_ref[...] += _dot_tn(kvn_ref[...], dkvp)
        norm_bwd(cq_ref[...], gq_ref[...], _dot_nt(dqu, wuq_ref[...]), dcq_ref, dgq_ref)
        norm_bwd(ckv_ref[...], gkv_ref[...], _dot_nt(dkvp, wukv_ref[...]), dckv_ref, dgkv_ref)

    full = lambda shp: pl.BlockSpec(shp, lambda i: (0, 0))
    tab = pl.BlockSpec((t, HEAD_BLK), lambda i: (i, 0))
    row = lambda w: pl.BlockSpec((t, w), lambda i: (i, 0))
    return pl.pallas_call(
        body, name="mla_prep_bwd", grid=(s // t,),
        in_specs=[row(hq), row(hq), row(D_MLA),
                  pl.BlockSpec((t, Q_RANK), lambda i: (i, 2)), pl.BlockSpec((t, KV_RANK), lambda i: (i, 6)),
                  row(Q_RANK), row(KV_RANK),
                  lay((1, Q_RANK)), lay((1, KV_RANK)), lay((Q_RANK, hq)), lay((KV_RANK, hq + D_MLA)),
                  tab, tab, tab],
        out_specs=[row(Q_RANK), row(KV_RANK), row(HEAD_BLK), full((Q_RANK, hq)), full((KV_RANK, hq + D_MLA)),
                   full((1, Q_RANK)), full((1, KV_RANK))],
        out_shape=[jax.ShapeDtypeStruct((s, Q_RANK), MXU_DTYPE), jax.ShapeDtypeStruct((s, KV_RANK), MXU_DTYPE),
                   jax.ShapeDtypeStruct((s, HEAD_BLK), MXU_DTYPE),
                   jax.ShapeDtypeStruct((Q_RANK, hq), F32), jax.ShapeDtypeStruct((KV_RANK, hq + D_MLA), F32),
                   jax.ShapeDtypeStruct((1, Q_RANK), F32), jax.ShapeDtypeStruct((1, KV_RANK), F32)],
        scratch_shapes=[pltpu.VMEM((t, hq), MXU_DTYPE), pltpu.VMEM((t, hq + D_MLA), MXU_DTYPE)],
        compiler_params=_cp(("arbitrary",), 48),
    )(dq, dk, dv, z, z, qn, kvn, gq, gkv, wuq, wukv, c_t, s1_t, s2_t)


def _outproj_fwd(x, ya, yb, yc, w, l):
    s = x.shape[0]
    t = _wide_tile(s)
    lay = _lay(l)

    def body(x_ref, ya_ref, yb_ref, yc_ref, wa_ref, wb_ref, wc_ref, o_ref):
        o_ref[...] = (x_ref[...] + _dot(ya_ref[...], wa_ref[...]) + _dot(yb_ref[...], wb_ref[...])
                      + _dot(yc_ref[...], wc_ref[...]))

    row = lambda w_: pl.BlockSpec((t, w_), lambda i: (i, 0))
    return pl.pallas_call(
        body, name="outproj_fwd", grid=(s // t,),
        in_specs=[row(D_MODEL), row(D_LRU), row(D_MLA), row(D_POOL),
                  lay((D_LRU, D_MODEL), 0), lay((D_MLA, D_MODEL), 1), lay((D_POOL, D_MODEL), 3)],
        out_specs=row(D_MODEL),
        out_shape=jax.ShapeDtypeStruct((s, D_MODEL), F32),
        compiler_params=_cp(("parallel",), 40),
    )(x, ya, yb, yc, w, w, w)


def _outproj_bwd(dx, ya, yb, yc, o, z, w, l):
    s = dx.shape[0]
    t = _tile(s)
    nq = s // t
    rows = N_PAIR * SUBLANES

    def body(dx_ref, ya_ref, yb_ref, yc_ref, o_ref, gb_ref, w_ref, dya_ref, dyc_ref, do_ref, dgb_ref, dl_ref, dw_ref):
        @pl.when(pl.program_id(0) == 0)
        def _():
            dw_ref[...] = jnp.zeros_like(dw_ref)

        dxv = dx_ref[...].astype(MXU_DTYPE)
        dy = _dot_nt(dxv, w_ref[...])
        dw_ref[0:D_LRU, :] += _dot_tn(ya_ref[...], dxv)
        dw_ref[D_LRU:D_LRU + D_MLA, :] += _dot_tn(yb_ref[...], dxv)
        dw_ref[D_LRU + D_MLA:D_MIX, :] += _dot_tn(yc_ref[...], dxv)
        dya_ref[...] = dy[:, :D_LRU]
        dyc_ref[...] = dy[:, D_LRU + D_MLA:]
        dyb = dy[:, D_LRU:D_LRU + D_MLA]
        gb = gb_ref[...]
        sg = _sig(gb)
        ov = o_ref[...]
        do = dyb * (gb * sg)
        do_ref[...] = do.astype(do_ref.dtype)
        dgb_ref[...] = (dyb * ov * (sg * (1.0 + gb * (1.0 - sg)))).astype(dgb_ref.dtype)
        r = _rows((rows, D_MLA))
        head = (r // SUBLANES) * 2 + (r % SUBLANES)
        sel = ((r % SUBLANES) < 2) & (_lanes((rows, D_MLA)) // V_DIM == head)
        dl = lax.dot_general(sel.astype(F32), do * ov, (((1,), (1,)), ((), ())),
                             precision=lax.Precision.HIGHEST, preferred_element_type=F32)
        for p in range(N_PAIR):
            dl_ref[p, 0] = dl[p * SUBLANES:(p + 1) * SUBLANES]

    row = lambda w_: pl.BlockSpec((t, w_), lambda i: (i, 0))
    return pl.pallas_call(
        body, name="outproj_bwd", grid=(nq,),
        in_specs=[row(D_MODEL), row(D_LRU), row(D_MLA), row(D_POOL), row(D_MLA),
                  pl.BlockSpec((t, D_MLA), lambda i: (i, 3)), _lay(l)((D_MIX, D_MODEL))],
        out_specs=[row(D_LRU), row(D_POOL), row(D_MLA), row(D_MLA),
                   pl.BlockSpec((N_PAIR, 1, SUBLANES, t), lambda i: (0, i, 0, 0)),
                   pl.BlockSpec((D_MIX, D_MODEL), lambda i: (0, 0))],
        out_shape=[jax.ShapeDtypeStruct((s, D_LRU), F32), jax.ShapeDtypeStruct((s, D_POOL), F32),
                   jax.ShapeDtypeStruct((s, D_MLA), MXU_DTYPE), jax.ShapeDtypeStruct((s, D_MLA), MXU_DTYPE),
                   jax.ShapeDtypeStruct((N_PAIR, nq, SUBLANES, t), F32),
                   jax.ShapeDtypeStruct((D_MIX, D_MODEL), F32)],
        compiler_params=_cp(("arbitrary",), 48),
    )(dx, ya, yb, yc, o, z, w)


DZ_WIDTHS = (D_LRU, D_LRU, Q_RANK, D_MLA, KV_RANK, D_POOL, D_POOL, HEAD_BLK)


def _dwin(h, dz_parts):
    s = h.shape[0]
    t = _tile(s)

    def body(h_ref, *refs):
        o_ref = refs[-1]

        @pl.when(pl.program_id(0) == 0)
        def _():
            o_ref[...] = jnp.zeros_like(o_ref)

        o_ref[...] += _dot_tn(h_ref[...], jnp.concatenate([r[...] for r in refs[:-1]], axis=1))

    row = lambda w_: pl.BlockSpec((t, w_), lambda i: (i, 0))
    return pl.pallas_call(
        body, name="dwin", grid=(s // t,),
        in_specs=[row(D_MODEL)] + [row(wd) for wd in DZ_WIDTHS],
        out_specs=pl.BlockSpec((D_MODEL, D_INP), lambda i: (0, 0)),
        out_shape=jax.ShapeDtypeStruct((D_MODEL, D_INP), F32),
        compiler_params=_cp(("arbitrary",), 56),
    )(h, *dz_parts)


def _inproj_bwd(dz_parts, w, x, g, dxn, l):
    s = x.shape[0]
    t = _tile(s)
    lay = _lay(l)
    n_parts = len(DZ_WIDTHS)

    def body(*refs):
        part_refs = refs[:n_parts]
        w_ref, x_ref, g_ref, dxn_ref, dx_ref, dg_ref = refs[n_parts:]

        @pl.when(pl.program_id(0) == 0)
        def _():
            dg_ref[...] = jnp.zeros_like(dg_ref)

        dh = _dot_nt(jnp.concatenate([r[...] for r in part_refs], axis=1), w_ref[...])
        xv = x_ref[...]
        rs = lax.rsqrt(jnp.mean(xv * xv, axis=-1, keepdims=True) + EPS)
        n = xv * rs
        dg_ref[...] += jnp.sum(dh * n, axis=0, keepdims=True)
        dn = dh * g_ref[...]
        dx_ref[...] = dxn_ref[...] + rs * (dn - n * jnp.mean(dn * n, axis=-1, keepdims=True))

    row = lambda w_: pl.BlockSpec((t, w_), lambda i: (i, 0))
    return pl.pallas_call(
        body, name="inproj_bwd", grid=(s // t,),
        in_specs=[row(wd) for wd in DZ_WIDTHS] + [lay((D_MODEL, D_INP)), row(D_MODEL), lay((1, D_MODEL)),
                                                  row(D_MODEL)],
        out_specs=[row(D_MODEL), pl.BlockSpec((1, D_MODEL), lambda i: (0, 0))],
        out_shape=[jax.ShapeDtypeStruct((s, D_MODEL), F32), jax.ShapeDtypeStruct((1, D_MODEL), F32)],
        compiler_params=_cp(("arbitrary",), 48),
    )(*dz_parts, w, x, g, dxn)


def _loss_head(x, g, tgt):
    s = x.shape[0]
    t = _tile(s)

    def body(x_ref, g_ref, t_ref, dx_ref, loss_ref, dg_ref):
        @pl.when(pl.program_id(0) == 0)
        def _():
            loss_ref[...] = jnp.zeros_like(loss_ref)
            dg_ref[...] = jnp.zeros_like(dg_ref)

        xv = x_ref[...]
        rs = lax.rsqrt(jnp.mean(xv * xv, axis=-1, keepdims=True) + EPS)
        n = xv * rs
        gv = g_ref[...]
        e = n * gv - t_ref[...]
        loss_ref[...] += 0.5 * jnp.sum(jnp.mean(e * e, axis=-1, keepdims=True))
        dyf = e * (1.0 / D_MODEL)
        dg_ref[...] += jnp.sum(dyf * n, axis=0, keepdims=True)
        dn = dyf * gv
        dx_ref[...] = rs * (dn - n * jnp.mean(dn * n, axis=-1, keepdims=True))

    row = pl.BlockSpec((t, D_MODEL), lambda i: (i, 0))
    vec = pl.BlockSpec((1, D_MODEL), lambda i: (0, 0))
    return pl.pallas_call(
        body, name="loss_head", grid=(s // t,),
        in_specs=[row, vec, row],
        out_specs=[row, pl.BlockSpec((1, LANES), lambda i: (0, 0)), vec],
        out_shape=[jax.ShapeDtypeStruct((s, D_MODEL), F32), jax.ShapeDtypeStruct((1, LANES), F32),
                   jax.ShapeDtypeStruct((1, D_MODEL), F32)],
        compiler_params=_cp(("arbitrary",), 32),
    )(x, g, tgt)


def _block_diag(w):
    n, h, d, _ = w.shape
    return jnp.einsum('lhij,hk->lhikj', w, jnp.eye(h, dtype=w.dtype)).reshape(n, h * d, h * d)


def _diag_blocks(wfull, h):
    d = wfull.shape[-1] // h
    return jnp.stack([wfull[:, i * d:(i + 1) * d, i * d:(i + 1) * d] for i in range(h)], axis=1)


REF_TO_PERM = np.concatenate([np.arange(0, 1152), np.arange(1536, 1792),
                              np.arange(2304 + KR_LANE0, 2304 + KR_LANE0 + QK_ROPE),
                              np.arange(1152, 1536), np.arange(1792, 2304)])
N_SHARD = 4
W_IN_SHARD = D_IN // N_SHARD


def _w_in_runs():
    ref_of_perm = -np.ones(D_INP, np.int64)
    ref_of_perm[REF_TO_PERM] = np.arange(D_IN)
    perm_runs, p = [], 0
    while p < D_INP:
        r, q = ref_of_perm[p], p + 1
        if r < 0:
            while q < D_INP and ref_of_perm[q] < 0:
                q += 1
            perm_runs.append((None, q - p, 0))
        else:
            while (q < D_INP and ref_of_perm[q] == ref_of_perm[q - 1] + 1
                   and ref_of_perm[q] // W_IN_SHARD == r // W_IN_SHARD):
                q += 1
            perm_runs.append((int(r // W_IN_SHARD), int(r % W_IN_SHARD), int(r % W_IN_SHARD + q - p)))
        p = q
    shard_runs = []
    for s in range(N_SHARD):
        cols = REF_TO_PERM[s * W_IN_SHARD:(s + 1) * W_IN_SHARD]
        runs, a = [], 0
        for b in range(1, W_IN_SHARD + 1):
            if b == W_IN_SHARD or cols[b] != cols[b - 1] + 1:
                runs.append((int(cols[a]), int(cols[b - 1]) + 1))
                a = b
        shard_runs.append(runs)
    return perm_runs, shard_runs


def _permute_w_in(shards):
    perm_runs, _ = _w_in_runs()
    lead = shards[0].shape[:-1]
    parts = [jnp.zeros(lead + (a,), shards[0].dtype) if s is None else shards[s][..., a:b] for s, a, b in perm_runs]
    return jnp.concatenate(parts, axis=-1)


def _w_in_shard(wp, s):
    _, shard_runs = _w_in_runs()
    return jnp.concatenate([wp[..., a:b] for a, b in shard_runs[s]], axis=-1)


def _pad_w_uq(w):
    w4 = w.reshape(w.shape[:2] + (MLA_HEADS, QK_NOPE + QK_ROPE))
    return jnp.pad(w4, ((0, 0),) * 3 + ((0, HEAD_BLK - QK_NOPE - QK_ROPE),)).reshape(w.shape[:2] + (-1,))


def _unpad_w_uq(w):
    return w.reshape(w.shape[:2] + (MLA_HEADS, HEAD_BLK))[..., :QK_NOPE + QK_ROPE].reshape(w.shape[:2] + (-1,))


def _pad_w_ukv(w):
    w4 = w.reshape(w.shape[:2] + (MLA_HEADS, QK_NOPE + V_DIM))
    kpart = jnp.pad(w4[..., :QK_NOPE], ((0, 0),) * 3 + ((0, HEAD_BLK - QK_NOPE),)).reshape(w.shape[:2] + (-1,))
    return jnp.concatenate([kpart, w4[..., QK_NOPE:].reshape(w.shape[:2] + (-1,))], axis=2)


def _unpad_w_ukv(w):
    hq = MLA_HEADS * HEAD_BLK
    kpart = w[..., :hq].reshape(w.shape[:2] + (MLA_HEADS, HEAD_BLK))[..., :QK_NOPE]
    vpart = w[..., hq:].reshape(w.shape[:2] + (MLA_HEADS, V_DIM))
    return jnp.concatenate([kpart, vpart], axis=3).reshape(w.shape[:2] + (-1,))


def _local_step(x, tgt, w):
    s = x.shape[0]
    tabs = _rope_tables(s)
    vec = lambda a: a[:, None, :]
    mxu = lambda a: a.astype(MXU_DTYPE)
    p = dict(g=vec(w['norm_g']), w_in=mxu(w['w_in']), cw=w['conv_w'], cb=vec(w['conv_b']),
             wr=mxu(_block_diag(w['w_rg'])), br=vec(w['b_rg']), wi=mxu(_block_diag(w['w_ig'])), bi=vec(w['b_ig']),
             lam=vec(w['lru_lambda']), gq=vec(w['q_norm_g']), gkv=vec(w['kv_norm_g']),
             wuq=mxu(_pad_w_uq(w['w_uq'])), wukv=mxu(_pad_w_ukv(w['w_ukv'])),
             wp=mxu(_block_diag(w['w_pool'])), ps=vec(w['pool_scale']), wout=mxu(w['w_out']))
    lru = lambda l: (p['cw'], p['cb'], p['wr'], p['br'], p['wi'], p['bi'], p['lam'], l)
    mla = lambda l: (p['gq'], p['gkv'], p['wuq'], p['wukv'], *tabs, l)

    saved = []
    for l in range(DEPTH):
        h, z = _inproj_fwd(x, p['g'], p['w_in'], l)
        ya, hs = _lru_fwd(z, *lru(l))
        yc = _pool_fwd(z, p['wp'], p['ps'], l)
        q, k, v, qn, kvn = _mla_prep_fwd(z, *mla(l))
        o, yb, lse = _flash_fwd(q, k, v, z)
        saved.append(dict(x=x, h=h, z=z, hs=hs, ya=ya, yb=yb, yc=yc, q=q, k=k, v=v, qn=qn, kvn=kvn, o=o, lse=lse))
        x = _outproj_fwd(x, ya, yb, yc, p['wout'], l)

    dx, loss, dgf = _loss_head(x, w['final_norm_g'][None], tgt)
    per_layer = {n: [None] * DEPTH for n in WEIGHT_NAMES if n != 'final_norm_g'}
    for l in reversed(range(DEPTH)):
        sv = saved[l]
        dya, dyc, do, dgb, delta, per_layer['w_out'][l] = _outproj_bwd(
            dx, sv['ya'], sv['yb'], sv['yc'], sv['o'], sv['z'], p['wout'], l)
        dza, dga, *lru_grads = _lru_bwd(sv['z'], sv['hs'], dya, *lru(l))
        for n, g in zip(('w_rg', 'w_ig', 'conv_w', 'conv_b', 'b_rg', 'b_ig', 'lru_lambda'), lru_grads):
            per_layer[n][l] = g
        dzc, dgc, per_layer['w_pool'][l], per_layer['pool_scale'][l] = _pool_bwd(sv['z'], dyc, p['wp'], p['ps'], l)
        dq, dk, dv = _flash_bwd(sv['q'], sv['k'], sv['v'], do, sv['lse'], delta)
        (dcq, dckv, dkr, per_layer['w_uq'][l], per_layer['w_ukv'][l], per_layer['q_norm_g'][l],
         per_layer['kv_norm_g'][l]) = _mla_prep_bwd(dq, dk, dv, sv['z'], sv['qn'], sv['kvn'], *mla(l))
        dz_parts = (dza, dga, dcq, dgb, dckv, dzc, dgc, dkr)
        per_layer['w_in'][l] = _dwin(sv['h'], dz_parts)
        dx, per_layer['norm_g'][l] = _inproj_bwd(dz_parts, p['w_in'], sv['x'], p['g'], dx, l)
    grads = {n: jnp.stack(g) for n, g in per_layer.items()}
    for n in ('norm_g', 'conv_b', 'b_rg', 'b_ig', 'lru_lambda', 'q_norm_g', 'kv_norm_g', 'pool_scale'):
        grads[n] = grads[n][:, 0, :]
    grads['w_rg'] = _diag_blocks(grads['w_rg'], LRU_HEADS)
    grads['w_ig'] = _diag_blocks(grads['w_ig'], LRU_HEADS)
    grads['w_pool'] = _diag_blocks(grads['w_pool'], len(POOL_WINDOWS))
    grads['w_uq'] = _unpad_w_uq(grads['w_uq'])
    grads['w_ukv'] = _unpad_w_ukv(grads['w_ukv'])
    grads['final_norm_g'] = dgf[0]
    return loss[0, 0], dx, grads


WIRE_DTYPE = jnp.bfloat16
MESH_IDS = pl.DeviceIdType.MESH
_HBM = pl.BlockSpec(memory_space=pltpu.HBM)


def _coords():
    return lax.axis_index("x"), lax.axis_index("y"), lax.axis_index("c")


def _comm_call(body, name, arrays, out_shapes, copies_per_array):
    n = len(arrays)
    return pl.pallas_call(
        body, name=name, out_shape=out_shapes, in_specs=[_HBM] * n, out_specs=[_HBM] * n,
        scratch_shapes=[pltpu.SemaphoreType.DMA((n, copies_per_array)), pltpu.SemaphoreType.DMA((n, copies_per_array))],
    )(*arrays)


def _all_gather8(blocks, name):
    n = len(blocks)
    every = range(n)

    def body(*refs):
        x_refs, out_refs = refs[:n], refs[n:2 * n]
        send_sems, recv_sems = refs[2 * n:]
        x, y, c = _coords()
        me, sibling = (x, y, c), (x, y, 1 - c)
        chips = [(1 - x, y), (x, 1 - y), (1 - x, 1 - y)]

        def slot(t, px, py, pc):
            return out_refs[t].at[4 * px + 2 * py + pc]

        def copy(t, k, block, to, own=False):
            return pltpu.make_async_remote_copy(
                src_ref=x_refs[t] if own else slot(t, *block), dst_ref=slot(t, *block),
                send_sem=send_sems.at[t, k], recv_sem=recv_sems.at[t, k], device_id=to, device_id_type=MESH_IDS)

        first = [copy(t, 0, me, sibling, own=True) for t in every]
        first += [copy(t, 1 + j, me, (*chip, c), own=True) for j, chip in enumerate(chips) for t in every]
        for cp in first:
            cp.start()
        passed = [[copy(t, 4 + j, (*chip, c), sibling) for t in every] for j, chip in enumerate(chips)]
        for j, chip in enumerate(chips):
            for t in every:
                copy(t, 1 + j, (*chip, c), me).wait_recv()
                passed[j][t].start()
        for t in every:
            copy(t, 0, sibling, me).wait_recv()
        for j, chip in enumerate(chips):
            for t in every:
                copy(t, 4 + j, (*chip, 1 - c), me).wait_recv()
        for cp in first + [cp for group in passed for cp in group]:
            cp.wait_send()

    outs = [jax.ShapeDtypeStruct((N_DEV,) + b.shape, b.dtype) for b in blocks]
    got = _comm_call(body, name, blocks, outs, 7)
    me = 4 * lax.axis_index("x") + 2 * lax.axis_index("y") + lax.axis_index("c")
    return [lax.dynamic_update_index_in_dim(g, b, me, 0) for g, b in zip(got, blocks)]


def _sibling_send(arrs, name, which_half=False):
    n = len(arrs)

    def body(*refs):
        a_refs, out_refs = refs[:n], refs[n:2 * n]
        send_sems, recv_sems = refs[2 * n:]
        x, y, c = _coords()
        sent = [pltpu.make_async_remote_copy(
            src_ref=a_refs[t].at[1 - c] if which_half else a_refs[t], dst_ref=out_refs[t],
            send_sem=send_sems.at[t, 0], recv_sem=recv_sems.at[t, 0],
            device_id=(x, y, 1 - c), device_id_type=MESH_IDS) for t in range(n)]
        for cp in sent:
            cp.start()
        for cp in sent:
            cp.wait()

    shapes = [jax.ShapeDtypeStruct(a.shape[1:] if which_half else a.shape, a.dtype) for a in arrs]
    return _comm_call(body, name, arrs, shapes, 1)


def _chip_exchange(arrs, name):
    n = len(arrs)

    def body(*refs):
        a_refs, out_refs = refs[:n], refs[n:2 * n]
        send_sems, recv_sems = refs[2 * n:]
        x, y, c = _coords()
        copies = []
        for j, (cx, cy) in enumerate([(1 - x, y), (x, 1 - y), (1 - x, 1 - y)]):
            copies += [pltpu.make_async_remote_copy(
                src_ref=a_refs[t].at[2 * cx + cy], dst_ref=out_refs[t].at[j], send_sem=send_sems.at[t, j],
                recv_sem=recv_sems.at[t, j], device_id=(cx, cy, c), device_id_type=MESH_IDS) for t in range(n)]
        for cp in copies:
            cp.start()
        for cp in copies:
            cp.wait()

    return _comm_call(body, name, arrs, [jax.ShapeDtypeStruct((3,) + a.shape[1:], a.dtype) for a in arrs], 3)


def _sum_leading(groups, out_dtype, steps, name):
    flat = [a for g in groups for a in g]

    def body(*refs):
        ins, outs, pos = refs[:len(flat)], refs[len(flat):], 0
        for g, o_ref in zip(groups, outs):
            acc = None
            for i_ref in ins[pos:pos + len(g)]:
                for k in range(i_ref.shape[0]):
                    term = i_ref[k].astype(F32)
                    acc = term if acc is None else acc + term
            pos += len(g)
            o_ref[...] = acc.astype(o_ref.dtype)

    return pl.pallas_call(
        body, name=name, grid=(steps,),
        in_specs=[pl.BlockSpec((a.shape[0], a.shape[1] // steps, a.shape[2]), lambda i: (0, i, 0)) for a in flat],
        out_specs=[pl.BlockSpec((g[0].shape[1] // steps, g[0].shape[2]), lambda i: (i, 0)) for g in groups],
        out_shape=[jax.ShapeDtypeStruct(g[0].shape[1:], out_dtype) for g in groups],
        compiler_params=_cp(("parallel",), 40),
    )(*flat)


def _adamw_update(w_ref, g_ref, m_ref, v_ref, d_ref, mo_ref, vo_ref):
    gv = g_ref[...]
    mn = ADAM_B1 * m_ref[...] + (1.0 - ADAM_B1) * gv
    vn = ADAM_B2 * v_ref[...] + (1.0 - ADAM_B2) * (gv * gv)
    mo_ref[...] = mn
    vo_ref[...] = vn
    m_hat = mn / (1.0 - ADAM_B1 ** ADAM_STEP)
    v_hat = vn / (1.0 - ADAM_B2 ** ADAM_STEP)
    d_ref[...] = (-ADAM_LR) * (m_hat / (jnp.sqrt(v_hat) + ADAM_EPS) + ADAM_WD * w_ref[...])


def _adamw(w, g, m, v, name):
    n, r, cdim = w.shape
    tr = math.gcd(r, 512)

    def body(*refs):
        _adamw_update(*refs)

    blk = pl.BlockSpec((None, tr, cdim), lambda l, i: (l, i, 0))
    return pl.pallas_call(
        body, name=name, grid=(n, r // tr),
        in_specs=[blk] * 4, out_specs=[blk] * 3,
        out_shape=[jax.ShapeDtypeStruct(w.shape, F32)] * 3,
        compiler_params=_cp(("parallel", "parallel"), 40),
    )(w, g, m, v)


def _adamw_small(ws, gs, ms, vs, name):
    n = len(ws)

    def body(*refs):
        ins, outs = refs[:4 * n], refs[4 * n:]
        for t in range(n):
            _adamw_update(ins[t], ins[n + t], ins[2 * n + t], ins[3 * n + t], outs[t], outs[n + t], outs[2 * n + t])

    shapes = [jax.ShapeDtypeStruct(w.shape, F32) for w in ws]
    outs = pl.pallas_call(body, name=name, out_shape=shapes * 3)(*ws, *gs, *ms, *vs)
    return outs[:n], outs[n:2 * n], outs[2 * n:]


HALF = DEPTH // 2
BIG = ['w_in', 'w_uq', 'w_ukv', 'w_out']
SHARD_AXIS = {'w_in': 2, 'conv_w': 2, 'w_uq': 2, 'w_ukv': 2, 'w_out': 1}
FULL_SHAPE = {'w_in': (DEPTH, D_MODEL, D_IN), 'conv_w': (DEPTH, CONV_WIDTH, D_LRU),
              'w_uq': (DEPTH, Q_RANK, MLA_HEADS * (QK_NOPE + QK_ROPE)),
              'w_ukv': (DEPTH, KV_RANK, MLA_HEADS * (QK_NOPE + V_DIM)), 'w_out': (DEPTH, D_MIX, D_MODEL)}


def _shard_shape(n):
    shp = list(FULL_SHAPE[n])
    shp[SHARD_AXIS[n]] //= N_SHARD
    return tuple(shp)


def _rows_view(a, lead=0):
    return a.reshape(a.shape[:lead] + (-1, a.shape[-1]))


def _gather_weights(local):
    c = lax.axis_index("c")
    names = BIG + ['conv_w']
    halves = [lax.dynamic_slice_in_dim(local[n], HALF * c, HALF, axis=0) for n in names]
    halves = [h.astype(WIRE_DTYPE) if n in BIG else h for n, h in zip(names, halves)]
    got = _all_gather8(halves, "gather_weights")
    full = {}
    for n, g in zip(names, got):
        g = g.reshape((N_SHARD, DEPTH) + g.shape[2:])
        if n == 'w_in':
            full[n] = _permute_w_in([g[s] for s in range(N_SHARD)])
        else:
            full[n] = jnp.moveaxis(g, 0, SHARD_AXIS[n]).reshape(FULL_SHAPE[n])
    return full


def _shard_blocks(g, n):
    width = _shard_shape(n)[SHARD_AXIS[n]]

    def block(h, s):
        part = g[HALF * h:HALF * (h + 1)]
        if n == 'w_in':
            part = _w_in_shard(part, s)
        else:
            part = lax.slice_in_dim(part, s * width, (s + 1) * width, axis=SHARD_AXIS[n])
        return _rows_view(part)

    return jnp.stack([jnp.stack([block(h, s) for s in range(N_SHARD)]) for h in range(2)]).astype(WIRE_DTYPE)


SUM_STEPS = 8


def _reduce_big(grads):
    c = lax.axis_index("c")
    shard = 2 * lax.axis_index("x") + lax.axis_index("y")
    contrib = [_shard_blocks(grads[n], n) for n in BIG]
    from_sibling = _sibling_send(contrib, "pair_exchange_big", which_half=True)
    own_half = [lax.dynamic_index_in_dim(a, c, 0, keepdims=False) for a in contrib]
    pair_sum = _sum_leading([[_rows_view(o)[None], _rows_view(r)[None]] for o, r in zip(own_half, from_sibling)],
                            WIRE_DTYPE, SUM_STEPS, "pair_sum_big")
    to_chips = [p.reshape(a.shape[1:]) for p, a in zip(pair_sum, contrib)]
    from_chips = _chip_exchange(to_chips, "chip_exchange_big")
    own_block = [lax.dynamic_index_in_dim(a, shard, 0, keepdims=True) for a in to_chips]
    mine = _sum_leading([[o, r] for o, r in zip(own_block, from_chips)], F32, SUM_STEPS, "chip_sum_big")
    theirs = _sibling_send(mine, "sibling_big")
    both = [jnp.where(c == 0, jnp.stack([m, t]), jnp.stack([t, m])) for m, t in zip(mine, theirs)]
    return {n: b.reshape(_shard_shape(n)) for n, b in zip(BIG, both)}


SMALL = REPLICATED + ['conv_w']


def _reduce_small(grads):
    views = [grads[n].reshape(-1, LANES) if grads[n].shape[-1] < LANES else _rows_view(jnp.atleast_2d(grads[n]))
             for n in SMALL]
    sums = _sum_leading([[g] for g in _all_gather8(views, "gather_small")], F32, 1, "sum_small")
    return {n: s.reshape(grads[n].shape) for n, s in zip(SMALL, sums)}


def kernel(x, norm_g, w_in, conv_w, conv_b, w_rg, b_rg, w_ig, b_ig, lru_lambda, q_norm_g, w_uq, kv_norm_g, w_ukv, w_pool, pool_scale, w_out, final_norm_g, loss_target, m_norm_g, m_w_in, m_conv_w, m_conv_b, m_w_rg, m_b_rg, m_w_ig, m_b_ig, m_lru_lambda, m_q_norm_g, m_w_uq, m_kv_norm_g, m_w_ukv, m_w_pool, m_pool_scale, m_w_out, m_final_norm_g, v_norm_g, v_w_in, v_conv_w, v_conv_b, v_w_rg, v_b_rg, v_w_ig, v_b_ig, v_lru_lambda, v_q_norm_g, v_w_uq, v_kv_norm_g, v_w_ukv, v_w_pool, v_pool_scale, v_w_out, v_final_norm_g):
    w_loc = dict(zip(WEIGHT_NAMES, (norm_g, w_in, conv_w, conv_b, w_rg, b_rg, w_ig, b_ig, lru_lambda, q_norm_g, w_uq,
                                    kv_norm_g, w_ukv, w_pool, pool_scale, w_out, final_norm_g)))
    m_loc = dict(zip(WEIGHT_NAMES, (m_norm_g, m_w_in, m_conv_w, m_conv_b, m_w_rg, m_b_rg, m_w_ig, m_b_ig, m_lru_lambda,
                                    m_q_norm_g, m_w_uq, m_kv_norm_g, m_w_ukv, m_w_pool, m_pool_scale, m_w_out,
                                    m_final_norm_g)))
    v_loc = dict(zip(WEIGHT_NAMES, (v_norm_g, v_w_in, v_conv_w, v_conv_b, v_w_rg, v_b_rg, v_w_ig, v_b_ig, v_lru_lambda,
                                    v_q_norm_g, v_w_uq, v_kv_norm_g, v_w_ukv, v_w_pool, v_pool_scale, v_w_out,
                                    v_final_norm_g)))
    w_full = dict(w_loc)
    w_full.update(_gather_weights(w_loc))
    loss_local, dx, g_local = _local_step(x[0], loss_target[0], w_full)
    loss = lax.psum(loss_local, ("x", "y", "c"))

    grads = _reduce_big(g_local)
    g_small = _reduce_small(g_local)
    shard = 2 * lax.axis_index("x") + lax.axis_index("y")
    width = D_LRU // N_SHARD
    grads['conv_w'] = lax.dynamic_slice_in_dim(g_small['conv_w'], shard * width, width, axis=2)
    for n in REPLICATED:
        grads[n] = g_small[n]

    delta, new_m, new_v = {}, {}, {}
    for n in BIG:
        delta[n], new_m[n], new_v[n] = _adamw(w_loc[n], grads[n], m_loc[n], v_loc[n], "adamw_" + n)
    small = [[_rows_view(jnp.atleast_2d(t[n])) for n in SMALL] for t in (w_loc, grads, m_loc, v_loc)]
    for tree, outs in zip((delta, new_m, new_v), _adamw_small(*small, "adamw_small")):
        tree.update({n: a.reshape(w_loc[n].shape) for n, a in zip(SMALL, outs)})

    return (loss, dx[None], *[grads[n] for n in WEIGHT_NAMES], *[delta[n] for n in WEIGHT_NAMES],
            *[new_m[n] for n in WEIGHT_NAMES], *[new_v[n] for n in WEIGHT_NAMES])
```

```python
import math

import jax
import jax.numpy as jnp
import numpy as np
from jax import lax
from jax.experimental import pallas as pl
from jax.experimental.pallas import tpu as pltpu

F32 = jnp.float32
MXU_DTYPE = jnp.bfloat16

D_MODEL = 1024
DEPTH = 4
EPS = 1e-6
D_LRU = 384
LRU_HEADS = 6
CONV_WIDTH = 4
LRU_C = 8.0
MLA_HEADS = 6
QK_NOPE = 64
QK_ROPE = 32
V_DIM = 64
D_MLA = MLA_HEADS * V_DIM
Q_RANK = 384
KV_RANK = 256
ROPE_BASE = 10000.0
POOL_WINDOWS = (2, 4, 8, 16)
D_POOL = 256
D_MIX = D_LRU + D_MLA + D_POOL
D_IN = 2336
ATT_SCALE = (QK_NOPE + QK_ROPE) ** -0.5
LOG2_E = 1.4426950408889634
LN_2 = 0.6931471805599453
Q_PRESCALE = ATT_SCALE * LOG2_E

ADAM_LR = 0.001
ADAM_B1 = 0.9
ADAM_B2 = 0.999
ADAM_EPS = 1e-08
ADAM_WD = 0.01
ADAM_STEP = 10

LANES = 128
SUBLANES = 8
N_DEV = 8

D_INP = 2432
KR_LANE0 = 64
HEAD_BLK = 128
N_PAIR = MLA_HEADS // 2

WEIGHT_NAMES = ['norm_g', 'w_in', 'conv_w', 'conv_b', 'w_rg', 'b_rg', 'w_ig', 'b_ig', 'lru_lambda', 'q_norm_g',
                'w_uq', 'kv_norm_g', 'w_ukv', 'w_pool', 'pool_scale', 'w_out', 'final_norm_g']
SHARDED = ['w_in', 'conv_w', 'w_uq', 'w_ukv', 'w_out']
REPLICATED = [n for n in WEIGHT_NAMES if n not in SHARDED]


def _cp(sem, vmem_mb=None):
    return pltpu.CompilerParams(dimension_semantics=sem,
                                vmem_limit_bytes=None if vmem_mb is None else vmem_mb << 20)


def _dot(a, b):
    return jnp.dot(a.astype(MXU_DTYPE), b.astype(MXU_DTYPE), preferred_element_type=F32)


def _dot_nt(a, b):
    return lax.dot_general(a.astype(MXU_DTYPE), b.astype(MXU_DTYPE), (((1,), (1,)), ((), ())),
                           preferred_element_type=F32)


def _dot_tn(a, b):
    return lax.dot_general(a.astype(MXU_DTYPE), b.astype(MXU_DTYPE), (((0,), (0,)), ((), ())),
                           preferred_element_type=F32)


def _sig(x):
    return 0.5 * jnp.tanh(0.5 * x) + 0.5


def _down(x, k):
    return pltpu.roll(x, k, 0)


def _up(x, k):
    return pltpu.roll(x, x.shape[0] - k, 0)


def _rows(shape):
    return lax.broadcasted_iota(jnp.int32, shape, 0)


def _lanes(shape):
    return lax.broadcasted_iota(jnp.int32, shape, 1)


def _tile(s):
    return min(512, s)


def _wide_tile(s):
    return min(2 * _tile(s), s)


def _lay(l):
    return lambda shp, blk=0: pl.BlockSpec((None,) + shp, lambda *_: (l, blk, 0))


def _inproj_fwd(x, g, w, l):
    s = x.shape[0]
    t = _tile(s)
    lay = _lay(l)

    def body(x_ref, g_ref, w_ref, h_ref, z_ref):
        xv = x_ref[...]
        rs = lax.rsqrt(jnp.mean(xv * xv, axis=-1, keepdims=True) + EPS)
        h = (xv * rs * g_ref[...]).astype(MXU_DTYPE)
        h_ref[...] = h
        z_ref[...] = jnp.dot(h, w_ref[...], preferred_element_type=F32)

    return pl.pallas_call(
        body, name="inproj_fwd", grid=(s // t,),
        in_specs=[pl.BlockSpec((t, D_MODEL), lambda i: (i, 0)),
                  lay((1, D_MODEL)), lay((D_MODEL, D_INP))],
        out_specs=[pl.BlockSpec((t, D_MODEL), lambda i: (i, 0)),
                   pl.BlockSpec((t, D_INP), lambda i: (i, 0))],
        out_shape=[jax.ShapeDtypeStruct((s, D_MODEL), MXU_DTYPE), jax.ShapeDtypeStruct((s, D_INP), F32)],
        compiler_params=_cp(("parallel",), 40),
    )(x, g, w)


def _lru_gates(za, halo, cw_ref, cb_ref, wr_ref, br_ref, wi_ref, bi_ref, lam_ref):
    t = za.shape[0]
    ext = jnp.concatenate([halo, za], axis=0)
    sh = [za] + [_down(ext, j)[SUBLANES:SUBLANES + t] for j in (1, 2, 3)]
    xa = cb_ref[...] + cw_ref[3:4, :] * sh[0] + cw_ref[2:3, :] * sh[1] + cw_ref[1:2, :] * sh[2] + cw_ref[0:1, :] * sh[3]
    r = 1.0 / (1.0 + jnp.exp(-(_dot(xa, wr_ref[...]) + br_ref[...])))
    ig = _sig(_dot(xa, wi_ref[...]) + bi_ref[...])
    lam = lam_ref[...]
    sp = jnp.maximum(-lam, 0.0) + jnp.log(1.0 + jnp.exp(-jnp.abs(lam)))
    la = (-LRU_C) * r * sp
    a = jnp.exp(la)
    y2 = 2.0 * la
    m2 = jnp.where(y2 > -0.01, -(y2 * (1.0 + y2 * (0.5 + y2 * (1.0 / 6.0)))), 1.0 - a * a)
    return xa, sh, r, ig, sp, a, jnp.sqrt(m2), m2


def _lru_fwd(z, cw, cb, wr, br, wi, bi, lam, l):
    s = z.shape[0]
    t = _tile(s)
    c = D_LRU
    lay = _lay(l)

    def body(za_ref, ga_ref, cw_ref, cb_ref, wr_ref, br_ref, wi_ref, bi_ref, lam_ref, ya_ref, hs_ref, zprev, hcar):
        i = pl.program_id(0)

        @pl.when(i == 0)
        def _():
            zprev[...] = jnp.zeros_like(zprev)
            hcar[...] = jnp.zeros_like(hcar)

        za = za_ref[...]
        xa, _, _, ig, _, a, m, _ = _lru_gates(za, zprev[...], cw_ref, cb_ref, wr_ref, br_ref, wi_ref, bi_ref, lam_ref)
        row = _rows((t, c))
        acc_h = m * (ig * xa) + jnp.where(row == 0, a * hcar[...], 0.0)
        acc_a = jnp.where(row == 0, 0.0, a)
        k = 1
        while k < t:
            acc_h = acc_h + acc_a * _down(acc_h, k)
            acc_a = acc_a * _down(acc_a, k)
            k *= 2
        hs = acc_h
        hs_ref[...] = hs
        ga = ga_ref[...]
        ya_ref[...] = (hs * (ga * _sig(ga))).astype(ya_ref.dtype)
        hcar[...] = jnp.sum(jnp.where(row == t - 1, hs, 0.0), axis=0, keepdims=True)
        zprev[...] = za_ref[t - SUBLANES:t, :]

    return pl.pallas_call(
        body, name="lru_fwd", grid=(s // t,),
        in_specs=[pl.BlockSpec((t, c), lambda i: (i, 0)), pl.BlockSpec((t, c), lambda i: (i, 1)),
                  lay((CONV_WIDTH, c)), lay((1, c)), lay((c, c)), lay((1, c)), lay((c, c)), lay((1, c)),
                  lay((1, c))],
        out_specs=[pl.BlockSpec((t, c), lambda i: (i, 0)), pl.BlockSpec((t, c), lambda i: (i, 0))],
        out_shape=[jax.ShapeDtypeStruct((s, c), MXU_DTYPE), jax.ShapeDtypeStruct((s, c), F32)],
        scratch_shapes=[pltpu.VMEM((SUBLANES, c), F32), pltpu.VMEM((1, c), F32)],
        compiler_params=_cp(("arbitrary",), 40),
    )(z, z, cw, cb, wr, br, wi, bi, lam)


def _lru_bwd(z, hs, dy, cw, cb, wr, br, wi, bi, lam, l):
    s = z.shape[0]
    t = _tile(s)
    lay = _lay(l)
    nt = s // t
    c = D_LRU
    hb = t // SUBLANES

    def body(za_ref, zh_ref, ga_ref, hs_ref, hh_ref, dy_ref, cw_ref, cb_ref, wr_ref, br_ref, wi_ref, bi_ref, lam_ref,
             dza_ref, dga_ref, dwr_ref, dwi_ref, dcw_ref, dcb_ref, dbr_ref, dbi_ref, dlam_ref, lcar, dxn):
        i = pl.program_id(0)
        tt = nt - 1 - i

        @pl.when(i == 0)
        def _():
            lcar[...] = jnp.zeros_like(lcar)
            dxn[...] = jnp.zeros_like(dxn)
            for ref in (dwr_ref, dwi_ref, dcw_ref, dcb_ref, dbr_ref, dbi_ref, dlam_ref):
                ref[...] = jnp.zeros_like(ref)

        first = (tt > 0).astype(F32)
        za = za_ref[...]
        xa, sh, r, ig, sp, a, m, m2 = _lru_gates(za, zh_ref[...] * first, cw_ref, cb_ref, wr_ref, br_ref, wi_ref,
                                                 bi_ref, lam_ref)
        hs_v = hs_ref[...]
        hprev = _down(jnp.concatenate([hh_ref[...] * first, hs_v], axis=0), 1)[SUBLANES:SUBLANES + t]
        ga = ga_ref[...]
        sg = _sig(ga)
        silu = ga * sg
        dya = dy_ref[...]
        dga_ref[...] = (dya * hs_v * (sg * (1.0 + ga * (1.0 - sg)))).astype(dga_ref.dtype)
        row = _rows((t, c))
        acc_h = dya * silu + jnp.where(row == t - 1, lcar[...], 0.0)
        acc_b = jnp.where(row < t - 1, _up(a, 1), 0.0)
        k = 1
        while k < t:
            acc_h = acc_h + acc_b * _up(acc_h, k)
            acc_b = acc_b * _up(acc_b, k)
            k *= 2
        lmb = acc_h
        lcar[...] = jnp.sum(jnp.where(row == 0, a * lmb, 0.0), axis=0, keepdims=True)
        da = lmb * hprev
        dxa = lmb * m * ig
        di = lmb * m * xa
        dm = lmb * ig * xa
        dla = da * a - dm * (a * a) * lax.rsqrt(m2)
        dr = dla * ((-LRU_C) * sp)
        lam = lam_ref[...]
        dsp = jnp.sum(dla * ((-LRU_C) * r), axis=0, keepdims=True)
        dlam_ref[...] += dsp * (-1.0 / (1.0 + jnp.exp(lam)))
        dpr = dr * r * (1.0 - r)
        dpi = di * ig * (1.0 - ig)
        dbr_ref[...] += jnp.sum(dpr, axis=0, keepdims=True)
        dbi_ref[...] += jnp.sum(dpi, axis=0, keepdims=True)
        dwr_ref[...] += _dot_tn(xa, dpr)
        dwi_ref[...] += _dot_tn(xa, dpi)
        dxa = dxa + _dot_nt(dpr, wr_ref[...]) + _dot_nt(dpi, wi_ref[...])
        dcb_ref[...] += jnp.sum(dxa, axis=0, keepdims=True)
        for k in range(CONV_WIDTH):
            dcw_ref[k:k + 1, :] += jnp.sum(dxa * sh[CONV_WIDTH - 1 - k], axis=0, keepdims=True)
        ext = jnp.concatenate([dxa, dxn[...]], axis=0)
        dza = cw_ref[3:4, :] * dxa
        for j in (1, 2, 3):
            dza = dza + cw_ref[3 - j:4 - j, :] * _up(ext, j)[:t]
        dza_ref[...] = dza.astype(dza_ref.dtype)
        dxn[...] = dxa[:SUBLANES]

    full = lambda shp: pl.BlockSpec(shp, lambda i: (0, 0))
    rev = lambda i: nt - 1 - i
    halo = lambda i: (jnp.maximum((nt - 1 - i) * hb - 1, 0), 0)
    outs = pl.pallas_call(
        body, name="lru_bwd", grid=(nt,),
        in_specs=[pl.BlockSpec((t, c), lambda i: (rev(i), 0)), pl.BlockSpec((SUBLANES, c), halo),
                  pl.BlockSpec((t, c), lambda i: (rev(i), 1)),
                  pl.BlockSpec((t, c), lambda i: (rev(i), 0)), pl.BlockSpec((SUBLANES, c), halo),
                  pl.BlockSpec((t, c), lambda i: (rev(i), 0)),
                  lay((CONV_WIDTH, c)), lay((1, c)), lay((c, c)), lay((1, c)), lay((c, c)), lay((1, c)),
                  lay((1, c))],
        out_specs=[pl.BlockSpec((t, c), lambda i: (rev(i), 0)), pl.BlockSpec((t, c), lambda i: (rev(i), 0)),
                   full((c, c)), full((c, c)), full((CONV_WIDTH, c)), full((1, c)), full((1, c)), full((1, c)),
                   full((1, c))],
        out_shape=[jax.ShapeDtypeStruct((s, c), MXU_DTYPE), jax.ShapeDtypeStruct((s, c), MXU_DTYPE),
                   jax.ShapeDtypeStruct((c, c), F32), jax.ShapeDtypeStruct((c, c), F32),
                   jax.ShapeDtypeStruct((CONV_WIDTH, c), F32)] + [jax.ShapeDtypeStruct((1, c), F32)] * 4,
        scratch_shapes=[pltpu.VMEM((1, c), F32), pltpu.VMEM((SUBLANES, c), F32)],
        compiler_params=_cp(("arbitrary",), 48),
    )(z, z, z, hs, hs, dy, cw, cb, wr, br, wi, bi, lam)
    return outs


POOL_HALO = 16


def _pool_select(lane, v2, v4, v8, v16):
    return jnp.where(lane < 64, v2, jnp.where(lane < 128, v4, jnp.where(lane < 192, v8, v16)))


def _pool_counts(t0, t, c):
    lane = _lanes((t, c))
    win = _pool_select(lane, 2.0, 4.0, 8.0, 16.0)
    seen = (t0 + _rows((t, c)) + 1).astype(F32)
    return lane, jnp.minimum(seen, win)


def _pooled(zc, halo, lane, cnt):
    t = zc.shape[0]
    ext = jnp.concatenate([halo, zc], axis=0)
    s2 = ext + _down(ext, 1)
    s4 = s2 + _down(s2, 2)
    s8 = s4 + _down(s4, 4)
    s16 = s8 + _down(s8, 8)
    cut = lambda v: v[POOL_HALO:POOL_HALO + t]
    return _pool_select(lane, cut(s2), cut(s4), cut(s8), cut(s16)) / cnt - zc


def _pool_fwd(z, wp, ps, l):
    s = z.shape[0]
    t = _wide_tile(s)
    c = D_POOL
    lay = _lay(l)

    def body(zc_ref, gc_ref, wp_ref, ps_ref, yc_ref, zprev):
        i = pl.program_id(0)

        @pl.when(i == 0)
        def _():
            zprev[...] = jnp.zeros_like(zprev)

        zc = zc_ref[...]
        lane, cnt = _pool_counts(i * t, t, c)
        pooled = _pooled(zc, zprev[...], lane, cnt)
        pc = _dot(pooled, wp_ref[...])
        gc = gc_ref[...]
        yc_ref[...] = (pc * ps_ref[...] * (gc * _sig(gc))).astype(yc_ref.dtype)
        zprev[...] = zc_ref[t - POOL_HALO:t, :]

    return pl.pallas_call(
        body, name="pool_fwd", grid=(s // t,),
        in_specs=[pl.BlockSpec((t, c), lambda i: (i, 7)), pl.BlockSpec((t, c), lambda i: (i, 8)),
                  lay((c, c)), lay((1, c))],
        out_specs=pl.BlockSpec((t, c), lambda i: (i, 0)),
        out_shape=jax.ShapeDtypeStruct((s, c), MXU_DTYPE),
        scratch_shapes=[pltpu.VMEM((POOL_HALO, c), F32)],
        compiler_params=_cp(("arbitrary",), 32),
    )(z, z, wp, ps)


def _pool_bwd(z, dy, wp, ps, l):
    s = z.shape[0]
    t = _wide_tile(s)
    lay = _lay(l)
    nt = s // t
    c = D_POOL
    hb = t // POOL_HALO

    def body(zc_ref, zh_ref, gc_ref, dy_ref, wp_ref, ps_ref, dzc_ref, dgc_ref, dwp_ref, dps_ref, ddn):
        i = pl.program_id(0)
        tt = nt - 1 - i

        @pl.when(i == 0)
        def _():
            ddn[...] = jnp.zeros_like(ddn)
            dwp_ref[...] = jnp.zeros_like(dwp_ref)
            dps_ref[...] = jnp.zeros_like(dps_ref)

        first = (tt > 0).astype(F32)
        zc = zc_ref[...]
        lane, cnt = _pool_counts(tt * t, t, c)
        pooled = _pooled(zc, zh_ref[...] * first, lane, cnt)
        pc = _dot(pooled, wp_ref[...])
        gc = gc_ref[...]
        sg = _sig(gc)
        silu = gc * sg
        dyc = dy_ref[...]
        ps_v = ps_ref[...]
        dgc_ref[...] = (dyc * pc * ps_v * (sg * (1.0 + gc * (1.0 - sg)))).astype(dgc_ref.dtype)
        dps_ref[...] += jnp.sum(dyc * pc * silu, axis=0, keepdims=True)
        dpc = dyc * ps_v * silu
        dwp_ref[...] += _dot_tn(pooled, dpc)
        dpooled = _dot_nt(dpc, wp_ref[...])
        dd = dpooled / cnt
        ext = jnp.concatenate([dd, ddn[...]], axis=0)
        f2 = ext + _up(ext, 1)
        f4 = f2 + _up(f2, 2)
        f8 = f4 + _up(f4, 4)
        f16 = f8 + _up(f8, 8)
        dzc = _pool_select(lane, f2[:t], f4[:t], f8[:t], f16[:t]) - dpooled
        dzc_ref[...] = dzc.astype(dzc_ref.dtype)
        ddn[...] = dd[:POOL_HALO]

    full = lambda shp: pl.BlockSpec(shp, lambda i: (0, 0))
    rev = lambda i: nt - 1 - i
    return pl.pallas_call(
        body, name="pool_bwd", grid=(nt,),
        in_specs=[pl.BlockSpec((t, c), lambda i: (rev(i), 7)),
                  pl.BlockSpec((POOL_HALO, c), lambda i: (jnp.maximum(rev(i) * hb - 1, 0), 7)),
                  pl.BlockSpec((t, c), lambda i: (rev(i), 8)),
                  pl.BlockSpec((t, c), lambda i: (rev(i), 0)),
                  lay((c, c)), lay((1, c))],
        out_specs=[pl.BlockSpec((t, c), lambda i: (rev(i), 0)), pl.BlockSpec((t, c), lambda i: (rev(i), 0)),
                   full((c, c)), full((1, c))],
        out_shape=[jax.ShapeDtypeStruct((s, c), MXU_DTYPE), jax.ShapeDtypeStruct((s, c), MXU_DTYPE),
                   jax.ShapeDtypeStruct((c, c), F32), jax.ShapeDtypeStruct((1, c), F32)],
        scratch_shapes=[pltpu.VMEM((POOL_HALO, c), F32)],
        compiler_params=_cp(("arbitrary",), 32),
    )(z, z, z, dy, wp, ps)


def _rope_tables(s):
    pos = jnp.arange(s, dtype=F32)
    inv_freq = ROPE_BASE ** (-jnp.arange(0, QK_ROPE, 2, dtype=F32) / QK_ROPE)
    ang = pos[:, None] * inv_freq[None, :]
    cos, sin = jnp.cos(ang), jnp.sin(ang)
    half = QK_ROPE // 2
    ones = jnp.ones((s, QK_NOPE), F32)
    z64 = jnp.zeros((s, QK_NOPE), F32)
    zh = jnp.zeros((s, half), F32)
    z32 = jnp.zeros((s, HEAD_BLK - QK_NOPE - QK_ROPE), F32)
    c_t = jnp.concatenate([ones, cos, cos, z32], axis=1)
    s1_t = jnp.concatenate([z64, -sin, zh, z32], axis=1)
    s2_t = jnp.concatenate([z64, zh, sin, z32], axis=1)
    return c_t, s1_t, s2_t


def _rope(x, c_t, s1_t, s2_t):
    return x * c_t + pltpu.roll(x, HEAD_BLK - 16, 1) * s1_t + pltpu.roll(x, 16, 1) * s2_t


def _unrope(d, c_t, s1_t, s2_t):
    return d * c_t + pltpu.roll(d * s1_t, 16, 1) + pltpu.roll(d * s2_t, HEAD_BLK - 16, 1)


def _mla_prep_fwd(z, gq, gkv, wuq, wukv, c_t, s1_t, s2_t, l):
    s = z.shape[0]
    t = _wide_tile(s)
    hq = MLA_HEADS * HEAD_BLK
    lay = _lay(l)

    def body(cq_ref, ckv_ref, kr_ref, gq_ref, gkv_ref, wuq_ref, wukv_ref, c_ref, s1_ref, s2_ref,
             q_ref, k_ref, v_ref, qn_ref, kvn_ref):
        ct, s1, s2 = c_ref[...], s1_ref[...], s2_ref[...]
        cq = cq_ref[...]
        qn = (cq * lax.rsqrt(jnp.mean(cq * cq, axis=-1, keepdims=True) + EPS) * gq_ref[...]).astype(MXU_DTYPE)
        qn_ref[...] = qn
        q = jnp.dot(qn, wuq_ref[...], preferred_element_type=F32)
        ckv = ckv_ref[...]
        kvn = (ckv * lax.rsqrt(jnp.mean(ckv * ckv, axis=-1, keepdims=True) + EPS) * gkv_ref[...]).astype(MXU_DTYPE)
        kvn_ref[...] = kvn
        kvp = jnp.dot(kvn, wukv_ref[...], preferred_element_type=F32)
        krr = _rope(kr_ref[...], ct, s1, s2)
        for h in range(MLA_HEADS):
            blk = slice(h * HEAD_BLK, (h + 1) * HEAD_BLK)
            q_ref[:, blk] = (_rope(q[:, blk], ct, s1, s2) * Q_PRESCALE).astype(q_ref.dtype)
            k_ref[:, blk] = (kvp[:, blk] + krr).astype(k_ref.dtype)
        v_ref[...] = kvp[:, hq:].astype(v_ref.dtype)

    tab = pl.BlockSpec((t, HEAD_BLK), lambda i: (i, 0))
    return pl.pallas_call(
        body, name="mla_prep_fwd", grid=(s // t,),
        in_specs=[pl.BlockSpec((t, Q_RANK), lambda i: (i, 2)), pl.BlockSpec((t, KV_RANK), lambda i: (i, 6)),
                  pl.BlockSpec((t, HEAD_BLK), lambda i: (i, 18)),
                  lay((1, Q_RANK)), lay((1, KV_RANK)), lay((Q_RANK, hq)), lay((KV_RANK, hq + D_MLA)),
                  tab, tab, tab],
        out_specs=[pl.BlockSpec((t, hq), lambda i: (i, 0)), pl.BlockSpec((t, hq), lambda i: (i, 0)),
                   pl.BlockSpec((t, D_MLA), lambda i: (i, 0)), pl.BlockSpec((t, Q_RANK), lambda i: (i, 0)),
                   pl.BlockSpec((t, KV_RANK), lambda i: (i, 0))],
        out_shape=[jax.ShapeDtypeStruct((s, hq), MXU_DTYPE), jax.ShapeDtypeStruct((s, hq), MXU_DTYPE),
                   jax.ShapeDtypeStruct((s, D_MLA), MXU_DTYPE), jax.ShapeDtypeStruct((s, Q_RANK), MXU_DTYPE),
                   jax.ShapeDtypeStruct((s, KV_RANK), MXU_DTYPE)],
        compiler_params=_cp(("parallel",), 56),
    )(z, z, z, gq, gkv, wuq, wukv, c_t, s1_t, s2_t)


SUM_LANE_A = V_DIM
SUM_LANE_B = 0
FLASH_TILES_PER_TRIP = 4


def _loop_in_trips(lo, hi, per_trip, step, carry):
    whole = (hi - lo) // per_trip

    def trip(n, cr):
        for u in range(per_trip):
            cr = step(lo + per_trip * n + u, cr)
        return cr

    carry = lax.fori_loop(0, whole, trip, carry)
    return lax.fori_loop(lo + per_trip * whole, hi, step, carry)


def _flash_fwd(q, k, v, z):
    s = q.shape[0]
    t = _tile(s)
    nq = s // t
    pw = 2 * HEAD_BLK

    def body(q_ref, k_ref, v_ref, gb_ref, o_ref, yb_ref, lse_ref):
        i = pl.program_id(1)
        qv = q_ref[...]
        qa, qb = qv[:, :HEAD_BLK], qv[:, HEAD_BLK:]
        lane = _lanes((t, HEAD_BLK))
        lo = lane < V_DIM

        def update(qh, kh, vh, m, acc, masked):
            sc = _dot_nt(qh, kh)
            if masked:
                sc = jnp.where(_lanes((t, t)) <= _rows((t, t)), sc, -1e30)
            m_new = jnp.maximum(m, jnp.max(sc, axis=-1, keepdims=True))
            p = jnp.exp2(sc - m_new).astype(MXU_DTYPE)
            return m_new, acc * jnp.exp2(m - m_new) + _dot(p, vh)

        def step(j, carry, masked):
            ma, mb, acc_a, acc_b = carry
            kv_rows = pl.ds(pl.multiple_of(j * t, t), t)
            kt = k_ref[kv_rows, :]
            vt = v_ref[kv_rows, :]
            lane_v = _lanes(vt.shape)
            one = jnp.ones_like(vt)
            zero_v = jnp.zeros_like(vt)
            v_a = jnp.where(lane_v < V_DIM, vt, jnp.where(lane_v == SUM_LANE_A, one, zero_v))
            v_b = jnp.where(lane_v >= V_DIM, vt, jnp.where(lane_v == SUM_LANE_B, one, zero_v))
            ma, acc_a = update(qa, kt[:, :HEAD_BLK], v_a, ma, acc_a, masked)
            mb, acc_b = update(qb, kt[:, HEAD_BLK:], v_b, mb, acc_b, masked)
            return ma, mb, acc_a, acc_b

        neg = jnp.full((t, 1), -1e30, F32)
        zero = jnp.zeros((t, HEAD_BLK), F32)
        carry = _loop_in_trips(0, i, FLASH_TILES_PER_TRIP, lambda j, cr: step(j, cr, False), (neg, neg, zero, zero))
        ma, mb, acc_a, acc_b = step(i, carry, True)
        la = jnp.sum(jnp.where(lane == SUM_LANE_A, acc_a, 0.0), axis=-1, keepdims=True)
        lb = jnp.sum(jnp.where(lane == SUM_LANE_B, acc_b, 0.0), axis=-1, keepdims=True)
        o = jnp.where(lo, acc_a * (1.0 / la), acc_b * (1.0 / lb))
        o_ref[...] = o
        gb = gb_ref[...]
        yb_ref[...] = (o * (gb * _sig(gb))).astype(yb_ref.dtype)
        lse = jnp.where(lo, ma + jnp.log(la) * LOG2_E, mb + jnp.log(lb) * LOG2_E)
        pick = ((_rows((SUBLANES, HEAD_BLK)) == 0) & (_lanes((SUBLANES, HEAD_BLK)) == 0)) | (
            (_rows((SUBLANES, HEAD_BLK)) == 1) & (_lanes((SUBLANES, HEAD_BLK)) == V_DIM))
        lse_ref[0, 0] = lax.dot_general(pick.astype(F32), lse, (((1,), (1,)), ((), ())),
                                        precision=lax.Precision.HIGHEST, preferred_element_type=F32)

    return pl.pallas_call(
        body, name="flash_fwd", grid=(N_PAIR, nq),
        in_specs=[pl.BlockSpec((t, pw), lambda p, i: (i, p)), pl.BlockSpec((s, pw), lambda p, i: (0, p)),
                  pl.BlockSpec((s, HEAD_BLK), lambda p, i: (0, p)),
                  pl.BlockSpec((t, HEAD_BLK), lambda p, i: (i, 9 + p))],
        out_specs=[pl.BlockSpec((t, HEAD_BLK), lambda p, i: (i, p)), pl.BlockSpec((t, HEAD_BLK), lambda p, i: (i, p)),
                   pl.BlockSpec((1, 1, SUBLANES, t), lambda p, i: (p, i, 0, 0))],
        out_shape=[jax.ShapeDtypeStruct((s, D_MLA), F32), jax.ShapeDtypeStruct((s, D_MLA), MXU_DTYPE),
                   jax.ShapeDtypeStruct((N_PAIR, nq, SUBLANES, t), F32)],
        compiler_params=_cp(("parallel", "parallel"), 48),
    )(q, k, v, z)


def _flash_bwd(q, k, v, do, lse, delta):
    s = q.shape[0]
    t = _tile(s)
    nq = s // t
    pw = 2 * HEAD_BLK

    def body(q_ref, do_ref, lse_ref, dl_ref, k_ref, v_ref, dq_ref, dk_ref, dv_ref):
        j = pl.program_id(1)

        @pl.when(j == 0)
        def _():
            dq_ref[...] = jnp.zeros_like(dq_ref)

        kt = k_ref[...]
        ka, kb = kt[:, :HEAD_BLK], kt[:, HEAD_BLK:]
        vt = v_ref[...]

        def head(kh, qh, do_h, lse_row, dl_row, masked):
            st = _dot_nt(kh, qh)
            if masked:
                st = jnp.where(_rows((t, t)) <= _lanes((t, t)), st, -1e30)
            pt = jnp.exp2(st - lse_row)
            dv_h = _dot(pt, do_h)
            dst = (pt * (_dot_nt(vt, do_h) - dl_row)).astype(MXU_DTYPE)
            return dv_h, _dot(dst, qh), _dot_tn(dst, kh)

        def step(i, carry, masked):
            dka, dkb, dv = carry
            q_rows = pl.ds(pl.multiple_of(i * t, t), t)
            qv = q_ref[q_rows, :]
            dov = do_ref[q_rows, :]
            lane = _lanes(dov.shape)
            do_lo = jnp.where(lane < V_DIM, dov, jnp.zeros_like(dov))
            do_hi = jnp.where(lane >= V_DIM, dov, jnp.zeros_like(dov))
            dva, dk_a, dq_a = head(ka, qv[:, :HEAD_BLK], do_lo, lse_ref[0, i, 0:1, :], dl_ref[0, i, 0:1, :], masked)
            dvb, dk_b, dq_b = head(kb, qv[:, HEAD_BLK:], do_hi, lse_ref[0, i, 1:2, :], dl_ref[0, i, 1:2, :], masked)
            dq_ref[q_rows, 0:HEAD_BLK] += dq_a
            dq_ref[q_rows, HEAD_BLK:pw] += dq_b
            return dka + dk_a, dkb + dk_b, dv + dva + dvb

        zero = jnp.zeros((t, HEAD_BLK), F32)
        carry = step(j, (zero, zero, zero), True)
        dka, dkb, dv = _loop_in_trips(j + 1, nq, FLASH_TILES_PER_TRIP, lambda i, cr: step(i, cr, False), carry)
        dk_ref[:, 0:HEAD_BLK] = dka * LN_2
        dk_ref[:, HEAD_BLK:pw] = dkb * LN_2
        dv_ref[...] = dv.astype(dv_ref.dtype)

    return pl.pallas_call(
        body, name="flash_bwd", grid=(N_PAIR, nq),
        in_specs=[pl.BlockSpec((s, pw), lambda p, j: (0, p)), pl.BlockSpec((s, HEAD_BLK), lambda p, j: (0, p)),
                  pl.BlockSpec((1, nq, SUBLANES, t), lambda p, j: (p, 0, 0, 0)),
                  pl.BlockSpec((1, nq, SUBLANES, t), lambda p, j: (p, 0, 0, 0)),
                  pl.BlockSpec((t, pw), lambda p, j: (j, p)), pl.BlockSpec((t, HEAD_BLK), lambda p, j: (j, p))],
        out_specs=[pl.BlockSpec((s, pw), lambda p, j: (0, p)), pl.BlockSpec((t, pw), lambda p, j: (j, p)),
                   pl.BlockSpec((t, HEAD_BLK), lambda p, j: (j, p))],
        out_shape=[jax.ShapeDtypeStruct((s, MLA_HEADS * HEAD_BLK), F32),
                   jax.ShapeDtypeStruct((s, MLA_HEADS * HEAD_BLK), F32),
                   jax.ShapeDtypeStruct((s, D_MLA), MXU_DTYPE)],
        compiler_params=_cp(("parallel", "arbitrary"), 56),
    )(q, do, lse, delta, k, v)


def _mla_prep_bwd(dq, dk, dv, z, qn, kvn, gq, gkv, wuq, wukv, c_t, s1_t, s2_t, l):
    s = z.shape[0]
    t = _wide_tile(s)
    hq = MLA_HEADS * HEAD_BLK
    lay = _lay(l)

    def body(dq_ref, dk_ref, dv_ref, cq_ref, ckv_ref, qn_ref, kvn_ref, gq_ref, gkv_ref, wuq_ref, wukv_ref,
             c_ref, s1_ref, s2_ref, dcq_ref, dckv_ref, dkr_ref, dwuq_ref, dwukv_ref, dgq_ref, dgkv_ref,
             dqu_ref, dkvp_ref):
        @pl.when(pl.program_id(0) == 0)
        def _():
            for ref in (dwuq_ref, dwukv_ref, dgq_ref, dgkv_ref):
                ref[...] = jnp.zeros_like(ref)

        ct, s1, s2 = c_ref[...], s1_ref[...], s2_ref[...]
        dk_sum = jnp.zeros((t, HEAD_BLK), F32)
        for h in range(MLA_HEADS):
            blk = slice(h * HEAD_BLK, (h + 1) * HEAD_BLK)
            dqu_ref[:, blk] = _unrope(dq_ref[:, blk] * ATT_SCALE, ct, s1, s2).astype(dqu_ref.dtype)
            dkh = dk_ref[:, blk]
            dk_sum = dk_sum + dkh
            dkvp_ref[:, blk] = dkh.astype(dkvp_ref.dtype)
        dkvp_ref[:, hq:] = dv_ref[...]
        lane = _lanes((t, HEAD_BLK))
        rope_lanes = (lane >= KR_LANE0) & (lane < KR_LANE0 + QK_ROPE)
        dkr_ref[...] = _unrope(jnp.where(rope_lanes, dk_sum, 0.0), ct, s1, s2).astype(dkr_ref.dtype)

        def norm_bwd(c_in, g, dn_out, dc_ref, dg_ref):
            rs = lax.rsqrt(jnp.mean(c_in * c_in, axis=-1, keepdims=True) + EPS)
            n = c_in * rs
            dg_ref[...] += jnp.sum(dn_out * n, axis=0, keepdims=True)
            dn = dn_out * g
            dc_ref[...] = (rs * (dn - n * jnp.mean(dn * n, axis=-1, keepdims=True))).astype(dc_ref.dtype)

        dqu, dkvp = dqu_ref[...], dkvp_ref[...]
        dwuq_ref[...] += _dot_tn(qn_ref[...], dqu)
        dwukv_ref[...] += _dot_tn(kvn_ref[...], dkvp)
        norm_bwd(cq_ref[...], gq_ref[...], _dot_nt(dqu, wuq_ref[...]), dcq_ref, dgq_ref)
        norm_bwd(ckv_ref[...], gkv_ref[...], _dot_nt(dkvp, wukv_ref[...]), dckv_ref, dgkv_ref)

    full = lambda shp: pl.BlockSpec(shp, lambda i: (0, 0))
    tab = pl.BlockSpec((t, HEAD_BLK), lambda i: (i, 0))
    row = lambda w: pl.BlockSpec((t, w), lambda i: (i, 0))
    return pl.pallas_call(
        body, name="mla_prep_bwd", grid=(s // t,),
        in_specs=[row(hq), row(hq), row(D_MLA),
                  pl.BlockSpec((t, Q_RANK), lambda i: (i, 2)), pl.BlockSpec((t, KV_RANK), lambda i: (i, 6)),
                  row(Q_RANK), row(KV_RANK),
                  lay((1, Q_RANK)), lay((1, KV_RANK)), lay((Q_RANK, hq)), lay((KV_RANK, hq + D_MLA)),
                  tab, tab, tab],
        out_specs=[row(Q_RANK), row(KV_RANK), row(HEAD_BLK), full((Q_RANK, hq)), full((KV_RANK, hq + D_MLA)),
                   full((1, Q_RANK)), full((1, KV_RANK))],
        out_shape=[jax.ShapeDtypeStruct((s, Q_RANK), MXU_DTYPE), jax.ShapeDtypeStruct((s, KV_RANK), MXU_DTYPE),
                   jax.ShapeDtypeStruct((s, HEAD_BLK), MXU_DTYPE),
                   jax.ShapeDtypeStruct((Q_RANK, hq), F32), jax.ShapeDtypeStruct((KV_RANK, hq + D_MLA), F32),
                   jax.ShapeDtypeStruct((1, Q_RANK), F32), jax.ShapeDtypeStruct((1, KV_RANK), F32)],
        scratch_shapes=[pltpu.VMEM((t, hq), MXU_DTYPE), pltpu.VMEM((t, hq + D_MLA), MXU_DTYPE)],
        compiler_params=_cp(("arbitrary",), 56),
    )(dq, dk, dv, z, z, qn, kvn, gq, gkv, wuq, wukv, c_t, s1_t, s2_t)


def _outproj_fwd(x, ya, yb, yc, w, l):
    s = x.shape[0]
    t = _wide_tile(s)
    lay = _lay(l)

    def body(x_ref, ya_ref, yb_ref, yc_ref, wa_ref, wb_ref, wc_ref, o_ref):
        o_ref[...] = (x_ref[...] + _dot(ya_ref[...], wa_ref[...]) + _dot(yb_ref[...], wb_ref[...])
                      + _dot(yc_ref[...], wc_ref[...]))

    row = lambda w_: pl.BlockSpec((t, w_), lambda i: (i, 0))
    return pl.pallas_call(
        body, name="outproj_fwd", grid=(s // t,),
        in_specs=[row(D_MODEL), row(D_LRU), row(D_MLA), row(D_POOL),
                  lay((D_LRU, D_MODEL), 0), lay((D_MLA, D_MODEL), 1), lay((D_POOL, D_MODEL), 3)],
        out_specs=row(D_MODEL),
        out_shape=jax.ShapeDtypeStruct((s, D_MODEL), F32),
        compiler_params=_cp(("parallel",), 40),
    )(x, ya, yb, yc, w, w, w)


def _outproj_bwd(dx, ya, yb, yc, o, z, w, l):
    s = dx.shape[0]
    t = _tile(s)
    nq = s // t
    rows = N_PAIR * SUBLANES

    def body(dx_ref, ya_ref, yb_ref, yc_ref, o_ref, gb_ref, w_ref, dya_ref, dyc_ref, do_ref, dgb_ref, dl_ref, dw_ref):
        @pl.when(pl.program_id(0) == 0)
        def _():
            dw_ref[...] = jnp.zeros_like(dw_ref)

        dxv = dx_ref[...].astype(MXU_DTYPE)
        dy = _dot_nt(dxv, w_ref[...])
        dw_ref[0:D_LRU, :] += _dot_tn(ya_ref[...], dxv)
        dw_ref[D_LRU:D_LRU + D_MLA, :] += _dot_tn(yb_ref[...], dxv)
        dw_ref[D_LRU + D_MLA:D_MIX, :] += _dot_tn(yc_ref[...], dxv)
        dya_ref[...] = dy[:, :D_LRU]
        dyc_ref[...] = dy[:, D_LRU + D_MLA:]
        dyb = dy[:, D_LRU:D_LRU + D_MLA]
        gb = gb_ref[...]
        sg = _sig(gb)
        ov = o_ref[...]
        do = dyb * (gb * sg)
        do_ref[...] = do.astype(do_ref.dtype)
        dgb_ref[...] = (dyb * ov * (sg * (1.0 + gb * (1.0 - sg)))).astype(dgb_ref.dtype)
        r = _rows((rows, D_MLA))
        head = (r // SUBLANES) * 2 + (r % SUBLANES)
        sel = ((r % SUBLANES) < 2) & (_lanes((rows, D_MLA)) // V_DIM == head)
        dl = lax.dot_general(sel.astype(F32), do * ov, (((1,), (1,)), ((), ())),
                             precision=lax.Precision.HIGHEST, preferred_element_type=F32)
        for p in range(N_PAIR):
            dl_ref[p, 0] = dl[p * SUBLANES:(p + 1) * SUBLANES]

    row = lambda w_: pl.BlockSpec((t, w_), lambda i: (i, 0))
    return pl.pallas_call(
        body, name="outproj_bwd", grid=(nq,),
        in_specs=[row(D_MODEL), row(D_LRU), row(D_MLA), row(D_POOL), row(D_MLA),
                  pl.BlockSpec((t, D_MLA), lambda i: (i, 3)), _lay(l)((D_MIX, D_MODEL))],
        out_specs=[row(D_LRU), row(D_POOL), row(D_MLA), row(D_MLA),
                   pl.BlockSpec((N_PAIR, 1, SUBLANES, t), lambda i: (0, i, 0, 0)),
                   pl.BlockSpec((D_MIX, D_MODEL), lambda i: (0, 0))],
        out_shape=[jax.ShapeDtypeStruct((s, D_LRU), F32), jax.ShapeDtypeStruct((s, D_POOL), F32),
                   jax.ShapeDtypeStruct((s, D_MLA), MXU_DTYPE), jax.ShapeDtypeStruct((s, D_MLA), MXU_DTYPE),
                   jax.ShapeDtypeStruct((N_PAIR, nq, SUBLANES, t), F32),
                   jax.ShapeDtypeStruct((D_MIX, D_MODEL), F32)],
        compiler_params=_cp(("arbitrary",), 48),
    )(dx, ya, yb, yc, o, z, w)


DZ_WIDTHS = (D_LRU, D_LRU, Q_RANK, D_MLA, KV_RANK, D_POOL, D_POOL, HEAD_BLK)


def _dwin(h, dz_parts):
    s = h.shape[0]
    t = _tile(s)

    def body(h_ref, *refs):
        o_ref = refs[-1]

        @pl.when(pl.program_id(0) == 0)
        def _():
            o_ref[...] = jnp.zeros_like(o_ref)

        o_ref[...] += _dot_tn(h_ref[...], jnp.concatenate([r[...] for r in refs[:-1]], axis=1))

    row = lambda w_: pl.BlockSpec((t, w_), lambda i: (i, 0))
    return pl.pallas_call(
        body, name="dwin", grid=(s // t,),
        in_specs=[row(D_MODEL)] + [row(wd) for wd in DZ_WIDTHS],
        out_specs=pl.BlockSpec((D_MODEL, D_INP), lambda i: (0, 0)),
        out_shape=jax.ShapeDtypeStruct((D_MODEL, D_INP), F32),
        compiler_params=_cp(("arbitrary",), 56),
    )(h, *dz_parts)


def _inproj_bwd(dz_parts, w, x, g, dxn, l):
    s = x.shape[0]
    t = _tile(s)
    lay = _lay(l)
    n_parts = len(DZ_WIDTHS)

    def body(*refs):
        part_refs = refs[:n_parts]
        w_ref, x_ref, g_ref, dxn_ref, dx_ref, dg_ref = refs[n_parts:]

        @pl.when(pl.program_id(0) == 0)
        def _():
            dg_ref[...] = jnp.zeros_like(dg_ref)

        dh = _dot_nt(jnp.concatenate([r[...] for r in part_refs], axis=1), w_ref[...])
        xv = x_ref[...]
        rs = lax.rsqrt(jnp.mean(xv * xv, axis=-1, keepdims=True) + EPS)
        n = xv * rs
        dg_ref[...] += jnp.sum(dh * n, axis=0, keepdims=True)
        dn = dh * g_ref[...]
        dx_ref[...] = dxn_ref[...] + rs * (dn - n * jnp.mean(dn * n, axis=-1, keepdims=True))

    row = lambda w_: pl.BlockSpec((t, w_), lambda i: (i, 0))
    return pl.pallas_call(
        body, name="inproj_bwd", grid=(s // t,),
        in_specs=[row(wd) for wd in DZ_WIDTHS] + [lay((D_MODEL, D_INP)), row(D_MODEL), lay((1, D_MODEL)),
                                                  row(D_MODEL)],
        out_specs=[row(D_MODEL), pl.BlockSpec((1, D_MODEL), lambda i: (0, 0))],
        out_shape=[jax.ShapeDtypeStruct((s, D_MODEL), F32), jax.ShapeDtypeStruct((1, D_MODEL), F32)],
        compiler_params=_cp(("arbitrary",), 48),
    )(*dz_parts, w, x, g, dxn)


def _loss_head(x, g, tgt):
    s = x.shape[0]
    t = _tile(s)

    def body(x_ref, g_ref, t_ref, dx_ref, loss_ref, dg_ref):
        @pl.when(pl.program_id(0) == 0)
        def _():
            loss_ref[...] = jnp.zeros_like(loss_ref)
            dg_ref[...] = jnp.zeros_like(dg_ref)

        xv = x_ref[...]
        rs = lax.rsqrt(jnp.mean(xv * xv, axis=-1, keepdims=True) + EPS)
        n = xv * rs
        gv = g_ref[...]
        e = n * gv - t_ref[...]
        loss_ref[...] += 0.5 * jnp.sum(jnp.mean(e * e, axis=-1, keepdims=True))
        dyf = e * (1.0 / D_MODEL)
        dg_ref[...] += jnp.sum(dyf * n, axis=0, keepdims=True)
        dn = dyf * gv
        dx_ref[...] = rs * (dn - n * jnp.mean(dn * n, axis=-1, keepdims=True))

    row = pl.BlockSpec((t, D_MODEL), lambda i: (i, 0))
    vec = pl.BlockSpec((1, D_MODEL), lambda i: (0, 0))
    return pl.pallas_call(
        body, name="loss_head", grid=(s // t,),
        in_specs=[row, vec, row],
        out_specs=[row, pl.BlockSpec((1, LANES), lambda i: (0, 0)), vec],
        out_shape=[jax.ShapeDtypeStruct((s, D_MODEL), F32), jax.ShapeDtypeStruct((1, LANES), F32),
                   jax.ShapeDtypeStruct((1, D_MODEL), F32)],
        compiler_params=_cp(("arbitrary",), 32),
    )(x, g, tgt)


def _block_diag(w):
    n, h, d, _ = w.shape
    return jnp.einsum('lhij,hk->lhikj', w, jnp.eye(h, dtype=w.dtype)).reshape(n, h * d, h * d)


def _diag_blocks(wfull, h):
    d = wfull.shape[-1] // h
    return jnp.stack([wfull[:, i * d:(i + 1) * d, i * d:(i + 1) * d] for i in range(h)], axis=1)


REF_TO_PERM = np.concatenate([np.arange(0, 1152), np.arange(1536, 1792),
                              np.arange(2304 + KR_LANE0, 2304 + KR_LANE0 + QK_ROPE),
                              np.arange(1152, 1536), np.arange(1792, 2304)])
N_SHARD = 4
W_IN_SHARD = D_IN // N_SHARD


def _w_in_runs():
    ref_of_perm = -np.ones(D_INP, np.int64)
    ref_of_perm[REF_TO_PERM] = np.arange(D_IN)
    perm_runs, p = [], 0
    while p < D_INP:
        r, q = ref_of_perm[p], p + 1
        if r < 0:
            while q < D_INP and ref_of_perm[q] < 0:
                q += 1
            perm_runs.append((None, q - p, 0))
        else:
            while (q < D_INP and ref_of_perm[q] == ref_of_perm[q - 1] + 1
                   and ref_of_perm[q] // W_IN_SHARD == r // W_IN_SHARD):
                q += 1
            perm_runs.append((int(r // W_IN_SHARD), int(r % W_IN_SHARD), int(r % W_IN_SHARD + q - p)))
        p = q
    shard_runs = []
    for s in range(N_SHARD):
        cols = REF_TO_PERM[s * W_IN_SHARD:(s + 1) * W_IN_SHARD]
        runs, a = [], 0
        for b in range(1, W_IN_SHARD + 1):
            if b == W_IN_SHARD or cols[b] != cols[b - 1] + 1:
                runs.append((int(cols[a]), int(cols[b - 1]) + 1))
                a = b
        shard_runs.append(runs)
    return perm_runs, shard_runs


def _permute_w_in(shards):
    perm_runs, _ = _w_in_runs()
    lead = shards[0].shape[:-1]
    parts = [jnp.zeros(lead + (a,), shards[0].dtype) if s is None else shards[s][..., a:b] for s, a, b in perm_runs]
    return jnp.concatenate(parts, axis=-1)


def _w_in_shard(wp, s):
    _, shard_runs = _w_in_runs()
    return jnp.concatenate([wp[..., a:b] for a, b in shard_runs[s]], axis=-1)


def _pad_w_uq(w):
    w4 = w.reshape(w.shape[:2] + (MLA_HEADS, QK_NOPE + QK_ROPE))
    return jnp.pad(w4, ((0, 0),) * 3 + ((0, HEAD_BLK - QK_NOPE - QK_ROPE),)).reshape(w.shape[:2] + (-1,))


def _unpad_w_uq(w):
    return w.reshape(w.shape[:2] + (MLA_HEADS, HEAD_BLK))[..., :QK_NOPE + QK_ROPE].reshape(w.shape[:2] + (-1,))


def _pad_w_ukv(w):
    w4 = w.reshape(w.shape[:2] + (MLA_HEADS, QK_NOPE + V_DIM))
    kpart = jnp.pad(w4[..., :QK_NOPE], ((0, 0),) * 3 + ((0, HEAD_BLK - QK_NOPE),)).reshape(w.shape[:2] + (-1,))
    return jnp.concatenate([kpart, w4[..., QK_NOPE:].reshape(w.shape[:2] + (-1,))], axis=2)


def _unpad_w_ukv(w):
    hq = MLA_HEADS * HEAD_BLK
    kpart = w[..., :hq].reshape(w.shape[:2] + (MLA_HEADS, HEAD_BLK))[..., :QK_NOPE]
    vpart = w[..., hq:].reshape(w.shape[:2] + (MLA_HEADS, V_DIM))
    return jnp.concatenate([kpart, vpart], axis=3).reshape(w.shape[:2] + (-1,))


def _local_step(x, tgt, w):
    s = x.shape[0]
    tabs = _rope_tables(s)
    vec = lambda a: a[:, None, :]
    mxu = lambda a: a.astype(MXU_DTYPE)
    p = dict(g=vec(w['norm_g']), w_in=mxu(w['w_in']), cw=w['conv_w'], cb=vec(w['conv_b']),
             wr=mxu(_block_diag(w['w_rg'])), br=vec(w['b_rg']), wi=mxu(_block_diag(w['w_ig'])), bi=vec(w['b_ig']),
             lam=vec(w['lru_lambda']), gq=vec(w['q_norm_g']), gkv=vec(w['kv_norm_g']),
             wuq=mxu(_pad_w_uq(w['w_uq'])), wukv=mxu(_pad_w_ukv(w['w_ukv'])),
             wp=mxu(_block_diag(w['w_pool'])), ps=vec(w['pool_scale']), wout=mxu(w['w_out']))
    lru = lambda l: (p['cw'], p['cb'], p['wr'], p['br'], p['wi'], p['bi'], p['lam'], l)
    mla = lambda l: (p['gq'], p['gkv'], p['wuq'], p['wukv'], *tabs, l)

    saved = []
    for l in range(DEPTH):
        h, z = _inproj_fwd(x, p['g'], p['w_in'], l)
        ya, hs = _lru_fwd(z, *lru(l))
        yc = _pool_fwd(z, p['wp'], p['ps'], l)
        q, k, v, qn, kvn = _mla_prep_fwd(z, *mla(l))
        o, yb, lse = _flash_fwd(q, k, v, z)
        saved.append(dict(x=x, h=h, z=z, hs=hs, ya=ya, yb=yb, yc=yc, q=q, k=k, v=v, qn=qn, kvn=kvn, o=o, lse=lse))
        x = _outproj_fwd(x, ya, yb, yc, p['wout'], l)

    dx, loss, dgf = _loss_head(x, w['final_norm_g'][None], tgt)
    per_layer = {n: [None] * DEPTH for n in WEIGHT_NAMES if n != 'final_norm_g'}
    for l in reversed(range(DEPTH)):
        sv = saved[l]
        dya, dyc, do, dgb, delta, per_layer['w_out'][l] = _outproj_bwd(
            dx, sv['ya'], sv['yb'], sv['yc'], sv['o'], sv['z'], p['wout'], l)
        dza, dga, *lru_grads = _lru_bwd(sv['z'], sv['hs'], dya, *lru(l))
        for n, g in zip(('w_rg', 'w_ig', 'conv_w', 'conv_b', 'b_rg', 'b_ig', 'lru_lambda'), lru_grads):
            per_layer[n][l] = g
        dzc, dgc, per_layer['w_pool'][l], per_layer['pool_scale'][l] = _pool_bwd(sv['z'], dyc, p['wp'], p['ps'], l)
        dq, dk, dv = _flash_bwd(sv['q'], sv['k'], sv['v'], do, sv['lse'], delta)
        (dcq, dckv, dkr, per_layer['w_uq'][l], per_layer['w_ukv'][l], per_layer['q_norm_g'][l],
         per_layer['kv_norm_g'][l]) = _mla_prep_bwd(dq, dk, dv, sv['z'], sv['qn'], sv['kvn'], *mla(l))
        dz_parts = (dza, dga, dcq, dgb, dckv, dzc, dgc, dkr)
        per_layer['w_in'][l] = _dwin(sv['h'], dz_parts)
        dx, per_layer['norm_g'][l] = _inproj_bwd(dz_parts, p['w_in'], sv['x'], p['g'], dx, l)
    grads = {n: jnp.stack(g) for n, g in per_layer.items()}
    for n in ('norm_g', 'conv_b', 'b_rg', 'b_ig', 'lru_lambda', 'q_norm_g', 'kv_norm_g', 'pool_scale'):
        grads[n] = grads[n][:, 0, :]
    grads['w_rg'] = _diag_blocks(grads['w_rg'], LRU_HEADS)
    grads['w_ig'] = _diag_blocks(grads['w_ig'], LRU_HEADS)
    grads['w_pool'] = _diag_blocks(grads['w_pool'], len(POOL_WINDOWS))
    grads['w_uq'] = _unpad_w_uq(grads['w_uq'])
    grads['w_ukv'] = _unpad_w_ukv(grads['w_ukv'])
    grads['final_norm_g'] = dgf[0]
    return loss[0, 0], dx, grads


WIRE_DTYPE = jnp.bfloat16
MESH_IDS = pl.DeviceIdType.MESH
_HBM = pl.BlockSpec(memory_space=pltpu.HBM)


def _coords():
    return lax.axis_index("x"), lax.axis_index("y"), lax.axis_index("c")


def _comm_call(body, name, arrays, out_shapes, copies_per_array):
    n = len(arrays)
    return pl.pallas_call(
        body, name=name, out_shape=out_shapes, in_specs=[_HBM] * n, out_specs=[_HBM] * n,
        scratch_shapes=[pltpu.SemaphoreType.DMA((n, copies_per_array)), pltpu.SemaphoreType.DMA((n, copies_per_array))],
    )(*arrays)


def _all_gather8(blocks, name):
    n = len(blocks)
    every = range(n)

    def body(*refs):
        x_refs, out_refs = refs[:n], refs[n:2 * n]
        send_sems, recv_sems = refs[2 * n:]
        x, y, c = _coords()
        me, sibling = (x, y, c), (x, y, 1 - c)
        chips = [(1 - x, y), (x, 1 - y), (1 - x, 1 - y)]

        def slot(t, px, py, pc):
            return out_refs[t].at[4 * px + 2 * py + pc]

        def copy(t, k, block, to, own=False):
            return pltpu.make_async_remote_copy(
                src_ref=x_refs[t] if own else slot(t, *block), dst_ref=slot(t, *block),
                send_sem=send_sems.at[t, k], recv_sem=recv_sems.at[t, k], device_id=to, device_id_type=MESH_IDS)

        first = [copy(t, 0, me, sibling, own=True) for t in every]
        first += [copy(t, 1 + j, me, (*chip, c), own=True) for j, chip in enumerate(chips) for t in every]
        for cp in first:
            cp.start()
        passed = [[copy(t, 4 + j, (*chip, c), sibling) for t in every] for j, chip in enumerate(chips)]
        for j, chip in enumerate(chips):
            for t in every:
                copy(t, 1 + j, (*chip, c), me).wait_recv()
                passed[j][t].start()
        for t in every:
            copy(t, 0, sibling, me).wait_recv()
        for j, chip in enumerate(chips):
            for t in every:
                copy(t, 4 + j, (*chip, 1 - c), me).wait_recv()
        for cp in first + [cp for group in passed for cp in group]:
            cp.wait_send()

    outs = [jax.ShapeDtypeStruct((N_DEV,) + b.shape, b.dtype) for b in blocks]
    got = _comm_call(body, name, blocks, outs, 7)
    me = 4 * lax.axis_index("x") + 2 * lax.axis_index("y") + lax.axis_index("c")
    return [lax.dynamic_update_index_in_dim(g, b, me, 0) for g, b in zip(got, blocks)]


def _sibling_send(arrs, name, which_half=False):
    n = len(arrs)

    def body(*refs):
        a_refs, out_refs = refs[:n], refs[n:2 * n]
        send_sems, recv_sems = refs[2 * n:]
        x, y, c = _coords()
        sent = [pltpu.make_async_remote_copy(
            src_ref=a_refs[t].at[1 - c] if which_half else a_refs[t], dst_ref=out_refs[t],
            send_sem=send_sems.at[t, 0], recv_sem=recv_sems.at[t, 0],
            device_id=(x, y, 1 - c), device_id_type=MESH_IDS) for t in range(n)]
        for cp in sent:
            cp.start()
        for cp in sent:
            cp.wait()

    shapes = [jax.ShapeDtypeStruct(a.shape[1:] if which_half else a.shape, a.dtype) for a in arrs]
    return _comm_call(body, name, arrs, shapes, 1)


def _chip_exchange(arrs, name):
    n = len(arrs)

    def body(*refs):
        a_refs, out_refs = refs[:n], refs[n:2 * n]
        send_sems, recv_sems = refs[2 * n:]
        x, y, c = _coords()
        copies = []
        for j, (cx, cy) in enumerate([(1 - x, y), (x, 1 - y), (1 - x, 1 - y)]):
            copies += [pltpu.make_async_remote_copy(
                src_ref=a_refs[t].at[2 * cx + cy], dst_ref=out_refs[t].at[j], send_sem=send_sems.at[t, j],
                recv_sem=recv_sems.at[t, j], device_id=(cx, cy, c), device_id_type=MESH_IDS) for t in range(n)]
        for cp in copies:
            cp.start()
        for cp in copies:
            cp.wait()

    return _comm_call(body, name, arrs, [jax.ShapeDtypeStruct((3,) + a.shape[1:], a.dtype) for a in arrs], 3)


def _sum_leading(groups, out_dtype, steps, name):
    flat = [a for g in groups for a in g]

    def body(*refs):
        ins, outs, pos = refs[:len(flat)], refs[len(flat):], 0
        for g, o_ref in zip(groups, outs):
            acc = None
            for i_ref in ins[pos:pos + len(g)]:
                for k in range(i_ref.shape[0]):
                    term = i_ref[k].astype(F32)
                    acc = term if acc is None else acc + term
            pos += len(g)
            o_ref[...] = acc.astype(o_ref.dtype)

    return pl.pallas_call(
        body, name=name, grid=(steps,),
        in_specs=[pl.BlockSpec((a.shape[0], a.shape[1] // steps, a.shape[2]), lambda i: (0, i, 0)) for a in flat],
        out_specs=[pl.BlockSpec((g[0].shape[1] // steps, g[0].shape[2]), lambda i: (i, 0)) for g in groups],
        out_shape=[jax.ShapeDtypeStruct(g[0].shape[1:], out_dtype) for g in groups],
        compiler_params=_cp(("parallel",), 40),
    )(*flat)


def _adamw_update(w_ref, g_ref, m_ref, v_ref, d_ref, mo_ref, vo_ref):
    gv = g_ref[...]
    mn = ADAM_B1 * m_ref[...] + (1.0 - ADAM_B1) * gv
    vn = ADAM_B2 * v_ref[...] + (1.0 - ADAM_B2) * (gv * gv)
    mo_ref[...] = mn
    vo_ref[...] = vn
    m_hat = mn / (1.0 - ADAM_B1 ** ADAM_STEP)
    v_hat = vn / (1.0 - ADAM_B2 ** ADAM_STEP)
    d_ref[...] = (-ADAM_LR) * (m_hat / (jnp.sqrt(v_hat) + ADAM_EPS) + ADAM_WD * w_ref[...])


def _adamw(w, g, m, v, name):
    n, r, cdim = w.shape
    tr = math.gcd(r, 512)

    def body(*refs):
        _adamw_update(*refs)

    blk = pl.BlockSpec((None, tr, cdim), lambda l, i: (l, i, 0))
    return pl.pallas_call(
        body, name=name, grid=(n, r // tr),
        in_specs=[blk] * 4, out_specs=[blk] * 3,
        out_shape=[jax.ShapeDtypeStruct(w.shape, F32)] * 3,
        compiler_params=_cp(("parallel", "parallel"), 40),
    )(w, g, m, v)


def _adamw_small(ws, gs, ms, vs, name):
    n = len(ws)

    def body(*refs):
        ins, outs = refs[:4 * n], refs[4 * n:]
        for t in range(n):
            _adamw_update(ins[t], ins[n + t], ins[2 * n + t], ins[3 * n + t], outs[t], outs[n + t], outs[2 * n + t])

    shapes = [jax.ShapeDtypeStruct(w.shape, F32) for w in ws]
    outs = pl.pallas_call(body, name=name, out_shape=shapes * 3)(*ws, *gs, *ms, *vs)
    return outs[:n], outs[n:2 * n], outs[2 * n:]


HALF = DEPTH // 2
BIG = ['w_in', 'w_uq', 'w_ukv', 'w_out']
SHARD_AXIS = {'w_in': 2, 'conv_w': 2, 'w_uq': 2, 'w_ukv': 2, 'w_out': 1}
FULL_SHAPE = {'w_in': (DEPTH, D_MODEL, D_IN), 'conv_w': (DEPTH, CONV_WIDTH, D_LRU),
              'w_uq': (DEPTH, Q_RANK, MLA_HEADS * (QK_NOPE + QK_ROPE)),
              'w_ukv': (DEPTH, KV_RANK, MLA_HEADS * (QK_NOPE + V_DIM)), 'w_out': (DEPTH, D_MIX, D_MODEL)}


def _shard_shape(n):
    shp = list(FULL_SHAPE[n])
    shp[SHARD_AXIS[n]] //= N_SHARD
    return tuple(shp)


def _rows_view(a, lead=0):
    return a.reshape(a.shape[:lead] + (-1, a.shape[-1]))


def _gather_weights(local):
    c = lax.axis_index("c")
    names = BIG + ['conv_w']
    halves = [lax.dynamic_slice_in_dim(local[n], HALF * c, HALF, axis=0) for n in names]
    halves = [h.astype(WIRE_DTYPE) if n in BIG else h for n, h in zip(names, halves)]
    got = _all_gather8(halves, "gather_weights")
    full = {}
    for n, g in zip(names, got):
        g = g.reshape((N_SHARD, DEPTH) + g.shape[2:])
        if n == 'w_in':
            full[n] = _permute_w_in([g[s] for s in range(N_SHARD)])
        else:
            full[n] = jnp.moveaxis(g, 0, SHARD_AXIS[n]).reshape(FULL_SHAPE[n])
    return full


def _shard_blocks(g, n):
    width = _shard_shape(n)[SHARD_AXIS[n]]

    def block(h, s):
        part = g[HALF * h:HALF * (h + 1)]
        if n == 'w_in':
            part = _w_in_shard(part, s)
        else:
            part = lax.slice_in_dim(part, s * width, (s + 1) * width, axis=SHARD_AXIS[n])
        return _rows_view(part)

    return jnp.stack([jnp.stack([block(h, s) for s in range(N_SHARD)]) for h in range(2)]).astype(WIRE_DTYPE)


SUM_STEPS = 8


def _reduce_big(grads):
    c = lax.axis_index("c")
    shard = 2 * lax.axis_index("x") + lax.axis_index("y")
    contrib = [_shard_blocks(grads[n], n) for n in BIG]
    from_sibling = _sibling_send(contrib, "pair_exchange_big", which_half=True)
    own_half = [lax.dynamic_index_in_dim(a, c, 0, keepdims=False) for a in contrib]
    pair_sum = _sum_leading([[_rows_view(o)[None], _rows_view(r)[None]] for o, r in zip(own_half, from_sibling)],
                            WIRE_DTYPE, SUM_STEPS, "pair_sum_big")
    to_chips = [p.reshape(a.shape[1:]) for p, a in zip(pair_sum, contrib)]
    from_chips = _chip_exchange(to_chips, "chip_exchange_big")
    own_block = [lax.dynamic_index_in_dim(a, shard, 0, keepdims=True) for a in to_chips]
    mine = _sum_leading([[o, r] for o, r in zip(own_block, from_chips)], F32, SUM_STEPS, "chip_sum_big")
    theirs = _sibling_send(mine, "sibling_big")
    both = [jnp.where(c == 0, jnp.stack([m, t]), jnp.stack([t, m])) for m, t in zip(mine, theirs)]
    return {n: b.reshape(_shard_shape(n)) for n, b in zip(BIG, both)}


SMALL = REPLICATED + ['conv_w']


def _reduce_small(grads):
    views = [grads[n].reshape(-1, LANES) if grads[n].shape[-1] < LANES else _rows_view(jnp.atleast_2d(grads[n]))
             for n in SMALL]
    sums = _sum_leading([[g] for g in _all_gather8(views, "gather_small")], F32, 1, "sum_small")
    return {n: s.reshape(grads[n].shape) for n, s in zip(SMALL, sums)}


def kernel(x, norm_g, w_in, conv_w, conv_b, w_rg, b_rg, w_ig, b_ig, lru_lambda, q_norm_g, w_uq, kv_norm_g, w_ukv, w_pool, pool_scale, w_out, final_norm_g, loss_target, m_norm_g, m_w_in, m_conv_w, m_conv_b, m_w_rg, m_b_rg, m_w_ig, m_b_ig, m_lru_lambda, m_q_norm_g, m_w_uq, m_kv_norm_g, m_w_ukv, m_w_pool, m_pool_scale, m_w_out, m_final_norm_g, v_norm_g, v_w_in, v_conv_w, v_conv_b, v_w_rg, v_b_rg, v_w_ig, v_b_ig, v_lru_lambda, v_q_norm_g, v_w_uq, v_kv_norm_g, v_w_ukv, v_w_pool, v_pool_scale, v_w_out, v_final_norm_g):
    w_loc = dict(zip(WEIGHT_NAMES, (norm_g, w_in, conv_w, conv_b, w_rg, b_rg, w_ig, b_ig, lru_lambda, q_norm_g, w_uq,
                                    kv_norm_g, w_ukv, w_pool, pool_scale, w_out, final_norm_g)))
    m_loc = dict(zip(WEIGHT_NAMES, (m_norm_g, m_w_in, m_conv_w, m_conv_b, m_w_rg, m_b_rg, m_w_ig, m_b_ig, m_lru_lambda,
                                    m_q_norm_g, m_w_uq, m_kv_norm_g, m_w_ukv, m_w_pool, m_pool_scale, m_w_out,
                                    m_final_norm_g)))
    v_loc = dict(zip(WEIGHT_NAMES, (v_norm_g, v_w_in, v_conv_w, v_conv_b, v_w_rg, v_b_rg, v_w_ig, v_b_ig, v_lru_lambda,
                                    v_q_norm_g, v_w_uq, v_kv_norm_g, v_w_ukv, v_w_pool, v_pool_scale, v_w_out,
                                    v_final_norm_g)))
    w_full = dict(w_loc)
    w_full.update(_gather_weights(w_loc))
    loss_local, dx, g_local = _local_step(x[0], loss_target[0], w_full)
    loss = lax.psum(loss_local, ("x", "y", "c"))

    grads = _reduce_big(g_local)
    g_small = _reduce_small(g_local)
    shard = 2 * lax.axis_index("x") + lax.axis_index("y")
    width = D_LRU // N_SHARD
    grads['conv_w'] = lax.dynamic_slice_in_dim(g_small['conv_w'], shard * width, width, axis=2)
    for n in REPLICATED:
        grads[n] = g_small[n]

    delta, new_m, new_v = {}, {}, {}
    for n in BIG:
        delta[n], new_m[n], new_v[n] = _adamw(w_loc[n], grads[n], m_loc[n], v_loc[n], "adamw_" + n)
    small = [[_rows_view(jnp.atleast_2d(t[n])) for n in SMALL] for t in (w_loc, grads, m_loc, v_loc)]
    for tree, outs in zip((delta, new_m, new_v), _adamw_small(*small, "adamw_small")):
        tree.update({n: a.reshape(w_loc[n].shape) for n, a in zip(SMALL, outs)})

    return (loss, dx[None], *[grads[n] for n in WEIGHT_NAMES], *[delta[n] for n in WEIGHT_NAMES],
            *[new_m[n] for n in WEIGHT_NAMES], *[new_v[n] for n in WEIGHT_NAMES])
```

```python
import math

import jax
import jax.numpy as jnp
import numpy as np
from jax import lax
from jax.experimental import pallas as pl
from jax.experimental.pallas import tpu as pltpu

F32 = jnp.float32
MXU_DTYPE = jnp.bfloat16

D_MODEL = 1024
DEPTH = 4
EPS = 1e-6
D_LRU = 384
LRU_HEADS = 6
CONV_WIDTH = 4
LRU_C = 8.0
MLA_HEADS = 6
QK_NOPE = 64
QK_ROPE = 32
V_DIM = 64
D_MLA = MLA_HEADS * V_DIM
Q_RANK = 384
KV_RANK = 256
ROPE_BASE = 10000.0
POOL_WINDOWS = (2, 4, 8, 16)
D_POOL = 256
D_MIX = D_LRU + D_MLA + D_POOL
D_IN = 2336
ATT_SCALE = (QK_NOPE + QK_ROPE) ** -0.5
LOG2_E = 1.4426950408889634
LN_2 = 0.6931471805599453
Q_PRESCALE = ATT_SCALE * LOG2_E

ADAM_LR = 0.001
ADAM_B1 = 0.9
ADAM_B2 = 0.999
ADAM_EPS = 1e-08
ADAM_WD = 0.01
ADAM_STEP = 10

LANES = 128
SUBLANES = 8
N_DEV = 8

D_INP = 2432
KR_LANE0 = 64
HEAD_BLK = 128
N_PAIR = MLA_HEADS // 2

WEIGHT_NAMES = ['norm_g', 'w_in', 'conv_w', 'conv_b', 'w_rg', 'b_rg', 'w_ig', 'b_ig', 'lru_lambda', 'q_norm_g',
                'w_uq', 'kv_norm_g', 'w_ukv', 'w_pool', 'pool_scale', 'w_out', 'final_norm_g']
SHARDED = ['w_in', 'conv_w', 'w_uq', 'w_ukv', 'w_out']
REPLICATED = [n for n in WEIGHT_NAMES if n not in SHARDED]


def _cp(sem, vmem_mb=None):
    return pltpu.CompilerParams(dimension_semantics=sem,
                                vmem_limit_bytes=None if vmem_mb is None else vmem_mb << 20)


def _dot(a, b):
    return jnp.dot(a.astype(MXU_DTYPE), b.astype(MXU_DTYPE), preferred_element_type=F32)


def _dot_nt(a, b):
    return lax.dot_general(a.astype(MXU_DTYPE), b.astype(MXU_DTYPE), (((1,), (1,)), ((), ())),
                           preferred_element_type=F32)


def _dot_tn(a, b):
    return lax.dot_general(a.astype(MXU_DTYPE), b.astype(MXU_DTYPE), (((0,), (0,)), ((), ())),
                           preferred_element_type=F32)


def _sig(x):
    return 0.5 * jnp.tanh(0.5 * x) + 0.5


def _down(x, k):
    return pltpu.roll(x, k, 0)


def _up(x, k):
    return pltpu.roll(x, x.shape[0] - k, 0)


def _rows(shape):
    return lax.broadcasted_iota(jnp.int32, shape, 0)


def _lanes(shape):
    return lax.broadcasted_iota(jnp.int32, shape, 1)


def _tile(s):
    return min(512, s)


def _wide_tile(s):
    return min(2 * _tile(s), s)


def _lay(l):
    return lambda shp, blk=0: pl.BlockSpec((None,) + shp, lambda *_: (l, blk, 0))


def _inproj_fwd(x, g, w, l):
    s = x.shape[0]
    t = _tile(s)
    lay = _lay(l)

    def body(x_ref, g_ref, w_ref, h_ref, z_ref):
        xv = x_ref[...]
        rs = lax.rsqrt(jnp.mean(xv * xv, axis=-1, keepdims=True) + EPS)
        h = (xv * rs * g_ref[...]).astype(MXU_DTYPE)
        h_ref[...] = h
        z_ref[...] = jnp.dot(h, w_ref[...], preferred_element_type=F32)

    return pl.pallas_call(
        body, name="inproj_fwd", grid=(s // t,),
        in_specs=[pl.BlockSpec((t, D_MODEL), lambda i: (i, 0)),
                  lay((1, D_MODEL)), lay((D_MODEL, D_INP))],
        out_specs=[pl.BlockSpec((t, D_MODEL), lambda i: (i, 0)),
                   pl.BlockSpec((t, D_INP), lambda i: (i, 0))],
        out_shape=[jax.ShapeDtypeStruct((s, D_MODEL), MXU_DTYPE), jax.ShapeDtypeStruct((s, D_INP), F32)],
        compiler_params=_cp(("parallel",), 40),
    )(x, g, w)


def _lru_gates(za, halo, cw_ref, cb_ref, wr_ref, br_ref, wi_ref, bi_ref, lam_ref):
    t = za.shape[0]
    ext = jnp.concatenate([halo, za], axis=0)
    sh = [za] + [_down(ext, j)[SUBLANES:SUBLANES + t] for j in (1, 2, 3)]
    xa = cb_ref[...] + cw_ref[3:4, :] * sh[0] + cw_ref[2:3, :] * sh[1] + cw_ref[1:2, :] * sh[2] + cw_ref[0:1, :] * sh[3]
    r = 1.0 / (1.0 + jnp.exp(-(_dot(xa, wr_ref[...]) + br_ref[...])))
    ig = _sig(_dot(xa, wi_ref[...]) + bi_ref[...])
    lam = lam_ref[...]
    sp = jnp.maximum(-lam, 0.0) + jnp.log(1.0 + jnp.exp(-jnp.abs(lam)))
    la = (-LRU_C) * r * sp
    a = jnp.exp(la)
    y2 = 2.0 * la
    m2 = jnp.where(y2 > -0.01, -(y2 * (1.0 + y2 * (0.5 + y2 * (1.0 / 6.0)))), 1.0 - a * a)
    return xa, sh, r, ig, sp, a, jnp.sqrt(m2), m2


def _lru_fwd(z, cw, cb, wr, br, wi, bi, lam, l):
    s = z.shape[0]
    t = _tile(s)
    c = D_LRU
    lay = _lay(l)

    def body(za_ref, ga_ref, cw_ref, cb_ref, wr_ref, br_ref, wi_ref, bi_ref, lam_ref, ya_ref, hs_ref, zprev, hcar):
        i = pl.program_id(0)

        @pl.when(i == 0)
        def _():
            zprev[...] = jnp.zeros_like(zprev)
            hcar[...] = jnp.zeros_like(hcar)

        za = za_ref[...]
        xa, _, _, ig, _, a, m, _ = _lru_gates(za, zprev[...], cw_ref, cb_ref, wr_ref, br_ref, wi_ref, bi_ref, lam_ref)
        row = _rows((t, c))
        acc_h = m * (ig * xa) + jnp.where(row == 0, a * hcar[...], 0.0)
        acc_a = jnp.where(row == 0, 0.0, a)
        k = 1
        while k < t:
            acc_h = acc_h + acc_a * _down(acc_h, k)
            acc_a = acc_a * _down(acc_a, k)
            k *= 2
        hs = acc_h
        hs_ref[...] = hs
        ga = ga_ref[...]
        ya_ref[...] = (hs * (ga * _sig(ga))).astype(ya_ref.dtype)
        hcar[...] = jnp.sum(jnp.where(row == t - 1, hs, 0.0), axis=0, keepdims=True)
        zprev[...] = za_ref[t - SUBLANES:t, :]

    return pl.pallas_call(
        body, name="lru_fwd", grid=(s // t,),
        in_specs=[pl.BlockSpec((t, c), lambda i: (i, 0)), pl.BlockSpec((t, c), lambda i: (i, 1)),
                  lay((CONV_WIDTH, c)), lay((1, c)), lay((c, c)), lay((1, c)), lay((c, c)), lay((1, c)),
                  lay((1, c))],
        out_specs=[pl.BlockSpec((t, c), lambda i: (i, 0)), pl.BlockSpec((t, c), lambda i: (i, 0))],
        out_shape=[jax.ShapeDtypeStruct((s, c), MXU_DTYPE), jax.ShapeDtypeStruct((s, c), F32)],
        scratch_shapes=[pltpu.VMEM((SUBLANES, c), F32), pltpu.VMEM((1, c), F32)],
        compiler_params=_cp(("arbitrary",), 40),
    )(z, z, cw, cb, wr, br, wi, bi, lam)


def _lru_bwd(z, hs, dy, cw, cb, wr, br, wi, bi, lam, l):
    s = z.shape[0]
    t = _tile(s)
    lay = _lay(l)
    nt = s // t
    c = D_LRU
    hb = t // SUBLANES

    def body(za_ref, zh_ref, ga_ref, hs_ref, hh_ref, dy_ref, cw_ref, cb_ref, wr_ref, br_ref, wi_ref, bi_ref, lam_ref,
             dza_ref, dga_ref, dwr_ref, dwi_ref, dcw_ref, dcb_ref, dbr_ref, dbi_ref, dlam_ref, lcar, dxn):
        i = pl.program_id(0)
        tt = nt - 1 - i

        @pl.when(i == 0)
        def _():
            lcar[...] = jnp.zeros_like(lcar)
            dxn[...] = jnp.zeros_like(dxn)
            for ref in (dwr_ref, dwi_ref, dcw_ref, dcb_ref, dbr_ref, dbi_ref, dlam_ref):
                ref[...] = jnp.zeros_like(ref)

        first = (tt > 0).astype(F32)
        za = za_ref[...]
        xa, sh, r, ig, sp, a, m, m2 = _lru_gates(za, zh_ref[...] * first, cw_ref, cb_ref, wr_ref, br_ref, wi_ref,
                                                 bi_ref, lam_ref)
        hs_v = hs_ref[...]
        hprev = _down(jnp.concatenate([hh_ref[...] * first, hs_v], axis=0), 1)[SUBLANES:SUBLANES + t]
        ga = ga_ref[...]
        sg = _sig(ga)
        silu = ga * sg
        dya = dy_ref[...]
        dga_ref[...] = (dya * hs_v * (sg * (1.0 + ga * (1.0 - sg)))).astype(dga_ref.dtype)
        row = _rows((t, c))
        acc_h = dya * silu + jnp.where(row == t - 1, lcar[...], 0.0)
        acc_b = jnp.where(row < t - 1, _up(a, 1), 0.0)
        k = 1
        while k < t:
            acc_h = acc_h + acc_b * _up(acc_h, k)
            acc_b = acc_b * _up(acc_b, k)
            k *= 2
        lmb = acc_h
        lcar[...] = jnp.sum(jnp.where(row == 0, a * lmb, 0.0), axis=0, keepdims=True)
        da = lmb * hprev
        dxa = lmb * m * ig
        di = lmb * m * xa
        dm = lmb * ig * xa
        dla = da * a - dm * (a * a) * lax.rsqrt(m2)
        dr = dla * ((-LRU_C) * sp)
        lam = lam_ref[...]
        dsp = jnp.sum(dla * ((-LRU_C) * r), axis=0, keepdims=True)
        dlam_ref[...] += dsp * (-1.0 / (1.0 + jnp.exp(lam)))
        dpr = dr * r * (1.0 - r)
        dpi = di * ig * (1.0 - ig)
        dbr_ref[...] += jnp.sum(dpr, axis=0, keepdims=True)
        dbi_ref[...] += jnp.sum(dpi, axis=0, keepdims=True)
        dwr_ref[...] += _dot_tn(xa, dpr)
        dwi_ref[...] += _dot_tn(xa, dpi)
        dxa = dxa + _dot_nt(dpr, wr_ref[...]) + _dot_nt(dpi, wi_ref[...])
        dcb_ref[...] += jnp.sum(dxa, axis=0, keepdims=True)
        for k in range(CONV_WIDTH):
            dcw_ref[k:k + 1, :] += jnp.sum(dxa * sh[CONV_WIDTH - 1 - k], axis=0, keepdims=True)
        ext = jnp.concatenate([dxa, dxn[...]], axis=0)
        dza = cw_ref[3:4, :] * dxa
        for j in (1, 2, 3):
            dza = dza + cw_ref[3 - j:4 - j, :] * _up(ext, j)[:t]
        dza_ref[...] = dza.astype(dza_ref.dtype)
        dxn[...] = dxa[:SUBLANES]

    full = lambda shp: pl.BlockSpec(shp, lambda i: (0, 0))
    rev = lambda i: nt - 1 - i
    halo = lambda i: (jnp.maximum((nt - 1 - i) * hb - 1, 0), 0)
    outs = pl.pallas_call(
        body, name="lru_bwd", grid=(nt,),
        in_specs=[pl.BlockSpec((t, c), lambda i: (rev(i), 0)), pl.BlockSpec((SUBLANES, c), halo),
                  pl.BlockSpec((t, c), lambda i: (rev(i), 1)),
                  pl.BlockSpec((t, c), lambda i: (rev(i), 0)), pl.BlockSpec((SUBLANES, c), halo),
                  pl.BlockSpec((t, c), lambda i: (rev(i), 0)),
                  lay((CONV_WIDTH, c)), lay((1, c)), lay((c, c)), lay((1, c)), lay((c, c)), lay((1, c)),
                  lay((1, c))],
        out_specs=[pl.BlockSpec((t, c), lambda i: (rev(i), 0)), pl.BlockSpec((t, c), lambda i: (rev(i), 0)),
                   full((c, c)), full((c, c)), full((CONV_WIDTH, c)), full((1, c)), full((1, c)), full((1, c)),
                   full((1, c))],
        out_shape=[jax.ShapeDtypeStruct((s, c), MXU_DTYPE), jax.ShapeDtypeStruct((s, c), MXU_DTYPE),
                   jax.ShapeDtypeStruct((c, c), F32), jax.ShapeDtypeStruct((c, c), F32),
                   jax.ShapeDtypeStruct((CONV_WIDTH, c), F32)] + [jax.ShapeDtypeStruct((1, c), F32)] * 4,
        scratch_shapes=[pltpu.VMEM((1, c), F32), pltpu.VMEM((SUBLANES, c), F32)],
        compiler_params=_cp(("arbitrary",), 48),
    )(z, z, z, hs, hs, dy, cw, cb, wr, br, wi, bi, lam)
    return outs


POOL_HALO = 16


def _pool_select(lane, v2, v4, v8, v16):
    return jnp.where(lane < 64, v2, jnp.where(lane < 128, v4, jnp.where(lane < 192, v8, v16)))


def _pool_counts(t0, t, c):
    lane = _lanes((t, c))
    win = _pool_select(lane, 2.0, 4.0, 8.0, 16.0)
    seen = (t0 + _rows((t, c)) + 1).astype(F32)
    return lane, jnp.minimum(seen, win)


def _pooled(zc, halo, lane, cnt):
    t = zc.shape[0]
    ext = jnp.concatenate([halo, zc], axis=0)
    s2 = ext + _down(ext, 1)
    s4 = s2 + _down(s2, 2)
    s8 = s4 + _down(s4, 4)
    s16 = s8 + _down(s8, 8)
    cut = lambda v: v[POOL_HALO:POOL_HALO + t]
    return _pool_select(lane, cut(s2), cut(s4), cut(s8), cut(s16)) / cnt - zc


def _pool_fwd(z, wp, ps, l):
    s = z.shape[0]
    t = _wide_tile(s)
    c = D_POOL
    lay = _lay(l)

    def body(zc_ref, gc_ref, wp_ref, ps_ref, yc_ref, zprev):
        i = pl.program_id(0)

        @pl.when(i == 0)
        def _():
            zprev[...] = jnp.zeros_like(zprev)

        zc = zc_ref[...]
        lane, cnt = _pool_counts(i * t, t, c)
        pooled = _pooled(zc, zprev[...], lane, cnt)
        pc = _dot(pooled, wp_ref[...])
        gc = gc_ref[...]
        yc_ref[...] = (pc * ps_ref[...] * (gc * _sig(gc))).astype(yc_ref.dtype)
        zprev[...] = zc_ref[t - POOL_HALO:t, :]

    return pl.pallas_call(
        body, name="pool_fwd", grid=(s // t,),
        in_specs=[pl.BlockSpec((t, c), lambda i: (i, 7)), pl.BlockSpec((t, c), lambda i: (i, 8)),
                  lay((c, c)), lay((1, c))],
        out_specs=pl.BlockSpec((t, c), lambda i: (i, 0)),
        out_shape=jax.ShapeDtypeStruct((s, c), MXU_DTYPE),
        scratch_shapes=[pltpu.VMEM((POOL_HALO, c), F32)],
        compiler_params=_cp(("arbitrary",), 32),
    )(z, z, wp, ps)


def _pool_bwd(z, dy, wp, ps, l):
    s = z.shape[0]
    t = _wide_tile(s)
    lay = _lay(l)
    nt = s // t
    c = D_POOL
    hb = t // POOL_HALO

    def body(zc_ref, zh_ref, gc_ref, dy_ref, wp_ref, ps_ref, dzc_ref, dgc_ref, dwp_ref, dps_ref, ddn):
        i = pl.program_id(0)
        tt = nt - 1 - i

        @pl.when(i == 0)
        def _():
            ddn[...] = jnp.zeros_like(ddn)
            dwp_ref[...] = jnp.zeros_like(dwp_ref)
            dps_ref[...] = jnp.zeros_like(dps_ref)

        first = (tt > 0).astype(F32)
        zc = zc_ref[...]
        lane, cnt = _pool_counts(tt * t, t, c)
        pooled = _pooled(zc, zh_ref[...] * first, lane, cnt)
        pc = _dot(pooled, wp_ref[...])
        gc = gc_ref[...]
        sg = _sig(gc)
        silu = gc * sg
        dyc = dy_ref[...]
        ps_v = ps_ref[...]
        dgc_ref[...] = (dyc * pc * ps_v * (sg * (1.0 + gc * (1.0 - sg)))).astype(dgc_ref.dtype)
        dps_ref[...] += jnp.sum(dyc * pc * silu, axis=0, keepdims=True)
        dpc = dyc * ps_v * silu
        dwp_ref[...] += _dot_tn(pooled, dpc)
        dpooled = _dot_nt(dpc, wp_ref[...])
        dd = dpooled / cnt
        ext = jnp.concatenate([dd, ddn[...]], axis=0)
        f2 = ext + _up(ext, 1)
        f4 = f2 + _up(f2, 2)
        f8 = f4 + _up(f4, 4)
        f16 = f8 + _up(f8, 8)
        dzc = _pool_select(lane, f2[:t], f4[:t], f8[:t], f16[:t]) - dpooled
        dzc_ref[...] = dzc.astype(dzc_ref.dtype)
        ddn[...] = dd[:POOL_HALO]

    full = lambda shp: pl.BlockSpec(shp, lambda i: (0, 0))
    rev = lambda i: nt - 1 - i
    return pl.pallas_call(
        body, name="pool_bwd", grid=(nt,),
        in_specs=[pl.BlockSpec((t, c), lambda i: (rev(i), 7)),
                  pl.BlockSpec((POOL_HALO, c), lambda i: (jnp.maximum(rev(i) * hb - 1, 0), 7)),
                  pl.BlockSpec((t, c), lambda i: (rev(i), 8)),
                  pl.BlockSpec((t, c), lambda i: (rev(i), 0)),
                  lay((c, c)), lay((1, c))],
        out_specs=[pl.BlockSpec((t, c), lambda i: (rev(i), 0)), pl.BlockSpec((t, c), lambda i: (rev(i), 0)),
                   full((c, c)), full((1, c))],
        out_shape=[jax.ShapeDtypeStruct((s, c), MXU_DTYPE), jax.ShapeDtypeStruct((s, c), MXU_DTYPE),
                   jax.ShapeDtypeStruct((c, c), F32), jax.ShapeDtypeStruct((1, c), F32)],
        scratch_shapes=[pltpu.VMEM((POOL_HALO, c), F32)],
        compiler_params=_cp(("arbitrary",), 32),
    )(z, z, z, dy, wp, ps)


def _rope_tables(s):
    pos = jnp.arange(s, dtype=F32)
    inv_freq = ROPE_BASE ** (-jnp.arange(0, QK_ROPE, 2, dtype=F32) / QK_ROPE)
    ang = pos[:, None] * inv_freq[None, :]
    cos, sin = jnp.cos(ang), jnp.sin(ang)
    half = QK_ROPE // 2
    ones = jnp.ones((s, QK_NOPE), F32)
    z64 = jnp.zeros((s, QK_NOPE), F32)
    zh = jnp.zeros((s, half), F32)
    z32 = jnp.zeros((s, HEAD_BLK - QK_NOPE - QK_ROPE), F32)
    c_t = jnp.concatenate([ones, cos, cos, z32], axis=1)
    s1_t = jnp.concatenate([z64, -sin, zh, z32], axis=1)
    s2_t = jnp.concatenate([z64, zh, sin, z32], axis=1)
    return c_t, s1_t, s2_t


def _rope(x, c_t, s1_t, s2_t):
    return x * c_t + pltpu.roll(x, HEAD_BLK - 16, 1) * s1_t + pltpu.roll(x, 16, 1) * s2_t


def _unrope(d, c_t, s1_t, s2_t):
    return d * c_t + pltpu.roll(d * s1_t, 16, 1) + pltpu.roll(d * s2_t, HEAD_BLK - 16, 1)


def _mla_prep_fwd(z, gq, gkv, wuq, wukv, c_t, s1_t, s2_t, l):
    s = z.shape[0]
    t = _wide_tile(s)
    hq = MLA_HEADS * HEAD_BLK
    lay = _lay(l)

    def body(cq_ref, ckv_ref, kr_ref, gq_ref, gkv_ref, wuq_ref, wukv_ref, c_ref, s1_ref, s2_ref,
             q_ref, k_ref, v_ref, qn_ref, kvn_ref):
        ct, s1, s2 = c_ref[...], s1_ref[...], s2_ref[...]
        cq = cq_ref[...]
        qn = (cq * lax.rsqrt(jnp.mean(cq * cq, axis=-1, keepdims=True) + EPS) * gq_ref[...]).astype(MXU_DTYPE)
        qn_ref[...] = qn
        q = jnp.dot(qn, wuq_ref[...], preferred_element_type=F32)
        ckv = ckv_ref[...]
        kvn = (ckv * lax.rsqrt(jnp.mean(ckv * ckv, axis=-1, keepdims=True) + EPS) * gkv_ref[...]).astype(MXU_DTYPE)
        kvn_ref[...] = kvn
        kvp = jnp.dot(kvn, wukv_ref[...], preferred_element_type=F32)
        krr = _rope(kr_ref[...], ct, s1, s2)
        for h in range(MLA_HEADS):
            blk = slice(h * HEAD_BLK, (h + 1) * HEAD_BLK)
            q_ref[:, blk] = (_rope(q[:, blk], ct, s1, s2) * Q_PRESCALE).astype(q_ref.dtype)
            k_ref[:, blk] = (kvp[:, blk] + krr).astype(k_ref.dtype)
        v_ref[...] = kvp[:, hq:].astype(v_ref.dtype)

    tab = pl.BlockSpec((t, HEAD_BLK), lambda i: (i, 0))
    return pl.pallas_call(
        body, name="mla_prep_fwd", grid=(s // t,),
        in_specs=[pl.BlockSpec((t, Q_RANK), lambda i: (i, 2)), pl.BlockSpec((t, KV_RANK), lambda i: (i, 6)),
                  pl.BlockSpec((t, HEAD_BLK), lambda i: (i, 18)),
                  lay((1, Q_RANK)), lay((1, KV_RANK)), lay((Q_RANK, hq)), lay((KV_RANK, hq + D_MLA)),
                  tab, tab, tab],
        out_specs=[pl.BlockSpec((t, hq), lambda i: (i, 0)), pl.BlockSpec((t, hq), lambda i: (i, 0)),
                   pl.BlockSpec((t, D_MLA), lambda i: (i, 0)), pl.BlockSpec((t, Q_RANK), lambda i: (i, 0)),
                   pl.BlockSpec((t, KV_RANK), lambda i: (i, 0))],
        out_shape=[jax.ShapeDtypeStruct((s, hq), MXU_DTYPE), jax.ShapeDtypeStruct((s, hq), MXU_DTYPE),
                   jax.ShapeDtypeStruct((s, D_MLA), MXU_DTYPE), jax.ShapeDtypeStruct((s, Q_RANK), MXU_DTYPE),
                   jax.ShapeDtypeStruct((s, KV_RANK), MXU_DTYPE)],
        compiler_params=_cp(("parallel",), 56),
    )(z, z, z, gq, gkv, wuq, wukv, c_t, s1_t, s2_t)


SUM_LANE_A = V_DIM
SUM_LANE_B = 0
FLASH_TILES_PER_TRIP = 4


def _loop_in_trips(lo, hi, per_trip, step, carry):
    whole = (hi - lo) // per_trip

    def trip(n, cr):
        for u in range(per_trip):
            cr = step(lo + per_trip * n + u, cr)
        return cr

    carry = lax.fori_loop(0, whole, trip, carry)
    return lax.fori_loop(lo + per_trip * whole, hi, step, carry)


def _flash_fwd(q, k, v, z):
    s = q.shape[0]
    t = _tile(s)
    nq = s // t
    pw = 2 * HEAD_BLK

    def body(q_ref, k_ref, v_ref, gb_ref, o_ref, yb_ref, lse_ref):
        i = pl.program_id(1)
        qv = q_ref[...]
        qa, qb = qv[:, :HEAD_BLK], qv[:, HEAD_BLK:]
        lane = _lanes((t, HEAD_BLK))
        lo = lane < V_DIM

        def update(qh, kh, vh, m, acc, masked):
            sc = _dot_nt(qh, kh)
            if masked:
                sc = jnp.where(_lanes((t, t)) <= _rows((t, t)), sc, -1e30)
            m_new = jnp.maximum(m, jnp.max(sc, axis=-1, keepdims=True))
            p = jnp.exp2(sc - m_new).astype(MXU_DTYPE)
            return m_new, acc * jnp.exp2(m - m_new) + _dot(p, vh)

        def step(j, carry, masked):
            ma, mb, acc_a, acc_b = carry
            kv_rows = pl.ds(pl.multiple_of(j * t, t), t)
            kt = k_ref[kv_rows, :]
            vt = v_ref[kv_rows, :]
            lane_v = _lanes(vt.shape)
            one = jnp.ones_like(vt)
            zero_v = jnp.zeros_like(vt)
            v_a = jnp.where(lane_v < V_DIM, vt, jnp.where(lane_v == SUM_LANE_A, one, zero_v))
            v_b = jnp.where(lane_v >= V_DIM, vt, jnp.where(lane_v == SUM_LANE_B, one, zero_v))
            ma, acc_a = update(qa, kt[:, :HEAD_BLK], v_a, ma, acc_a, masked)
            mb, acc_b = update(qb, kt[:, HEAD_BLK:], v_b, mb, acc_b, masked)
            return ma, mb, acc_a, acc_b

        neg = jnp.full((t, 1), -1e30, F32)
        zero = jnp.zeros((t, HEAD_BLK), F32)
        carry = _loop_in_trips(0, i, FLASH_TILES_PER_TRIP, lambda j, cr: step(j, cr, False), (neg, neg, zero, zero))
        ma, mb, acc_a, acc_b = step(i, carry, True)
        la = jnp.sum(jnp.where(lane == SUM_LANE_A, acc_a, 0.0), axis=-1, keepdims=True)
        lb = jnp.sum(jnp.where(lane == SUM_LANE_B, acc_b, 0.0), axis=-1, keepdims=True)
        o = jnp.where(lo, acc_a * (1.0 / la), acc_b * (1.0 / lb))
        o_ref[...] = o
        gb = gb_ref[...]
        yb_ref[...] = (o * (gb * _sig(gb))).astype(yb_ref.dtype)
        lse = jnp.where(lo, ma + jnp.log(la) * LOG2_E, mb + jnp.log(lb) * LOG2_E)
        lse_t = lse.T
        r8 = _rows((SUBLANES, t))
        lse_ref[0, 0] = jnp.where(r8 == 0, lse_t[0:SUBLANES],
                                  jnp.where(r8 == 1, _down(lse_t[V_DIM:V_DIM + SUBLANES], 1), 0.0))

    return pl.pallas_call(
        body, name="flash_fwd", grid=(N_PAIR, nq),
        in_specs=[pl.BlockSpec((t, pw), lambda p, i: (i, p)), pl.BlockSpec((s, pw), lambda p, i: (0, p)),
                  pl.BlockSpec((s, HEAD_BLK), lambda p, i: (0, p)),
                  pl.BlockSpec((t, HEAD_BLK), lambda p, i: (i, 9 + p))],
        out_specs=[pl.BlockSpec((t, HEAD_BLK), lambda p, i: (i, p)), pl.BlockSpec((t, HEAD_BLK), lambda p, i: (i, p)),
                   pl.BlockSpec((1, 1, SUBLANES, t), lambda p, i: (p, i, 0, 0))],
        out_shape=[jax.ShapeDtypeStruct((s, D_MLA), F32), jax.ShapeDtypeStruct((s, D_MLA), MXU_DTYPE),
                   jax.ShapeDtypeStruct((N_PAIR, nq, SUBLANES, t), F32)],
        compiler_params=_cp(("parallel", "parallel"), 48),
    )(q, k, v, z)


def _flash_bwd(q, k, v, do, lse, delta):
    s = q.shape[0]
    t = _tile(s)
    nq = s // t
    pw = 2 * HEAD_BLK

    def body(q_ref, do_ref, lse_ref, dl_ref, k_ref, v_ref, dq_ref, dk_ref, dv_ref):
        j = pl.program_id(1)

        @pl.when(j == 0)
        def _():
            dq_ref[...] = jnp.zeros_like(dq_ref)

        kt = k_ref[...]
        ka, kb = kt[:, :HEAD_BLK], kt[:, HEAD_BLK:]
        vt = v_ref[...]

        def head(kh, qh, do_h, lse_row, dl_row, masked):
            st = _dot_nt(kh, qh)
            if masked:
                st = jnp.where(_rows((t, t)) <= _lanes((t, t)), st, -1e30)
            pt = jnp.exp2(st - lse_row)
            dv_h = _dot(pt, do_h)
            dst = (pt * (_dot_nt(vt, do_h) - dl_row)).astype(MXU_DTYPE)
            return dv_h, _dot(dst, qh), _dot_tn(dst, kh)

        def step(i, carry, masked):
            dka, dkb, dv = carry
            q_rows = pl.ds(pl.multiple_of(i * t, t), t)
            qv = q_ref[q_rows, :]
            dov = do_ref[q_rows, :]
            lane = _lanes(dov.shape)
            do_lo = jnp.where(lane < V_DIM, dov, jnp.zeros_like(dov))
            do_hi = jnp.where(lane >= V_DIM, dov, jnp.zeros_like(dov))
            dva, dk_a, dq_a = head(ka, qv[:, :HEAD_BLK], do_lo, lse_ref[0, i, 0:1, :], dl_ref[0, i, 0:1, :], masked)
            dvb, dk_b, dq_b = head(kb, qv[:, HEAD_BLK:], do_hi, lse_ref[0, i, 1:2, :], dl_ref[0, i, 1:2, :], masked)
            dq_ref[q_rows, 0:HEAD_BLK] += dq_a
            dq_ref[q_rows, HEAD_BLK:pw] += dq_b
            return dka + dk_a, dkb + dk_b, dv + dva + dvb

        zero = jnp.zeros((t, HEAD_BLK), F32)
        carry = step(j, (zero, zero, zero), True)
        dka, dkb, dv = _loop_in_trips(j + 1, nq, FLASH_TILES_PER_TRIP, lambda i, cr: step(i, cr, False), carry)
        dk_ref[:, 0:HEAD_BLK] = dka * LN_2
        dk_ref[:, HEAD_BLK:pw] = dkb * LN_2
        dv_ref[...] = dv.astype(dv_ref.dtype)

    return pl.pallas_call(
        body, name="flash_bwd", grid=(N_PAIR, nq),
        in_specs=[pl.BlockSpec((s, pw), lambda p, j: (0, p)), pl.BlockSpec((s, HEAD_BLK), lambda p, j: (0, p)),
                  pl.BlockSpec((1, nq, SUBLANES, t), lambda p, j: (p, 0, 0, 0)),
                  pl.BlockSpec((1, nq, SUBLANES, t), lambda p, j: (p, 0, 0, 0)),
                  pl.BlockSpec((t, pw), lambda p, j: (j, p)), pl.BlockSpec((t, HEAD_BLK), lambda p, j: (j, p))],
        out_specs=[pl.BlockSpec((s, pw), lambda p, j: (0, p)), pl.BlockSpec((t, pw), lambda p, j: (j, p)),
                   pl.BlockSpec((t, HEAD_BLK), lambda p, j: (j, p))],
        out_shape=[jax.ShapeDtypeStruct((s, MLA_HEADS * HEAD_BLK), F32),
                   jax.ShapeDtypeStruct((s, MLA_HEADS * HEAD_BLK), F32),
                   jax.ShapeDtypeStruct((s, D_MLA), MXU_DTYPE)],
        compiler_params=_cp(("parallel", "arbitrary"), 56),
    )(q, do, lse, delta, k, v)


def _mla_prep_bwd(dq, dk, dv, z, qn, kvn, gq, gkv, wuq, wukv, c_t, s1_t, s2_t, l):
    s = z.shape[0]
    t = _wide_tile(s)
    hq = MLA_HEADS * HEAD_BLK
    lay = _lay(l)

    def body(dq_ref, dk_ref, dv_ref, cq_ref, ckv_ref, qn_ref, kvn_ref, gq_ref, gkv_ref, wuq_ref, wukv_ref,
             c_ref, s1_ref, s2_ref, dcq_ref, dckv_ref, dkr_ref, dwuq_ref, dwukv_ref, dgq_ref, dgkv_ref,
             dqu_ref, dkvp_ref):
        @pl.when(pl.program_id(0) == 0)
        def _():
            for ref in (dwuq_ref, dwukv_ref, dgq_ref, dgkv_ref):
                ref[...] = jnp.zeros_like(ref)

        ct, s1, s2 = c_ref[...], s1_ref[...], s2_ref[...]
        dk_sum = jnp.zeros((t, HEAD_BLK), F32)
        for h in range(MLA_HEADS):
            blk = slice(h * HEAD_BLK, (h + 1) * HEAD_BLK)
            dqu_ref[:, blk] = _unrope(dq_ref[:, blk] * ATT_SCALE, ct, s1, s2).astype(dqu_ref.dtype)
            dkh = dk_ref[:, blk]
            dk_sum = dk_sum + dkh
            dkvp_ref[:, blk] = dkh.astype(dkvp_ref.dtype)
        dkvp_ref[:, hq:] = dv_ref[...]
        lane = _lanes((t, HEAD_BLK))
        rope_lanes = (lane >= KR_LANE0) & (lane < KR_LANE0 + QK_ROPE)
        dkr_ref[...] = _unrope(jnp.where(rope_lanes, dk_sum, 0.0), ct, s1, s2).astype(dkr_ref.dtype)

        def norm_bwd(c_in, g, dn_out, dc_ref, dg_ref):
            rs = lax.rsqrt(jnp.mean(c_in * c_in, axis=-1, keepdims=True) + EPS)
            n = c_in * rs
            dg_ref[...] += jnp.sum(dn_out * n, axis=0, keepdims=True)
            dn = dn_out * g
            dc_ref[...] = (rs * (dn - n * jnp.mean(dn * n, axis=-1, keepdims=True))).astype(dc_ref.dtype)

        dqu, dkvp = dqu_ref[...], dkvp_ref[...]
        dwuq_ref[...] += _dot_tn(qn_ref[...], dqu)
        dwukv_ref[...] += _dot_tn(kvn_ref[...], dkvp)
        norm_bwd(cq_ref[...], gq_ref[...], _dot_nt(dqu, wuq_ref[...]), dcq_ref, dgq_ref)
        norm_bwd(ckv_ref[...], gkv_ref[...], _dot_nt(dkvp, wukv_ref[...]), dckv_ref, dgkv_ref)

    full = lambda shp: pl.BlockSpec(shp, lambda i: (0, 0))
    tab = pl.BlockSpec((t, HEAD_BLK), lambda i: (i, 0))
    row = lambda w: pl.BlockSpec((t, w), lambda i: (i, 0))
    return pl.pallas_call(
        body, name="mla_prep_bwd", grid=(s // t,),
        in_specs=[row(hq), row(hq), row(D_MLA),
                  pl.BlockSpec((t, Q_RANK), lambda i: (i, 2)), pl.BlockSpec((t, KV_RANK), lambda i: (i, 6)),
                  row(Q_RANK), row(KV_RANK),
                  lay((1, Q_RANK)), lay((1, KV_RANK)), lay((Q_RANK, hq)), lay((KV_RANK, hq + D_MLA)),
                  tab, tab, tab],
        out_specs=[row(Q_RANK), row(KV_RANK), row(HEAD_BLK), full((Q_RANK, hq)), full((KV_RANK, hq + D_MLA)),
                   full((1, Q_RANK)), full((1, KV_RANK))],
        out_shape=[jax.ShapeDtypeStruct((s, Q_RANK), MXU_DTYPE), jax.ShapeDtypeStruct((s, KV_RANK), MXU_DTYPE),
                   jax.ShapeDtypeStruct((s, HEAD_BLK), MXU_DTYPE),
                   jax.ShapeDtypeStruct((Q_RANK, hq), F32), jax.ShapeDtypeStruct((KV_RANK, hq + D_MLA), F32),
                   jax.ShapeDtypeStruct((1, Q_RANK), F32), jax.ShapeDtypeStruct((1, KV_RANK), F32)],
        scratch_shapes=[pltpu.VMEM((t, hq), MXU_DTYPE), pltpu.VMEM((t, hq + D_MLA), MXU_DTYPE)],
        compiler_params=_cp(("arbitrary",), 56),
    )(dq, dk, dv, z, z, qn, kvn, gq, gkv, wuq, wukv, c_t, s1_t, s2_t)


def _outproj_fwd(x, ya, yb, yc, w, l):
    s = x.shape[0]
    t = _wide_tile(s)
    lay = _lay(l)

    def body(x_ref, ya_ref, yb_ref, yc_ref, wa_ref, wb_ref, wc_ref, o_ref):
        o_ref[...] = (x_ref[...] + _dot(ya_ref[...], wa_ref[...]) + _dot(yb_ref[...], wb_ref[...])
                      + _dot(yc_ref[...], wc_ref[...]))

    row = lambda w_: pl.BlockSpec((t, w_), lambda i: (i, 0))
    return pl.pallas_call(
        body, name="outproj_fwd", grid=(s // t,),
        in_specs=[row(D_MODEL), row(D_LRU), row(D_MLA), row(D_POOL),
                  lay((D_LRU, D_MODEL), 0), lay((D_MLA, D_MODEL), 1), lay((D_POOL, D_MODEL), 3)],
        out_specs=row(D_MODEL),
        out_shape=jax.ShapeDtypeStruct((s, D_MODEL), F32),
        compiler_params=_cp(("parallel",), 40),
    )(x, ya, yb, yc, w, w, w)


def _outproj_bwd(dx, ya, yb, yc, o, z, w, l):
    s = dx.shape[0]
    t = _tile(s)
    nq = s // t
    rows = N_PAIR * SUBLANES

    def body(dx_ref, ya_ref, yb_ref, yc_ref, o_ref, gb_ref, w_ref, dya_ref, dyc_ref, do_ref, dgb_ref, dl_ref, dw_ref):
        @pl.when(pl.program_id(0) == 0)
        def _():
            dw_ref[...] = jnp.zeros_like(dw_ref)

        dxv = dx_ref[...].astype(MXU_DTYPE)
        dy = _dot_nt(dxv, w_ref[...])
        dw_ref[0:D_LRU, :] += _dot_tn(ya_ref[...], dxv)
        dw_ref[D_LRU:D_LRU + D_MLA, :] += _dot_tn(yb_ref[...], dxv)
        dw_ref[D_LRU + D_MLA:D_MIX, :] += _dot_tn(yc_ref[...], dxv)
        dya_ref[...] = dy[:, :D_LRU]
        dyc_ref[...] = dy[:, D_LRU + D_MLA:]
        dyb = dy[:, D_LRU:D_LRU + D_MLA]
        gb = gb_ref[...]
        sg = _sig(gb)
        ov = o_ref[...]
        do = dyb * (gb * sg)
        do_ref[...] = do.astype(do_ref.dtype)
        dgb_ref[...] = (dyb * ov * (sg * (1.0 + gb * (1.0 - sg)))).astype(dgb_ref.dtype)
        r = _rows((rows, D_MLA))
        head = (r // SUBLANES) * 2 + (r % SUBLANES)
        sel = ((r % SUBLANES) < 2) & (_lanes((rows, D_MLA)) // V_DIM == head)
        dl = lax.dot_general(sel.astype(F32), do * ov, (((1,), (1,)), ((), ())),
                             precision=lax.Precision.HIGHEST, preferred_element_type=F32)
        for p in range(N_PAIR):
            dl_ref[p, 0] = dl[p * SUBLANES:(p + 1) * SUBLANES]

    row = lambda w_: pl.BlockSpec((t, w_), lambda i: (i, 0))
    return pl.pallas_call(
        body, name="outproj_bwd", grid=(nq,),
        in_specs=[row(D_MODEL), row(D_LRU), row(D_MLA), row(D_POOL), row(D_MLA),
                  pl.BlockSpec((t, D_MLA), lambda i: (i, 3)), _lay(l)((D_MIX, D_MODEL))],
        out_specs=[row(D_LRU), row(D_POOL), row(D_MLA), row(D_MLA),
                   pl.BlockSpec((N_PAIR, 1, SUBLANES, t), lambda i: (0, i, 0, 0)),
                   pl.BlockSpec((D_MIX, D_MODEL), lambda i: (0, 0))],
        out_shape=[jax.ShapeDtypeStruct((s, D_LRU), F32), jax.ShapeDtypeStruct((s, D_POOL), F32),
                   jax.ShapeDtypeStruct((s, D_MLA), MXU_DTYPE), jax.ShapeDtypeStruct((s, D_MLA), MXU_DTYPE),
                   jax.ShapeDtypeStruct((N_PAIR, nq, SUBLANES, t), F32),
                   jax.ShapeDtypeStruct((D_MIX, D_MODEL), F32)],
        compiler_params=_cp(("arbitrary",), 48),
    )(dx, ya, yb, yc, o, z, w)


DZ_WIDTHS = (D_LRU, D_LRU, Q_RANK, D_MLA, KV_RANK, D_POOL, D_POOL, HEAD_BLK)


def _dwin(h, dz_parts):
    s = h.shape[0]
    t = _tile(s)

    def body(h_ref, *refs):
        o_ref = refs[-1]

        @pl.when(pl.program_id(0) == 0)
        def _():
            o_ref[...] = jnp.zeros_like(o_ref)

        o_ref[...] += _dot_tn(h_ref[...], jnp.concatenate([r[...] for r in refs[:-1]], axis=1))

    row = lambda w_: pl.BlockSpec((t, w_), lambda i: (i, 0))
    return pl.pallas_call(
        body, name="dwin", grid=(s // t,),
        in_specs=[row(D_MODEL)] + [row(wd) for wd in DZ_WIDTHS],
        out_specs=pl.BlockSpec((D_MODEL, D_INP), lambda i: (0, 0)),
        out_shape=jax.ShapeDtypeStruct((D_MODEL, D_INP), F32),
        compiler_params=_cp(("arbitrary",), 56),
    )(h, *dz_parts)


def _inproj_bwd(dz_parts, w, x, g, dxn, l):
    s = x.shape[0]
    t = _tile(s)
    lay = _lay(l)
    n_parts = len(DZ_WIDTHS)

    def body(*refs):
        part_refs = refs[:n_parts]
        w_ref, x_ref, g_ref, dxn_ref, dx_ref, dg_ref = refs[n_parts:]

        @pl.when(pl.program_id(0) == 0)
        def _():
            dg_ref[...] = jnp.zeros_like(dg_ref)

        dh = _dot_nt(jnp.concatenate([r[...] for r in part_refs], axis=1), w_ref[...])
        xv = x_ref[...]
        rs = lax.rsqrt(jnp.mean(xv * xv, axis=-1, keepdims=True) + EPS)
        n = xv * rs
        dg_ref[...] += jnp.sum(dh * n, axis=0, keepdims=True)
        dn = dh * g_ref[...]
        dx_ref[...] = dxn_ref[...] + rs * (dn - n * jnp.mean(dn * n, axis=-1, keepdims=True))

    row = lambda w_: pl.BlockSpec((t, w_), lambda i: (i, 0))
    return pl.pallas_call(
        body, name="inproj_bwd", grid=(s // t,),
        in_specs=[row(wd) for wd in DZ_WIDTHS] + [lay((D_MODEL, D_INP)), row(D_MODEL), lay((1, D_MODEL)),
                                                  row(D_MODEL)],
        out_specs=[row(D_MODEL), pl.BlockSpec((1, D_MODEL), lambda i: (0, 0))],
        out_shape=[jax.ShapeDtypeStruct((s, D_MODEL), F32), jax.ShapeDtypeStruct((1, D_MODEL), F32)],
        compiler_params=_cp(("arbitrary",), 48),
    )(*dz_parts, w, x, g, dxn)


def _loss_head(x, g, tgt):
    s = x.shape[0]
    t = _tile(s)

    def body(x_ref, g_ref, t_ref, dx_ref, loss_ref, dg_ref):
        @pl.when(pl.program_id(0) == 0)
        def _():
            loss_ref[...] = jnp.zeros_like(loss_ref)
            dg_ref[...] = jnp.zeros_like(dg_ref)

        xv = x_ref[...]
        rs = lax.rsqrt(jnp.mean(xv * xv, axis=-1, keepdims=True) + EPS)
        n = xv * rs
        gv = g_ref[...]
        e = n * gv - t_ref[...]
        loss_ref[...] += 0.5 * jnp.sum(jnp.mean(e * e, axis=-1, keepdims=True))
        dyf = e * (1.0 / D_MODEL)
        dg_ref[...] += jnp.sum(dyf * n, axis=0, keepdims=True)
        dn = dyf * gv
        dx_ref[...] = rs * (dn - n * jnp.mean(dn * n, axis=-1, keepdims=True))

    row = pl.BlockSpec((t, D_MODEL), lambda i: (i, 0))
    vec = pl.BlockSpec((1, D_MODEL), lambda i: (0, 0))
    return pl.pallas_call(
        body, name="loss_head", grid=(s // t,),
        in_specs=[row, vec, row],
        out_specs=[row, pl.BlockSpec((1, LANES), lambda i: (0, 0)), vec],
        out_shape=[jax.ShapeDtypeStruct((s, D_MODEL), F32), jax.ShapeDtypeStruct((1, LANES), F32),
                   jax.ShapeDtypeStruct((1, D_MODEL), F32)],
        compiler_params=_cp(("arbitrary",), 32),
    )(x, g, tgt)


def _block_diag(w):
    n, h, d, _ = w.shape
    return jnp.einsum('lhij,hk->lhikj', w, jnp.eye(h, dtype=w.dtype)).reshape(n, h * d, h * d)


def _diag_blocks(wfull, h):
    d = wfull.shape[-1] // h
    return jnp.stack([wfull[:, i * d:(i + 1) * d, i * d:(i + 1) * d] for i in range(h)], axis=1)


REF_TO_PERM = np.concatenate([np.arange(0, 1152), np.arange(1536, 1792),
                              np.arange(2304 + KR_LANE0, 2304 + KR_LANE0 + QK_ROPE),
                              np.arange(1152, 1536), np.arange(1792, 2304)])
N_SHARD = 4
W_IN_SHARD = D_IN // N_SHARD


def _w_in_runs():
    ref_of_perm = -np.ones(D_INP, np.int64)
    ref_of_perm[REF_TO_PERM] = np.arange(D_IN)
    perm_runs, p = [], 0
    while p < D_INP:
        r, q = ref_of_perm[p], p + 1
        if r < 0:
            while q < D_INP and ref_of_perm[q] < 0:
                q += 1
            perm_runs.append((None, q - p, 0))
        else:
            while (q < D_INP and ref_of_perm[q] == ref_of_perm[q - 1] + 1
                   and ref_of_perm[q] // W_IN_SHARD == r // W_IN_SHARD):
                q += 1
            perm_runs.append((int(r // W_IN_SHARD), int(r % W_IN_SHARD), int(r % W_IN_SHARD + q - p)))
        p = q
    shard_runs = []
    for s in range(N_SHARD):
        cols = REF_TO_PERM[s * W_IN_SHARD:(s + 1) * W_IN_SHARD]
        runs, a = [], 0
        for b in range(1, W_IN_SHARD + 1):
            if b == W_IN_SHARD or cols[b] != cols[b - 1] + 1:
                runs.append((int(cols[a]), int(cols[b - 1]) + 1))
                a = b
        shard_runs.append(runs)
    return perm_runs, shard_runs


def _permute_w_in(shards):
    perm_runs, _ = _w_in_runs()
    lead = shards[0].shape[:-1]
    parts = [jnp.zeros(lead + (a,), shards[0].dtype) if s is None else shards[s][..., a:b] for s, a, b in perm_runs]
    return jnp.concatenate(parts, axis=-1)


def _w_in_shard(wp, s):
    _, shard_runs = _w_in_runs()
    return jnp.concatenate([wp[..., a:b] for a, b in shard_runs[s]], axis=-1)


def _pad_w_uq(w):
    w4 = w.reshape(w.shape[:2] + (MLA_HEADS, QK_NOPE + QK_ROPE))
    return jnp.pad(w4, ((0, 0),) * 3 + ((0, HEAD_BLK - QK_NOPE - QK_ROPE),)).reshape(w.shape[:2] + (-1,))


def _unpad_w_uq(w):
    return w.reshape(w.shape[:2] + (MLA_HEADS, HEAD_BLK))[..., :QK_NOPE + QK_ROPE].reshape(w.shape[:2] + (-1,))


def _pad_w_ukv(w):
    w4 = w.reshape(w.shape[:2] + (MLA_HEADS, QK_NOPE + V_DIM))
    kpart = jnp.pad(w4[..., :QK_NOPE], ((0, 0),) * 3 + ((0, HEAD_BLK - QK_NOPE),)).reshape(w.shape[:2] + (-1,))
    return jnp.concatenate([kpart, w4[..., QK_NOPE:].reshape(w.shape[:2] + (-1,))], axis=2)


def _unpad_w_ukv(w):
    hq = MLA_HEADS * HEAD_BLK
    kpart = w[..., :hq].reshape(w.shape[:2] + (MLA_HEADS, HEAD_BLK))[..., :QK_NOPE]
    vpart = w[..., hq:].reshape(w.shape[:2] + (MLA_HEADS, V_DIM))
    return jnp.concatenate([kpart, vpart], axis=3).reshape(w.shape[:2] + (-1,))


def _local_step(x, tgt, w):
    s = x.shape[0]
    tabs = _rope_tables(s)
    vec = lambda a: a[:, None, :]
    mxu = lambda a: a.astype(MXU_DTYPE)
    p = dict(g=vec(w['norm_g']), w_in=mxu(w['w_in']), cw=w['conv_w'], cb=vec(w['conv_b']),
             wr=mxu(_block_diag(w['w_rg'])), br=vec(w['b_rg']), wi=mxu(_block_diag(w['w_ig'])), bi=vec(w['b_ig']),
             lam=vec(w['lru_lambda']), gq=vec(w['q_norm_g']), gkv=vec(w['kv_norm_g']),
             wuq=mxu(_pad_w_uq(w['w_uq'])), wukv=mxu(_pad_w_ukv(w['w_ukv'])),
             wp=mxu(_block_diag(w['w_pool'])), ps=vec(w['pool_scale']), wout=mxu(w['w_out']))
    lru = lambda l: (p['cw'], p['cb'], p['wr'], p['br'], p['wi'], p['bi'], p['lam'], l)
    mla = lambda l: (p['gq'], p['gkv'], p['wuq'], p['wukv'], *tabs, l)

    saved = []
    for l in range(DEPTH):
        h, z = _inproj_fwd(x, p['g'], p['w_in'], l)
        ya, hs = _lru_fwd(z, *lru(l))
        yc = _pool_fwd(z, p['wp'], p['ps'], l)
        q, k, v, qn, kvn = _mla_prep_fwd(z, *mla(l))
        o, yb, lse = _flash_fwd(q, k, v, z)
        saved.append(dict(x=x, h=h, z=z, hs=hs, ya=ya, yb=yb, yc=yc, q=q, k=k, v=v, qn=qn, kvn=kvn, o=o, lse=lse))
        x = _outproj_fwd(x, ya, yb, yc, p['wout'], l)

    dx, loss, dgf = _loss_head(x, w['final_norm_g'][None], tgt)
    per_layer = {n: [None] * DEPTH for n in WEIGHT_NAMES if n != 'final_norm_g'}
    for l in reversed(range(DEPTH)):
        sv = saved[l]
        dya, dyc, do, dgb, delta, per_layer['w_out'][l] = _outproj_bwd(
            dx, sv['ya'], sv['yb'], sv['yc'], sv['o'], sv['z'], p['wout'], l)
        dza, dga, *lru_grads = _lru_bwd(sv['z'], sv['hs'], dya, *lru(l))
        for n, g in zip(('w_rg', 'w_ig', 'conv_w', 'conv_b', 'b_rg', 'b_ig', 'lru_lambda'), lru_grads):
            per_layer[n][l] = g
        dzc, dgc, per_layer['w_pool'][l], per_layer['pool_scale'][l] = _pool_bwd(sv['z'], dyc, p['wp'], p['ps'], l)
        dq, dk, dv = _flash_bwd(sv['q'], sv['k'], sv['v'], do, sv['lse'], delta)
        (dcq, dckv, dkr, per_layer['w_uq'][l], per_layer['w_ukv'][l], per_layer['q_norm_g'][l],
         per_layer['kv_norm_g'][l]) = _mla_prep_bwd(dq, dk, dv, sv['z'], sv['qn'], sv['kvn'], *mla(l))
        dz_parts = (dza, dga, dcq, dgb, dckv, dzc, dgc, dkr)
        per_layer['w_in'][l] = _dwin(sv['h'], dz_parts)
        dx, per_layer['norm_g'][l] = _inproj_bwd(dz_parts, p['w_in'], sv['x'], p['g'], dx, l)
    grads = {n: jnp.stack(g) for n, g in per_layer.items()}
    for n in ('norm_g', 'conv_b', 'b_rg', 'b_ig', 'lru_lambda', 'q_norm_g', 'kv_norm_g', 'pool_scale'):
        grads[n] = grads[n][:, 0, :]
    grads['w_rg'] = _diag_blocks(grads['w_rg'], LRU_HEADS)
    grads['w_ig'] = _diag_blocks(grads['w_ig'], LRU_HEADS)
    grads['w_pool'] = _diag_blocks(grads['w_pool'], len(POOL_WINDOWS))
    grads['w_uq'] = _unpad_w_uq(grads['w_uq'])
    grads['w_ukv'] = _unpad_w_ukv(grads['w_ukv'])
    grads['final_norm_g'] = dgf[0]
    return loss[0, 0], dx, grads


WIRE_DTYPE = jnp.bfloat16
MESH_IDS = pl.DeviceIdType.MESH
_HBM = pl.BlockSpec(memory_space=pltpu.HBM)


def _coords():
    return lax.axis_index("x"), lax.axis_index("y"), lax.axis_index("c")


def _comm_call(body, name, arrays, out_shapes, copies_per_array):
    n = len(arrays)
    return pl.pallas_call(
        body, name=name, out_shape=out_shapes, in_specs=[_HBM] * n, out_specs=[_HBM] * n,
        scratch_shapes=[pltpu.SemaphoreType.DMA((n, copies_per_array)), pltpu.SemaphoreType.DMA((n, copies_per_array))],
    )(*arrays)


def _all_gather8(blocks, name):
    n = len(blocks)
    every = range(n)

    def body(*refs):
        x_refs, out_refs = refs[:n], refs[n:2 * n]
        send_sems, recv_sems = refs[2 * n:]
        x, y, c = _coords()
        me, sibling = (x, y, c), (x, y, 1 - c)
        chips = [(1 - x, y), (x, 1 - y), (1 - x, 1 - y)]

        def slot(t, px, py, pc):
            return out_refs[t].at[4 * px + 2 * py + pc]

        def copy(t, k, block, to, own=False):
            return pltpu.make_async_remote_copy(
                src_ref=x_refs[t] if own else slot(t, *block), dst_ref=slot(t, *block),
                send_sem=send_sems.at[t, k], recv_sem=recv_sems.at[t, k], device_id=to, device_id_type=MESH_IDS)

        first = [copy(t, 0, me, sibling, own=True) for t in every]
        first += [copy(t, 1 + j, me, (*chip, c), own=True) for j, chip in enumerate(chips) for t in every]
        for cp in first:
            cp.start()
        passed = [[copy(t, 4 + j, (*chip, c), sibling) for t in every] for j, chip in enumerate(chips)]
        for j, chip in enumerate(chips):
            for t in every:
                copy(t, 1 + j, (*chip, c), me).wait_recv()
                passed[j][t].start()
        for t in every:
            copy(t, 0, sibling, me).wait_recv()
        for j, chip in enumerate(chips):
            for t in every:
                copy(t, 4 + j, (*chip, 1 - c), me).wait_recv()
        for cp in first + [cp for group in passed for cp in group]:
            cp.wait_send()

    outs = [jax.ShapeDtypeStruct((N_DEV,) + b.shape, b.dtype) for b in blocks]
    got = _comm_call(body, name, blocks, outs, 7)
    me = 4 * lax.axis_index("x") + 2 * lax.axis_index("y") + lax.axis_index("c")
    return [lax.dynamic_update_index_in_dim(g, b, me, 0) for g, b in zip(got, blocks)]


def _sibling_send(arrs, name, which_half=False):
    n = len(arrs)

    def body(*refs):
        a_refs, out_refs = refs[:n], refs[n:2 * n]
        send_sems, recv_sems = refs[2 * n:]
        x, y, c = _coords()
        sent = [pltpu.make_async_remote_copy(
            src_ref=a_refs[t].at[1 - c] if which_half else a_refs[t], dst_ref=out_refs[t],
            send_sem=send_sems.at[t, 0], recv_sem=recv_sems.at[t, 0],
            device_id=(x, y, 1 - c), device_id_type=MESH_IDS) for t in range(n)]
        for cp in sent:
            cp.start()
        for cp in sent:
            cp.wait()

    shapes = [jax.ShapeDtypeStruct(a.shape[1:] if which_half else a.shape, a.dtype) for a in arrs]
    return _comm_call(body, name, arrs, shapes, 1)


def _chip_exchange(arrs, name):
    n = len(arrs)

    def body(*refs):
        a_refs, out_refs = refs[:n], refs[n:2 * n]
        send_sems, recv_sems = refs[2 * n:]
        x, y, c = _coords()
        copies = []
        for j, (cx, cy) in enumerate([(1 - x, y), (x, 1 - y), (1 - x, 1 - y)]):
            copies += [pltpu.make_async_remote_copy(
                src_ref=a_refs[t].at[2 * cx + cy], dst_ref=out_refs[t].at[j], send_sem=send_sems.at[t, j],
                recv_sem=recv_sems.at[t, j], device_id=(cx, cy, c), device_id_type=MESH_IDS) for t in range(n)]
        for cp in copies:
            cp.start()
        for cp in copies:
            cp.wait()

    return _comm_call(body, name, arrs, [jax.ShapeDtypeStruct((3,) + a.shape[1:], a.dtype) for a in arrs], 3)


def _sum_leading(groups, out_dtype, steps, name):
    flat = [a for g in groups for a in g]

    def body(*refs):
        ins, outs, pos = refs[:len(flat)], refs[len(flat):], 0
        for g, o_ref in zip(groups, outs):
            acc = None
            for i_ref in ins[pos:pos + len(g)]:
                for k in range(i_ref.shape[0]):
                    term = i_ref[k].astype(F32)
                    acc = term if acc is None else acc + term
            pos += len(g)
            o_ref[...] = acc.astype(o_ref.dtype)

    return pl.pallas_call(
        body, name=name, grid=(steps,),
        in_specs=[pl.BlockSpec((a.shape[0], a.shape[1] // steps, a.shape[2]), lambda i: (0, i, 0)) for a in flat],
        out_specs=[pl.BlockSpec((g[0].shape[1] // steps, g[0].shape[2]), lambda i: (i, 0)) for g in groups],
        out_shape=[jax.ShapeDtypeStruct(g[0].shape[1:], out_dtype) for g in groups],
        compiler_params=_cp(("parallel",), 40),
    )(*flat)


def _adamw_update(w_ref, g_ref, m_ref, v_ref, d_ref, mo_ref, vo_ref):
    gv = g_ref[...]
    mn = ADAM_B1 * m_ref[...] + (1.0 - ADAM_B1) * gv
    vn = ADAM_B2 * v_ref[...] + (1.0 - ADAM_B2) * (gv * gv)
    mo_ref[...] = mn
    vo_ref[...] = vn
    m_hat = mn / (1.0 - ADAM_B1 ** ADAM_STEP)
    v_hat = vn / (1.0 - ADAM_B2 ** ADAM_STEP)
    d_ref[...] = (-ADAM_LR) * (m_hat / (jnp.sqrt(v_hat) + ADAM_EPS) + ADAM_WD * w_ref[...])


def _adamw(w, g, m, v, name):
    n, r, cdim = w.shape
    tr = math.gcd(r, 512)

    def body(*refs):
        _adamw_update(*refs)

    blk = pl.BlockSpec((None, tr, cdim), lambda l, i: (l, i, 0))
    return pl.pallas_call(
        body, name=name, grid=(n, r // tr),
        in_specs=[blk] * 4, out_specs=[blk] * 3,
        out_shape=[jax.ShapeDtypeStruct(w.shape, F32)] * 3,
        compiler_params=_cp(("parallel", "parallel"), 40),
    )(w, g, m, v)


def _adamw_small(ws, gs, ms, vs, name):
    n = len(ws)

    def body(*refs):
        ins, outs = refs[:4 * n], refs[4 * n:]
        for t in range(n):
            _adamw_update(ins[t], ins[n + t], ins[2 * n + t], ins[3 * n + t], outs[t], outs[n + t], outs[2 * n + t])

    shapes = [jax.ShapeDtypeStruct(w.shape, F32) for w in ws]
    outs = pl.pallas_call(body, name=name, out_shape=shapes * 3)(*ws, *gs, *ms, *vs)
    return outs[:n], outs[n:2 * n], outs[2 * n:]


HALF = DEPTH // 2
BIG = ['w_in', 'w_uq', 'w_ukv', 'w_out']
SHARD_AXIS = {'w_in': 2, 'conv_w': 2, 'w_uq': 2, 'w_ukv': 2, 'w_out': 1}
FULL_SHAPE = {'w_in': (DEPTH, D_MODEL, D_IN), 'conv_w': (DEPTH, CONV_WIDTH, D_LRU),
              'w_uq': (DEPTH, Q_RANK, MLA_HEADS * (QK_NOPE + QK_ROPE)),
              'w_ukv': (DEPTH, KV_RANK, MLA_HEADS * (QK_NOPE + V_DIM)), 'w_out': (DEPTH, D_MIX, D_MODEL)}


def _shard_shape(n):
    shp = list(FULL_SHAPE[n])
    shp[SHARD_AXIS[n]] //= N_SHARD
    return tuple(shp)


def _rows_view(a, lead=0):
    return a.reshape(a.shape[:lead] + (-1, a.shape[-1]))


def _gather_weights(local):
    c = lax.axis_index("c")
    names = BIG + ['conv_w']
    halves = [lax.dynamic_slice_in_dim(local[n], HALF * c, HALF, axis=0) for n in names]
    halves = [h.astype(WIRE_DTYPE) if n in BIG else h for n, h in zip(names, halves)]
    got = _all_gather8(halves, "gather_weights")
    full = {}
    for n, g in zip(names, got):
        g = g.reshape((N_SHARD, DEPTH) + g.shape[2:])
        if n == 'w_in':
            full[n] = _permute_w_in([g[s] for s in range(N_SHARD)])
        else:
            full[n] = jnp.moveaxis(g, 0, SHARD_AXIS[n]).reshape(FULL_SHAPE[n])
    return full


def _shard_blocks(g, n):
    width = _shard_shape(n)[SHARD_AXIS[n]]

    def block(h, s):
        part = g[HALF * h:HALF * (h + 1)]
        if n == 'w_in':
            part = _w_in_shard(part, s)
        else:
            part = lax.slice_in_dim(part, s * width, (s + 1) * width, axis=SHARD_AXIS[n])
        return _rows_view(part)

    return jnp.stack([jnp.stack([block(h, s) for s in range(N_SHARD)]) for h in range(2)]).astype(WIRE_DTYPE)


SUM_STEPS = 8


def _reduce_big(grads):
    c = lax.axis_index("c")
    shard = 2 * lax.axis_index("x") + lax.axis_index("y")
    contrib = [_shard_blocks(grads[n], n) for n in BIG]
    from_sibling = _sibling_send(contrib, "pair_exchange_big", which_half=True)
    own_half = [lax.dynamic_index_in_dim(a, c, 0, keepdims=False) for a in contrib]
    pair_sum = _sum_leading([[_rows_view(o)[None], _rows_view(r)[None]] for o, r in zip(own_half, from_sibling)],
                            WIRE_DTYPE, SUM_STEPS, "pair_sum_big")
    to_chips = [p.reshape(a.shape[1:]) for p, a in zip(pair_sum, contrib)]
    from_chips = _chip_exchange(to_chips, "chip_exchange_big")
    own_block = [lax.dynamic_index_in_dim(a, shard, 0, keepdims=True) for a in to_chips]
    mine = _sum_leading([[o, r] for o, r in zip(own_block, from_chips)], F32, SUM_STEPS, "chip_sum_big")
    theirs = _sibling_send(mine, "sibling_big")
    both = [jnp.where(c == 0, jnp.stack([m, t]), jnp.stack([t, m])) for m, t in zip(mine, theirs)]
    return {n: b.reshape(_shard_shape(n)) for n, b in zip(BIG, both)}


SMALL = REPLICATED + ['conv_w']


def _reduce_small(grads):
    views = [grads[n].reshape(-1, LANES) if grads[n].shape[-1] < LANES else _rows_view(jnp.atleast_2d(grads[n]))
             for n in SMALL]
    sums = _sum_leading([[g] for g in _all_gather8(views, "gather_small")], F32, 1, "sum_small")
    return {n: s.reshape(grads[n].shape) for n, s in zip(SMALL, sums)}


def kernel(x, norm_g, w_in, conv_w, conv_b, w_rg, b_rg, w_ig, b_ig, lru_lambda, q_norm_g, w_uq, kv_norm_g, w_ukv, w_pool, pool_scale, w_out, final_norm_g, loss_target, m_norm_g, m_w_in, m_conv_w, m_conv_b, m_w_rg, m_b_rg, m_w_ig, m_b_ig, m_lru_lambda, m_q_norm_g, m_w_uq, m_kv_norm_g, m_w_ukv, m_w_pool, m_pool_scale, m_w_out, m_final_norm_g, v_norm_g, v_w_in, v_conv_w, v_conv_b, v_w_rg, v_b_rg, v_w_ig, v_b_ig, v_lru_lambda, v_q_norm_g, v_w_uq, v_kv_norm_g, v_w_ukv, v_w_pool, v_pool_scale, v_w_out, v_final_norm_g):
    w_loc = dict(zip(WEIGHT_NAMES, (norm_g, w_in, conv_w, conv_b, w_rg, b_rg, w_ig, b_ig, lru_lambda, q_norm_g, w_uq,
                                    kv_norm_g, w_ukv, w_pool, pool_scale, w_out, final_norm_g)))
    m_loc = dict(zip(WEIGHT_NAMES, (m_norm_g, m_w_in, m_conv_w, m_conv_b, m_w_rg, m_b_rg, m_w_ig, m_b_ig, m_lru_lambda,
                                    m_q_norm_g, m_w_uq, m_kv_norm_g, m_w_ukv, m_w_pool, m_pool_scale, m_w_out,
                                    m_final_norm_g)))
    v_loc = dict(zip(WEIGHT_NAMES, (v_norm_g, v_w_in, v_conv_w, v_conv_b, v_w_rg, v_b_rg, v_w_ig, v_b_ig, v_lru_lambda,
                                    v_q_norm_g, v_w_uq, v_kv_norm_g, v_w_ukv, v_w_pool, v_pool_scale, v_w_out,
                                    v_final_norm_g)))
    w_full = dict(w_loc)
    w_full.update(_gather_weights(w_loc))
    loss_local, dx, g_local = _local_step(x[0], loss_target[0], w_full)
    loss = lax.psum(loss_local, ("x", "y", "c"))

    grads = _reduce_big(g_local)
    g_small = _reduce_small(g_local)
    shard = 2 * lax.axis_index("x") + lax.axis_index("y")
    width = D_LRU // N_SHARD
    grads['conv_w'] = lax.dynamic_slice_in_dim(g_small['conv_w'], shard * width, width, axis=2)
    for n in REPLICATED:
        grads[n] = g_small[n]

    delta, new_m, new_v = {}, {}, {}
    for n in BIG:
        delta[n], new_m[n], new_v[n] = _adamw(w_loc[n], grads[n], m_loc[n], v_loc[n], "adamw_" + n)
    small = [[_rows_view(jnp.atleast_2d(t[n])) for n in SMALL] for t in (w_loc, grads, m_loc, v_loc)]
    for tree, outs in zip((delta, new_m, new_v), _adamw_small(*small, "adamw_small")):
        tree.update({n: a.reshape(w_loc[n].shape) for n, a in zip(SMALL, outs)})

    return (loss, dx[None], *[grads[n] for n in WEIGHT_NAMES], *[delta[n] for n in WEIGHT_NAMES],
            *[new_m[n] for n in WEIGHT_NAMES], *[new_v[n] for n in WEIGHT_NAMES])
```
